```python
import math
import jax, jax.numpy as jnp
from jax import lax
import numpy as np

D_MODEL = 1024
BATCH = 8
SEQ = 8192
DEPTH = 1

HEAD_DIM = 64
POOL_WINDOWS = (2, 4, 8, 16)
POOL_GROUP_DIM = D_MODEL // 16
POOL_DIM = len(POOL_WINDOWS) * POOL_GROUP_DIM
ATTN_GROUPS = ((128, 1), (512, 4), (2048, 16))
N_ATTN_HEADS = (D_MODEL - POOL_DIM) // HEAD_DIM
HEADS_PER_GROUP = N_ATTN_HEADS // len(ATTN_GROUPS)
ATTN_DIM = N_ATTN_HEADS * HEAD_DIM
D_MIX = POOL_DIM + ATTN_DIM
D_IN = POOL_DIM + 3 * ATTN_DIM
D_FF = 64 * int(math.ceil(8 * D_MODEL / (3 * 64)))
RMS_EPS = 1e-6

kernel_name = "hymba_pool_dilated_alibi_macaron"


def alibi_slopes():
    s = np.array([2.0 ** (-8.0 * (i + 1) / N_ATTN_HEADS) for i in range(N_ATTN_HEADS)], np.float32)
    return jnp.asarray(s)


def rmsnorm(x, g):
    xf = x.astype(jnp.float32)
    y = xf * lax.rsqrt(jnp.mean(xf * xf, axis=-1, keepdims=True) + RMS_EPS)
    return (y * g.astype(jnp.float32)).astype(x.dtype)


def swiglu(h, w_gate, w_up, w_down):
    return (jax.nn.silu(h @ w_gate) * (h @ w_up)) @ w_down


def pool_mixer(u, w_lin, scale):
    B, S, _ = u.shape
    G = len(POOL_WINDOWS)
    uf = u.astype(jnp.float32).reshape(B, S, G, POOL_GROUP_DIM)
    csum = jnp.concatenate([jnp.zeros((B, 1, G, POOL_GROUP_DIM), jnp.float32),
                            jnp.cumsum(uf, axis=1)], axis=1)
    t = jnp.arange(S)
    outs = []
    for g, w in enumerate(POOL_WINDOWS):
        lo = jnp.clip(t - w // 2, 0, S)
        hi = jnp.clip(t - w // 2 + w, 0, S)
        cg = csum[:, :, g]
        win_sum = jnp.take(cg, hi, axis=1) - jnp.take(cg, lo, axis=1)
        mean = win_sum / (hi - lo).astype(jnp.float32)[None, :, None]
        outs.append(mean - uf[:, :, g])
    y = jnp.stack(outs, axis=2).astype(u.dtype)
    y = jnp.einsum('bsgc,gce->bsge', y, w_lin)
    return y.reshape(B, S, POOL_DIM) * scale


def dilated_window_attention(q, k, v, window, dilation, slopes):
    B, S, H, Dh = q.shape
    n_side = (window // 2) // dilation
    blk = n_side
    L = S // dilation
    nb = -(-L // blk)
    Lp = nb * blk

    def to_sub(a):
        return a.reshape(B, L, dilation, H, Dh).transpose(0, 2, 1, 3, 4).reshape(B * dilation, L, H, Dh)

    def windows(a):
        a = jnp.pad(to_sub(a), ((0, 0), (blk, Lp - L + blk), (0, 0), (0, 0)))
        a = a.reshape(-1, nb + 2, blk, H, Dh)
        return jnp.concatenate([a[:, :-2], a[:, 1:-1], a[:, 2:]], axis=2)

    qs = jnp.pad(to_sub(q), ((0, 0), (0, Lp - L), (0, 0), (0, 0))).reshape(-1, nb, blk, H, Dh)
    kw, vw = windows(k), windows(v)

    rel = jnp.arange(3 * blk)[None, :] - blk - jnp.arange(blk)[:, None]
    key_idx = (jnp.arange(nb)[:, None] - 1) * blk + jnp.arange(3 * blk)[None, :]
    valid = (jnp.abs(rel) <= n_side)[None] & ((key_idx >= 0) & (key_idx < L))[:, None, :]
    dist = (jnp.abs(rel) * dilation).astype(jnp.float32)
    bias = -slopes.astype(jnp.float32)[:, None, None] * dist[None]

    s = jnp.einsum('znqhd,znkhd->znhqk', qs, kw).astype(jnp.float32) * (Dh ** -0.5)
    s = jnp.where(valid[None, :, None], s + bias[None, None], -jnp.inf)
    m = jnp.max(s, axis=-1, keepdims=True)
    p = jnp.exp(s - m)
    den = jnp.sum(p, axis=-1)
    o = jnp.einsum('znhqk,znkhd->znqhd', p, vw.astype(jnp.float32))
    den_t = jnp.swapaxes(den, 2, 3)
    o = o / den_t[..., None]
    lse = jnp.swapaxes(m[..., 0], 2, 3) + jnp.log(den_t)

    def from_sub(a):
        a = a.reshape((B, dilation, Lp) + a.shape[3:])[:, :, :L]
        a = jnp.swapaxes(a, 1, 2)
        return a.reshape((B, S) + a.shape[3:])

    return from_sub(o), from_sub(lse)


def attention_mixer(q, k, v):
    B, S = q.shape[:2]
    slopes = alibi_slopes()
    outs, lses = [], []
    for gi, (window, dilation) in enumerate(ATTN_GROUPS):
        hs = slice(gi * HEADS_PER_GROUP, (gi + 1) * HEADS_PER_GROUP)
        o, lse = dilated_window_attention(q[:, :, hs], k[:, :, hs], v[:, :, hs],
                                          window, dilation, slopes[hs])
        outs.append(o)
        lses.append(lse)
    alpha = jax.nn.softmax(jnp.stack(lses, axis=0), axis=0)
    o = jnp.concatenate([outs[g] * alpha[g][..., None] for g in range(len(ATTN_GROUPS))], axis=2)
    return o.reshape(B, S, ATTN_DIM).astype(q.dtype)


def _fwd_setup_inputs(seed: int = 0) -> dict:
    key = jax.random.key(seed)
    ks = jax.random.split(key, 20)
    f32 = jnp.float32

    def nrm(k, shape, fan_in):
        return jax.random.normal(k, shape, f32) * (fan_in ** -0.5)

    def gain(k, n):
        return 1.0 + 0.02 * jax.random.normal(k, (DEPTH, n), f32)

    return {
        "x": jax.random.normal(ks[0], (BATCH, SEQ, D_MODEL), f32),
        "g_ffn1_pre": gain(ks[1], D_MODEL),
        "w1_gate": nrm(ks[2], (DEPTH, D_MODEL, D_FF), D_MODEL),
        "w1_up": nrm(ks[3], (DEPTH, D_MODEL, D_FF), D_MODEL),
        "w1_down": nrm(ks[4], (DEPTH, D_FF, D_MODEL), D_FF),
        "g_ffn1_post": gain(ks[5], D_MODEL),
        "g_mix_pre": gain(ks[6], D_MODEL),
        "w_in": nrm(ks[7], (DEPTH, D_MODEL, D_IN), D_MODEL),
        "w_pool_lin": nrm(ks[8], (DEPTH, len(POOL_WINDOWS), POOL_GROUP_DIM, POOL_GROUP_DIM), POOL_GROUP_DIM),
        "pool_scale": gain(ks[9], POOL_DIM),
        "w_out": nrm(ks[10], (DEPTH, D_MIX, D_MODEL), D_MIX),
        "g_mix_post": gain(ks[11], D_MODEL),
        "g_ffn2_pre": gain(ks[12], D_MODEL),
        "w2_gate": nrm(ks[13], (DEPTH, D_MODEL, D_FF), D_MODEL),
        "w2_up": nrm(ks[14], (DEPTH, D_MODEL, D_FF), D_MODEL),
        "w2_down": nrm(ks[15], (DEPTH, D_FF, D_MODEL), D_FF),
        "g_ffn2_post": gain(ks[16], D_MODEL),
    }


def _fwd_reference(x, g_ffn1_pre, w1_gate, w1_up, w1_down, g_ffn1_post, g_mix_pre, w_in,
              w_pool_lin, pool_scale, w_out, g_mix_post, g_ffn2_pre, w2_gate, w2_up,
              w2_down, g_ffn2_post):
    B, S, _ = x.shape
    for l in range(DEPTH):
        x = x + 0.5 * rmsnorm(swiglu(rmsnorm(x, g_ffn1_pre[l]), w1_gate[l], w1_up[l], w1_down[l]),
                              g_ffn1_post[l])
        h = rmsnorm(x, g_mix_pre[l])
        z = h @ w_in[l]
        u = z[..., :POOL_DIM]
        qkv = z[..., POOL_DIM:].reshape(B, S, 3, N_ATTN_HEADS, HEAD_DIM)
        a_pool = pool_mixer(u, w_pool_lin[l], pool_scale[l])
        a_attn = attention_mixer(qkv[:, :, 0], qkv[:, :, 1], qkv[:, :, 2])
        mix = jnp.concatenate([a_pool.astype(x.dtype), a_attn.astype(x.dtype)], axis=-1) @ w_out[l]
        x = x + rmsnorm(mix, g_mix_post[l])
        x = x + 0.5 * rmsnorm(swiglu(rmsnorm(x, g_ffn2_pre[l]), w2_gate[l], w2_up[l], w2_down[l]),
                              g_ffn2_post[l])
    return x


import jax as _jax
import jax.numpy as _jnp

TWIN_FORMAT = 'train_step'
FWD_PARAMS = ['x', 'g_ffn1_pre', 'w1_gate', 'w1_up', 'w1_down', 'g_ffn1_post', 'g_mix_pre', 'w_in', 'w_pool_lin', 'pool_scale', 'w_out', 'g_mix_post', 'g_ffn2_pre', 'w2_gate', 'w2_up', 'w2_down', 'g_ffn2_post']
TWIN_WEIGHTS = ['g_ffn1_pre', 'w1_gate', 'w1_up', 'w1_down', 'g_ffn1_post', 'g_mix_pre', 'w_in', 'w_pool_lin', 'pool_scale', 'w_out', 'g_mix_post', 'g_ffn2_pre', 'w2_gate', 'w2_up', 'w2_down', 'g_ffn2_post']
TWIN_DIFF_INPUT = 'x'
TWIN_INPUTS = ['x', 'g_ffn1_pre', 'w1_gate', 'w1_up', 'w1_down', 'g_ffn1_post', 'g_mix_pre', 'w_in', 'w_pool_lin', 'pool_scale', 'w_out', 'g_mix_post', 'g_ffn2_pre', 'w2_gate', 'w2_up', 'w2_down', 'g_ffn2_post', 'loss_target', 'm_g_ffn1_pre', 'm_w1_gate', 'm_w1_up', 'm_w1_down', 'm_g_ffn1_post', 'm_g_mix_pre', 'm_w_in', 'm_w_pool_lin', 'm_pool_scale', 'm_w_out', 'm_g_mix_post', 'm_g_ffn2_pre', 'm_w2_gate', 'm_w2_up', 'm_w2_down', 'm_g_ffn2_post', 'v_g_ffn1_pre', 'v_w1_gate', 'v_w1_up', 'v_w1_down', 'v_g_ffn1_post', 'v_g_mix_pre', 'v_w_in', 'v_w_pool_lin', 'v_pool_scale', 'v_w_out', 'v_g_mix_post', 'v_g_ffn2_pre', 'v_w2_gate', 'v_w2_up', 'v_w2_down', 'v_g_ffn2_post']
TWIN_OUTPUTS = ['loss', 'grad_x', 'grad_g_ffn1_pre', 'grad_w1_gate', 'grad_w1_up', 'grad_w1_down', 'grad_g_ffn1_post', 'grad_g_mix_pre', 'grad_w_in', 'grad_w_pool_lin', 'grad_pool_scale', 'grad_w_out', 'grad_g_mix_post', 'grad_g_ffn2_pre', 'grad_w2_gate', 'grad_w2_up', 'grad_w2_down', 'grad_g_ffn2_post', 'delta_g_ffn1_pre', 'delta_w1_gate', 'delta_w1_up', 'delta_w1_down', 'delta_g_ffn1_post', 'delta_g_mix_pre', 'delta_w_in', 'delta_w_pool_lin', 'delta_pool_scale', 'delta_w_out', 'delta_g_mix_post', 'delta_g_ffn2_pre', 'delta_w2_gate', 'delta_w2_up', 'delta_w2_down', 'delta_g_ffn2_post', 'new_m_g_ffn1_pre', 'new_m_w1_gate', 'new_m_w1_up', 'new_m_w1_down', 'new_m_g_ffn1_post', 'new_m_g_mix_pre', 'new_m_w_in', 'new_m_w_pool_lin', 'new_m_pool_scale', 'new_m_w_out', 'new_m_g_mix_post', 'new_m_g_ffn2_pre', 'new_m_w2_gate', 'new_m_w2_up', 'new_m_w2_down', 'new_m_g_ffn2_post', 'new_v_g_ffn1_pre', 'new_v_w1_gate', 'new_v_w1_up', 'new_v_w1_down', 'new_v_g_ffn1_post', 'new_v_g_mix_pre', 'new_v_w_in', 'new_v_w_pool_lin', 'new_v_pool_scale', 'new_v_w_out', 'new_v_g_mix_post', 'new_v_g_ffn2_pre', 'new_v_w2_gate', 'new_v_w2_up', 'new_v_w2_down', 'new_v_g_ffn2_post']
TWIN_LEAF_KINDS = {'loss': 'loss', 'grad_x': 'grad_x', 'grad_g_ffn1_pre': 'grad_w', 'grad_w1_gate': 'grad_w', 'grad_w1_up': 'grad_w', 'grad_w1_down': 'grad_w', 'grad_g_ffn1_post': 'grad_w', 'grad_g_mix_pre': 'grad_w', 'grad_w_in': 'grad_w', 'grad_w_pool_lin': 'grad_w', 'grad_pool_scale': 'grad_w', 'grad_w_out': 'grad_w', 'grad_g_mix_post': 'grad_w', 'grad_g_ffn2_pre': 'grad_w', 'grad_w2_gate': 'grad_w', 'grad_w2_up': 'grad_w', 'grad_w2_down': 'grad_w', 'grad_g_ffn2_post': 'grad_w', 'delta_g_ffn1_pre': 'delta_w', 'delta_w1_gate': 'delta_w', 'delta_w1_up': 'delta_w', 'delta_w1_down': 'delta_w', 'delta_g_ffn1_post': 'delta_w', 'delta_g_mix_pre': 'delta_w', 'delta_w_in': 'delta_w', 'delta_w_pool_lin': 'delta_w', 'delta_pool_scale': 'delta_w', 'delta_w_out': 'delta_w', 'delta_g_mix_post': 'delta_w', 'delta_g_ffn2_pre': 'delta_w', 'delta_w2_gate': 'delta_w', 'delta_w2_up': 'delta_w', 'delta_w2_down': 'delta_w', 'delta_g_ffn2_post': 'delta_w', 'new_m_g_ffn1_pre': 'new_m', 'new_m_w1_gate': 'new_m', 'new_m_w1_up': 'new_m', 'new_m_w1_down': 'new_m', 'new_m_g_ffn1_post': 'new_m', 'new_m_g_mix_pre': 'new_m', 'new_m_w_in': 'new_m', 'new_m_w_pool_lin': 'new_m', 'new_m_pool_scale': 'new_m', 'new_m_w_out': 'new_m', 'new_m_g_mix_post': 'new_m', 'new_m_g_ffn2_pre': 'new_m', 'new_m_w2_gate': 'new_m', 'new_m_w2_up': 'new_m', 'new_m_w2_down': 'new_m', 'new_m_g_ffn2_post': 'new_m', 'new_v_g_ffn1_pre': 'new_v', 'new_v_w1_gate': 'new_v', 'new_v_w1_up': 'new_v', 'new_v_w1_down': 'new_v', 'new_v_g_ffn1_post': 'new_v', 'new_v_g_mix_pre': 'new_v', 'new_v_w_in': 'new_v', 'new_v_w_pool_lin': 'new_v', 'new_v_pool_scale': 'new_v', 'new_v_w_out': 'new_v', 'new_v_g_mix_post': 'new_v', 'new_v_g_ffn2_pre': 'new_v', 'new_v_w2_gate': 'new_v', 'new_v_w2_up': 'new_v', 'new_v_w2_down': 'new_v', 'new_v_g_ffn2_post': 'new_v'}


def _forward(args):
    return _fwd_reference(*[args[k] for k in FWD_PARAMS])


def _output_shape():
    out = _jax.eval_shape(lambda: _forward(_fwd_setup_inputs(0)))
    return out.shape, out.dtype

N_MICROBATCH = 1
ADAM_LR = 0.001
ADAM_B1 = 0.9
ADAM_B2 = 0.999
ADAM_EPS = 1e-08
ADAM_WD = 0.01
ADAM_STEP = 10
PER_EXAMPLE_BATCH_AXIS = {'x': 0, 'loss_target': 0}
SHARED_INPUTS = []
_WEIGHT_DTYPES = {'g_ffn1_pre': _jnp.float32, 'w1_gate': _jnp.float32, 'w1_up': _jnp.float32, 'w1_down': _jnp.float32, 'g_ffn1_post': _jnp.float32, 'g_mix_pre': _jnp.float32, 'w_in': _jnp.float32, 'w_pool_lin': _jnp.float32, 'pool_scale': _jnp.float32, 'w_out': _jnp.float32, 'g_mix_post': _jnp.float32, 'g_ffn2_pre': _jnp.float32, 'w2_gate': _jnp.float32, 'w2_up': _jnp.float32, 'w2_down': _jnp.float32, 'g_ffn2_post': _jnp.float32}
MOMENT_SCALE = {'g_ffn1_pre': 5.148670e-01, 'w1_gate': 2.075529e-01, 'w1_up': 2.403136e-01, 'w1_down': 4.191955e-01, 'g_ffn1_post': 1.582770e+01, 'g_mix_pre': 1.167493e+00, 'w_in': 7.244021e-01, 'w_pool_lin': 2.596313e+00, 'pool_scale': 3.499522e+00, 'w_out': 1.322011e+00, 'g_mix_post': 6.394932e+01, 'g_ffn2_pre': 5.336725e-01, 'w2_gate': 1.466869e-01, 'w2_up': 3.124628e-01, 'w2_down': 5.110717e-01, 'g_ffn2_post': 1.580316e+01}


def _to_microbatches(a, axis):
    t = _jnp.moveaxis(a, axis, 0)
    t = t.reshape((N_MICROBATCH, t.shape[0] // N_MICROBATCH) + t.shape[1:])
    return _jnp.moveaxis(t, 1, axis + 1)


def setup_inputs(seed: int = 0) -> dict:
    inp = _fwd_setup_inputs(seed)
    key = _jax.random.fold_in(_jax.random.key(seed), 7919)
    shape, _ = _output_shape()
    out = dict(inp)
    out["loss_target"] = _jax.random.normal(_jax.random.fold_in(key, 0), shape, _jnp.float32)
    for i, name in enumerate(TWIN_WEIGHTS):
        w = inp[name].astype(_jnp.float32)
        if MOMENT_SCALE is None:
            s = _jnp.sqrt(_jnp.mean(_jnp.square(w)) + 1e-30)
        else:
            s = MOMENT_SCALE[name]
        km, kv = _jax.random.split(_jax.random.fold_in(key, i + 1))
        out[name] = w
        out["m_" + name] = s * _jax.random.normal(km, w.shape, _jnp.float32)
        out["v_" + name] = (s * s) * _jax.random.uniform(kv, w.shape, _jnp.float32, 0.5, 1.5)
    if N_MICROBATCH > 1:
        for name, axis in PER_EXAMPLE_BATCH_AXIS.items():
            out[name] = _to_microbatches(out[name], axis)
    return {'x': out['x'], 'g_ffn1_pre': out['g_ffn1_pre'], 'w1_gate': out['w1_gate'], 'w1_up': out['w1_up'], 'w1_down': out['w1_down'], 'g_ffn1_post': out['g_ffn1_post'], 'g_mix_pre': out['g_mix_pre'], 'w_in': out['w_in'], 'w_pool_lin': out['w_pool_lin'], 'pool_scale': out['pool_scale'], 'w_out': out['w_out'], 'g_mix_post': out['g_mix_post'], 'g_ffn2_pre': out['g_ffn2_pre'], 'w2_gate': out['w2_gate'], 'w2_up': out['w2_up'], 'w2_down': out['w2_down'], 'g_ffn2_post': out['g_ffn2_post'], 'loss_target': out['loss_target'], 'm_g_ffn1_pre': out['m_g_ffn1_pre'], 'm_w1_gate': out['m_w1_gate'], 'm_w1_up': out['m_w1_up'], 'm_w1_down': out['m_w1_down'], 'm_g_ffn1_post': out['m_g_ffn1_post'], 'm_g_mix_pre': out['m_g_mix_pre'], 'm_w_in': out['m_w_in'], 'm_w_pool_lin': out['m_w_pool_lin'], 'm_pool_scale': out['m_pool_scale'], 'm_w_out': out['m_w_out'], 'm_g_mix_post': out['m_g_mix_post'], 'm_g_ffn2_pre': out['m_g_ffn2_pre'], 'm_w2_gate': out['m_w2_gate'], 'm_w2_up': out['m_w2_up'], 'm_w2_down': out['m_w2_down'], 'm_g_ffn2_post': out['m_g_ffn2_post'], 'v_g_ffn1_pre': out['v_g_ffn1_pre'], 'v_w1_gate': out['v_w1_gate'], 'v_w1_up': out['v_w1_up'], 'v_w1_down': out['v_w1_down'], 'v_g_ffn1_post': out['v_g_ffn1_post'], 'v_g_mix_pre': out['v_g_mix_pre'], 'v_w_in': out['v_w_in'], 'v_w_pool_lin': out['v_w_pool_lin'], 'v_pool_scale': out['v_pool_scale'], 'v_w_out': out['v_w_out'], 'v_g_mix_post': out['v_g_mix_post'], 'v_g_ffn2_pre': out['v_g_ffn2_pre'], 'v_w2_gate': out['v_w2_gate'], 'v_w2_up': out['v_w2_up'], 'v_w2_down': out['v_w2_down'], 'v_g_ffn2_post': out['v_g_ffn2_post']}


def _loss(weights, diff, rest, loss_target):
    with _jax.named_scope("forward"):
        args = {**rest, TWIN_DIFF_INPUT: diff, **{k: w.astype(_WEIGHT_DTYPES[k]) for k, w in weights.items()}}
        y = _forward(args)
    with _jax.named_scope("loss_head"):
        err = _jnp.square(y.astype(_jnp.float32) - loss_target)
        return 0.5 * _jnp.sum(_jnp.mean(err, axis=-1)) if err.ndim else 0.5 * err


def _adamw(w, g, m, v):
    m = ADAM_B1 * m + (1.0 - ADAM_B1) * g
    v = ADAM_B2 * v + (1.0 - ADAM_B2) * _jnp.square(g)
    m_hat = m / (1.0 - ADAM_B1 ** ADAM_STEP)
    v_hat = v / (1.0 - ADAM_B2 ** ADAM_STEP)
    delta = -ADAM_LR * (m_hat / (_jnp.sqrt(v_hat) + ADAM_EPS) + ADAM_WD * w)
    return delta, m, v


def reference(x, g_ffn1_pre, w1_gate, w1_up, w1_down, g_ffn1_post, g_mix_pre, w_in, w_pool_lin, pool_scale, w_out, g_mix_post, g_ffn2_pre, w2_gate, w2_up, w2_down, g_ffn2_post, loss_target, m_g_ffn1_pre, m_w1_gate, m_w1_up, m_w1_down, m_g_ffn1_post, m_g_mix_pre, m_w_in, m_w_pool_lin, m_pool_scale, m_w_out, m_g_mix_post, m_g_ffn2_pre, m_w2_gate, m_w2_up, m_w2_down, m_g_ffn2_post, v_g_ffn1_pre, v_w1_gate, v_w1_up, v_w1_down, v_g_ffn1_post, v_g_mix_pre, v_w_in, v_w_pool_lin, v_pool_scale, v_w_out, v_g_mix_post, v_g_ffn2_pre, v_w2_gate, v_w2_up, v_w2_down, v_g_ffn2_post):
    given = dict(x=x, g_ffn1_pre=g_ffn1_pre, w1_gate=w1_gate, w1_up=w1_up, w1_down=w1_down, g_ffn1_post=g_ffn1_post, g_mix_pre=g_mix_pre, w_in=w_in, w_pool_lin=w_pool_lin, pool_scale=pool_scale, w_out=w_out, g_mix_post=g_mix_post, g_ffn2_pre=g_ffn2_pre, w2_gate=w2_gate, w2_up=w2_up, w2_down=w2_down, g_ffn2_post=g_ffn2_post, loss_target=loss_target, m_g_ffn1_pre=m_g_ffn1_pre, m_w1_gate=m_w1_gate, m_w1_up=m_w1_up, m_w1_down=m_w1_down, m_g_ffn1_post=m_g_ffn1_post, m_g_mix_pre=m_g_mix_pre, m_w_in=m_w_in, m_w_pool_lin=m_w_pool_lin, m_pool_scale=m_pool_scale, m_w_out=m_w_out, m_g_mix_post=m_g_mix_post, m_g_ffn2_pre=m_g_ffn2_pre, m_w2_gate=m_w2_gate, m_w2_up=m_w2_up, m_w2_down=m_w2_down, m_g_ffn2_post=m_g_ffn2_post, v_g_ffn1_pre=v_g_ffn1_pre, v_w1_gate=v_w1_gate, v_w1_up=v_w1_up, v_w1_down=v_w1_down, v_g_ffn1_post=v_g_ffn1_post, v_g_mix_pre=v_g_mix_pre, v_w_in=v_w_in, v_w_pool_lin=v_w_pool_lin, v_pool_scale=v_pool_scale, v_w_out=v_w_out, v_g_mix_post=v_g_mix_post, v_g_ffn2_pre=v_g_ffn2_pre, v_w2_gate=v_w2_gate, v_w2_up=v_w2_up, v_w2_down=v_w2_down, v_g_ffn2_post=v_g_ffn2_post)
    weights = {n: given[n] for n in TWIN_WEIGHTS}
    shared = {n: given[n] for n in SHARED_INPUTS}
    per_example = {n: given[n] for n in ['x']}
    grad_fn = _jax.value_and_grad(_loss, argnums=(0, 1))

    def one_microbatch(ex, loss_target):
        ex = dict(ex)
        diff = ex.pop(TWIN_DIFF_INPUT)
        return grad_fn(weights, diff, {**shared, **ex}, loss_target)

    if N_MICROBATCH == 1:
        loss, (grad_w, grad_x) = one_microbatch(per_example, given["loss_target"])
    else:
        def body(carry, xs):
            loss_sum, grad_sum = carry
            l_k, (gw_k, gx_k) = one_microbatch(xs[0], xs[1])
            with _jax.named_scope("update"):
                return (loss_sum + l_k, _jax.tree.map(_jnp.add, grad_sum, gw_k)), gx_k

        init = (_jnp.zeros((), _jnp.float32), _jax.tree.map(_jnp.zeros_like, weights))
        (loss, grad_w), grad_x = _jax.lax.scan(body, init, (per_example, given["loss_target"]))
    with _jax.named_scope("update"):
        delta_w, new_m, new_v = {}, {}, {}
        for n in TWIN_WEIGHTS:
            delta_w[n], new_m[n], new_v[n] = _adamw(weights[n], grad_w[n], given["m_" + n], given["v_" + n])
    return (loss, grad_x, *[grad_w[n] for n in TWIN_WEIGHTS], *[delta_w[n] for n in TWIN_WEIGHTS],
            *[new_m[n] for n in TWIN_WEIGHTS], *[new_v[n] for n in TWIN_WEIGHTS])
```

```python
import functools
import math

import numpy as np
import jax
import jax.numpy as jnp
from jax import lax
from jax.experimental import pallas as pl
from jax.experimental.pallas import tpu as pltpu

F32 = jnp.float32
BF16 = jnp.bfloat16
MESH = pl.DeviceIdType.MESH

RMS_EPS = 1e-6
HEAD_DIM = 64
POOL_HALF_WINDOWS = (1, 2, 4, 8)
POOL_DIM = 256
GROUP_DIM = 256
DILATIONS = (1, 4, 16)
N_SIDE = 64
N_ATTN_HEADS = 12
ADAM_LR, ADAM_B1, ADAM_B2, ADAM_EPS, ADAM_WD, ADAM_STEP = 0.001, 0.9, 0.999, 1e-08, 0.01, 10

N_CHIPS = 4
V7X_VMEM_LIMIT = 60 * 1024 * 1024

_NT = (((1,), (1,)), ((), ()))
_TN = (((0,), (0,)), ((), ()))


def _dot(a, b):
    return jnp.dot(a, b, preferred_element_type=F32)


def _dot_nt(a, b):
    return lax.dot_general(a, b, _NT, preferred_element_type=F32)


def _dot_tn(a, b):
    return lax.dot_general(a, b, _TN, preferred_element_type=F32)


def _params(**kw):
    return pltpu.CompilerParams(vmem_limit_bytes=V7X_VMEM_LIMIT, **kw)


def _rows(tm, width):
    return pl.BlockSpec((tm, width), lambda i: (i, 0))


def _resident(shape):
    return pl.BlockSpec(shape, lambda i: (0,) * len(shape), pipeline_mode=pl.Buffered(1))


def _const(shape):
    return pl.BlockSpec(shape, lambda i: (0,) * len(shape))


def _tile(rows, cap):
    return max(t for t in range(16, cap + 1, 16) if rows % t == 0)


def _inv_rms(x):
    return lax.rsqrt(jnp.mean(x * x, axis=-1, keepdims=True) + RMS_EPS)


def _rms_bwd(x, inv, g, dy):
    n = x * inv
    dn = dy * g
    dx = inv * (dn - n * jnp.mean(dn * n, axis=-1, keepdims=True))
    return dx, jnp.sum(dy * n, axis=0, keepdims=True)


def _accumulate(ref, value):
    @pl.when(pl.program_id(0) == 0)
    def _():
        ref[...] = jnp.zeros_like(ref)

    ref[...] += value


def _ffn_fwd(x, g_pre, wg_t, wu_t, wd, g_post, target, name, tm=256):
    s, d = x.shape
    ff = wd.shape[0]
    with_loss = target is not None

    def body(*refs):
        if with_loss:
            x_ref, gpre_ref, wg_ref, wu_ref, wd_ref, gpost_ref, t_ref, xo_ref, a_ref, b_ref, f_ref, loss_ref = refs
        else:
            x_ref, gpre_ref, wg_ref, wu_ref, wd_ref, gpost_ref, xo_ref, a_ref, b_ref, f_ref = refs
        xv = x_ref[...]
        hb = (xv * _inv_rms(xv) * gpre_ref[...]).astype(BF16)
        a = _dot_nt(hb, wg_ref[...])
        b = _dot_nt(hb, wu_ref[...])
        hh = (a * jax.nn.sigmoid(a)) * b
        f = _dot(hh.astype(BF16), wd_ref[...])
        xo = xv + 0.5 * (f * _inv_rms(f) * gpost_ref[...])
        a_ref[...] = a.astype(BF16)
        b_ref[...] = b.astype(BF16)
        f_ref[...] = f
        if with_loss:
            e = xo - t_ref[...]
            xo_ref[...] = e * (1.0 / d)
            _accumulate(loss_ref, 0.5 * jnp.sum(jnp.mean(e * e, axis=-1, keepdims=True)))
        else:
            xo_ref[...] = xo

    in_specs = [_rows(tm, d), _const((1, d)), _resident((ff, d)), _resident((ff, d)), _resident((ff, d)), _const((1, d))]
    args = [x, g_pre, wg_t, wu_t, wd, g_post]
    out_shape = [jax.ShapeDtypeStruct((s, d), F32), jax.ShapeDtypeStruct((s, ff), BF16),
                 jax.ShapeDtypeStruct((s, ff), BF16), jax.ShapeDtypeStruct((s, d), F32)]
    out_specs = [_rows(tm, d), _rows(tm, ff), _rows(tm, ff), _rows(tm, d)]
    if with_loss:
        in_specs.append(_rows(tm, d))
        args.append(target)
        out_shape.append(jax.ShapeDtypeStruct((8, 128), F32))
        out_specs.append(_const((8, 128)))
    return pl.pallas_call(body, name=name, grid=(s // tm,), in_specs=in_specs, out_specs=out_specs,
                          out_shape=out_shape, compiler_params=_params(dimension_semantics=("arbitrary",)))(*args)


def _ffn_bwd(dxo, x, f, a, b, g_pre, g_post, wg_t, wu_t, wd, name, tm=256):
    s, d = x.shape
    ff = wd.shape[0]

    def body(dxo_ref, x_ref, f_ref, a_ref, b_ref, gpre_ref, gpost_ref, wg_ref, wu_ref, wd_ref,
             dx_ref, hh_ref, da_ref, db_ref, df_ref, h_ref, dgpre_ref, dgpost_ref):
        dxo_v = dxo_ref[...]
        fv = f_ref[...]
        df, dgpost = _rms_bwd(fv, _inv_rms(fv), gpost_ref[...], 0.5 * dxo_v)
        dfb = df.astype(BF16)
        dhh = _dot_nt(dfb, wd_ref[...])
        av = a_ref[...].astype(F32)
        bv = b_ref[...].astype(F32)
        sig = jax.nn.sigmoid(av)
        sa = av * sig
        da = (dhh * bv * (sig * (1.0 + av * (1.0 - sig)))).astype(BF16)
        db = (dhh * sa).astype(BF16)
        dh = _dot(da, wg_ref[...]) + _dot(db, wu_ref[...])
        xv = x_ref[...]
        inv = _inv_rms(xv)
        dxn, dgpre = _rms_bwd(xv, inv, gpre_ref[...], dh)
        dx_ref[...] = dxo_v + dxn
        hh_ref[...] = (sa * bv).astype(BF16)
        da_ref[...] = da
        db_ref[...] = db
        df_ref[...] = dfb
        h_ref[...] = (xv * inv * gpre_ref[...]).astype(BF16)
        _accumulate(dgpre_ref, dgpre)
        _accumulate(dgpost_ref, dgpost)

    return pl.pallas_call(
        body, name=name, grid=(s // tm,),
        in_specs=[_rows(tm, d), _rows(tm, d), _rows(tm, d), _rows(tm, ff), _rows(tm, ff), _const((1, d)), _const((1, d)),
                  _resident((ff, d)), _resident((ff, d)), _resident((ff, d))],
        out_specs=[_rows(tm, d), _rows(tm, ff), _rows(tm, ff), _rows(tm, ff), _rows(tm, d), _rows(tm, d),
                   _const((1, d)), _const((1, d))],
        out_shape=[jax.ShapeDtypeStruct((s, d), F32), jax.ShapeDtypeStruct((s, ff), BF16), jax.ShapeDtypeStruct((s, ff), BF16),
                   jax.ShapeDtypeStruct((s, ff), BF16), jax.ShapeDtypeStruct((s, d), BF16), jax.ShapeDtypeStruct((s, d), BF16),
                   jax.ShapeDtypeStruct((1, d), F32), jax.ShapeDtypeStruct((1, d), F32)],
        compiler_params=_params(dimension_semantics=("arbitrary",)))(dxo, x, f, a, b, g_pre, g_post, wg_t, wu_t, wd)


def _wgrad(lhs, rhs, name, tk=512):
    s, r = lhs.shape
    c = rhs.shape[1]

    def body(l_ref, r_ref, o_ref):
        _accumulate(o_ref, _dot_tn(l_ref[...], r_ref[...]))

    return pl.pallas_call(body, name=name, grid=(s // tk,), in_specs=[_rows(tk, r), _rows(tk, c)],
                          out_specs=_const((r, c)), out_shape=jax.ShapeDtypeStruct((r, c), F32),
                          compiler_params=_params(dimension_semantics=("arbitrary",)))(lhs, rhs)


def _in_fwd(x, g, w_in_t, name, tm=512):
    s, d = x.shape
    d_in = w_in_t.shape[0]
    n_parts = (d_in - POOL_DIM) // GROUP_DIM

    def body(x_ref, g_ref, w_ref, u_ref, *part_refs):
        xv = x_ref[...]
        hb = (xv * _inv_rms(xv) * g_ref[...]).astype(BF16)
        z = _dot_nt(hb, w_ref[...])
        u_ref[...] = z[:, :POOL_DIM]
        for j, ref in enumerate(part_refs):
            ref[...] = z[:, POOL_DIM + GROUP_DIM * j:POOL_DIM + GROUP_DIM * (j + 1)].astype(BF16)

    return pl.pallas_call(
        body, name=name, grid=(s // tm,), in_specs=[_rows(tm, d), _const((1, d)), _resident((d_in, d))],
        out_specs=[_rows(tm, POOL_DIM)] + [_rows(tm, GROUP_DIM)] * n_parts,
        out_shape=[jax.ShapeDtypeStruct((s, POOL_DIM), F32)] + [jax.ShapeDtypeStruct((s, GROUP_DIM), BF16)] * n_parts,
        compiler_params=_params(dimension_semantics=("arbitrary",)))(x, g, w_in_t)


def _in_bwd(du, dparts, x, dxo, g, w_in_t, name, tm=512):
    s, d = x.shape
    d_in = w_in_t.shape[0]
    n_parts = len(dparts)

    def body(du_ref, *refs):
        part_refs = refs[:n_parts]
        x_ref, dxo_ref, g_ref, w_ref, dx_ref, dz_ref, h_ref, dg_ref = refs[n_parts:]
        dz = jnp.concatenate([du_ref[...].astype(BF16)] + [r[...] for r in part_refs], axis=1)
        dz_ref[...] = dz
        dh = _dot(dz, w_ref[...])
        xv = x_ref[...]
        inv = _inv_rms(xv)
        dxn, dg = _rms_bwd(xv, inv, g_ref[...], dh)
        dx_ref[...] = dxo_ref[...] + dxn
        h_ref[...] = (xv * inv * g_ref[...]).astype(BF16)
        _accumulate(dg_ref, dg)

    return pl.pallas_call(
        body, name=name, grid=(s // tm,),
        in_specs=[_rows(tm, POOL_DIM)] + [_rows(tm, GROUP_DIM)] * n_parts + [_rows(tm, d), _rows(tm, d), _const((1, d)),
                                                                             _resident((d_in, d))],
        out_specs=[_rows(tm, d), _rows(tm, d_in), _rows(tm, d), _const((1, d))],
        out_shape=[jax.ShapeDtypeStruct((s, d), F32), jax.ShapeDtypeStruct((s, d_in), BF16), jax.ShapeDtypeStruct((s, d), BF16),
                   jax.ShapeDtypeStruct((1, d), F32)],
        compiler_params=_params(dimension_semantics=("arbitrary",)))(du, *dparts, x, dxo, g, w_in_t)


_POOL_HALO = 8


def _pool_chain(v, first_shift):
    n = v.shape[0]
    p2 = v + pltpu.roll(v, first_shift, 0)
    p4 = pltpu.roll(p2, 1, 0) + pltpu.roll(p2, n - 1, 0)
    p8 = pltpu.roll(p4, 2, 0) + pltpu.roll(p4, n - 2, 0)
    p16 = pltpu.roll(p8, 4, 0) + pltpu.roll(p8, n - 4, 0)
    group = lax.broadcasted_iota(jnp.int32, v.shape, 1) // HEAD_DIM
    return jnp.where(group == 0, p2, jnp.where(group == 1, p4, jnp.where(group == 2, p8, p16)))


def _pool_count(t0, rows, s):
    t = t0 + lax.broadcasted_iota(jnp.int32, (rows, POOL_DIM), 0)
    group = lax.broadcasted_iota(jnp.int32, (rows, POOL_DIM), 1) // HEAD_DIM
    half = jnp.where(group == 0, 1, jnp.where(group == 1, 2, jnp.where(group == 2, 4, 8)))
    cnt = jnp.minimum(t + half, s) - jnp.maximum(t - half, 0)
    return jnp.maximum(cnt, 1).astype(F32)


def _pad_rows(ref, pad_ref, s):
    zeros = jnp.zeros((_POOL_HALO, pad_ref.shape[1]), pad_ref.dtype)
    pad_ref[pl.ds(0, _POOL_HALO), :] = zeros
    pad_ref[pl.ds(_POOL_HALO + s, _POOL_HALO), :] = zeros
    pad_ref[pl.ds(_POOL_HALO, s), :] = ref[...]


def _pool_fwd(u, w_bd, scale, name, tm=512):
    s = u.shape[0]
    ext = tm + 2 * _POOL_HALO

    def body(u_ref, w_ref, sc_ref, o_ref, upad):
        _pad_rows(u_ref, upad, s)

        def tile(i, carry):
            t0 = pl.multiple_of(i * tm, tm)
            uv = upad[pl.ds(t0, ext), :]
            win = _pool_chain(uv, 1)[_POOL_HALO:_POOL_HALO + tm]
            y = win / _pool_count(t0, tm, s) - uv[_POOL_HALO:_POOL_HALO + tm]
            o_ref[pl.ds(t0, tm), :] = (_dot(y.astype(BF16), w_ref[...]) * sc_ref[...]).astype(BF16)
            return carry

        lax.fori_loop(0, s // tm, tile, 0)

    return pl.pallas_call(body, name=name, out_shape=jax.ShapeDtypeStruct((s, POOL_DIM), BF16),
                          scratch_shapes=[pltpu.VMEM((s + 2 * _POOL_HALO, POOL_DIM), F32)],
                          compiler_params=_params())(u, w_bd, scale)


def _pool_bwd(u, da, w_bd, scale, name, tm=512):
    s = u.shape[0]
    ext = tm + 2 * _POOL_HALO

    def body(u_ref, da_ref, w_ref, sc_ref, du_ref, dw_ref, dsc_ref, upad, dapad):
        _pad_rows(u_ref, upad, s)
        _pad_rows(da_ref, dapad, s)
        dw_ref[...] = jnp.zeros_like(dw_ref)
        dsc_ref[...] = jnp.zeros_like(dsc_ref)

        def tile(i, carry):
            t0 = pl.multiple_of(i * tm, tm)
            uv = upad[pl.ds(t0, ext), :]
            dav = dapad[pl.ds(t0, ext), :]
            win = _pool_chain(uv, 1)[_POOL_HALO:_POOL_HALO + tm]
            yb = (win / _pool_count(t0, tm, s) - uv[_POOL_HALO:_POOL_HALO + tm]).astype(BF16)
            yl = _dot(yb, w_ref[...])
            da_c = dav[_POOL_HALO:_POOL_HALO + tm]
            dsc_ref[...] += jnp.sum(da_c * yl, axis=0, keepdims=True)
            dyl = (dav * sc_ref[...]).astype(BF16)
            dw_ref[...] += _dot_tn(yb, dyl[_POOL_HALO:_POOL_HALO + tm])
            dy = _dot_nt(dyl, w_ref[...])
            dyc = dy / _pool_count(t0 - _POOL_HALO, ext, s)
            du_ref[pl.ds(t0, tm), :] = (_pool_chain(dyc, ext - 1) - dy)[_POOL_HALO:_POOL_HALO + tm]
            return carry

        lax.fori_loop(0, s // tm, tile, 0)

    return pl.pallas_call(
        body, name=name,
        out_shape=[jax.ShapeDtypeStruct((s, POOL_DIM), F32), jax.ShapeDtypeStruct((POOL_DIM, POOL_DIM), F32),
                   jax.ShapeDtypeStruct((1, POOL_DIM), F32)],
        scratch_shapes=[pltpu.VMEM((s + 2 * _POOL_HALO, POOL_DIM), F32), pltpu.VMEM((s + 2 * _POOL_HALO, POOL_DIM), F32)],
        compiler_params=_params())(u, da, w_bd, scale)


_BQ = 128
_KW = _BQ + 2 * N_SIDE
_PAIR = 2 * HEAD_DIM
_NEG = -1e30


def _attn_block_geometry(i, length):
    q0 = pl.multiple_of(i * _BQ, _BQ)
    ws = pl.multiple_of(jnp.clip(q0 - N_SIDE, 0, length - _KW), N_SIDE)
    rel = (lax.broadcasted_iota(jnp.int32, (_BQ, _KW), 1) - lax.broadcasted_iota(jnp.int32, (_BQ, _KW), 0)) + (ws - q0)
    dist = jnp.abs(rel)
    return q0, ws, dist <= N_SIDE, dist.astype(F32)


def _head_slope(slopes_ref, head):
    return slopes_ref[2 * (pl.program_id(0) % 2) + head]


def _attn_fwd(q, k, v, slopes, dilation, name):
    length, width = q.shape

    def body(sl_ref, q_ref, k_ref, v_ref, o_ref, lse_ref):
        lane_head = lax.broadcasted_iota(jnp.int32, (_BQ, _PAIR), 1) // HEAD_DIM

        def block(i, carry):
            q0, ws, valid, dist = _attn_block_geometry(i, length)
            qv = q_ref[pl.ds(q0, _BQ), :]
            kw = k_ref[pl.ds(ws, _KW), :]
            vw = v_ref[pl.ds(ws, _KW), :]
            outs, lses = [], []
            for head in range(2):
                qm = jnp.where(lane_head == head, qv, jnp.zeros_like(qv))
                sc = _dot_nt(qm, kw) * (HEAD_DIM ** -0.5) - (_head_slope(sl_ref, head) * dilation) * dist
                sc = jnp.where(valid, sc, _NEG)
                m = jnp.max(sc, axis=-1, keepdims=True)
                p = jnp.exp(sc - m)
                den = jnp.sum(p, axis=-1, keepdims=True)
                outs.append(_dot(p.astype(BF16), vw) / den)
                lses.append(m + jnp.log(den))
            o_ref[pl.ds(q0, _BQ), :] = jnp.where(lane_head == 0, outs[0], outs[1])
            lse_ref[pl.ds(q0, _BQ), :] = jnp.where(lane_head == 0, lses[0], lses[1])
            return carry

        lax.fori_loop(0, length // _BQ, block, 0)

    col = pl.BlockSpec((length, _PAIR), lambda c: (0, c))
    return pl.pallas_call(
        body, name=name, grid=(width // _PAIR,),
        in_specs=[pl.BlockSpec(memory_space=pltpu.SMEM), col, col, col], out_specs=[col, col],
        out_shape=[jax.ShapeDtypeStruct((length, width), F32), jax.ShapeDtypeStruct((length, width), F32)],
        compiler_params=_params(dimension_semantics=("arbitrary",)))(slopes, q, k, v)


def _attn_bwd(q, k, v, do, lse, cterm, slopes, dilation, name):
    length, width = q.shape

    def body(sl_ref, q_ref, k_ref, v_ref, do_ref, lse_ref, c_ref, dq_ref, dk_ref, dv_ref, dk_acc, dv_acc):
        lane_head = lax.broadcasted_iota(jnp.int32, (_BQ, _PAIR), 1) // HEAD_DIM
        dk_acc[...] = jnp.zeros_like(dk_acc)
        dv_acc[...] = jnp.zeros_like(dv_acc)

        def block(i, carry):
            q0, ws, valid, dist = _attn_block_geometry(i, length)
            qv = q_ref[pl.ds(q0, _BQ), :]
            dov = do_ref[pl.ds(q0, _BQ), :]
            lsev = lse_ref[pl.ds(q0, _BQ), :]
            cv = c_ref[pl.ds(q0, _BQ), :]
            kw = k_ref[pl.ds(ws, _KW), :]
            vw = v_ref[pl.ds(ws, _KW), :]
            dqs = []
            dk_blk = jnp.zeros((_KW, _PAIR), F32)
            dv_blk = jnp.zeros((_KW, _PAIR), F32)
            for head in range(2):
                mine = lane_head == head
                qm = jnp.where(mine, qv, jnp.zeros_like(qv))
                dom = jnp.where(mine, dov, jnp.zeros_like(dov))
                lse_h = jnp.max(jnp.where(mine, lsev, _NEG), axis=-1, keepdims=True)
                c_h = jnp.max(jnp.where(mine, cv, _NEG), axis=-1, keepdims=True)
                sc = _dot_nt(qm, kw) * (HEAD_DIM ** -0.5) - (_head_slope(sl_ref, head) * dilation) * dist
                p = jnp.where(valid, jnp.exp(sc - lse_h), 0.0)
                ds = (p * (_dot_nt(dom, vw) + c_h) * (HEAD_DIM ** -0.5)).astype(BF16)
                dqs.append(_dot(ds, kw))
                dk_blk += _dot_tn(ds, qm)
                dv_blk += _dot_tn(p.astype(BF16), dom)
            dq_ref[pl.ds(q0, _BQ), :] = jnp.where(lane_head == 0, dqs[0], dqs[1]).astype(BF16)
            dk_acc[pl.ds(ws, _KW), :] += dk_blk
            dv_acc[pl.ds(ws, _KW), :] += dv_blk
            return carry

        lax.fori_loop(0, length // _BQ, block, 0)
        dk_ref[...] = dk_acc[...].astype(BF16)
        dv_ref[...] = dv_acc[...].astype(BF16)

    col = pl.BlockSpec((length, _PAIR), lambda c: (0, c))
    return pl.pallas_call(
        body, name=name, grid=(width // _PAIR,),
        in_specs=[pl.BlockSpec(memory_space=pltpu.SMEM), col, col, col, col, col, col], out_specs=[col, col, col],
        out_shape=[jax.ShapeDtypeStruct((length, width), BF16)] * 3,
        scratch_shapes=[pltpu.VMEM((length, _PAIR), F32), pltpu.VMEM((length, _PAIR), F32)],
        compiler_params=_params(dimension_semantics=("arbitrary",)))(slopes, q, k, v, do, lse, cterm)


def _group_weights(lses):
    m = jnp.maximum(jnp.maximum(lses[0], lses[1]), lses[2])
    es = [jnp.exp(l - m) for l in lses]
    den = es[0] + es[1] + es[2]
    return [e / den for e in es]


def _combine_fwd(a_pool, outs, lses, name, tm=512):
    s = a_pool.shape[0]

    def body(ap_ref, o0, o1, o2, l0, l1, l2, cat_ref):
        alphas = _group_weights([l0[...], l1[...], l2[...]])
        parts = [ap_ref[...]] + [(o[...] * al).astype(BF16) for o, al in zip((o0, o1, o2), alphas)]
        cat_ref[...] = jnp.concatenate(parts, axis=1)

    width = POOL_DIM + 3 * GROUP_DIM
    return pl.pallas_call(body, name=name, grid=(s // tm,), in_specs=[_rows(tm, POOL_DIM)] + [_rows(tm, GROUP_DIM)] * 6,
                          out_specs=_rows(tm, width), out_shape=jax.ShapeDtypeStruct((s, width), BF16),
                          compiler_params=_params(dimension_semantics=("arbitrary",)))(a_pool, *outs, *lses)


def _combine_bwd(dcat, outs, lses, head_ones, name, tm=512):
    s = dcat.shape[0]

    def body(dc_ref, o0, o1, o2, l0, l1, l2, ones_ref, do0, do1, do2, c0, c1, c2):
        alphas = _group_weights([l0[...], l1[...], l2[...]])
        dcat_v = dc_ref[...]
        das = [dcat_v[:, POOL_DIM + GROUP_DIM * g:POOL_DIM + GROUP_DIM * (g + 1)] for g in range(3)]
        prod = sum(da * (o[...] * al) for da, o, al in zip(das, (o0, o1, o2), alphas))
        hi = prod.astype(BF16)
        lo = (prod - hi.astype(F32)).astype(BF16)
        total = _dot(hi, ones_ref[...]) + _dot(lo, ones_ref[...])
        for da, al, do_ref, c_ref in zip(das, alphas, (do0, do1, do2), (c0, c1, c2)):
            do_ref[...] = (da * al).astype(BF16)
            c_ref[...] = -al * total

    width = POOL_DIM + 3 * GROUP_DIM
    return pl.pallas_call(
        body, name=name, grid=(s // tm,),
        in_specs=[_rows(tm, width)] + [_rows(tm, GROUP_DIM)] * 6 + [_const((GROUP_DIM, GROUP_DIM))],
        out_specs=[_rows(tm, GROUP_DIM)] * 6,
        out_shape=[jax.ShapeDtypeStruct((s, GROUP_DIM), BF16)] * 3 + [jax.ShapeDtypeStruct((s, GROUP_DIM), F32)] * 3,
        compiler_params=_params(dimension_semantics=("arbitrary",)))(dcat, *outs, *lses, head_ones)


def _out_fwd(cat, x, w_out, g, name, tm=512):
    s, d = x.shape

    def body(cat_ref, x_ref, w_ref, g_ref, xo_ref, mix_ref):
        mix = _dot(cat_ref[...], w_ref[...])
        mix_ref[...] = mix
        xo_ref[...] = x_ref[...] + mix * _inv_rms(mix) * g_ref[...]

    return pl.pallas_call(body, name=name, grid=(s // tm,),
                          in_specs=[_rows(tm, cat.shape[1]), _rows(tm, d), _resident(w_out.shape), _const((1, d))],
                          out_specs=[_rows(tm, d), _rows(tm, d)], out_shape=[jax.ShapeDtypeStruct((s, d), F32)] * 2,
                          compiler_params=_params(dimension_semantics=("arbitrary",)))(cat, x, w_out, g)


def _out_bwd(dxo, mix, w_out, g, name, tm=512):
    s, d = mix.shape
    width = w_out.shape[0]

    def body(dxo_ref, mix_ref, w_ref, g_ref, dcat_ref, dmix_ref, dg_ref):
        mv = mix_ref[...]
        dmix, dg = _rms_bwd(mv, _inv_rms(mv), g_ref[...], dxo_ref[...])
        dmb = dmix.astype(BF16)
        dmix_ref[...] = dmb
        dcat_ref[...] = _dot_nt(dmb, w_ref[...])
        _accumulate(dg_ref, dg)

    return pl.pallas_call(
        body, name=name, grid=(s // tm,), in_specs=[_rows(tm, d), _rows(tm, d), _resident(w_out.shape), _const((1, d))],
        out_specs=[_rows(tm, width), _rows(tm, d), _const((1, d))],
        out_shape=[jax.ShapeDtypeStruct((s, width), F32), jax.ShapeDtypeStruct((s, d), BF16), jax.ShapeDtypeStruct((1, d), F32)],
        compiler_params=_params(dimension_semantics=("arbitrary",)))(dxo, mix, w_out, g)


def _alibi_slopes():
    return np.array([2.0 ** (-8.0 * (i + 1) / N_ATTN_HEADS) for i in range(N_ATTN_HEADS)], np.float32)


def _dilate(a, dilation):
    return a.reshape(a.shape[0] // dilation, a.shape[1] * dilation)


def _block_diag(w_lin):
    n, c, _ = w_lin.shape
    eye = jnp.eye(n, dtype=w_lin.dtype)
    return (eye[:, None, :, None] * w_lin[:, :, None, :]).reshape(n * c, n * c)


def _local_step(x, target, small, full):
    s, d = x.shape
    slopes = _alibi_slopes()
    group_slopes = [jnp.asarray(slopes[4 * g:4 * g + 4]) for g in range(3)]
    w_bd = _block_diag(small["w_pool_lin"]).astype(BF16)
    head_ones = jnp.asarray(np.kron(np.eye(GROUP_DIM // HEAD_DIM), np.ones((HEAD_DIM, HEAD_DIM))), BF16)

    x1, a1, b1, f1 = _ffn_fwd(x, small["g_ffn1_pre"], full["w1_gate"], full["w1_up"], full["w1_down"], small["g_ffn1_post"],
                              None, "ffn1_fwd")
    u, *parts = _in_fwd(x1, small["g_mix_pre"], full["w_in"], "in_fwd")
    qs, ks, vs = parts[0:3], parts[3:6], parts[6:9]
    a_pool = _pool_fwd(u, w_bd, small["pool_scale"], "pool_fwd")
    outs, lses = [], []
    for g, dil in enumerate(DILATIONS):
        o, lse = _attn_fwd(_dilate(qs[g], dil), _dilate(ks[g], dil), _dilate(vs[g], dil), group_slopes[g], dil, f"attn_fwd{g}")
        outs.append(o.reshape(s, GROUP_DIM))
        lses.append(lse.reshape(s, GROUP_DIM))
    cat = _combine_fwd(a_pool, outs, lses, "combine_fwd")
    x2, mix = _out_fwd(cat, x1, full["w_out"], small["g_mix_post"], "out_fwd")
    dx3, a2, b2, f2, loss_part = _ffn_fwd(x2, small["g_ffn2_pre"], full["w2_gate"], full["w2_up"], full["w2_down"],
                                          small["g_ffn2_post"], target, "ffn2_fwd")

    grads, small_grads = {}, {}

    def ffn_backward(tag, dxo, x_in, f, a, b):
        dx, hh, da, db, df, h, dg_pre, dg_post = _ffn_bwd(
            dxo, x_in, f, a, b, small[f"g_{tag}_pre"], small[f"g_{tag}_post"],
            full[f"w{tag[-1]}_gate"], full[f"w{tag[-1]}_up"], full[f"w{tag[-1]}_down"], f"{tag}_bwd")
        grads[f"w{tag[-1]}_down"] = _wgrad(hh, df, f"{tag}_wgrad_down")
        grads[f"w{tag[-1]}_gate"] = _wgrad(da, h, f"{tag}_wgrad_gate")
        grads[f"w{tag[-1]}_up"] = _wgrad(db, h, f"{tag}_wgrad_up")
        small_grads[f"g_{tag}_pre"], small_grads[f"g_{tag}_post"] = dg_pre, dg_post
        return dx

    dx2 = ffn_backward("ffn2", dx3, x2, f2, a2, b2)
    dcat, dmix, small_grads["g_mix_post"] = _out_bwd(dx2, mix, full["w_out"], small["g_mix_post"], "out_bwd")
    grads["w_out"] = _wgrad(cat, dmix, "wgrad_out")
    dos_cs = _combine_bwd(dcat, outs, lses, head_ones, "combine_bwd")
    dos, cs = dos_cs[:3], dos_cs[3:]
    dqs, dks, dvs = [], [], []
    for g, dil in enumerate(DILATIONS):
        dq, dk, dv = _attn_bwd(_dilate(qs[g], dil), _dilate(ks[g], dil), _dilate(vs[g], dil), _dilate(dos[g], dil),
                               _dilate(lses[g], dil), _dilate(cs[g], dil), group_slopes[g], dil, f"attn_bwd{g}")
        dqs.append(dq.reshape(s, GROUP_DIM))
        dks.append(dk.reshape(s, GROUP_DIM))
        dvs.append(dv.reshape(s, GROUP_DIM))
    du, dw_bd, small_grads["pool_scale"] = _pool_bwd(u, dcat[:, :POOL_DIM], w_bd, small["pool_scale"], "pool_bwd")
    n_pool = len(POOL_HALF_WINDOWS)
    small_grads["w_pool_lin"] = jnp.stack(
        [dw_bd[HEAD_DIM * g:HEAD_DIM * (g + 1), HEAD_DIM * g:HEAD_DIM * (g + 1)] for g in range(n_pool)])
    dx1, dz, h2, small_grads["g_mix_pre"] = _in_bwd(du, dqs + dks + dvs, x1, dx2, small["g_mix_pre"], full["w_in"], "in_bwd")
    grads["w_in"] = _wgrad(dz, h2, "wgrad_in")
    dx0 = ffn_backward("ffn1", dx1, x, f1, a1, b1)
    return loss_part[0, 0], dx0, grads, small_grads


SEGMENTS = ("w1_gate", "w1_up", "w1_down", "w_in", "w_out", "w2_gate", "w2_up", "w2_down")
TRANSPOSED = ("w1_gate", "w1_up", "w_in", "w2_gate", "w2_up")
HALF = 512


def _place():
    x, y, c = lax.axis_index("x"), lax.axis_index("y"), lax.axis_index("c")
    other_chips = [(1 - x, y), (x, 1 - y), (1 - x, 1 - y)]
    return x, y, c, other_chips


def _chip_rows(chip, rows):
    return pl.ds(pl.multiple_of((2 * chip[0] + chip[1]) * rows, 16), rows)


def _cols(c):
    return pl.ds(pl.multiple_of(c * HALF, HALF), HALF)


def _cast_shard(w, transpose, name, tm=256):
    r, c = w.shape
    if transpose:
        def body(w_ref, o_ref):
            o_ref[...] = w_ref[...].T.astype(BF16)

        return pl.pallas_call(body, name=name, grid=(r // tm,), in_specs=[pl.BlockSpec((tm, c), lambda i: (i, 0))],
                              out_specs=pl.BlockSpec((c, tm), lambda i: (0, i)), out_shape=jax.ShapeDtypeStruct((c, r), BF16),
                              compiler_params=_params(dimension_semantics=("arbitrary",)))(w)

    def body(w_ref, o_ref):
        o_ref[...] = w_ref[...].astype(BF16)

    return pl.pallas_call(body, name=name, out_shape=jax.ShapeDtypeStruct((r, c), BF16), compiler_params=_params())(w)


def _gather_weights(shards):
    n = len(shards)
    rows = [sh.shape[0] for sh in shards]

    def body(*refs):
        ins, outs = refs[:n], refs[n:2 * n]
        local_sem, send_sems, recv_sems, fwd_send_sems, fwd_recv_sems = refs[2 * n:]
        x, y, c, chips = _place()
        me = (x, y)
        local = [pltpu.make_async_copy(ins[k], outs[k].at[_chip_rows(me, rows[k]), :], local_sem.at[k]) for k in range(n)]
        for cp in local:
            cp.start()

        def ici(j, k, src_chip, to):
            dst = outs[k].at[_chip_rows(src_chip, rows[k]), _cols(c)]
            src = ins[k].at[:, _cols(c)] if src_chip is me else dst
            return pltpu.make_async_remote_copy(src_ref=src, dst_ref=dst, send_sem=send_sems.at[j, k], recv_sem=recv_sems.at[j, k],
                                                device_id=to, device_id_type=MESH)

        def d2d(j, k, src_chip, half):
            blk = outs[k].at[_chip_rows(src_chip, rows[k]), _cols(half)]
            return pltpu.make_async_remote_copy(src_ref=blk, dst_ref=blk, send_sem=fwd_send_sems.at[j, k],
                                                recv_sem=fwd_recv_sems.at[j, k], device_id=(x, y, 1 - c), device_id_type=MESH)

        sends = [ici(j, k, me, (*chip, c)) for j, chip in enumerate(chips) for k in range(n)]
        for cp in sends:
            cp.start()
        forwards = []
        for j, chip in enumerate(chips):
            for k in range(n):
                ici(j, k, chip, (x, y, c)).wait_recv()
                fw = d2d(j, k, chip, c)
                fw.start()
                forwards.append(fw)
        for j, chip in enumerate(chips):
            for k in range(n):
                d2d(j, k, chip, 1 - c).wait_recv()
        for cp in sends + forwards:
            cp.wait_send()
        for cp in local:
            cp.wait()

    any_spec = pl.BlockSpec(memory_space=pl.ANY)
    return pl.pallas_call(
        body, name="gather_weights", in_specs=[any_spec] * n, out_specs=[any_spec] * n,
        out_shape=[jax.ShapeDtypeStruct((N_CHIPS * r, sh.shape[1]), sh.dtype) for r, sh in zip(rows, shards)],
        scratch_shapes=[pltpu.SemaphoreType.DMA((n,)), pltpu.SemaphoreType.DMA((3, n)), pltpu.SemaphoreType.DMA((3, n)),
                        pltpu.SemaphoreType.DMA((3, n)), pltpu.SemaphoreType.DMA((3, n))])(*shards)


def _sibling_halves(grads):
    n = len(grads)

    def body(*refs):
        ins, outs = refs[:n], refs[n:2 * n]
        send_sems, recv_sems = refs[2 * n:]
        x, y, c, _ = _place()
        copies = [pltpu.make_async_remote_copy(src_ref=ins[k].at[:, _cols(1 - c)], dst_ref=outs[k], send_sem=send_sems.at[k],
                                               recv_sem=recv_sems.at[k], device_id=(x, y, 1 - c), device_id_type=MESH)
                  for k in range(n)]
        for cp in copies:
            cp.start()
        for cp in copies:
            cp.wait()

    any_spec = pl.BlockSpec(memory_space=pl.ANY)
    return pl.pallas_call(
        body, name="reduce_sibling", in_specs=[any_spec] * n, out_specs=[any_spec] * n,
        out_shape=[jax.ShapeDtypeStruct((g.shape[0], HALF), F32) for g in grads],
        scratch_shapes=[pltpu.SemaphoreType.DMA((n,)), pltpu.SemaphoreType.DMA((n,))])(*grads)


def _chip_sum(grad, from_sibling, half_index, name):
    r = grad.shape[0]
    tm = r // N_CHIPS

    def body(idx_ref, g_ref, s_ref, o_ref, ob_ref):
        total = g_ref[...] + s_ref[...]
        o_ref[...] = total
        ob_ref[...] = total.astype(BF16)

    return pl.pallas_call(
        body, name=name,
        grid_spec=pltpu.PrefetchScalarGridSpec(
            num_scalar_prefetch=1, grid=(r // tm,),
            in_specs=[pl.BlockSpec((tm, HALF), lambda i, idx: (i, idx[0])), pl.BlockSpec((tm, HALF), lambda i, idx: (i, 0))],
            out_specs=[pl.BlockSpec((tm, HALF), lambda i, idx: (i, 0)), pl.BlockSpec((tm, HALF), lambda i, idx: (i, 0))]),
        out_shape=[jax.ShapeDtypeStruct((r, HALF), F32), jax.ShapeDtypeStruct((r, HALF), BF16)],
        compiler_params=_params(dimension_semantics=("arbitrary",)))(half_index, grad, from_sibling)


def _scatter_chip_sums(sums):
    n = len(sums)
    rows = [sm.shape[0] // N_CHIPS for sm in sums]

    def body(*refs):
        ins, outs = refs[:n], refs[n:2 * n]
        send_sems, recv_sems = refs[2 * n:]
        x, y, c, chips = _place()
        copies = [pltpu.make_async_remote_copy(src_ref=ins[k].at[_chip_rows(chip, rows[k]), :], dst_ref=outs[k].at[j],
                                               send_sem=send_sems.at[j, k], recv_sem=recv_sems.at[j, k],
                                               device_id=(*chip, c), device_id_type=MESH)
                  for j, chip in enumerate(chips) for k in range(n)]
        for cp in copies:
            cp.start()
        for cp in copies:
            cp.wait()

    any_spec = pl.BlockSpec(memory_space=pl.ANY)
    return pl.pallas_call(
        body, name="reduce_chips", in_specs=[any_spec] * n, out_specs=[any_spec] * n,
        out_shape=[jax.ShapeDtypeStruct((3, r, HALF), BF16) for r in rows],
        scratch_shapes=[pltpu.SemaphoreType.DMA((3, n)), pltpu.SemaphoreType.DMA((3, n))])(*sums)


def _total_sum(own, received, name):
    r = own.shape[0]

    def body(o_ref, r_ref, t_ref):
        total = o_ref[...]
        for j in range(3):
            total = total + r_ref[j].astype(F32)
        t_ref[...] = total

    return pl.pallas_call(body, name=name, out_shape=jax.ShapeDtypeStruct((r, HALF), F32), compiler_params=_params())(own, received)


def _join_halves(halves):
    n = len(halves)

    def body(*refs):
        ins, outs = refs[:n], refs[n:2 * n]
        local_sems, send_sems, recv_sems = refs[2 * n:]
        x, y, c, _ = _place()
        local = [pltpu.make_async_copy(ins[k], outs[k].at[:, _cols(c)], local_sems.at[k]) for k in range(n)]
        copies = [pltpu.make_async_remote_copy(src_ref=ins[k], dst_ref=outs[k].at[:, _cols(c)], send_sem=send_sems.at[k],
                                               recv_sem=recv_sems.at[k], device_id=(x, y, 1 - c), device_id_type=MESH)
                  for k in range(n)]
        for cp in local + copies:
            cp.start()
        for k in range(n):
            pltpu.make_async_remote_copy(src_ref=ins[k], dst_ref=outs[k].at[:, _cols(1 - c)], send_sem=send_sems.at[k],
                                         recv_sem=recv_sems.at[k], device_id=(x, y, 1 - c), device_id_type=MESH).wait()
        for cp in local:
            cp.wait()

    any_spec = pl.BlockSpec(memory_space=pl.ANY)
    return pl.pallas_call(
        body, name="join_halves", in_specs=[any_spec] * n, out_specs=[any_spec] * n,
        out_shape=[jax.ShapeDtypeStruct((h.shape[0], 2 * HALF), F32) for h in halves],
        scratch_shapes=[pltpu.SemaphoreType.DMA((n,)), pltpu.SemaphoreType.DMA((n,)), pltpu.SemaphoreType.DMA((n,))])(*halves)


N_DEV = 8


def _gather_small(block):
    m_per, width = block.shape

    def body(x_ref, out_ref, send_sems, recv_sems, local_sem):
        x, y, c, chips = _place()
        me, sibling = (x, y, c), (x, y, 1 - c)

        def rows(px, py, pc):
            return out_ref.at[pl.ds((4 * px + 2 * py + pc) * m_per, m_per), :]

        def copy(k, blk, to, src=None):
            return pltpu.make_async_remote_copy(src_ref=rows(*blk) if src is None else src, dst_ref=rows(*blk),
                                                send_sem=send_sems.at[k], recv_sem=recv_sems.at[k], device_id=to, device_id_type=MESH)

        mine = pltpu.make_async_copy(x_ref, rows(*me), local_sem)
        mine.start()
        first = [copy(0, me, sibling, src=x_ref)] + [copy(1 + j, me, (*chip, c), src=x_ref) for j, chip in enumerate(chips)]
        for cp in first:
            cp.start()
        passed = [copy(4 + j, (*chip, c), sibling) for j, chip in enumerate(chips)]
        for j, chip in enumerate(chips):
            copy(1 + j, (*chip, c), me).wait_recv()
            passed[j].start()
        copy(0, sibling, me).wait_recv()
        for j, chip in enumerate(chips):
            copy(4 + j, (*chip, 1 - c), me).wait_recv()
        for cp in first + passed:
            cp.wait_send()
        mine.wait()

    vmem = pl.BlockSpec(memory_space=pltpu.VMEM)
    return pl.pallas_call(body, name="gather_small", out_shape=jax.ShapeDtypeStruct((N_DEV * m_per, width), F32),
                          in_specs=[vmem], out_specs=vmem,
                          scratch_shapes=[pltpu.SemaphoreType.DMA((7,)), pltpu.SemaphoreType.DMA((7,)),
                                          pltpu.SemaphoreType.DMA])(block)


def _adamw_math(w, g, m, v):
    m = ADAM_B1 * m + (1.0 - ADAM_B1) * g
    v = ADAM_B2 * v + (1.0 - ADAM_B2) * (g * g)
    m_hat = m / (1.0 - ADAM_B1 ** ADAM_STEP)
    v_hat = v / (1.0 - ADAM_B2 ** ADAM_STEP)
    delta = -ADAM_LR * (m_hat / (jnp.sqrt(v_hat) + ADAM_EPS) + ADAM_WD * w)
    return delta, m, v


def _adamw(w, grad, m, v, transposed, name):
    r, c = w.shape
    tm = 256 if transposed else _tile(r, 512)

    def body(w_ref, g_ref, m_ref, v_ref, go_ref, d_ref, mo_ref, vo_ref):
        g = g_ref[...].T if transposed else g_ref[...]
        go_ref[...] = g
        d_ref[...], mo_ref[...], vo_ref[...] = _adamw_math(w_ref[...], g, m_ref[...], v_ref[...])

    blk = pl.BlockSpec((tm, c), lambda i: (i, 0))
    g_spec = pl.BlockSpec((c, tm), lambda i: (0, i)) if transposed else blk
    return pl.pallas_call(body, name=name, grid=(r // tm,), in_specs=[blk, g_spec, blk, blk], out_specs=[blk] * 4,
                          out_shape=[jax.ShapeDtypeStruct((r, c), F32)] * 4,
                          compiler_params=_params(dimension_semantics=("arbitrary",)))(w, grad, m, v)


def _adamw_small(gathered, w, m, v, name):
    def body(ga_ref, w_ref, m_ref, v_ref, go_ref, d_ref, mo_ref, vo_ref):
        g = ga_ref[0]
        for dev in range(1, N_DEV):
            g = g + ga_ref[dev]
        go_ref[...] = g
        d_ref[...], mo_ref[...], vo_ref[...] = _adamw_math(w_ref[...], g, m_ref[...], v_ref[...])

    return pl.pallas_call(body, name=name, out_shape=[jax.ShapeDtypeStruct(w.shape, F32)] * 4,
                          compiler_params=_params())(gathered, w, m, v)


SMALL = ("g_ffn1_pre", "g_ffn1_post", "g_mix_pre", "w_pool_lin", "pool_scale", "g_mix_post", "g_ffn2_pre", "g_ffn2_post")
WEIGHTS = ("g_ffn1_pre", "w1_gate", "w1_up", "w1_down", "g_ffn1_post", "g_mix_pre", "w_in", "w_pool_lin", "pool_scale", "w_out",
           "g_mix_post", "g_ffn2_pre", "w2_gate", "w2_up", "w2_down", "g_ffn2_post")
LANES = 128


def _pack_small(tree):
    flat = jnp.concatenate([tree[k].reshape(-1) for k in SMALL])
    rows = -(-flat.shape[0] // (8 * LANES)) * 8
    return jnp.pad(flat, (0, rows * LANES - flat.shape[0])).reshape(rows, LANES)


def _unpack_small(packed, like):
    flat, out, at = packed.reshape(-1), {}, 0
    for k in SMALL:
        size = math.prod(like[k].shape)
        out[k] = flat[at:at + size].reshape(like[k].shape)
        at += size
    return out


def kernel(x, g_ffn1_pre, w1_gate, w1_up, w1_down, g_ffn1_post, g_mix_pre, w_in, w_pool_lin, pool_scale, w_out, g_mix_post, g_ffn2_pre, w2_gate, w2_up, w2_down, g_ffn2_post, loss_target, m_g_ffn1_pre, m_w1_gate, m_w1_up, m_w1_down, m_g_ffn1_post, m_g_mix_pre, m_w_in, m_w_pool_lin, m_pool_scale, m_w_out, m_g_mix_post, m_g_ffn2_pre, m_w2_gate, m_w2_up, m_w2_down, m_g_ffn2_post, v_g_ffn1_pre, v_w1_gate, v_w1_up, v_w1_down, v_g_ffn1_post, v_g_mix_pre, v_w_in, v_w_pool_lin, v_pool_scale, v_w_out, v_g_mix_post, v_g_ffn2_pre, v_w2_gate, v_w2_up, v_w2_down, v_g_ffn2_post):
    given = dict(locals())
    w = {k: given[k] for k in WEIGHTS}
    m = {k: given["m_" + k] for k in WEIGHTS}
    v = {k: given["v_" + k] for k in WEIGHTS}
    small = {k: (w[k][0] if k == "w_pool_lin" else w[k].reshape(1, -1)) for k in SMALL}

    shards = [_cast_shard(w[k][0], k in TRANSPOSED, f"cast_{k}") for k in SEGMENTS]
    full = dict(zip(SEGMENTS, _gather_weights(shards)))

    loss_part, grad_x, grads, small_grads = _local_step(x[0], loss_target[0], small, full)
    loss = lax.psum(loss_part, ("x", "y", "c"))

    c = lax.axis_index("c")
    chip = 2 * lax.axis_index("x") + lax.axis_index("y")
    half_index = jnp.reshape(c, (1,)).astype(jnp.int32)
    local = [grads[k] for k in SEGMENTS]
    from_sibling = _sibling_halves(local)
    sums = [_chip_sum(g, fs, half_index, f"chip_sum_{k}") for g, fs, k in zip(local, from_sibling, SEGMENTS)]
    received = _scatter_chip_sums([sm[1] for sm in sums])
    halves = []
    for (total, _), rec, k in zip(sums, received, SEGMENTS):
        rows = total.shape[0] // N_CHIPS
        halves.append(_total_sum(lax.dynamic_slice_in_dim(total, chip * rows, rows, axis=0), rec, f"total_{k}"))
    summed = dict(zip(SEGMENTS, _join_halves(halves)))

    out_grad, out_delta, out_m, out_v = {}, {}, {}, {}
    for k in SEGMENTS:
        out_grad[k], out_delta[k], out_m[k], out_v[k] = (
            a[None] for a in _adamw(w[k][0], summed[k], m[k][0], v[k][0], k in TRANSPOSED, f"adamw_{k}"))

    small_grads["w_pool_lin"] = small_grads["w_pool_lin"][None]
    packed = _pack_small(small_grads)
    gathered = _gather_small(packed).reshape(N_DEV, *packed.shape)
    like = {k: w[k] for k in SMALL}
    results = _adamw_small(gathered, _pack_small(like), _pack_small({k: m[k] for k in SMALL}),
                           _pack_small({k: v[k] for k in SMALL}), "adamw_small")
    for tree, res in zip((out_grad, out_delta, out_m, out_v), results):
        tree.update(_unpack_small(res, like))

    return (loss, grad_x[None], *[out_grad[k] for k in WEIGHTS], *[out_delta[k] for k in WEIGHTS],
            *[out_m[k] for k in WEIGHTS], *[out_v[k] for k in WEIGHTS])
```

```python
import functools
import math

import numpy as np
import jax
import jax.numpy as jnp
from jax import lax
from jax.experimental import pallas as pl
from jax.experimental.pallas import tpu as pltpu

F32 = jnp.float32
BF16 = jnp.bfloat16
MESH = pl.DeviceIdType.MESH

RMS_EPS = 1e-6
HEAD_DIM = 64
POOL_HALF_WINDOWS = (1, 2, 4, 8)
POOL_DIM = 256
GROUP_DIM = 256
DILATIONS = (1, 4, 16)
N_SIDE = 64
N_ATTN_HEADS = 12
ADAM_LR, ADAM_B1, ADAM_B2, ADAM_EPS, ADAM_WD, ADAM_STEP = 0.001, 0.9, 0.999, 1e-08, 0.01, 10

N_CHIPS = 4
V7X_VMEM_LIMIT = 60 * 1024 * 1024

_NT = (((1,), (1,)), ((), ()))
_TN = (((0,), (0,)), ((), ()))


def _dot(a, b):
    return jnp.dot(a, b, preferred_element_type=F32)


def _dot_nt(a, b):
    return lax.dot_general(a, b, _NT, preferred_element_type=F32)


def _dot_tn(a, b):
    return lax.dot_general(a, b, _TN, preferred_element_type=F32)


def _params(**kw):
    return pltpu.CompilerParams(vmem_limit_bytes=V7X_VMEM_LIMIT, **kw)


def _rows(tm, width):
    return pl.BlockSpec((tm, width), lambda i: (i, 0))


def _resident(shape):
    return pl.BlockSpec(shape, lambda i: (0,) * len(shape), pipeline_mode=pl.Buffered(1))


def _const(shape):
    return pl.BlockSpec(shape, lambda i: (0,) * len(shape))


def _tile(rows, cap):
    return max(t for t in range(16, cap + 1, 16) if rows % t == 0)


def _inv_rms(x):
    return lax.rsqrt(jnp.mean(x * x, axis=-1, keepdims=True) + RMS_EPS)


def _rms_bwd(x, inv, g, dy):
    n = x * inv
    dn = dy * g
    dx = inv * (dn - n * jnp.mean(dn * n, axis=-1, keepdims=True))
    return dx, jnp.sum(dy * n, axis=0, keepdims=True)


def _accumulate(ref, value):
    @pl.when(pl.program_id(0) == 0)
    def _():
        ref[...] = jnp.zeros_like(ref)

    ref[...] += value


def _ffn_fwd(x, g_pre, wg_t, wu_t, wd, g_post, target, name, tm=256):
    s, d = x.shape
    ff = wd.shape[0]
    with_loss = target is not None

    def body(*refs):
        if with_loss:
            x_ref, gpre_ref, wg_ref, wu_ref, wd_ref, gpost_ref, t_ref, xo_ref, a_ref, b_ref, f_ref, loss_ref = refs
        else:
            x_ref, gpre_ref, wg_ref, wu_ref, wd_ref, gpost_ref, xo_ref, a_ref, b_ref, f_ref = refs
        xv = x_ref[...]
        hb = (xv * _inv_rms(xv) * gpre_ref[...]).astype(BF16)
        a = _dot_nt(hb, wg_ref[...])
        b = _dot_nt(hb, wu_ref[...])
        hh = (a * jax.nn.sigmoid(a)) * b
        f = _dot(hh.astype(BF16), wd_ref[...])
        xo = xv + 0.5 * (f * _inv_rms(f) * gpost_ref[...])
        a_ref[...] = a.astype(BF16)
        b_ref[...] = b.astype(BF16)
        f_ref[...] = f
        if with_loss:
            e = xo - t_ref[...]
            xo_ref[...] = e * (1.0 / d)
            _accumulate(loss_ref, 0.5 * jnp.sum(jnp.mean(e * e, axis=-1, keepdims=True)))
        else:
            xo_ref[...] = xo

    in_specs = [_rows(tm, d), _const((1, d)), _resident((ff, d)), _resident((ff, d)), _resident((ff, d)), _const((1, d))]
    args = [x, g_pre, wg_t, wu_t, wd, g_post]
    out_shape = [jax.ShapeDtypeStruct((s, d), F32), jax.ShapeDtypeStruct((s, ff), BF16),
                 jax.ShapeDtypeStruct((s, ff), BF16), jax.ShapeDtypeStruct((s, d), F32)]
    out_specs = [_rows(tm, d), _rows(tm, ff), _rows(tm, ff), _rows(tm, d)]
    if with_loss:
        in_specs.append(_rows(tm, d))
        args.append(target)
        out_shape.append(jax.ShapeDtypeStruct((8, 128), F32))
        out_specs.append(_const((8, 128)))
    return pl.pallas_call(body, name=name, grid=(s // tm,), in_specs=in_specs, out_specs=out_specs,
                          out_shape=out_shape, compiler_params=_params(dimension_semantics=("arbitrary",)))(*args)


def _ffn_bwd(dxo, x, f, a, b, g_pre, g_post, wg_t, wu_t, wd, name, tm=256):
    s, d = x.shape
    ff = wd.shape[0]

    def body(dxo_ref, x_ref, f_ref, a_ref, b_ref, gpre_ref, gpost_ref, wg_ref, wu_ref, wd_ref,
             dx_ref, hh_ref, da_ref, db_ref, df_ref, h_ref, dgpre_ref, dgpost_ref):
        dxo_v = dxo_ref[...]
        fv = f_ref[...]
        df, dgpost = _rms_bwd(fv, _inv_rms(fv), gpost_ref[...], 0.5 * dxo_v)
        dfb = df.astype(BF16)
        dhh = _dot_nt(dfb, wd_ref[...])
        av = a_ref[...].astype(F32)
        bv = b_ref[...].astype(F32)
        sig = jax.nn.sigmoid(av)
        sa = av * sig
        da = (dhh * bv * (sig * (1.0 + av * (1.0 - sig)))).astype(BF16)
        db = (dhh * sa).astype(BF16)
        dh = _dot(da, wg_ref[...]) + _dot(db, wu_ref[...])
        xv = x_ref[...]
        inv = _inv_rms(xv)
        dxn, dgpre = _rms_bwd(xv, inv, gpre_ref[...], dh)
        dx_ref[...] = dxo_v + dxn
        hh_ref[...] = (sa * bv).astype(BF16)
        da_ref[...] = da
        db_ref[...] = db
        df_ref[...] = dfb
        h_ref[...] = (xv * inv * gpre_ref[...]).astype(BF16)
        _accumulate(dgpre_ref, dgpre)
        _accumulate(dgpost_ref, dgpost)

    return pl.pallas_call(
        body, name=name, grid=(s // tm,),
        in_specs=[_rows(tm, d), _rows(tm, d), _rows(tm, d), _rows(tm, ff), _rows(tm, ff), _const((1, d)), _const((1, d)),
                  _resident((ff, d)), _resident((ff, d)), _resident((ff, d))],
        out_specs=[_rows(tm, d), _rows(tm, ff), _rows(tm, ff), _rows(tm, ff), _rows(tm, d), _rows(tm, d),
                   _const((1, d)), _const((1, d))],
        out_shape=[jax.ShapeDtypeStruct((s, d), F32), jax.ShapeDtypeStruct((s, ff), BF16), jax.ShapeDtypeStruct((s, ff), BF16),
                   jax.ShapeDtypeStruct((s, ff), BF16), jax.ShapeDtypeStruct((s, d), BF16), jax.ShapeDtypeStruct((s, d), BF16),
                   jax.ShapeDtypeStruct((1, d), F32), jax.ShapeDtypeStruct((1, d), F32)],
        compiler_params=_params(dimension_semantics=("arbitrary",)))(dxo, x, f, a, b, g_pre, g_post, wg_t, wu_t, wd)


def _wgrad(lhs, rhs, name, tk=512):
    s, r = lhs.shape
    c = rhs.shape[1]

    def body(l_ref, r_ref, o_ref):
        _accumulate(o_ref, _dot_tn(l_ref[...], r_ref[...]))

    return pl.pallas_call(body, name=name, grid=(s // tk,), in_specs=[_rows(tk, r), _rows(tk, c)],
                          out_specs=_const((r, c)), out_shape=jax.ShapeDtypeStruct((r, c), F32),
                          compiler_params=_params(dimension_semantics=("arbitrary",)))(lhs, rhs)


def _in_fwd(x, g, w_in_t, name, tm=512):
    s, d = x.shape
    d_in = w_in_t.shape[0]
    n_parts = (d_in - POOL_DIM) // GROUP_DIM

    def body(x_ref, g_ref, w_ref, u_ref, *part_refs):
        xv = x_ref[...]
        hb = (xv * _inv_rms(xv) * g_ref[...]).astype(BF16)
        z = _dot_nt(hb, w_ref[...])
        u_ref[...] = z[:, :POOL_DIM]
        for j, ref in enumerate(part_refs):
            ref[...] = z[:, POOL_DIM + GROUP_DIM * j:POOL_DIM + GROUP_DIM * (j + 1)]

    return pl.pallas_call(
        body, name=name, grid=(s // tm,), in_specs=[_rows(tm, d), _const((1, d)), _resident((d_in, d))],
        out_specs=[_rows(tm, POOL_DIM)] + [_rows(tm, GROUP_DIM)] * n_parts,
        out_shape=[jax.ShapeDtypeStruct((s, POOL_DIM), F32)] + [jax.ShapeDtypeStruct((s, GROUP_DIM), F32)] * n_parts,
        compiler_params=_params(dimension_semantics=("arbitrary",)))(x, g, w_in_t)


def _in_bwd(du, dparts, x, dxo, g, w_in_t, name, tm=512):
    s, d = x.shape
    d_in = w_in_t.shape[0]
    n_parts = len(dparts)

    def body(du_ref, *refs):
        part_refs = refs[:n_parts]
        x_ref, dxo_ref, g_ref, w_ref, dx_ref, dz_ref, h_ref, dg_ref = refs[n_parts:]
        dz = jnp.concatenate([r[...].astype(BF16) for r in (du_ref,) + part_refs], axis=1)
        dz_ref[...] = dz
        dh = _dot(dz, w_ref[...])
        xv = x_ref[...]
        inv = _inv_rms(xv)
        dxn, dg = _rms_bwd(xv, inv, g_ref[...], dh)
        dx_ref[...] = dxo_ref[...] + dxn
        h_ref[...] = (xv * inv * g_ref[...]).astype(BF16)
        _accumulate(dg_ref, dg)

    return pl.pallas_call(
        body, name=name, grid=(s // tm,),
        in_specs=[_rows(tm, POOL_DIM)] + [_rows(tm, GROUP_DIM)] * n_parts + [_rows(tm, d), _rows(tm, d), _const((1, d)),
                                                                             _resident((d_in, d))],
        out_specs=[_rows(tm, d), _rows(tm, d_in), _rows(tm, d), _const((1, d))],
        out_shape=[jax.ShapeDtypeStruct((s, d), F32), jax.ShapeDtypeStruct((s, d_in), BF16), jax.ShapeDtypeStruct((s, d), BF16),
                   jax.ShapeDtypeStruct((1, d), F32)],
        compiler_params=_params(dimension_semantics=("arbitrary",)))(du, *dparts, x, dxo, g, w_in_t)


_POOL_HALO = 8


def _pool_chain(v, first_shift):
    n = v.shape[0]
    p2 = v + pltpu.roll(v, first_shift, 0)
    p4 = pltpu.roll(p2, 1, 0) + pltpu.roll(p2, n - 1, 0)
    p8 = pltpu.roll(p4, 2, 0) + pltpu.roll(p4, n - 2, 0)
    p16 = pltpu.roll(p8, 4, 0) + pltpu.roll(p8, n - 4, 0)
    group = lax.broadcasted_iota(jnp.int32, v.shape, 1) // HEAD_DIM
    return jnp.where(group == 0, p2, jnp.where(group == 1, p4, jnp.where(group == 2, p8, p16)))


def _pool_count(t0, rows, s):
    t = t0 + lax.broadcasted_iota(jnp.int32, (rows, POOL_DIM), 0)
    group = lax.broadcasted_iota(jnp.int32, (rows, POOL_DIM), 1) // HEAD_DIM
    half = jnp.where(group == 0, 1, jnp.where(group == 1, 2, jnp.where(group == 2, 4, 8)))
    cnt = jnp.minimum(t + half, s) - jnp.maximum(t - half, 0)
    return jnp.maximum(cnt, 1).astype(F32)


def _pad_rows(ref, pad_ref, s):
    zeros = jnp.zeros((_POOL_HALO, pad_ref.shape[1]), pad_ref.dtype)
    pad_ref[pl.ds(0, _POOL_HALO), :] = zeros
    pad_ref[pl.ds(_POOL_HALO + s, _POOL_HALO), :] = zeros
    pad_ref[pl.ds(_POOL_HALO, s), :] = ref[...]


def _pool_fwd(u, w_bd, scale, name, tm=512):
    s = u.shape[0]
    ext = tm + 2 * _POOL_HALO

    def body(u_ref, w_ref, sc_ref, o_ref, upad):
        _pad_rows(u_ref, upad, s)

        def tile(i, carry):
            t0 = pl.multiple_of(i * tm, tm)
            uv = upad[pl.ds(t0, ext), :]
            win = _pool_chain(uv, 1)[_POOL_HALO:_POOL_HALO + tm]
            y = win / _pool_count(t0, tm, s) - uv[_POOL_HALO:_POOL_HALO + tm]
            o_ref[pl.ds(t0, tm), :] = (_dot(y.astype(BF16), w_ref[...]) * sc_ref[...]).astype(BF16)
            return carry

        lax.fori_loop(0, s // tm, tile, 0)

    return pl.pallas_call(body, name=name, out_shape=jax.ShapeDtypeStruct((s, POOL_DIM), BF16),
                          scratch_shapes=[pltpu.VMEM((s + 2 * _POOL_HALO, POOL_DIM), F32)],
                          compiler_params=_params())(u, w_bd, scale)


def _pool_bwd(u, da, w_bd, scale, name, tm=512):
    s = u.shape[0]
    ext = tm + 2 * _POOL_HALO

    def body(u_ref, da_ref, w_ref, sc_ref, du_ref, dw_ref, dsc_ref, upad, dapad):
        _pad_rows(u_ref, upad, s)
        _pad_rows(da_ref, dapad, s)
        dw_ref[...] = jnp.zeros_like(dw_ref)
        dsc_ref[...] = jnp.zeros_like(dsc_ref)

        def tile(i, carry):
            t0 = pl.multiple_of(i * tm, tm)
            uv = upad[pl.ds(t0, ext), :]
            dav = dapad[pl.ds(t0, ext), :]
            win = _pool_chain(uv, 1)[_POOL_HALO:_POOL_HALO + tm]
            yb = (win / _pool_count(t0, tm, s) - uv[_POOL_HALO:_POOL_HALO + tm]).astype(BF16)
            yl = _dot(yb, w_ref[...])
            da_c = dav[_POOL_HALO:_POOL_HALO + tm]
            dsc_ref[...] += jnp.sum(da_c * yl, axis=0, keepdims=True)
            dyl = (dav * sc_ref[...]).astype(BF16)
            dw_ref[...] += _dot_tn(yb, dyl[_POOL_HALO:_POOL_HALO + tm])
            dy = _dot_nt(dyl, w_ref[...])
            dyc = dy / _pool_count(t0 - _POOL_HALO, ext, s)
            du_ref[pl.ds(t0, tm), :] = (_pool_chain(dyc, ext - 1) - dy)[_POOL_HALO:_POOL_HALO + tm]
            return carry

        lax.fori_loop(0, s // tm, tile, 0)

    pool_cols = pl.BlockSpec((s, POOL_DIM), lambda i: (0, 0), pipeline_mode=pl.Buffered(1))
    return pl.pallas_call(
        body, name=name, grid=(1,),
        in_specs=[pool_cols, pool_cols, _const((POOL_DIM, POOL_DIM)), _const((1, POOL_DIM))],
        out_specs=[_const((s, POOL_DIM)), _const((POOL_DIM, POOL_DIM)), _const((1, POOL_DIM))],
        out_shape=[jax.ShapeDtypeStruct((s, POOL_DIM), F32), jax.ShapeDtypeStruct((POOL_DIM, POOL_DIM), F32),
                   jax.ShapeDtypeStruct((1, POOL_DIM), F32)],
        scratch_shapes=[pltpu.VMEM((s + 2 * _POOL_HALO, POOL_DIM), F32), pltpu.VMEM((s + 2 * _POOL_HALO, POOL_DIM), F32)],
        compiler_params=_params(dimension_semantics=("arbitrary",)))(u, da, w_bd, scale)


_BQ = 128
_KW = _BQ + 2 * N_SIDE
_PAIR = 2 * HEAD_DIM
_NEG = -1e30


def _attn_block_geometry(i, length):
    q0 = pl.multiple_of(i * _BQ, _BQ)
    ws = pl.multiple_of(jnp.clip(q0 - N_SIDE, 0, length - _KW), N_SIDE)
    rel = (lax.broadcasted_iota(jnp.int32, (_BQ, _KW), 1) - lax.broadcasted_iota(jnp.int32, (_BQ, _KW), 0)) + (ws - q0)
    dist = jnp.abs(rel)
    return q0, ws, dist <= N_SIDE, dist.astype(F32)


def _head_slope(slopes_ref, head):
    return slopes_ref[2 * pl.program_id(0) + head]


def _residue_rows(dilation, start, count):
    if dilation == 1:
        return pl.ds(start, count)
    return pl.ds(start * dilation + pl.program_id(1), count, stride=dilation)


def _attn_call(body, name, dilation, seq, n_in, n_out, scratch):
    col = pl.BlockSpec((seq, _PAIR), lambda c, r: (0, c), pipeline_mode=pl.Buffered(1))
    return pl.pallas_call(
        body, name=name, grid=(GROUP_DIM // _PAIR, dilation),
        in_specs=[pl.BlockSpec(memory_space=pltpu.SMEM)] + [col] * n_in, out_specs=[col] * n_out,
        out_shape=[jax.ShapeDtypeStruct((seq, GROUP_DIM), F32)] * n_out, scratch_shapes=scratch,
        compiler_params=_params(dimension_semantics=("arbitrary", "arbitrary")))


def _attn_fwd(q, k, v, slopes, dilation, name):
    seq = q.shape[0]
    length = seq // dilation

    def body(sl_ref, q_ref, k_ref, v_ref, o_ref, lse_ref, qs, ks, vs):
        lane_head = lax.broadcasted_iota(jnp.int32, (_BQ, _PAIR), 1) // HEAD_DIM
        for src, dst in ((q_ref, qs), (k_ref, ks), (v_ref, vs)):
            dst[...] = src[_residue_rows(dilation, 0, length), :].astype(BF16)

        def block(i, carry):
            q0, ws, valid, dist = _attn_block_geometry(i, length)
            qv = qs[pl.ds(q0, _BQ), :]
            kw = ks[pl.ds(ws, _KW), :]
            vw = vs[pl.ds(ws, _KW), :]
            outs, lses = [], []
            for head in range(2):
                qm = jnp.where(lane_head == head, qv, jnp.zeros_like(qv))
                sc = _dot_nt(qm, kw) * (HEAD_DIM ** -0.5) - (_head_slope(sl_ref, head) * dilation) * dist
                sc = jnp.where(valid, sc, _NEG)
                m = jnp.max(sc, axis=-1, keepdims=True)
                p = jnp.exp(sc - m)
                den = jnp.sum(p, axis=-1, keepdims=True)
                outs.append(_dot(p.astype(BF16), vw) / den)
                lses.append(m + jnp.log(den))
            rows = _residue_rows(dilation, q0, _BQ)
            o_ref[rows, :] = jnp.where(lane_head == 0, outs[0], outs[1])
            lse_ref[rows, :] = jnp.where(lane_head == 0, lses[0], lses[1])
            return carry

        lax.fori_loop(0, length // _BQ, block, 0)

    stage = pltpu.VMEM((length, _PAIR), BF16)
    return _attn_call(body, name, dilation, seq, 3, 2, [stage] * 3)(slopes, q, k, v)


def _attn_bwd(q, k, v, do, lse, cterm, slopes, dilation, name):
    seq = q.shape[0]
    length = seq // dilation

    def body(sl_ref, q_ref, k_ref, v_ref, do_ref, lse_ref, c_ref, dq_ref, dk_ref, dv_ref, qs, ks, vs, dos, dk_acc, dv_acc):
        lane_head = lax.broadcasted_iota(jnp.int32, (_BQ, _PAIR), 1) // HEAD_DIM
        for src, dst in ((q_ref, qs), (k_ref, ks), (v_ref, vs), (do_ref, dos)):
            dst[...] = src[_residue_rows(dilation, 0, length), :].astype(BF16)
        dk_acc[...] = jnp.zeros_like(dk_acc)
        dv_acc[...] = jnp.zeros_like(dv_acc)

        def block(i, carry):
            q0, ws, valid, dist = _attn_block_geometry(i, length)
            rows = _residue_rows(dilation, q0, _BQ)
            qv = qs[pl.ds(q0, _BQ), :]
            dov = dos[pl.ds(q0, _BQ), :]
            lsev = lse_ref[rows, :]
            cv = c_ref[rows, :]
            kw = ks[pl.ds(ws, _KW), :]
            vw = vs[pl.ds(ws, _KW), :]
            dqs = []
            dk_blk = jnp.zeros((_KW, _PAIR), F32)
            dv_blk = jnp.zeros((_KW, _PAIR), F32)
            for head in range(2):
                mine = lane_head == head
                qm = jnp.where(mine, qv, jnp.zeros_like(qv))
                dom = jnp.where(mine, dov, jnp.zeros_like(dov))
                lse_h = jnp.max(jnp.where(mine, lsev, _NEG), axis=-1, keepdims=True)
                c_h = jnp.max(jnp.where(mine, cv, _NEG), axis=-1, keepdims=True)
                sc = _dot_nt(qm, kw) * (HEAD_DIM ** -0.5) - (_head_slope(sl_ref, head) * dilation) * dist
                p = jnp.where(valid, jnp.exp(sc - lse_h), 0.0)
                ds = (p * (_dot_nt(dom, vw) + c_h) * (HEAD_DIM ** -0.5)).astype(BF16)
                dqs.append(_dot(ds, kw))
                dk_blk += _dot_tn(ds, qm)
                dv_blk += _dot_tn(p.astype(BF16), dom)
            dq_ref[rows, :] = jnp.where(lane_head == 0, dqs[0], dqs[1])
            dk_acc[pl.ds(ws, _KW), :] += dk_blk
            dv_acc[pl.ds(ws, _KW), :] += dv_blk
            return carry

        lax.fori_loop(0, length // _BQ, block, 0)
        all_rows = _residue_rows(dilation, 0, length)
        dk_ref[all_rows, :] = dk_acc[...]
        dv_ref[all_rows, :] = dv_acc[...]

    stage = pltpu.VMEM((length, _PAIR), BF16)
    acc = pltpu.VMEM((length, _PAIR), F32)
    return _attn_call(body, name, dilation, seq, 6, 3, [stage] * 4 + [acc] * 2)(slopes, q, k, v, do, lse, cterm)


def _group_weights(lses):
    m = jnp.maximum(jnp.maximum(lses[0], lses[1]), lses[2])
    es = [jnp.exp(l - m) for l in lses]
    den = es[0] + es[1] + es[2]
    return [e / den for e in es]


def _combine_fwd(a_pool, outs, lses, name, tm=512):
    s = a_pool.shape[0]

    def body(ap_ref, o0, o1, o2, l0, l1, l2, cat_ref):
        alphas = _group_weights([l0[...], l1[...], l2[...]])
        parts = [ap_ref[...]] + [(o[...] * al).astype(BF16) for o, al in zip((o0, o1, o2), alphas)]
        cat_ref[...] = jnp.concatenate(parts, axis=1)

    width = POOL_DIM + 3 * GROUP_DIM
    return pl.pallas_call(body, name=name, grid=(s // tm,), in_specs=[_rows(tm, POOL_DIM)] + [_rows(tm, GROUP_DIM)] * 6,
                          out_specs=_rows(tm, width), out_shape=jax.ShapeDtypeStruct((s, width), BF16),
                          compiler_params=_params(dimension_semantics=("arbitrary",)))(a_pool, *outs, *lses)


def _combine_bwd(dcat, outs, lses, head_ones, name, tm=512):
    s = dcat.shape[0]

    def body(dc_ref, o0, o1, o2, l0, l1, l2, ones_ref, do0, do1, do2, c0, c1, c2):
        alphas = _group_weights([l0[...], l1[...], l2[...]])
        dcat_v = dc_ref[...]
        das = [dcat_v[:, POOL_DIM + GROUP_DIM * g:POOL_DIM + GROUP_DIM * (g + 1)] for g in range(3)]
        prod = sum(da * (o[...] * al) for da, o, al in zip(das, (o0, o1, o2), alphas))
        hi = prod.astype(BF16)
        lo = (prod - hi.astype(F32)).astype(BF16)
        total = _dot(hi, ones_ref[...]) + _dot(lo, ones_ref[...])
        for da, al, do_ref, c_ref in zip(das, alphas, (do0, do1, do2), (c0, c1, c2)):
            do_ref[...] = da * al
            c_ref[...] = -al * total

    width = POOL_DIM + 3 * GROUP_DIM
    return pl.pallas_call(
        body, name=name, grid=(s // tm,),
        in_specs=[_rows(tm, width)] + [_rows(tm, GROUP_DIM)] * 6 + [_const((GROUP_DIM, GROUP_DIM))],
        out_specs=[_rows(tm, GROUP_DIM)] * 6,
        out_shape=[jax.ShapeDtypeStruct((s, GROUP_DIM), F32)] * 6,
        compiler_params=_params(dimension_semantics=("arbitrary",)))(dcat, *outs, *lses, head_ones)


def _out_fwd(cat, x, w_out, g, name, tm=512):
    s, d = x.shape

    def body(cat_ref, x_ref, w_ref, g_ref, xo_ref, mix_ref):
        mix = _dot(cat_ref[...], w_ref[...])
        mix_ref[...] = mix
        xo_ref[...] = x_ref[...] + mix * _inv_rms(mix) * g_ref[...]

    return pl.pallas_call(body, name=name, grid=(s // tm,),
                          in_specs=[_rows(tm, cat.shape[1]), _rows(tm, d), _resident(w_out.shape), _const((1, d))],
                          out_specs=[_rows(tm, d), _rows(tm, d)], out_shape=[jax.ShapeDtypeStruct((s, d), F32)] * 2,
                          compiler_params=_params(dimension_semantics=("arbitrary",)))(cat, x, w_out, g)


def _out_bwd(dxo, mix, w_out, g, name, tm=512):
    s, d = mix.shape
    width = w_out.shape[0]

    def body(dxo_ref, mix_ref, w_ref, g_ref, dcat_ref, dmix_ref, dg_ref):
        mv = mix_ref[...]
        dmix, dg = _rms_bwd(mv, _inv_rms(mv), g_ref[...], dxo_ref[...])
        dmb = dmix.astype(BF16)
        dmix_ref[...] = dmb
        dcat_ref[...] = _dot_nt(dmb, w_ref[...])
        _accumulate(dg_ref, dg)

    return pl.pallas_call(
        body, name=name, grid=(s // tm,), in_specs=[_rows(tm, d), _rows(tm, d), _resident(w_out.shape), _const((1, d))],
        out_specs=[_rows(tm, width), _rows(tm, d), _const((1, d))],
        out_shape=[jax.ShapeDtypeStruct((s, width), F32), jax.ShapeDtypeStruct((s, d), BF16), jax.ShapeDtypeStruct((1, d), F32)],
        compiler_params=_params(dimension_semantics=("arbitrary",)))(dxo, mix, w_out, g)


def _alibi_slopes():
    return np.array([2.0 ** (-8.0 * (i + 1) / N_ATTN_HEADS) for i in range(N_ATTN_HEADS)], np.float32)


def _block_diag(w_lin):
    n, c, _ = w_lin.shape
    eye = jnp.eye(n, dtype=w_lin.dtype)
    return (eye[:, None, :, None] * w_lin[:, :, None, :]).reshape(n * c, n * c)


def _local_step(x, target, small, full):
    s, d = x.shape
    slopes = _alibi_slopes()
    group_slopes = [jnp.asarray(slopes[4 * g:4 * g + 4]) for g in range(3)]
    w_bd = _block_diag(small["w_pool_lin"]).astype(BF16)
    head_ones = jnp.asarray(np.kron(np.eye(GROUP_DIM // HEAD_DIM), np.ones((HEAD_DIM, HEAD_DIM))), BF16)

    x1, a1, b1, f1 = _ffn_fwd(x, small["g_ffn1_pre"], full["w1_gate"], full["w1_up"], full["w1_down"], small["g_ffn1_post"],
                              None, "ffn1_fwd")
    u, *parts = _in_fwd(x1, small["g_mix_pre"], full["w_in"], "in_fwd")
    qs, ks, vs = parts[0:3], parts[3:6], parts[6:9]
    a_pool = _pool_fwd(u, w_bd, small["pool_scale"], "pool_fwd")
    outs, lses = [], []
    for g, dil in enumerate(DILATIONS):
        o, lse = _attn_fwd(qs[g], ks[g], vs[g], group_slopes[g], dil, f"attn_fwd{g}")
        outs.append(o)
        lses.append(lse)
    cat = _combine_fwd(a_pool, outs, lses, "combine_fwd")
    x2, mix = _out_fwd(cat, x1, full["w_out"], small["g_mix_post"], "out_fwd")
    dx3, a2, b2, f2, loss_part = _ffn_fwd(x2, small["g_ffn2_pre"], full["w2_gate"], full["w2_up"], full["w2_down"],
                                          small["g_ffn2_post"], target, "ffn2_fwd")

    grads, small_grads = {}, {}

    def ffn_backward(tag, dxo, x_in, f, a, b):
        dx, hh, da, db, df, h, dg_pre, dg_post = _ffn_bwd(
            dxo, x_in, f, a, b, small[f"g_{tag}_pre"], small[f"g_{tag}_post"],
            full[f"w{tag[-1]}_gate"], full[f"w{tag[-1]}_up"], full[f"w{tag[-1]}_down"], f"{tag}_bwd")
        grads[f"w{tag[-1]}_down"] = _wgrad(hh, df, f"{tag}_wgrad_down")
        grads[f"w{tag[-1]}_gate"] = _wgrad(da, h, f"{tag}_wgrad_gate")
        grads[f"w{tag[-1]}_up"] = _wgrad(db, h, f"{tag}_wgrad_up")
        small_grads[f"g_{tag}_pre"], small_grads[f"g_{tag}_post"] = dg_pre, dg_post
        return dx

    dx2 = ffn_backward("ffn2", dx3, x2, f2, a2, b2)
    dcat, dmix, small_grads["g_mix_post"] = _out_bwd(dx2, mix, full["w_out"], small["g_mix_post"], "out_bwd")
    grads["w_out"] = _wgrad(cat, dmix, "wgrad_out")
    dos_cs = _combine_bwd(dcat, outs, lses, head_ones, "combine_bwd")
    dos, cs = dos_cs[:3], dos_cs[3:]
    dqs, dks, dvs = [], [], []
    for g, dil in enumerate(DILATIONS):
        dq, dk, dv = _attn_bwd(qs[g], ks[g], vs[g], dos[g], lses[g], cs[g], group_slopes[g], dil, f"attn_bwd{g}")
        dqs.append(dq)
        dks.append(dk)
        dvs.append(dv)
    du, dw_bd, small_grads["pool_scale"] = _pool_bwd(u, dcat, w_bd, small["pool_scale"], "pool_bwd")
    n_pool = len(POOL_HALF_WINDOWS)
    small_grads["w_pool_lin"] = jnp.stack(
        [dw_bd[HEAD_DIM * g:HEAD_DIM * (g + 1), HEAD_DIM * g:HEAD_DIM * (g + 1)] for g in range(n_pool)])
    dx1, dz, h2, small_grads["g_mix_pre"] = _in_bwd(du, dqs + dks + dvs, x1, dx2, small["g_mix_pre"], full["w_in"], "in_bwd")
    grads["w_in"] = _wgrad(dz, h2, "wgrad_in")
    dx0 = ffn_backward("ffn1", dx1, x, f1, a1, b1)
    return loss_part[0, 0], dx0, grads, small_grads


SEGMENTS = ("w1_gate", "w1_up", "w1_down", "w_in", "w_out", "w2_gate", "w2_up", "w2_down")
TRANSPOSED = ("w1_gate", "w1_up", "w_in", "w2_gate", "w2_up")
HALF = 512


def _place():
    x, y, c = lax.axis_index("x"), lax.axis_index("y"), lax.axis_index("c")
    other_chips = [(1 - x, y), (x, 1 - y), (1 - x, 1 - y)]
    return x, y, c, other_chips


def _chip_rows(chip, rows):
    return pl.ds(pl.multiple_of((2 * chip[0] + chip[1]) * rows, 16), rows)


def _cols(c):
    return pl.ds(pl.multiple_of(c * HALF, HALF), HALF)


def _cast_shard(w, transpose, name, tm=256):
    r, c = w.shape
    if transpose:
        def body(w_ref, o_ref):
            o_ref[...] = w_ref[...].T.astype(BF16)

        return pl.pallas_call(body, name=name, grid=(r // tm,), in_specs=[pl.BlockSpec((tm, c), lambda i: (i, 0))],
                              out_specs=pl.BlockSpec((c, tm), lambda i: (0, i)), out_shape=jax.ShapeDtypeStruct((c, r), BF16),
                              compiler_params=_params(dimension_semantics=("arbitrary",)))(w)

    def body(w_ref, o_ref):
        o_ref[...] = w_ref[...].astype(BF16)

    return pl.pallas_call(body, name=name, out_shape=jax.ShapeDtypeStruct((r, c), BF16), compiler_params=_params())(w)


def _gather_weights(shards):
    n = len(shards)
    rows = [sh.shape[0] for sh in shards]

    def body(*refs):
        ins, outs = refs[:n], refs[n:2 * n]
        local_sem, send_sems, recv_sems, fwd_send_sems, fwd_recv_sems = refs[2 * n:]
        x, y, c, chips = _place()
        me = (x, y)
        local = [pltpu.make_async_copy(ins[k], outs[k].at[_chip_rows(me, rows[k]), :], local_sem.at[k]) for k in range(n)]
        for cp in local:
            cp.start()

        def ici(j, k, src_chip, to):
            dst = outs[k].at[_chip_rows(src_chip, rows[k]), _cols(c)]
            src = ins[k].at[:, _cols(c)] if src_chip is me else dst
            return pltpu.make_async_remote_copy(src_ref=src, dst_ref=dst, send_sem=send_sems.at[j, k], recv_sem=recv_sems.at[j, k],
                                                device_id=to, device_id_type=MESH)

        def d2d(j, k, src_chip, half):
            blk = outs[k].at[_chip_rows(src_chip, rows[k]), _cols(half)]
            return pltpu.make_async_remote_copy(src_ref=blk, dst_ref=blk, send_sem=fwd_send_sems.at[j, k],
                                                recv_sem=fwd_recv_sems.at[j, k], device_id=(x, y, 1 - c), device_id_type=MESH)

        sends = [ici(j, k, me, (*chip, c)) for j, chip in enumerate(chips) for k in range(n)]
        for cp in sends:
            cp.start()
        forwards = []
        for j, chip in enumerate(chips):
            for k in range(n):
                ici(j, k, chip, (x, y, c)).wait_recv()
                fw = d2d(j, k, chip, c)
                fw.start()
                forwards.append(fw)
        for j, chip in enumerate(chips):
            for k in range(n):
                d2d(j, k, chip, 1 - c).wait_recv()
        for cp in sends + forwards:
            cp.wait_send()
        for cp in local:
            cp.wait()

    any_spec = pl.BlockSpec(memory_space=pl.ANY)
    return pl.pallas_call(
        body, name="gather_weights", in_specs=[any_spec] * n, out_specs=[any_spec] * n,
        out_shape=[jax.ShapeDtypeStruct((N_CHIPS * r, sh.shape[1]), sh.dtype) for r, sh in zip(rows, shards)],
        scratch_shapes=[pltpu.SemaphoreType.DMA((n,)), pltpu.SemaphoreType.DMA((3, n)), pltpu.SemaphoreType.DMA((3, n)),
                        pltpu.SemaphoreType.DMA((3, n)), pltpu.SemaphoreType.DMA((3, n))])(*shards)


def _sibling_halves(grads):
    n = len(grads)

    def body(*refs):
        ins, outs = refs[:n], refs[n:2 * n]
        send_sems, recv_sems = refs[2 * n:]
        x, y, c, _ = _place()
        copies = [pltpu.make_async_remote_copy(src_ref=ins[k].at[:, pl.ds(1 - c, 1)], dst_ref=outs[k], send_sem=send_sems.at[k],
                                               recv_sem=recv_sems.at[k], device_id=(x, y, 1 - c), device_id_type=MESH)
                  for k in range(n)]
        for cp in copies:
            cp.start()
        for cp in copies:
            cp.wait()

    any_spec = pl.BlockSpec(memory_space=pl.ANY)
    return pl.pallas_call(
        body, name="reduce_sibling", in_specs=[any_spec] * n, out_specs=[any_spec] * n,
        out_shape=[jax.ShapeDtypeStruct((N_CHIPS, 1) + g.shape[2:], F32) for g in grads],
        scratch_shapes=[pltpu.SemaphoreType.DMA((n,)), pltpu.SemaphoreType.DMA((n,))])(*grads)


def _chip_sum(grad, from_sibling, place, name):
    rh, width = grad.shape[2:]

    def body(place_ref, g_ref, s_ref, own_ref, all_ref):
        total = g_ref[0, 0] + s_ref[0, 0]
        all_ref[0, 0] = total.astype(BF16)

        @pl.when(pl.program_id(0) == place_ref[0])
        def _():
            own_ref[0] = total

    blk = (1, 1, rh, width)
    return pl.pallas_call(
        body, name=name,
        grid_spec=pltpu.PrefetchScalarGridSpec(
            num_scalar_prefetch=1, grid=(N_CHIPS,),
            in_specs=[pl.BlockSpec(blk, lambda p, place: (p, place[1], 0, 0)), pl.BlockSpec(blk, lambda p, place: (p, 0, 0, 0))],
            out_specs=[pl.BlockSpec((1, rh, width), lambda p, place: (0, 0, 0)), pl.BlockSpec(blk, lambda p, place: (p, 0, 0, 0))]),
        out_shape=[jax.ShapeDtypeStruct((1, rh, width), F32), jax.ShapeDtypeStruct((N_CHIPS, 1, rh, width), BF16)],
        compiler_params=_params(dimension_semantics=("arbitrary",)))(place, grad, from_sibling)


def _scatter_chip_sums(sums):
    n = len(sums)

    def body(*refs):
        ins, outs = refs[:n], refs[n:2 * n]
        send_sems, recv_sems = refs[2 * n:]
        x, y, c, chips = _place()
        copies = [pltpu.make_async_remote_copy(src_ref=ins[k].at[pl.ds(2 * chip[0] + chip[1], 1)], dst_ref=outs[k].at[pl.ds(j, 1)],
                                               send_sem=send_sems.at[j, k], recv_sem=recv_sems.at[j, k],
                                               device_id=(*chip, c), device_id_type=MESH)
                  for j, chip in enumerate(chips) for k in range(n)]
        for cp in copies:
            cp.start()
        for cp in copies:
            cp.wait()

    any_spec = pl.BlockSpec(memory_space=pl.ANY)
    return pl.pallas_call(
        body, name="reduce_chips", in_specs=[any_spec] * n, out_specs=[any_spec] * n,
        out_shape=[jax.ShapeDtypeStruct((3,) + sm.shape[1:], BF16) for sm in sums],
        scratch_shapes=[pltpu.SemaphoreType.DMA((3, n)), pltpu.SemaphoreType.DMA((3, n))])(*sums)


def _total_sum(own, received, name):
    def body(o_ref, r_ref, t_ref):
        total = o_ref[0]
        for j in range(3):
            total = total + r_ref[j, 0].astype(F32)
        t_ref[0] = total

    return pl.pallas_call(body, name=name, out_shape=jax.ShapeDtypeStruct(own.shape, F32), compiler_params=_params())(own, received)


def _join_halves(halves):
    n = len(halves)

    def body(*refs):
        ins, outs = refs[:n], refs[n:2 * n]
        local_sems, send_sems, recv_sems = refs[2 * n:]
        x, y, c, _ = _place()
        local = [pltpu.make_async_copy(ins[k], outs[k].at[pl.ds(c, 1)], local_sems.at[k]) for k in range(n)]
        copies = [pltpu.make_async_remote_copy(src_ref=ins[k], dst_ref=outs[k].at[pl.ds(c, 1)], send_sem=send_sems.at[k],
                                               recv_sem=recv_sems.at[k], device_id=(x, y, 1 - c), device_id_type=MESH)
                  for k in range(n)]
        for cp in local + copies:
            cp.start()
        for k in range(n):
            pltpu.make_async_remote_copy(src_ref=ins[k], dst_ref=outs[k].at[pl.ds(1 - c, 1)], send_sem=send_sems.at[k],
                                         recv_sem=recv_sems.at[k], device_id=(x, y, 1 - c), device_id_type=MESH).wait()
        for cp in local:
            cp.wait()

    any_spec = pl.BlockSpec(memory_space=pl.ANY)
    return pl.pallas_call(
        body, name="join_halves", in_specs=[any_spec] * n, out_specs=[any_spec] * n,
        out_shape=[jax.ShapeDtypeStruct((2,) + h.shape[1:], F32) for h in halves],
        scratch_shapes=[pltpu.SemaphoreType.DMA((n,)), pltpu.SemaphoreType.DMA((n,)), pltpu.SemaphoreType.DMA((n,))])(*halves)


N_DEV = 8


def _gather_small(block):
    m_per, width = block.shape

    def body(x_ref, out_ref, send_sems, recv_sems, local_sem):
        x, y, c, chips = _place()
        me, sibling = (x, y, c), (x, y, 1 - c)

        def rows(px, py, pc):
            return out_ref.at[pl.ds((4 * px + 2 * py + pc) * m_per, m_per), :]

        def copy(k, blk, to, src=None):
            return pltpu.make_async_remote_copy(src_ref=rows(*blk) if src is None else src, dst_ref=rows(*blk),
                                                send_sem=send_sems.at[k], recv_sem=recv_sems.at[k], device_id=to, device_id_type=MESH)

        mine = pltpu.make_async_copy(x_ref, rows(*me), local_sem)
        mine.start()
        first = [copy(0, me, sibling, src=x_ref)] + [copy(1 + j, me, (*chip, c), src=x_ref) for j, chip in enumerate(chips)]
        for cp in first:
            cp.start()
        passed = [copy(4 + j, (*chip, c), sibling) for j, chip in enumerate(chips)]
        for j, chip in enumerate(chips):
            copy(1 + j, (*chip, c), me).wait_recv()
            passed[j].start()
        copy(0, sibling, me).wait_recv()
        for j, chip in enumerate(chips):
            copy(4 + j, (*chip, 1 - c), me).wait_recv()
        for cp in first + passed:
            cp.wait_send()
        mine.wait()

    vmem = pl.BlockSpec(memory_space=pltpu.VMEM)
    return pl.pallas_call(body, name="gather_small", out_shape=jax.ShapeDtypeStruct((N_DEV * m_per, width), F32),
                          in_specs=[vmem], out_specs=vmem,
                          scratch_shapes=[pltpu.SemaphoreType.DMA((7,)), pltpu.SemaphoreType.DMA((7,)),
                                          pltpu.SemaphoreType.DMA])(block)


def _adamw_math(w, g, m, v):
    m = ADAM_B1 * m + (1.0 - ADAM_B1) * g
    v = ADAM_B2 * v + (1.0 - ADAM_B2) * (g * g)
    m_hat = m / (1.0 - ADAM_B1 ** ADAM_STEP)
    v_hat = v / (1.0 - ADAM_B2 ** ADAM_STEP)
    delta = -ADAM_LR * (m_hat / (jnp.sqrt(v_hat) + ADAM_EPS) + ADAM_WD * w)
    return delta, m, v


def _adamw(w, grad, m, v, transposed, name):
    r, c = w.shape
    tm = 256 if transposed else _tile(r, 512)

    def body(w_ref, g_ref, m_ref, v_ref, go_ref, d_ref, mo_ref, vo_ref):
        g = g_ref[...].T if transposed else g_ref[...]
        go_ref[...] = g
        d_ref[...], mo_ref[...], vo_ref[...] = _adamw_math(w_ref[...], g, m_ref[...], v_ref[...])

    blk = pl.BlockSpec((tm, c), lambda i: (i, 0))
    g_spec = pl.BlockSpec((c, tm), lambda i: (0, i)) if transposed else blk
    return pl.pallas_call(body, name=name, grid=(r // tm,), in_specs=[blk, g_spec, blk, blk], out_specs=[blk] * 4,
                          out_shape=[jax.ShapeDtypeStruct((r, c), F32)] * 4,
                          compiler_params=_params(dimension_semantics=("arbitrary",)))(w, grad, m, v)


def _adamw_small(gathered, w, m, v, name):
    def body(ga_ref, w_ref, m_ref, v_ref, go_ref, d_ref, mo_ref, vo_ref):
        g = ga_ref[0]
        for dev in range(1, N_DEV):
            g = g + ga_ref[dev]
        go_ref[...] = g
        d_ref[...], mo_ref[...], vo_ref[...] = _adamw_math(w_ref[...], g, m_ref[...], v_ref[...])

    return pl.pallas_call(body, name=name, out_shape=[jax.ShapeDtypeStruct(w.shape, F32)] * 4,
                          compiler_params=_params())(gathered, w, m, v)


SMALL = ("g_ffn1_pre", "g_ffn1_post", "g_mix_pre", "w_pool_lin", "pool_scale", "g_mix_post", "g_ffn2_pre", "g_ffn2_post")
WEIGHTS = ("g_ffn1_pre", "w1_gate", "w1_up", "w1_down", "g_ffn1_post", "g_mix_pre", "w_in", "w_pool_lin", "pool_scale", "w_out",
           "g_mix_post", "g_ffn2_pre", "w2_gate", "w2_up", "w2_down", "g_ffn2_post")
LANES = 128


def _pack_small(tree):
    flat = jnp.concatenate([tree[k].reshape(-1) for k in SMALL])
    rows = -(-flat.shape[0] // (8 * LANES)) * 8
    return jnp.pad(flat, (0, rows * LANES - flat.shape[0])).reshape(rows, LANES)


def _unpack_small(packed, like):
    flat, out, at = packed.reshape(-1), {}, 0
    for k in SMALL:
        size = math.prod(like[k].shape)
        out[k] = flat[at:at + size].reshape(like[k].shape)
        at += size
    return out


def kernel(x, g_ffn1_pre, w1_gate, w1_up, w1_down, g_ffn1_post, g_mix_pre, w_in, w_pool_lin, pool_scale, w_out, g_mix_post, g_ffn2_pre, w2_gate, w2_up, w2_down, g_ffn2_post, loss_target, m_g_ffn1_pre, m_w1_gate, m_w1_up, m_w1_down, m_g_ffn1_post, m_g_mix_pre, m_w_in, m_w_pool_lin, m_pool_scale, m_w_out, m_g_mix_post, m_g_ffn2_pre, m_w2_gate, m_w2_up, m_w2_down, m_g_ffn2_post, v_g_ffn1_pre, v_w1_gate, v_w1_up, v_w1_down, v_g_ffn1_post, v_g_mix_pre, v_w_in, v_w_pool_lin, v_pool_scale, v_w_out, v_g_mix_post, v_g_ffn2_pre, v_w2_gate, v_w2_up, v_w2_down, v_g_ffn2_post):
    given = dict(locals())
    w = {k: given[k] for k in WEIGHTS}
    m = {k: given["m_" + k] for k in WEIGHTS}
    v = {k: given["v_" + k] for k in WEIGHTS}
    small = {k: (w[k][0] if k == "w_pool_lin" else w[k].reshape(1, -1)) for k in SMALL}

    shards = [_cast_shard(w[k][0], k in TRANSPOSED, f"cast_{k}") for k in SEGMENTS]
    full = dict(zip(SEGMENTS, _gather_weights(shards)))

    loss_part, grad_x, grads, small_grads = _local_step(x[0], loss_target[0], small, full)
    loss = lax.psum(loss_part, ("x", "y", "c"))

    place = jnp.stack([2 * lax.axis_index("x") + lax.axis_index("y"), lax.axis_index("c")]).astype(jnp.int32)
    local = [grads[k].reshape(N_CHIPS, 2, grads[k].shape[0] // (2 * N_CHIPS), grads[k].shape[1]) for k in SEGMENTS]
    from_sibling = _sibling_halves(local)
    sums = [_chip_sum(g, fs, place, f"chip_sum_{k}") for g, fs, k in zip(local, from_sibling, SEGMENTS)]
    received = _scatter_chip_sums([sm[1] for sm in sums])
    halves = [_total_sum(own, rec, f"total_{k}") for (own, _), rec, k in zip(sums, received, SEGMENTS)]
    summed = {k: j.reshape(-1, j.shape[-1]) for k, j in zip(SEGMENTS, _join_halves(halves))}

    out_grad, out_delta, out_m, out_v = {}, {}, {}, {}
    for k in SEGMENTS:
        out_grad[k], out_delta[k], out_m[k], out_v[k] = (
            a[None] for a in _adamw(w[k][0], summed[k], m[k][0], v[k][0], k in TRANSPOSED, f"adamw_{k}"))

    small_grads["w_pool_lin"] = small_grads["w_pool_lin"][None]
    packed = _pack_small(small_grads)
    gathered = _gather_small(packed).reshape(N_DEV, *packed.shape)
    like = {k: w[k] for k in SMALL}
    results = _adamw_small(gathered, _pack_small(like), _pack_small({k: m[k] for k in SMALL}),
                           _pack_small({k: v[k] for k in SMALL}), "adamw_small")
    for tree, res in zip((out_grad, out_delta, out_m, out_v), results):
        tree.update(_unpack_small(res, like))

    return (loss, grad_x[None], *[out_grad[k] for k in WEIGHTS], *[out_delta[k] for k in WEIGHTS],
            *[out_m[k] for k in WEIGHTS], *[out_v[k] for k in WEIGHTS])
```

```python
import functools
import math

import numpy as np
import jax
import jax.numpy as jnp
from jax import lax
from jax.experimental import pallas as pl
from jax.experimental.pallas import tpu as pltpu

F32 = jnp.float32
BF16 = jnp.bfloat16
MESH = pl.DeviceIdType.MESH

RMS_EPS = 1e-6
HEAD_DIM = 64
POOL_HALF_WINDOWS = (1, 2, 4, 8)
POOL_DIM = 256
GROUP_DIM = 256
DILATIONS = (1, 4, 16)
N_SIDE = 64
N_ATTN_HEADS = 12
ADAM_LR, ADAM_B1, ADAM_B2, ADAM_EPS, ADAM_WD, ADAM_STEP = 0.001, 0.9, 0.999, 1e-08, 0.01, 10

N_CHIPS = 4
V7X_VMEM_LIMIT = 60 * 1024 * 1024

_NT = (((1,), (1,)), ((), ()))
_TN = (((0,), (0,)), ((), ()))


def _dot(a, b):
    return jnp.dot(a, b, preferred_element_type=F32)


def _dot_nt(a, b):
    return lax.dot_general(a, b, _NT, preferred_element_type=F32)


def _dot_tn(a, b):
    return lax.dot_general(a, b, _TN, preferred_element_type=F32)


def _params(**kw):
    return pltpu.CompilerParams(vmem_limit_bytes=V7X_VMEM_LIMIT, **kw)


def _rows(tm, width):
    return pl.BlockSpec((tm, width), lambda i: (i, 0))


def _resident(shape):
    return pl.BlockSpec(shape, lambda i: (0,) * len(shape), pipeline_mode=pl.Buffered(1))


def _const(shape):
    return pl.BlockSpec(shape, lambda i: (0,) * len(shape))


def _tile(rows, cap):
    return max(t for t in range(16, cap + 1, 16) if rows % t == 0)


def _inv_rms(x):
    return lax.rsqrt(jnp.mean(x * x, axis=-1, keepdims=True) + RMS_EPS)


def _rms_bwd(x, inv, g, dy):
    n = x * inv
    dn = dy * g
    dx = inv * (dn - n * jnp.mean(dn * n, axis=-1, keepdims=True))
    return dx, jnp.sum(dy * n, axis=0, keepdims=True)


def _accumulate(ref, value):
    @pl.when(pl.program_id(0) == 0)
    def _():
        ref[...] = jnp.zeros_like(ref)

    ref[...] += value


def _ffn_fwd(x, g_pre, wg_t, wu_t, wd, g_post, target, name, tm=256):
    s, d = x.shape
    ff = wd.shape[0]
    with_loss = target is not None

    def body(*refs):
        if with_loss:
            x_ref, gpre_ref, wg_ref, wu_ref, wd_ref, gpost_ref, t_ref, xo_ref, a_ref, b_ref, f_ref, loss_ref = refs
        else:
            x_ref, gpre_ref, wg_ref, wu_ref, wd_ref, gpost_ref, xo_ref, a_ref, b_ref, f_ref = refs
        xv = x_ref[...]
        hb = (xv * _inv_rms(xv) * gpre_ref[...]).astype(BF16)
        a = _dot_nt(hb, wg_ref[...])
        b = _dot_nt(hb, wu_ref[...])
        hh = (a * jax.nn.sigmoid(a)) * b
        f = _dot(hh.astype(BF16), wd_ref[...])
        xo = xv + 0.5 * (f * _inv_rms(f) * gpost_ref[...])
        a_ref[...] = a.astype(BF16)
        b_ref[...] = b.astype(BF16)
        f_ref[...] = f
        if with_loss:
            e = xo - t_ref[...]
            xo_ref[...] = e * (1.0 / d)
            _accumulate(loss_ref, 0.5 * jnp.sum(jnp.mean(e * e, axis=-1, keepdims=True)))
        else:
            xo_ref[...] = xo

    in_specs = [_rows(tm, d), _const((1, d)), _resident((ff, d)), _resident((ff, d)), _resident((ff, d)), _const((1, d))]
    args = [x, g_pre, wg_t, wu_t, wd, g_post]
    out_shape = [jax.ShapeDtypeStruct((s, d), F32), jax.ShapeDtypeStruct((s, ff), BF16),
                 jax.ShapeDtypeStruct((s, ff), BF16), jax.ShapeDtypeStruct((s, d), F32)]
    out_specs = [_rows(tm, d), _rows(tm, ff), _rows(tm, ff), _rows(tm, d)]
    if with_loss:
        in_specs.append(_rows(tm, d))
        args.append(target)
        out_shape.append(jax.ShapeDtypeStruct((8, 128), F32))
        out_specs.append(_const((8, 128)))
    return pl.pallas_call(body, name=name, grid=(s // tm,), in_specs=in_specs, out_specs=out_specs,
                          out_shape=out_shape, compiler_params=_params(dimension_semantics=("arbitrary",)))(*args)


def _ffn_bwd(dxo, x, f, a, b, g_pre, g_post, wg_t, wu_t, wd, name, tm=256):
    s, d = x.shape
    ff = wd.shape[0]

    def body(dxo_ref, x_ref, f_ref, a_ref, b_ref, gpre_ref, gpost_ref, wg_ref, wu_ref, wd_ref,
             dx_ref, hh_ref, da_ref, db_ref, df_ref, h_ref, dgpre_ref, dgpost_ref):
        dxo_v = dxo_ref[...]
        fv = f_ref[...]
        df, dgpost = _rms_bwd(fv, _inv_rms(fv), gpost_ref[...], 0.5 * dxo_v)
        dfb = df.astype(BF16)
        dhh = _dot_nt(dfb, wd_ref[...])
        av = a_ref[...].astype(F32)
        bv = b_ref[...].astype(F32)
        sig = jax.nn.sigmoid(av)
        sa = av * sig
        da = (dhh * bv * (sig * (1.0 + av * (1.0 - sig)))).astype(BF16)
        db = (dhh * sa).astype(BF16)
        dh = _dot(da, wg_ref[...]) + _dot(db, wu_ref[...])
        xv = x_ref[...]
        inv = _inv_rms(xv)
        dxn, dgpre = _rms_bwd(xv, inv, gpre_ref[...], dh)
        dx_ref[...] = dxo_v + dxn
        hh_ref[...] = (sa * bv).astype(BF16)
        da_ref[...] = da
        db_ref[...] = db
        df_ref[...] = dfb
        h_ref[...] = (xv * inv * gpre_ref[...]).astype(BF16)
        _accumulate(dgpre_ref, dgpre)
        _accumulate(dgpost_ref, dgpost)

    return pl.pallas_call(
        body, name=name, grid=(s // tm,),
        in_specs=[_rows(tm, d), _rows(tm, d), _rows(tm, d), _rows(tm, ff), _rows(tm, ff), _const((1, d)), _const((1, d)),
                  _resident((ff, d)), _resident((ff, d)), _resident((ff, d))],
        out_specs=[_rows(tm, d), _rows(tm, ff), _rows(tm, ff), _rows(tm, ff), _rows(tm, d), _rows(tm, d),
                   _const((1, d)), _const((1, d))],
        out_shape=[jax.ShapeDtypeStruct((s, d), F32), jax.ShapeDtypeStruct((s, ff), BF16), jax.ShapeDtypeStruct((s, ff), BF16),
                   jax.ShapeDtypeStruct((s, ff), BF16), jax.ShapeDtypeStruct((s, d), BF16), jax.ShapeDtypeStruct((s, d), BF16),
                   jax.ShapeDtypeStruct((1, d), F32), jax.ShapeDtypeStruct((1, d), F32)],
        compiler_params=_params(dimension_semantics=("arbitrary",)))(dxo, x, f, a, b, g_pre, g_post, wg_t, wu_t, wd)


def _wgrad(lhs, rhs, name, tk=512):
    s, r = lhs.shape
    c = rhs.shape[1]

    def body(l_ref, r_ref, o_ref):
        _accumulate(o_ref, _dot_tn(l_ref[...], r_ref[...]))

    return pl.pallas_call(body, name=name, grid=(s // tk,), in_specs=[_rows(tk, r), _rows(tk, c)],
                          out_specs=_const((r, c)), out_shape=jax.ShapeDtypeStruct((r, c), F32),
                          compiler_params=_params(dimension_semantics=("arbitrary",)))(lhs, rhs)


def _in_fwd(x, g, w_in_t, name, tm=512):
    s, d = x.shape
    d_in = w_in_t.shape[0]
    n_parts = (d_in - POOL_DIM) // GROUP_DIM

    def body(x_ref, g_ref, w_ref, u_ref, *part_refs):
        xv = x_ref[...]
        hb = (xv * _inv_rms(xv) * g_ref[...]).astype(BF16)
        z = _dot_nt(hb, w_ref[...])
        u_ref[...] = z[:, :POOL_DIM]
        for j, ref in enumerate(part_refs):
            ref[...] = z[:, POOL_DIM + GROUP_DIM * j:POOL_DIM + GROUP_DIM * (j + 1)]

    return pl.pallas_call(
        body, name=name, grid=(s // tm,), in_specs=[_rows(tm, d), _const((1, d)), _resident((d_in, d))],
        out_specs=[_rows(tm, POOL_DIM)] + [_rows(tm, GROUP_DIM)] * n_parts,
        out_shape=[jax.ShapeDtypeStruct((s, POOL_DIM), F32)] + [jax.ShapeDtypeStruct((s, GROUP_DIM), F32)] * n_parts,
        compiler_params=_params(dimension_semantics=("arbitrary",)))(x, g, w_in_t)


def _in_bwd(du, dparts, x, dxo, g, w_in_t, name, tm=512):
    s, d = x.shape
    d_in = w_in_t.shape[0]
    n_parts = len(dparts)

    def body(du_ref, *refs):
        part_refs = refs[:n_parts]
        x_ref, dxo_ref, g_ref, w_ref, dx_ref, dz_ref, h_ref, dg_ref = refs[n_parts:]
        dz = jnp.concatenate([r[...].astype(BF16) for r in (du_ref,) + part_refs], axis=1)
        dz_ref[...] = dz
        dh = _dot(dz, w_ref[...])
        xv = x_ref[...]
        inv = _inv_rms(xv)
        dxn, dg = _rms_bwd(xv, inv, g_ref[...], dh)
        dx_ref[...] = dxo_ref[...] + dxn
        h_ref[...] = (xv * inv * g_ref[...]).astype(BF16)
        _accumulate(dg_ref, dg)

    return pl.pallas_call(
        body, name=name, grid=(s // tm,),
        in_specs=[_rows(tm, POOL_DIM)] + [_rows(tm, GROUP_DIM)] * n_parts + [_rows(tm, d), _rows(tm, d), _const((1, d)),
                                                                             _resident((d_in, d))],
        out_specs=[_rows(tm, d), _rows(tm, d_in), _rows(tm, d), _const((1, d))],
        out_shape=[jax.ShapeDtypeStruct((s, d), F32), jax.ShapeDtypeStruct((s, d_in), BF16), jax.ShapeDtypeStruct((s, d), BF16),
                   jax.ShapeDtypeStruct((1, d), F32)],
        compiler_params=_params(dimension_semantics=("arbitrary",)))(du, *dparts, x, dxo, g, w_in_t)


_POOL_HALO = 8


def _pool_chain(v, first_shift):
    n = v.shape[0]
    p2 = v + pltpu.roll(v, first_shift, 0)
    p4 = pltpu.roll(p2, 1, 0) + pltpu.roll(p2, n - 1, 0)
    p8 = pltpu.roll(p4, 2, 0) + pltpu.roll(p4, n - 2, 0)
    p16 = pltpu.roll(p8, 4, 0) + pltpu.roll(p8, n - 4, 0)
    group = lax.broadcasted_iota(jnp.int32, v.shape, 1) // HEAD_DIM
    return jnp.where(group == 0, p2, jnp.where(group == 1, p4, jnp.where(group == 2, p8, p16)))


def _pool_count(t0, rows, s):
    t = t0 + lax.broadcasted_iota(jnp.int32, (rows, POOL_DIM), 0)
    group = lax.broadcasted_iota(jnp.int32, (rows, POOL_DIM), 1) // HEAD_DIM
    half = jnp.where(group == 0, 1, jnp.where(group == 1, 2, jnp.where(group == 2, 4, 8)))
    cnt = jnp.minimum(t + half, s) - jnp.maximum(t - half, 0)
    return jnp.maximum(cnt, 1).astype(F32)


def _pad_rows(ref, pad_ref, s):
    zeros = jnp.zeros((_POOL_HALO, pad_ref.shape[1]), pad_ref.dtype)
    pad_ref[pl.ds(0, _POOL_HALO), :] = zeros
    pad_ref[pl.ds(_POOL_HALO + s, _POOL_HALO), :] = zeros
    pad_ref[pl.ds(_POOL_HALO, s), :] = ref[...]


def _pool_fwd(u, w_bd, scale, name, tm=512):
    s = u.shape[0]
    ext = tm + 2 * _POOL_HALO

    def body(u_ref, w_ref, sc_ref, o_ref, upad):
        _pad_rows(u_ref, upad, s)

        def tile(i, carry):
            t0 = pl.multiple_of(i * tm, tm)
            uv = upad[pl.ds(t0, ext), :]
            win = _pool_chain(uv, 1)[_POOL_HALO:_POOL_HALO + tm]
            y = win / _pool_count(t0, tm, s) - uv[_POOL_HALO:_POOL_HALO + tm]
            o_ref[pl.ds(t0, tm), :] = (_dot(y.astype(BF16), w_ref[...]) * sc_ref[...]).astype(BF16)
            return carry

        lax.fori_loop(0, s // tm, tile, 0)

    return pl.pallas_call(body, name=name, out_shape=jax.ShapeDtypeStruct((s, POOL_DIM), BF16),
                          scratch_shapes=[pltpu.VMEM((s + 2 * _POOL_HALO, POOL_DIM), F32)],
                          compiler_params=_params())(u, w_bd, scale)


def _pool_bwd(u, da, w_bd, scale, name, tm=512):
    s = u.shape[0]
    ext = tm + 2 * _POOL_HALO

    def body(u_ref, da_ref, w_ref, sc_ref, du_ref, dw_ref, dsc_ref, upad, dapad):
        _pad_rows(u_ref, upad, s)
        _pad_rows(da_ref, dapad, s)
        dw_ref[...] = jnp.zeros_like(dw_ref)
        dsc_ref[...] = jnp.zeros_like(dsc_ref)

        def tile(i, carry):
            t0 = pl.multiple_of(i * tm, tm)
            uv = upad[pl.ds(t0, ext), :]
            dav = dapad[pl.ds(t0, ext), :]
            win = _pool_chain(uv, 1)[_POOL_HALO:_POOL_HALO + tm]
            yb = (win / _pool_count(t0, tm, s) - uv[_POOL_HALO:_POOL_HALO + tm]).astype(BF16)
            yl = _dot(yb, w_ref[...])
            da_c = dav[_POOL_HALO:_POOL_HALO + tm]
            dsc_ref[...] += jnp.sum(da_c * yl, axis=0, keepdims=True)
            dyl = (dav * sc_ref[...]).astype(BF16)
            dw_ref[...] += _dot_tn(yb, dyl[_POOL_HALO:_POOL_HALO + tm])
            dy = _dot_nt(dyl, w_ref[...])
            dyc = dy / _pool_count(t0 - _POOL_HALO, ext, s)
            du_ref[pl.ds(t0, tm), :] = (_pool_chain(dyc, ext - 1) - dy)[_POOL_HALO:_POOL_HALO + tm]
            return carry

        lax.fori_loop(0, s // tm, tile, 0)

    pool_cols = pl.BlockSpec((s, POOL_DIM), lambda i: (0, 0), pipeline_mode=pl.Buffered(1))
    return pl.pallas_call(
        body, name=name, grid=(1,),
        in_specs=[pool_cols, pool_cols, _const((POOL_DIM, POOL_DIM)), _const((1, POOL_DIM))],
        out_specs=[_const((s, POOL_DIM)), _const((POOL_DIM, POOL_DIM)), _const((1, POOL_DIM))],
        out_shape=[jax.ShapeDtypeStruct((s, POOL_DIM), F32), jax.ShapeDtypeStruct((POOL_DIM, POOL_DIM), F32),
                   jax.ShapeDtypeStruct((1, POOL_DIM), F32)],
        scratch_shapes=[pltpu.VMEM((s + 2 * _POOL_HALO, POOL_DIM), F32), pltpu.VMEM((s + 2 * _POOL_HALO, POOL_DIM), F32)],
        compiler_params=_params(dimension_semantics=("arbitrary",)))(u, da, w_bd, scale)


_BQ = 128
_KW = _BQ + 2 * N_SIDE
_PAIR = 2 * HEAD_DIM
_NEG = -1e30


def _attn_block_geometry(i, length):
    q0 = pl.multiple_of(i * _BQ, _BQ)
    ws = pl.multiple_of(jnp.clip(q0 - N_SIDE, 0, length - _KW), N_SIDE)
    rel = (lax.broadcasted_iota(jnp.int32, (_BQ, _KW), 1) - lax.broadcasted_iota(jnp.int32, (_BQ, _KW), 0)) + (ws - q0)
    dist = jnp.abs(rel)
    return q0, ws, dist <= N_SIDE, dist.astype(F32)


def _head_slope(slopes_ref, head):
    return slopes_ref[2 * pl.program_id(0) + head]


def _residue_rows(dilation, start, count):
    if dilation == 1:
        return pl.ds(start, count)
    return pl.ds(start * dilation + pl.program_id(1), count, stride=dilation)


def _attn_call(body, name, dilation, seq, n_in, n_out, scratch):
    col = pl.BlockSpec((seq, _PAIR), lambda c, r: (0, c), pipeline_mode=pl.Buffered(1))
    return pl.pallas_call(
        body, name=name, grid=(GROUP_DIM // _PAIR, dilation),
        in_specs=[pl.BlockSpec(memory_space=pltpu.SMEM)] + [col] * n_in, out_specs=[col] * n_out,
        out_shape=[jax.ShapeDtypeStruct((seq, GROUP_DIM), F32)] * n_out, scratch_shapes=scratch,
        compiler_params=_params(dimension_semantics=("arbitrary", "arbitrary")))


def _attn_fwd(q, k, v, slopes, dilation, name):
    seq = q.shape[0]
    length = seq // dilation

    def body(sl_ref, q_ref, k_ref, v_ref, o_ref, lse_ref, qs, ks, vs):
        lane_head = lax.broadcasted_iota(jnp.int32, (_BQ, _PAIR), 1) // HEAD_DIM
        for src, dst in ((q_ref, qs), (k_ref, ks), (v_ref, vs)):
            dst[...] = src[_residue_rows(dilation, 0, length), :].astype(BF16)

        def block(i, carry):
            q0, ws, valid, dist = _attn_block_geometry(i, length)
            qv = qs[pl.ds(q0, _BQ), :]
            kw = ks[pl.ds(ws, _KW), :]
            vw = vs[pl.ds(ws, _KW), :]
            outs, lses = [], []
            for head in range(2):
                qm = jnp.where(lane_head == head, qv, jnp.zeros_like(qv))
                sc = _dot_nt(qm, kw) * (HEAD_DIM ** -0.5) - (_head_slope(sl_ref, head) * dilation) * dist
                sc = jnp.where(valid, sc, _NEG)
                m = jnp.max(sc, axis=-1, keepdims=True)
                p = jnp.exp(sc - m)
                den = jnp.sum(p, axis=-1, keepdims=True)
                outs.append(_dot(p.astype(BF16), vw) / den)
                lses.append(m + jnp.log(den))
            rows = _residue_rows(dilation, q0, _BQ)
            o_ref[rows, :] = jnp.where(lane_head == 0, outs[0], outs[1])
            lse_ref[rows, :] = jnp.where(lane_head == 0, lses[0], lses[1])
            return carry

        lax.fori_loop(0, length // _BQ, block, 0)

    stage = pltpu.VMEM((length, _PAIR), BF16)
    return _attn_call(body, name, dilation, seq, 3, 2, [stage] * 3)(slopes, q, k, v)


def _attn_bwd(q, k, v, do, lse, cterm, slopes, dilation, name):
    seq = q.shape[0]
    length = seq // dilation

    def body(sl_ref, q_ref, k_ref, v_ref, do_ref, lse_ref, c_ref, dq_ref, dk_ref, dv_ref, qs, ks, vs, dos, dk_acc, dv_acc):
        lane_head = lax.broadcasted_iota(jnp.int32, (_BQ, _PAIR), 1) // HEAD_DIM
        for src, dst in ((q_ref, qs), (k_ref, ks), (v_ref, vs), (do_ref, dos)):
            dst[...] = src[_residue_rows(dilation, 0, length), :].astype(BF16)
        dk_acc[...] = jnp.zeros_like(dk_acc)
        dv_acc[...] = jnp.zeros_like(dv_acc)

        def block(i, carry):
            q0, ws, valid, dist = _attn_block_geometry(i, length)
            rows = _residue_rows(dilation, q0, _BQ)
            qv = qs[pl.ds(q0, _BQ), :]
            dov = dos[pl.ds(q0, _BQ), :]
            lsev = lse_ref[rows, :]
            cv = c_ref[rows, :]
            kw = ks[pl.ds(ws, _KW), :]
            vw = vs[pl.ds(ws, _KW), :]
            dqs = []
            dk_blk = jnp.zeros((_KW, _PAIR), F32)
            dv_blk = jnp.zeros((_KW, _PAIR), F32)
            for head in range(2):
                mine = lane_head == head
                qm = jnp.where(mine, qv, jnp.zeros_like(qv))
                dom = jnp.where(mine, dov, jnp.zeros_like(dov))
                lse_h = jnp.max(jnp.where(mine, lsev, _NEG), axis=-1, keepdims=True)
                c_h = jnp.max(jnp.where(mine, cv, _NEG), axis=-1, keepdims=True)
                sc = _dot_nt(qm, kw) * (HEAD_DIM ** -0.5) - (_head_slope(sl_ref, head) * dilation) * dist
                p = jnp.where(valid, jnp.exp(sc - lse_h), 0.0)
                ds = (p * (_dot_nt(dom, vw) + c_h) * (HEAD_DIM ** -0.5)).astype(BF16)
                dqs.append(_dot(ds, kw))
                dk_blk += _dot_tn(ds, qm)
                dv_blk += _dot_tn(p.astype(BF16), dom)
            dq_ref[rows, :] = jnp.where(lane_head == 0, dqs[0], dqs[1])
            dk_acc[pl.ds(ws, _KW), :] += dk_blk
            dv_acc[pl.ds(ws, _KW), :] += dv_blk
            return carry

        lax.fori_loop(0, length // _BQ, block, 0)
        all_rows = _residue_rows(dilation, 0, length)
        dk_ref[all_rows, :] = dk_acc[...]
        dv_ref[all_rows, :] = dv_acc[...]

    stage = pltpu.VMEM((length, _PAIR), BF16)
    acc = pltpu.VMEM((length, _PAIR), F32)
    return _attn_call(body, name, dilation, seq, 6, 3, [stage] * 4 + [acc] * 2)(slopes, q, k, v, do, lse, cterm)


def _group_weights(lses):
    m = jnp.maximum(jnp.maximum(lses[0], lses[1]), lses[2])
    es = [jnp.exp(l - m) for l in lses]
    den = es[0] + es[1] + es[2]
    return [e / den for e in es]


def _combine_fwd(a_pool, outs, lses, name, tm=512):
    s = a_pool.shape[0]

    def body(ap_ref, o0, o1, o2, l0, l1, l2, cat_ref):
        alphas = _group_weights([l0[...], l1[...], l2[...]])
        parts = [ap_ref[...]] + [(o[...] * al).astype(BF16) for o, al in zip((o0, o1, o2), alphas)]
        cat_ref[...] = jnp.concatenate(parts, axis=1)

    width = POOL_DIM + 3 * GROUP_DIM
    return pl.pallas_call(body, name=name, grid=(s // tm,), in_specs=[_rows(tm, POOL_DIM)] + [_rows(tm, GROUP_DIM)] * 6,
                          out_specs=_rows(tm, width), out_shape=jax.ShapeDtypeStruct((s, width), BF16),
                          compiler_params=_params(dimension_semantics=("arbitrary",)))(a_pool, *outs, *lses)


def _combine_bwd(dcat, outs, lses, head_ones, name, tm=512):
    s = dcat.shape[0]

    def body(dc_ref, o0, o1, o2, l0, l1, l2, ones_ref, do0, do1, do2, c0, c1, c2):
        alphas = _group_weights([l0[...], l1[...], l2[...]])
        dcat_v = dc_ref[...]
        das = [dcat_v[:, POOL_DIM + GROUP_DIM * g:POOL_DIM + GROUP_DIM * (g + 1)] for g in range(3)]
        prod = sum(da * (o[...] * al) for da, o, al in zip(das, (o0, o1, o2), alphas))
        hi = prod.astype(BF16)
        lo = (prod - hi.astype(F32)).astype(BF16)
        total = _dot(hi, ones_ref[...]) + _dot(lo, ones_ref[...])
        for da, al, do_ref, c_ref in zip(das, alphas, (do0, do1, do2), (c0, c1, c2)):
            do_ref[...] = da * al
            c_ref[...] = -al * total

    width = POOL_DIM + 3 * GROUP_DIM
    return pl.pallas_call(
        body, name=name, grid=(s // tm,),
        in_specs=[_rows(tm, width)] + [_rows(tm, GROUP_DIM)] * 6 + [_const((GROUP_DIM, GROUP_DIM))],
        out_specs=[_rows(tm, GROUP_DIM)] * 6,
        out_shape=[jax.ShapeDtypeStruct((s, GROUP_DIM), F32)] * 6,
        compiler_params=_params(dimension_semantics=("arbitrary",)))(dcat, *outs, *lses, head_ones)


def _out_fwd(cat, x, w_out, g, name, tm=512):
    s, d = x.shape

    def body(cat_ref, x_ref, w_ref, g_ref, xo_ref, mix_ref):
        mix = _dot(cat_ref[...], w_ref[...])
        mix_ref[...] = mix
        xo_ref[...] = x_ref[...] + mix * _inv_rms(mix) * g_ref[...]

    return pl.pallas_call(body, name=name, grid=(s // tm,),
                          in_specs=[_rows(tm, cat.shape[1]), _rows(tm, d), _resident(w_out.shape), _const((1, d))],
                          out_specs=[_rows(tm, d), _rows(tm, d)], out_shape=[jax.ShapeDtypeStruct((s, d), F32)] * 2,
                          compiler_params=_params(dimension_semantics=("arbitrary",)))(cat, x, w_out, g)


def _out_bwd(dxo, mix, w_out, g, name, tm=512):
    s, d = mix.shape
    width = w_out.shape[0]

    def body(dxo_ref, mix_ref, w_ref, g_ref, dcat_ref, dmix_ref, dg_ref):
        mv = mix_ref[...]
        dmix, dg = _rms_bwd(mv, _inv_rms(mv), g_ref[...], dxo_ref[...])
        dmb = dmix.astype(BF16)
        dmix_ref[...] = dmb
        dcat_ref[...] = _dot_nt(dmb, w_ref[...])
        _accumulate(dg_ref, dg)

    return pl.pallas_call(
        body, name=name, grid=(s // tm,), in_specs=[_rows(tm, d), _rows(tm, d), _resident(w_out.shape), _const((1, d))],
        out_specs=[_rows(tm, width), _rows(tm, d), _const((1, d))],
        out_shape=[jax.ShapeDtypeStruct((s, width), F32), jax.ShapeDtypeStruct((s, d), BF16), jax.ShapeDtypeStruct((1, d), F32)],
        compiler_params=_params(dimension_semantics=("arbitrary",)))(dxo, mix, w_out, g)


def _alibi_slopes():
    return np.array([2.0 ** (-8.0 * (i + 1) / N_ATTN_HEADS) for i in range(N_ATTN_HEADS)], np.float32)


def _block_diag(w_lin):
    n, c, _ = w_lin.shape
    eye = jnp.eye(n, dtype=w_lin.dtype)
    return (eye[:, None, :, None] * w_lin[:, :, None, :]).reshape(n * c, n * c)


def _local_step(x, target, small, full):
    s, d = x.shape
    slopes = _alibi_slopes()
    group_slopes = [jnp.asarray(slopes[4 * g:4 * g + 4]) for g in range(3)]
    w_bd = _block_diag(small["w_pool_lin"]).astype(BF16)
    head_ones = jnp.asarray(np.kron(np.eye(GROUP_DIM // HEAD_DIM), np.ones((HEAD_DIM, HEAD_DIM))), BF16)

    x1, a1, b1, f1 = _ffn_fwd(x, small["g_ffn1_pre"], full["w1_gate"], full["w1_up"], full["w1_down"], small["g_ffn1_post"],
                              None, "ffn1_fwd")
    u, *parts = _in_fwd(x1, small["g_mix_pre"], full["w_in"], "in_fwd")
    qs, ks, vs = parts[0:3], parts[3:6], parts[6:9]
    a_pool = _pool_fwd(u, w_bd, small["pool_scale"], "pool_fwd")
    outs, lses = [], []
    for g, dil in enumerate(DILATIONS):
        o, lse = _attn_fwd(qs[g], ks[g], vs[g], group_slopes[g], dil, f"attn_fwd{g}")
        outs.append(o)
        lses.append(lse)
    cat = _combine_fwd(a_pool, outs, lses, "combine_fwd")
    x2, mix = _out_fwd(cat, x1, full["w_out"], small["g_mix_post"], "out_fwd")
    dx3, a2, b2, f2, loss_part = _ffn_fwd(x2, small["g_ffn2_pre"], full["w2_gate"], full["w2_up"], full["w2_down"],
                                          small["g_ffn2_post"], target, "ffn2_fwd")

    grads, small_grads = {}, {}

    def ffn_backward(tag, dxo, x_in, f, a, b):
        dx, hh, da, db, df, h, dg_pre, dg_post = _ffn_bwd(
            dxo, x_in, f, a, b, small[f"g_{tag}_pre"], small[f"g_{tag}_post"],
            full[f"w{tag[-1]}_gate"], full[f"w{tag[-1]}_up"], full[f"w{tag[-1]}_down"], f"{tag}_bwd")
        grads[f"w{tag[-1]}_down"] = _wgrad(hh, df, f"{tag}_wgrad_down")
        grads[f"w{tag[-1]}_gate"] = _wgrad(da, h, f"{tag}_wgrad_gate")
        grads[f"w{tag[-1]}_up"] = _wgrad(db, h, f"{tag}_wgrad_up")
        small_grads[f"g_{tag}_pre"], small_grads[f"g_{tag}_post"] = dg_pre, dg_post
        return dx

    dx2 = ffn_backward("ffn2", dx3, x2, f2, a2, b2)
    dcat, dmix, small_grads["g_mix_post"] = _out_bwd(dx2, mix, full["w_out"], small["g_mix_post"], "out_bwd")
    grads["w_out"] = _wgrad(cat, dmix, "wgrad_out")
    dos_cs = _combine_bwd(dcat, outs, lses, head_ones, "combine_bwd")
    dos, cs = dos_cs[:3], dos_cs[3:]
    dqs, dks, dvs = [], [], []
    for g, dil in enumerate(DILATIONS):
        dq, dk, dv = _attn_bwd(qs[g], ks[g], vs[g], dos[g], lses[g], cs[g], group_slopes[g], dil, f"attn_bwd{g}")
        dqs.append(dq)
        dks.append(dk)
        dvs.append(dv)
    du, dw_bd, small_grads["pool_scale"] = _pool_bwd(u, dcat, w_bd, small["pool_scale"], "pool_bwd")
    n_pool = len(POOL_HALF_WINDOWS)
    small_grads["w_pool_lin"] = jnp.stack(
        [dw_bd[HEAD_DIM * g:HEAD_DIM * (g + 1), HEAD_DIM * g:HEAD_DIM * (g + 1)] for g in range(n_pool)])
    dx1, dz, h2, small_grads["g_mix_pre"] = _in_bwd(du, dqs + dks + dvs, x1, dx2, small["g_mix_pre"], full["w_in"], "in_bwd")
    grads["w_in"] = _wgrad(dz, h2, "wgrad_in")
    dx0 = ffn_backward("ffn1", dx1, x, f1, a1, b1)
    return loss_part[0, 0], dx0, grads, small_grads


SEGMENTS = ("w1_gate", "w1_up", "w1_down", "w_in", "w_out", "w2_gate", "w2_up", "w2_down")
TRANSPOSED = ("w1_gate", "w1_up", "w_in", "w2_gate", "w2_up")
HALF = 512


def _place():
    x, y, c = lax.axis_index("x"), lax.axis_index("y"), lax.axis_index("c")
    other_chips = [(1 - x, y), (x, 1 - y), (1 - x, 1 - y)]
    return x, y, c, other_chips


def _chip_rows(chip, rows):
    return pl.ds(pl.multiple_of((2 * chip[0] + chip[1]) * rows, 16), rows)


def _cols(c):
    return pl.ds(pl.multiple_of(c * HALF, HALF), HALF)


def _cast_shard(w, transpose, place, name, tm=256):
    r, c = w.shape
    if transpose:
        def body(place_ref, w_ref, o_ref):
            o_ref[...] = w_ref[...].T.astype(BF16)

        grid, in_block, out_block, rows = (r // tm,), (tm, c), (c, tm), c
    else:
        def body(place_ref, w_ref, o_ref):
            o_ref[...] = w_ref[...].astype(BF16)

        grid, in_block, out_block, rows = (1,), (r, c), (r, c), r
    return pl.pallas_call(
        body, name=name,
        grid_spec=pltpu.PrefetchScalarGridSpec(
            num_scalar_prefetch=1, grid=grid, in_specs=[pl.BlockSpec(in_block, lambda i, place: (i, 0))],
            out_specs=pl.BlockSpec(out_block, lambda i, place: (place[0], i))),
        out_shape=jax.ShapeDtypeStruct((N_CHIPS * rows, 1024), BF16),
        compiler_params=_params(dimension_semantics=("arbitrary",)))(place, w)


def _gather_weights(bufs):
    n = len(bufs)
    rows = [b.shape[0] // N_CHIPS for b in bufs]

    def body(*refs):
        outs = refs[n:2 * n]
        send_sems, recv_sems, fwd_send_sems, fwd_recv_sems = refs[2 * n:]
        x, y, c, chips = _place()
        me = (x, y)

        def ici(j, k, src_chip, to):
            blk = outs[k].at[_chip_rows(src_chip, rows[k]), _cols(c)]
            return pltpu.make_async_remote_copy(src_ref=blk, dst_ref=blk, send_sem=send_sems.at[j, k], recv_sem=recv_sems.at[j, k],
                                                device_id=to, device_id_type=MESH)

        def d2d(j, k, src_chip, half):
            blk = outs[k].at[_chip_rows(src_chip, rows[k]), _cols(half)]
            return pltpu.make_async_remote_copy(src_ref=blk, dst_ref=blk, send_sem=fwd_send_sems.at[j, k],
                                                recv_sem=fwd_recv_sems.at[j, k], device_id=(x, y, 1 - c), device_id_type=MESH)

        sends = [ici(j, k, me, (*chip, c)) for j, chip in enumerate(chips) for k in range(n)]
        for cp in sends:
            cp.start()
        forwards = []
        for j, chip in enumerate(chips):
            for k in range(n):
                ici(j, k, chip, (x, y, c)).wait_recv()
                fw = d2d(j, k, chip, c)
                fw.start()
                forwards.append(fw)
        for j, chip in enumerate(chips):
            for k in range(n):
                d2d(j, k, chip, 1 - c).wait_recv()
        for cp in sends + forwards:
            cp.wait_send()

    any_spec = pl.BlockSpec(memory_space=pl.ANY)
    return pl.pallas_call(
        body, name="gather_weights", in_specs=[any_spec] * n, out_specs=[any_spec] * n,
        out_shape=[jax.ShapeDtypeStruct(b.shape, b.dtype) for b in bufs], input_output_aliases={k: k for k in range(n)},
        scratch_shapes=[pltpu.SemaphoreType.DMA((3, n)), pltpu.SemaphoreType.DMA((3, n)),
                        pltpu.SemaphoreType.DMA((3, n)), pltpu.SemaphoreType.DMA((3, n))])(*bufs)


def _sibling_halves(grads):
    n = len(grads)

    def body(*refs):
        ins, outs = refs[:n], refs[n:2 * n]
        send_sems, recv_sems = refs[2 * n:]
        x, y, c, _ = _place()
        copies = [pltpu.make_async_remote_copy(src_ref=ins[k].at[:, pl.ds(1 - c, 1)], dst_ref=outs[k], send_sem=send_sems.at[k],
                                               recv_sem=recv_sems.at[k], device_id=(x, y, 1 - c), device_id_type=MESH)
                  for k in range(n)]
        for cp in copies:
            cp.start()
        for cp in copies:
            cp.wait()

    any_spec = pl.BlockSpec(memory_space=pl.ANY)
    return pl.pallas_call(
        body, name="reduce_sibling", in_specs=[any_spec] * n, out_specs=[any_spec] * n,
        out_shape=[jax.ShapeDtypeStruct((N_CHIPS, 1) + g.shape[2:], F32) for g in grads],
        scratch_shapes=[pltpu.SemaphoreType.DMA((n,)), pltpu.SemaphoreType.DMA((n,))])(*grads)


def _chip_sum(grad, from_sibling, place, name):
    rh, width = grad.shape[2:]

    def body(place_ref, g_ref, s_ref, own_ref, all_ref):
        total = g_ref[0, 0] + s_ref[0, 0]
        all_ref[0, 0] = total.astype(BF16)

        @pl.when(pl.program_id(0) == place_ref[0])
        def _():
            own_ref[0] = total

    blk = (1, 1, rh, width)
    return pl.pallas_call(
        body, name=name,
        grid_spec=pltpu.PrefetchScalarGridSpec(
            num_scalar_prefetch=1, grid=(N_CHIPS,),
            in_specs=[pl.BlockSpec(blk, lambda p, place: (p, place[1], 0, 0)), pl.BlockSpec(blk, lambda p, place: (p, 0, 0, 0))],
            out_specs=[pl.BlockSpec((1, rh, width), lambda p, place: (0, 0, 0)), pl.BlockSpec(blk, lambda p, place: (p, 0, 0, 0))]),
        out_shape=[jax.ShapeDtypeStruct((1, rh, width), F32), jax.ShapeDtypeStruct((N_CHIPS, 1, rh, width), BF16)],
        compiler_params=_params(dimension_semantics=("arbitrary",)))(place, grad, from_sibling)


def _scatter_chip_sums(sums):
    n = len(sums)

    def body(*refs):
        ins, outs = refs[:n], refs[n:2 * n]
        send_sems, recv_sems = refs[2 * n:]
        x, y, c, chips = _place()
        copies = [pltpu.make_async_remote_copy(src_ref=ins[k].at[pl.ds(2 * chip[0] + chip[1], 1)], dst_ref=outs[k].at[pl.ds(j, 1)],
                                               send_sem=send_sems.at[j, k], recv_sem=recv_sems.at[j, k],
                                               device_id=(*chip, c), device_id_type=MESH)
                  for j, chip in enumerate(chips) for k in range(n)]
        for cp in copies:
            cp.start()
        for cp in copies:
            cp.wait()

    any_spec = pl.BlockSpec(memory_space=pl.ANY)
    return pl.pallas_call(
        body, name="reduce_chips", in_specs=[any_spec] * n, out_specs=[any_spec] * n,
        out_shape=[jax.ShapeDtypeStruct((3,) + sm.shape[1:], BF16) for sm in sums],
        scratch_shapes=[pltpu.SemaphoreType.DMA((3, n)), pltpu.SemaphoreType.DMA((3, n))])(*sums)


def _total_sum(own, received, name):
    def body(o_ref, r_ref, t_ref):
        total = o_ref[0]
        for j in range(3):
            total = total + r_ref[j, 0].astype(F32)
        t_ref[0] = total

    return pl.pallas_call(body, name=name, out_shape=jax.ShapeDtypeStruct(own.shape, F32), compiler_params=_params())(own, received)


def _swap_halves(halves):
    n = len(halves)

    def body(*refs):
        ins, outs = refs[:n], refs[n:2 * n]
        send_sems, recv_sems = refs[2 * n:]
        x, y, c, _ = _place()
        copies = [pltpu.make_async_remote_copy(src_ref=ins[k], dst_ref=outs[k], send_sem=send_sems.at[k],
                                               recv_sem=recv_sems.at[k], device_id=(x, y, 1 - c), device_id_type=MESH)
                  for k in range(n)]
        for cp in copies:
            cp.start()
        for cp in copies:
            cp.wait()

    any_spec = pl.BlockSpec(memory_space=pl.ANY)
    return pl.pallas_call(
        body, name="swap_halves", in_specs=[any_spec] * n, out_specs=[any_spec] * n,
        out_shape=[jax.ShapeDtypeStruct(h.shape, F32) for h in halves],
        scratch_shapes=[pltpu.SemaphoreType.DMA((n,)), pltpu.SemaphoreType.DMA((n,))])(*halves)


N_DEV = 8


def _gather_small(block):
    m_per, width = block.shape

    def body(x_ref, out_ref, send_sems, recv_sems, local_sem):
        x, y, c, chips = _place()
        me, sibling = (x, y, c), (x, y, 1 - c)

        def rows(px, py, pc):
            return out_ref.at[pl.ds((4 * px + 2 * py + pc) * m_per, m_per), :]

        def copy(k, blk, to, src=None):
            return pltpu.make_async_remote_copy(src_ref=rows(*blk) if src is None else src, dst_ref=rows(*blk),
                                                send_sem=send_sems.at[k], recv_sem=recv_sems.at[k], device_id=to, device_id_type=MESH)

        mine = pltpu.make_async_copy(x_ref, rows(*me), local_sem)
        mine.start()
        first = [copy(0, me, sibling, src=x_ref)] + [copy(1 + j, me, (*chip, c), src=x_ref) for j, chip in enumerate(chips)]
        for cp in first:
            cp.start()
        passed = [copy(4 + j, (*chip, c), sibling) for j, chip in enumerate(chips)]
        for j, chip in enumerate(chips):
            copy(1 + j, (*chip, c), me).wait_recv()
            passed[j].start()
        copy(0, sibling, me).wait_recv()
        for j, chip in enumerate(chips):
            copy(4 + j, (*chip, 1 - c), me).wait_recv()
        for cp in first + passed:
            cp.wait_send()
        mine.wait()

    vmem = pl.BlockSpec(memory_space=pltpu.VMEM)
    return pl.pallas_call(body, name="gather_small", out_shape=jax.ShapeDtypeStruct((N_DEV * m_per, width), F32),
                          in_specs=[vmem], out_specs=vmem,
                          scratch_shapes=[pltpu.SemaphoreType.DMA((7,)), pltpu.SemaphoreType.DMA((7,)),
                                          pltpu.SemaphoreType.DMA])(block)


def _adamw_math(w, g, m, v):
    m = ADAM_B1 * m + (1.0 - ADAM_B1) * g
    v = ADAM_B2 * v + (1.0 - ADAM_B2) * (g * g)
    m_hat = m / (1.0 - ADAM_B1 ** ADAM_STEP)
    v_hat = v / (1.0 - ADAM_B2 ** ADAM_STEP)
    delta = -ADAM_LR * (m_hat / (jnp.sqrt(v_hat) + ADAM_EPS) + ADAM_WD * w)
    return delta, m, v


def _adamw(w, mine, siblings, place, m, v, transposed, name):
    def body(place_ref, w_ref, mine_ref, sib_ref, m_ref, v_ref, go_ref, d_ref, mo_ref, vo_ref):
        first = place_ref[1] == 0
        g = jnp.concatenate([jnp.where(first, mine_ref[0], sib_ref[0]), jnp.where(first, sib_ref[0], mine_ref[0])], axis=0)
        g = g.T if transposed else g
        go_ref[...] = g
        d_ref[...], mo_ref[...], vo_ref[...] = _adamw_math(w_ref[...], g, m_ref[...], v_ref[...])

    vmem = pl.BlockSpec(memory_space=pltpu.VMEM)
    return pl.pallas_call(body, name=name, in_specs=[pl.BlockSpec(memory_space=pltpu.SMEM)] + [vmem] * 5, out_specs=[vmem] * 4,
                          out_shape=[jax.ShapeDtypeStruct(w.shape, F32)] * 4, compiler_params=_params())(
                              place, w, mine, siblings, m, v)


def _adamw_small(gathered, w, m, v, name):
    def body(ga_ref, w_ref, m_ref, v_ref, go_ref, d_ref, mo_ref, vo_ref):
        g = ga_ref[0]
        for dev in range(1, N_DEV):
            g = g + ga_ref[dev]
        go_ref[...] = g
        d_ref[...], mo_ref[...], vo_ref[...] = _adamw_math(w_ref[...], g, m_ref[...], v_ref[...])

    return pl.pallas_call(body, name=name, out_shape=[jax.ShapeDtypeStruct(w.shape, F32)] * 4,
                          compiler_params=_params())(gathered, w, m, v)


SMALL = ("g_ffn1_pre", "g_ffn1_post", "g_mix_pre", "w_pool_lin", "pool_scale", "g_mix_post", "g_ffn2_pre", "g_ffn2_post")
WEIGHTS = ("g_ffn1_pre", "w1_gate", "w1_up", "w1_down", "g_ffn1_post", "g_mix_pre", "w_in", "w_pool_lin", "pool_scale", "w_out",
           "g_mix_post", "g_ffn2_pre", "w2_gate", "w2_up", "w2_down", "g_ffn2_post")
LANES = 128


def _pack_small(tree):
    flat = jnp.concatenate([tree[k].reshape(-1) for k in SMALL])
    rows = -(-flat.shape[0] // (8 * LANES)) * 8
    return jnp.pad(flat, (0, rows * LANES - flat.shape[0])).reshape(rows, LANES)


def _unpack_small(packed, like):
    flat, out, at = packed.reshape(-1), {}, 0
    for k in SMALL:
        size = math.prod(like[k].shape)
        out[k] = flat[at:at + size].reshape(like[k].shape)
        at += size
    return out


def kernel(x, g_ffn1_pre, w1_gate, w1_up, w1_down, g_ffn1_post, g_mix_pre, w_in, w_pool_lin, pool_scale, w_out, g_mix_post, g_ffn2_pre, w2_gate, w2_up, w2_down, g_ffn2_post, loss_target, m_g_ffn1_pre, m_w1_gate, m_w1_up, m_w1_down, m_g_ffn1_post, m_g_mix_pre, m_w_in, m_w_pool_lin, m_pool_scale, m_w_out, m_g_mix_post, m_g_ffn2_pre, m_w2_gate, m_w2_up, m_w2_down, m_g_ffn2_post, v_g_ffn1_pre, v_w1_gate, v_w1_up, v_w1_down, v_g_ffn1_post, v_g_mix_pre, v_w_in, v_w_pool_lin, v_pool_scale, v_w_out, v_g_mix_post, v_g_ffn2_pre, v_w2_gate, v_w2_up, v_w2_down, v_g_ffn2_post):
    given = dict(locals())
    w = {k: given[k] for k in WEIGHTS}
    m = {k: given["m_" + k] for k in WEIGHTS}
    v = {k: given["v_" + k] for k in WEIGHTS}
    small = {k: (w[k][0] if k == "w_pool_lin" else w[k].reshape(1, -1)) for k in SMALL}

    place = jnp.stack([2 * lax.axis_index("x") + lax.axis_index("y"), lax.axis_index("c")]).astype(jnp.int32)
    shards = [_cast_shard(w[k][0], k in TRANSPOSED, place, f"cast_{k}") for k in SEGMENTS]
    full = dict(zip(SEGMENTS, _gather_weights(shards)))

    loss_part, grad_x, grads, small_grads = _local_step(x[0], loss_target[0], small, full)
    loss = lax.psum(loss_part, ("x", "y", "c"))

    local = [grads[k].reshape(N_CHIPS, 2, grads[k].shape[0] // (2 * N_CHIPS), grads[k].shape[1]) for k in SEGMENTS]
    from_sibling = _sibling_halves(local)
    sums = [_chip_sum(g, fs, place, f"chip_sum_{k}") for g, fs, k in zip(local, from_sibling, SEGMENTS)]
    received = _scatter_chip_sums([sm[1] for sm in sums])
    halves = [_total_sum(own, rec, f"total_{k}") for (own, _), rec, k in zip(sums, received, SEGMENTS)]
    from_sibling = _swap_halves(halves)

    out_grad, out_delta, out_m, out_v = {}, {}, {}, {}
    for k, mine, sib in zip(SEGMENTS, halves, from_sibling):
        out_grad[k], out_delta[k], out_m[k], out_v[k] = (
            a[None] for a in _adamw(w[k][0], mine, sib, place, m[k][0], v[k][0], k in TRANSPOSED, f"adamw_{k}"))

    small_grads["w_pool_lin"] = small_grads["w_pool_lin"][None]
    packed = _pack_small(small_grads)
    gathered = _gather_small(packed).reshape(N_DEV, *packed.shape)
    like = {k: w[k] for k in SMALL}
    results = _adamw_small(gathered, _pack_small(like), _pack_small({k: m[k] for k in SMALL}),
                           _pack_small({k: v[k] for k in SMALL}), "adamw_small")
    for tree, res in zip((out_grad, out_delta, out_m, out_v), results):
        tree.update(_unpack_small(res, like))

    return (loss, grad_x[None], *[out_grad[k] for k in WEIGHTS], *[out_delta[k] for k in WEIGHTS],
            *[out_m[k] for k in WEIGHTS], *[out_v[k] for k in WEIGHTS])
```

```python
import functools
import math

import numpy as np
import jax
import jax.numpy as jnp
from jax import lax
from jax.experimental import pallas as pl
from jax.experimental.pallas import tpu as pltpu

F32 = jnp.float32
BF16 = jnp.bfloat16
MESH = pl.DeviceIdType.MESH

RMS_EPS = 1e-6
HEAD_DIM = 64
POOL_HALF_WINDOWS = (1, 2, 4, 8)
POOL_DIM = 256
GROUP_DIM = 256
DILATIONS = (1, 4, 16)
N_SIDE = 64
N_ATTN_HEADS = 12
ADAM_LR, ADAM_B1, ADAM_B2, ADAM_EPS, ADAM_WD, ADAM_STEP = 0.001, 0.9, 0.999, 1e-08, 0.01, 10

N_CHIPS = 4
V7X_VMEM_LIMIT = 60 * 1024 * 1024

_NT = (((1,), (1,)), ((), ()))
_TN = (((0,), (0,)), ((), ()))


def _dot(a, b):
    return jnp.dot(a, b, preferred_element_type=F32)


def _dot_nt(a, b):
    return lax.dot_general(a, b, _NT, preferred_element_type=F32)


def _dot_tn(a, b):
    return lax.dot_general(a, b, _TN, preferred_element_type=F32)


def _params(**kw):
    return pltpu.CompilerParams(vmem_limit_bytes=V7X_VMEM_LIMIT, **kw)


def _rows(tm, width):
    return pl.BlockSpec((tm, width), lambda i: (i, 0))


def _resident(shape):
    return pl.BlockSpec(shape, lambda i: (0,) * len(shape), pipeline_mode=pl.Buffered(1))


def _const(shape):
    return pl.BlockSpec(shape, lambda i: (0,) * len(shape))


def _tile(rows, cap):
    return max(t for t in range(16, cap + 1, 16) if rows % t == 0)


def _inv_rms(x):
    return lax.rsqrt(jnp.mean(x * x, axis=-1, keepdims=True) + RMS_EPS)


def _rms_bwd(x, inv, g, dy):
    n = x * inv
    dn = dy * g
    dx = inv * (dn - n * jnp.mean(dn * n, axis=-1, keepdims=True))
    return dx, jnp.sum(dy * n, axis=0, keepdims=True)


def _accumulate(ref, value):
    @pl.when(pl.program_id(0) == 0)
    def _():
        ref[...] = jnp.zeros_like(ref)

    ref[...] += value


def _ffn_fwd(x, g_pre, wg_t, wu_t, wd, g_post, target, name, tm=256):
    s, d = x.shape
    ff = wd.shape[0]
    with_loss = target is not None

    def body(*refs):
        if with_loss:
            x_ref, gpre_ref, wg_ref, wu_ref, wd_ref, gpost_ref, t_ref, xo_ref, a_ref, b_ref, f_ref, loss_ref = refs
        else:
            x_ref, gpre_ref, wg_ref, wu_ref, wd_ref, gpost_ref, xo_ref, a_ref, b_ref, f_ref = refs
        xv = x_ref[...]
        hb = (xv * _inv_rms(xv) * gpre_ref[...]).astype(BF16)
        a = _dot_nt(hb, wg_ref[...])
        b = _dot_nt(hb, wu_ref[...])
        hh = (a * jax.nn.sigmoid(a)) * b
        f = _dot(hh.astype(BF16), wd_ref[...])
        xo = xv + 0.5 * (f * _inv_rms(f) * gpost_ref[...])
        a_ref[...] = a.astype(BF16)
        b_ref[...] = b.astype(BF16)
        f_ref[...] = f
        if with_loss:
            e = xo - t_ref[...]
            xo_ref[...] = e * (1.0 / d)
            _accumulate(loss_ref, 0.5 * jnp.sum(jnp.mean(e * e, axis=-1, keepdims=True)))
        else:
            xo_ref[...] = xo

    in_specs = [_rows(tm, d), _const((1, d)), _resident((ff, d)), _resident((ff, d)), _resident((ff, d)), _const((1, d))]
    args = [x, g_pre, wg_t, wu_t, wd, g_post]
    out_shape = [jax.ShapeDtypeStruct((s, d), F32), jax.ShapeDtypeStruct((s, ff), BF16),
                 jax.ShapeDtypeStruct((s, ff), BF16), jax.ShapeDtypeStruct((s, d), F32)]
    out_specs = [_rows(tm, d), _rows(tm, ff), _rows(tm, ff), _rows(tm, d)]
    if with_loss:
        in_specs.append(_rows(tm, d))
        args.append(target)
        out_shape.append(jax.ShapeDtypeStruct((8, 128), F32))
        out_specs.append(_const((8, 128)))
    return pl.pallas_call(body, name=name, grid=(s // tm,), in_specs=in_specs, out_specs=out_specs,
                          out_shape=out_shape, compiler_params=_params(dimension_semantics=("arbitrary",)))(*args)


def _ffn_bwd(dxo, x, f, a, b, g_pre, g_post, wg_t, wu_t, wd, name, tm=256):
    s, d = x.shape
    ff = wd.shape[0]

    def body(dxo_ref, x_ref, f_ref, a_ref, b_ref, gpre_ref, gpost_ref, wg_ref, wu_ref, wd_ref,
             dx_ref, hh_ref, da_ref, db_ref, df_ref, h_ref, dgpre_ref, dgpost_ref):
        dxo_v = dxo_ref[...]
        fv = f_ref[...]
        df, dgpost = _rms_bwd(fv, _inv_rms(fv), gpost_ref[...], 0.5 * dxo_v)
        dfb = df.astype(BF16)
        dhh = _dot_nt(dfb, wd_ref[...])
        av = a_ref[...].astype(F32)
        bv = b_ref[...].astype(F32)
        sig = jax.nn.sigmoid(av)
        sa = av * sig
        da = (dhh * bv * (sig * (1.0 + av * (1.0 - sig)))).astype(BF16)
        db = (dhh * sa).astype(BF16)
        dh = _dot(da, wg_ref[...]) + _dot(db, wu_ref[...])
        xv = x_ref[...]
        inv = _inv_rms(xv)
        dxn, dgpre = _rms_bwd(xv, inv, gpre_ref[...], dh)
        dx_ref[...] = dxo_v + dxn
        hh_ref[...] = (sa * bv).astype(BF16)
        da_ref[...] = da
        db_ref[...] = db
        df_ref[...] = dfb
        h_ref[...] = (xv * inv * gpre_ref[...]).astype(BF16)
        _accumulate(dgpre_ref, dgpre)
        _accumulate(dgpost_ref, dgpost)

    return pl.pallas_call(
        body, name=name, grid=(s // tm,),
        in_specs=[_rows(tm, d), _rows(tm, d), _rows(tm, d), _rows(tm, ff), _rows(tm, ff), _const((1, d)), _const((1, d)),
                  _resident((ff, d)), _resident((ff, d)), _resident((ff, d))],
        out_specs=[_rows(tm, d), _rows(tm, ff), _rows(tm, ff), _rows(tm, ff), _rows(tm, d), _rows(tm, d),
                   _const((1, d)), _const((1, d))],
        out_shape=[jax.ShapeDtypeStruct((s, d), F32), jax.ShapeDtypeStruct((s, ff), BF16), jax.ShapeDtypeStruct((s, ff), BF16),
                   jax.ShapeDtypeStruct((s, ff), BF16), jax.ShapeDtypeStruct((s, d), BF16), jax.ShapeDtypeStruct((s, d), BF16),
                   jax.ShapeDtypeStruct((1, d), F32), jax.ShapeDtypeStruct((1, d), F32)],
        compiler_params=_params(dimension_semantics=("arbitrary",)))(dxo, x, f, a, b, g_pre, g_post, wg_t, wu_t, wd)


def _wgrad(lhs, rhs, name, tk=512):
    s, r = lhs.shape
    c = rhs.shape[1]

    def body(l_ref, r_ref, o_ref):
        _accumulate(o_ref, _dot_tn(l_ref[...], r_ref[...]))

    return pl.pallas_call(body, name=name, grid=(s // tk,), in_specs=[_rows(tk, r), _rows(tk, c)],
                          out_specs=_const((r, c)), out_shape=jax.ShapeDtypeStruct((r, c), F32),
                          compiler_params=_params(dimension_semantics=("arbitrary",)))(lhs, rhs)


def _in_fwd(x, g, w_in_t, name, tm=512):
    s, d = x.shape
    d_in = w_in_t.shape[0]
    n_parts = (d_in - POOL_DIM) // GROUP_DIM

    def body(x_ref, g_ref, w_ref, u_ref, *part_refs):
        xv = x_ref[...]
        hb = (xv * _inv_rms(xv) * g_ref[...]).astype(BF16)
        z = _dot_nt(hb, w_ref[...])
        u_ref[...] = z[:, :POOL_DIM]
        for j, ref in enumerate(part_refs):
            ref[...] = z[:, POOL_DIM + GROUP_DIM * j:POOL_DIM + GROUP_DIM * (j + 1)]

    return pl.pallas_call(
        body, name=name, grid=(s // tm,), in_specs=[_rows(tm, d), _const((1, d)), _resident((d_in, d))],
        out_specs=[_rows(tm, POOL_DIM)] + [_rows(tm, GROUP_DIM)] * n_parts,
        out_shape=[jax.ShapeDtypeStruct((s, POOL_DIM), F32)] + [jax.ShapeDtypeStruct((s, GROUP_DIM), F32)] * n_parts,
        compiler_params=_params(dimension_semantics=("arbitrary",)))(x, g, w_in_t)


def _in_bwd(du, dparts, x, dxo, g, w_in_t, name, tm=512):
    s, d = x.shape
    d_in = w_in_t.shape[0]
    n_parts = len(dparts)

    def body(du_ref, *refs):
        part_refs = refs[:n_parts]
        x_ref, dxo_ref, g_ref, w_ref, dx_ref, dz_ref, h_ref, dg_ref = refs[n_parts:]
        dz = jnp.concatenate([r[...].astype(BF16) for r in (du_ref,) + part_refs], axis=1)
        dz_ref[...] = dz
        dh = _dot(dz, w_ref[...])
        xv = x_ref[...]
        inv = _inv_rms(xv)
        dxn, dg = _rms_bwd(xv, inv, g_ref[...], dh)
        dx_ref[...] = dxo_ref[...] + dxn
        h_ref[...] = (xv * inv * g_ref[...]).astype(BF16)
        _accumulate(dg_ref, dg)

    return pl.pallas_call(
        body, name=name, grid=(s // tm,),
        in_specs=[_rows(tm, POOL_DIM)] + [_rows(tm, GROUP_DIM)] * n_parts + [_rows(tm, d), _rows(tm, d), _const((1, d)),
                                                                             _resident((d_in, d))],
        out_specs=[_rows(tm, d), _rows(tm, d_in), _rows(tm, d), _const((1, d))],
        out_shape=[jax.ShapeDtypeStruct((s, d), F32), jax.ShapeDtypeStruct((s, d_in), BF16), jax.ShapeDtypeStruct((s, d), BF16),
                   jax.ShapeDtypeStruct((1, d), F32)],
        compiler_params=_params(dimension_semantics=("arbitrary",)))(du, *dparts, x, dxo, g, w_in_t)


_POOL_HALO = 8


def _pool_chain(v, first_shift):
    n = v.shape[0]
    p2 = v + pltpu.roll(v, first_shift, 0)
    p4 = pltpu.roll(p2, 1, 0) + pltpu.roll(p2, n - 1, 0)
    p8 = pltpu.roll(p4, 2, 0) + pltpu.roll(p4, n - 2, 0)
    p16 = pltpu.roll(p8, 4, 0) + pltpu.roll(p8, n - 4, 0)
    group = lax.broadcasted_iota(jnp.int32, v.shape, 1) // HEAD_DIM
    return jnp.where(group == 0, p2, jnp.where(group == 1, p4, jnp.where(group == 2, p8, p16)))


def _pool_count(t0, rows, s):
    t = t0 + lax.broadcasted_iota(jnp.int32, (rows, POOL_DIM), 0)
    group = lax.broadcasted_iota(jnp.int32, (rows, POOL_DIM), 1) // HEAD_DIM
    half = jnp.where(group == 0, 1, jnp.where(group == 1, 2, jnp.where(group == 2, 4, 8)))
    cnt = jnp.minimum(t + half, s) - jnp.maximum(t - half, 0)
    return jnp.maximum(cnt, 1).astype(F32)


def _pad_rows(ref, pad_ref, s):
    zeros = jnp.zeros((_POOL_HALO, pad_ref.shape[1]), pad_ref.dtype)
    pad_ref[pl.ds(0, _POOL_HALO), :] = zeros
    pad_ref[pl.ds(_POOL_HALO + s, _POOL_HALO), :] = zeros
    pad_ref[pl.ds(_POOL_HALO, s), :] = ref[...]


def _pool_fwd(u, w_bd, scale, name, tm=512):
    s = u.shape[0]
    ext = tm + 2 * _POOL_HALO

    def body(u_ref, w_ref, sc_ref, o_ref, upad):
        _pad_rows(u_ref, upad, s)

        def tile(i, carry):
            t0 = pl.multiple_of(i * tm, tm)
            uv = upad[pl.ds(t0, ext), :]
            win = _pool_chain(uv, 1)[_POOL_HALO:_POOL_HALO + tm]
            y = win / _pool_count(t0, tm, s) - uv[_POOL_HALO:_POOL_HALO + tm]
            o_ref[pl.ds(t0, tm), :] = (_dot(y.astype(BF16), w_ref[...]) * sc_ref[...]).astype(BF16)
            return carry

        lax.fori_loop(0, s // tm, tile, 0)

    return pl.pallas_call(body, name=name, out_shape=jax.ShapeDtypeStruct((s, POOL_DIM), BF16),
                          scratch_shapes=[pltpu.VMEM((s + 2 * _POOL_HALO, POOL_DIM), F32)],
                          compiler_params=_params())(u, w_bd, scale)


def _pool_bwd(u, da, w_bd, scale, name, tm=512):
    s = u.shape[0]
    ext = tm + 2 * _POOL_HALO

    def body(u_ref, da_ref, w_ref, sc_ref, du_ref, dw_ref, dsc_ref, upad, dapad):
        _pad_rows(u_ref, upad, s)
        _pad_rows(da_ref, dapad, s)
        dw_ref[...] = jnp.zeros_like(dw_ref)
        dsc_ref[...] = jnp.zeros_like(dsc_ref)

        def tile(i, carry):
            t0 = pl.multiple_of(i * tm, tm)
            uv = upad[pl.ds(t0, ext), :]
            dav = dapad[pl.ds(t0, ext), :]
            win = _pool_chain(uv, 1)[_POOL_HALO:_POOL_HALO + tm]
            yb = (win / _pool_count(t0, tm, s) - uv[_POOL_HALO:_POOL_HALO + tm]).astype(BF16)
            yl = _dot(yb, w_ref[...])
            da_c = dav[_POOL_HALO:_POOL_HALO + tm]
            dsc_ref[...] += jnp.sum(da_c * yl, axis=0, keepdims=True)
            dyl = (dav * sc_ref[...]).astype(BF16)
            dw_ref[...] += _dot_tn(yb, dyl[_POOL_HALO:_POOL_HALO + tm])
            dy = _dot_nt(dyl, w_ref[...])
            dyc = dy / _pool_count(t0 - _POOL_HALO, ext, s)
            du_ref[pl.ds(t0, tm), :] = (_pool_chain(dyc, ext - 1) - dy)[_POOL_HALO:_POOL_HALO + tm]
            return carry

        lax.fori_loop(0, s // tm, tile, 0)

    pool_cols = pl.BlockSpec((s, POOL_DIM), lambda i: (0, 0), pipeline_mode=pl.Buffered(1))
    return pl.pallas_call(
        body, name=name, grid=(1,),
        in_specs=[pool_cols, pool_cols, _const((POOL_DIM, POOL_DIM)), _const((1, POOL_DIM))],
        out_specs=[_const((s, POOL_DIM)), _const((POOL_DIM, POOL_DIM)), _const((1, POOL_DIM))],
        out_shape=[jax.ShapeDtypeStruct((s, POOL_DIM), F32), jax.ShapeDtypeStruct((POOL_DIM, POOL_DIM), F32),
                   jax.ShapeDtypeStruct((1, POOL_DIM), F32)],
        scratch_shapes=[pltpu.VMEM((s + 2 * _POOL_HALO, POOL_DIM), F32), pltpu.VMEM((s + 2 * _POOL_HALO, POOL_DIM), F32)],
        compiler_params=_params(dimension_semantics=("arbitrary",)))(u, da, w_bd, scale)


_BQ = 128
_KW = _BQ + 2 * N_SIDE
_PAIR = 2 * HEAD_DIM
_NEG = -1e30
_ATTN_UNROLL = 4
_SCORE_SCALE = HEAD_DIM ** -0.5


def _stack_heads(x):
    lane_head = lax.broadcasted_iota(jnp.int32, x.shape, 1) // HEAD_DIM
    zero = jnp.zeros_like(x)
    return jnp.concatenate([jnp.where(lane_head == 0, x, zero), jnp.where(lane_head == 1, x, zero)], axis=0)


def _unstack_heads(x):
    lane_head = lax.broadcasted_iota(jnp.int32, (_BQ, _PAIR), 1) // HEAD_DIM
    return jnp.where(lane_head == 0, x[:_BQ], x[_BQ:])


def _stack_cols(x):
    return jnp.concatenate([x[:, 0:1], x[:, HEAD_DIM:HEAD_DIM + 1]], axis=0)


def _fill_bias(bias_ref, slopes_ref, dilation):
    row = lax.broadcasted_iota(jnp.int32, (2 * _BQ, _KW), 0)
    col = lax.broadcasted_iota(jnp.int32, (2 * _BQ, _KW), 1)
    pair = 2 * pl.program_id(0)
    slope = jnp.where(row < _BQ, slopes_ref[pair], slopes_ref[pair + 1]) * float(dilation)
    for j in range(3):
        dist = jnp.abs(col - (row & (_BQ - 1)) - j * N_SIDE)
        bias_ref[j] = jnp.where(dist <= N_SIDE, -slope * dist.astype(F32), _NEG)


def _block_window(i, n_blocks, length):
    q0 = pl.multiple_of(i * _BQ, _BQ)
    ws = pl.multiple_of(jnp.clip(q0 - N_SIDE, 0, length - _KW), N_SIDE)
    return q0, ws, jnp.where(i == 0, 0, jnp.where(i == n_blocks - 1, 2, 1))


def _residue_rows(dilation, start, count):
    if dilation == 1:
        return pl.ds(start, count)
    return pl.ds(start * dilation + pl.program_id(1), count, stride=dilation)


def _attn_call(body, name, dilation, seq, n_in, n_out, scratch):
    col = pl.BlockSpec((seq, _PAIR), lambda c, r: (0, c), pipeline_mode=pl.Buffered(1))
    return pl.pallas_call(
        body, name=name, grid=(GROUP_DIM // _PAIR, dilation),
        in_specs=[pl.BlockSpec(memory_space=pltpu.SMEM)] + [col] * n_in, out_specs=[col] * n_out,
        out_shape=[jax.ShapeDtypeStruct((seq, GROUP_DIM), F32)] * n_out, scratch_shapes=scratch,
        compiler_params=_params(dimension_semantics=("arbitrary", "arbitrary")))


def _attn_fwd(q, k, v, slopes, dilation, name):
    seq = q.shape[0]
    length = seq // dilation
    n_blocks = length // _BQ

    def body(sl_ref, q_ref, k_ref, v_ref, o_ref, lse_ref, qs, ks, vs, bias_ref):
        all_rows = _residue_rows(dilation, 0, length)
        qs[...] = (q_ref[all_rows, :] * _SCORE_SCALE).astype(BF16)
        ks[...] = k_ref[all_rows, :].astype(BF16)
        vs[...] = v_ref[all_rows, :].astype(BF16)
        _fill_bias(bias_ref, sl_ref, dilation)

        def block(i, carry):
            q0, ws, which = _block_window(i, n_blocks, length)
            kw = ks[pl.ds(ws, _KW), :]
            vw = vs[pl.ds(ws, _KW), :]
            sc = _dot_nt(_stack_heads(qs[pl.ds(q0, _BQ), :]), kw) + bias_ref[which]
            m = jnp.max(sc, axis=-1, keepdims=True)
            p = jnp.exp(sc - m)
            den = jnp.sum(p, axis=-1, keepdims=True)
            rows = _residue_rows(dilation, q0, _BQ)
            o_ref[rows, :] = _unstack_heads(_dot(p.astype(BF16), vw) / den)
            lse_ref[rows, :] = _unstack_heads(jnp.broadcast_to(m + jnp.log(den), (2 * _BQ, _PAIR)))
            return carry

        lax.fori_loop(0, n_blocks, block, 0, unroll=min(_ATTN_UNROLL, n_blocks))

    stage = pltpu.VMEM((length, _PAIR), BF16)
    bias = pltpu.VMEM((3, 2 * _BQ, _KW), F32)
    return _attn_call(body, name, dilation, seq, 3, 2, [stage] * 3 + [bias])(slopes, q, k, v)


def _attn_bwd(q, k, v, do, lse, cterm, slopes, dilation, name):
    seq = q.shape[0]
    length = seq // dilation
    n_blocks = length // _BQ

    def body(sl_ref, q_ref, k_ref, v_ref, do_ref, lse_ref, c_ref, dq_ref, dk_ref, dv_ref, qs, ks, vs, dos, dk_acc, dv_acc, bias_ref):
        all_rows = _residue_rows(dilation, 0, length)
        qs[...] = (q_ref[all_rows, :] * _SCORE_SCALE).astype(BF16)
        for src, dst in ((k_ref, ks), (v_ref, vs), (do_ref, dos)):
            dst[...] = src[all_rows, :].astype(BF16)
        dk_acc[...] = jnp.zeros_like(dk_acc)
        dv_acc[...] = jnp.zeros_like(dv_acc)
        _fill_bias(bias_ref, sl_ref, dilation)

        def block(i, carry):
            q0, ws, which = _block_window(i, n_blocks, length)
            rows = _residue_rows(dilation, q0, _BQ)
            qm = _stack_heads(qs[pl.ds(q0, _BQ), :])
            dom = _stack_heads(dos[pl.ds(q0, _BQ), :])
            kw = ks[pl.ds(ws, _KW), :]
            vw = vs[pl.ds(ws, _KW), :]
            p = jnp.exp(_dot_nt(qm, kw) + bias_ref[which] - _stack_cols(lse_ref[rows, :]))
            ds = (p * (_dot_nt(dom, vw) + _stack_cols(c_ref[rows, :]))).astype(BF16)
            dq_ref[rows, :] = _unstack_heads(_dot(ds, kw)) * _SCORE_SCALE
            dk_acc[pl.ds(ws, _KW), :] += _dot_tn(ds, qm)
            dv_acc[pl.ds(ws, _KW), :] += _dot_tn(p.astype(BF16), dom)
            return carry

        lax.fori_loop(0, n_blocks, block, 0, unroll=min(_ATTN_UNROLL, n_blocks))
        dk_ref[all_rows, :] = dk_acc[...]
        dv_ref[all_rows, :] = dv_acc[...]

    stage = pltpu.VMEM((length, _PAIR), BF16)
    acc = pltpu.VMEM((length, _PAIR), F32)
    bias = pltpu.VMEM((3, 2 * _BQ, _KW), F32)
    return _attn_call(body, name, dilation, seq, 6, 3, [stage] * 4 + [acc] * 2 + [bias])(slopes, q, k, v, do, lse, cterm)


def _group_weights(lses):
    m = jnp.maximum(jnp.maximum(lses[0], lses[1]), lses[2])
    es = [jnp.exp(l - m) for l in lses]
    den = es[0] + es[1] + es[2]
    return [e / den for e in es]


def _combine_fwd(a_pool, outs, lses, name, tm=512):
    s = a_pool.shape[0]

    def body(ap_ref, o0, o1, o2, l0, l1, l2, cat_ref):
        alphas = _group_weights([l0[...], l1[...], l2[...]])
        parts = [ap_ref[...]] + [(o[...] * al).astype(BF16) for o, al in zip((o0, o1, o2), alphas)]
        cat_ref[...] = jnp.concatenate(parts, axis=1)

    width = POOL_DIM + 3 * GROUP_DIM
    return pl.pallas_call(body, name=name, grid=(s // tm,), in_specs=[_rows(tm, POOL_DIM)] + [_rows(tm, GROUP_DIM)] * 6,
                          out_specs=_rows(tm, width), out_shape=jax.ShapeDtypeStruct((s, width), BF16),
                          compiler_params=_params(dimension_semantics=("arbitrary",)))(a_pool, *outs, *lses)


def _combine_bwd(dcat, outs, lses, head_ones, name, tm=512):
    s = dcat.shape[0]

    def body(dc_ref, o0, o1, o2, l0, l1, l2, ones_ref, do0, do1, do2, c0, c1, c2):
        alphas = _group_weights([l0[...], l1[...], l2[...]])
        dcat_v = dc_ref[...]
        das = [dcat_v[:, POOL_DIM + GROUP_DIM * g:POOL_DIM + GROUP_DIM * (g + 1)] for g in range(3)]
        prod = sum(da * (o[...] * al) for da, o, al in zip(das, (o0, o1, o2), alphas))
        hi = prod.astype(BF16)
        lo = (prod - hi.astype(F32)).astype(BF16)
        total = _dot(hi, ones_ref[...]) + _dot(lo, ones_ref[...])
        for da, al, do_ref, c_ref in zip(das, alphas, (do0, do1, do2), (c0, c1, c2)):
            do_ref[...] = da * al
            c_ref[...] = -al * total

    width = POOL_DIM + 3 * GROUP_DIM
    return pl.pallas_call(
        body, name=name, grid=(s // tm,),
        in_specs=[_rows(tm, width)] + [_rows(tm, GROUP_DIM)] * 6 + [_const((GROUP_DIM, GROUP_DIM))],
        out_specs=[_rows(tm, GROUP_DIM)] * 6,
        out_shape=[jax.ShapeDtypeStruct((s, GROUP_DIM), F32)] * 6,
        compiler_params=_params(dimension_semantics=("arbitrary",)))(dcat, *outs, *lses, head_ones)


def _out_fwd(cat, x, w_out, g, name, tm=512):
    s, d = x.shape

    def body(cat_ref, x_ref, w_ref, g_ref, xo_ref, mix_ref):
        mix = _dot(cat_ref[...], w_ref[...])
        mix_ref[...] = mix
        xo_ref[...] = x_ref[...] + mix * _inv_rms(mix) * g_ref[...]

    return pl.pallas_call(body, name=name, grid=(s // tm,),
                          in_specs=[_rows(tm, cat.shape[1]), _rows(tm, d), _resident(w_out.shape), _const((1, d))],
                          out_specs=[_rows(tm, d), _rows(tm, d)], out_shape=[jax.ShapeDtypeStruct((s, d), F32)] * 2,
                          compiler_params=_params(dimension_semantics=("arbitrary",)))(cat, x, w_out, g)


def _out_bwd(dxo, mix, w_out, g, name, tm=512):
    s, d = mix.shape
    width = w_out.shape[0]

    def body(dxo_ref, mix_ref, w_ref, g_ref, dcat_ref, dmix_ref, dg_ref):
        mv = mix_ref[...]
        dmix, dg = _rms_bwd(mv, _inv_rms(mv), g_ref[...], dxo_ref[...])
        dmb = dmix.astype(BF16)
        dmix_ref[...] = dmb
        dcat_ref[...] = _dot_nt(dmb, w_ref[...])
        _accumulate(dg_ref, dg)

    return pl.pallas_call(
        body, name=name, grid=(s // tm,), in_specs=[_rows(tm, d), _rows(tm, d), _resident(w_out.shape), _const((1, d))],
        out_specs=[_rows(tm, width), _rows(tm, d), _const((1, d))],
        out_shape=[jax.ShapeDtypeStruct((s, width), F32), jax.ShapeDtypeStruct((s, d), BF16), jax.ShapeDtypeStruct((1, d), F32)],
        compiler_params=_params(dimension_semantics=("arbitrary",)))(dxo, mix, w_out, g)


def _alibi_slopes():
    return np.array([2.0 ** (-8.0 * (i + 1) / N_ATTN_HEADS) for i in range(N_ATTN_HEADS)], np.float32)


def _block_diag(w_lin):
    n, c, _ = w_lin.shape
    eye = jnp.eye(n, dtype=w_lin.dtype)
    return (eye[:, None, :, None] * w_lin[:, :, None, :]).reshape(n * c, n * c)


def _local_step(x, target, small, full):
    s, d = x.shape
    slopes = _alibi_slopes()
    group_slopes = [jnp.asarray(slopes[4 * g:4 * g + 4]) for g in range(3)]
    w_bd = _block_diag(small["w_pool_lin"]).astype(BF16)
    head_ones = jnp.asarray(np.kron(np.eye(GROUP_DIM // HEAD_DIM), np.ones((HEAD_DIM, HEAD_DIM))), BF16)

    x1, a1, b1, f1 = _ffn_fwd(x, small["g_ffn1_pre"], full["w1_gate"], full["w1_up"], full["w1_down"], small["g_ffn1_post"],
                              None, "ffn1_fwd")
    u, *parts = _in_fwd(x1, small["g_mix_pre"], full["w_in"], "in_fwd")
    qs, ks, vs = parts[0:3], parts[3:6], parts[6:9]
    a_pool = _pool_fwd(u, w_bd, small["pool_scale"], "pool_fwd")
    outs, lses = [], []
    for g, dil in enumerate(DILATIONS):
        o, lse = _attn_fwd(qs[g], ks[g], vs[g], group_slopes[g], dil, f"attn_fwd{g}")
        outs.append(o)
        lses.append(lse)
    cat = _combine_fwd(a_pool, outs, lses, "combine_fwd")
    x2, mix = _out_fwd(cat, x1, full["w_out"], small["g_mix_post"], "out_fwd")
    dx3, a2, b2, f2, loss_part = _ffn_fwd(x2, small["g_ffn2_pre"], full["w2_gate"], full["w2_up"], full["w2_down"],
                                          small["g_ffn2_post"], target, "ffn2_fwd")

    grads, small_grads = {}, {}

    def ffn_backward(tag, dxo, x_in, f, a, b):
        dx, hh, da, db, df, h, dg_pre, dg_post = _ffn_bwd(
            dxo, x_in, f, a, b, small[f"g_{tag}_pre"], small[f"g_{tag}_post"],
            full[f"w{tag[-1]}_gate"], full[f"w{tag[-1]}_up"], full[f"w{tag[-1]}_down"], f"{tag}_bwd")
        grads[f"w{tag[-1]}_down"] = _wgrad(hh, df, f"{tag}_wgrad_down")
        grads[f"w{tag[-1]}_gate"] = _wgrad(da, h, f"{tag}_wgrad_gate")
        grads[f"w{tag[-1]}_up"] = _wgrad(db, h, f"{tag}_wgrad_up")
        small_grads[f"g_{tag}_pre"], small_grads[f"g_{tag}_post"] = dg_pre, dg_post
        return dx

    dx2 = ffn_backward("ffn2", dx3, x2, f2, a2, b2)
    dcat, dmix, small_grads["g_mix_post"] = _out_bwd(dx2, mix, full["w_out"], small["g_mix_post"], "out_bwd")
    grads["w_out"] = _wgrad(cat, dmix, "wgrad_out")
    dos_cs = _combine_bwd(dcat, outs, lses, head_ones, "combine_bwd")
    dos, cs = dos_cs[:3], dos_cs[3:]
    dqs, dks, dvs = [], [], []
    for g, dil in enumerate(DILATIONS):
        dq, dk, dv = _attn_bwd(qs[g], ks[g], vs[g], dos[g], lses[g], cs[g], group_slopes[g], dil, f"attn_bwd{g}")
        dqs.append(dq)
        dks.append(dk)
        dvs.append(dv)
    du, dw_bd, small_grads["pool_scale"] = _pool_bwd(u, dcat, w_bd, small["pool_scale"], "pool_bwd")
    n_pool = len(POOL_HALF_WINDOWS)
    small_grads["w_pool_lin"] = jnp.stack(
        [dw_bd[HEAD_DIM * g:HEAD_DIM * (g + 1), HEAD_DIM * g:HEAD_DIM * (g + 1)] for g in range(n_pool)])
    dx1, dz, h2, small_grads["g_mix_pre"] = _in_bwd(du, dqs + dks + dvs, x1, dx2, small["g_mix_pre"], full["w_in"], "in_bwd")
    grads["w_in"] = _wgrad(dz, h2, "wgrad_in")
    dx0 = ffn_backward("ffn1", dx1, x, f1, a1, b1)
    return loss_part[0, 0], dx0, grads, small_grads


SEGMENTS = ("w1_gate", "w1_up", "w1_down", "w_in", "w_out", "w2_gate", "w2_up", "w2_down")
TRANSPOSED = ("w1_gate", "w1_up", "w_in", "w2_gate", "w2_up")
HALF = 512


def _place():
    x, y, c = lax.axis_index("x"), lax.axis_index("y"), lax.axis_index("c")
    other_chips = [(1 - x, y), (x, 1 - y), (1 - x, 1 - y)]
    return x, y, c, other_chips


def _chip_rows(chip, rows):
    return pl.ds(pl.multiple_of((2 * chip[0] + chip[1]) * rows, 16), rows)


def _cols(c):
    return pl.ds(pl.multiple_of(c * HALF, HALF), HALF)


def _cast_shard(w, transpose, place, name, tm=256):
    r, c = w.shape
    if transpose:
        def body(place_ref, w_ref, o_ref):
            o_ref[...] = w_ref[...].T.astype(BF16)

        grid, in_block, out_block, rows = (r // tm,), (tm, c), (c, tm), c
    else:
        def body(place_ref, w_ref, o_ref):
            o_ref[...] = w_ref[...].astype(BF16)

        grid, in_block, out_block, rows = (1,), (r, c), (r, c), r
    return pl.pallas_call(
        body, name=name,
        grid_spec=pltpu.PrefetchScalarGridSpec(
            num_scalar_prefetch=1, grid=grid, in_specs=[pl.BlockSpec(in_block, lambda i, place: (i, 0))],
            out_specs=pl.BlockSpec(out_block, lambda i, place: (place[0], i))),
        out_shape=jax.ShapeDtypeStruct((N_CHIPS * rows, 1024), BF16),
        compiler_params=_params(dimension_semantics=("arbitrary",)))(place, w)


def _gather_weights(bufs):
    n = len(bufs)
    rows = [b.shape[0] // N_CHIPS for b in bufs]

    def body(*refs):
        outs = refs[n:2 * n]
        send_sems, recv_sems, fwd_send_sems, fwd_recv_sems = refs[2 * n:]
        x, y, c, chips = _place()
        me = (x, y)

        def ici(j, k, src_chip, to):
            blk = outs[k].at[_chip_rows(src_chip, rows[k]), _cols(c)]
            return pltpu.make_async_remote_copy(src_ref=blk, dst_ref=blk, send_sem=send_sems.at[j, k], recv_sem=recv_sems.at[j, k],
                                                device_id=to, device_id_type=MESH)

        def d2d(j, k, src_chip, half):
            blk = outs[k].at[_chip_rows(src_chip, rows[k]), _cols(half)]
            return pltpu.make_async_remote_copy(src_ref=blk, dst_ref=blk, send_sem=fwd_send_sems.at[j, k],
                                                recv_sem=fwd_recv_sems.at[j, k], device_id=(x, y, 1 - c), device_id_type=MESH)

        sends = [ici(j, k, me, (*chip, c)) for j, chip in enumerate(chips) for k in range(n)]
        for cp in sends:
            cp.start()
        forwards = []
        for j, chip in enumerate(chips):
            for k in range(n):
                ici(j, k, chip, (x, y, c)).wait_recv()
                fw = d2d(j, k, chip, c)
                fw.start()
                forwards.append(fw)
        for j, chip in enumerate(chips):
            for k in range(n):
                d2d(j, k, chip, 1 - c).wait_recv()
        for cp in sends + forwards:
            cp.wait_send()

    any_spec = pl.BlockSpec(memory_space=pl.ANY)
    return pl.pallas_call(
        body, name="gather_weights", in_specs=[any_spec] * n, out_specs=[any_spec] * n,
        out_shape=[jax.ShapeDtypeStruct(b.shape, b.dtype) for b in bufs], input_output_aliases={k: k for k in range(n)},
        scratch_shapes=[pltpu.SemaphoreType.DMA((3, n)), pltpu.SemaphoreType.DMA((3, n)),
                        pltpu.SemaphoreType.DMA((3, n)), pltpu.SemaphoreType.DMA((3, n))])(*bufs)


def _sibling_halves(grads):
    n = len(grads)

    def body(*refs):
        ins, outs = refs[:n], refs[n:2 * n]
        send_sems, recv_sems = refs[2 * n:]
        x, y, c, _ = _place()
        copies = [pltpu.make_async_remote_copy(src_ref=ins[k].at[:, pl.ds(1 - c, 1)], dst_ref=outs[k], send_sem=send_sems.at[k],
                                               recv_sem=recv_sems.at[k], device_id=(x, y, 1 - c), device_id_type=MESH)
                  for k in range(n)]
        for cp in copies:
            cp.start()
        for cp in copies:
            cp.wait()

    any_spec = pl.BlockSpec(memory_space=pl.ANY)
    return pl.pallas_call(
        body, name="reduce_sibling", in_specs=[any_spec] * n, out_specs=[any_spec] * n,
        out_shape=[jax.ShapeDtypeStruct((N_CHIPS, 1) + g.shape[2:], F32) for g in grads],
        scratch_shapes=[pltpu.SemaphoreType.DMA((n,)), pltpu.SemaphoreType.DMA((n,))])(*grads)


def _chip_sum(grad, from_sibling, place, name):
    rh, width = grad.shape[2:]

    def body(place_ref, g_ref, s_ref, own_ref, all_ref):
        total = g_ref[0, 0] + s_ref[0, 0]
        all_ref[0, 0] = total.astype(BF16)

        @pl.when(pl.program_id(0) == place_ref[0])
        def _():
            own_ref[0] = total

    blk = (1, 1, rh, width)
    return pl.pallas_call(
        body, name=name,
        grid_spec=pltpu.PrefetchScalarGridSpec(
            num_scalar_prefetch=1, grid=(N_CHIPS,),
            in_specs=[pl.BlockSpec(blk, lambda p, place: (p, place[1], 0, 0)), pl.BlockSpec(blk, lambda p, place: (p, 0, 0, 0))],
            out_specs=[pl.BlockSpec((1, rh, width), lambda p, place: (0, 0, 0)), pl.BlockSpec(blk, lambda p, place: (p, 0, 0, 0))]),
        out_shape=[jax.ShapeDtypeStruct((1, rh, width), F32), jax.ShapeDtypeStruct((N_CHIPS, 1, rh, width), BF16)],
        compiler_params=_params(dimension_semantics=("arbitrary",)))(place, grad, from_sibling)


def _scatter_chip_sums(sums):
    n = len(sums)

    def body(*refs):
        ins, outs = refs[:n], refs[n:2 * n]
        send_sems, recv_sems = refs[2 * n:]
        x, y, c, chips = _place()
        copies = [pltpu.make_async_remote_copy(src_ref=ins[k].at[pl.ds(2 * chip[0] + chip[1], 1)], dst_ref=outs[k].at[pl.ds(j, 1)],
                                               send_sem=send_sems.at[j, k], recv_sem=recv_sems.at[j, k],
                                               device_id=(*chip, c), device_id_type=MESH)
                  for j, chip in enumerate(chips) for k in range(n)]
        for cp in copies:
            cp.start()
        for cp in copies:
            cp.wait()

    any_spec = pl.BlockSpec(memory_space=pl.ANY)
    return pl.pallas_call(
        body, name="reduce_chips", in_specs=[any_spec] * n, out_specs=[any_spec] * n,
        out_shape=[jax.ShapeDtypeStruct((3,) + sm.shape[1:], BF16) for sm in sums],
        scratch_shapes=[pltpu.SemaphoreType.DMA((3, n)), pltpu.SemaphoreType.DMA((3, n))])(*sums)


def _total_sum(own, received, name):
    def body(o_ref, r_ref, t_ref):
        total = o_ref[0]
        for j in range(3):
            total = total + r_ref[j, 0].astype(F32)
        t_ref[0] = total

    return pl.pallas_call(body, name=name, out_shape=jax.ShapeDtypeStruct(own.shape, F32), compiler_params=_params())(own, received)


def _swap_halves(halves):
    n = len(halves)

    def body(*refs):
        ins, outs = refs[:n], refs[n:2 * n]
        send_sems, recv_sems = refs[2 * n:]
        x, y, c, _ = _place()
        copies = [pltpu.make_async_remote_copy(src_ref=ins[k], dst_ref=outs[k], send_sem=send_sems.at[k],
                                               recv_sem=recv_sems.at[k], device_id=(x, y, 1 - c), device_id_type=MESH)
                  for k in range(n)]
        for cp in copies:
            cp.start()
        for cp in copies:
            cp.wait()

    any_spec = pl.BlockSpec(memory_space=pl.ANY)
    return pl.pallas_call(
        body, name="swap_halves", in_specs=[any_spec] * n, out_specs=[any_spec] * n,
        out_shape=[jax.ShapeDtypeStruct(h.shape, F32) for h in halves],
        scratch_shapes=[pltpu.SemaphoreType.DMA((n,)), pltpu.SemaphoreType.DMA((n,))])(*halves)


N_DEV = 8


def _gather_small(block):
    m_per, width = block.shape

    def body(x_ref, out_ref, send_sems, recv_sems, local_sem):
        x, y, c, chips = _place()
        me, sibling = (x, y, c), (x, y, 1 - c)

        def rows(px, py, pc):
            return out_ref.at[pl.ds((4 * px + 2 * py + pc) * m_per, m_per), :]

        def copy(k, blk, to, src=None):
            return pltpu.make_async_remote_copy(src_ref=rows(*blk) if src is None else src, dst_ref=rows(*blk),
                                                send_sem=send_sems.at[k], recv_sem=recv_sems.at[k], device_id=to, device_id_type=MESH)

        mine = pltpu.make_async_copy(x_ref, rows(*me), local_sem)
        mine.start()
        first = [copy(0, me, sibling, src=x_ref)] + [copy(1 + j, me, (*chip, c), src=x_ref) for j, chip in enumerate(chips)]
        for cp in first:
            cp.start()
        passed = [copy(4 + j, (*chip, c), sibling) for j, chip in enumerate(chips)]
        for j, chip in enumerate(chips):
            copy(1 + j, (*chip, c), me).wait_recv()
            passed[j].start()
        copy(0, sibling, me).wait_recv()
        for j, chip in enumerate(chips):
            copy(4 + j, (*chip, 1 - c), me).wait_recv()
        for cp in first + passed:
            cp.wait_send()
        mine.wait()

    vmem = pl.BlockSpec(memory_space=pltpu.VMEM)
    return pl.pallas_call(body, name="gather_small", out_shape=jax.ShapeDtypeStruct((N_DEV * m_per, width), F32),
                          in_specs=[vmem], out_specs=vmem,
                          scratch_shapes=[pltpu.SemaphoreType.DMA((7,)), pltpu.SemaphoreType.DMA((7,)),
                                          pltpu.SemaphoreType.DMA])(block)


def _adamw_math(w, g, m, v):
    m = ADAM_B1 * m + (1.0 - ADAM_B1) * g
    v = ADAM_B2 * v + (1.0 - ADAM_B2) * (g * g)
    m_hat = m / (1.0 - ADAM_B1 ** ADAM_STEP)
    v_hat = v / (1.0 - ADAM_B2 ** ADAM_STEP)
    delta = -ADAM_LR * (m_hat / (jnp.sqrt(v_hat) + ADAM_EPS) + ADAM_WD * w)
    return delta, m, v


def _adamw(w, mine, siblings, place, m, v, transposed, name):
    def body(place_ref, w_ref, mine_ref, sib_ref, m_ref, v_ref, go_ref, d_ref, mo_ref, vo_ref):
        first = place_ref[1] == 0
        g = jnp.concatenate([jnp.where(first, mine_ref[0], sib_ref[0]), jnp.where(first, sib_ref[0], mine_ref[0])], axis=0)
        g = g.T if transposed else g
        go_ref[...] = g
        d_ref[...], mo_ref[...], vo_ref[...] = _adamw_math(w_ref[...], g, m_ref[...], v_ref[...])

    vmem = pl.BlockSpec(memory_space=pltpu.VMEM)
    return pl.pallas_call(body, name=name, in_specs=[pl.BlockSpec(memory_space=pltpu.SMEM)] + [vmem] * 5, out_specs=[vmem] * 4,
                          out_shape=[jax.ShapeDtypeStruct(w.shape, F32)] * 4, compiler_params=_params())(
                              place, w, mine, siblings, m, v)


def _adamw_small(gathered, w, m, v, name):
    def body(ga_ref, w_ref, m_ref, v_ref, go_ref, d_ref, mo_ref, vo_ref):
        g = ga_ref[0]
        for dev in range(1, N_DEV):
            g = g + ga_ref[dev]
        go_ref[...] = g
        d_ref[...], mo_ref[...], vo_ref[...] = _adamw_math(w_ref[...], g, m_ref[...], v_ref[...])

    return pl.pallas_call(body, name=name, out_shape=[jax.ShapeDtypeStruct(w.shape, F32)] * 4,
                          compiler_params=_params())(gathered, w, m, v)


SMALL = ("g_ffn1_pre", "g_ffn1_post", "g_mix_pre", "w_pool_lin", "pool_scale", "g_mix_post", "g_ffn2_pre", "g_ffn2_post")
WEIGHTS = ("g_ffn1_pre", "w1_gate", "w1_up", "w1_down", "g_ffn1_post", "g_mix_pre", "w_in", "w_pool_lin", "pool_scale", "w_out",
           "g_mix_post", "g_ffn2_pre", "w2_gate", "w2_up", "w2_down", "g_ffn2_post")
LANES = 128


def _pack_small(tree):
    flat = jnp.concatenate([tree[k].reshape(-1) for k in SMALL])
    rows = -(-flat.shape[0] // (8 * LANES)) * 8
    return jnp.pad(flat, (0, rows * LANES - flat.shape[0])).reshape(rows, LANES)


def _unpack_small(packed, like):
    flat, out, at = packed.reshape(-1), {}, 0
    for k in SMALL:
        size = math.prod(like[k].shape)
        out[k] = flat[at:at + size].reshape(like[k].shape)
        at += size
    return out


def kernel(x, g_ffn1_pre, w1_gate, w1_up, w1_down, g_ffn1_post, g_mix_pre, w_in, w_pool_lin, pool_scale, w_out, g_mix_post, g_ffn2_pre, w2_gate, w2_up, w2_down, g_ffn2_post, loss_target, m_g_ffn1_pre, m_w1_gate, m_w1_up, m_w1_down, m_g_ffn1_post, m_g_mix_pre, m_w_in, m_w_pool_lin, m_pool_scale, m_w_out, m_g_mix_post, m_g_ffn2_pre, m_w2_gate, m_w2_up, m_w2_down, m_g_ffn2_post, v_g_ffn1_pre, v_w1_gate, v_w1_up, v_w1_down, v_g_ffn1_post, v_g_mix_pre, v_w_in, v_w_pool_lin, v_pool_scale, v_w_out, v_g_mix_post, v_g_ffn2_pre, v_w2_gate, v_w2_up, v_w2_down, v_g_ffn2_post):
    given = dict(locals())
    w = {k: given[k] for k in WEIGHTS}
    m = {k: given["m_" + k] for k in WEIGHTS}
    v = {k: given["v_" + k] for k in WEIGHTS}
    small = {k: (w[k][0] if k == "w_pool_lin" else w[k].reshape(1, -1)) for k in SMALL}

    place = jnp.stack([2 * lax.axis_index("x") + lax.axis_index("y"), lax.axis_index("c")]).astype(jnp.int32)
    shards = [_cast_shard(w[k][0], k in TRANSPOSED, place, f"cast_{k}") for k in SEGMENTS]
    full = dict(zip(SEGMENTS, _gather_weights(shards)))

    loss_part, grad_x, grads, small_grads = _local_step(x[0], loss_target[0], small, full)
    loss = lax.psum(loss_part, ("x", "y", "c"))

    local = [grads[k].reshape(N_CHIPS, 2, grads[k].shape[0] // (2 * N_CHIPS), grads[k].shape[1]) for k in SEGMENTS]
    from_sibling = _sibling_halves(local)
    sums = [_chip_sum(g, fs, place, f"chip_sum_{k}") for g, fs, k in zip(local, from_sibling, SEGMENTS)]
    received = _scatter_chip_sums([sm[1] for sm in sums])
    halves = [_total_sum(own, rec, f"total_{k}") for (own, _), rec, k in zip(sums, received, SEGMENTS)]
    from_sibling = _swap_halves(halves)

    out_grad, out_delta, out_m, out_v = {}, {}, {}, {}
    for k, mine, sib in zip(SEGMENTS, halves, from_sibling):
        out_grad[k], out_delta[k], out_m[k], out_v[k] = (
            a[None] for a in _adamw(w[k][0], mine, sib, place, m[k][0], v[k][0], k in TRANSPOSED, f"adamw_{k}"))

    small_grads["w_pool_lin"] = small_grads["w_pool_lin"][None]
    packed = _pack_small(small_grads)
    gathered = _gather_small(packed).reshape(N_DEV, *packed.shape)
    like = {k: w[k] for k in SMALL}
    results = _adamw_small(gathered, _pack_small(like), _pack_small({k: m[k] for k in SMALL}),
                           _pack_small({k: v[k] for k in SMALL}), "adamw_small")
    for tree, res in zip((out_grad, out_delta, out_m, out_v), results):
        tree.update(_unpack_small(res, like))

    return (loss, grad_x[None], *[out_grad[k] for k in WEIGHTS], *[out_delta[k] for k in WEIGHTS],
            *[out_m[k] for k in WEIGHTS], *[out_v[k] for k in WEIGHTS])
```

```python
import math
import typing

import numpy as np
import jax
import jax.numpy as jnp
from jax import lax
from jax.experimental import pallas as pl
from jax.experimental.pallas import tpu as pltpu

F32 = jnp.float32
BF16 = jnp.bfloat16
MESH = pl.DeviceIdType.MESH

RMS_EPS = 1e-6
HEAD_DIM = 64
POOL_HALF_WINDOWS = (1, 2, 4, 8)
POOL_DIM = 256
GROUP_DIM = 256
DILATIONS = (1, 4, 16)
N_SIDE = 64
N_ATTN_HEADS = 12
ADAM_LR, ADAM_B1, ADAM_B2, ADAM_EPS, ADAM_WD, ADAM_STEP = 0.001, 0.9, 0.999, 1e-08, 0.01, 10

N_CHIPS = 4
V7X_VMEM_LIMIT = 60 * 1024 * 1024

_NT = (((1,), (1,)), ((), ()))
_TN = (((0,), (0,)), ((), ()))


def _dot(a, b):
    return jnp.dot(a, b, preferred_element_type=F32)


def _dot_nt(a, b):
    return lax.dot_general(a, b, _NT, preferred_element_type=F32)


def _dot_tn(a, b):
    return lax.dot_general(a, b, _TN, preferred_element_type=F32)


def _params(**kw):
    return pltpu.CompilerParams(vmem_limit_bytes=V7X_VMEM_LIMIT, **kw)


def _rows(tm, width):
    return pl.BlockSpec((tm, width), lambda i: (i, 0))


def _resident(shape):
    return pl.BlockSpec(shape, lambda i: (0,) * len(shape), pipeline_mode=pl.Buffered(1))


def _const(shape):
    return pl.BlockSpec(shape, lambda i: (0,) * len(shape))


def _tile(rows, cap):
    return max(t for t in range(16, cap + 1, 16) if rows % t == 0)


def _inv_rms(x):
    return lax.rsqrt(jnp.mean(x * x, axis=-1, keepdims=True) + RMS_EPS)


def _rms_bwd(x, inv, g, dy):
    n = x * inv
    dn = dy * g
    dx = inv * (dn - n * jnp.mean(dn * n, axis=-1, keepdims=True))
    return dx, jnp.sum(dy * n, axis=0, keepdims=True)


def _accumulate(ref, value):
    @pl.when(pl.program_id(0) == 0)
    def _():
        ref[...] = jnp.zeros_like(ref)

    ref[...] += value


class _Rider(typing.NamedTuple):
    operands: list
    landing: typing.Optional[list]
    sems: tuple
    start: typing.Callable
    wait: typing.Callable


def _hosted_call(body, rider, *, name, steps, in_specs, out_specs, out_shape, args, scratch_shapes=()):
    params = _params(dimension_semantics=("arbitrary",))
    if rider is None:
        res = pl.pallas_call(body, name=name, grid=(steps,), in_specs=in_specs, out_specs=out_specs, out_shape=out_shape,
                             scratch_shapes=list(scratch_shapes), compiler_params=params)(*args)
        return list(res), []
    n_in, n_out, n_scratch, r_in = len(in_specs), len(out_specs), len(scratch_shapes), len(rider.operands)
    landing = rider.landing if rider.landing is not None else [jax.ShapeDtypeStruct(a.shape, a.dtype) for a in rider.operands]
    aliases = {n_in + i: n_out + i for i in range(r_in)} if rider.landing is None else {}

    def riding(*refs):
        rider_in = refs[n_in:n_in + r_in]
        outs_at = n_in + r_in
        rider_out = refs[outs_at + n_out:outs_at + n_out + len(landing)]
        scratch_at = outs_at + n_out + len(landing)
        send_sems, recv_sems = refs[scratch_at + n_scratch:]

        @pl.when(pl.program_id(0) == 0)
        def _():
            rider.start(rider_in, rider_out, send_sems, recv_sems)

        body(*refs[:n_in], *refs[outs_at:outs_at + n_out], *refs[scratch_at:scratch_at + n_scratch])

        @pl.when(pl.program_id(0) == steps - 1)
        def _():
            rider.wait(rider_in, rider_out, send_sems, recv_sems)

    any_spec = pl.BlockSpec(memory_space=pl.ANY)
    res = pl.pallas_call(
        riding, name=name, grid=(steps,), in_specs=list(in_specs) + [any_spec] * r_in,
        out_specs=list(out_specs) + [any_spec] * len(landing), out_shape=list(out_shape) + landing,
        scratch_shapes=list(scratch_shapes) + [pltpu.SemaphoreType.DMA(rider.sems)] * 2,
        input_output_aliases=aliases, compiler_params=params)(*args, *rider.operands)
    return list(res[:n_out]), list(res[n_out:])


def _ffn_fwd(x, g_pre, wg_t, wu_t, wd, g_post, target, name, rider=None, tm=256):
    s, d = x.shape
    ff = wd.shape[0]
    with_loss = target is not None

    def body(*refs):
        if with_loss:
            x_ref, gpre_ref, wg_ref, wu_ref, wd_ref, gpost_ref, t_ref, xo_ref, a_ref, b_ref, f_ref, loss_ref = refs
        else:
            x_ref, gpre_ref, wg_ref, wu_ref, wd_ref, gpost_ref, xo_ref, a_ref, b_ref, f_ref = refs
        xv = x_ref[...]
        hb = (xv * _inv_rms(xv) * gpre_ref[...]).astype(BF16)
        a = _dot_nt(hb, wg_ref[...])
        b = _dot_nt(hb, wu_ref[...])
        hh = (a * jax.nn.sigmoid(a)) * b
        f = _dot(hh.astype(BF16), wd_ref[...])
        xo = xv + 0.5 * (f * _inv_rms(f) * gpost_ref[...])
        a_ref[...] = a.astype(BF16)
        b_ref[...] = b.astype(BF16)
        f_ref[...] = f
        if with_loss:
            e = xo - t_ref[...]
            xo_ref[...] = e * (1.0 / d)
            _accumulate(loss_ref, 0.5 * jnp.sum(jnp.mean(e * e, axis=-1, keepdims=True)))
        else:
            xo_ref[...] = xo

    in_specs = [_rows(tm, d), _const((1, d)), _resident((ff, d)), _resident((ff, d)), _resident((ff, d)), _const((1, d))]
    args = [x, g_pre, wg_t, wu_t, wd, g_post]
    out_shape = [jax.ShapeDtypeStruct((s, d), F32), jax.ShapeDtypeStruct((s, ff), BF16),
                 jax.ShapeDtypeStruct((s, ff), BF16), jax.ShapeDtypeStruct((s, d), F32)]
    out_specs = [_rows(tm, d), _rows(tm, ff), _rows(tm, ff), _rows(tm, d)]
    if with_loss:
        in_specs.append(_rows(tm, d))
        args.append(target)
        out_shape.append(jax.ShapeDtypeStruct((8, 128), F32))
        out_specs.append(_const((8, 128)))
    return _hosted_call(body, rider, name=name, steps=s // tm, in_specs=in_specs, out_specs=out_specs, out_shape=out_shape, args=args)


def _ffn_bwd(dxo, x, f, a, b, g_pre, g_post, wg_t, wu_t, wd, name, rider=None, tm=256):
    s, d = x.shape
    ff = wd.shape[0]

    def body(dxo_ref, x_ref, f_ref, a_ref, b_ref, gpre_ref, gpost_ref, wg_ref, wu_ref, wd_ref,
             dx_ref, hh_ref, da_ref, db_ref, df_ref, h_ref, dgpre_ref, dgpost_ref):
        dxo_v = dxo_ref[...]
        fv = f_ref[...]
        df, dgpost = _rms_bwd(fv, _inv_rms(fv), gpost_ref[...], 0.5 * dxo_v)
        dfb = df.astype(BF16)
        dhh = _dot_nt(dfb, wd_ref[...])
        av = a_ref[...].astype(F32)
        bv = b_ref[...].astype(F32)
        sig = jax.nn.sigmoid(av)
        sa = av * sig
        da = (dhh * bv * (sig * (1.0 + av * (1.0 - sig)))).astype(BF16)
        db = (dhh * sa).astype(BF16)
        dh = _dot(da, wg_ref[...]) + _dot(db, wu_ref[...])
        xv = x_ref[...]
        inv = _inv_rms(xv)
        dxn, dgpre = _rms_bwd(xv, inv, gpre_ref[...], dh)
        dx_ref[...] = dxo_v + dxn
        hh_ref[...] = (sa * bv).astype(BF16)
        da_ref[...] = da
        db_ref[...] = db
        df_ref[...] = dfb
        h_ref[...] = (xv * inv * gpre_ref[...]).astype(BF16)
        _accumulate(dgpre_ref, dgpre)
        _accumulate(dgpost_ref, dgpost)

    return _hosted_call(
        body, rider, name=name, steps=s // tm,
        in_specs=[_rows(tm, d), _rows(tm, d), _rows(tm, d), _rows(tm, ff), _rows(tm, ff), _const((1, d)), _const((1, d)),
                  _resident((ff, d)), _resident((ff, d)), _resident((ff, d))],
        out_specs=[_rows(tm, d), _rows(tm, ff), _rows(tm, ff), _rows(tm, ff), _rows(tm, d), _rows(tm, d),
                   _const((1, d)), _const((1, d))],
        out_shape=[jax.ShapeDtypeStruct((s, d), F32), jax.ShapeDtypeStruct((s, ff), BF16), jax.ShapeDtypeStruct((s, ff), BF16),
                   jax.ShapeDtypeStruct((s, ff), BF16), jax.ShapeDtypeStruct((s, d), BF16), jax.ShapeDtypeStruct((s, d), BF16),
                   jax.ShapeDtypeStruct((1, d), F32), jax.ShapeDtypeStruct((1, d), F32)],
        args=[dxo, x, f, a, b, g_pre, g_post, wg_t, wu_t, wd])


def _wgrad(lhs, rhs, name, rider=None, tk=512):
    s, r = lhs.shape
    c = rhs.shape[1]

    def body(l_ref, r_ref, o_ref):
        _accumulate(o_ref, _dot_tn(l_ref[...], r_ref[...]))

    (out,), riding = _hosted_call(body, rider, name=name, steps=s // tk, in_specs=[_rows(tk, r), _rows(tk, c)],
                                  out_specs=[_const((r, c))], out_shape=[jax.ShapeDtypeStruct((r, c), F32)], args=[lhs, rhs])
    return out, riding


def _in_fwd(x, g, w_in_t, name, tm=512):
    s, d = x.shape
    d_in = w_in_t.shape[0]
    n_parts = (d_in - POOL_DIM) // GROUP_DIM

    def body(x_ref, g_ref, w_ref, u_ref, *part_refs):
        xv = x_ref[...]
        hb = (xv * _inv_rms(xv) * g_ref[...]).astype(BF16)
        z = _dot_nt(hb, w_ref[...])
        u_ref[...] = z[:, :POOL_DIM]
        for j, ref in enumerate(part_refs):
            ref[...] = z[:, POOL_DIM + GROUP_DIM * j:POOL_DIM + GROUP_DIM * (j + 1)]

    return pl.pallas_call(
        body, name=name, grid=(s // tm,), in_specs=[_rows(tm, d), _const((1, d)), _resident((d_in, d))],
        out_specs=[_rows(tm, POOL_DIM)] + [_rows(tm, GROUP_DIM)] * n_parts,
        out_shape=[jax.ShapeDtypeStruct((s, POOL_DIM), F32)] + [jax.ShapeDtypeStruct((s, GROUP_DIM), F32)] * n_parts,
        compiler_params=_params(dimension_semantics=("arbitrary",)))(x, g, w_in_t)


def _in_bwd(du, dparts, x, dxo, g, w_in_t, name, tm=512):
    s, d = x.shape
    d_in = w_in_t.shape[0]
    n_parts = len(dparts)

    def body(du_ref, *refs):
        part_refs = refs[:n_parts]
        x_ref, dxo_ref, g_ref, w_ref, dx_ref, dz_ref, h_ref, dg_ref = refs[n_parts:]
        dz = jnp.concatenate([r[...].astype(BF16) for r in (du_ref,) + part_refs], axis=1)
        dz_ref[...] = dz
        dh = _dot(dz, w_ref[...])
        xv = x_ref[...]
        inv = _inv_rms(xv)
        dxn, dg = _rms_bwd(xv, inv, g_ref[...], dh)
        dx_ref[...] = dxo_ref[...] + dxn
        h_ref[...] = (xv * inv * g_ref[...]).astype(BF16)
        _accumulate(dg_ref, dg)

    return pl.pallas_call(
        body, name=name, grid=(s // tm,),
        in_specs=[_rows(tm, POOL_DIM)] + [_rows(tm, GROUP_DIM)] * n_parts + [_rows(tm, d), _rows(tm, d), _const((1, d)),
                                                                             _resident((d_in, d))],
        out_specs=[_rows(tm, d), _rows(tm, d_in), _rows(tm, d), _const((1, d))],
        out_shape=[jax.ShapeDtypeStruct((s, d), F32), jax.ShapeDtypeStruct((s, d_in), BF16), jax.ShapeDtypeStruct((s, d), BF16),
                   jax.ShapeDtypeStruct((1, d), F32)],
        compiler_params=_params(dimension_semantics=("arbitrary",)))(du, *dparts, x, dxo, g, w_in_t)


_POOL_HALO = 8


def _pool_chain(v, first_shift):
    n = v.shape[0]
    p2 = v + pltpu.roll(v, first_shift, 0)
    p4 = pltpu.roll(p2, 1, 0) + pltpu.roll(p2, n - 1, 0)
    p8 = pltpu.roll(p4, 2, 0) + pltpu.roll(p4, n - 2, 0)
    p16 = pltpu.roll(p8, 4, 0) + pltpu.roll(p8, n - 4, 0)
    group = lax.broadcasted_iota(jnp.int32, v.shape, 1) // HEAD_DIM
    return jnp.where(group == 0, p2, jnp.where(group == 1, p4, jnp.where(group == 2, p8, p16)))


def _pool_count(t0, rows, s):
    t = t0 + lax.broadcasted_iota(jnp.int32, (rows, POOL_DIM), 0)
    group = lax.broadcasted_iota(jnp.int32, (rows, POOL_DIM), 1) // HEAD_DIM
    half = jnp.where(group == 0, 1, jnp.where(group == 1, 2, jnp.where(group == 2, 4, 8)))
    cnt = jnp.minimum(t + half, s) - jnp.maximum(t - half, 0)
    return jnp.maximum(cnt, 1).astype(F32)


def _pad_rows(ref, pad_ref, s):
    zeros = jnp.zeros((_POOL_HALO, pad_ref.shape[1]), pad_ref.dtype)
    pad_ref[pl.ds(0, _POOL_HALO), :] = zeros
    pad_ref[pl.ds(_POOL_HALO + s, _POOL_HALO), :] = zeros
    pad_ref[pl.ds(_POOL_HALO, s), :] = ref[...]


def _pool_fwd(u, w_bd, scale, name, tm=512):
    s = u.shape[0]
    ext = tm + 2 * _POOL_HALO

    def body(u_ref, w_ref, sc_ref, o_ref, upad):
        _pad_rows(u_ref, upad, s)

        def tile(i, carry):
            t0 = pl.multiple_of(i * tm, tm)
            uv = upad[pl.ds(t0, ext), :]
            win = _pool_chain(uv, 1)[_POOL_HALO:_POOL_HALO + tm]
            y = win / _pool_count(t0, tm, s) - uv[_POOL_HALO:_POOL_HALO + tm]
            o_ref[pl.ds(t0, tm), :] = (_dot(y.astype(BF16), w_ref[...]) * sc_ref[...]).astype(BF16)
            return carry

        lax.fori_loop(0, s // tm, tile, 0)

    return pl.pallas_call(body, name=name, out_shape=jax.ShapeDtypeStruct((s, POOL_DIM), BF16),
                          scratch_shapes=[pltpu.VMEM((s + 2 * _POOL_HALO, POOL_DIM), F32)],
                          compiler_params=_params())(u, w_bd, scale)


def _pool_bwd(u, da, w_bd, scale, name, tm=512):
    s = u.shape[0]
    ext = tm + 2 * _POOL_HALO

    def body(u_ref, da_ref, w_ref, sc_ref, du_ref, dw_ref, dsc_ref, upad, dapad):
        _pad_rows(u_ref, upad, s)
        _pad_rows(da_ref, dapad, s)
        dw_ref[...] = jnp.zeros_like(dw_ref)
        dsc_ref[...] = jnp.zeros_like(dsc_ref)

        def tile(i, carry):
            t0 = pl.multiple_of(i * tm, tm)
            uv = upad[pl.ds(t0, ext), :]
            dav = dapad[pl.ds(t0, ext), :]
            win = _pool_chain(uv, 1)[_POOL_HALO:_POOL_HALO + tm]
            yb = (win / _pool_count(t0, tm, s) - uv[_POOL_HALO:_POOL_HALO + tm]).astype(BF16)
            yl = _dot(yb, w_ref[...])
            da_c = dav[_POOL_HALO:_POOL_HALO + tm]
            dsc_ref[...] += jnp.sum(da_c * yl, axis=0, keepdims=True)
            dyl = (dav * sc_ref[...]).astype(BF16)
            dw_ref[...] += _dot_tn(yb, dyl[_POOL_HALO:_POOL_HALO + tm])
            dy = _dot_nt(dyl, w_ref[...])
            dyc = dy / _pool_count(t0 - _POOL_HALO, ext, s)
            du_ref[pl.ds(t0, tm), :] = (_pool_chain(dyc, ext - 1) - dy)[_POOL_HALO:_POOL_HALO + tm]
            return carry

        lax.fori_loop(0, s // tm, tile, 0)

    pool_cols = pl.BlockSpec((s, POOL_DIM), lambda i: (0, 0), pipeline_mode=pl.Buffered(1))
    return pl.pallas_call(
        body, name=name, grid=(1,),
        in_specs=[pool_cols, pool_cols, _const((POOL_DIM, POOL_DIM)), _const((1, POOL_DIM))],
        out_specs=[_const((s, POOL_DIM)), _const((POOL_DIM, POOL_DIM)), _const((1, POOL_DIM))],
        out_shape=[jax.ShapeDtypeStruct((s, POOL_DIM), F32), jax.ShapeDtypeStruct((POOL_DIM, POOL_DIM), F32),
                   jax.ShapeDtypeStruct((1, POOL_DIM), F32)],
        scratch_shapes=[pltpu.VMEM((s + 2 * _POOL_HALO, POOL_DIM), F32), pltpu.VMEM((s + 2 * _POOL_HALO, POOL_DIM), F32)],
        compiler_params=_params(dimension_semantics=("arbitrary",)))(u, da, w_bd, scale)


_BQ = 128
_KW = _BQ + 2 * N_SIDE
_PAIR = 2 * HEAD_DIM
_NEG = -1e30
_ATTN_UNROLL = 4
_SCORE_SCALE = HEAD_DIM ** -0.5


def _stack_heads(x):
    lane_head = lax.broadcasted_iota(jnp.int32, x.shape, 1) // HEAD_DIM
    zero = jnp.zeros_like(x)
    return jnp.concatenate([jnp.where(lane_head == 0, x, zero), jnp.where(lane_head == 1, x, zero)], axis=0)


def _unstack_heads(x):
    lane_head = lax.broadcasted_iota(jnp.int32, (_BQ, _PAIR), 1) // HEAD_DIM
    return jnp.where(lane_head == 0, x[:_BQ], x[_BQ:])


def _stack_cols(x):
    return jnp.concatenate([x[:, 0:1], x[:, HEAD_DIM:HEAD_DIM + 1]], axis=0)


def _fill_bias(bias_ref, slopes_ref, dilation):
    row = lax.broadcasted_iota(jnp.int32, (2 * _BQ, _KW), 0)
    col = lax.broadcasted_iota(jnp.int32, (2 * _BQ, _KW), 1)
    pair = 2 * pl.program_id(0)
    slope = jnp.where(row < _BQ, slopes_ref[pair], slopes_ref[pair + 1]) * float(dilation)
    for j in range(3):
        dist = jnp.abs(col - (row & (_BQ - 1)) - j * N_SIDE)
        bias_ref[j] = jnp.where(dist <= N_SIDE, -slope * dist.astype(F32), _NEG)


def _block_window(i, n_blocks, length):
    q0 = pl.multiple_of(i * _BQ, _BQ)
    ws = pl.multiple_of(jnp.clip(q0 - N_SIDE, 0, length - _KW), N_SIDE)
    return q0, ws, jnp.where(i == 0, 0, jnp.where(i == n_blocks - 1, 2, 1))


def _residue_rows(dilation, start, count):
    if dilation == 1:
        return pl.ds(start, count)
    return pl.ds(start * dilation + pl.program_id(1), count, stride=dilation)


def _attn_call(body, name, dilation, seq, n_in, n_out, scratch):
    col = pl.BlockSpec((seq, _PAIR), lambda c, r: (0, c), pipeline_mode=pl.Buffered(1))
    return pl.pallas_call(
        body, name=name, grid=(GROUP_DIM // _PAIR, dilation),
        in_specs=[pl.BlockSpec(memory_space=pltpu.SMEM)] + [col] * n_in, out_specs=[col] * n_out,
        out_shape=[jax.ShapeDtypeStruct((seq, GROUP_DIM), F32)] * n_out, scratch_shapes=scratch,
        compiler_params=_params(dimension_semantics=("arbitrary", "arbitrary")))


def _attn_fwd(q, k, v, slopes, dilation, name):
    seq = q.shape[0]
    length = seq // dilation
    n_blocks = length // _BQ

    def body(sl_ref, q_ref, k_ref, v_ref, o_ref, lse_ref, qs, ks, vs, bias_ref):
        all_rows = _residue_rows(dilation, 0, length)
        qs[...] = (q_ref[all_rows, :] * _SCORE_SCALE).astype(BF16)
        ks[...] = k_ref[all_rows, :].astype(BF16)
        vs[...] = v_ref[all_rows, :].astype(BF16)
        _fill_bias(bias_ref, sl_ref, dilation)

        def block(i, carry):
            q0, ws, which = _block_window(i, n_blocks, length)
            kw = ks[pl.ds(ws, _KW), :]
            vw = vs[pl.ds(ws, _KW), :]
            sc = _dot_nt(_stack_heads(qs[pl.ds(q0, _BQ), :]), kw) + bias_ref[which]
            m = jnp.max(sc, axis=-1, keepdims=True)
            p = jnp.exp(sc - m)
            den = jnp.sum(p, axis=-1, keepdims=True)
            rows = _residue_rows(dilation, q0, _BQ)
            o_ref[rows, :] = _unstack_heads(_dot(p.astype(BF16), vw) / den)
            lse_ref[rows, :] = _unstack_heads(jnp.broadcast_to(m + jnp.log(den), (2 * _BQ, _PAIR)))
            return carry

        lax.fori_loop(0, n_blocks, block, 0, unroll=min(_ATTN_UNROLL, n_blocks))

    stage = pltpu.VMEM((length, _PAIR), BF16)
    bias = pltpu.VMEM((3, 2 * _BQ, _KW), F32)
    return _attn_call(body, name, dilation, seq, 3, 2, [stage] * 3 + [bias])(slopes, q, k, v)


def _attn_bwd(q, k, v, do, lse, cterm, slopes, dilation, name):
    seq = q.shape[0]
    length = seq // dilation
    n_blocks = length // _BQ

    def body(sl_ref, q_ref, k_ref, v_ref, do_ref, lse_ref, c_ref, dq_ref, dk_ref, dv_ref, qs, ks, vs, dos, dk_acc, dv_acc, bias_ref):
        all_rows = _residue_rows(dilation, 0, length)
        qs[...] = (q_ref[all_rows, :] * _SCORE_SCALE).astype(BF16)
        for src, dst in ((k_ref, ks), (v_ref, vs), (do_ref, dos)):
            dst[...] = src[all_rows, :].astype(BF16)
        dk_acc[...] = jnp.zeros_like(dk_acc)
        dv_acc[...] = jnp.zeros_like(dv_acc)
        _fill_bias(bias_ref, sl_ref, dilation)

        def block(i, carry):
            q0, ws, which = _block_window(i, n_blocks, length)
            rows = _residue_rows(dilation, q0, _BQ)
            qm = _stack_heads(qs[pl.ds(q0, _BQ), :])
            dom = _stack_heads(dos[pl.ds(q0, _BQ), :])
            kw = ks[pl.ds(ws, _KW), :]
            vw = vs[pl.ds(ws, _KW), :]
            p = jnp.exp(_dot_nt(qm, kw) + bias_ref[which] - _stack_cols(lse_ref[rows, :]))
            ds = (p * (_dot_nt(dom, vw) + _stack_cols(c_ref[rows, :]))).astype(BF16)
            dq_ref[rows, :] = _unstack_heads(_dot(ds, kw)) * _SCORE_SCALE
            dk_acc[pl.ds(ws, _KW), :] += _dot_tn(ds, qm)
            dv_acc[pl.ds(ws, _KW), :] += _dot_tn(p.astype(BF16), dom)
            return carry

        lax.fori_loop(0, n_blocks, block, 0, unroll=min(_ATTN_UNROLL, n_blocks))
        dk_ref[all_rows, :] = dk_acc[...]
        dv_ref[all_rows, :] = dv_acc[...]

    stage = pltpu.VMEM((length, _PAIR), BF16)
    acc = pltpu.VMEM((length, _PAIR), F32)
    bias = pltpu.VMEM((3, 2 * _BQ, _KW), F32)
    return _attn_call(body, name, dilation, seq, 6, 3, [stage] * 4 + [acc] * 2 + [bias])(slopes, q, k, v, do, lse, cterm)


def _group_weights(lses):
    m = jnp.maximum(jnp.maximum(lses[0], lses[1]), lses[2])
    es = [jnp.exp(l - m) for l in lses]
    den = es[0] + es[1] + es[2]
    return [e / den for e in es]


def _combine_fwd(a_pool, outs, lses, name, tm=512):
    s = a_pool.shape[0]

    def body(ap_ref, o0, o1, o2, l0, l1, l2, cat_ref):
        alphas = _group_weights([l0[...], l1[...], l2[...]])
        parts = [ap_ref[...]] + [(o[...] * al).astype(BF16) for o, al in zip((o0, o1, o2), alphas)]
        cat_ref[...] = jnp.concatenate(parts, axis=1)

    width = POOL_DIM + 3 * GROUP_DIM
    return pl.pallas_call(body, name=name, grid=(s // tm,), in_specs=[_rows(tm, POOL_DIM)] + [_rows(tm, GROUP_DIM)] * 6,
                          out_specs=_rows(tm, width), out_shape=jax.ShapeDtypeStruct((s, width), BF16),
                          compiler_params=_params(dimension_semantics=("arbitrary",)))(a_pool, *outs, *lses)


def _combine_bwd(dcat, outs, lses, head_ones, name, tm=512):
    s = dcat.shape[0]

    def body(dc_ref, o0, o1, o2, l0, l1, l2, ones_ref, do0, do1, do2, c0, c1, c2):
        alphas = _group_weights([l0[...], l1[...], l2[...]])
        dcat_v = dc_ref[...]
        das = [dcat_v[:, POOL_DIM + GROUP_DIM * g:POOL_DIM + GROUP_DIM * (g + 1)] for g in range(3)]
        prod = sum(da * (o[...] * al) for da, o, al in zip(das, (o0, o1, o2), alphas))
        hi = prod.astype(BF16)
        lo = (prod - hi.astype(F32)).astype(BF16)
        total = _dot(hi, ones_ref[...]) + _dot(lo, ones_ref[...])
        for da, al, do_ref, c_ref in zip(das, alphas, (do0, do1, do2), (c0, c1, c2)):
            do_ref[...] = da * al
            c_ref[...] = -al * total

    width = POOL_DIM + 3 * GROUP_DIM
    return pl.pallas_call(
        body, name=name, grid=(s // tm,),
        in_specs=[_rows(tm, width)] + [_rows(tm, GROUP_DIM)] * 6 + [_const((GROUP_DIM, GROUP_DIM))],
        out_specs=[_rows(tm, GROUP_DIM)] * 6,
        out_shape=[jax.ShapeDtypeStruct((s, GROUP_DIM), F32)] * 6,
        compiler_params=_params(dimension_semantics=("arbitrary",)))(dcat, *outs, *lses, head_ones)


def _out_fwd(cat, x, w_out, g, name, tm=512):
    s, d = x.shape

    def body(cat_ref, x_ref, w_ref, g_ref, xo_ref, mix_ref):
        mix = _dot(cat_ref[...], w_ref[...])
        mix_ref[...] = mix
        xo_ref[...] = x_ref[...] + mix * _inv_rms(mix) * g_ref[...]

    return pl.pallas_call(body, name=name, grid=(s // tm,),
                          in_specs=[_rows(tm, cat.shape[1]), _rows(tm, d), _resident(w_out.shape), _const((1, d))],
                          out_specs=[_rows(tm, d), _rows(tm, d)], out_shape=[jax.ShapeDtypeStruct((s, d), F32)] * 2,
                          compiler_params=_params(dimension_semantics=("arbitrary",)))(cat, x, w_out, g)


def _out_bwd(dxo, mix, w_out, g, name, tm=512):
    s, d = mix.shape
    width = w_out.shape[0]

    def body(dxo_ref, mix_ref, w_ref, g_ref, dcat_ref, dmix_ref, dg_ref):
        mv = mix_ref[...]
        dmix, dg = _rms_bwd(mv, _inv_rms(mv), g_ref[...], dxo_ref[...])
        dmb = dmix.astype(BF16)
        dmix_ref[...] = dmb
        dcat_ref[...] = _dot_nt(dmb, w_ref[...])
        _accumulate(dg_ref, dg)

    return pl.pallas_call(
        body, name=name, grid=(s // tm,), in_specs=[_rows(tm, d), _rows(tm, d), _resident(w_out.shape), _const((1, d))],
        out_specs=[_rows(tm, width), _rows(tm, d), _const((1, d))],
        out_shape=[jax.ShapeDtypeStruct((s, width), F32), jax.ShapeDtypeStruct((s, d), BF16), jax.ShapeDtypeStruct((1, d), F32)],
        compiler_params=_params(dimension_semantics=("arbitrary",)))(dxo, mix, w_out, g)


def _alibi_slopes():
    return np.array([2.0 ** (-8.0 * (i + 1) / N_ATTN_HEADS) for i in range(N_ATTN_HEADS)], np.float32)


def _block_diag(w_lin):
    n, c, _ = w_lin.shape
    eye = jnp.eye(n, dtype=w_lin.dtype)
    return (eye[:, None, :, None] * w_lin[:, :, None, :]).reshape(n * c, n * c)


class _NoExchange:
    def __init__(self, full):
        self.full, self.grads = full, {}

    def first_weights(self):
        return self.full

    def rider(self, host):
        return None

    def landed(self, host, results):
        pass

    def rest_weights(self):
        return self.full

    def gradient(self, name, grad):
        self.grads[name] = grad


def _local_step(x, target, small, exchange):
    s, d = x.shape
    slopes = _alibi_slopes()
    group_slopes = [jnp.asarray(slopes[4 * g:4 * g + 4]) for g in range(3)]
    w_bd = _block_diag(small["w_pool_lin"]).astype(BF16)
    head_ones = jnp.asarray(np.kron(np.eye(GROUP_DIM // HEAD_DIM), np.ones((HEAD_DIM, HEAD_DIM))), BF16)

    full = exchange.first_weights()
    (x1, a1, b1, f1), riding = _ffn_fwd(x, small["g_ffn1_pre"], full["w1_gate"], full["w1_up"], full["w1_down"],
                                        small["g_ffn1_post"], None, "ffn1_fwd", exchange.rider("ffn1_fwd"))
    exchange.landed("ffn1_fwd", riding)
    full = {**full, **exchange.rest_weights()}
    u, *parts = _in_fwd(x1, small["g_mix_pre"], full["w_in"], "in_fwd")
    qs, ks, vs = parts[0:3], parts[3:6], parts[6:9]
    a_pool = _pool_fwd(u, w_bd, small["pool_scale"], "pool_fwd")
    outs, lses = [], []
    for g, dil in enumerate(DILATIONS):
        o, lse = _attn_fwd(qs[g], ks[g], vs[g], group_slopes[g], dil, f"attn_fwd{g}")
        outs.append(o)
        lses.append(lse)
    cat = _combine_fwd(a_pool, outs, lses, "combine_fwd")
    x2, mix = _out_fwd(cat, x1, full["w_out"], small["g_mix_post"], "out_fwd")
    (dx3, a2, b2, f2, loss_part), _ = _ffn_fwd(x2, small["g_ffn2_pre"], full["w2_gate"], full["w2_up"], full["w2_down"],
                                               small["g_ffn2_post"], target, "ffn2_fwd")

    small_grads = {}

    def hosted(call, host, *args):
        results, riding = call(*args, host, exchange.rider(host))
        exchange.landed(host, riding)
        return results

    def ffn_backward(tag, dxo, x_in, f, a, b):
        n = tag[-1]
        dx, hh, da, db, df, h, dg_pre, dg_post = hosted(
            _ffn_bwd, f"{tag}_bwd", dxo, x_in, f, a, b, small[f"g_{tag}_pre"], small[f"g_{tag}_post"],
            full[f"w{n}_gate"], full[f"w{n}_up"], full[f"w{n}_down"])
        for part, lhs, rhs in (("down", hh, df), ("gate", da, h), ("up", db, h)):
            exchange.gradient(f"w{n}_{part}", hosted(_wgrad, f"{tag}_wgrad_{part}", lhs, rhs))
        small_grads[f"g_{tag}_pre"], small_grads[f"g_{tag}_post"] = dg_pre, dg_post
        return dx

    dx2 = ffn_backward("ffn2", dx3, x2, f2, a2, b2)
    dcat, dmix, small_grads["g_mix_post"] = _out_bwd(dx2, mix, full["w_out"], small["g_mix_post"], "out_bwd")
    exchange.gradient("w_out", hosted(_wgrad, "wgrad_out", cat, dmix))
    dos_cs = _combine_bwd(dcat, outs, lses, head_ones, "combine_bwd")
    dos, cs = dos_cs[:3], dos_cs[3:]
    dqs, dks, dvs = [], [], []
    for g, dil in enumerate(DILATIONS):
        dq, dk, dv = _attn_bwd(qs[g], ks[g], vs[g], dos[g], lses[g], cs[g], group_slopes[g], dil, f"attn_bwd{g}")
        dqs.append(dq)
        dks.append(dk)
        dvs.append(dv)
    du, dw_bd, small_grads["pool_scale"] = _pool_bwd(u, dcat, w_bd, small["pool_scale"], "pool_bwd")
    n_pool = len(POOL_HALF_WINDOWS)
    small_grads["w_pool_lin"] = jnp.stack(
        [dw_bd[HEAD_DIM * g:HEAD_DIM * (g + 1), HEAD_DIM * g:HEAD_DIM * (g + 1)] for g in range(n_pool)])
    dx1, dz, h2, small_grads["g_mix_pre"] = _in_bwd(du, dqs + dks + dvs, x1, dx2, small["g_mix_pre"], full["w_in"], "in_bwd")
    exchange.gradient("w_in", hosted(_wgrad, "wgrad_in", dz, h2))
    dx0 = ffn_backward("ffn1", dx1, x, f1, a1, b1)
    return loss_part[0, 0], dx0, small_grads


SEGMENTS = ("w1_gate", "w1_up", "w1_down", "w_in", "w_out", "w2_gate", "w2_up", "w2_down")
TRANSPOSED = ("w1_gate", "w1_up", "w_in", "w2_gate", "w2_up")
HALF = 512


def _place():
    x, y, c = lax.axis_index("x"), lax.axis_index("y"), lax.axis_index("c")
    other_chips = [(1 - x, y), (x, 1 - y), (1 - x, 1 - y)]
    return x, y, c, other_chips


def _chip_rows(chip, rows):
    return pl.ds(pl.multiple_of((2 * chip[0] + chip[1]) * rows, 16), rows)


def _cols(c):
    return pl.ds(pl.multiple_of(c * HALF, HALF), HALF)


def _cast_shard(w, transpose, place, name, tm=256):
    r, c = w.shape
    if transpose:
        def body(place_ref, w_ref, o_ref):
            o_ref[...] = w_ref[...].T.astype(BF16)

        grid, in_block, out_block, rows = (r // tm,), (tm, c), (c, tm), c
    else:
        def body(place_ref, w_ref, o_ref):
            o_ref[...] = w_ref[...].astype(BF16)

        grid, in_block, out_block, rows = (1,), (r, c), (r, c), r
    return pl.pallas_call(
        body, name=name,
        grid_spec=pltpu.PrefetchScalarGridSpec(
            num_scalar_prefetch=1, grid=grid, in_specs=[pl.BlockSpec(in_block, lambda i, place: (i, 0))],
            out_specs=pl.BlockSpec(out_block, lambda i, place: (place[0], i))),
        out_shape=jax.ShapeDtypeStruct((N_CHIPS * rows, 1024), BF16),
        compiler_params=_params(dimension_semantics=("arbitrary",)))(place, w)


def _gather_weights(bufs):
    n = len(bufs)
    rows = [b.shape[0] // N_CHIPS for b in bufs]

    def body(*refs):
        outs = refs[n:2 * n]
        send_sems, recv_sems, fwd_send_sems, fwd_recv_sems = refs[2 * n:]
        x, y, c, chips = _place()
        me = (x, y)

        def ici(j, k, src_chip, to):
            blk = outs[k].at[_chip_rows(src_chip, rows[k]), _cols(c)]
            return pltpu.make_async_remote_copy(src_ref=blk, dst_ref=blk, send_sem=send_sems.at[j, k], recv_sem=recv_sems.at[j, k],
                                                device_id=to, device_id_type=MESH)

        def d2d(j, k, src_chip, half):
            blk = outs[k].at[_chip_rows(src_chip, rows[k]), _cols(half)]
            return pltpu.make_async_remote_copy(src_ref=blk, dst_ref=blk, send_sem=fwd_send_sems.at[j, k],
                                                recv_sem=fwd_recv_sems.at[j, k], device_id=(x, y, 1 - c), device_id_type=MESH)

        sends = [ici(j, k, me, (*chip, c)) for j, chip in enumerate(chips) for k in range(n)]
        for cp in sends:
            cp.start()
        forwards = []
        for j, chip in enumerate(chips):
            for k in range(n):
                ici(j, k, chip, (x, y, c)).wait_recv()
                fw = d2d(j, k, chip, c)
                fw.start()
                forwards.append(fw)
        for j, chip in enumerate(chips):
            for k in range(n):
                d2d(j, k, chip, 1 - c).wait_recv()
        for cp in sends + forwards:
            cp.wait_send()

    any_spec = pl.BlockSpec(memory_space=pl.ANY)
    return pl.pallas_call(
        body, name="gather_weights", in_specs=[any_spec] * n, out_specs=[any_spec] * n,
        out_shape=[jax.ShapeDtypeStruct(b.shape, b.dtype) for b in bufs], input_output_aliases={k: k for k in range(n)},
        scratch_shapes=[pltpu.SemaphoreType.DMA((3, n)), pltpu.SemaphoreType.DMA((3, n)),
                        pltpu.SemaphoreType.DMA((3, n)), pltpu.SemaphoreType.DMA((3, n))])(*bufs)


def _gather_rider(bufs):
    n = len(bufs)
    rows = [b.shape[0] // N_CHIPS for b in bufs]

    def copies(outs, send_sems, recv_sems, inbound):
        x, y, c, chips = _place()
        for j, chip in enumerate(chips):
            for k in range(n):
                src_chip = chip if inbound else (x, y)
                blk = outs[k].at[_chip_rows(src_chip, rows[k]), _cols(c)]
                yield pltpu.make_async_remote_copy(src_ref=blk, dst_ref=blk, send_sem=send_sems.at[j, k], recv_sem=recv_sems.at[j, k],
                                                   device_id=(*chip, c), device_id_type=MESH)

    def start(ins, outs, send_sems, recv_sems):
        for cp in copies(outs, send_sems, recv_sems, False):
            cp.start()

    def wait(ins, outs, send_sems, recv_sems):
        for cp in copies(outs, send_sems, recv_sems, True):
            cp.wait_recv()
        for cp in copies(outs, send_sems, recv_sems, False):
            cp.wait_send()

    return _Rider(list(bufs), None, (3, n), start, wait)


def _forward_halves(bufs, name):
    n = len(bufs)
    rows = [b.shape[0] // N_CHIPS for b in bufs]

    def body(*refs):
        outs = refs[n:2 * n]
        send_sems, recv_sems = refs[2 * n:]
        x, y, c, chips = _place()

        def d2d(j, k, chip, half):
            blk = outs[k].at[_chip_rows(chip, rows[k]), _cols(half)]
            return pltpu.make_async_remote_copy(src_ref=blk, dst_ref=blk, send_sem=send_sems.at[j, k], recv_sem=recv_sems.at[j, k],
                                                device_id=(x, y, 1 - c), device_id_type=MESH)

        forwards = [d2d(j, k, chip, c) for j, chip in enumerate(chips) for k in range(n)]
        for cp in forwards:
            cp.start()
        for j, chip in enumerate(chips):
            for k in range(n):
                d2d(j, k, chip, 1 - c).wait_recv()
        for cp in forwards:
            cp.wait_send()

    any_spec = pl.BlockSpec(memory_space=pl.ANY)
    return pl.pallas_call(
        body, name=name, in_specs=[any_spec] * n, out_specs=[any_spec] * n,
        out_shape=[jax.ShapeDtypeStruct(b.shape, b.dtype) for b in bufs], input_output_aliases={k: k for k in range(n)},
        scratch_shapes=[pltpu.SemaphoreType.DMA((3, n)), pltpu.SemaphoreType.DMA((3, n))])(*bufs)


def _sibling_halves(grads, name):
    n = len(grads)

    def body(*refs):
        ins, outs = refs[:n], refs[n:2 * n]
        send_sems, recv_sems = refs[2 * n:]
        x, y, c, _ = _place()
        copies = [pltpu.make_async_remote_copy(src_ref=ins[k].at[:, pl.ds(1 - c, 1)], dst_ref=outs[k], send_sem=send_sems.at[k],
                                               recv_sem=recv_sems.at[k], device_id=(x, y, 1 - c), device_id_type=MESH)
                  for k in range(n)]
        for cp in copies:
            cp.start()
        for cp in copies:
            cp.wait()

    any_spec = pl.BlockSpec(memory_space=pl.ANY)
    return pl.pallas_call(
        body, name=name, in_specs=[any_spec] * n, out_specs=[any_spec] * n,
        out_shape=[jax.ShapeDtypeStruct((N_CHIPS, 1) + g.shape[2:], F32) for g in grads],
        scratch_shapes=[pltpu.SemaphoreType.DMA((n,)), pltpu.SemaphoreType.DMA((n,))])(*grads)


def _chip_sum(grad, from_sibling, place, name):
    rh, width = grad.shape[2:]

    def body(place_ref, g_ref, s_ref, own_ref, all_ref):
        total = g_ref[0, 0] + s_ref[0, 0]
        all_ref[0, 0] = total.astype(BF16)

        @pl.when(pl.program_id(0) == place_ref[0])
        def _():
            own_ref[0] = total

    blk = (1, 1, rh, width)
    return pl.pallas_call(
        body, name=name,
        grid_spec=pltpu.PrefetchScalarGridSpec(
            num_scalar_prefetch=1, grid=(N_CHIPS,),
            in_specs=[pl.BlockSpec(blk, lambda p, place: (p, place[1], 0, 0)), pl.BlockSpec(blk, lambda p, place: (p, 0, 0, 0))],
            out_specs=[pl.BlockSpec((1, rh, width), lambda p, place: (0, 0, 0)), pl.BlockSpec(blk, lambda p, place: (p, 0, 0, 0))]),
        out_shape=[jax.ShapeDtypeStruct((1, rh, width), F32), jax.ShapeDtypeStruct((N_CHIPS, 1, rh, width), BF16)],
        compiler_params=_params(dimension_semantics=("arbitrary",)))(place, grad, from_sibling)


def _scatter_rider(sums):
    n = len(sums)

    def copies(ins, outs, send_sems, recv_sems):
        x, y, c, chips = _place()
        return [pltpu.make_async_remote_copy(src_ref=ins[k].at[pl.ds(2 * chip[0] + chip[1], 1)], dst_ref=outs[k].at[pl.ds(j, 1)],
                                             send_sem=send_sems.at[j, k], recv_sem=recv_sems.at[j, k],
                                             device_id=(*chip, c), device_id_type=MESH)
                for j, chip in enumerate(chips) for k in range(n)]

    def start(*refs):
        for cp in copies(*refs):
            cp.start()

    def wait(*refs):
        for cp in copies(*refs):
            cp.wait()

    return _Rider(list(sums), [jax.ShapeDtypeStruct((3,) + sm.shape[1:], BF16) for sm in sums], (3, n), start, wait)


def _scatter_chip_sums(sums, name):
    n = len(sums)
    rider = _scatter_rider(sums)

    def body(*refs):
        rider.start(refs[:n], refs[n:2 * n], *refs[2 * n:])
        rider.wait(refs[:n], refs[n:2 * n], *refs[2 * n:])

    any_spec = pl.BlockSpec(memory_space=pl.ANY)
    return pl.pallas_call(body, name=name, in_specs=[any_spec] * n, out_specs=[any_spec] * n, out_shape=rider.landing,
                          scratch_shapes=[pltpu.SemaphoreType.DMA(rider.sems)] * 2)(*sums)


def _total_sum(own, received, name):
    def body(o_ref, r_ref, t_ref):
        total = o_ref[0]
        for j in range(3):
            total = total + r_ref[j, 0].astype(F32)
        t_ref[0] = total

    return pl.pallas_call(body, name=name, out_shape=jax.ShapeDtypeStruct(own.shape, F32), compiler_params=_params())(own, received)


def _swap_halves(halves):
    n = len(halves)

    def body(*refs):
        ins, outs = refs[:n], refs[n:2 * n]
        send_sems, recv_sems = refs[2 * n:]
        x, y, c, _ = _place()
        copies = [pltpu.make_async_remote_copy(src_ref=ins[k], dst_ref=outs[k], send_sem=send_sems.at[k],
                                               recv_sem=recv_sems.at[k], device_id=(x, y, 1 - c), device_id_type=MESH)
                  for k in range(n)]
        for cp in copies:
            cp.start()
        for cp in copies:
            cp.wait()

    any_spec = pl.BlockSpec(memory_space=pl.ANY)
    return pl.pallas_call(
        body, name="swap_halves", in_specs=[any_spec] * n, out_specs=[any_spec] * n,
        out_shape=[jax.ShapeDtypeStruct(h.shape, F32) for h in halves],
        scratch_shapes=[pltpu.SemaphoreType.DMA((n,)), pltpu.SemaphoreType.DMA((n,))])(*halves)


N_DEV = 8


def _gather_small(block):
    m_per, width = block.shape

    def body(x_ref, out_ref, send_sems, recv_sems, local_sem):
        x, y, c, chips = _place()
        me, sibling = (x, y, c), (x, y, 1 - c)

        def rows(px, py, pc):
            return out_ref.at[pl.ds((4 * px + 2 * py + pc) * m_per, m_per), :]

        def copy(k, blk, to, src=None):
            return pltpu.make_async_remote_copy(src_ref=rows(*blk) if src is None else src, dst_ref=rows(*blk),
                                                send_sem=send_sems.at[k], recv_sem=recv_sems.at[k], device_id=to, device_id_type=MESH)

        mine = pltpu.make_async_copy(x_ref, rows(*me), local_sem)
        mine.start()
        first = [copy(0, me, sibling, src=x_ref)] + [copy(1 + j, me, (*chip, c), src=x_ref) for j, chip in enumerate(chips)]
        for cp in first:
            cp.start()
        passed = [copy(4 + j, (*chip, c), sibling) for j, chip in enumerate(chips)]
        for j, chip in enumerate(chips):
            copy(1 + j, (*chip, c), me).wait_recv()
            passed[j].start()
        copy(0, sibling, me).wait_recv()
        for j, chip in enumerate(chips):
            copy(4 + j, (*chip, 1 - c), me).wait_recv()
        for cp in first + passed:
            cp.wait_send()
        mine.wait()

    vmem = pl.BlockSpec(memory_space=pltpu.VMEM)
    return pl.pallas_call(body, name="gather_small", out_shape=jax.ShapeDtypeStruct((N_DEV * m_per, width), F32),
                          in_specs=[vmem], out_specs=vmem,
                          scratch_shapes=[pltpu.SemaphoreType.DMA((7,)), pltpu.SemaphoreType.DMA((7,)),
                                          pltpu.SemaphoreType.DMA])(block)


def _adamw_math(w, g, m, v):
    m = ADAM_B1 * m + (1.0 - ADAM_B1) * g
    v = ADAM_B2 * v + (1.0 - ADAM_B2) * (g * g)
    m_hat = m / (1.0 - ADAM_B1 ** ADAM_STEP)
    v_hat = v / (1.0 - ADAM_B2 ** ADAM_STEP)
    delta = -ADAM_LR * (m_hat / (jnp.sqrt(v_hat) + ADAM_EPS) + ADAM_WD * w)
    return delta, m, v


def _adamw(w, mine, siblings, place, m, v, transposed, name):
    def body(place_ref, w_ref, mine_ref, sib_ref, m_ref, v_ref, go_ref, d_ref, mo_ref, vo_ref):
        first = place_ref[1] == 0
        g = jnp.concatenate([jnp.where(first, mine_ref[0], sib_ref[0]), jnp.where(first, sib_ref[0], mine_ref[0])], axis=0)
        g = g.T if transposed else g
        go_ref[...] = g
        d_ref[...], mo_ref[...], vo_ref[...] = _adamw_math(w_ref[...], g, m_ref[...], v_ref[...])

    vmem = pl.BlockSpec(memory_space=pltpu.VMEM)
    return pl.pallas_call(body, name=name, in_specs=[pl.BlockSpec(memory_space=pltpu.SMEM)] + [vmem] * 5, out_specs=[vmem] * 4,
                          out_shape=[jax.ShapeDtypeStruct(w.shape, F32)] * 4, compiler_params=_params())(
                              place, w, mine, siblings, m, v)


def _adamw_small(gathered, w, m, v, name):
    def body(ga_ref, w_ref, m_ref, v_ref, go_ref, d_ref, mo_ref, vo_ref):
        g = ga_ref[0]
        for dev in range(1, N_DEV):
            g = g + ga_ref[dev]
        go_ref[...] = g
        d_ref[...], mo_ref[...], vo_ref[...] = _adamw_math(w_ref[...], g, m_ref[...], v_ref[...])

    return pl.pallas_call(body, name=name, out_shape=[jax.ShapeDtypeStruct(w.shape, F32)] * 4,
                          compiler_params=_params())(gathered, w, m, v)


class _Exchange:
    FIRST = ("w1_gate", "w1_up", "w1_down")
    HOSTS = {"ffn1_bwd": ("w2_down", "w2_gate", "w2_up", "w_out", "w_in"), "ffn1_wgrad_gate": ("w1_down",),
             "ffn1_wgrad_up": ("w1_gate",)}

    def __init__(self, bufs, place):
        self.bufs, self.place = bufs, place
        self.later = [k for k in SEGMENTS if k not in self.FIRST]
        self.own, self.to_send, self.received = {}, {}, {}

    def first_weights(self):
        return dict(zip(self.FIRST, _gather_weights([self.bufs[k] for k in self.FIRST])))

    def rider(self, host):
        if host == "ffn1_fwd":
            return _gather_rider([self.bufs[k] for k in self.later])
        if host in self.HOSTS:
            return _scatter_rider([self.to_send[k] for k in self.HOSTS[host]])
        return None

    def landed(self, host, results):
        if host == "ffn1_fwd":
            self.rest = dict(zip(self.later, _forward_halves(results, "gather_rest_forward")))
        elif host in self.HOSTS:
            self.received.update(zip(self.HOSTS[host], results))

    def rest_weights(self):
        return self.rest

    def gradient(self, name, grad):
        split = grad.reshape(N_CHIPS, 2, grad.shape[0] // (2 * N_CHIPS), grad.shape[1])
        (from_sibling,) = _sibling_halves([split], f"reduce_sibling_{name}")
        self.own[name], self.to_send[name] = _chip_sum(split, from_sibling, self.place, f"chip_sum_{name}")

    def summed_halves(self):
        late = [k for k in SEGMENTS if k not in self.received]
        self.received.update(zip(late, _scatter_chip_sums([self.to_send[k] for k in late], "reduce_chips_last")))
        return [_total_sum(self.own[k], self.received[k], f"total_{k}") for k in SEGMENTS]


SMALL = ("g_ffn1_pre", "g_ffn1_post", "g_mix_pre", "w_pool_lin", "pool_scale", "g_mix_post", "g_ffn2_pre", "g_ffn2_post")
WEIGHTS = ("g_ffn1_pre", "w1_gate", "w1_up", "w1_down", "g_ffn1_post", "g_mix_pre", "w_in", "w_pool_lin", "pool_scale", "w_out",
           "g_mix_post", "g_ffn2_pre", "w2_gate", "w2_up", "w2_down", "g_ffn2_post")
LANES = 128


def _pack_small(tree):
    flat = jnp.concatenate([tree[k].reshape(-1) for k in SMALL])
    rows = -(-flat.shape[0] // (8 * LANES)) * 8
    return jnp.pad(flat, (0, rows * LANES - flat.shape[0])).reshape(rows, LANES)


def _unpack_small(packed, like):
    flat, out, at = packed.reshape(-1), {}, 0
    for k in SMALL:
        size = math.prod(like[k].shape)
        out[k] = flat[at:at + size].reshape(like[k].shape)
        at += size
    return out


def kernel(x, g_ffn1_pre, w1_gate, w1_up, w1_down, g_ffn1_post, g_mix_pre, w_in, w_pool_lin, pool_scale, w_out, g_mix_post, g_ffn2_pre, w2_gate, w2_up, w2_down, g_ffn2_post, loss_target, m_g_ffn1_pre, m_w1_gate, m_w1_up, m_w1_down, m_g_ffn1_post, m_g_mix_pre, m_w_in, m_w_pool_lin, m_pool_scale, m_w_out, m_g_mix_post, m_g_ffn2_pre, m_w2_gate, m_w2_up, m_w2_down, m_g_ffn2_post, v_g_ffn1_pre, v_w1_gate, v_w1_up, v_w1_down, v_g_ffn1_post, v_g_mix_pre, v_w_in, v_w_pool_lin, v_pool_scale, v_w_out, v_g_mix_post, v_g_ffn2_pre, v_w2_gate, v_w2_up, v_w2_down, v_g_ffn2_post):
    given = dict(locals())
    w = {k: given[k] for k in WEIGHTS}
    m = {k: given["m_" + k] for k in WEIGHTS}
    v = {k: given["v_" + k] for k in WEIGHTS}
    small = {k: (w[k][0] if k == "w_pool_lin" else w[k].reshape(1, -1)) for k in SMALL}

    place = jnp.stack([2 * lax.axis_index("x") + lax.axis_index("y"), lax.axis_index("c")]).astype(jnp.int32)
    exchange = _Exchange({k: _cast_shard(w[k][0], k in TRANSPOSED, place, f"cast_{k}") for k in SEGMENTS}, place)
    loss_part, grad_x, small_grads = _local_step(x[0], loss_target[0], small, exchange)
    loss = lax.psum(loss_part, ("x", "y", "c"))

    halves = exchange.summed_halves()
    from_sibling = _swap_halves(halves)

    out_grad, out_delta, out_m, out_v = {}, {}, {}, {}
    for k, mine, sib in zip(SEGMENTS, halves, from_sibling):
        out_grad[k], out_delta[k], out_m[k], out_v[k] = (
            a[None] for a in _adamw(w[k][0], mine, sib, place, m[k][0], v[k][0], k in TRANSPOSED, f"adamw_{k}"))

    small_grads["w_pool_lin"] = small_grads["w_pool_lin"][None]
    packed = _pack_small(small_grads)
    gathered = _gather_small(packed).reshape(N_DEV, *packed.shape)
    like = {k: w[k] for k in SMALL}
    results = _adamw_small(gathered, _pack_small(like), _pack_small({k: m[k] for k in SMALL}),
                           _pack_small({k: v[k] for k in SMALL}), "adamw_small")
    for tree, res in zip((out_grad, out_delta, out_m, out_v), results):
        tree.update(_unpack_small(res, like))

    return (loss, grad_x[None], *[out_grad[k] for k in WEIGHTS], *[out_delta[k] for k in WEIGHTS],
            *[out_m[k] for k in WEIGHTS], *[out_v[k] for k in WEIGHTS])
```

```python
import math
import typing

import numpy as np
import jax
import jax.numpy as jnp
from jax import lax
from jax.experimental import pallas as pl
from jax.experimental.pallas import tpu as pltpu

F32 = jnp.float32
BF16 = jnp.bfloat16
MESH = pl.DeviceIdType.MESH

RMS_EPS = 1e-6
HEAD_DIM = 64
POOL_HALF_WINDOWS = (1, 2, 4, 8)
POOL_DIM = 256
GROUP_DIM = 256
DILATIONS = (1, 4, 16)
N_SIDE = 64
N_ATTN_HEADS = 12
ADAM_LR, ADAM_B1, ADAM_B2, ADAM_EPS, ADAM_WD, ADAM_STEP = 0.001, 0.9, 0.999, 1e-08, 0.01, 10

N_CHIPS = 4
V7X_VMEM_LIMIT = 60 * 1024 * 1024

_NT = (((1,), (1,)), ((), ()))
_TN = (((0,), (0,)), ((), ()))


def _dot(a, b):
    return jnp.dot(a, b, preferred_element_type=F32)


def _dot_nt(a, b):
    return lax.dot_general(a, b, _NT, preferred_element_type=F32)


def _dot_tn(a, b):
    return lax.dot_general(a, b, _TN, preferred_element_type=F32)


def _params(**kw):
    return pltpu.CompilerParams(vmem_limit_bytes=V7X_VMEM_LIMIT, **kw)


def _rows(tm, width):
    return pl.BlockSpec((tm, width), lambda i: (i, 0))


def _resident(shape):
    return pl.BlockSpec(shape, lambda i: (0,) * len(shape), pipeline_mode=pl.Buffered(1))


def _const(shape):
    return pl.BlockSpec(shape, lambda i: (0,) * len(shape))


def _tile(rows, cap):
    return max(t for t in range(16, cap + 1, 16) if rows % t == 0)


def _inv_rms(x):
    return lax.rsqrt(jnp.mean(x * x, axis=-1, keepdims=True) + RMS_EPS)


def _rms_bwd(x, inv, g, dy):
    n = x * inv
    dn = dy * g
    dx = inv * (dn - n * jnp.mean(dn * n, axis=-1, keepdims=True))
    return dx, jnp.sum(dy * n, axis=0, keepdims=True)


def _accumulate(ref, value):
    @pl.when(pl.program_id(0) == 0)
    def _():
        ref[...] = jnp.zeros_like(ref)

    ref[...] += value


class _Rider(typing.NamedTuple):
    operands: list
    landing: typing.Optional[list]
    sems: tuple
    start: typing.Callable
    wait: typing.Callable


def _hosted_call(body, rider, *, name, steps, in_specs, out_specs, out_shape, args, scratch_shapes=()):
    params = _params(dimension_semantics=("arbitrary",))
    if rider is None:
        res = pl.pallas_call(body, name=name, grid=(steps,), in_specs=in_specs, out_specs=out_specs, out_shape=out_shape,
                             scratch_shapes=list(scratch_shapes), compiler_params=params)(*args)
        return list(res), []
    n_in, n_out, n_scratch, r_in = len(in_specs), len(out_specs), len(scratch_shapes), len(rider.operands)
    landing = rider.landing if rider.landing is not None else [jax.ShapeDtypeStruct(a.shape, a.dtype) for a in rider.operands]
    aliases = {n_in + i: n_out + i for i in range(r_in)} if rider.landing is None else {}

    def riding(*refs):
        rider_in = refs[n_in:n_in + r_in]
        outs_at = n_in + r_in
        rider_out = refs[outs_at + n_out:outs_at + n_out + len(landing)]
        scratch_at = outs_at + n_out + len(landing)
        send_sems, recv_sems = refs[scratch_at + n_scratch:]

        @pl.when(pl.program_id(0) == 0)
        def _():
            rider.start(rider_in, rider_out, send_sems, recv_sems)

        body(*refs[:n_in], *refs[outs_at:outs_at + n_out], *refs[scratch_at:scratch_at + n_scratch])

        @pl.when(pl.program_id(0) == steps - 1)
        def _():
            rider.wait(rider_in, rider_out, send_sems, recv_sems)

    any_spec = pl.BlockSpec(memory_space=pl.ANY)
    res = pl.pallas_call(
        riding, name=name, grid=(steps,), in_specs=list(in_specs) + [any_spec] * r_in,
        out_specs=list(out_specs) + [any_spec] * len(landing), out_shape=list(out_shape) + landing,
        scratch_shapes=list(scratch_shapes) + [pltpu.SemaphoreType.DMA(rider.sems)] * 2,
        input_output_aliases=aliases, compiler_params=params)(*args, *rider.operands)
    return list(res[:n_out]), list(res[n_out:])


def _ffn_fwd(x, g_pre, wg_t, wu_t, wd, g_post, target, name, rider=None, tm=256):
    s, d = x.shape
    ff = wd.shape[0]
    with_loss = target is not None

    def body(*refs):
        if with_loss:
            x_ref, gpre_ref, wg_ref, wu_ref, wd_ref, gpost_ref, t_ref, xo_ref, a_ref, b_ref, f_ref, loss_ref = refs
        else:
            x_ref, gpre_ref, wg_ref, wu_ref, wd_ref, gpost_ref, xo_ref, a_ref, b_ref, f_ref = refs
        xv = x_ref[...]
        hb = (xv * _inv_rms(xv) * gpre_ref[...]).astype(BF16)
        a = _dot_nt(hb, wg_ref[...])
        b = _dot_nt(hb, wu_ref[...])
        hh = (a * jax.nn.sigmoid(a)) * b
        f = _dot(hh.astype(BF16), wd_ref[...])
        xo = xv + 0.5 * (f * _inv_rms(f) * gpost_ref[...])
        a_ref[...] = a.astype(BF16)
        b_ref[...] = b.astype(BF16)
        f_ref[...] = f
        if with_loss:
            e = xo - t_ref[...]
            xo_ref[...] = e * (1.0 / d)
            _accumulate(loss_ref, 0.5 * jnp.sum(jnp.mean(e * e, axis=-1, keepdims=True)))
        else:
            xo_ref[...] = xo

    in_specs = [_rows(tm, d), _const((1, d)), _resident((ff, d)), _resident((ff, d)), _resident((ff, d)), _const((1, d))]
    args = [x, g_pre, wg_t, wu_t, wd, g_post]
    out_shape = [jax.ShapeDtypeStruct((s, d), F32), jax.ShapeDtypeStruct((s, ff), BF16),
                 jax.ShapeDtypeStruct((s, ff), BF16), jax.ShapeDtypeStruct((s, d), F32)]
    out_specs = [_rows(tm, d), _rows(tm, ff), _rows(tm, ff), _rows(tm, d)]
    if with_loss:
        in_specs.append(_rows(tm, d))
        args.append(target)
        out_shape.append(jax.ShapeDtypeStruct((8, 128), F32))
        out_specs.append(_const((8, 128)))
    return _hosted_call(body, rider, name=name, steps=s // tm, in_specs=in_specs, out_specs=out_specs, out_shape=out_shape, args=args)


def _ffn_bwd(dxo, x, f, a, b, g_pre, g_post, wg_t, wu_t, wd, name, rider=None, tm=256):
    s, d = x.shape
    ff = wd.shape[0]

    def body(dxo_ref, x_ref, f_ref, a_ref, b_ref, gpre_ref, gpost_ref, wg_ref, wu_ref, wd_ref,
             dx_ref, hh_ref, da_ref, db_ref, df_ref, h_ref, dgpre_ref, dgpost_ref):
        dxo_v = dxo_ref[...]
        fv = f_ref[...]
        df, dgpost = _rms_bwd(fv, _inv_rms(fv), gpost_ref[...], 0.5 * dxo_v)
        dfb = df.astype(BF16)
        dhh = _dot_nt(dfb, wd_ref[...])
        av = a_ref[...].astype(F32)
        bv = b_ref[...].astype(F32)
        sig = jax.nn.sigmoid(av)
        sa = av * sig
        da = (dhh * bv * (sig * (1.0 + av * (1.0 - sig)))).astype(BF16)
        db = (dhh * sa).astype(BF16)
        dh = _dot(da, wg_ref[...]) + _dot(db, wu_ref[...])
        xv = x_ref[...]
        inv = _inv_rms(xv)
        dxn, dgpre = _rms_bwd(xv, inv, gpre_ref[...], dh)
        dx_ref[...] = dxo_v + dxn
        hh_ref[...] = (sa * bv).astype(BF16)
        da_ref[...] = da
        db_ref[...] = db
        df_ref[...] = dfb
        h_ref[...] = (xv * inv * gpre_ref[...]).astype(BF16)
        _accumulate(dgpre_ref, dgpre)
        _accumulate(dgpost_ref, dgpost)

    return _hosted_call(
        body, rider, name=name, steps=s // tm,
        in_specs=[_rows(tm, d), _rows(tm, d), _rows(tm, d), _rows(tm, ff), _rows(tm, ff), _const((1, d)), _const((1, d)),
                  _resident((ff, d)), _resident((ff, d)), _resident((ff, d))],
        out_specs=[_rows(tm, d), _rows(tm, ff), _rows(tm, ff), _rows(tm, ff), _rows(tm, d), _rows(tm, d),
                   _const((1, d)), _const((1, d))],
        out_shape=[jax.ShapeDtypeStruct((s, d), F32), jax.ShapeDtypeStruct((s, ff), BF16), jax.ShapeDtypeStruct((s, ff), BF16),
                   jax.ShapeDtypeStruct((s, ff), BF16), jax.ShapeDtypeStruct((s, d), BF16), jax.ShapeDtypeStruct((s, d), BF16),
                   jax.ShapeDtypeStruct((1, d), F32), jax.ShapeDtypeStruct((1, d), F32)],
        args=[dxo, x, f, a, b, g_pre, g_post, wg_t, wu_t, wd])


def _wgrad(lhs, rhs, name, rider=None, rt=256):
    s, r = lhs.shape
    c = rhs.shape[1]

    def body(l_ref, r_ref, o_ref):
        o_ref[...] = _dot_tn(l_ref[...], r_ref[...])

    (out,), riding = _hosted_call(
        body, rider, name=name, steps=pl.cdiv(r, rt), in_specs=[pl.BlockSpec((s, rt), lambda i: (0, i)), _resident((s, c))],
        out_specs=[pl.BlockSpec((rt, c), lambda i: (i, 0))], out_shape=[jax.ShapeDtypeStruct((r, c), F32)], args=[lhs, rhs])
    return out, riding


def _in_fwd(x, g, w_in_t, name, tm=512):
    s, d = x.shape
    d_in = w_in_t.shape[0]
    n_parts = (d_in - POOL_DIM) // GROUP_DIM

    def body(x_ref, g_ref, w_ref, u_ref, *part_refs):
        xv = x_ref[...]
        hb = (xv * _inv_rms(xv) * g_ref[...]).astype(BF16)
        z = _dot_nt(hb, w_ref[...])
        u_ref[...] = z[:, :POOL_DIM]
        for j, ref in enumerate(part_refs):
            ref[...] = z[:, POOL_DIM + GROUP_DIM * j:POOL_DIM + GROUP_DIM * (j + 1)]

    return pl.pallas_call(
        body, name=name, grid=(s // tm,), in_specs=[_rows(tm, d), _const((1, d)), _resident((d_in, d))],
        out_specs=[_rows(tm, POOL_DIM)] + [_rows(tm, GROUP_DIM)] * n_parts,
        out_shape=[jax.ShapeDtypeStruct((s, POOL_DIM), F32)] + [jax.ShapeDtypeStruct((s, GROUP_DIM), F32)] * n_parts,
        compiler_params=_params(dimension_semantics=("arbitrary",)))(x, g, w_in_t)


def _in_bwd(du, dparts, x, dxo, g, w_in_t, name, rider=None, tm=512):
    s, d = x.shape
    d_in = w_in_t.shape[0]
    n_parts = len(dparts)

    def body(du_ref, *refs):
        part_refs = refs[:n_parts]
        x_ref, dxo_ref, g_ref, w_ref, dx_ref, dz_ref, h_ref, dg_ref = refs[n_parts:]
        dz = jnp.concatenate([r[...].astype(BF16) for r in (du_ref,) + part_refs], axis=1)
        dz_ref[...] = dz
        dh = _dot(dz, w_ref[...])
        xv = x_ref[...]
        inv = _inv_rms(xv)
        dxn, dg = _rms_bwd(xv, inv, g_ref[...], dh)
        dx_ref[...] = dxo_ref[...] + dxn
        h_ref[...] = (xv * inv * g_ref[...]).astype(BF16)
        _accumulate(dg_ref, dg)

    return _hosted_call(
        body, rider, name=name, steps=s // tm,
        in_specs=[_rows(tm, POOL_DIM)] + [_rows(tm, GROUP_DIM)] * n_parts + [_rows(tm, d), _rows(tm, d), _const((1, d)),
                                                                             _resident((d_in, d))],
        out_specs=[_rows(tm, d), _rows(tm, d_in), _rows(tm, d), _const((1, d))],
        out_shape=[jax.ShapeDtypeStruct((s, d), F32), jax.ShapeDtypeStruct((s, d_in), BF16), jax.ShapeDtypeStruct((s, d), BF16),
                   jax.ShapeDtypeStruct((1, d), F32)],
        args=[du, *dparts, x, dxo, g, w_in_t])


_POOL_HALO = 8


def _pool_chain(v, first_shift):
    n = v.shape[0]
    p2 = v + pltpu.roll(v, first_shift, 0)
    p4 = pltpu.roll(p2, 1, 0) + pltpu.roll(p2, n - 1, 0)
    p8 = pltpu.roll(p4, 2, 0) + pltpu.roll(p4, n - 2, 0)
    p16 = pltpu.roll(p8, 4, 0) + pltpu.roll(p8, n - 4, 0)
    group = lax.broadcasted_iota(jnp.int32, v.shape, 1) // HEAD_DIM
    return jnp.where(group == 0, p2, jnp.where(group == 1, p4, jnp.where(group == 2, p8, p16)))


def _pool_count(t0, rows, s):
    t = t0 + lax.broadcasted_iota(jnp.int32, (rows, POOL_DIM), 0)
    group = lax.broadcasted_iota(jnp.int32, (rows, POOL_DIM), 1) // HEAD_DIM
    half = jnp.where(group == 0, 1, jnp.where(group == 1, 2, jnp.where(group == 2, 4, 8)))
    cnt = jnp.minimum(t + half, s) - jnp.maximum(t - half, 0)
    return jnp.maximum(cnt, 1).astype(F32)


def _pad_rows(ref, pad_ref, s):
    zeros = jnp.zeros((_POOL_HALO, pad_ref.shape[1]), pad_ref.dtype)
    pad_ref[pl.ds(0, _POOL_HALO), :] = zeros
    pad_ref[pl.ds(_POOL_HALO + s, _POOL_HALO), :] = zeros
    pad_ref[pl.ds(_POOL_HALO, s), :] = ref[...]


def _pool_fwd(u, w_bd, scale, name, tm=512):
    s = u.shape[0]
    ext = tm + 2 * _POOL_HALO

    def body(u_ref, w_ref, sc_ref, o_ref, upad):
        _pad_rows(u_ref, upad, s)

        def tile(i, carry):
            t0 = pl.multiple_of(i * tm, tm)
            uv = upad[pl.ds(t0, ext), :]
            win = _pool_chain(uv, 1)[_POOL_HALO:_POOL_HALO + tm]
            y = win / _pool_count(t0, tm, s) - uv[_POOL_HALO:_POOL_HALO + tm]
            o_ref[pl.ds(t0, tm), :] = (_dot(y.astype(BF16), w_ref[...]) * sc_ref[...]).astype(BF16)
            return carry

        lax.fori_loop(0, s // tm, tile, 0)

    return pl.pallas_call(body, name=name, out_shape=jax.ShapeDtypeStruct((s, POOL_DIM), BF16),
                          scratch_shapes=[pltpu.VMEM((s + 2 * _POOL_HALO, POOL_DIM), F32)],
                          compiler_params=_params())(u, w_bd, scale)


def _pool_bwd(u, da, w_bd, scale, name, tm=512):
    s = u.shape[0]
    ext = tm + 2 * _POOL_HALO

    def body(u_ref, da_ref, w_ref, sc_ref, du_ref, dw_ref, dsc_ref, upad, dapad):
        _pad_rows(u_ref, upad, s)
        _pad_rows(da_ref, dapad, s)
        dw_ref[...] = jnp.zeros_like(dw_ref)
        dsc_ref[...] = jnp.zeros_like(dsc_ref)

        def tile(i, carry):
            t0 = pl.multiple_of(i * tm, tm)
            uv = upad[pl.ds(t0, ext), :]
            dav = dapad[pl.ds(t0, ext), :]
            win = _pool_chain(uv, 1)[_POOL_HALO:_POOL_HALO + tm]
            yb = (win / _pool_count(t0, tm, s) - uv[_POOL_HALO:_POOL_HALO + tm]).astype(BF16)
            yl = _dot(yb, w_ref[...])
            da_c = dav[_POOL_HALO:_POOL_HALO + tm]
            dsc_ref[...] += jnp.sum(da_c * yl, axis=0, keepdims=True)
            dyl = (dav * sc_ref[...]).astype(BF16)
            dw_ref[...] += _dot_tn(yb, dyl[_POOL_HALO:_POOL_HALO + tm])
            dy = _dot_nt(dyl, w_ref[...])
            dyc = dy / _pool_count(t0 - _POOL_HALO, ext, s)
            du_ref[pl.ds(t0, tm), :] = (_pool_chain(dyc, ext - 1) - dy)[_POOL_HALO:_POOL_HALO + tm]
            return carry

        lax.fori_loop(0, s // tm, tile, 0)

    pool_cols = pl.BlockSpec((s, POOL_DIM), lambda i: (0, 0), pipeline_mode=pl.Buffered(1))
    return pl.pallas_call(
        body, name=name, grid=(1,),
        in_specs=[pool_cols, pool_cols, _const((POOL_DIM, POOL_DIM)), _const((1, POOL_DIM))],
        out_specs=[_const((s, POOL_DIM)), _const((POOL_DIM, POOL_DIM)), _const((1, POOL_DIM))],
        out_shape=[jax.ShapeDtypeStruct((s, POOL_DIM), F32), jax.ShapeDtypeStruct((POOL_DIM, POOL_DIM), F32),
                   jax.ShapeDtypeStruct((1, POOL_DIM), F32)],
        scratch_shapes=[pltpu.VMEM((s + 2 * _POOL_HALO, POOL_DIM), F32), pltpu.VMEM((s + 2 * _POOL_HALO, POOL_DIM), F32)],
        compiler_params=_params(dimension_semantics=("arbitrary",)))(u, da, w_bd, scale)


_BQ = 128
_KW = _BQ + 2 * N_SIDE
_PAIR = 2 * HEAD_DIM
_NEG = -1e30
_ATTN_UNROLL = 4
_SCORE_SCALE = HEAD_DIM ** -0.5


def _stack_heads(x):
    lane_head = lax.broadcasted_iota(jnp.int32, x.shape, 1) // HEAD_DIM
    zero = jnp.zeros_like(x)
    return jnp.concatenate([jnp.where(lane_head == 0, x, zero), jnp.where(lane_head == 1, x, zero)], axis=0)


def _unstack_heads(x):
    lane_head = lax.broadcasted_iota(jnp.int32, (_BQ, _PAIR), 1) // HEAD_DIM
    return jnp.where(lane_head == 0, x[:_BQ], x[_BQ:])


def _stack_cols(x):
    return jnp.concatenate([x[:, 0:1], x[:, HEAD_DIM:HEAD_DIM + 1]], axis=0)


def _fill_bias(bias_ref, slopes_ref, dilation):
    row = lax.broadcasted_iota(jnp.int32, (2 * _BQ, _KW), 0)
    col = lax.broadcasted_iota(jnp.int32, (2 * _BQ, _KW), 1)
    pair = 2 * pl.program_id(0)
    slope = jnp.where(row < _BQ, slopes_ref[pair], slopes_ref[pair + 1]) * float(dilation)
    for j in range(3):
        dist = jnp.abs(col - (row & (_BQ - 1)) - j * N_SIDE)
        bias_ref[j] = jnp.where(dist <= N_SIDE, -slope * dist.astype(F32), _NEG)


def _block_window(i, n_blocks, length):
    q0 = pl.multiple_of(i * _BQ, _BQ)
    ws = pl.multiple_of(jnp.clip(q0 - N_SIDE, 0, length - _KW), N_SIDE)
    return q0, ws, jnp.where(i == 0, 0, jnp.where(i == n_blocks - 1, 2, 1))


def _residue_rows(dilation, start, count):
    if dilation == 1:
        return pl.ds(start, count)
    return pl.ds(start * dilation + pl.program_id(1), count, stride=dilation)


def _attn_call(body, name, dilation, seq, n_in, n_out, scratch):
    col = pl.BlockSpec((seq, _PAIR), lambda c, r: (0, c), pipeline_mode=pl.Buffered(1))
    return pl.pallas_call(
        body, name=name, grid=(GROUP_DIM // _PAIR, dilation),
        in_specs=[pl.BlockSpec(memory_space=pltpu.SMEM)] + [col] * n_in, out_specs=[col] * n_out,
        out_shape=[jax.ShapeDtypeStruct((seq, GROUP_DIM), F32)] * n_out, scratch_shapes=scratch,
        compiler_params=_params(dimension_semantics=("arbitrary", "arbitrary")))


def _attn_fwd(q, k, v, slopes, dilation, name):
    seq = q.shape[0]
    length = seq // dilation
    n_blocks = length // _BQ

    def body(sl_ref, q_ref, k_ref, v_ref, o_ref, lse_ref, qs, ks, vs, bias_ref):
        all_rows = _residue_rows(dilation, 0, length)
        qs[...] = (q_ref[all_rows, :] * _SCORE_SCALE).astype(BF16)
        ks[...] = k_ref[all_rows, :].astype(BF16)
        vs[...] = v_ref[all_rows, :].astype(BF16)
        _fill_bias(bias_ref, sl_ref, dilation)

        def block(i, carry):
            q0, ws, which = _block_window(i, n_blocks, length)
            kw = ks[pl.ds(ws, _KW), :]
            vw = vs[pl.ds(ws, _KW), :]
            sc = _dot_nt(_stack_heads(qs[pl.ds(q0, _BQ), :]), kw) + bias_ref[which]
            m = jnp.max(sc, axis=-1, keepdims=True)
            p = jnp.exp(sc - m)
            den = jnp.sum(p, axis=-1, keepdims=True)
            rows = _residue_rows(dilation, q0, _BQ)
            o_ref[rows, :] = _unstack_heads(_dot(p.astype(BF16), vw) / den)
            lse_ref[rows, :] = _unstack_heads(jnp.broadcast_to(m + jnp.log(den), (2 * _BQ, _PAIR)))
            return carry

        lax.fori_loop(0, n_blocks, block, 0, unroll=min(_ATTN_UNROLL, n_blocks))

    stage = pltpu.VMEM((length, _PAIR), BF16)
    bias = pltpu.VMEM((3, 2 * _BQ, _KW), F32)
    return _attn_call(body, name, dilation, seq, 3, 2, [stage] * 3 + [bias])(slopes, q, k, v)


def _attn_bwd(q, k, v, do, lse, cterm, slopes, dilation, name):
    seq = q.shape[0]
    length = seq // dilation
    n_blocks = length // _BQ

    def body(sl_ref, q_ref, k_ref, v_ref, do_ref, lse_ref, c_ref, dq_ref, dk_ref, dv_ref, qs, ks, vs, dos, dk_acc, dv_acc, bias_ref):
        all_rows = _residue_rows(dilation, 0, length)
        qs[...] = (q_ref[all_rows, :] * _SCORE_SCALE).astype(BF16)
        for src, dst in ((k_ref, ks), (v_ref, vs), (do_ref, dos)):
            dst[...] = src[all_rows, :].astype(BF16)
        dk_acc[...] = jnp.zeros_like(dk_acc)
        dv_acc[...] = jnp.zeros_like(dv_acc)
        _fill_bias(bias_ref, sl_ref, dilation)

        def block(i, carry):
            q0, ws, which = _block_window(i, n_blocks, length)
            rows = _residue_rows(dilation, q0, _BQ)
            qm = _stack_heads(qs[pl.ds(q0, _BQ), :])
            dom = _stack_heads(dos[pl.ds(q0, _BQ), :])
            kw = ks[pl.ds(ws, _KW), :]
            vw = vs[pl.ds(ws, _KW), :]
            p = jnp.exp(_dot_nt(qm, kw) + bias_ref[which] - _stack_cols(lse_ref[rows, :]))
            ds = (p * (_dot_nt(dom, vw) + _stack_cols(c_ref[rows, :]))).astype(BF16)
            dq_ref[rows, :] = _unstack_heads(_dot(ds, kw)) * _SCORE_SCALE
            dk_acc[pl.ds(ws, _KW), :] += _dot_tn(ds, qm)
            dv_acc[pl.ds(ws, _KW), :] += _dot_tn(p.astype(BF16), dom)
            return carry

        lax.fori_loop(0, n_blocks, block, 0, unroll=min(_ATTN_UNROLL, n_blocks))
        dk_ref[all_rows, :] = dk_acc[...]
        dv_ref[all_rows, :] = dv_acc[...]

    stage = pltpu.VMEM((length, _PAIR), BF16)
    acc = pltpu.VMEM((length, _PAIR), F32)
    bias = pltpu.VMEM((3, 2 * _BQ, _KW), F32)
    return _attn_call(body, name, dilation, seq, 6, 3, [stage] * 4 + [acc] * 2 + [bias])(slopes, q, k, v, do, lse, cterm)


def _group_weights(lses):
    m = jnp.maximum(jnp.maximum(lses[0], lses[1]), lses[2])
    es = [jnp.exp(l - m) for l in lses]
    den = es[0] + es[1] + es[2]
    return [e / den for e in es]


def _combine_fwd(a_pool, outs, lses, name, tm=512):
    s = a_pool.shape[0]

    def body(ap_ref, o0, o1, o2, l0, l1, l2, cat_ref):
        alphas = _group_weights([l0[...], l1[...], l2[...]])
        parts = [ap_ref[...]] + [(o[...] * al).astype(BF16) for o, al in zip((o0, o1, o2), alphas)]
        cat_ref[...] = jnp.concatenate(parts, axis=1)

    width = POOL_DIM + 3 * GROUP_DIM
    return pl.pallas_call(body, name=name, grid=(s // tm,), in_specs=[_rows(tm, POOL_DIM)] + [_rows(tm, GROUP_DIM)] * 6,
                          out_specs=_rows(tm, width), out_shape=jax.ShapeDtypeStruct((s, width), BF16),
                          compiler_params=_params(dimension_semantics=("arbitrary",)))(a_pool, *outs, *lses)


def _combine_bwd(dcat, outs, lses, head_ones, name, tm=512):
    s = dcat.shape[0]

    def body(dc_ref, o0, o1, o2, l0, l1, l2, ones_ref, do0, do1, do2, c0, c1, c2):
        alphas = _group_weights([l0[...], l1[...], l2[...]])
        dcat_v = dc_ref[...]
        das = [dcat_v[:, POOL_DIM + GROUP_DIM * g:POOL_DIM + GROUP_DIM * (g + 1)] for g in range(3)]
        prod = sum(da * (o[...] * al) for da, o, al in zip(das, (o0, o1, o2), alphas))
        hi = prod.astype(BF16)
        lo = (prod - hi.astype(F32)).astype(BF16)
        total = _dot(hi, ones_ref[...]) + _dot(lo, ones_ref[...])
        for da, al, do_ref, c_ref in zip(das, alphas, (do0, do1, do2), (c0, c1, c2)):
            do_ref[...] = da * al
            c_ref[...] = -al * total

    width = POOL_DIM + 3 * GROUP_DIM
    return pl.pallas_call(
        body, name=name, grid=(s // tm,),
        in_specs=[_rows(tm, width)] + [_rows(tm, GROUP_DIM)] * 6 + [_const((GROUP_DIM, GROUP_DIM))],
        out_specs=[_rows(tm, GROUP_DIM)] * 6,
        out_shape=[jax.ShapeDtypeStruct((s, GROUP_DIM), F32)] * 6,
        compiler_params=_params(dimension_semantics=("arbitrary",)))(dcat, *outs, *lses, head_ones)


def _out_fwd(cat, x, w_out, g, name, tm=512):
    s, d = x.shape

    def body(cat_ref, x_ref, w_ref, g_ref, xo_ref, mix_ref):
        mix = _dot(cat_ref[...], w_ref[...])
        mix_ref[...] = mix
        xo_ref[...] = x_ref[...] + mix * _inv_rms(mix) * g_ref[...]

    return pl.pallas_call(body, name=name, grid=(s // tm,),
                          in_specs=[_rows(tm, cat.shape[1]), _rows(tm, d), _resident(w_out.shape), _const((1, d))],
                          out_specs=[_rows(tm, d), _rows(tm, d)], out_shape=[jax.ShapeDtypeStruct((s, d), F32)] * 2,
                          compiler_params=_params(dimension_semantics=("arbitrary",)))(cat, x, w_out, g)


def _out_bwd(dxo, mix, w_out, g, name, tm=512):
    s, d = mix.shape
    width = w_out.shape[0]

    def body(dxo_ref, mix_ref, w_ref, g_ref, dcat_ref, dmix_ref, dg_ref):
        mv = mix_ref[...]
        dmix, dg = _rms_bwd(mv, _inv_rms(mv), g_ref[...], dxo_ref[...])
        dmb = dmix.astype(BF16)
        dmix_ref[...] = dmb
        dcat_ref[...] = _dot_nt(dmb, w_ref[...])
        _accumulate(dg_ref, dg)

    return pl.pallas_call(
        body, name=name, grid=(s // tm,), in_specs=[_rows(tm, d), _rows(tm, d), _resident(w_out.shape), _const((1, d))],
        out_specs=[_rows(tm, width), _rows(tm, d), _const((1, d))],
        out_shape=[jax.ShapeDtypeStruct((s, width), F32), jax.ShapeDtypeStruct((s, d), BF16), jax.ShapeDtypeStruct((1, d), F32)],
        compiler_params=_params(dimension_semantics=("arbitrary",)))(dxo, mix, w_out, g)


def _alibi_slopes():
    return np.array([2.0 ** (-8.0 * (i + 1) / N_ATTN_HEADS) for i in range(N_ATTN_HEADS)], np.float32)


def _block_diag(w_lin):
    n, c, _ = w_lin.shape
    eye = jnp.eye(n, dtype=w_lin.dtype)
    return (eye[:, None, :, None] * w_lin[:, :, None, :]).reshape(n * c, n * c)


class _NoExchange:
    def __init__(self, full):
        self.full, self.grads = full, {}

    def first_weights(self):
        return self.full

    def rider(self, host):
        return None

    def landed(self, host, results):
        pass

    def rest_weights(self):
        return self.full

    def gradient(self, name, grad):
        self.grads[name] = grad


def _local_step(x, target, small, exchange):
    s, d = x.shape
    slopes = _alibi_slopes()
    group_slopes = [jnp.asarray(slopes[4 * g:4 * g + 4]) for g in range(3)]
    w_bd = _block_diag(small["w_pool_lin"]).astype(BF16)
    head_ones = jnp.asarray(np.kron(np.eye(GROUP_DIM // HEAD_DIM), np.ones((HEAD_DIM, HEAD_DIM))), BF16)

    full = exchange.first_weights()
    (x1, a1, b1, f1), riding = _ffn_fwd(x, small["g_ffn1_pre"], full["w1_gate"], full["w1_up"], full["w1_down"],
                                        small["g_ffn1_post"], None, "ffn1_fwd", exchange.rider("ffn1_fwd"))
    exchange.landed("ffn1_fwd", riding)
    full = {**full, **exchange.rest_weights()}
    u, *parts = _in_fwd(x1, small["g_mix_pre"], full["w_in"], "in_fwd")
    qs, ks, vs = parts[0:3], parts[3:6], parts[6:9]
    a_pool = _pool_fwd(u, w_bd, small["pool_scale"], "pool_fwd")
    outs, lses = [], []
    for g, dil in enumerate(DILATIONS):
        o, lse = _attn_fwd(qs[g], ks[g], vs[g], group_slopes[g], dil, f"attn_fwd{g}")
        outs.append(o)
        lses.append(lse)
    cat = _combine_fwd(a_pool, outs, lses, "combine_fwd")
    x2, mix = _out_fwd(cat, x1, full["w_out"], small["g_mix_post"], "out_fwd")
    (dx3, a2, b2, f2, loss_part), _ = _ffn_fwd(x2, small["g_ffn2_pre"], full["w2_gate"], full["w2_up"], full["w2_down"],
                                               small["g_ffn2_post"], target, "ffn2_fwd")

    small_grads = {}

    def hosted(call, host, *args):
        results, riding = call(*args, host, exchange.rider(host))
        exchange.landed(host, riding)
        return results

    def ffn_backward(tag, dxo, x_in, f, a, b):
        n = tag[-1]
        dx, hh, da, db, df, h, dg_pre, dg_post = hosted(
            _ffn_bwd, f"{tag}_bwd", dxo, x_in, f, a, b, small[f"g_{tag}_pre"], small[f"g_{tag}_post"],
            full[f"w{n}_gate"], full[f"w{n}_up"], full[f"w{n}_down"])
        for part, lhs, rhs in (("down", hh, df), ("gate", da, h), ("up", db, h)):
            exchange.gradient(f"w{n}_{part}", hosted(_wgrad, f"{tag}_wgrad_{part}", lhs, rhs))
        small_grads[f"g_{tag}_pre"], small_grads[f"g_{tag}_post"] = dg_pre, dg_post
        return dx

    dx2 = ffn_backward("ffn2", dx3, x2, f2, a2, b2)
    dcat, dmix, small_grads["g_mix_post"] = _out_bwd(dx2, mix, full["w_out"], small["g_mix_post"], "out_bwd")
    exchange.gradient("w_out", hosted(_wgrad, "wgrad_out", cat, dmix))
    dos_cs = _combine_bwd(dcat, outs, lses, head_ones, "combine_bwd")
    dos, cs = dos_cs[:3], dos_cs[3:]
    dqs, dks, dvs = [], [], []
    for g, dil in enumerate(DILATIONS):
        dq, dk, dv = _attn_bwd(qs[g], ks[g], vs[g], dos[g], lses[g], cs[g], group_slopes[g], dil, f"attn_bwd{g}")
        dqs.append(dq)
        dks.append(dk)
        dvs.append(dv)
    du, dw_bd, small_grads["pool_scale"] = _pool_bwd(u, dcat, w_bd, small["pool_scale"], "pool_bwd")
    n_pool = len(POOL_HALF_WINDOWS)
    small_grads["w_pool_lin"] = jnp.stack(
        [dw_bd[HEAD_DIM * g:HEAD_DIM * (g + 1), HEAD_DIM * g:HEAD_DIM * (g + 1)] for g in range(n_pool)])
    dx1, dz, h2, small_grads["g_mix_pre"] = hosted(_in_bwd, "in_bwd", du, dqs + dks + dvs, x1, dx2, small["g_mix_pre"], full["w_in"])
    exchange.gradient("w_in", hosted(_wgrad, "wgrad_in", dz, h2))
    dx0 = ffn_backward("ffn1", dx1, x, f1, a1, b1)
    return loss_part[0, 0], dx0, small_grads


SEGMENTS = ("w1_gate", "w1_up", "w1_down", "w_in", "w_out", "w2_gate", "w2_up", "w2_down")
TRANSPOSED = ("w1_gate", "w1_up", "w_in", "w2_gate", "w2_up")
HALF = 512


def _place():
    x, y, c = lax.axis_index("x"), lax.axis_index("y"), lax.axis_index("c")
    other_chips = [(1 - x, y), (x, 1 - y), (1 - x, 1 - y)]
    return x, y, c, other_chips


def _chip_rows(chip, rows):
    return pl.ds(pl.multiple_of((2 * chip[0] + chip[1]) * rows, 16), rows)


def _cols(c):
    return pl.ds(pl.multiple_of(c * HALF, HALF), HALF)


def _cast_shard(w, transpose, place, name, tm=256):
    r, c = w.shape
    if transpose:
        def body(place_ref, w_ref, o_ref):
            o_ref[...] = w_ref[...].T.astype(BF16)

        grid, in_block, out_block, rows = (r // tm,), (tm, c), (c, tm), c
    else:
        def body(place_ref, w_ref, o_ref):
            o_ref[...] = w_ref[...].astype(BF16)

        grid, in_block, out_block, rows = (1,), (r, c), (r, c), r
    return pl.pallas_call(
        body, name=name,
        grid_spec=pltpu.PrefetchScalarGridSpec(
            num_scalar_prefetch=1, grid=grid, in_specs=[pl.BlockSpec(in_block, lambda i, place: (i, 0))],
            out_specs=pl.BlockSpec(out_block, lambda i, place: (place[0], i))),
        out_shape=jax.ShapeDtypeStruct((N_CHIPS * rows, 1024), BF16),
        compiler_params=_params(dimension_semantics=("arbitrary",)))(place, w)


def _gather_weights(bufs):
    n = len(bufs)
    rows = [b.shape[0] // N_CHIPS for b in bufs]

    def body(*refs):
        outs = refs[n:2 * n]
        send_sems, recv_sems, fwd_send_sems, fwd_recv_sems = refs[2 * n:]
        x, y, c, chips = _place()
        me = (x, y)

        def ici(j, k, src_chip, to):
            blk = outs[k].at[_chip_rows(src_chip, rows[k]), _cols(c)]
            return pltpu.make_async_remote_copy(src_ref=blk, dst_ref=blk, send_sem=send_sems.at[j, k], recv_sem=recv_sems.at[j, k],
                                                device_id=to, device_id_type=MESH)

        def d2d(j, k, src_chip, half):
            blk = outs[k].at[_chip_rows(src_chip, rows[k]), _cols(half)]
            return pltpu.make_async_remote_copy(src_ref=blk, dst_ref=blk, send_sem=fwd_send_sems.at[j, k],
                                                recv_sem=fwd_recv_sems.at[j, k], device_id=(x, y, 1 - c), device_id_type=MESH)

        sends = [ici(j, k, me, (*chip, c)) for j, chip in enumerate(chips) for k in range(n)]
        for cp in sends:
            cp.start()
        forwards = []
        for j, chip in enumerate(chips):
            for k in range(n):
                ici(j, k, chip, (x, y, c)).wait_recv()
                fw = d2d(j, k, chip, c)
                fw.start()
                forwards.append(fw)
        for j, chip in enumerate(chips):
            for k in range(n):
                d2d(j, k, chip, 1 - c).wait_recv()
        for cp in sends + forwards:
            cp.wait_send()

    any_spec = pl.BlockSpec(memory_space=pl.ANY)
    return pl.pallas_call(
        body, name="gather_weights", in_specs=[any_spec] * n, out_specs=[any_spec] * n,
        out_shape=[jax.ShapeDtypeStruct(b.shape, b.dtype) for b in bufs], input_output_aliases={k: k for k in range(n)},
        scratch_shapes=[pltpu.SemaphoreType.DMA((3, n)), pltpu.SemaphoreType.DMA((3, n)),
                        pltpu.SemaphoreType.DMA((3, n)), pltpu.SemaphoreType.DMA((3, n))])(*bufs)


def _gather_rider(bufs):
    n = len(bufs)
    rows = [b.shape[0] // N_CHIPS for b in bufs]

    def copies(outs, send_sems, recv_sems, inbound):
        x, y, c, chips = _place()
        for j, chip in enumerate(chips):
            for k in range(n):
                src_chip = chip if inbound else (x, y)
                blk = outs[k].at[_chip_rows(src_chip, rows[k]), _cols(c)]
                yield pltpu.make_async_remote_copy(src_ref=blk, dst_ref=blk, send_sem=send_sems.at[j, k], recv_sem=recv_sems.at[j, k],
                                                   device_id=(*chip, c), device_id_type=MESH)

    def start(ins, outs, send_sems, recv_sems):
        for cp in copies(outs, send_sems, recv_sems, False):
            cp.start()

    def wait(ins, outs, send_sems, recv_sems):
        for cp in copies(outs, send_sems, recv_sems, True):
            cp.wait_recv()
        for cp in copies(outs, send_sems, recv_sems, False):
            cp.wait_send()

    return _Rider(list(bufs), None, (3, n), start, wait)


def _forward_halves(bufs, name):
    n = len(bufs)
    rows = [b.shape[0] // N_CHIPS for b in bufs]

    def body(*refs):
        outs = refs[n:2 * n]
        send_sems, recv_sems = refs[2 * n:]
        x, y, c, chips = _place()

        def d2d(j, k, chip, half):
            blk = outs[k].at[_chip_rows(chip, rows[k]), _cols(half)]
            return pltpu.make_async_remote_copy(src_ref=blk, dst_ref=blk, send_sem=send_sems.at[j, k], recv_sem=recv_sems.at[j, k],
                                                device_id=(x, y, 1 - c), device_id_type=MESH)

        forwards = [d2d(j, k, chip, c) for j, chip in enumerate(chips) for k in range(n)]
        for cp in forwards:
            cp.start()
        for j, chip in enumerate(chips):
            for k in range(n):
                d2d(j, k, chip, 1 - c).wait_recv()
        for cp in forwards:
            cp.wait_send()

    any_spec = pl.BlockSpec(memory_space=pl.ANY)
    return pl.pallas_call(
        body, name=name, in_specs=[any_spec] * n, out_specs=[any_spec] * n,
        out_shape=[jax.ShapeDtypeStruct(b.shape, b.dtype) for b in bufs], input_output_aliases={k: k for k in range(n)},
        scratch_shapes=[pltpu.SemaphoreType.DMA((3, n)), pltpu.SemaphoreType.DMA((3, n))])(*bufs)


def _sibling_halves(grads, name):
    n = len(grads)

    def body(*refs):
        ins, outs = refs[:n], refs[n:2 * n]
        send_sems, recv_sems = refs[2 * n:]
        x, y, c, _ = _place()
        copies = [pltpu.make_async_remote_copy(src_ref=ins[k].at[:, pl.ds(1 - c, 1)], dst_ref=outs[k], send_sem=send_sems.at[k],
                                               recv_sem=recv_sems.at[k], device_id=(x, y, 1 - c), device_id_type=MESH)
                  for k in range(n)]
        for cp in copies:
            cp.start()
        for cp in copies:
            cp.wait()

    any_spec = pl.BlockSpec(memory_space=pl.ANY)
    return pl.pallas_call(
        body, name=name, in_specs=[any_spec] * n, out_specs=[any_spec] * n,
        out_shape=[jax.ShapeDtypeStruct((N_CHIPS, 1) + g.shape[2:], F32) for g in grads],
        scratch_shapes=[pltpu.SemaphoreType.DMA((n,)), pltpu.SemaphoreType.DMA((n,))])(*grads)


def _chip_sum(grad, from_sibling, place, name):
    rh, width = grad.shape[2:]

    def body(place_ref, g_ref, s_ref, own_ref, all_ref):
        total = g_ref[0, 0] + s_ref[0, 0]
        all_ref[0, 0] = total.astype(BF16)

        @pl.when(pl.program_id(0) == place_ref[0])
        def _():
            own_ref[0] = total

    blk = (1, 1, rh, width)
    return pl.pallas_call(
        body, name=name,
        grid_spec=pltpu.PrefetchScalarGridSpec(
            num_scalar_prefetch=1, grid=(N_CHIPS,),
            in_specs=[pl.BlockSpec(blk, lambda p, place: (p, place[1], 0, 0)), pl.BlockSpec(blk, lambda p, place: (p, 0, 0, 0))],
            out_specs=[pl.BlockSpec((1, rh, width), lambda p, place: (0, 0, 0)), pl.BlockSpec(blk, lambda p, place: (p, 0, 0, 0))]),
        out_shape=[jax.ShapeDtypeStruct((1, rh, width), F32), jax.ShapeDtypeStruct((N_CHIPS, 1, rh, width), BF16)],
        compiler_params=_params(dimension_semantics=("arbitrary",)))(place, grad, from_sibling)


def _scatter_rider(sums):
    n = len(sums)

    def copies(ins, outs, send_sems, recv_sems):
        x, y, c, chips = _place()
        return [pltpu.make_async_remote_copy(src_ref=ins[k].at[pl.ds(2 * chip[0] + chip[1], 1)], dst_ref=outs[k].at[pl.ds(j, 1)],
                                             send_sem=send_sems.at[j, k], recv_sem=recv_sems.at[j, k],
                                             device_id=(*chip, c), device_id_type=MESH)
                for j, chip in enumerate(chips) for k in range(n)]

    def start(*refs):
        for cp in copies(*refs):
            cp.start()

    def wait(*refs):
        for cp in copies(*refs):
            cp.wait()

    return _Rider(list(sums), [jax.ShapeDtypeStruct((3,) + sm.shape[1:], BF16) for sm in sums], (3, n), start, wait)


def _scatter_chip_sums(sums, name):
    n = len(sums)
    rider = _scatter_rider(sums)

    def body(*refs):
        rider.start(refs[:n], refs[n:2 * n], *refs[2 * n:])
        rider.wait(refs[:n], refs[n:2 * n], *refs[2 * n:])

    any_spec = pl.BlockSpec(memory_space=pl.ANY)
    return pl.pallas_call(body, name=name, in_specs=[any_spec] * n, out_specs=[any_spec] * n, out_shape=rider.landing,
                          scratch_shapes=[pltpu.SemaphoreType.DMA(rider.sems)] * 2)(*sums)


def _total_sum(own, received, name):
    def body(o_ref, r_ref, t_ref):
        total = o_ref[0]
        for j in range(3):
            total = total + r_ref[j, 0].astype(F32)
        t_ref[0] = total

    return pl.pallas_call(body, name=name, out_shape=jax.ShapeDtypeStruct(own.shape, F32), compiler_params=_params())(own, received)


def _swap_halves(halves):
    n = len(halves)

    def body(*refs):
        ins, outs = refs[:n], refs[n:2 * n]
        send_sems, recv_sems = refs[2 * n:]
        x, y, c, _ = _place()
        copies = [pltpu.make_async_remote_copy(src_ref=ins[k], dst_ref=outs[k], send_sem=send_sems.at[k],
                                               recv_sem=recv_sems.at[k], device_id=(x, y, 1 - c), device_id_type=MESH)
                  for k in range(n)]
        for cp in copies:
            cp.start()
        for cp in copies:
            cp.wait()

    any_spec = pl.BlockSpec(memory_space=pl.ANY)
    return pl.pallas_call(
        body, name="swap_halves", in_specs=[any_spec] * n, out_specs=[any_spec] * n,
        out_shape=[jax.ShapeDtypeStruct(h.shape, F32) for h in halves],
        scratch_shapes=[pltpu.SemaphoreType.DMA((n,)), pltpu.SemaphoreType.DMA((n,))])(*halves)


N_DEV = 8


def _gather_small(block):
    m_per, width = block.shape

    def body(x_ref, out_ref, send_sems, recv_sems, local_sem):
        x, y, c, chips = _place()
        me, sibling = (x, y, c), (x, y, 1 - c)

        def rows(px, py, pc):
            return out_ref.at[pl.ds((4 * px + 2 * py + pc) * m_per, m_per), :]

        def copy(k, blk, to, src=None):
            return pltpu.make_async_remote_copy(src_ref=rows(*blk) if src is None else src, dst_ref=rows(*blk),
                                                send_sem=send_sems.at[k], recv_sem=recv_sems.at[k], device_id=to, device_id_type=MESH)

        mine = pltpu.make_async_copy(x_ref, rows(*me), local_sem)
        mine.start()
        first = [copy(0, me, sibling, src=x_ref)] + [copy(1 + j, me, (*chip, c), src=x_ref) for j, chip in enumerate(chips)]
        for cp in first:
            cp.start()
        passed = [copy(4 + j, (*chip, c), sibling) for j, chip in enumerate(chips)]
        for j, chip in enumerate(chips):
            copy(1 + j, (*chip, c), me).wait_recv()
            passed[j].start()
        copy(0, sibling, me).wait_recv()
        for j, chip in enumerate(chips):
            copy(4 + j, (*chip, 1 - c), me).wait_recv()
        for cp in first + passed:
            cp.wait_send()
        mine.wait()

    vmem = pl.BlockSpec(memory_space=pltpu.VMEM)
    return pl.pallas_call(body, name="gather_small", out_shape=jax.ShapeDtypeStruct((N_DEV * m_per, width), F32),
                          in_specs=[vmem], out_specs=vmem,
                          scratch_shapes=[pltpu.SemaphoreType.DMA((7,)), pltpu.SemaphoreType.DMA((7,)),
                                          pltpu.SemaphoreType.DMA])(block)


def _adamw_math(w, g, m, v):
    m = ADAM_B1 * m + (1.0 - ADAM_B1) * g
    v = ADAM_B2 * v + (1.0 - ADAM_B2) * (g * g)
    m_hat = m / (1.0 - ADAM_B1 ** ADAM_STEP)
    v_hat = v / (1.0 - ADAM_B2 ** ADAM_STEP)
    delta = -ADAM_LR * (m_hat / (jnp.sqrt(v_hat) + ADAM_EPS) + ADAM_WD * w)
    return delta, m, v


def _adamw(w, mine, siblings, place, m, v, transposed, name):
    def body(place_ref, w_ref, mine_ref, sib_ref, m_ref, v_ref, go_ref, d_ref, mo_ref, vo_ref):
        first = place_ref[1] == 0
        g = jnp.concatenate([jnp.where(first, mine_ref[0], sib_ref[0]), jnp.where(first, sib_ref[0], mine_ref[0])], axis=0)
        g = g.T if transposed else g
        go_ref[...] = g
        d_ref[...], mo_ref[...], vo_ref[...] = _adamw_math(w_ref[...], g, m_ref[...], v_ref[...])

    vmem = pl.BlockSpec(memory_space=pltpu.VMEM)
    return pl.pallas_call(body, name=name, in_specs=[pl.BlockSpec(memory_space=pltpu.SMEM)] + [vmem] * 5, out_specs=[vmem] * 4,
                          out_shape=[jax.ShapeDtypeStruct(w.shape, F32)] * 4, compiler_params=_params())(
                              place, w, mine, siblings, m, v)


def _adamw_small(gathered, w, m, v, name):
    def body(ga_ref, w_ref, m_ref, v_ref, go_ref, d_ref, mo_ref, vo_ref):
        g = ga_ref[0]
        for dev in range(1, N_DEV):
            g = g + ga_ref[dev]
        go_ref[...] = g
        d_ref[...], mo_ref[...], vo_ref[...] = _adamw_math(w_ref[...], g, m_ref[...], v_ref[...])

    return pl.pallas_call(body, name=name, out_shape=[jax.ShapeDtypeStruct(w.shape, F32)] * 4,
                          compiler_params=_params())(gathered, w, m, v)


class _Exchange:
    FIRST = ("w1_gate", "w1_up", "w1_down")
    HOSTS = {"in_bwd": ("w2_down", "w2_gate", "w2_up"), "ffn1_wgrad_down": ("w_out", "w_in"), "ffn1_wgrad_gate": ("w1_down",),
             "ffn1_wgrad_up": ("w1_gate",)}

    def __init__(self, bufs, place):
        self.bufs, self.place = bufs, place
        self.later = [k for k in SEGMENTS if k not in self.FIRST]
        self.own, self.to_send, self.received = {}, {}, {}

    def first_weights(self):
        return dict(zip(self.FIRST, _gather_weights([self.bufs[k] for k in self.FIRST])))

    def rider(self, host):
        if host == "ffn1_fwd":
            return _gather_rider([self.bufs[k] for k in self.later])
        if host in self.HOSTS:
            return _scatter_rider([self.to_send[k] for k in self.HOSTS[host]])
        return None

    def landed(self, host, results):
        if host == "ffn1_fwd":
            self.rest = dict(zip(self.later, _forward_halves(results, "gather_rest_forward")))
        elif host in self.HOSTS:
            self.received.update(zip(self.HOSTS[host], results))

    def rest_weights(self):
        return self.rest

    def gradient(self, name, grad):
        split = grad.reshape(N_CHIPS, 2, grad.shape[0] // (2 * N_CHIPS), grad.shape[1])
        (from_sibling,) = _sibling_halves([split], f"reduce_sibling_{name}")
        self.own[name], self.to_send[name] = _chip_sum(split, from_sibling, self.place, f"chip_sum_{name}")

    def summed_halves(self):
        late = [k for k in SEGMENTS if k not in self.received]
        self.received.update(zip(late, _scatter_chip_sums([self.to_send[k] for k in late], "reduce_chips_last")))
        return [_total_sum(self.own[k], self.received[k], f"total_{k}") for k in SEGMENTS]


SMALL = ("g_ffn1_pre", "g_ffn1_post", "g_mix_pre", "w_pool_lin", "pool_scale", "g_mix_post", "g_ffn2_pre", "g_ffn2_post")
WEIGHTS = ("g_ffn1_pre", "w1_gate", "w1_up", "w1_down", "g_ffn1_post", "g_mix_pre", "w_in", "w_pool_lin", "pool_scale", "w_out",
           "g_mix_post", "g_ffn2_pre", "w2_gate", "w2_up", "w2_down", "g_ffn2_post")
LANES = 128


def _pack_small(tree):
    flat = jnp.concatenate([tree[k].reshape(-1) for k in SMALL])
    rows = -(-flat.shape[0] // (8 * LANES)) * 8
    return jnp.pad(flat, (0, rows * LANES - flat.shape[0])).reshape(rows, LANES)


def _unpack_small(packed, like):
    flat, out, at = packed.reshape(-1), {}, 0
    for k in SMALL:
        size = math.prod(like[k].shape)
        out[k] = flat[at:at + size].reshape(like[k].shape)
        at += size
    return out


def kernel(x, g_ffn1_pre, w1_gate, w1_up, w1_down, g_ffn1_post, g_mix_pre, w_in, w_pool_lin, pool_scale, w_out, g_mix_post, g_ffn2_pre, w2_gate, w2_up, w2_down, g_ffn2_post, loss_target, m_g_ffn1_pre, m_w1_gate, m_w1_up, m_w1_down, m_g_ffn1_post, m_g_mix_pre, m_w_in, m_w_pool_lin, m_pool_scale, m_w_out, m_g_mix_post, m_g_ffn2_pre, m_w2_gate, m_w2_up, m_w2_down, m_g_ffn2_post, v_g_ffn1_pre, v_w1_gate, v_w1_up, v_w1_down, v_g_ffn1_post, v_g_mix_pre, v_w_in, v_w_pool_lin, v_pool_scale, v_w_out, v_g_mix_post, v_g_ffn2_pre, v_w2_gate, v_w2_up, v_w2_down, v_g_ffn2_post):
    given = dict(locals())
    w = {k: given[k] for k in WEIGHTS}
    m = {k: given["m_" + k] for k in WEIGHTS}
    v = {k: given["v_" + k] for k in WEIGHTS}
    small = {k: (w[k][0] if k == "w_pool_lin" else w[k].reshape(1, -1)) for k in SMALL}

    place = jnp.stack([2 * lax.axis_index("x") + lax.axis_index("y"), lax.axis_index("c")]).astype(jnp.int32)
    exchange = _Exchange({k: _cast_shard(w[k][0], k in TRANSPOSED, place, f"cast_{k}") for k in SEGMENTS}, place)
    loss_part, grad_x, small_grads = _local_step(x[0], loss_target[0], small, exchange)
    loss = lax.psum(loss_part, ("x", "y", "c"))

    halves = exchange.summed_halves()
    from_sibling = _swap_halves(halves)

    out_grad, out_delta, out_m, out_v = {}, {}, {}, {}
    for k, mine, sib in zip(SEGMENTS, halves, from_sibling):
        out_grad[k], out_delta[k], out_m[k], out_v[k] = (
            a[None] for a in _adamw(w[k][0], mine, sib, place, m[k][0], v[k][0], k in TRANSPOSED, f"adamw_{k}"))

    small_grads["w_pool_lin"] = small_grads["w_pool_lin"][None]
    packed = _pack_small(small_grads)
    gathered = _gather_small(packed).reshape(N_DEV, *packed.shape)
    like = {k: w[k] for k in SMALL}
    results = _adamw_small(gathered, _pack_small(like), _pack_small({k: m[k] for k in SMALL}),
                           _pack_small({k: v[k] for k in SMALL}), "adamw_small")
    for tree, res in zip((out_grad, out_delta, out_m, out_v), results):
        tree.update(_unpack_small(res, like))

    return (loss, grad_x[None], *[out_grad[k] for k in WEIGHTS], *[out_delta[k] for k in WEIGHTS],
            *[out_m[k] for k in WEIGHTS], *[out_v[k] for k in WEIGHTS])
```

```python
import math
import typing

import numpy as np
import jax
import jax.numpy as jnp
from jax import lax
from jax.experimental import pallas as pl
from jax.experimental.pallas import tpu as pltpu

F32 = jnp.float32
BF16 = jnp.bfloat16
MESH = pl.DeviceIdType.MESH

RMS_EPS = 1e-6
HEAD_DIM = 64
POOL_HALF_WINDOWS = (1, 2, 4, 8)
POOL_DIM = 256
GROUP_DIM = 256
DILATIONS = (1, 4, 16)
N_SIDE = 64
N_ATTN_HEADS = 12
ADAM_LR, ADAM_B1, ADAM_B2, ADAM_EPS, ADAM_WD, ADAM_STEP = 0.001, 0.9, 0.999, 1e-08, 0.01, 10

N_CHIPS = 4
V7X_VMEM_LIMIT = 60 * 1024 * 1024

_NT = (((1,), (1,)), ((), ()))
_TN = (((0,), (0,)), ((), ()))


def _dot(a, b):
    return jnp.dot(a, b, preferred_element_type=F32)


def _dot_nt(a, b):
    return lax.dot_general(a, b, _NT, preferred_element_type=F32)


def _dot_tn(a, b):
    return lax.dot_general(a, b, _TN, preferred_element_type=F32)


def _params(**kw):
    return pltpu.CompilerParams(vmem_limit_bytes=V7X_VMEM_LIMIT, **kw)


def _rows(tm, width):
    return pl.BlockSpec((tm, width), lambda i: (i, 0))


def _resident(shape):
    return pl.BlockSpec(shape, lambda i: (0,) * len(shape), pipeline_mode=pl.Buffered(1))


def _const(shape):
    return pl.BlockSpec(shape, lambda i: (0,) * len(shape))


def _tile(rows, cap):
    return max(t for t in range(16, cap + 1, 16) if rows % t == 0)


def _inv_rms(x):
    return lax.rsqrt(jnp.mean(x * x, axis=-1, keepdims=True) + RMS_EPS)


def _rms_bwd(x, inv, g, dy):
    n = x * inv
    dn = dy * g
    dx = inv * (dn - n * jnp.mean(dn * n, axis=-1, keepdims=True))
    return dx, jnp.sum(dy * n, axis=0, keepdims=True)


def _accumulate(ref, value):
    @pl.when(pl.program_id(0) == 0)
    def _():
        ref[...] = jnp.zeros_like(ref)

    ref[...] += value


class _Rider(typing.NamedTuple):
    operands: list
    landing: typing.Optional[list]
    sems: tuple
    start: typing.Callable
    wait: typing.Callable


def _hosted_call(body, riders, *, name, steps, in_specs, out_specs, out_shape, args, scratch_shapes=()):
    params = _params(dimension_semantics=("arbitrary",))
    riders = list(riders or [])
    if not riders:
        res = pl.pallas_call(body, name=name, grid=(steps,), in_specs=in_specs, out_specs=out_specs, out_shape=out_shape,
                             scratch_shapes=list(scratch_shapes), compiler_params=params)(*args)
        return list(res), []
    n_in, n_out, n_scratch = len(in_specs), len(out_specs), len(scratch_shapes)
    operands, landing, aliases, spans = [], [], {}, []
    for rd in riders:
        lands = rd.landing if rd.landing is not None else [jax.ShapeDtypeStruct(a.shape, a.dtype) for a in rd.operands]
        if rd.landing is None:
            aliases.update({n_in + len(operands) + i: n_out + len(landing) + i for i in range(len(lands))})
        spans.append((len(operands), len(rd.operands), len(landing), len(lands)))
        operands += rd.operands
        landing += lands
    outs_at = n_in + len(operands)
    scratch_at = outs_at + n_out + len(landing)

    def riding(*refs):
        def each(action):
            for i, (rd, (in_at, n_ops, out_at, n_lands)) in enumerate(zip(riders, spans)):
                sems = refs[scratch_at + n_scratch + 2 * i:scratch_at + n_scratch + 2 * i + 2]
                getattr(rd, action)(refs[n_in + in_at:n_in + in_at + n_ops],
                                    refs[outs_at + n_out + out_at:outs_at + n_out + out_at + n_lands], *sems)

        @pl.when(pl.program_id(0) == 0)
        def _():
            each("start")

        body(*refs[:n_in], *refs[outs_at:outs_at + n_out], *refs[scratch_at:scratch_at + n_scratch])

        @pl.when(pl.program_id(0) == steps - 1)
        def _():
            each("wait")

    any_spec = pl.BlockSpec(memory_space=pl.ANY)
    res = pl.pallas_call(
        riding, name=name, grid=(steps,), in_specs=list(in_specs) + [any_spec] * len(operands),
        out_specs=list(out_specs) + [any_spec] * len(landing), out_shape=list(out_shape) + landing,
        scratch_shapes=list(scratch_shapes) + [pltpu.SemaphoreType.DMA(rd.sems) for rd in riders for _ in range(2)],
        input_output_aliases=aliases, compiler_params=params)(*args, *operands)
    return list(res[:n_out]), [list(res[n_out + out_at:n_out + out_at + n_lands]) for _, _, out_at, n_lands in spans]


_SUB_TILE = 256


def _sub_tiles(tm):
    return [pl.ds(r, _SUB_TILE) for r in range(0, tm, _SUB_TILE)]


def _ffn_fwd(x, g_pre, wg_t, wu_t, wd, g_post, target, name, riders=None, tm=512):
    s, d = x.shape
    ff = wd.shape[0]
    with_loss = target is not None

    def body(*refs):
        if with_loss:
            x_ref, gpre_ref, wg_ref, wu_ref, wd_ref, gpost_ref, t_ref, xo_ref, a_ref, b_ref, f_ref, loss_ref = refs
        else:
            x_ref, gpre_ref, wg_ref, wu_ref, wd_ref, gpost_ref, xo_ref, a_ref, b_ref, f_ref = refs
        loss = 0.0
        for rows in _sub_tiles(tm):
            xv = x_ref[rows, :]
            hb = (xv * _inv_rms(xv) * gpre_ref[...]).astype(BF16)
            a = _dot_nt(hb, wg_ref[...])
            b = _dot_nt(hb, wu_ref[...])
            hh = (a * jax.nn.sigmoid(a)) * b
            f = _dot(hh.astype(BF16), wd_ref[...])
            xo = xv + 0.5 * (f * _inv_rms(f) * gpost_ref[...])
            a_ref[rows, :] = a.astype(BF16)
            b_ref[rows, :] = b.astype(BF16)
            f_ref[rows, :] = f
            if with_loss:
                e = xo - t_ref[rows, :]
                xo_ref[rows, :] = e * (1.0 / d)
                loss = loss + 0.5 * jnp.sum(jnp.mean(e * e, axis=-1, keepdims=True))
            else:
                xo_ref[rows, :] = xo
        if with_loss:
            _accumulate(loss_ref, loss)

    in_specs = [_rows(tm, d), _const((1, d)), _resident((ff, d)), _resident((ff, d)), _resident((ff, d)), _const((1, d))]
    args = [x, g_pre, wg_t, wu_t, wd, g_post]
    out_shape = [jax.ShapeDtypeStruct((s, d), F32), jax.ShapeDtypeStruct((s, ff), BF16),
                 jax.ShapeDtypeStruct((s, ff), BF16), jax.ShapeDtypeStruct((s, d), F32)]
    out_specs = [_rows(tm, d), _rows(tm, ff), _rows(tm, ff), _rows(tm, d)]
    if with_loss:
        in_specs.append(_rows(tm, d))
        args.append(target)
        out_shape.append(jax.ShapeDtypeStruct((8, 128), F32))
        out_specs.append(_const((8, 128)))
    return _hosted_call(body, riders, name=name, steps=s // tm, in_specs=in_specs, out_specs=out_specs, out_shape=out_shape, args=args)


def _ffn_bwd(dxo, x, f, a, b, g_pre, g_post, wg_t, wu_t, wd, name, riders=None, tm=256):
    s, d = x.shape
    ff = wd.shape[0]

    def body(dxo_ref, x_ref, f_ref, a_ref, b_ref, gpre_ref, gpost_ref, wg_ref, wu_ref, wd_ref,
             dx_ref, hh_ref, da_ref, db_ref, df_ref, h_ref, dgpre_ref, dgpost_ref):
        dgpre_sum = dgpost_sum = 0.0
        for rows in _sub_tiles(tm):
            dxo_v = dxo_ref[rows, :]
            fv = f_ref[rows, :]
            df, dgpost = _rms_bwd(fv, _inv_rms(fv), gpost_ref[...], 0.5 * dxo_v)
            dfb = df.astype(BF16)
            dhh = _dot_nt(dfb, wd_ref[...])
            av = a_ref[rows, :].astype(F32)
            bv = b_ref[rows, :].astype(F32)
            sig = jax.nn.sigmoid(av)
            sa = av * sig
            da = (dhh * bv * (sig * (1.0 + av * (1.0 - sig)))).astype(BF16)
            db = (dhh * sa).astype(BF16)
            dh = _dot(da, wg_ref[...]) + _dot(db, wu_ref[...])
            xv = x_ref[rows, :]
            inv = _inv_rms(xv)
            dxn, dgpre = _rms_bwd(xv, inv, gpre_ref[...], dh)
            dx_ref[rows, :] = dxo_v + dxn
            hh_ref[rows, :] = (sa * bv).astype(BF16)
            da_ref[rows, :] = da
            db_ref[rows, :] = db
            df_ref[rows, :] = dfb
            h_ref[rows, :] = (xv * inv * gpre_ref[...]).astype(BF16)
            dgpre_sum, dgpost_sum = dgpre_sum + dgpre, dgpost_sum + dgpost
        _accumulate(dgpre_ref, dgpre_sum)
        _accumulate(dgpost_ref, dgpost_sum)

    return _hosted_call(
        body, riders, name=name, steps=s // tm,
        in_specs=[_rows(tm, d), _rows(tm, d), _rows(tm, d), _rows(tm, ff), _rows(tm, ff), _const((1, d)), _const((1, d)),
                  _resident((ff, d)), _resident((ff, d)), _resident((ff, d))],
        out_specs=[_rows(tm, d), _rows(tm, ff), _rows(tm, ff), _rows(tm, ff), _rows(tm, d), _rows(tm, d),
                   _const((1, d)), _const((1, d))],
        out_shape=[jax.ShapeDtypeStruct((s, d), F32), jax.ShapeDtypeStruct((s, ff), BF16), jax.ShapeDtypeStruct((s, ff), BF16),
                   jax.ShapeDtypeStruct((s, ff), BF16), jax.ShapeDtypeStruct((s, d), BF16), jax.ShapeDtypeStruct((s, d), BF16),
                   jax.ShapeDtypeStruct((1, d), F32), jax.ShapeDtypeStruct((1, d), F32)],
        args=[dxo, x, f, a, b, g_pre, g_post, wg_t, wu_t, wd])


def _wgrad(lhs, rhs, name, riders=None, rt=256):
    s, r = lhs.shape
    c = rhs.shape[1]

    def body(l_ref, r_ref, o_ref):
        o_ref[...] = _dot_tn(l_ref[...], r_ref[...])

    (out,), riding = _hosted_call(
        body, riders, name=name, steps=pl.cdiv(r, rt), in_specs=[pl.BlockSpec((s, rt), lambda i: (0, i)), _resident((s, c))],
        out_specs=[pl.BlockSpec((rt, c), lambda i: (i, 0))], out_shape=[jax.ShapeDtypeStruct((r, c), F32)], args=[lhs, rhs])
    return out, riding


def _in_fwd(x, g, w_in_t, name, tm=512):
    s, d = x.shape
    d_in = w_in_t.shape[0]
    n_parts = (d_in - POOL_DIM) // GROUP_DIM

    def body(x_ref, g_ref, w_ref, u_ref, *part_refs):
        xv = x_ref[...]
        hb = (xv * _inv_rms(xv) * g_ref[...]).astype(BF16)
        z = _dot_nt(hb, w_ref[...])
        u_ref[...] = z[:, :POOL_DIM]
        for j, ref in enumerate(part_refs):
            ref[...] = z[:, POOL_DIM + GROUP_DIM * j:POOL_DIM + GROUP_DIM * (j + 1)]

    return pl.pallas_call(
        body, name=name, grid=(s // tm,), in_specs=[_rows(tm, d), _const((1, d)), _resident((d_in, d))],
        out_specs=[_rows(tm, POOL_DIM)] + [_rows(tm, GROUP_DIM)] * n_parts,
        out_shape=[jax.ShapeDtypeStruct((s, POOL_DIM), F32)] + [jax.ShapeDtypeStruct((s, GROUP_DIM), F32)] * n_parts,
        compiler_params=_params(dimension_semantics=("arbitrary",)))(x, g, w_in_t)


def _in_bwd(du, dparts, x, dxo, g, w_in_t, name, riders=None, tm=512):
    s, d = x.shape
    d_in = w_in_t.shape[0]
    n_parts = len(dparts)

    def body(du_ref, *refs):
        part_refs = refs[:n_parts]
        x_ref, dxo_ref, g_ref, w_ref, dx_ref, dz_ref, h_ref, dg_ref = refs[n_parts:]
        dz = jnp.concatenate([r[...].astype(BF16) for r in (du_ref,) + part_refs], axis=1)
        dz_ref[...] = dz
        dh = _dot(dz, w_ref[...])
        xv = x_ref[...]
        inv = _inv_rms(xv)
        dxn, dg = _rms_bwd(xv, inv, g_ref[...], dh)
        dx_ref[...] = dxo_ref[...] + dxn
        h_ref[...] = (xv * inv * g_ref[...]).astype(BF16)
        _accumulate(dg_ref, dg)

    return _hosted_call(
        body, riders, name=name, steps=s // tm,
        in_specs=[_rows(tm, POOL_DIM)] + [_rows(tm, GROUP_DIM)] * n_parts + [_rows(tm, d), _rows(tm, d), _const((1, d)),
                                                                             _resident((d_in, d))],
        out_specs=[_rows(tm, d), _rows(tm, d_in), _rows(tm, d), _const((1, d))],
        out_shape=[jax.ShapeDtypeStruct((s, d), F32), jax.ShapeDtypeStruct((s, d_in), BF16), jax.ShapeDtypeStruct((s, d), BF16),
                   jax.ShapeDtypeStruct((1, d), F32)],
        args=[du, *dparts, x, dxo, g, w_in_t])


_POOL_HALO = 8


def _pool_chain(v, first_shift):
    n = v.shape[0]
    p2 = v + pltpu.roll(v, first_shift, 0)
    p4 = pltpu.roll(p2, 1, 0) + pltpu.roll(p2, n - 1, 0)
    p8 = pltpu.roll(p4, 2, 0) + pltpu.roll(p4, n - 2, 0)
    p16 = pltpu.roll(p8, 4, 0) + pltpu.roll(p8, n - 4, 0)
    group = lax.broadcasted_iota(jnp.int32, v.shape, 1) // HEAD_DIM
    return jnp.where(group == 0, p2, jnp.where(group == 1, p4, jnp.where(group == 2, p8, p16)))


def _pool_count(t0, rows, s):
    t = t0 + lax.broadcasted_iota(jnp.int32, (rows, POOL_DIM), 0)
    group = lax.broadcasted_iota(jnp.int32, (rows, POOL_DIM), 1) // HEAD_DIM
    half = jnp.where(group == 0, 1, jnp.where(group == 1, 2, jnp.where(group == 2, 4, 8)))
    cnt = jnp.minimum(t + half, s) - jnp.maximum(t - half, 0)
    return jnp.maximum(cnt, 1).astype(F32)


def _pad_rows(ref, pad_ref, s):
    zeros = jnp.zeros((_POOL_HALO, pad_ref.shape[1]), pad_ref.dtype)
    pad_ref[pl.ds(0, _POOL_HALO), :] = zeros
    pad_ref[pl.ds(_POOL_HALO + s, _POOL_HALO), :] = zeros
    pad_ref[pl.ds(_POOL_HALO, s), :] = ref[...]


def _pool_fwd(u, w_bd, scale, name, tm=512):
    s = u.shape[0]
    ext = tm + 2 * _POOL_HALO

    def body(u_ref, w_ref, sc_ref, o_ref, upad):
        _pad_rows(u_ref, upad, s)

        def tile(i, carry):
            t0 = pl.multiple_of(i * tm, tm)
            uv = upad[pl.ds(t0, ext), :]
            win = _pool_chain(uv, 1)[_POOL_HALO:_POOL_HALO + tm]
            y = win / _pool_count(t0, tm, s) - uv[_POOL_HALO:_POOL_HALO + tm]
            o_ref[pl.ds(t0, tm), :] = (_dot(y.astype(BF16), w_ref[...]) * sc_ref[...]).astype(BF16)
            return carry

        lax.fori_loop(0, s // tm, tile, 0)

    return pl.pallas_call(body, name=name, out_shape=jax.ShapeDtypeStruct((s, POOL_DIM), BF16),
                          scratch_shapes=[pltpu.VMEM((s + 2 * _POOL_HALO, POOL_DIM), F32)],
                          compiler_params=_params())(u, w_bd, scale)


def _pool_bwd(u, da, w_bd, scale, name, tm=512):
    s = u.shape[0]
    ext = tm + 2 * _POOL_HALO

    def body(u_ref, da_ref, w_ref, sc_ref, du_ref, dw_ref, dsc_ref, upad, dapad):
        _pad_rows(u_ref, upad, s)
        _pad_rows(da_ref, dapad, s)
        dw_ref[...] = jnp.zeros_like(dw_ref)
        dsc_ref[...] = jnp.zeros_like(dsc_ref)

        def tile(i, carry):
            t0 = pl.multiple_of(i * tm, tm)
            uv = upad[pl.ds(t0, ext), :]
            dav = dapad[pl.ds(t0, ext), :]
            win = _pool_chain(uv, 1)[_POOL_HALO:_POOL_HALO + tm]
            yb = (win / _pool_count(t0, tm, s) - uv[_POOL_HALO:_POOL_HALO + tm]).astype(BF16)
            yl = _dot(yb, w_ref[...])
            da_c = dav[_POOL_HALO:_POOL_HALO + tm]
            dsc_ref[...] += jnp.sum(da_c * yl, axis=0, keepdims=True)
            dyl = (dav * sc_ref[...]).astype(BF16)
            dw_ref[...] += _dot_tn(yb, dyl[_POOL_HALO:_POOL_HALO + tm])
            dy = _dot_nt(dyl, w_ref[...])
            dyc = dy / _pool_count(t0 - _POOL_HALO, ext, s)
            du_ref[pl.ds(t0, tm), :] = (_pool_chain(dyc, ext - 1) - dy)[_POOL_HALO:_POOL_HALO + tm]
            return carry

        lax.fori_loop(0, s // tm, tile, 0)

    pool_cols = pl.BlockSpec((s, POOL_DIM), lambda i: (0, 0), pipeline_mode=pl.Buffered(1))
    return pl.pallas_call(
        body, name=name, grid=(1,),
        in_specs=[pool_cols, pool_cols, _const((POOL_DIM, POOL_DIM)), _const((1, POOL_DIM))],
        out_specs=[_const((s, POOL_DIM)), _const((POOL_DIM, POOL_DIM)), _const((1, POOL_DIM))],
        out_shape=[jax.ShapeDtypeStruct((s, POOL_DIM), F32), jax.ShapeDtypeStruct((POOL_DIM, POOL_DIM), F32),
                   jax.ShapeDtypeStruct((1, POOL_DIM), F32)],
        scratch_shapes=[pltpu.VMEM((s + 2 * _POOL_HALO, POOL_DIM), F32), pltpu.VMEM((s + 2 * _POOL_HALO, POOL_DIM), F32)],
        compiler_params=_params(dimension_semantics=("arbitrary",)))(u, da, w_bd, scale)


_BQ = 128
_KW = _BQ + 2 * N_SIDE
_PAIR = 2 * HEAD_DIM
_NEG = -1e30
_ATTN_UNROLL = 4
_SCORE_SCALE = HEAD_DIM ** -0.5


def _stack_heads(x):
    lane_head = lax.broadcasted_iota(jnp.int32, x.shape, 1) // HEAD_DIM
    zero = jnp.zeros_like(x)
    return jnp.concatenate([jnp.where(lane_head == 0, x, zero), jnp.where(lane_head == 1, x, zero)], axis=0)


def _unstack_heads(x):
    lane_head = lax.broadcasted_iota(jnp.int32, (_BQ, _PAIR), 1) // HEAD_DIM
    return jnp.where(lane_head == 0, x[:_BQ], x[_BQ:])


def _stack_cols(x):
    return jnp.concatenate([x[:, 0:1], x[:, HEAD_DIM:HEAD_DIM + 1]], axis=0)


def _fill_bias(bias_ref, slopes_ref, dilation):
    row = lax.broadcasted_iota(jnp.int32, (2 * _BQ, _KW), 0)
    col = lax.broadcasted_iota(jnp.int32, (2 * _BQ, _KW), 1)
    pair = 2 * pl.program_id(0)
    slope = jnp.where(row < _BQ, slopes_ref[pair], slopes_ref[pair + 1]) * float(dilation)
    for j in range(3):
        dist = jnp.abs(col - (row & (_BQ - 1)) - j * N_SIDE)
        bias_ref[j] = jnp.where(dist <= N_SIDE, -slope * dist.astype(F32), _NEG)


def _block_window(i, n_blocks, length):
    q0 = pl.multiple_of(i * _BQ, _BQ)
    ws = pl.multiple_of(jnp.clip(q0 - N_SIDE, 0, length - _KW), N_SIDE)
    return q0, ws, jnp.where(i == 0, 0, jnp.where(i == n_blocks - 1, 2, 1))


def _residue_rows(dilation, start, count):
    if dilation == 1:
        return pl.ds(start, count)
    return pl.ds(start * dilation + pl.program_id(1), count, stride=dilation)


def _attn_call(body, name, dilation, seq, n_in, n_out, scratch):
    col = pl.BlockSpec((seq, _PAIR), lambda c, r: (0, c), pipeline_mode=pl.Buffered(1))
    return pl.pallas_call(
        body, name=name, grid=(GROUP_DIM // _PAIR, dilation),
        in_specs=[pl.BlockSpec(memory_space=pltpu.SMEM)] + [col] * n_in, out_specs=[col] * n_out,
        out_shape=[jax.ShapeDtypeStruct((seq, GROUP_DIM), F32)] * n_out, scratch_shapes=scratch,
        compiler_params=_params(dimension_semantics=("arbitrary", "arbitrary")))


def _attn_fwd(q, k, v, slopes, dilation, name):
    seq = q.shape[0]
    length = seq // dilation
    n_blocks = length // _BQ

    def body(sl_ref, q_ref, k_ref, v_ref, o_ref, lse_ref, qs, ks, vs, bias_ref):
        all_rows = _residue_rows(dilation, 0, length)
        qs[...] = (q_ref[all_rows, :] * _SCORE_SCALE).astype(BF16)
        ks[...] = k_ref[all_rows, :].astype(BF16)
        vs[...] = v_ref[all_rows, :].astype(BF16)
        _fill_bias(bias_ref, sl_ref, dilation)

        def block(i, carry):
            q0, ws, which = _block_window(i, n_blocks, length)
            kw = ks[pl.ds(ws, _KW), :]
            vw = vs[pl.ds(ws, _KW), :]
            sc = _dot_nt(_stack_heads(qs[pl.ds(q0, _BQ), :]), kw) + bias_ref[which]
            m = jnp.max(sc, axis=-1, keepdims=True)
            p = jnp.exp(sc - m)
            den = jnp.sum(p, axis=-1, keepdims=True)
            rows = _residue_rows(dilation, q0, _BQ)
            o_ref[rows, :] = _unstack_heads(_dot(p.astype(BF16), vw) / den)
            lse_ref[rows, :] = _unstack_heads(jnp.broadcast_to(m + jnp.log(den), (2 * _BQ, _PAIR)))
            return carry

        lax.fori_loop(0, n_blocks, block, 0, unroll=min(_ATTN_UNROLL, n_blocks))

    stage = pltpu.VMEM((length, _PAIR), BF16)
    bias = pltpu.VMEM((3, 2 * _BQ, _KW), F32)
    return _attn_call(body, name, dilation, seq, 3, 2, [stage] * 3 + [bias])(slopes, q, k, v)


def _attn_bwd(q, k, v, do, lse, cterm, slopes, dilation, name):
    seq = q.shape[0]
    length = seq // dilation
    n_blocks = length // _BQ

    def body(sl_ref, q_ref, k_ref, v_ref, do_ref, lse_ref, c_ref, dq_ref, dk_ref, dv_ref, qs, ks, vs, dos, dk_acc, dv_acc, bias_ref):
        all_rows = _residue_rows(dilation, 0, length)
        qs[...] = (q_ref[all_rows, :] * _SCORE_SCALE).astype(BF16)
        for src, dst in ((k_ref, ks), (v_ref, vs), (do_ref, dos)):
            dst[...] = src[all_rows, :].astype(BF16)
        dk_acc[...] = jnp.zeros_like(dk_acc)
        dv_acc[...] = jnp.zeros_like(dv_acc)
        _fill_bias(bias_ref, sl_ref, dilation)

        def block(i, carry):
            q0, ws, which = _block_window(i, n_blocks, length)
            rows = _residue_rows(dilation, q0, _BQ)
            qm = _stack_heads(qs[pl.ds(q0, _BQ), :])
            dom = _stack_heads(dos[pl.ds(q0, _BQ), :])
            kw = ks[pl.ds(ws, _KW), :]
            vw = vs[pl.ds(ws, _KW), :]
            p = jnp.exp(_dot_nt(qm, kw) + bias_ref[which] - _stack_cols(lse_ref[rows, :]))
            ds = (p * (_dot_nt(dom, vw) + _stack_cols(c_ref[rows, :]))).astype(BF16)
            dq_ref[rows, :] = _unstack_heads(_dot(ds, kw)) * _SCORE_SCALE
            dk_acc[pl.ds(ws, _KW), :] += _dot_tn(ds, qm)
            dv_acc[pl.ds(ws, _KW), :] += _dot_tn(p.astype(BF16), dom)
            return carry

        lax.fori_loop(0, n_blocks, block, 0, unroll=min(_ATTN_UNROLL, n_blocks))
        dk_ref[all_rows, :] = dk_acc[...]
        dv_ref[all_rows, :] = dv_acc[...]

    stage = pltpu.VMEM((length, _PAIR), BF16)
    acc = pltpu.VMEM((length, _PAIR), F32)
    bias = pltpu.VMEM((3, 2 * _BQ, _KW), F32)
    return _attn_call(body, name, dilation, seq, 6, 3, [stage] * 4 + [acc] * 2 + [bias])(slopes, q, k, v, do, lse, cterm)


def _group_weights(lses):
    m = jnp.maximum(jnp.maximum(lses[0], lses[1]), lses[2])
    es = [jnp.exp(l - m) for l in lses]
    den = es[0] + es[1] + es[2]
    return [e / den for e in es]


def _combine_fwd(a_pool, outs, lses, name, tm=512):
    s = a_pool.shape[0]

    def body(ap_ref, o0, o1, o2, l0, l1, l2, cat_ref):
        alphas = _group_weights([l0[...], l1[...], l2[...]])
        parts = [ap_ref[...]] + [(o[...] * al).astype(BF16) for o, al in zip((o0, o1, o2), alphas)]
        cat_ref[...] = jnp.concatenate(parts, axis=1)

    width = POOL_DIM + 3 * GROUP_DIM
    return pl.pallas_call(body, name=name, grid=(s // tm,), in_specs=[_rows(tm, POOL_DIM)] + [_rows(tm, GROUP_DIM)] * 6,
                          out_specs=_rows(tm, width), out_shape=jax.ShapeDtypeStruct((s, width), BF16),
                          compiler_params=_params(dimension_semantics=("arbitrary",)))(a_pool, *outs, *lses)


def _combine_bwd(dcat, outs, lses, head_ones, name, tm=512):
    s = dcat.shape[0]

    def body(dc_ref, o0, o1, o2, l0, l1, l2, ones_ref, do0, do1, do2, c0, c1, c2):
        alphas = _group_weights([l0[...], l1[...], l2[...]])
        dcat_v = dc_ref[...]
        das = [dcat_v[:, POOL_DIM + GROUP_DIM * g:POOL_DIM + GROUP_DIM * (g + 1)] for g in range(3)]
        prod = sum(da * (o[...] * al) for da, o, al in zip(das, (o0, o1, o2), alphas))
        hi = prod.astype(BF16)
        lo = (prod - hi.astype(F32)).astype(BF16)
        total = _dot(hi, ones_ref[...]) + _dot(lo, ones_ref[...])
        for da, al, do_ref, c_ref in zip(das, alphas, (do0, do1, do2), (c0, c1, c2)):
            do_ref[...] = da * al
            c_ref[...] = -al * total

    width = POOL_DIM + 3 * GROUP_DIM
    return pl.pallas_call(
        body, name=name, grid=(s // tm,),
        in_specs=[_rows(tm, width)] + [_rows(tm, GROUP_DIM)] * 6 + [_const((GROUP_DIM, GROUP_DIM))],
        out_specs=[_rows(tm, GROUP_DIM)] * 6,
        out_shape=[jax.ShapeDtypeStruct((s, GROUP_DIM), F32)] * 6,
        compiler_params=_params(dimension_semantics=("arbitrary",)))(dcat, *outs, *lses, head_ones)


def _out_fwd(cat, x, w_out, g, name, tm=512):
    s, d = x.shape

    def body(cat_ref, x_ref, w_ref, g_ref, xo_ref, mix_ref):
        mix = _dot(cat_ref[...], w_ref[...])
        mix_ref[...] = mix
        xo_ref[...] = x_ref[...] + mix * _inv_rms(mix) * g_ref[...]

    return pl.pallas_call(body, name=name, grid=(s // tm,),
                          in_specs=[_rows(tm, cat.shape[1]), _rows(tm, d), _resident(w_out.shape), _const((1, d))],
                          out_specs=[_rows(tm, d), _rows(tm, d)], out_shape=[jax.ShapeDtypeStruct((s, d), F32)] * 2,
                          compiler_params=_params(dimension_semantics=("arbitrary",)))(cat, x, w_out, g)


def _out_bwd(dxo, mix, w_out, g, name, tm=512):
    s, d = mix.shape
    width = w_out.shape[0]

    def body(dxo_ref, mix_ref, w_ref, g_ref, dcat_ref, dmix_ref, dg_ref):
        mv = mix_ref[...]
        dmix, dg = _rms_bwd(mv, _inv_rms(mv), g_ref[...], dxo_ref[...])
        dmb = dmix.astype(BF16)
        dmix_ref[...] = dmb
        dcat_ref[...] = _dot_nt(dmb, w_ref[...])
        _accumulate(dg_ref, dg)

    return pl.pallas_call(
        body, name=name, grid=(s // tm,), in_specs=[_rows(tm, d), _rows(tm, d), _resident(w_out.shape), _const((1, d))],
        out_specs=[_rows(tm, width), _rows(tm, d), _const((1, d))],
        out_shape=[jax.ShapeDtypeStruct((s, width), F32), jax.ShapeDtypeStruct((s, d), BF16), jax.ShapeDtypeStruct((1, d), F32)],
        compiler_params=_params(dimension_semantics=("arbitrary",)))(dxo, mix, w_out, g)


def _alibi_slopes():
    return np.array([2.0 ** (-8.0 * (i + 1) / N_ATTN_HEADS) for i in range(N_ATTN_HEADS)], np.float32)


def _block_diag(w_lin):
    n, c, _ = w_lin.shape
    eye = jnp.eye(n, dtype=w_lin.dtype)
    return (eye[:, None, :, None] * w_lin[:, :, None, :]).reshape(n * c, n * c)


class _NoExchange:
    def __init__(self, full):
        self.full, self.grads = full, {}

    def first_weights(self):
        return self.full

    def riders(self, host):
        return []

    def landed(self, host, results):
        pass

    def rest_weights(self):
        return self.full

    def gradient(self, name, grad):
        self.grads[name] = grad


def _local_step(x, target, small, exchange):
    s, d = x.shape
    slopes = _alibi_slopes()
    group_slopes = [jnp.asarray(slopes[4 * g:4 * g + 4]) for g in range(3)]
    w_bd = _block_diag(small["w_pool_lin"]).astype(BF16)
    head_ones = jnp.asarray(np.kron(np.eye(GROUP_DIM // HEAD_DIM), np.ones((HEAD_DIM, HEAD_DIM))), BF16)

    full = exchange.first_weights()
    (x1, a1, b1, f1), riding = _ffn_fwd(x, small["g_ffn1_pre"], full["w1_gate"], full["w1_up"], full["w1_down"],
                                        small["g_ffn1_post"], None, "ffn1_fwd", exchange.riders("ffn1_fwd"))
    exchange.landed("ffn1_fwd", riding)
    full = {**full, **exchange.rest_weights()}
    u, *parts = _in_fwd(x1, small["g_mix_pre"], full["w_in"], "in_fwd")
    qs, ks, vs = parts[0:3], parts[3:6], parts[6:9]
    a_pool = _pool_fwd(u, w_bd, small["pool_scale"], "pool_fwd")
    outs, lses = [], []
    for g, dil in enumerate(DILATIONS):
        o, lse = _attn_fwd(qs[g], ks[g], vs[g], group_slopes[g], dil, f"attn_fwd{g}")
        outs.append(o)
        lses.append(lse)
    cat = _combine_fwd(a_pool, outs, lses, "combine_fwd")
    x2, mix = _out_fwd(cat, x1, full["w_out"], small["g_mix_post"], "out_fwd")
    (dx3, a2, b2, f2, loss_part), _ = _ffn_fwd(x2, small["g_ffn2_pre"], full["w2_gate"], full["w2_up"], full["w2_down"],
                                               small["g_ffn2_post"], target, "ffn2_fwd")

    small_grads = {}

    def hosted(call, host, *args):
        results, riding = call(*args, host, exchange.riders(host))
        exchange.landed(host, riding)
        return results

    def ffn_backward(tag, dxo, x_in, f, a, b):
        n = tag[-1]
        dx, hh, da, db, df, h, dg_pre, dg_post = hosted(
            _ffn_bwd, f"{tag}_bwd", dxo, x_in, f, a, b, small[f"g_{tag}_pre"], small[f"g_{tag}_post"],
            full[f"w{n}_gate"], full[f"w{n}_up"], full[f"w{n}_down"])
        for part, lhs, rhs in (("down", hh, df), ("gate", da, h), ("up", db, h)):
            exchange.gradient(f"w{n}_{part}", hosted(_wgrad, f"{tag}_wgrad_{part}", lhs, rhs))
        small_grads[f"g_{tag}_pre"], small_grads[f"g_{tag}_post"] = dg_pre, dg_post
        return dx

    dx2 = ffn_backward("ffn2", dx3, x2, f2, a2, b2)
    dcat, dmix, small_grads["g_mix_post"] = _out_bwd(dx2, mix, full["w_out"], small["g_mix_post"], "out_bwd")
    exchange.gradient("w_out", hosted(_wgrad, "wgrad_out", cat, dmix))
    dos_cs = _combine_bwd(dcat, outs, lses, head_ones, "combine_bwd")
    dos, cs = dos_cs[:3], dos_cs[3:]
    dqs, dks, dvs = [], [], []
    for g, dil in enumerate(DILATIONS):
        dq, dk, dv = _attn_bwd(qs[g], ks[g], vs[g], dos[g], lses[g], cs[g], group_slopes[g], dil, f"attn_bwd{g}")
        dqs.append(dq)
        dks.append(dk)
        dvs.append(dv)
    du, dw_bd, small_grads["pool_scale"] = _pool_bwd(u, dcat, w_bd, small["pool_scale"], "pool_bwd")
    n_pool = len(POOL_HALF_WINDOWS)
    small_grads["w_pool_lin"] = jnp.stack(
        [dw_bd[HEAD_DIM * g:HEAD_DIM * (g + 1), HEAD_DIM * g:HEAD_DIM * (g + 1)] for g in range(n_pool)])
    dx1, dz, h2, small_grads["g_mix_pre"] = hosted(_in_bwd, "in_bwd", du, dqs + dks + dvs, x1, dx2, small["g_mix_pre"], full["w_in"])
    exchange.gradient("w_in", hosted(_wgrad, "wgrad_in", dz, h2))
    dx0 = ffn_backward("ffn1", dx1, x, f1, a1, b1)
    return loss_part[0, 0], dx0, small_grads


SEGMENTS = ("w1_gate", "w1_up", "w1_down", "w_in", "w_out", "w2_gate", "w2_up", "w2_down")
TRANSPOSED = ("w1_gate", "w1_up", "w_in", "w2_gate", "w2_up")
ROWS_OUTSIDE = ("w1_gate", "w1_up", "w2_gate", "w2_up")
HALF = 512


def _place():
    x, y, c = lax.axis_index("x"), lax.axis_index("y"), lax.axis_index("c")
    other_chips = [(1 - x, y), (x, 1 - y), (1 - x, 1 - y)]
    return x, y, c, other_chips


def _chip_rows(chip, rows):
    return pl.ds(pl.multiple_of((2 * chip[0] + chip[1]) * rows, 16), rows)


def _cols(c):
    return pl.ds(pl.multiple_of(c * HALF, HALF), HALF)


def _cast_shard(w, transpose, place, name, tm=256):
    r, c = w.shape
    if transpose:
        def body(place_ref, w_ref, o_ref):
            o_ref[...] = w_ref[...].T.astype(BF16)

        grid, in_block, out_block, rows = (r // tm,), (tm, c), (c, tm), c
    else:
        def body(place_ref, w_ref, o_ref):
            o_ref[...] = w_ref[...].astype(BF16)

        grid, in_block, out_block, rows = (1,), (r, c), (r, c), r
    return pl.pallas_call(
        body, name=name,
        grid_spec=pltpu.PrefetchScalarGridSpec(
            num_scalar_prefetch=1, grid=grid, in_specs=[pl.BlockSpec(in_block, lambda i, place: (i, 0))],
            out_specs=pl.BlockSpec(out_block, lambda i, place: (place[0], i))),
        out_shape=jax.ShapeDtypeStruct((N_CHIPS * rows, 1024), BF16),
        compiler_params=_params(dimension_semantics=("arbitrary",)))(place, w)


def _gather_weights(bufs):
    n = len(bufs)
    rows = [b.shape[0] // N_CHIPS for b in bufs]

    def body(*refs):
        outs = refs[n:2 * n]
        send_sems, recv_sems, fwd_send_sems, fwd_recv_sems = refs[2 * n:]
        x, y, c, chips = _place()
        me = (x, y)

        def ici(j, k, src_chip, to):
            blk = outs[k].at[_chip_rows(src_chip, rows[k]), _cols(c)]
            return pltpu.make_async_remote_copy(src_ref=blk, dst_ref=blk, send_sem=send_sems.at[j, k], recv_sem=recv_sems.at[j, k],
                                                device_id=to, device_id_type=MESH)

        def d2d(j, k, src_chip, half):
            blk = outs[k].at[_chip_rows(src_chip, rows[k]), _cols(half)]
            return pltpu.make_async_remote_copy(src_ref=blk, dst_ref=blk, send_sem=fwd_send_sems.at[j, k],
                                                recv_sem=fwd_recv_sems.at[j, k], device_id=(x, y, 1 - c), device_id_type=MESH)

        sends = [ici(j, k, me, (*chip, c)) for j, chip in enumerate(chips) for k in range(n)]
        for cp in sends:
            cp.start()
        forwards = []
        for j, chip in enumerate(chips):
            for k in range(n):
                ici(j, k, chip, (x, y, c)).wait_recv()
                fw = d2d(j, k, chip, c)
                fw.start()
                forwards.append(fw)
        for j, chip in enumerate(chips):
            for k in range(n):
                d2d(j, k, chip, 1 - c).wait_recv()
        for cp in sends + forwards:
            cp.wait_send()

    any_spec = pl.BlockSpec(memory_space=pl.ANY)
    return pl.pallas_call(
        body, name="gather_weights", in_specs=[any_spec] * n, out_specs=[any_spec] * n,
        out_shape=[jax.ShapeDtypeStruct(b.shape, b.dtype) for b in bufs], input_output_aliases={k: k for k in range(n)},
        scratch_shapes=[pltpu.SemaphoreType.DMA((3, n)), pltpu.SemaphoreType.DMA((3, n)),
                        pltpu.SemaphoreType.DMA((3, n)), pltpu.SemaphoreType.DMA((3, n))])(*bufs)


def _gather_rider(bufs):
    n = len(bufs)
    rows = [b.shape[0] // N_CHIPS for b in bufs]

    def copies(outs, send_sems, recv_sems, inbound):
        x, y, c, chips = _place()
        for j, chip in enumerate(chips):
            for k in range(n):
                src_chip = chip if inbound else (x, y)
                blk = outs[k].at[_chip_rows(src_chip, rows[k]), _cols(c)]
                yield pltpu.make_async_remote_copy(src_ref=blk, dst_ref=blk, send_sem=send_sems.at[j, k], recv_sem=recv_sems.at[j, k],
                                                   device_id=(*chip, c), device_id_type=MESH)

    def start(ins, outs, send_sems, recv_sems):
        for cp in copies(outs, send_sems, recv_sems, False):
            cp.start()

    def wait(ins, outs, send_sems, recv_sems):
        for cp in copies(outs, send_sems, recv_sems, True):
            cp.wait_recv()
        for cp in copies(outs, send_sems, recv_sems, False):
            cp.wait_send()

    return _Rider(list(bufs), None, (3, n), start, wait)


def _forward_halves(bufs, name):
    n = len(bufs)
    rows = [b.shape[0] // N_CHIPS for b in bufs]

    def body(*refs):
        outs = refs[n:2 * n]
        send_sems, recv_sems = refs[2 * n:]
        x, y, c, chips = _place()

        def d2d(j, k, chip, half):
            blk = outs[k].at[_chip_rows(chip, rows[k]), _cols(half)]
            return pltpu.make_async_remote_copy(src_ref=blk, dst_ref=blk, send_sem=send_sems.at[j, k], recv_sem=recv_sems.at[j, k],
                                                device_id=(x, y, 1 - c), device_id_type=MESH)

        forwards = [d2d(j, k, chip, c) for j, chip in enumerate(chips) for k in range(n)]
        for cp in forwards:
            cp.start()
        for j, chip in enumerate(chips):
            for k in range(n):
                d2d(j, k, chip, 1 - c).wait_recv()
        for cp in forwards:
            cp.wait_send()

    any_spec = pl.BlockSpec(memory_space=pl.ANY)
    return pl.pallas_call(
        body, name=name, in_specs=[any_spec] * n, out_specs=[any_spec] * n,
        out_shape=[jax.ShapeDtypeStruct(b.shape, b.dtype) for b in bufs], input_output_aliases={k: k for k in range(n)},
        scratch_shapes=[pltpu.SemaphoreType.DMA((3, n)), pltpu.SemaphoreType.DMA((3, n))])(*bufs)


def _sibling_rider(grads):
    n = len(grads)

    def copies(ins, outs, send_sems, recv_sems):
        x, y, c, _ = _place()
        return [pltpu.make_async_remote_copy(src_ref=ins[k].at[:, pl.ds(1 - c, 1)], dst_ref=outs[k], send_sem=send_sems.at[k],
                                             recv_sem=recv_sems.at[k], device_id=(x, y, 1 - c), device_id_type=MESH)
                for k in range(n)]

    def start(*refs):
        for cp in copies(*refs):
            cp.start()

    def wait(*refs):
        for cp in copies(*refs):
            cp.wait()

    return _Rider(list(grads), [jax.ShapeDtypeStruct((N_CHIPS, 1) + g.shape[2:], F32) for g in grads], (n,), start, wait)


def _alone(rider, name):
    n, n_out = len(rider.operands), len(rider.landing)

    def body(*refs):
        rider.start(refs[:n], refs[n:n + n_out], *refs[n + n_out:])
        rider.wait(refs[:n], refs[n:n + n_out], *refs[n + n_out:])

    any_spec = pl.BlockSpec(memory_space=pl.ANY)
    return pl.pallas_call(body, name=name, in_specs=[any_spec] * n, out_specs=[any_spec] * n_out, out_shape=rider.landing,
                          scratch_shapes=[pltpu.SemaphoreType.DMA(rider.sems)] * 2)(*rider.operands)


def _chip_sum(grad, from_sibling, place, name):
    rh, width = grad.shape[2:]

    def body(place_ref, g_ref, s_ref, own_ref, all_ref):
        total = g_ref[0, 0] + s_ref[0, 0]
        all_ref[0, 0] = total.astype(BF16)

        @pl.when(pl.program_id(0) == place_ref[0])
        def _():
            own_ref[0] = total

    blk = (1, 1, rh, width)
    return pl.pallas_call(
        body, name=name,
        grid_spec=pltpu.PrefetchScalarGridSpec(
            num_scalar_prefetch=1, grid=(N_CHIPS,),
            in_specs=[pl.BlockSpec(blk, lambda p, place: (p, place[1], 0, 0)), pl.BlockSpec(blk, lambda p, place: (p, 0, 0, 0))],
            out_specs=[pl.BlockSpec((1, rh, width), lambda p, place: (0, 0, 0)), pl.BlockSpec(blk, lambda p, place: (p, 0, 0, 0))]),
        out_shape=[jax.ShapeDtypeStruct((1, rh, width), F32), jax.ShapeDtypeStruct((N_CHIPS, 1, rh, width), BF16)],
        compiler_params=_params(dimension_semantics=("arbitrary",)))(place, grad, from_sibling)


def _scatter_rider(sums):
    n = len(sums)

    def copies(ins, outs, send_sems, recv_sems):
        x, y, c, chips = _place()
        return [pltpu.make_async_remote_copy(src_ref=ins[k].at[pl.ds(2 * chip[0] + chip[1], 1)], dst_ref=outs[k].at[pl.ds(j, 1)],
                                             send_sem=send_sems.at[j, k], recv_sem=recv_sems.at[j, k],
                                             device_id=(*chip, c), device_id_type=MESH)
                for j, chip in enumerate(chips) for k in range(n)]

    def start(*refs):
        for cp in copies(*refs):
            cp.start()

    def wait(*refs):
        for cp in copies(*refs):
            cp.wait()

    return _Rider(list(sums), [jax.ShapeDtypeStruct((3,) + sm.shape[1:], BF16) for sm in sums], (3, n), start, wait)


def _total_sum(own, received, name):
    def body(o_ref, r_ref, t_ref):
        total = o_ref[0]
        for j in range(3):
            total = total + r_ref[j, 0].astype(F32)
        t_ref[0] = total

    return pl.pallas_call(body, name=name, out_shape=jax.ShapeDtypeStruct(own.shape, F32), compiler_params=_params())(own, received)


def _swap_halves(halves):
    n = len(halves)

    def body(*refs):
        ins, outs = refs[:n], refs[n:2 * n]
        send_sems, recv_sems = refs[2 * n:]
        x, y, c, _ = _place()
        copies = [pltpu.make_async_remote_copy(src_ref=ins[k], dst_ref=outs[k], send_sem=send_sems.at[k],
                                               recv_sem=recv_sems.at[k], device_id=(x, y, 1 - c), device_id_type=MESH)
                  for k in range(n)]
        for cp in copies:
            cp.start()
        for cp in copies:
            cp.wait()

    any_spec = pl.BlockSpec(memory_space=pl.ANY)
    return pl.pallas_call(
        body, name="swap_halves", in_specs=[any_spec] * n, out_specs=[any_spec] * n,
        out_shape=[jax.ShapeDtypeStruct(h.shape, F32) for h in halves],
        scratch_shapes=[pltpu.SemaphoreType.DMA((n,)), pltpu.SemaphoreType.DMA((n,))])(*halves)


N_DEV = 8


def _gather_small(block):
    m_per, width = block.shape

    def body(x_ref, out_ref, send_sems, recv_sems, local_sem):
        x, y, c, chips = _place()
        me, sibling = (x, y, c), (x, y, 1 - c)

        def rows(px, py, pc):
            return out_ref.at[pl.ds((4 * px + 2 * py + pc) * m_per, m_per), :]

        def copy(k, blk, to, src=None):
            return pltpu.make_async_remote_copy(src_ref=rows(*blk) if src is None else src, dst_ref=rows(*blk),
                                                send_sem=send_sems.at[k], recv_sem=recv_sems.at[k], device_id=to, device_id_type=MESH)

        mine = pltpu.make_async_copy(x_ref, rows(*me), local_sem)
        mine.start()
        first = [copy(0, me, sibling, src=x_ref)] + [copy(1 + j, me, (*chip, c), src=x_ref) for j, chip in enumerate(chips)]
        for cp in first:
            cp.start()
        passed = [copy(4 + j, (*chip, c), sibling) for j, chip in enumerate(chips)]
        for j, chip in enumerate(chips):
            copy(1 + j, (*chip, c), me).wait_recv()
            passed[j].start()
        copy(0, sibling, me).wait_recv()
        for j, chip in enumerate(chips):
            copy(4 + j, (*chip, 1 - c), me).wait_recv()
        for cp in first + passed:
            cp.wait_send()
        mine.wait()

    vmem = pl.BlockSpec(memory_space=pltpu.VMEM)
    return pl.pallas_call(body, name="gather_small", out_shape=jax.ShapeDtypeStruct((N_DEV * m_per, width), F32),
                          in_specs=[vmem], out_specs=vmem,
                          scratch_shapes=[pltpu.SemaphoreType.DMA((7,)), pltpu.SemaphoreType.DMA((7,)),
                                          pltpu.SemaphoreType.DMA])(block)


def _adamw_math(w, g, m, v):
    m = ADAM_B1 * m + (1.0 - ADAM_B1) * g
    v = ADAM_B2 * v + (1.0 - ADAM_B2) * (g * g)
    m_hat = m / (1.0 - ADAM_B1 ** ADAM_STEP)
    v_hat = v / (1.0 - ADAM_B2 ** ADAM_STEP)
    delta = -ADAM_LR * (m_hat / (jnp.sqrt(v_hat) + ADAM_EPS) + ADAM_WD * w)
    return delta, m, v


def _adamw(w, mine, siblings, place, m, v, transposed, name):
    if transposed:
        def body(place_ref, w_ref, mine_ref, sib_ref, m_ref, v_ref, go_ref, d_ref, mo_ref, vo_ref):
            first = place_ref[1] == 0
            g = jnp.concatenate([jnp.where(first, mine_ref[0], sib_ref[0]), jnp.where(first, sib_ref[0], mine_ref[0])], axis=0).T
            go_ref[...] = g
            d_ref[...], mo_ref[...], vo_ref[...] = _adamw_math(w_ref[...], g, m_ref[...], v_ref[...])

        vmem = pl.BlockSpec(memory_space=pltpu.VMEM)
        return pl.pallas_call(body, name=name, in_specs=[pl.BlockSpec(memory_space=pltpu.SMEM)] + [vmem] * 5, out_specs=[vmem] * 4,
                              out_shape=[jax.ShapeDtypeStruct(w.shape, F32)] * 4, compiler_params=_params())(
                                  place, w, mine, siblings, m, v)

    rh, width = mine.shape[1:]

    def body(place_ref, w_ref, mine_ref, sib_ref, m_ref, v_ref, go_ref, d_ref, mo_ref, vo_ref):
        g = jnp.where(pl.program_id(0) == place_ref[1], mine_ref[0], sib_ref[0])
        go_ref[...] = g
        d_ref[...], mo_ref[...], vo_ref[...] = _adamw_math(w_ref[...], g, m_ref[...], v_ref[...])

    half = pl.BlockSpec((rh, width), lambda h, place: (h, 0))
    whole = pl.BlockSpec((1, rh, width), lambda h, place: (0, 0, 0))
    return pl.pallas_call(
        body, name=name,
        grid_spec=pltpu.PrefetchScalarGridSpec(num_scalar_prefetch=1, grid=(2,), in_specs=[half, whole, whole, half, half],
                                               out_specs=[half] * 4),
        out_shape=[jax.ShapeDtypeStruct(w.shape, F32)] * 4,
        compiler_params=_params(dimension_semantics=("arbitrary",)))(place, w, mine, siblings, m, v)


def _adamw_small(gathered, w, m, v, name):
    def body(ga_ref, w_ref, m_ref, v_ref, go_ref, d_ref, mo_ref, vo_ref):
        g = ga_ref[0]
        for dev in range(1, N_DEV):
            g = g + ga_ref[dev]
        go_ref[...] = g
        d_ref[...], mo_ref[...], vo_ref[...] = _adamw_math(w_ref[...], g, m_ref[...], v_ref[...])

    return pl.pallas_call(body, name=name, out_shape=[jax.ShapeDtypeStruct(w.shape, F32)] * 4,
                          compiler_params=_params())(gathered, w, m, v)


class _Exchange:
    FIRST = ("w1_gate", "w1_up", "w1_down")
    HOSTS = {"ffn2_wgrad_gate": (("w2_down",), ()), "ffn2_wgrad_up": (("w2_gate",), ("w2_down",)),
             "wgrad_out": (("w2_up",), ("w2_gate",)), "in_bwd": (("w_out",), ("w2_up",)), "wgrad_in": ((), ("w_out",)),
             "ffn1_wgrad_down": ((), ("w_in",)), "ffn1_wgrad_gate": (("w1_down",), ()), "ffn1_wgrad_up": ((), ("w1_down", "w1_gate"))}
    ALONE = ("w_in", "w1_gate", "w1_up")

    def __init__(self, bufs, place):
        self.bufs, self.place = bufs, place
        self.later = [k for k in SEGMENTS if k not in self.FIRST]
        self.split, self.own, self.to_send, self.received = {}, {}, {}, {}

    def first_weights(self):
        return dict(zip(self.FIRST, _gather_weights([self.bufs[k] for k in self.FIRST])))

    def riders(self, host):
        if host == "ffn1_fwd":
            return [_gather_rider([self.bufs[k] for k in self.later])]
        halves, sums = self.HOSTS.get(host, ((), ()))
        return ([_sibling_rider([self.split[k] for k in halves])] if halves else []) + (
            [_scatter_rider([self.to_send[k] for k in sums])] if sums else [])

    def landed(self, host, results):
        if host == "ffn1_fwd":
            self.rest = dict(zip(self.later, _forward_halves(results[0], "gather_rest_forward")))
            return
        halves, sums = self.HOSTS.get(host, ((), ()))
        if halves:
            self._chip_sums(halves, results[0])
        if sums:
            self.received.update(zip(sums, results[-1]))

    def rest_weights(self):
        return self.rest

    def gradient(self, name, grad):
        self.split[name] = grad.reshape(N_CHIPS, 2, grad.shape[0] // (2 * N_CHIPS), grad.shape[1])
        if name in self.ALONE:
            self._chip_sums([name], _alone(_sibling_rider([self.split[name]]), f"reduce_sibling_{name}"))

    def _chip_sums(self, names, from_sibling):
        for k, fs in zip(names, from_sibling):
            self.own[k], self.to_send[k] = _chip_sum(self.split[k], fs, self.place, f"chip_sum_{k}")

    def summed_halves(self):
        late = [k for k in SEGMENTS if k not in self.received]
        self.received.update(zip(late, _alone(_scatter_rider([self.to_send[k] for k in late]), "reduce_chips_last")))
        return [_total_sum(self.own[k], self.received[k], f"total_{k}") for k in SEGMENTS]


SMALL = ("g_ffn1_pre", "g_ffn1_post", "g_mix_pre", "w_pool_lin", "pool_scale", "g_mix_post", "g_ffn2_pre", "g_ffn2_post")
WEIGHTS = ("g_ffn1_pre", "w1_gate", "w1_up", "w1_down", "g_ffn1_post", "g_mix_pre", "w_in", "w_pool_lin", "pool_scale", "w_out",
           "g_mix_post", "g_ffn2_pre", "w2_gate", "w2_up", "w2_down", "g_ffn2_post")
LANES = 128


def _pack_small(tree):
    flat = jnp.concatenate([tree[k].reshape(-1) for k in SMALL])
    rows = -(-flat.shape[0] // (8 * LANES)) * 8
    return jnp.pad(flat, (0, rows * LANES - flat.shape[0])).reshape(rows, LANES)


def _unpack_small(packed, like):
    flat, out, at = packed.reshape(-1), {}, 0
    for k in SMALL:
        size = math.prod(like[k].shape)
        out[k] = flat[at:at + size].reshape(like[k].shape)
        at += size
    return out


def kernel(x, g_ffn1_pre, w1_gate, w1_up, w1_down, g_ffn1_post, g_mix_pre, w_in, w_pool_lin, pool_scale, w_out, g_mix_post, g_ffn2_pre, w2_gate, w2_up, w2_down, g_ffn2_post, loss_target, m_g_ffn1_pre, m_w1_gate, m_w1_up, m_w1_down, m_g_ffn1_post, m_g_mix_pre, m_w_in, m_w_pool_lin, m_pool_scale, m_w_out, m_g_mix_post, m_g_ffn2_pre, m_w2_gate, m_w2_up, m_w2_down, m_g_ffn2_post, v_g_ffn1_pre, v_w1_gate, v_w1_up, v_w1_down, v_g_ffn1_post, v_g_mix_pre, v_w_in, v_w_pool_lin, v_pool_scale, v_w_out, v_g_mix_post, v_g_ffn2_pre, v_w2_gate, v_w2_up, v_w2_down, v_g_ffn2_post):
    given = dict(locals())
    w = {k: given[k] for k in WEIGHTS}
    m = {k: given["m_" + k] for k in WEIGHTS}
    v = {k: given["v_" + k] for k in WEIGHTS}
    small = {k: (w[k][0] if k == "w_pool_lin" else w[k].reshape(1, -1)) for k in SMALL}

    place = jnp.stack([2 * lax.axis_index("x") + lax.axis_index("y"), lax.axis_index("c")]).astype(jnp.int32)
    def as_rows(a, k):
        return jnp.swapaxes(a, 1, 2)[0] if k in ROWS_OUTSIDE else a[0]

    def as_given(a, k):
        return jnp.swapaxes(a[None], 1, 2) if k in ROWS_OUTSIDE else a[None]

    in_kernel = [k for k in TRANSPOSED if k not in ROWS_OUTSIDE]
    exchange = _Exchange({k: _cast_shard(as_rows(w[k], k), k in in_kernel, place, f"cast_{k}") for k in SEGMENTS}, place)
    loss_part, grad_x, small_grads = _local_step(x[0], loss_target[0], small, exchange)
    loss = lax.psum(loss_part, ("x", "y", "c"))

    halves = exchange.summed_halves()
    from_sibling = _swap_halves(halves)

    out_grad, out_delta, out_m, out_v = {}, {}, {}, {}
    for k, mine, sib in zip(SEGMENTS, halves, from_sibling):
        out_grad[k], out_delta[k], out_m[k], out_v[k] = (
            as_given(a, k) for a in _adamw(as_rows(w[k], k), mine, sib, place, as_rows(m[k], k), as_rows(v[k], k),
                                           k in in_kernel, f"adamw_{k}"))

    small_grads["w_pool_lin"] = small_grads["w_pool_lin"][None]
    packed = _pack_small(small_grads)
    gathered = _gather_small(packed).reshape(N_DEV, *packed.shape)
    like = {k: w[k] for k in SMALL}
    results = _adamw_small(gathered, _pack_small(like), _pack_small({k: m[k] for k in SMALL}),
                           _pack_small({k: v[k] for k in SMALL}), "adamw_small")
    for tree, res in zip((out_grad, out_delta, out_m, out_v), results):
        tree.update(_unpack_small(res, like))

    return (loss, grad_x[None], *[out_grad[k] for k in WEIGHTS], *[out_delta[k] for k in WEIGHTS],
            *[out_m[k] for k in WEIGHTS], *[out_v[k] for k in WEIGHTS])
```

```python
import math
import typing

import numpy as np
import jax
import jax.numpy as jnp
from jax import lax
from jax.experimental import pallas as pl
from jax.experimental.pallas import tpu as pltpu

F32 = jnp.float32
BF16 = jnp.bfloat16
MESH = pl.DeviceIdType.MESH

RMS_EPS = 1e-6
HEAD_DIM = 64
POOL_HALF_WINDOWS = (1, 2, 4, 8)
POOL_DIM = 256
GROUP_DIM = 256
DILATIONS = (1, 4, 16)
N_SIDE = 64
N_ATTN_HEADS = 12
ADAM_LR, ADAM_B1, ADAM_B2, ADAM_EPS, ADAM_WD, ADAM_STEP = 0.001, 0.9, 0.999, 1e-08, 0.01, 10

N_CHIPS = 4
V7X_VMEM_LIMIT = 60 * 1024 * 1024

_NT = (((1,), (1,)), ((), ()))
_TN = (((0,), (0,)), ((), ()))


def _dot(a, b):
    return jnp.dot(a, b, preferred_element_type=F32)


def _dot_nt(a, b):
    return lax.dot_general(a, b, _NT, preferred_element_type=F32)


def _dot_tn(a, b):
    return lax.dot_general(a, b, _TN, preferred_element_type=F32)


def _params(**kw):
    return pltpu.CompilerParams(vmem_limit_bytes=V7X_VMEM_LIMIT, **kw)


def _rows(tm, width):
    return pl.BlockSpec((tm, width), lambda i: (i, 0))


def _resident(shape):
    return pl.BlockSpec(shape, lambda i: (0,) * len(shape), pipeline_mode=pl.Buffered(1))


def _const(shape):
    return pl.BlockSpec(shape, lambda i: (0,) * len(shape))


def _tile(rows, cap):
    return max(t for t in range(16, cap + 1, 16) if rows % t == 0)


def _inv_rms(x):
    return lax.rsqrt(jnp.mean(x * x, axis=-1, keepdims=True) + RMS_EPS)


def _rms_bwd(x, inv, g, dy):
    n = x * inv
    dn = dy * g
    dx = inv * (dn - n * jnp.mean(dn * n, axis=-1, keepdims=True))
    return dx, jnp.sum(dy * n, axis=0, keepdims=True)


def _accumulate(ref, value):
    @pl.when(pl.program_id(0) == 0)
    def _():
        ref[...] = jnp.zeros_like(ref)

    ref[...] += value


class _Rider(typing.NamedTuple):
    operands: list
    landing: typing.Optional[list]
    sems: tuple
    start: typing.Callable
    wait: typing.Callable


def _hosted_call(body, riders, *, name, steps, in_specs, out_specs, out_shape, args, scratch_shapes=()):
    params = _params(dimension_semantics=("arbitrary",))
    riders = list(riders or [])
    if not riders:
        res = pl.pallas_call(body, name=name, grid=(steps,), in_specs=in_specs, out_specs=out_specs, out_shape=out_shape,
                             scratch_shapes=list(scratch_shapes), compiler_params=params)(*args)
        return list(res), []
    n_in, n_out, n_scratch = len(in_specs), len(out_specs), len(scratch_shapes)
    operands, landing, aliases, spans = [], [], {}, []
    for rd in riders:
        lands = rd.landing if rd.landing is not None else [jax.ShapeDtypeStruct(a.shape, a.dtype) for a in rd.operands]
        if rd.landing is None:
            aliases.update({n_in + len(operands) + i: n_out + len(landing) + i for i in range(len(lands))})
        spans.append((len(operands), len(rd.operands), len(landing), len(lands)))
        operands += rd.operands
        landing += lands
    outs_at = n_in + len(operands)
    scratch_at = outs_at + n_out + len(landing)

    def riding(*refs):
        def each(action):
            for i, (rd, (in_at, n_ops, out_at, n_lands)) in enumerate(zip(riders, spans)):
                sems = refs[scratch_at + n_scratch + 2 * i:scratch_at + n_scratch + 2 * i + 2]
                getattr(rd, action)(refs[n_in + in_at:n_in + in_at + n_ops],
                                    refs[outs_at + n_out + out_at:outs_at + n_out + out_at + n_lands], *sems)

        @pl.when(pl.program_id(0) == 0)
        def _():
            each("start")

        body(*refs[:n_in], *refs[outs_at:outs_at + n_out], *refs[scratch_at:scratch_at + n_scratch])

        @pl.when(pl.program_id(0) == steps - 1)
        def _():
            each("wait")

    any_spec = pl.BlockSpec(memory_space=pl.ANY)
    res = pl.pallas_call(
        riding, name=name, grid=(steps,), in_specs=list(in_specs) + [any_spec] * len(operands),
        out_specs=list(out_specs) + [any_spec] * len(landing), out_shape=list(out_shape) + landing,
        scratch_shapes=list(scratch_shapes) + [pltpu.SemaphoreType.DMA(rd.sems) for rd in riders for _ in range(2)],
        input_output_aliases=aliases, compiler_params=params)(*args, *operands)
    return list(res[:n_out]), [list(res[n_out + out_at:n_out + out_at + n_lands]) for _, _, out_at, n_lands in spans]


_SUB_TILE = 256


def _sub_tiles(tm):
    return [pl.ds(r, _SUB_TILE) for r in range(0, tm, _SUB_TILE)]


def _ffn_fwd(x, g_pre, wg_t, wu_t, wd, g_post, target, name, riders=None, tm=512):
    s, d = x.shape
    ff = wd.shape[0]
    with_loss = target is not None

    def body(*refs):
        if with_loss:
            x_ref, gpre_ref, wg_ref, wu_ref, wd_ref, gpost_ref, t_ref, xo_ref, a_ref, b_ref, f_ref, loss_ref = refs
        else:
            x_ref, gpre_ref, wg_ref, wu_ref, wd_ref, gpost_ref, xo_ref, a_ref, b_ref, f_ref = refs
        loss = 0.0
        for rows in _sub_tiles(tm):
            xv = x_ref[rows, :]
            hb = (xv * _inv_rms(xv) * gpre_ref[...]).astype(BF16)
            a = _dot_nt(hb, wg_ref[...])
            b = _dot_nt(hb, wu_ref[...])
            hh = (a * jax.nn.sigmoid(a)) * b
            f = _dot(hh.astype(BF16), wd_ref[...])
            xo = xv + 0.5 * (f * _inv_rms(f) * gpost_ref[...])
            a_ref[rows, :] = a.astype(BF16)
            b_ref[rows, :] = b.astype(BF16)
            f_ref[rows, :] = f
            if with_loss:
                e = xo - t_ref[rows, :]
                xo_ref[rows, :] = e * (1.0 / d)
                loss = loss + 0.5 * jnp.sum(jnp.mean(e * e, axis=-1, keepdims=True))
            else:
                xo_ref[rows, :] = xo
        if with_loss:
            _accumulate(loss_ref, loss)

    in_specs = [_rows(tm, d), _const((1, d)), _resident((ff, d)), _resident((ff, d)), _resident((ff, d)), _const((1, d))]
    args = [x, g_pre, wg_t, wu_t, wd, g_post]
    out_shape = [jax.ShapeDtypeStruct((s, d), F32), jax.ShapeDtypeStruct((s, ff), BF16),
                 jax.ShapeDtypeStruct((s, ff), BF16), jax.ShapeDtypeStruct((s, d), F32)]
    out_specs = [_rows(tm, d), _rows(tm, ff), _rows(tm, ff), _rows(tm, d)]
    if with_loss:
        in_specs.append(_rows(tm, d))
        args.append(target)
        out_shape.append(jax.ShapeDtypeStruct((8, 128), F32))
        out_specs.append(_const((8, 128)))
    return _hosted_call(body, riders, name=name, steps=s // tm, in_specs=in_specs, out_specs=out_specs, out_shape=out_shape, args=args)


def _ffn_bwd(dxo, x, f, a, b, g_pre, g_post, wg_t, wu_t, wd, name, riders=None, tm=256):
    s, d = x.shape
    ff = wd.shape[0]

    def body(dxo_ref, x_ref, f_ref, a_ref, b_ref, gpre_ref, gpost_ref, wg_ref, wu_ref, wd_ref,
             dx_ref, hh_ref, da_ref, db_ref, df_ref, h_ref, dgpre_ref, dgpost_ref):
        dgpre_sum = dgpost_sum = 0.0
        for rows in _sub_tiles(tm):
            dxo_v = dxo_ref[rows, :]
            fv = f_ref[rows, :]
            df, dgpost = _rms_bwd(fv, _inv_rms(fv), gpost_ref[...], 0.5 * dxo_v)
            dfb = df.astype(BF16)
            dhh = _dot_nt(dfb, wd_ref[...])
            av = a_ref[rows, :].astype(F32)
            bv = b_ref[rows, :].astype(F32)
            sig = jax.nn.sigmoid(av)
            sa = av * sig
            da = (dhh * bv * (sig * (1.0 + av * (1.0 - sig)))).astype(BF16)
            db = (dhh * sa).astype(BF16)
            dh = _dot(da, wg_ref[...]) + _dot(db, wu_ref[...])
            xv = x_ref[rows, :]
            inv = _inv_rms(xv)
            dxn, dgpre = _rms_bwd(xv, inv, gpre_ref[...], dh)
            dx_ref[rows, :] = dxo_v + dxn
            hh_ref[rows, :] = (sa * bv).astype(BF16)
            da_ref[rows, :] = da
            db_ref[rows, :] = db
            df_ref[rows, :] = dfb
            h_ref[rows, :] = (xv * inv * gpre_ref[...]).astype(BF16)
            dgpre_sum, dgpost_sum = dgpre_sum + dgpre, dgpost_sum + dgpost
        _accumulate(dgpre_ref, dgpre_sum)
        _accumulate(dgpost_ref, dgpost_sum)

    return _hosted_call(
        body, riders, name=name, steps=s // tm,
        in_specs=[_rows(tm, d), _rows(tm, d), _rows(tm, d), _rows(tm, ff), _rows(tm, ff), _const((1, d)), _const((1, d)),
                  _resident((ff, d)), _resident((ff, d)), _resident((ff, d))],
        out_specs=[_rows(tm, d), _rows(tm, ff), _rows(tm, ff), _rows(tm, ff), _rows(tm, d), _rows(tm, d),
                   _const((1, d)), _const((1, d))],
        out_shape=[jax.ShapeDtypeStruct((s, d), F32), jax.ShapeDtypeStruct((s, ff), BF16), jax.ShapeDtypeStruct((s, ff), BF16),
                   jax.ShapeDtypeStruct((s, ff), BF16), jax.ShapeDtypeStruct((s, d), BF16), jax.ShapeDtypeStruct((s, d), BF16),
                   jax.ShapeDtypeStruct((1, d), F32), jax.ShapeDtypeStruct((1, d), F32)],
        args=[dxo, x, f, a, b, g_pre, g_post, wg_t, wu_t, wd])


def _wgrad(lhs, rhs, name, riders=None, rt=256):
    s, r = lhs.shape
    c = rhs.shape[1]

    def body(l_ref, r_ref, o_ref):
        o_ref[...] = _dot_tn(l_ref[...], r_ref[...])

    (out,), riding = _hosted_call(
        body, riders, name=name, steps=pl.cdiv(r, rt), in_specs=[pl.BlockSpec((s, rt), lambda i: (0, i)), _resident((s, c))],
        out_specs=[pl.BlockSpec((rt, c), lambda i: (i, 0))], out_shape=[jax.ShapeDtypeStruct((r, c), F32)], args=[lhs, rhs])
    return out, riding


def _in_fwd(x, g, w_in_t, name, tm=512):
    s, d = x.shape
    d_in = w_in_t.shape[0]
    n_parts = (d_in - POOL_DIM) // GROUP_DIM

    def body(x_ref, g_ref, w_ref, u_ref, *part_refs):
        xv = x_ref[...]
        hb = (xv * _inv_rms(xv) * g_ref[...]).astype(BF16)
        z = _dot_nt(hb, w_ref[...])
        u_ref[...] = z[:, :POOL_DIM]
        for j, ref in enumerate(part_refs):
            ref[...] = z[:, POOL_DIM + GROUP_DIM * j:POOL_DIM + GROUP_DIM * (j + 1)]

    return pl.pallas_call(
        body, name=name, grid=(s // tm,), in_specs=[_rows(tm, d), _const((1, d)), _resident((d_in, d))],
        out_specs=[_rows(tm, POOL_DIM)] + [_rows(tm, GROUP_DIM)] * n_parts,
        out_shape=[jax.ShapeDtypeStruct((s, POOL_DIM), F32)] + [jax.ShapeDtypeStruct((s, GROUP_DIM), F32)] * n_parts,
        compiler_params=_params(dimension_semantics=("arbitrary",)))(x, g, w_in_t)


def _in_bwd(du, dparts, x, dxo, g, w_in_t, name, riders=None, tm=512):
    s, d = x.shape
    d_in = w_in_t.shape[0]
    n_parts = len(dparts)

    def body(du_ref, *refs):
        part_refs = refs[:n_parts]
        x_ref, dxo_ref, g_ref, w_ref, dx_ref, dz_ref, h_ref, dg_ref = refs[n_parts:]
        dz = jnp.concatenate([r[...].astype(BF16) for r in (du_ref,) + part_refs], axis=1)
        dz_ref[...] = dz
        dh = _dot(dz, w_ref[...])
        xv = x_ref[...]
        inv = _inv_rms(xv)
        dxn, dg = _rms_bwd(xv, inv, g_ref[...], dh)
        dx_ref[...] = dxo_ref[...] + dxn
        h_ref[...] = (xv * inv * g_ref[...]).astype(BF16)
        _accumulate(dg_ref, dg)

    return _hosted_call(
        body, riders, name=name, steps=s // tm,
        in_specs=[_rows(tm, POOL_DIM)] + [_rows(tm, GROUP_DIM)] * n_parts + [_rows(tm, d), _rows(tm, d), _const((1, d)),
                                                                             _resident((d_in, d))],
        out_specs=[_rows(tm, d), _rows(tm, d_in), _rows(tm, d), _const((1, d))],
        out_shape=[jax.ShapeDtypeStruct((s, d), F32), jax.ShapeDtypeStruct((s, d_in), BF16), jax.ShapeDtypeStruct((s, d), BF16),
                   jax.ShapeDtypeStruct((1, d), F32)],
        args=[du, *dparts, x, dxo, g, w_in_t])


_POOL_HALO = 8


def _pool_chain(v, first_shift):
    n = v.shape[0]
    p2 = v + pltpu.roll(v, first_shift, 0)
    p4 = pltpu.roll(p2, 1, 0) + pltpu.roll(p2, n - 1, 0)
    p8 = pltpu.roll(p4, 2, 0) + pltpu.roll(p4, n - 2, 0)
    p16 = pltpu.roll(p8, 4, 0) + pltpu.roll(p8, n - 4, 0)
    group = lax.broadcasted_iota(jnp.int32, v.shape, 1) // HEAD_DIM
    return jnp.where(group == 0, p2, jnp.where(group == 1, p4, jnp.where(group == 2, p8, p16)))


def _pool_count(t0, rows, s):
    t = t0 + lax.broadcasted_iota(jnp.int32, (rows, POOL_DIM), 0)
    group = lax.broadcasted_iota(jnp.int32, (rows, POOL_DIM), 1) // HEAD_DIM
    half = jnp.where(group == 0, 1, jnp.where(group == 1, 2, jnp.where(group == 2, 4, 8)))
    cnt = jnp.minimum(t + half, s) - jnp.maximum(t - half, 0)
    return jnp.maximum(cnt, 1).astype(F32)


def _pad_rows(ref, pad_ref, s):
    zeros = jnp.zeros((_POOL_HALO, pad_ref.shape[1]), pad_ref.dtype)
    pad_ref[pl.ds(0, _POOL_HALO), :] = zeros
    pad_ref[pl.ds(_POOL_HALO + s, _POOL_HALO), :] = zeros
    pad_ref[pl.ds(_POOL_HALO, s), :] = ref[...]


def _pool_fwd(u, w_bd, scale, name, tm=512):
    s = u.shape[0]
    ext = tm + 2 * _POOL_HALO

    def body(u_ref, w_ref, sc_ref, o_ref, upad):
        _pad_rows(u_ref, upad, s)

        def tile(i, carry):
            t0 = pl.multiple_of(i * tm, tm)
            uv = upad[pl.ds(t0, ext), :]
            win = _pool_chain(uv, 1)[_POOL_HALO:_POOL_HALO + tm]
            y = win / _pool_count(t0, tm, s) - uv[_POOL_HALO:_POOL_HALO + tm]
            o_ref[pl.ds(t0, tm), :] = (_dot(y.astype(BF16), w_ref[...]) * sc_ref[...]).astype(BF16)
            return carry

        lax.fori_loop(0, s // tm, tile, 0)

    return pl.pallas_call(body, name=name, out_shape=jax.ShapeDtypeStruct((s, POOL_DIM), BF16),
                          scratch_shapes=[pltpu.VMEM((s + 2 * _POOL_HALO, POOL_DIM), F32)],
                          compiler_params=_params())(u, w_bd, scale)


def _pool_bwd(u, da, w_bd, scale, name, tm=512):
    s = u.shape[0]
    ext = tm + 2 * _POOL_HALO

    def body(u_ref, da_ref, w_ref, sc_ref, du_ref, dw_ref, dsc_ref, upad, dapad):
        _pad_rows(u_ref, upad, s)
        _pad_rows(da_ref, dapad, s)
        dw_ref[...] = jnp.zeros_like(dw_ref)
        dsc_ref[...] = jnp.zeros_like(dsc_ref)

        def tile(i, carry):
            t0 = pl.multiple_of(i * tm, tm)
            uv = upad[pl.ds(t0, ext), :]
            dav = dapad[pl.ds(t0, ext), :]
            win = _pool_chain(uv, 1)[_POOL_HALO:_POOL_HALO + tm]
            yb = (win / _pool_count(t0, tm, s) - uv[_POOL_HALO:_POOL_HALO + tm]).astype(BF16)
            yl = _dot(yb, w_ref[...])
            da_c = dav[_POOL_HALO:_POOL_HALO + tm]
            dsc_ref[...] += jnp.sum(da_c * yl, axis=0, keepdims=True)
            dyl = (dav * sc_ref[...]).astype(BF16)
            dw_ref[...] += _dot_tn(yb, dyl[_POOL_HALO:_POOL_HALO + tm])
            dy = _dot_nt(dyl, w_ref[...])
            dyc = dy / _pool_count(t0 - _POOL_HALO, ext, s)
            du_ref[pl.ds(t0, tm), :] = (_pool_chain(dyc, ext - 1) - dy)[_POOL_HALO:_POOL_HALO + tm]
            return carry

        lax.fori_loop(0, s // tm, tile, 0)

    pool_cols = pl.BlockSpec((s, POOL_DIM), lambda i: (0, 0), pipeline_mode=pl.Buffered(1))
    return pl.pallas_call(
        body, name=name, grid=(1,),
        in_specs=[pool_cols, pool_cols, _const((POOL_DIM, POOL_DIM)), _const((1, POOL_DIM))],
        out_specs=[_const((s, POOL_DIM)), _const((POOL_DIM, POOL_DIM)), _const((1, POOL_DIM))],
        out_shape=[jax.ShapeDtypeStruct((s, POOL_DIM), F32), jax.ShapeDtypeStruct((POOL_DIM, POOL_DIM), F32),
                   jax.ShapeDtypeStruct((1, POOL_DIM), F32)],
        scratch_shapes=[pltpu.VMEM((s + 2 * _POOL_HALO, POOL_DIM), F32), pltpu.VMEM((s + 2 * _POOL_HALO, POOL_DIM), F32)],
        compiler_params=_params(dimension_semantics=("arbitrary",)))(u, da, w_bd, scale)


_BQ = 128
_KW = _BQ + 2 * N_SIDE
_PAIR = 2 * HEAD_DIM
_NEG = -1e30
_ATTN_UNROLL = 4
_SCORE_SCALE = HEAD_DIM ** -0.5


def _stack_heads(x):
    lane_head = lax.broadcasted_iota(jnp.int32, x.shape, 1) // HEAD_DIM
    zero = jnp.zeros_like(x)
    return jnp.concatenate([jnp.where(lane_head == 0, x, zero), jnp.where(lane_head == 1, x, zero)], axis=0)


def _unstack_heads(x):
    lane_head = lax.broadcasted_iota(jnp.int32, (_BQ, _PAIR), 1) // HEAD_DIM
    return jnp.where(lane_head == 0, x[:_BQ], x[_BQ:])


def _stack_cols(x):
    return jnp.concatenate([x[:, 0:1], x[:, HEAD_DIM:HEAD_DIM + 1]], axis=0)


def _fill_bias(bias_ref, slopes_ref, dilation):
    row = lax.broadcasted_iota(jnp.int32, (2 * _BQ, _KW), 0)
    col = lax.broadcasted_iota(jnp.int32, (2 * _BQ, _KW), 1)
    pair = 2 * pl.program_id(0)
    slope = jnp.where(row < _BQ, slopes_ref[pair], slopes_ref[pair + 1]) * float(dilation)

    @pl.when(pl.program_id(1) == 0)
    def _():
        for j in range(3):
            dist = jnp.abs(col - (row & (_BQ - 1)) - j * N_SIDE)
            bias_ref[j] = jnp.where(dist <= N_SIDE, -slope * dist.astype(F32), _NEG)


def _block_window(i, n_blocks, length):
    q0 = pl.multiple_of(i * _BQ, _BQ)
    ws = pl.multiple_of(jnp.clip(q0 - N_SIDE, 0, length - _KW), N_SIDE)
    return q0, ws, jnp.where(i == 0, 0, jnp.where(i == n_blocks - 1, 2, 1))


def _residue_rows(dilation, start, count):
    if dilation == 1:
        return pl.ds(start, count)
    return pl.ds(start * dilation + pl.program_id(1), count, stride=dilation)


def _attn_call(body, name, dilation, seq, n_in, n_out, scratch, buffers):
    col = pl.BlockSpec((seq, _PAIR), lambda c, r: (0, c), pipeline_mode=pl.Buffered(buffers))
    return pl.pallas_call(
        body, name=name, grid=(GROUP_DIM // _PAIR, dilation),
        in_specs=[pl.BlockSpec(memory_space=pltpu.SMEM)] + [col] * n_in, out_specs=[col] * n_out,
        out_shape=[jax.ShapeDtypeStruct((seq, GROUP_DIM), F32)] * n_out, scratch_shapes=scratch,
        compiler_params=_params(dimension_semantics=("arbitrary", "arbitrary")))


def _attn_fwd(q, k, v, slopes, dilation, name):
    seq = q.shape[0]
    length = seq // dilation
    n_blocks = length // _BQ

    def body(sl_ref, q_ref, k_ref, v_ref, o_ref, lse_ref, qs, ks, vs, bias_ref):
        all_rows = _residue_rows(dilation, 0, length)
        qs[...] = (q_ref[all_rows, :] * _SCORE_SCALE).astype(BF16)
        ks[...] = k_ref[all_rows, :].astype(BF16)
        vs[...] = v_ref[all_rows, :].astype(BF16)
        _fill_bias(bias_ref, sl_ref, dilation)

        def block(i, carry):
            q0, ws, which = _block_window(i, n_blocks, length)
            kw = ks[pl.ds(ws, _KW), :]
            vw = vs[pl.ds(ws, _KW), :]
            sc = _dot_nt(_stack_heads(qs[pl.ds(q0, _BQ), :]), kw) + bias_ref[which]
            m = jnp.max(sc, axis=-1, keepdims=True)
            p = jnp.exp(sc - m)
            den = jnp.sum(p, axis=-1, keepdims=True)
            rows = _residue_rows(dilation, q0, _BQ)
            o_ref[rows, :] = _unstack_heads(_dot(p.astype(BF16), vw) / den)
            lse_ref[rows, :] = _unstack_heads(jnp.broadcast_to(m + jnp.log(den), (2 * _BQ, _PAIR)))
            return carry

        lax.fori_loop(0, n_blocks, block, 0, unroll=min(_ATTN_UNROLL, n_blocks))

    stage = pltpu.VMEM((length, _PAIR), BF16)
    bias = pltpu.VMEM((3, 2 * _BQ, _KW), F32)
    return _attn_call(body, name, dilation, seq, 3, 2, [stage] * 3 + [bias], 2)(slopes, q, k, v)


def _attn_bwd(q, k, v, do, lse, cterm, slopes, dilation, name):
    seq = q.shape[0]
    length = seq // dilation
    n_blocks = length // _BQ

    def body(sl_ref, q_ref, k_ref, v_ref, do_ref, lse_ref, c_ref, dq_ref, dk_ref, dv_ref, qs, ks, vs, dos, dk_acc, dv_acc, bias_ref):
        all_rows = _residue_rows(dilation, 0, length)
        qs[...] = (q_ref[all_rows, :] * _SCORE_SCALE).astype(BF16)
        for src, dst in ((k_ref, ks), (v_ref, vs), (do_ref, dos)):
            dst[...] = src[all_rows, :].astype(BF16)
        dk_acc[...] = jnp.zeros_like(dk_acc)
        dv_acc[...] = jnp.zeros_like(dv_acc)
        _fill_bias(bias_ref, sl_ref, dilation)

        def block(i, carry):
            q0, ws, which = _block_window(i, n_blocks, length)
            rows = _residue_rows(dilation, q0, _BQ)
            qm = _stack_heads(qs[pl.ds(q0, _BQ), :])
            dom = _stack_heads(dos[pl.ds(q0, _BQ), :])
            kw = ks[pl.ds(ws, _KW), :]
            vw = vs[pl.ds(ws, _KW), :]
            p = jnp.exp(_dot_nt(qm, kw) + bias_ref[which] - _stack_cols(lse_ref[rows, :]))
            ds = (p * (_dot_nt(dom, vw) + _stack_cols(c_ref[rows, :]))).astype(BF16)
            dq_ref[rows, :] = _unstack_heads(_dot(ds, kw)) * _SCORE_SCALE
            dk_acc[pl.ds(ws, _KW), :] += _dot_tn(ds, qm)
            dv_acc[pl.ds(ws, _KW), :] += _dot_tn(p.astype(BF16), dom)
            return carry

        lax.fori_loop(0, n_blocks, block, 0, unroll=min(_ATTN_UNROLL, n_blocks))
        dk_ref[all_rows, :] = dk_acc[...]
        dv_ref[all_rows, :] = dv_acc[...]

    stage = pltpu.VMEM((length, _PAIR), BF16)
    acc = pltpu.VMEM((length, _PAIR), F32)
    bias = pltpu.VMEM((3, 2 * _BQ, _KW), F32)
    return _attn_call(body, name, dilation, seq, 6, 3, [stage] * 4 + [acc] * 2 + [bias], 1)(slopes, q, k, v, do, lse, cterm)


def _group_weights(lses):
    m = jnp.maximum(jnp.maximum(lses[0], lses[1]), lses[2])
    es = [jnp.exp(l - m) for l in lses]
    den = es[0] + es[1] + es[2]
    return [e / den for e in es]


def _out_fwd(a_pool, outs, lses, x, w_out, g, name, tm=512):
    s, d = x.shape
    width = POOL_DIM + 3 * GROUP_DIM

    def body(ap_ref, o0, o1, o2, l0, l1, l2, x_ref, w_ref, g_ref, xo_ref, mix_ref, cat_ref):
        alphas = _group_weights([l0[...], l1[...], l2[...]])
        cat = jnp.concatenate([ap_ref[...]] + [(o[...] * al).astype(BF16) for o, al in zip((o0, o1, o2), alphas)], axis=1)
        cat_ref[...] = cat
        mix = _dot(cat, w_ref[...])
        mix_ref[...] = mix
        xo_ref[...] = x_ref[...] + mix * _inv_rms(mix) * g_ref[...]

    return pl.pallas_call(
        body, name=name, grid=(s // tm,),
        in_specs=[_rows(tm, POOL_DIM)] + [_rows(tm, GROUP_DIM)] * 6 + [_rows(tm, d), _resident(w_out.shape), _const((1, d))],
        out_specs=[_rows(tm, d), _rows(tm, d), _rows(tm, width)],
        out_shape=[jax.ShapeDtypeStruct((s, d), F32), jax.ShapeDtypeStruct((s, d), F32), jax.ShapeDtypeStruct((s, width), BF16)],
        compiler_params=_params(dimension_semantics=("arbitrary",)))(a_pool, *outs, *lses, x, w_out, g)


def _out_bwd(dxo, mix, outs, lses, w_out, g, head_ones, name, tm=512):
    s, d = mix.shape

    def body(dxo_ref, mix_ref, o0, o1, o2, l0, l1, l2, w_ref, g_ref, ones_ref, dpool_ref, dmix_ref, do0, do1, do2, c0, c1, c2, dg_ref):
        mv = mix_ref[...]
        dmix, dg = _rms_bwd(mv, _inv_rms(mv), g_ref[...], dxo_ref[...])
        dmb = dmix.astype(BF16)
        dmix_ref[...] = dmb
        _accumulate(dg_ref, dg)
        dcat = _dot_nt(dmb, w_ref[...])
        dpool_ref[...] = dcat[:, :POOL_DIM]
        alphas = _group_weights([l0[...], l1[...], l2[...]])
        das = [dcat[:, POOL_DIM + GROUP_DIM * j:POOL_DIM + GROUP_DIM * (j + 1)] for j in range(3)]
        prod = sum(da * (o[...] * al) for da, o, al in zip(das, (o0, o1, o2), alphas))
        hi = prod.astype(BF16)
        lo = (prod - hi.astype(F32)).astype(BF16)
        total = _dot(hi, ones_ref[...]) + _dot(lo, ones_ref[...])
        for da, al, do_ref, c_ref in zip(das, alphas, (do0, do1, do2), (c0, c1, c2)):
            do_ref[...] = da * al
            c_ref[...] = -al * total

    return pl.pallas_call(
        body, name=name, grid=(s // tm,),
        in_specs=[_rows(tm, d), _rows(tm, d)] + [_rows(tm, GROUP_DIM)] * 6 + [_resident(w_out.shape), _const((1, d)),
                                                                             _const((GROUP_DIM, GROUP_DIM))],
        out_specs=[_rows(tm, POOL_DIM), _rows(tm, d)] + [_rows(tm, GROUP_DIM)] * 6 + [_const((1, d))],
        out_shape=[jax.ShapeDtypeStruct((s, POOL_DIM), F32), jax.ShapeDtypeStruct((s, d), BF16)]
        + [jax.ShapeDtypeStruct((s, GROUP_DIM), F32)] * 6 + [jax.ShapeDtypeStruct((1, d), F32)],
        compiler_params=_params(dimension_semantics=("arbitrary",)))(dxo, mix, *outs, *lses, w_out, g, head_ones)


def _alibi_slopes():
    return np.array([2.0 ** (-8.0 * (i + 1) / N_ATTN_HEADS) for i in range(N_ATTN_HEADS)], np.float32)


def _block_diag(w_lin):
    n, c, _ = w_lin.shape
    eye = jnp.eye(n, dtype=w_lin.dtype)
    return (eye[:, None, :, None] * w_lin[:, :, None, :]).reshape(n * c, n * c)


class _NoExchange:
    def __init__(self, full):
        self.full, self.grads = full, {}

    def first_weights(self):
        return self.full

    def riders(self, host):
        return []

    def landed(self, host, results):
        pass

    def rest_weights(self):
        return self.full

    def gradient(self, name, grad):
        self.grads[name] = grad


def _local_step(x, target, small, exchange):
    s, d = x.shape
    slopes = _alibi_slopes()
    group_slopes = [jnp.asarray(slopes[4 * g:4 * g + 4]) for g in range(3)]
    w_bd = _block_diag(small["w_pool_lin"]).astype(BF16)
    head_ones = jnp.asarray(np.kron(np.eye(GROUP_DIM // HEAD_DIM), np.ones((HEAD_DIM, HEAD_DIM))), BF16)

    full = exchange.first_weights()
    (x1, a1, b1, f1), riding = _ffn_fwd(x, small["g_ffn1_pre"], full["w1_gate"], full["w1_up"], full["w1_down"],
                                        small["g_ffn1_post"], None, "ffn1_fwd", exchange.riders("ffn1_fwd"))
    exchange.landed("ffn1_fwd", riding)
    full = {**full, **exchange.rest_weights()}
    u, *parts = _in_fwd(x1, small["g_mix_pre"], full["w_in"], "in_fwd")
    qs, ks, vs = parts[0:3], parts[3:6], parts[6:9]
    a_pool = _pool_fwd(u, w_bd, small["pool_scale"], "pool_fwd")
    outs, lses = [], []
    for g, dil in enumerate(DILATIONS):
        o, lse = _attn_fwd(qs[g], ks[g], vs[g], group_slopes[g], dil, f"attn_fwd{g}")
        outs.append(o)
        lses.append(lse)
    x2, mix, cat = _out_fwd(a_pool, outs, lses, x1, full["w_out"], small["g_mix_post"], "out_fwd")
    (dx3, a2, b2, f2, loss_part), _ = _ffn_fwd(x2, small["g_ffn2_pre"], full["w2_gate"], full["w2_up"], full["w2_down"],
                                               small["g_ffn2_post"], target, "ffn2_fwd")

    small_grads = {}

    def hosted(call, host, *args):
        results, riding = call(*args, host, exchange.riders(host))
        exchange.landed(host, riding)
        return results

    def ffn_backward(tag, dxo, x_in, f, a, b):
        n = tag[-1]
        dx, hh, da, db, df, h, dg_pre, dg_post = hosted(
            _ffn_bwd, f"{tag}_bwd", dxo, x_in, f, a, b, small[f"g_{tag}_pre"], small[f"g_{tag}_post"],
            full[f"w{n}_gate"], full[f"w{n}_up"], full[f"w{n}_down"])
        for part, lhs, rhs in (("down", hh, df), ("gate", da, h), ("up", db, h)):
            exchange.gradient(f"w{n}_{part}", hosted(_wgrad, f"{tag}_wgrad_{part}", lhs, rhs))
        small_grads[f"g_{tag}_pre"], small_grads[f"g_{tag}_post"] = dg_pre, dg_post
        return dx

    dx2 = ffn_backward("ffn2", dx3, x2, f2, a2, b2)
    dpool, dmix, *dos_cs, small_grads["g_mix_post"] = _out_bwd(dx2, mix, outs, lses, full["w_out"], small["g_mix_post"],
                                                               head_ones, "out_bwd")
    dos, cs = dos_cs[:3], dos_cs[3:]
    dqs, dks, dvs = [], [], []
    for g, dil in enumerate(DILATIONS):
        dq, dk, dv = _attn_bwd(qs[g], ks[g], vs[g], dos[g], lses[g], cs[g], group_slopes[g], dil, f"attn_bwd{g}")
        dqs.append(dq)
        dks.append(dk)
        dvs.append(dv)
    du, dw_bd, small_grads["pool_scale"] = _pool_bwd(u, dpool, w_bd, small["pool_scale"], "pool_bwd")
    n_pool = len(POOL_HALF_WINDOWS)
    small_grads["w_pool_lin"] = jnp.stack(
        [dw_bd[HEAD_DIM * g:HEAD_DIM * (g + 1), HEAD_DIM * g:HEAD_DIM * (g + 1)] for g in range(n_pool)])
    dx1, dz, h2, small_grads["g_mix_pre"] = hosted(_in_bwd, "in_bwd", du, dqs + dks + dvs, x1, dx2, small["g_mix_pre"], full["w_in"])
    exchange.gradient("w_in", hosted(_wgrad, "wgrad_in", dz, h2))
    dx0 = ffn_backward("ffn1", dx1, x, f1, a1, b1)
    exchange.gradient("w_out", hosted(_wgrad, "wgrad_out", cat, dmix))
    return loss_part[0, 0], dx0, small_grads


SEGMENTS = ("w1_gate", "w1_up", "w1_down", "w_in", "w_out", "w2_gate", "w2_up", "w2_down")
TRANSPOSED = ("w1_gate", "w1_up", "w_in", "w2_gate", "w2_up")
ROWS_OUTSIDE = ("w1_gate", "w1_up", "w2_gate", "w2_up")
HALF = 512


def _place():
    x, y, c = lax.axis_index("x"), lax.axis_index("y"), lax.axis_index("c")
    other_chips = [(1 - x, y), (x, 1 - y), (1 - x, 1 - y)]
    return x, y, c, other_chips


def _chip_rows(chip, rows):
    return pl.ds(pl.multiple_of((2 * chip[0] + chip[1]) * rows, 16), rows)


def _cols(c):
    return pl.ds(pl.multiple_of(c * HALF, HALF), HALF)


def _cast_shard(w, transpose, place, name, tm=256):
    r, c = w.shape
    if transpose:
        def body(place_ref, w_ref, o_ref):
            o_ref[...] = w_ref[...].T.astype(BF16)

        grid, in_block, out_block, rows = (r // tm,), (tm, c), (c, tm), c
    else:
        def body(place_ref, w_ref, o_ref):
            o_ref[...] = w_ref[...].astype(BF16)

        grid, in_block, out_block, rows = (1,), (r, c), (r, c), r
    return pl.pallas_call(
        body, name=name,
        grid_spec=pltpu.PrefetchScalarGridSpec(
            num_scalar_prefetch=1, grid=grid, in_specs=[pl.BlockSpec(in_block, lambda i, place: (i, 0))],
            out_specs=pl.BlockSpec(out_block, lambda i, place: (place[0], i))),
        out_shape=jax.ShapeDtypeStruct((N_CHIPS * rows, 1024), BF16),
        compiler_params=_params(dimension_semantics=("arbitrary",)))(place, w)


def _gather_weights(bufs):
    n = len(bufs)
    rows = [b.shape[0] // N_CHIPS for b in bufs]

    def body(*refs):
        outs = refs[n:2 * n]
        send_sems, recv_sems, fwd_send_sems, fwd_recv_sems = refs[2 * n:]
        x, y, c, chips = _place()
        me = (x, y)

        def ici(j, k, src_chip, to):
            blk = outs[k].at[_chip_rows(src_chip, rows[k]), _cols(c)]
            return pltpu.make_async_remote_copy(src_ref=blk, dst_ref=blk, send_sem=send_sems.at[j, k], recv_sem=recv_sems.at[j, k],
                                                device_id=to, device_id_type=MESH)

        def d2d(j, k, src_chip, half):
            blk = outs[k].at[_chip_rows(src_chip, rows[k]), _cols(half)]
            return pltpu.make_async_remote_copy(src_ref=blk, dst_ref=blk, send_sem=fwd_send_sems.at[j, k],
                                                recv_sem=fwd_recv_sems.at[j, k], device_id=(x, y, 1 - c), device_id_type=MESH)

        sends = [ici(j, k, me, (*chip, c)) for j, chip in enumerate(chips) for k in range(n)]
        for cp in sends:
            cp.start()
        forwards = []
        for j, chip in enumerate(chips):
            for k in range(n):
                ici(j, k, chip, (x, y, c)).wait_recv()
                fw = d2d(j, k, chip, c)
                fw.start()
                forwards.append(fw)
        for j, chip in enumerate(chips):
            for k in range(n):
                d2d(j, k, chip, 1 - c).wait_recv()
        for cp in sends + forwards:
            cp.wait_send()

    any_spec = pl.BlockSpec(memory_space=pl.ANY)
    return pl.pallas_call(
        body, name="gather_weights", in_specs=[any_spec] * n, out_specs=[any_spec] * n,
        out_shape=[jax.ShapeDtypeStruct(b.shape, b.dtype) for b in bufs], input_output_aliases={k: k for k in range(n)},
        scratch_shapes=[pltpu.SemaphoreType.DMA((3, n)), pltpu.SemaphoreType.DMA((3, n)),
                        pltpu.SemaphoreType.DMA((3, n)), pltpu.SemaphoreType.DMA((3, n))])(*bufs)


def _gather_rider(bufs):
    n = len(bufs)
    rows = [b.shape[0] // N_CHIPS for b in bufs]

    def copies(outs, send_sems, recv_sems, inbound):
        x, y, c, chips = _place()
        for j, chip in enumerate(chips):
            for k in range(n):
                src_chip = chip if inbound else (x, y)
                blk = outs[k].at[_chip_rows(src_chip, rows[k]), _cols(c)]
                yield pltpu.make_async_remote_copy(src_ref=blk, dst_ref=blk, send_sem=send_sems.at[j, k], recv_sem=recv_sems.at[j, k],
                                                   device_id=(*chip, c), device_id_type=MESH)

    def start(ins, outs, send_sems, recv_sems):
        for cp in copies(outs, send_sems, recv_sems, False):
            cp.start()

    def wait(ins, outs, send_sems, recv_sems):
        for cp in copies(outs, send_sems, recv_sems, True):
            cp.wait_recv()
        for cp in copies(outs, send_sems, recv_sems, False):
            cp.wait_send()

    return _Rider(list(bufs), None, (3, n), start, wait)


def _forward_halves(bufs, name):
    n = len(bufs)
    rows = [b.shape[0] // N_CHIPS for b in bufs]

    def body(*refs):
        outs = refs[n:2 * n]
        send_sems, recv_sems = refs[2 * n:]
        x, y, c, chips = _place()

        def d2d(j, k, chip, half):
            blk = outs[k].at[_chip_rows(chip, rows[k]), _cols(half)]
            return pltpu.make_async_remote_copy(src_ref=blk, dst_ref=blk, send_sem=send_sems.at[j, k], recv_sem=recv_sems.at[j, k],
                                                device_id=(x, y, 1 - c), device_id_type=MESH)

        forwards = [d2d(j, k, chip, c) for j, chip in enumerate(chips) for k in range(n)]
        for cp in forwards:
            cp.start()
        for j, chip in enumerate(chips):
            for k in range(n):
                d2d(j, k, chip, 1 - c).wait_recv()
        for cp in forwards:
            cp.wait_send()

    any_spec = pl.BlockSpec(memory_space=pl.ANY)
    return pl.pallas_call(
        body, name=name, in_specs=[any_spec] * n, out_specs=[any_spec] * n,
        out_shape=[jax.ShapeDtypeStruct(b.shape, b.dtype) for b in bufs], input_output_aliases={k: k for k in range(n)},
        scratch_shapes=[pltpu.SemaphoreType.DMA((3, n)), pltpu.SemaphoreType.DMA((3, n))])(*bufs)


def _sibling_rider(grads):
    n = len(grads)

    def copies(ins, outs, send_sems, recv_sems):
        x, y, c, _ = _place()
        return [pltpu.make_async_remote_copy(src_ref=ins[k].at[:, pl.ds(1 - c, 1)], dst_ref=outs[k], send_sem=send_sems.at[k],
                                             recv_sem=recv_sems.at[k], device_id=(x, y, 1 - c), device_id_type=MESH)
                for k in range(n)]

    def start(*refs):
        for cp in copies(*refs):
            cp.start()

    def wait(*refs):
        for cp in copies(*refs):
            cp.wait()

    return _Rider(list(grads), [jax.ShapeDtypeStruct((N_CHIPS, 1) + g.shape[2:], F32) for g in grads], (n,), start, wait)


def _alone(rider, name):
    n, n_out = len(rider.operands), len(rider.landing)

    def body(*refs):
        rider.start(refs[:n], refs[n:n + n_out], *refs[n + n_out:])
        rider.wait(refs[:n], refs[n:n + n_out], *refs[n + n_out:])

    any_spec = pl.BlockSpec(memory_space=pl.ANY)
    return pl.pallas_call(body, name=name, in_specs=[any_spec] * n, out_specs=[any_spec] * n_out, out_shape=rider.landing,
                          scratch_shapes=[pltpu.SemaphoreType.DMA(rider.sems)] * 2)(*rider.operands)


def _chip_sum(grad, from_sibling, place, name):
    rh, width = grad.shape[2:]

    def body(place_ref, g_ref, s_ref, own_ref, all_ref):
        total = g_ref[0, 0] + s_ref[0, 0]
        all_ref[0, 0] = total.astype(BF16)

        @pl.when(pl.program_id(0) == place_ref[0])
        def _():
            own_ref[0] = total

    blk = (1, 1, rh, width)
    return pl.pallas_call(
        body, name=name,
        grid_spec=pltpu.PrefetchScalarGridSpec(
            num_scalar_prefetch=1, grid=(N_CHIPS,),
            in_specs=[pl.BlockSpec(blk, lambda p, place: (p, place[1], 0, 0)), pl.BlockSpec(blk, lambda p, place: (p, 0, 0, 0))],
            out_specs=[pl.BlockSpec((1, rh, width), lambda p, place: (0, 0, 0)), pl.BlockSpec(blk, lambda p, place: (p, 0, 0, 0))]),
        out_shape=[jax.ShapeDtypeStruct((1, rh, width), F32), jax.ShapeDtypeStruct((N_CHIPS, 1, rh, width), BF16)],
        compiler_params=_params(dimension_semantics=("arbitrary",)))(place, grad, from_sibling)


def _scatter_rider(sums):
    n = len(sums)

    def copies(ins, outs, send_sems, recv_sems):
        x, y, c, chips = _place()
        return [pltpu.make_async_remote_copy(src_ref=ins[k].at[pl.ds(2 * chip[0] + chip[1], 1)], dst_ref=outs[k].at[pl.ds(j, 1)],
                                             send_sem=send_sems.at[j, k], recv_sem=recv_sems.at[j, k],
                                             device_id=(*chip, c), device_id_type=MESH)
                for j, chip in enumerate(chips) for k in range(n)]

    def start(*refs):
        for cp in copies(*refs):
            cp.start()

    def wait(*refs):
        for cp in copies(*refs):
            cp.wait()

    return _Rider(list(sums), [jax.ShapeDtypeStruct((3,) + sm.shape[1:], BF16) for sm in sums], (3, n), start, wait)


def _total_sum(own, received, name):
    def body(o_ref, r_ref, t_ref):
        total = o_ref[0]
        for j in range(3):
            total = total + r_ref[j, 0].astype(F32)
        t_ref[0] = total

    return pl.pallas_call(body, name=name, out_shape=jax.ShapeDtypeStruct(own.shape, F32), compiler_params=_params())(own, received)


def _swap_halves(halves):
    n = len(halves)

    def body(*refs):
        ins, outs = refs[:n], refs[n:2 * n]
        send_sems, recv_sems = refs[2 * n:]
        x, y, c, _ = _place()
        copies = [pltpu.make_async_remote_copy(src_ref=ins[k], dst_ref=outs[k], send_sem=send_sems.at[k],
                                               recv_sem=recv_sems.at[k], device_id=(x, y, 1 - c), device_id_type=MESH)
                  for k in range(n)]
        for cp in copies:
            cp.start()
        for cp in copies:
            cp.wait()

    any_spec = pl.BlockSpec(memory_space=pl.ANY)
    return pl.pallas_call(
        body, name="swap_halves", in_specs=[any_spec] * n, out_specs=[any_spec] * n,
        out_shape=[jax.ShapeDtypeStruct(h.shape, F32) for h in halves],
        scratch_shapes=[pltpu.SemaphoreType.DMA((n,)), pltpu.SemaphoreType.DMA((n,))])(*halves)


N_DEV = 8


def _gather_small(block):
    m_per, width = block.shape

    def body(x_ref, out_ref, send_sems, recv_sems, local_sem):
        x, y, c, chips = _place()
        me, sibling = (x, y, c), (x, y, 1 - c)

        def rows(px, py, pc):
            return out_ref.at[pl.ds((4 * px + 2 * py + pc) * m_per, m_per), :]

        def copy(k, blk, to, src=None):
            return pltpu.make_async_remote_copy(src_ref=rows(*blk) if src is None else src, dst_ref=rows(*blk),
                                                send_sem=send_sems.at[k], recv_sem=recv_sems.at[k], device_id=to, device_id_type=MESH)

        mine = pltpu.make_async_copy(x_ref, rows(*me), local_sem)
        mine.start()
        first = [copy(0, me, sibling, src=x_ref)] + [copy(1 + j, me, (*chip, c), src=x_ref) for j, chip in enumerate(chips)]
        for cp in first:
            cp.start()
        passed = [copy(4 + j, (*chip, c), sibling) for j, chip in enumerate(chips)]
        for j, chip in enumerate(chips):
            copy(1 + j, (*chip, c), me).wait_recv()
            passed[j].start()
        copy(0, sibling, me).wait_recv()
        for j, chip in enumerate(chips):
            copy(4 + j, (*chip, 1 - c), me).wait_recv()
        for cp in first + passed:
            cp.wait_send()
        mine.wait()

    vmem = pl.BlockSpec(memory_space=pltpu.VMEM)
    return pl.pallas_call(body, name="gather_small", out_shape=jax.ShapeDtypeStruct((N_DEV * m_per, width), F32),
                          in_specs=[vmem], out_specs=vmem,
                          scratch_shapes=[pltpu.SemaphoreType.DMA((7,)), pltpu.SemaphoreType.DMA((7,)),
                                          pltpu.SemaphoreType.DMA])(block)


def _adamw_math(w, g, m, v):
    m = ADAM_B1 * m + (1.0 - ADAM_B1) * g
    v = ADAM_B2 * v + (1.0 - ADAM_B2) * (g * g)
    m_hat = m / (1.0 - ADAM_B1 ** ADAM_STEP)
    v_hat = v / (1.0 - ADAM_B2 ** ADAM_STEP)
    delta = -ADAM_LR * (m_hat / (jnp.sqrt(v_hat) + ADAM_EPS) + ADAM_WD * w)
    return delta, m, v


def _adamw(w, mine, siblings, place, m, v, transposed, name):
    if transposed:
        def body(place_ref, w_ref, mine_ref, sib_ref, m_ref, v_ref, go_ref, d_ref, mo_ref, vo_ref):
            first = place_ref[1] == 0
            g = jnp.concatenate([jnp.where(first, mine_ref[0], sib_ref[0]), jnp.where(first, sib_ref[0], mine_ref[0])], axis=0).T
            go_ref[...] = g
            d_ref[...], mo_ref[...], vo_ref[...] = _adamw_math(w_ref[...], g, m_ref[...], v_ref[...])

        vmem = pl.BlockSpec(memory_space=pltpu.VMEM)
        return pl.pallas_call(body, name=name, in_specs=[pl.BlockSpec(memory_space=pltpu.SMEM)] + [vmem] * 5, out_specs=[vmem] * 4,
                              out_shape=[jax.ShapeDtypeStruct(w.shape, F32)] * 4, compiler_params=_params())(
                                  place, w, mine, siblings, m, v)

    rh, width = mine.shape[1:]

    def body(place_ref, w_ref, mine_ref, sib_ref, m_ref, v_ref, go_ref, d_ref, mo_ref, vo_ref):
        g = jnp.where(pl.program_id(0) == place_ref[1], mine_ref[0], sib_ref[0])
        go_ref[...] = g
        d_ref[...], mo_ref[...], vo_ref[...] = _adamw_math(w_ref[...], g, m_ref[...], v_ref[...])

    half = pl.BlockSpec((rh, width), lambda h, place: (h, 0))
    whole = pl.BlockSpec((1, rh, width), lambda h, place: (0, 0, 0))
    return pl.pallas_call(
        body, name=name,
        grid_spec=pltpu.PrefetchScalarGridSpec(num_scalar_prefetch=1, grid=(2,), in_specs=[half, whole, whole, half, half],
                                               out_specs=[half] * 4),
        out_shape=[jax.ShapeDtypeStruct(w.shape, F32)] * 4,
        compiler_params=_params(dimension_semantics=("arbitrary",)))(place, w, mine, siblings, m, v)


def _adamw_small(gathered, w, m, v, name):
    def body(ga_ref, w_ref, m_ref, v_ref, go_ref, d_ref, mo_ref, vo_ref):
        g = ga_ref[0]
        for dev in range(1, N_DEV):
            g = g + ga_ref[dev]
        go_ref[...] = g
        d_ref[...], mo_ref[...], vo_ref[...] = _adamw_math(w_ref[...], g, m_ref[...], v_ref[...])

    return pl.pallas_call(body, name=name, out_shape=[jax.ShapeDtypeStruct(w.shape, F32)] * 4,
                          compiler_params=_params())(gathered, w, m, v)


class _Exchange:
    FIRST = ("w1_gate", "w1_up", "w1_down")
    HOSTS = {"ffn2_wgrad_gate": (("w2_down",), ()), "ffn2_wgrad_up": (("w2_gate",), ("w2_down",)),
             "in_bwd": (("w2_up",), ("w2_gate",)), "wgrad_in": ((), ("w2_up",)),
             "ffn1_wgrad_down": ((), ("w_in",)), "ffn1_wgrad_gate": (("w1_down",), ()), "ffn1_wgrad_up": ((), ("w1_down", "w1_gate")),
             "wgrad_out": ((), ("w1_up",))}
    ALONE = ("w_in", "w1_gate", "w1_up", "w_out")

    def __init__(self, bufs, place):
        self.bufs, self.place = bufs, place
        self.later = [k for k in SEGMENTS if k not in self.FIRST]
        self.split, self.own, self.to_send, self.received = {}, {}, {}, {}

    def first_weights(self):
        return dict(zip(self.FIRST, _gather_weights([self.bufs[k] for k in self.FIRST])))

    def riders(self, host):
        if host == "ffn1_fwd":
            return [_gather_rider([self.bufs[k] for k in self.later])]
        halves, sums = self.HOSTS.get(host, ((), ()))
        return ([_sibling_rider([self.split[k] for k in halves])] if halves else []) + (
            [_scatter_rider([self.to_send[k] for k in sums])] if sums else [])

    def landed(self, host, results):
        if host == "ffn1_fwd":
            self.rest = dict(zip(self.later, _forward_halves(results[0], "gather_rest_forward")))
            return
        halves, sums = self.HOSTS.get(host, ((), ()))
        if halves:
            self._chip_sums(halves, results[0])
        if sums:
            self.received.update(zip(sums, results[-1]))

    def rest_weights(self):
        return self.rest

    def gradient(self, name, grad):
        self.split[name] = grad.reshape(N_CHIPS, 2, grad.shape[0] // (2 * N_CHIPS), grad.shape[1])
        if name in self.ALONE:
            self._chip_sums([name], _alone(_sibling_rider([self.split[name]]), f"reduce_sibling_{name}"))

    def _chip_sums(self, names, from_sibling):
        for k, fs in zip(names, from_sibling):
            self.own[k], self.to_send[k] = _chip_sum(self.split[k], fs, self.place, f"chip_sum_{k}")

    def summed_halves(self):
        late = [k for k in SEGMENTS if k not in self.received]
        self.received.update(zip(late, _alone(_scatter_rider([self.to_send[k] for k in late]), "reduce_chips_last")))
        return [_total_sum(self.own[k], self.received[k], f"total_{k}") for k in SEGMENTS]


SMALL = ("g_ffn1_pre", "g_ffn1_post", "g_mix_pre", "w_pool_lin", "pool_scale", "g_mix_post", "g_ffn2_pre", "g_ffn2_post")
WEIGHTS = ("g_ffn1_pre", "w1_gate", "w1_up", "w1_down", "g_ffn1_post", "g_mix_pre", "w_in", "w_pool_lin", "pool_scale", "w_out",
           "g_mix_post", "g_ffn2_pre", "w2_gate", "w2_up", "w2_down", "g_ffn2_post")
LANES = 128


def _pack_small(tree):
    flat = jnp.concatenate([tree[k].reshape(-1) for k in SMALL])
    rows = -(-flat.shape[0] // (8 * LANES)) * 8
    return jnp.pad(flat, (0, rows * LANES - flat.shape[0])).reshape(rows, LANES)


def _unpack_small(packed, like):
    flat, out, at = packed.reshape(-1), {}, 0
    for k in SMALL:
        size = math.prod(like[k].shape)
        out[k] = flat[at:at + size].reshape(like[k].shape)
        at += size
    return out


def kernel(x, g_ffn1_pre, w1_gate, w1_up, w1_down, g_ffn1_post, g_mix_pre, w_in, w_pool_lin, pool_scale, w_out, g_mix_post, g_ffn2_pre, w2_gate, w2_up, w2_down, g_ffn2_post, loss_target, m_g_ffn1_pre, m_w1_gate, m_w1_up, m_w1_down, m_g_ffn1_post, m_g_mix_pre, m_w_in, m_w_pool_lin, m_pool_scale, m_w_out, m_g_mix_post, m_g_ffn2_pre, m_w2_gate, m_w2_up, m_w2_down, m_g_ffn2_post, v_g_ffn1_pre, v_w1_gate, v_w1_up, v_w1_down, v_g_ffn1_post, v_g_mix_pre, v_w_in, v_w_pool_lin, v_pool_scale, v_w_out, v_g_mix_post, v_g_ffn2_pre, v_w2_gate, v_w2_up, v_w2_down, v_g_ffn2_post):
    given = dict(locals())
    w = {k: given[k] for k in WEIGHTS}
    m = {k: given["m_" + k] for k in WEIGHTS}
    v = {k: given["v_" + k] for k in WEIGHTS}
    small = {k: (w[k][0] if k == "w_pool_lin" else w[k].reshape(1, -1)) for k in SMALL}

    place = jnp.stack([2 * lax.axis_index("x") + lax.axis_index("y"), lax.axis_index("c")]).astype(jnp.int32)
    def as_rows(a, k):
        return jnp.swapaxes(a, 1, 2)[0] if k in ROWS_OUTSIDE else a[0]

    def as_given(a, k):
        return jnp.swapaxes(a[None], 1, 2) if k in ROWS_OUTSIDE else a[None]

    in_kernel = [k for k in TRANSPOSED if k not in ROWS_OUTSIDE]
    exchange = _Exchange({k: _cast_shard(as_rows(w[k], k), k in in_kernel, place, f"cast_{k}") for k in SEGMENTS}, place)
    loss_part, grad_x, small_grads = _local_step(x[0], loss_target[0], small, exchange)
    loss = lax.psum(loss_part, ("x", "y", "c"))

    halves = exchange.summed_halves()
    from_sibling = _swap_halves(halves)

    out_grad, out_delta, out_m, out_v = {}, {}, {}, {}
    for k, mine, sib in zip(SEGMENTS, halves, from_sibling):
        out_grad[k], out_delta[k], out_m[k], out_v[k] = (
            as_given(a, k) for a in _adamw(as_rows(w[k], k), mine, sib, place, as_rows(m[k], k), as_rows(v[k], k),
                                           k in in_kernel, f"adamw_{k}"))

    small_grads["w_pool_lin"] = small_grads["w_pool_lin"][None]
    packed = _pack_small(small_grads)
    gathered = _gather_small(packed).reshape(N_DEV, *packed.shape)
    like = {k: w[k] for k in SMALL}
    results = _adamw_small(gathered, _pack_small(like), _pack_small({k: m[k] for k in SMALL}),
                           _pack_small({k: v[k] for k in SMALL}), "adamw_small")
    for tree, res in zip((out_grad, out_delta, out_m, out_v), results):
        tree.update(_unpack_small(res, like))

    return (loss, grad_x[None], *[out_grad[k] for k in WEIGHTS], *[out_delta[k] for k in WEIGHTS],
            *[out_m[k] for k in WEIGHTS], *[out_v[k] for k in WEIGHTS])
```

```python
import math
import typing

import numpy as np
import jax
import jax.numpy as jnp
from jax import lax
from jax.experimental import pallas as pl
from jax.experimental.pallas import tpu as pltpu

F32 = jnp.float32
BF16 = jnp.bfloat16
MESH = pl.DeviceIdType.MESH

RMS_EPS = 1e-6
HEAD_DIM = 64
POOL_HALF_WINDOWS = (1, 2, 4, 8)
POOL_DIM = 256
GROUP_DIM = 256
DILATIONS = (1, 4, 16)
N_SIDE = 64
N_ATTN_HEADS = 12
ADAM_LR, ADAM_B1, ADAM_B2, ADAM_EPS, ADAM_WD, ADAM_STEP = 0.001, 0.9, 0.999, 1e-08, 0.01, 10

N_CHIPS = 4
V7X_VMEM_LIMIT = 60 * 1024 * 1024

_NT = (((1,), (1,)), ((), ()))
_TN = (((0,), (0,)), ((), ()))


def _dot(a, b):
    return jnp.dot(a, b, preferred_element_type=F32)


def _dot_nt(a, b):
    return lax.dot_general(a, b, _NT, preferred_element_type=F32)


def _dot_tn(a, b):
    return lax.dot_general(a, b, _TN, preferred_element_type=F32)


def _params(**kw):
    return pltpu.CompilerParams(vmem_limit_bytes=V7X_VMEM_LIMIT, **kw)


def _rows(tm, width):
    return pl.BlockSpec((tm, width), lambda i: (i, 0))


def _resident(shape):
    return pl.BlockSpec(shape, lambda i: (0,) * len(shape), pipeline_mode=pl.Buffered(1))


def _const(shape):
    return pl.BlockSpec(shape, lambda i: (0,) * len(shape))


def _tile(rows, cap):
    return max(t for t in range(16, cap + 1, 16) if rows % t == 0)


def _inv_rms(x):
    return lax.rsqrt(jnp.mean(x * x, axis=-1, keepdims=True) + RMS_EPS)


def _rms_bwd(x, inv, g, dy):
    n = x * inv
    dn = dy * g
    dx = inv * (dn - n * jnp.mean(dn * n, axis=-1, keepdims=True))
    return dx, jnp.sum(dy * n, axis=0, keepdims=True)


def _accumulate(ref, value):
    @pl.when(pl.program_id(0) == 0)
    def _():
        ref[...] = jnp.zeros_like(ref)

    ref[...] += value


class _Rider(typing.NamedTuple):
    operands: list
    landing: typing.Optional[list]
    sems: tuple
    start: typing.Callable
    wait: typing.Callable


def _hosted_call(body, riders, *, name, steps, in_specs, out_specs, out_shape, args, scratch_shapes=()):
    params = _params(dimension_semantics=("arbitrary",))
    riders = list(riders or [])
    if not riders:
        res = pl.pallas_call(body, name=name, grid=(steps,), in_specs=in_specs, out_specs=out_specs, out_shape=out_shape,
                             scratch_shapes=list(scratch_shapes), compiler_params=params)(*args)
        return list(res), []
    n_in, n_out, n_scratch = len(in_specs), len(out_specs), len(scratch_shapes)
    operands, landing, aliases, spans = [], [], {}, []
    for rd in riders:
        lands = rd.landing if rd.landing is not None else [jax.ShapeDtypeStruct(a.shape, a.dtype) for a in rd.operands]
        if rd.landing is None:
            aliases.update({n_in + len(operands) + i: n_out + len(landing) + i for i in range(len(lands))})
        spans.append((len(operands), len(rd.operands), len(landing), len(lands)))
        operands += rd.operands
        landing += lands
    outs_at = n_in + len(operands)
    scratch_at = outs_at + n_out + len(landing)

    def riding(*refs):
        def each(action):
            for i, (rd, (in_at, n_ops, out_at, n_lands)) in enumerate(zip(riders, spans)):
                sems = refs[scratch_at + n_scratch + 2 * i:scratch_at + n_scratch + 2 * i + 2]
                getattr(rd, action)(refs[n_in + in_at:n_in + in_at + n_ops],
                                    refs[outs_at + n_out + out_at:outs_at + n_out + out_at + n_lands], *sems)

        @pl.when(pl.program_id(0) == 0)
        def _():
            each("start")

        body(*refs[:n_in], *refs[outs_at:outs_at + n_out], *refs[scratch_at:scratch_at + n_scratch])

        @pl.when(pl.program_id(0) == steps - 1)
        def _():
            each("wait")

    any_spec = pl.BlockSpec(memory_space=pl.ANY)
    res = pl.pallas_call(
        riding, name=name, grid=(steps,), in_specs=list(in_specs) + [any_spec] * len(operands),
        out_specs=list(out_specs) + [any_spec] * len(landing), out_shape=list(out_shape) + landing,
        scratch_shapes=list(scratch_shapes) + [pltpu.SemaphoreType.DMA(rd.sems) for rd in riders for _ in range(2)],
        input_output_aliases=aliases, compiler_params=params)(*args, *operands)
    return list(res[:n_out]), [list(res[n_out + out_at:n_out + out_at + n_lands]) for _, _, out_at, n_lands in spans]


_SUB_TILE = 256


def _sub_tiles(tm):
    return [pl.ds(r, _SUB_TILE) for r in range(0, tm, _SUB_TILE)]


def _ffn_fwd(x, g_pre, wg_t, wu_t, wd, g_post, target, name, riders=None, tm=512):
    s, d = x.shape
    ff = wd.shape[0]
    with_loss = target is not None

    def body(*refs):
        if with_loss:
            x_ref, gpre_ref, wg_ref, wu_ref, wd_ref, gpost_ref, t_ref, xo_ref, a_ref, b_ref, f_ref, loss_ref = refs
        else:
            x_ref, gpre_ref, wg_ref, wu_ref, wd_ref, gpost_ref, xo_ref, a_ref, b_ref, f_ref = refs
        loss = 0.0
        for rows in _sub_tiles(tm):
            xv = x_ref[rows, :]
            hb = (xv * _inv_rms(xv) * gpre_ref[...]).astype(BF16)
            a = _dot_nt(hb, wg_ref[...])
            b = _dot_nt(hb, wu_ref[...])
            hh = (a * jax.nn.sigmoid(a)) * b
            f = _dot(hh.astype(BF16), wd_ref[...])
            xo = xv + 0.5 * (f * _inv_rms(f) * gpost_ref[...])
            a_ref[rows, :] = a.astype(BF16)
            b_ref[rows, :] = b.astype(BF16)
            f_ref[rows, :] = f
            if with_loss:
                e = xo - t_ref[rows, :]
                xo_ref[rows, :] = e * (1.0 / d)
                loss = loss + 0.5 * jnp.sum(jnp.mean(e * e, axis=-1, keepdims=True))
            else:
                xo_ref[rows, :] = xo
        if with_loss:
            _accumulate(loss_ref, loss)

    in_specs = [_rows(tm, d), _const((1, d)), _resident((ff, d)), _resident((ff, d)), _resident((ff, d)), _const((1, d))]
    args = [x, g_pre, wg_t, wu_t, wd, g_post]
    out_shape = [jax.ShapeDtypeStruct((s, d), F32), jax.ShapeDtypeStruct((s, ff), BF16),
                 jax.ShapeDtypeStruct((s, ff), BF16), jax.ShapeDtypeStruct((s, d), F32)]
    out_specs = [_rows(tm, d), _rows(tm, ff), _rows(tm, ff), _rows(tm, d)]
    if with_loss:
        in_specs.append(_rows(tm, d))
        args.append(target)
        out_shape.append(jax.ShapeDtypeStruct((8, 128), F32))
        out_specs.append(_const((8, 128)))
    return _hosted_call(body, riders, name=name, steps=s // tm, in_specs=in_specs, out_specs=out_specs, out_shape=out_shape, args=args)


def _ffn_bwd(dxo, x, f, a, b, g_pre, g_post, wg_t, wu_t, wd, name, riders=None, tm=256):
    s, d = x.shape
    ff = wd.shape[0]

    def body(dxo_ref, x_ref, f_ref, a_ref, b_ref, gpre_ref, gpost_ref, wg_ref, wu_ref, wd_ref,
             dx_ref, hh_ref, da_ref, db_ref, df_ref, h_ref, dgpre_ref, dgpost_ref):
        dgpre_sum = dgpost_sum = 0.0
        for rows in _sub_tiles(tm):
            dxo_v = dxo_ref[rows, :]
            fv = f_ref[rows, :]
            df, dgpost = _rms_bwd(fv, _inv_rms(fv), gpost_ref[...], 0.5 * dxo_v)
            dfb = df.astype(BF16)
            dhh = _dot_nt(dfb, wd_ref[...])
            av = a_ref[rows, :].astype(F32)
            bv = b_ref[rows, :].astype(F32)
            sig = jax.nn.sigmoid(av)
            sa = av * sig
            da = (dhh * bv * (sig * (1.0 + av * (1.0 - sig)))).astype(BF16)
            db = (dhh * sa).astype(BF16)
            dh = _dot(da, wg_ref[...]) + _dot(db, wu_ref[...])
            xv = x_ref[rows, :]
            inv = _inv_rms(xv)
            dxn, dgpre = _rms_bwd(xv, inv, gpre_ref[...], dh)
            dx_ref[rows, :] = dxo_v + dxn
            hh_ref[rows, :] = (sa * bv).astype(BF16)
            da_ref[rows, :] = da
            db_ref[rows, :] = db
            df_ref[rows, :] = dfb
            h_ref[rows, :] = (xv * inv * gpre_ref[...]).astype(BF16)
            dgpre_sum, dgpost_sum = dgpre_sum + dgpre, dgpost_sum + dgpost
        _accumulate(dgpre_ref, dgpre_sum)
        _accumulate(dgpost_ref, dgpost_sum)

    return _hosted_call(
        body, riders, name=name, steps=s // tm,
        in_specs=[_rows(tm, d), _rows(tm, d), _rows(tm, d), _rows(tm, ff), _rows(tm, ff), _const((1, d)), _const((1, d)),
                  _resident((ff, d)), _resident((ff, d)), _resident((ff, d))],
        out_specs=[_rows(tm, d), _rows(tm, ff), _rows(tm, ff), _rows(tm, ff), _rows(tm, d), _rows(tm, d),
                   _const((1, d)), _const((1, d))],
        out_shape=[jax.ShapeDtypeStruct((s, d), F32), jax.ShapeDtypeStruct((s, ff), BF16), jax.ShapeDtypeStruct((s, ff), BF16),
                   jax.ShapeDtypeStruct((s, ff), BF16), jax.ShapeDtypeStruct((s, d), BF16), jax.ShapeDtypeStruct((s, d), BF16),
                   jax.ShapeDtypeStruct((1, d), F32), jax.ShapeDtypeStruct((1, d), F32)],
        args=[dxo, x, f, a, b, g_pre, g_post, wg_t, wu_t, wd])


def _wgrad(lhs, rhs, name, riders=None, rt=256):
    s, r = lhs.shape
    c = rhs.shape[1]

    def body(l_ref, r_ref, o_ref):
        o_ref[...] = _dot_tn(l_ref[...], r_ref[...])

    (out,), riding = _hosted_call(
        body, riders, name=name, steps=pl.cdiv(r, rt), in_specs=[pl.BlockSpec((s, rt), lambda i: (0, i)), _resident((s, c))],
        out_specs=[pl.BlockSpec((rt, c), lambda i: (i, 0))], out_shape=[jax.ShapeDtypeStruct((r, c), F32)], args=[lhs, rhs])
    return out, riding


def _in_fwd(x, g, w_in_t, name, tm=512):
    s, d = x.shape
    d_in = w_in_t.shape[0]
    n_parts = (d_in - POOL_DIM) // GROUP_DIM

    def body(x_ref, g_ref, w_ref, u_ref, *part_refs):
        xv = x_ref[...]
        hb = (xv * _inv_rms(xv) * g_ref[...]).astype(BF16)
        z = _dot_nt(hb, w_ref[...])
        u_ref[...] = z[:, :POOL_DIM]
        for j, ref in enumerate(part_refs):
            ref[...] = z[:, POOL_DIM + GROUP_DIM * j:POOL_DIM + GROUP_DIM * (j + 1)]

    return pl.pallas_call(
        body, name=name, grid=(s // tm,), in_specs=[_rows(tm, d), _const((1, d)), _resident((d_in, d))],
        out_specs=[_rows(tm, POOL_DIM)] + [_rows(tm, GROUP_DIM)] * n_parts,
        out_shape=[jax.ShapeDtypeStruct((s, POOL_DIM), F32)] + [jax.ShapeDtypeStruct((s, GROUP_DIM), F32)] * n_parts,
        compiler_params=_params(dimension_semantics=("arbitrary",)))(x, g, w_in_t)


def _in_bwd(du, dparts, x, dxo, g, w_in_t, name, riders=None, tm=512):
    s, d = x.shape
    d_in = w_in_t.shape[0]
    n_parts = len(dparts)

    def body(du_ref, *refs):
        part_refs = refs[:n_parts]
        x_ref, dxo_ref, g_ref, w_ref, dx_ref, dz_ref, h_ref, dg_ref = refs[n_parts:]
        dz = jnp.concatenate([r[...].astype(BF16) for r in (du_ref,) + part_refs], axis=1)
        dz_ref[...] = dz
        dh = _dot(dz, w_ref[...])
        xv = x_ref[...]
        inv = _inv_rms(xv)
        dxn, dg = _rms_bwd(xv, inv, g_ref[...], dh)
        dx_ref[...] = dxo_ref[...] + dxn
        h_ref[...] = (xv * inv * g_ref[...]).astype(BF16)
        _accumulate(dg_ref, dg)

    return _hosted_call(
        body, riders, name=name, steps=s // tm,
        in_specs=[_rows(tm, POOL_DIM)] + [_rows(tm, GROUP_DIM)] * n_parts + [_rows(tm, d), _rows(tm, d), _const((1, d)),
                                                                             _resident((d_in, d))],
        out_specs=[_rows(tm, d), _rows(tm, d_in), _rows(tm, d), _const((1, d))],
        out_shape=[jax.ShapeDtypeStruct((s, d), F32), jax.ShapeDtypeStruct((s, d_in), BF16), jax.ShapeDtypeStruct((s, d), BF16),
                   jax.ShapeDtypeStruct((1, d), F32)],
        args=[du, *dparts, x, dxo, g, w_in_t])


_POOL_HALO = 8


def _pool_chain(v, first_shift):
    n = v.shape[0]
    p2 = v + pltpu.roll(v, first_shift, 0)
    p4 = pltpu.roll(p2, 1, 0) + pltpu.roll(p2, n - 1, 0)
    p8 = pltpu.roll(p4, 2, 0) + pltpu.roll(p4, n - 2, 0)
    p16 = pltpu.roll(p8, 4, 0) + pltpu.roll(p8, n - 4, 0)
    group = lax.broadcasted_iota(jnp.int32, v.shape, 1) // HEAD_DIM
    return jnp.where(group == 0, p2, jnp.where(group == 1, p4, jnp.where(group == 2, p8, p16)))


def _pool_count(t0, rows, s):
    t = t0 + lax.broadcasted_iota(jnp.int32, (rows, POOL_DIM), 0)
    group = lax.broadcasted_iota(jnp.int32, (rows, POOL_DIM), 1) // HEAD_DIM
    half = jnp.where(group == 0, 1, jnp.where(group == 1, 2, jnp.where(group == 2, 4, 8)))
    cnt = jnp.minimum(t + half, s) - jnp.maximum(t - half, 0)
    return jnp.maximum(cnt, 1).astype(F32)


def _pad_rows(ref, pad_ref, s):
    zeros = jnp.zeros((_POOL_HALO, pad_ref.shape[1]), pad_ref.dtype)
    pad_ref[pl.ds(0, _POOL_HALO), :] = zeros
    pad_ref[pl.ds(_POOL_HALO + s, _POOL_HALO), :] = zeros
    pad_ref[pl.ds(_POOL_HALO, s), :] = ref[...]


def _pool_fwd(u, w_bd, scale, name, tm=512):
    s = u.shape[0]
    ext = tm + 2 * _POOL_HALO

    def body(u_ref, w_ref, sc_ref, o_ref, upad):
        _pad_rows(u_ref, upad, s)

        def tile(i, carry):
            t0 = pl.multiple_of(i * tm, tm)
            uv = upad[pl.ds(t0, ext), :]
            win = _pool_chain(uv, 1)[_POOL_HALO:_POOL_HALO + tm]
            y = win / _pool_count(t0, tm, s) - uv[_POOL_HALO:_POOL_HALO + tm]
            o_ref[pl.ds(t0, tm), :] = (_dot(y.astype(BF16), w_ref[...]) * sc_ref[...]).astype(BF16)
            return carry

        lax.fori_loop(0, s // tm, tile, 0)

    return pl.pallas_call(body, name=name, out_shape=jax.ShapeDtypeStruct((s, POOL_DIM), BF16),
                          scratch_shapes=[pltpu.VMEM((s + 2 * _POOL_HALO, POOL_DIM), F32)],
                          compiler_params=_params())(u, w_bd, scale)


def _pool_bwd(u, da, w_bd, scale, name, tm=512):
    s = u.shape[0]
    ext = tm + 2 * _POOL_HALO

    def body(u_ref, da_ref, w_ref, sc_ref, du_ref, dw_ref, dsc_ref, upad, dapad):
        _pad_rows(u_ref, upad, s)
        _pad_rows(da_ref, dapad, s)
        dw_ref[...] = jnp.zeros_like(dw_ref)
        dsc_ref[...] = jnp.zeros_like(dsc_ref)

        def tile(i, carry):
            t0 = pl.multiple_of(i * tm, tm)
            uv = upad[pl.ds(t0, ext), :]
            dav = dapad[pl.ds(t0, ext), :]
            win = _pool_chain(uv, 1)[_POOL_HALO:_POOL_HALO + tm]
            yb = (win / _pool_count(t0, tm, s) - uv[_POOL_HALO:_POOL_HALO + tm]).astype(BF16)
            yl = _dot(yb, w_ref[...])
            da_c = dav[_POOL_HALO:_POOL_HALO + tm]
            dsc_ref[...] += jnp.sum(da_c * yl, axis=0, keepdims=True)
            dyl = (dav * sc_ref[...]).astype(BF16)
            dw_ref[...] += _dot_tn(yb, dyl[_POOL_HALO:_POOL_HALO + tm])
            dy = _dot_nt(dyl, w_ref[...])
            dyc = dy / _pool_count(t0 - _POOL_HALO, ext, s)
            du_ref[pl.ds(t0, tm), :] = (_pool_chain(dyc, ext - 1) - dy)[_POOL_HALO:_POOL_HALO + tm]
            return carry

        lax.fori_loop(0, s // tm, tile, 0)

    pool_cols = pl.BlockSpec((s, POOL_DIM), lambda i: (0, 0), pipeline_mode=pl.Buffered(1))
    return pl.pallas_call(
        body, name=name, grid=(1,),
        in_specs=[pool_cols, pool_cols, _const((POOL_DIM, POOL_DIM)), _const((1, POOL_DIM))],
        out_specs=[_const((s, POOL_DIM)), _const((POOL_DIM, POOL_DIM)), _const((1, POOL_DIM))],
        out_shape=[jax.ShapeDtypeStruct((s, POOL_DIM), F32), jax.ShapeDtypeStruct((POOL_DIM, POOL_DIM), F32),
                   jax.ShapeDtypeStruct((1, POOL_DIM), F32)],
        scratch_shapes=[pltpu.VMEM((s + 2 * _POOL_HALO, POOL_DIM), F32), pltpu.VMEM((s + 2 * _POOL_HALO, POOL_DIM), F32)],
        compiler_params=_params(dimension_semantics=("arbitrary",)))(u, da, w_bd, scale)


_BQ = 128
_KW = _BQ + 2 * N_SIDE
_PAIR = 2 * HEAD_DIM
_NEG = -1e30
_ATTN_UNROLL = 8
_SCORE_SCALE = HEAD_DIM ** -0.5


def _stack_heads(x):
    lane_head = lax.broadcasted_iota(jnp.int32, x.shape, 1) // HEAD_DIM
    zero = jnp.zeros_like(x)
    return jnp.concatenate([jnp.where(lane_head == 0, x, zero), jnp.where(lane_head == 1, x, zero)], axis=0)


def _unstack_heads(x):
    lane_head = lax.broadcasted_iota(jnp.int32, (_BQ, _PAIR), 1) // HEAD_DIM
    return jnp.where(lane_head == 0, x[:_BQ], x[_BQ:])


def _stack_cols(x):
    return jnp.concatenate([x[:, 0:1], x[:, HEAD_DIM:HEAD_DIM + 1]], axis=0)


def _fill_bias(bias_ref, slopes_ref, dilation):
    row = lax.broadcasted_iota(jnp.int32, (2 * _BQ, _KW), 0)
    col = lax.broadcasted_iota(jnp.int32, (2 * _BQ, _KW), 1)
    pair = 2 * pl.program_id(0)
    slope = jnp.where(row < _BQ, slopes_ref[pair], slopes_ref[pair + 1]) * float(dilation)

    @pl.when(pl.program_id(1) == 0)
    def _():
        for j in range(3):
            dist = jnp.abs(col - (row & (_BQ - 1)) - j * N_SIDE)
            bias_ref[j] = jnp.where(dist <= N_SIDE, -slope * dist.astype(F32), _NEG)


def _block_window(i, n_blocks, length):
    q0 = pl.multiple_of(i * _BQ, _BQ)
    ws = pl.multiple_of(jnp.clip(q0 - N_SIDE, 0, length - _KW), N_SIDE)
    return q0, ws, jnp.where(i == 0, 0, jnp.where(i == n_blocks - 1, 2, 1))


_FREE_STRIDE = 4


def _residue_views(dilation, seq, ins, outs, tmps):
    step = pl.program_id(1)
    if dilation <= _FREE_STRIDE:
        def rows(start, count):
            return pl.ds(start, count) if dilation == 1 else pl.ds(start * dilation + step, count, stride=dilation)

        return ins, outs, rows, lambda: None
    inner = dilation // _FREE_STRIDE
    assert inner <= _FREE_STRIDE and len(tmps) == len(ins) + len(outs)
    first, second = step // inner, step % inner
    coarse = pl.ds(first, seq // _FREE_STRIDE, stride=_FREE_STRIDE)
    in_tmps, out_tmps = tmps[:len(ins)], tmps[len(ins):]

    @pl.when(second == 0)
    def _():
        for ref, tmp in zip(ins, in_tmps):
            tmp[...] = ref[coarse, :]

    def flush():
        @pl.when(second == inner - 1)
        def _():
            for ref, tmp in zip(outs, out_tmps):
                ref[coarse, :] = tmp[...]

    return in_tmps, out_tmps, lambda start, count: pl.ds(start * inner + second, count, stride=inner), flush


def _attn_call(body, name, dilation, seq, n_in, n_out, scratch, buffers):
    col = pl.BlockSpec((seq, _PAIR), lambda c, r: (0, c), pipeline_mode=pl.Buffered(buffers))
    tmps = [pltpu.VMEM((seq // _FREE_STRIDE, _PAIR), F32)] * (n_in + n_out if dilation > _FREE_STRIDE else 0)
    return pl.pallas_call(
        body, name=name, grid=(GROUP_DIM // _PAIR, dilation),
        in_specs=[pl.BlockSpec(memory_space=pltpu.SMEM)] + [col] * n_in, out_specs=[col] * n_out,
        out_shape=[jax.ShapeDtypeStruct((seq, GROUP_DIM), F32)] * n_out, scratch_shapes=scratch + tmps,
        compiler_params=_params(dimension_semantics=("arbitrary", "arbitrary")))


def _attn_fwd(q, k, v, slopes, dilation, name):
    seq = q.shape[0]
    length = seq // dilation
    n_blocks = length // _BQ

    def body(sl_ref, q_ref, k_ref, v_ref, o_ref, lse_ref, qs, ks, vs, bias_ref, *tmps):
        (q_in, k_in, v_in), (o_out, lse_out), rows, flush = _residue_views(dilation, seq, (q_ref, k_ref, v_ref), (o_ref, lse_ref), tmps)
        all_rows = rows(0, length)
        qs[...] = (q_in[all_rows, :] * _SCORE_SCALE).astype(BF16)
        ks[...] = k_in[all_rows, :].astype(BF16)
        vs[...] = v_in[all_rows, :].astype(BF16)
        _fill_bias(bias_ref, sl_ref, dilation)

        def block(i, carry):
            q0, ws, which = _block_window(i, n_blocks, length)
            kw = ks[pl.ds(ws, _KW), :]
            vw = vs[pl.ds(ws, _KW), :]
            sc = _dot_nt(_stack_heads(qs[pl.ds(q0, _BQ), :]), kw) + bias_ref[which]
            m = jnp.max(sc, axis=-1, keepdims=True)
            p = jnp.exp(sc - m)
            den = jnp.sum(p, axis=-1, keepdims=True)
            o_out[rows(q0, _BQ), :] = _unstack_heads(_dot(p.astype(BF16), vw) / den)
            lse_out[rows(q0, _BQ), :] = _unstack_heads(jnp.broadcast_to(m + jnp.log(den), (2 * _BQ, _PAIR)))
            return carry

        lax.fori_loop(0, n_blocks, block, 0, unroll=min(_ATTN_UNROLL, n_blocks))
        flush()

    stage = pltpu.VMEM((length, _PAIR), BF16)
    bias = pltpu.VMEM((3, 2 * _BQ, _KW), F32)
    return _attn_call(body, name, dilation, seq, 3, 2, [stage] * 3 + [bias], 2)(slopes, q, k, v)


def _attn_bwd(q, k, v, do, lse, cterm, slopes, dilation, name):
    seq = q.shape[0]
    length = seq // dilation
    n_blocks = length // _BQ

    def body(sl_ref, q_ref, k_ref, v_ref, do_ref, lse_ref, c_ref, dq_ref, dk_ref, dv_ref, qs, ks, vs, dos, dk_acc, dv_acc, bias_ref,
             *tmps):
        (q_in, k_in, v_in, do_in, lse_in, c_in), (dq_out, dk_out, dv_out), rows, flush = _residue_views(
            dilation, seq, (q_ref, k_ref, v_ref, do_ref, lse_ref, c_ref), (dq_ref, dk_ref, dv_ref), tmps)
        all_rows = rows(0, length)
        qs[...] = (q_in[all_rows, :] * _SCORE_SCALE).astype(BF16)
        for src, dst in ((k_in, ks), (v_in, vs), (do_in, dos)):
            dst[...] = src[all_rows, :].astype(BF16)
        dk_acc[...] = jnp.zeros_like(dk_acc)
        dv_acc[...] = jnp.zeros_like(dv_acc)
        _fill_bias(bias_ref, sl_ref, dilation)

        def block(i, carry):
            q0, ws, which = _block_window(i, n_blocks, length)
            qm = _stack_heads(qs[pl.ds(q0, _BQ), :])
            dom = _stack_heads(dos[pl.ds(q0, _BQ), :])
            kw = ks[pl.ds(ws, _KW), :]
            vw = vs[pl.ds(ws, _KW), :]
            p = jnp.exp(_dot_nt(qm, kw) + bias_ref[which] - _stack_cols(lse_in[rows(q0, _BQ), :]))
            ds = (p * (_dot_nt(dom, vw) + _stack_cols(c_in[rows(q0, _BQ), :]))).astype(BF16)
            dq_out[rows(q0, _BQ), :] = _unstack_heads(_dot(ds, kw)) * _SCORE_SCALE
            dk_acc[pl.ds(ws, _KW), :] += _dot_tn(ds, qm)
            dv_acc[pl.ds(ws, _KW), :] += _dot_tn(p.astype(BF16), dom)
            return carry

        lax.fori_loop(0, n_blocks, block, 0, unroll=min(_ATTN_UNROLL, n_blocks))
        dk_out[all_rows, :] = dk_acc[...]
        dv_out[all_rows, :] = dv_acc[...]
        flush()

    stage = pltpu.VMEM((length, _PAIR), BF16)
    acc = pltpu.VMEM((length, _PAIR), F32)
    bias = pltpu.VMEM((3, 2 * _BQ, _KW), F32)
    return _attn_call(body, name, dilation, seq, 6, 3, [stage] * 4 + [acc] * 2 + [bias], 1)(slopes, q, k, v, do, lse, cterm)


def _group_weights(lses):
    m = jnp.maximum(jnp.maximum(lses[0], lses[1]), lses[2])
    es = [jnp.exp(l - m) for l in lses]
    den = es[0] + es[1] + es[2]
    return [e / den for e in es]


def _out_fwd(a_pool, outs, lses, x, w_out, g, name, tm=512):
    s, d = x.shape
    width = POOL_DIM + 3 * GROUP_DIM

    def body(ap_ref, o0, o1, o2, l0, l1, l2, x_ref, w_ref, g_ref, xo_ref, mix_ref, cat_ref):
        alphas = _group_weights([l0[...], l1[...], l2[...]])
        cat = jnp.concatenate([ap_ref[...]] + [(o[...] * al).astype(BF16) for o, al in zip((o0, o1, o2), alphas)], axis=1)
        cat_ref[...] = cat
        mix = _dot(cat, w_ref[...])
        mix_ref[...] = mix
        xo_ref[...] = x_ref[...] + mix * _inv_rms(mix) * g_ref[...]

    return pl.pallas_call(
        body, name=name, grid=(s // tm,),
        in_specs=[_rows(tm, POOL_DIM)] + [_rows(tm, GROUP_DIM)] * 6 + [_rows(tm, d), _resident(w_out.shape), _const((1, d))],
        out_specs=[_rows(tm, d), _rows(tm, d), _rows(tm, width)],
        out_shape=[jax.ShapeDtypeStruct((s, d), F32), jax.ShapeDtypeStruct((s, d), F32), jax.ShapeDtypeStruct((s, width), BF16)],
        compiler_params=_params(dimension_semantics=("arbitrary",)))(a_pool, *outs, *lses, x, w_out, g)


def _out_bwd(dxo, mix, outs, lses, w_out, g, head_ones, name, tm=512):
    s, d = mix.shape

    def body(dxo_ref, mix_ref, o0, o1, o2, l0, l1, l2, w_ref, g_ref, ones_ref, dpool_ref, dmix_ref, do0, do1, do2, c0, c1, c2, dg_ref):
        mv = mix_ref[...]
        dmix, dg = _rms_bwd(mv, _inv_rms(mv), g_ref[...], dxo_ref[...])
        dmb = dmix.astype(BF16)
        dmix_ref[...] = dmb
        _accumulate(dg_ref, dg)
        dcat = _dot_nt(dmb, w_ref[...])
        dpool_ref[...] = dcat[:, :POOL_DIM]
        alphas = _group_weights([l0[...], l1[...], l2[...]])
        das = [dcat[:, POOL_DIM + GROUP_DIM * j:POOL_DIM + GROUP_DIM * (j + 1)] for j in range(3)]
        prod = sum(da * (o[...] * al) for da, o, al in zip(das, (o0, o1, o2), alphas))
        hi = prod.astype(BF16)
        lo = (prod - hi.astype(F32)).astype(BF16)
        total = _dot(hi, ones_ref[...]) + _dot(lo, ones_ref[...])
        for da, al, do_ref, c_ref in zip(das, alphas, (do0, do1, do2), (c0, c1, c2)):
            do_ref[...] = da * al
            c_ref[...] = -al * total

    return pl.pallas_call(
        body, name=name, grid=(s // tm,),
        in_specs=[_rows(tm, d), _rows(tm, d)] + [_rows(tm, GROUP_DIM)] * 6 + [_resident(w_out.shape), _const((1, d)),
                                                                             _const((GROUP_DIM, GROUP_DIM))],
        out_specs=[_rows(tm, POOL_DIM), _rows(tm, d)] + [_rows(tm, GROUP_DIM)] * 6 + [_const((1, d))],
        out_shape=[jax.ShapeDtypeStruct((s, POOL_DIM), F32), jax.ShapeDtypeStruct((s, d), BF16)]
        + [jax.ShapeDtypeStruct((s, GROUP_DIM), F32)] * 6 + [jax.ShapeDtypeStruct((1, d), F32)],
        compiler_params=_params(dimension_semantics=("arbitrary",)))(dxo, mix, *outs, *lses, w_out, g, head_ones)


def _alibi_slopes():
    return np.array([2.0 ** (-8.0 * (i + 1) / N_ATTN_HEADS) for i in range(N_ATTN_HEADS)], np.float32)


def _block_diag(w_lin):
    n, c, _ = w_lin.shape
    eye = jnp.eye(n, dtype=w_lin.dtype)
    return (eye[:, None, :, None] * w_lin[:, :, None, :]).reshape(n * c, n * c)


class _NoExchange:
    def __init__(self, full):
        self.full, self.grads = full, {}

    def first_weights(self):
        return self.full

    def riders(self, host):
        return []

    def landed(self, host, results):
        pass

    def rest_weights(self):
        return self.full

    def gradient(self, name, grad):
        self.grads[name] = grad


def _local_step(x, target, small, exchange):
    s, d = x.shape
    slopes = _alibi_slopes()
    group_slopes = [jnp.asarray(slopes[4 * g:4 * g + 4]) for g in range(3)]
    w_bd = _block_diag(small["w_pool_lin"]).astype(BF16)
    head_ones = jnp.asarray(np.kron(np.eye(GROUP_DIM // HEAD_DIM), np.ones((HEAD_DIM, HEAD_DIM))), BF16)

    full = exchange.first_weights()
    (x1, a1, b1, f1), riding = _ffn_fwd(x, small["g_ffn1_pre"], full["w1_gate"], full["w1_up"], full["w1_down"],
                                        small["g_ffn1_post"], None, "ffn1_fwd", exchange.riders("ffn1_fwd"))
    exchange.landed("ffn1_fwd", riding)
    full = {**full, **exchange.rest_weights()}
    u, *parts = _in_fwd(x1, small["g_mix_pre"], full["w_in"], "in_fwd")
    qs, ks, vs = parts[0:3], parts[3:6], parts[6:9]
    a_pool = _pool_fwd(u, w_bd, small["pool_scale"], "pool_fwd")
    outs, lses = [], []
    for g, dil in enumerate(DILATIONS):
        o, lse = _attn_fwd(qs[g], ks[g], vs[g], group_slopes[g], dil, f"attn_fwd{g}")
        outs.append(o)
        lses.append(lse)
    x2, mix, cat = _out_fwd(a_pool, outs, lses, x1, full["w_out"], small["g_mix_post"], "out_fwd")
    (dx3, a2, b2, f2, loss_part), _ = _ffn_fwd(x2, small["g_ffn2_pre"], full["w2_gate"], full["w2_up"], full["w2_down"],
                                               small["g_ffn2_post"], target, "ffn2_fwd")

    small_grads = {}

    def hosted(call, host, *args):
        results, riding = call(*args, host, exchange.riders(host))
        exchange.landed(host, riding)
        return results

    def ffn_backward(tag, dxo, x_in, f, a, b):
        n = tag[-1]
        dx, hh, da, db, df, h, dg_pre, dg_post = hosted(
            _ffn_bwd, f"{tag}_bwd", dxo, x_in, f, a, b, small[f"g_{tag}_pre"], small[f"g_{tag}_post"],
            full[f"w{n}_gate"], full[f"w{n}_up"], full[f"w{n}_down"])
        for part, lhs, rhs in (("down", hh, df), ("gate", da, h), ("up", db, h)):
            exchange.gradient(f"w{n}_{part}", hosted(_wgrad, f"{tag}_wgrad_{part}", lhs, rhs))
        small_grads[f"g_{tag}_pre"], small_grads[f"g_{tag}_post"] = dg_pre, dg_post
        return dx

    dx2 = ffn_backward("ffn2", dx3, x2, f2, a2, b2)
    dpool, dmix, *dos_cs, small_grads["g_mix_post"] = _out_bwd(dx2, mix, outs, lses, full["w_out"], small["g_mix_post"],
                                                               head_ones, "out_bwd")
    dos, cs = dos_cs[:3], dos_cs[3:]
    dqs, dks, dvs = [], [], []
    for g, dil in enumerate(DILATIONS):
        dq, dk, dv = _attn_bwd(qs[g], ks[g], vs[g], dos[g], lses[g], cs[g], group_slopes[g], dil, f"attn_bwd{g}")
        dqs.append(dq)
        dks.append(dk)
        dvs.append(dv)
    du, dw_bd, small_grads["pool_scale"] = _pool_bwd(u, dpool, w_bd, small["pool_scale"], "pool_bwd")
    n_pool = len(POOL_HALF_WINDOWS)
    small_grads["w_pool_lin"] = jnp.stack(
        [dw_bd[HEAD_DIM * g:HEAD_DIM * (g + 1), HEAD_DIM * g:HEAD_DIM * (g + 1)] for g in range(n_pool)])
    dx1, dz, h2, small_grads["g_mix_pre"] = hosted(_in_bwd, "in_bwd", du, dqs + dks + dvs, x1, dx2, small["g_mix_pre"], full["w_in"])
    exchange.gradient("w_in", hosted(_wgrad, "wgrad_in", dz, h2))
    dx0 = ffn_backward("ffn1", dx1, x, f1, a1, b1)
    exchange.gradient("w_out", hosted(_wgrad, "wgrad_out", cat, dmix))
    return loss_part[0, 0], dx0, small_grads


SEGMENTS = ("w1_gate", "w1_up", "w1_down", "w_in", "w_out", "w2_gate", "w2_up", "w2_down")
TRANSPOSED = ("w1_gate", "w1_up", "w_in", "w2_gate", "w2_up")
ROWS_OUTSIDE = ("w1_gate", "w1_up", "w2_gate", "w2_up")
HALF = 512


def _place():
    x, y, c = lax.axis_index("x"), lax.axis_index("y"), lax.axis_index("c")
    other_chips = [(1 - x, y), (x, 1 - y), (1 - x, 1 - y)]
    return x, y, c, other_chips


def _chip_rows(chip, rows):
    return pl.ds(pl.multiple_of((2 * chip[0] + chip[1]) * rows, 16), rows)


def _cols(c):
    return pl.ds(pl.multiple_of(c * HALF, HALF), HALF)


def _cast_shard(w, transpose, place, name, tm=256):
    r, c = w.shape
    if transpose:
        def body(place_ref, w_ref, o_ref):
            o_ref[...] = w_ref[...].T.astype(BF16)

        grid, in_block, out_block, rows = (r // tm,), (tm, c), (c, tm), c
    else:
        def body(place_ref, w_ref, o_ref):
            o_ref[...] = w_ref[...].astype(BF16)

        grid, in_block, out_block, rows = (1,), (r, c), (r, c), r
    return pl.pallas_call(
        body, name=name,
        grid_spec=pltpu.PrefetchScalarGridSpec(
            num_scalar_prefetch=1, grid=grid, in_specs=[pl.BlockSpec(in_block, lambda i, place: (i, 0))],
            out_specs=pl.BlockSpec(out_block, lambda i, place: (place[0], i))),
        out_shape=jax.ShapeDtypeStruct((N_CHIPS * rows, 1024), BF16),
        compiler_params=_params(dimension_semantics=("arbitrary",)))(place, w)


def _gather_weights(bufs):
    n = len(bufs)
    rows = [b.shape[0] // N_CHIPS for b in bufs]

    def body(*refs):
        outs = refs[n:2 * n]
        send_sems, recv_sems, fwd_send_sems, fwd_recv_sems = refs[2 * n:]
        x, y, c, chips = _place()
        me = (x, y)

        def ici(j, k, src_chip, to):
            blk = outs[k].at[_chip_rows(src_chip, rows[k]), _cols(c)]
            return pltpu.make_async_remote_copy(src_ref=blk, dst_ref=blk, send_sem=send_sems.at[j, k], recv_sem=recv_sems.at[j, k],
                                                device_id=to, device_id_type=MESH)

        def d2d(j, k, src_chip, half):
            blk = outs[k].at[_chip_rows(src_chip, rows[k]), _cols(half)]
            return pltpu.make_async_remote_copy(src_ref=blk, dst_ref=blk, send_sem=fwd_send_sems.at[j, k],
                                                recv_sem=fwd_recv_sems.at[j, k], device_id=(x, y, 1 - c), device_id_type=MESH)

        sends = [ici(j, k, me, (*chip, c)) for j, chip in enumerate(chips) for k in range(n)]
        for cp in sends:
            cp.start()
        forwards = []
        for j, chip in enumerate(chips):
            for k in range(n):
                ici(j, k, chip, (x, y, c)).wait_recv()
                fw = d2d(j, k, chip, c)
                fw.start()
                forwards.append(fw)
        for j, chip in enumerate(chips):
            for k in range(n):
                d2d(j, k, chip, 1 - c).wait_recv()
        for cp in sends + forwards:
            cp.wait_send()

    any_spec = pl.BlockSpec(memory_space=pl.ANY)
    return pl.pallas_call(
        body, name="gather_weights", in_specs=[any_spec] * n, out_specs=[any_spec] * n,
        out_shape=[jax.ShapeDtypeStruct(b.shape, b.dtype) for b in bufs], input_output_aliases={k: k for k in range(n)},
        scratch_shapes=[pltpu.SemaphoreType.DMA((3, n)), pltpu.SemaphoreType.DMA((3, n)),
                        pltpu.SemaphoreType.DMA((3, n)), pltpu.SemaphoreType.DMA((3, n))])(*bufs)


def _gather_rider(bufs):
    n = len(bufs)
    rows = [b.shape[0] // N_CHIPS for b in bufs]

    def copies(outs, send_sems, recv_sems, inbound):
        x, y, c, chips = _place()
        for j, chip in enumerate(chips):
            for k in range(n):
                src_chip = chip if inbound else (x, y)
                blk = outs[k].at[_chip_rows(src_chip, rows[k]), _cols(c)]
                yield pltpu.make_async_remote_copy(src_ref=blk, dst_ref=blk, send_sem=send_sems.at[j, k], recv_sem=recv_sems.at[j, k],
                                                   device_id=(*chip, c), device_id_type=MESH)

    def start(ins, outs, send_sems, recv_sems):
        for cp in copies(outs, send_sems, recv_sems, False):
            cp.start()

    def wait(ins, outs, send_sems, recv_sems):
        for cp in copies(outs, send_sems, recv_sems, True):
            cp.wait_recv()
        for cp in copies(outs, send_sems, recv_sems, False):
            cp.wait_send()

    return _Rider(list(bufs), None, (3, n), start, wait)


def _forward_halves(bufs, name):
    n = len(bufs)
    rows = [b.shape[0] // N_CHIPS for b in bufs]

    def body(*refs):
        outs = refs[n:2 * n]
        send_sems, recv_sems = refs[2 * n:]
        x, y, c, chips = _place()

        def d2d(j, k, chip, half):
            blk = outs[k].at[_chip_rows(chip, rows[k]), _cols(half)]
            return pltpu.make_async_remote_copy(src_ref=blk, dst_ref=blk, send_sem=send_sems.at[j, k], recv_sem=recv_sems.at[j, k],
                                                device_id=(x, y, 1 - c), device_id_type=MESH)

        forwards = [d2d(j, k, chip, c) for j, chip in enumerate(chips) for k in range(n)]
        for cp in forwards:
            cp.start()
        for j, chip in enumerate(chips):
            for k in range(n):
                d2d(j, k, chip, 1 - c).wait_recv()
        for cp in forwards:
            cp.wait_send()

    any_spec = pl.BlockSpec(memory_space=pl.ANY)
    return pl.pallas_call(
        body, name=name, in_specs=[any_spec] * n, out_specs=[any_spec] * n,
        out_shape=[jax.ShapeDtypeStruct(b.shape, b.dtype) for b in bufs], input_output_aliases={k: k for k in range(n)},
        scratch_shapes=[pltpu.SemaphoreType.DMA((3, n)), pltpu.SemaphoreType.DMA((3, n))])(*bufs)


def _sibling_rider(grads):
    n = len(grads)

    def copies(ins, outs, send_sems, recv_sems):
        x, y, c, _ = _place()
        return [pltpu.make_async_remote_copy(src_ref=ins[k].at[:, pl.ds(1 - c, 1)], dst_ref=outs[k], send_sem=send_sems.at[k],
                                             recv_sem=recv_sems.at[k], device_id=(x, y, 1 - c), device_id_type=MESH)
                for k in range(n)]

    def start(*refs):
        for cp in copies(*refs):
            cp.start()

    def wait(*refs):
        for cp in copies(*refs):
            cp.wait()

    return _Rider(list(grads), [jax.ShapeDtypeStruct((N_CHIPS, 1) + g.shape[2:], F32) for g in grads], (n,), start, wait)


def _alone(rider, name):
    n, n_out = len(rider.operands), len(rider.landing)

    def body(*refs):
        rider.start(refs[:n], refs[n:n + n_out], *refs[n + n_out:])
        rider.wait(refs[:n], refs[n:n + n_out], *refs[n + n_out:])

    any_spec = pl.BlockSpec(memory_space=pl.ANY)
    return pl.pallas_call(body, name=name, in_specs=[any_spec] * n, out_specs=[any_spec] * n_out, out_shape=rider.landing,
                          scratch_shapes=[pltpu.SemaphoreType.DMA(rider.sems)] * 2)(*rider.operands)


def _chip_sum(grad, from_sibling, place, name):
    rh, width = grad.shape[2:]

    def body(place_ref, g_ref, s_ref, own_ref, all_ref):
        total = g_ref[0, 0] + s_ref[0, 0]
        all_ref[0, 0] = total.astype(BF16)

        @pl.when(pl.program_id(0) == place_ref[0])
        def _():
            own_ref[0] = total

    blk = (1, 1, rh, width)
    return pl.pallas_call(
        body, name=name,
        grid_spec=pltpu.PrefetchScalarGridSpec(
            num_scalar_prefetch=1, grid=(N_CHIPS,),
            in_specs=[pl.BlockSpec(blk, lambda p, place: (p, place[1], 0, 0)), pl.BlockSpec(blk, lambda p, place: (p, 0, 0, 0))],
            out_specs=[pl.BlockSpec((1, rh, width), lambda p, place: (0, 0, 0)), pl.BlockSpec(blk, lambda p, place: (p, 0, 0, 0))]),
        out_shape=[jax.ShapeDtypeStruct((1, rh, width), F32), jax.ShapeDtypeStruct((N_CHIPS, 1, rh, width), BF16)],
        compiler_params=_params(dimension_semantics=("arbitrary",)))(place, grad, from_sibling)


def _scatter_rider(sums):
    n = len(sums)

    def copies(ins, outs, send_sems, recv_sems):
        x, y, c, chips = _place()
        return [pltpu.make_async_remote_copy(src_ref=ins[k].at[pl.ds(2 * chip[0] + chip[1], 1)], dst_ref=outs[k].at[pl.ds(j, 1)],
                                             send_sem=send_sems.at[j, k], recv_sem=recv_sems.at[j, k],
                                             device_id=(*chip, c), device_id_type=MESH)
                for j, chip in enumerate(chips) for k in range(n)]

    def start(*refs):
        for cp in copies(*refs):
            cp.start()

    def wait(*refs):
        for cp in copies(*refs):
            cp.wait()

    return _Rider(list(sums), [jax.ShapeDtypeStruct((3,) + sm.shape[1:], BF16) for sm in sums], (3, n), start, wait)


def _total_sums(owns, received, name):
    n = len(owns)

    def body(*refs):
        for o_ref, r_ref, t_ref in zip(refs[:n], refs[n:2 * n], refs[2 * n:]):
            total = o_ref[0]
            for j in range(3):
                total = total + r_ref[j, 0].astype(F32)
            t_ref[0] = total

    return pl.pallas_call(body, name=name, out_shape=[jax.ShapeDtypeStruct(o.shape, F32) for o in owns],
                          compiler_params=_params())(*owns, *received)


def _swap_halves(halves):
    n = len(halves)

    def body(*refs):
        ins, outs = refs[:n], refs[n:2 * n]
        send_sems, recv_sems = refs[2 * n:]
        x, y, c, _ = _place()
        copies = [pltpu.make_async_remote_copy(src_ref=ins[k], dst_ref=outs[k], send_sem=send_sems.at[k],
                                               recv_sem=recv_sems.at[k], device_id=(x, y, 1 - c), device_id_type=MESH)
                  for k in range(n)]
        for cp in copies:
            cp.start()
        for cp in copies:
            cp.wait()

    any_spec = pl.BlockSpec(memory_space=pl.ANY)
    return pl.pallas_call(
        body, name="swap_halves", in_specs=[any_spec] * n, out_specs=[any_spec] * n,
        out_shape=[jax.ShapeDtypeStruct(h.shape, F32) for h in halves],
        scratch_shapes=[pltpu.SemaphoreType.DMA((n,)), pltpu.SemaphoreType.DMA((n,))])(*halves)


N_DEV = 8


def _gather_small(block):
    m_per, width = block.shape

    def body(x_ref, out_ref, send_sems, recv_sems, local_sem):
        x, y, c, chips = _place()
        me, sibling = (x, y, c), (x, y, 1 - c)

        def rows(px, py, pc):
            return out_ref.at[pl.ds((4 * px + 2 * py + pc) * m_per, m_per), :]

        def copy(k, blk, to, src=None):
            return pltpu.make_async_remote_copy(src_ref=rows(*blk) if src is None else src, dst_ref=rows(*blk),
                                                send_sem=send_sems.at[k], recv_sem=recv_sems.at[k], device_id=to, device_id_type=MESH)

        mine = pltpu.make_async_copy(x_ref, rows(*me), local_sem)
        mine.start()
        first = [copy(0, me, sibling, src=x_ref)] + [copy(1 + j, me, (*chip, c), src=x_ref) for j, chip in enumerate(chips)]
        for cp in first:
            cp.start()
        passed = [copy(4 + j, (*chip, c), sibling) for j, chip in enumerate(chips)]
        for j, chip in enumerate(chips):
            copy(1 + j, (*chip, c), me).wait_recv()
            passed[j].start()
        copy(0, sibling, me).wait_recv()
        for j, chip in enumerate(chips):
            copy(4 + j, (*chip, 1 - c), me).wait_recv()
        for cp in first + passed:
            cp.wait_send()
        mine.wait()

    vmem = pl.BlockSpec(memory_space=pltpu.VMEM)
    return pl.pallas_call(body, name="gather_small", out_shape=jax.ShapeDtypeStruct((N_DEV * m_per, width), F32),
                          in_specs=[vmem], out_specs=vmem,
                          scratch_shapes=[pltpu.SemaphoreType.DMA((7,)), pltpu.SemaphoreType.DMA((7,)),
                                          pltpu.SemaphoreType.DMA])(block)


def _adamw_math(w, g, m, v):
    m = ADAM_B1 * m + (1.0 - ADAM_B1) * g
    v = ADAM_B2 * v + (1.0 - ADAM_B2) * (g * g)
    m_hat = m / (1.0 - ADAM_B1 ** ADAM_STEP)
    v_hat = v / (1.0 - ADAM_B2 ** ADAM_STEP)
    delta = -ADAM_LR * (m_hat / (jnp.sqrt(v_hat) + ADAM_EPS) + ADAM_WD * w)
    return delta, m, v


def _adamw(w, mine, siblings, place, m, v, transposed, name):
    if transposed:
        def body(place_ref, w_ref, mine_ref, sib_ref, m_ref, v_ref, go_ref, d_ref, mo_ref, vo_ref):
            first = place_ref[1] == 0
            g = jnp.concatenate([jnp.where(first, mine_ref[0], sib_ref[0]), jnp.where(first, sib_ref[0], mine_ref[0])], axis=0).T
            go_ref[...] = g
            d_ref[...], mo_ref[...], vo_ref[...] = _adamw_math(w_ref[...], g, m_ref[...], v_ref[...])

        vmem = pl.BlockSpec(memory_space=pltpu.VMEM)
        return pl.pallas_call(body, name=name, in_specs=[pl.BlockSpec(memory_space=pltpu.SMEM)] + [vmem] * 5, out_specs=[vmem] * 4,
                              out_shape=[jax.ShapeDtypeStruct(w.shape, F32)] * 4, compiler_params=_params())(
                                  place, w, mine, siblings, m, v)

    rh, width = mine.shape[1:]

    def body(place_ref, w_ref, mine_ref, sib_ref, m_ref, v_ref, go_ref, d_ref, mo_ref, vo_ref):
        g = jnp.where(pl.program_id(0) == place_ref[1], mine_ref[0], sib_ref[0])
        go_ref[...] = g
        d_ref[...], mo_ref[...], vo_ref[...] = _adamw_math(w_ref[...], g, m_ref[...], v_ref[...])

    half = pl.BlockSpec((rh, width), lambda h, place: (h, 0))
    whole = pl.BlockSpec((1, rh, width), lambda h, place: (0, 0, 0))
    return pl.pallas_call(
        body, name=name,
        grid_spec=pltpu.PrefetchScalarGridSpec(num_scalar_prefetch=1, grid=(2,), in_specs=[half, whole, whole, half, half],
                                               out_specs=[half] * 4),
        out_shape=[jax.ShapeDtypeStruct(w.shape, F32)] * 4,
        compiler_params=_params(dimension_semantics=("arbitrary",)))(place, w, mine, siblings, m, v)


def _adamw_small(gathered, w, m, v, name):
    def body(ga_ref, w_ref, m_ref, v_ref, go_ref, d_ref, mo_ref, vo_ref):
        g = ga_ref[0]
        for dev in range(1, N_DEV):
            g = g + ga_ref[dev]
        go_ref[...] = g
        d_ref[...], mo_ref[...], vo_ref[...] = _adamw_math(w_ref[...], g, m_ref[...], v_ref[...])

    return pl.pallas_call(body, name=name, out_shape=[jax.ShapeDtypeStruct(w.shape, F32)] * 4,
                          compiler_params=_params())(gathered, w, m, v)


class _Exchange:
    FIRST = ("w1_gate", "w1_up", "w1_down")
    HOSTS = {"ffn2_wgrad_gate": (("w2_down",), ()), "ffn2_wgrad_up": (("w2_gate",), ("w2_down",)),
             "in_bwd": (("w2_up",), ("w2_gate",)), "wgrad_in": ((), ("w2_up",)),
             "ffn1_wgrad_down": ((), ("w_in",)), "ffn1_wgrad_gate": (("w1_down",), ()), "ffn1_wgrad_up": ((), ("w1_down", "w1_gate")),
             "wgrad_out": ((), ("w1_up",))}
    ALONE = ("w_in", "w1_gate", "w1_up", "w_out")

    def __init__(self, bufs, place):
        self.bufs, self.place = bufs, place
        self.later = [k for k in SEGMENTS if k not in self.FIRST]
        self.split, self.own, self.to_send, self.received = {}, {}, {}, {}

    def first_weights(self):
        return dict(zip(self.FIRST, _gather_weights([self.bufs[k] for k in self.FIRST])))

    def riders(self, host):
        if host == "ffn1_fwd":
            return [_gather_rider([self.bufs[k] for k in self.later])]
        halves, sums = self.HOSTS.get(host, ((), ()))
        return ([_sibling_rider([self.split[k] for k in halves])] if halves else []) + (
            [_scatter_rider([self.to_send[k] for k in sums])] if sums else [])

    def landed(self, host, results):
        if host == "ffn1_fwd":
            self.rest = dict(zip(self.later, _forward_halves(results[0], "gather_rest_forward")))
            return
        halves, sums = self.HOSTS.get(host, ((), ()))
        if halves:
            self._chip_sums(halves, results[0])
        if sums:
            self.received.update(zip(sums, results[-1]))

    def rest_weights(self):
        return self.rest

    def gradient(self, name, grad):
        self.split[name] = grad.reshape(N_CHIPS, 2, grad.shape[0] // (2 * N_CHIPS), grad.shape[1])
        if name in self.ALONE:
            self._chip_sums([name], _alone(_sibling_rider([self.split[name]]), f"reduce_sibling_{name}"))

    def _chip_sums(self, names, from_sibling):
        for k, fs in zip(names, from_sibling):
            self.own[k], self.to_send[k] = _chip_sum(self.split[k], fs, self.place, f"chip_sum_{k}")

    def summed_halves(self):
        late = [k for k in SEGMENTS if k not in self.received]
        self.received.update(zip(late, _alone(_scatter_rider([self.to_send[k] for k in late]), "reduce_chips_last")))
        return _total_sums([self.own[k] for k in SEGMENTS], [self.received[k] for k in SEGMENTS], "total_sums")


SMALL = ("g_ffn1_pre", "g_ffn1_post", "g_mix_pre", "w_pool_lin", "pool_scale", "g_mix_post", "g_ffn2_pre", "g_ffn2_post")
WEIGHTS = ("g_ffn1_pre", "w1_gate", "w1_up", "w1_down", "g_ffn1_post", "g_mix_pre", "w_in", "w_pool_lin", "pool_scale", "w_out",
           "g_mix_post", "g_ffn2_pre", "w2_gate", "w2_up", "w2_down", "g_ffn2_post")
LANES = 128


def _pack_small(tree):
    flat = jnp.concatenate([tree[k].reshape(-1) for k in SMALL])
    rows = -(-flat.shape[0] // (8 * LANES)) * 8
    return jnp.pad(flat, (0, rows * LANES - flat.shape[0])).reshape(rows, LANES)


def _unpack_small(packed, like):
    flat, out, at = packed.reshape(-1), {}, 0
    for k in SMALL:
        size = math.prod(like[k].shape)
        out[k] = flat[at:at + size].reshape(like[k].shape)
        at += size
    return out


def kernel(x, g_ffn1_pre, w1_gate, w1_up, w1_down, g_ffn1_post, g_mix_pre, w_in, w_pool_lin, pool_scale, w_out, g_mix_post, g_ffn2_pre, w2_gate, w2_up, w2_down, g_ffn2_post, loss_target, m_g_ffn1_pre, m_w1_gate, m_w1_up, m_w1_down, m_g_ffn1_post, m_g_mix_pre, m_w_in, m_w_pool_lin, m_pool_scale, m_w_out, m_g_mix_post, m_g_ffn2_pre, m_w2_gate, m_w2_up, m_w2_down, m_g_ffn2_post, v_g_ffn1_pre, v_w1_gate, v_w1_up, v_w1_down, v_g_ffn1_post, v_g_mix_pre, v_w_in, v_w_pool_lin, v_pool_scale, v_w_out, v_g_mix_post, v_g_ffn2_pre, v_w2_gate, v_w2_up, v_w2_down, v_g_ffn2_post):
    given = dict(locals())
    w = {k: given[k] for k in WEIGHTS}
    m = {k: given["m_" + k] for k in WEIGHTS}
    v = {k: given["v_" + k] for k in WEIGHTS}
    small = {k: (w[k][0] if k == "w_pool_lin" else w[k].reshape(1, -1)) for k in SMALL}

    place = jnp.stack([2 * lax.axis_index("x") + lax.axis_index("y"), lax.axis_index("c")]).astype(jnp.int32)
    def as_rows(a, k):
        return jnp.swapaxes(a, 1, 2)[0] if k in ROWS_OUTSIDE else a[0]

    def as_given(a, k):
        return jnp.swapaxes(a[None], 1, 2) if k in ROWS_OUTSIDE else a[None]

    in_kernel = [k for k in TRANSPOSED if k not in ROWS_OUTSIDE]
    exchange = _Exchange({k: _cast_shard(as_rows(w[k], k), k in in_kernel, place, f"cast_{k}") for k in SEGMENTS}, place)
    loss_part, grad_x, small_grads = _local_step(x[0], loss_target[0], small, exchange)
    loss = lax.psum(loss_part, ("x", "y", "c"))

    halves = exchange.summed_halves()
    from_sibling = _swap_halves(halves)

    out_grad, out_delta, out_m, out_v = {}, {}, {}, {}
    for k, mine, sib in zip(SEGMENTS, halves, from_sibling):
        out_grad[k], out_delta[k], out_m[k], out_v[k] = (
            as_given(a, k) for a in _adamw(as_rows(w[k], k), mine, sib, place, as_rows(m[k], k), as_rows(v[k], k),
                                           k in in_kernel, f"adamw_{k}"))

    small_grads["w_pool_lin"] = small_grads["w_pool_lin"][None]
    packed = _pack_small(small_grads)
    gathered = _gather_small(packed).reshape(N_DEV, *packed.shape)
    like = {k: w[k] for k in SMALL}
    results = _adamw_small(gathered, _pack_small(like), _pack_small({k: m[k] for k in SMALL}),
                           _pack_small({k: v[k] for k in SMALL}), "adamw_small")
    for tree, res in zip((out_grad, out_delta, out_m, out_v), results):
        tree.update(_unpack_small(res, like))

    return (loss, grad_x[None], *[out_grad[k] for k in WEIGHTS], *[out_delta[k] for k in WEIGHTS],
            *[out_m[k] for k in WEIGHTS], *[out_v[k] for k in WEIGHTS])
```

```python
import math
import typing

import numpy as np
import jax
import jax.numpy as jnp
from jax import lax
from jax.experimental import pallas as pl
from jax.experimental.pallas import tpu as pltpu

F32 = jnp.float32
BF16 = jnp.bfloat16
MESH = pl.DeviceIdType.MESH

RMS_EPS = 1e-6
HEAD_DIM = 64
POOL_HALF_WINDOWS = (1, 2, 4, 8)
POOL_DIM = 256
GROUP_DIM = 256
DILATIONS = (1, 4, 16)
N_SIDE = 64
N_ATTN_HEADS = 12
ADAM_LR, ADAM_B1, ADAM_B2, ADAM_EPS, ADAM_WD, ADAM_STEP = 0.001, 0.9, 0.999, 1e-08, 0.01, 10

N_CHIPS = 4
V7X_VMEM_LIMIT = 60 * 1024 * 1024

_NT = (((1,), (1,)), ((), ()))
_TN = (((0,), (0,)), ((), ()))


def _dot(a, b):
    return jnp.dot(a, b, preferred_element_type=F32)


def _dot_nt(a, b):
    return lax.dot_general(a, b, _NT, preferred_element_type=F32)


def _dot_tn(a, b):
    return lax.dot_general(a, b, _TN, preferred_element_type=F32)


def _params(**kw):
    return pltpu.CompilerParams(vmem_limit_bytes=V7X_VMEM_LIMIT, **kw)


def _rows(tm, width):
    return pl.BlockSpec((tm, width), lambda i: (i, 0))


def _resident(shape):
    return pl.BlockSpec(shape, lambda i: (0,) * len(shape), pipeline_mode=pl.Buffered(1))


def _const(shape):
    return pl.BlockSpec(shape, lambda i: (0,) * len(shape))


def _inv_rms(x):
    return lax.rsqrt(jnp.mean(x * x, axis=-1, keepdims=True) + RMS_EPS)


def _rms_bwd(x, inv, g, dy):
    n = x * inv
    dn = dy * g
    dx = inv * (dn - n * jnp.mean(dn * n, axis=-1, keepdims=True))
    return dx, jnp.sum(dy * n, axis=0, keepdims=True)


def _accumulate(ref, value):
    @pl.when(pl.program_id(0) == 0)
    def _():
        ref[...] = jnp.zeros_like(ref)

    ref[...] += value


class _Rider(typing.NamedTuple):
    operands: list
    landing: typing.Optional[list]
    sems: tuple
    start: typing.Callable
    wait: typing.Callable


def _hosted_call(body, riders, *, name, steps, in_specs, out_specs, out_shape, args, scratch_shapes=()):
    params = _params(dimension_semantics=("arbitrary",))
    riders = list(riders or [])
    if not riders:
        res = pl.pallas_call(body, name=name, grid=(steps,), in_specs=in_specs, out_specs=out_specs, out_shape=out_shape,
                             scratch_shapes=list(scratch_shapes), compiler_params=params)(*args)
        return list(res), []
    n_in, n_out, n_scratch = len(in_specs), len(out_specs), len(scratch_shapes)
    operands, landing, aliases, spans = [], [], {}, []
    for rd in riders:
        lands = rd.landing if rd.landing is not None else [jax.ShapeDtypeStruct(a.shape, a.dtype) for a in rd.operands]
        if rd.landing is None:
            aliases.update({n_in + len(operands) + i: n_out + len(landing) + i for i in range(len(lands))})
        spans.append((len(operands), len(rd.operands), len(landing), len(lands)))
        operands += rd.operands
        landing += lands
    outs_at = n_in + len(operands)
    scratch_at = outs_at + n_out + len(landing)

    def riding(*refs):
        def each(action):
            for i, (rd, (in_at, n_ops, out_at, n_lands)) in enumerate(zip(riders, spans)):
                sems = refs[scratch_at + n_scratch + 2 * i:scratch_at + n_scratch + 2 * i + 2]
                getattr(rd, action)(refs[n_in + in_at:n_in + in_at + n_ops],
                                    refs[outs_at + n_out + out_at:outs_at + n_out + out_at + n_lands], *sems)

        @pl.when(pl.program_id(0) == 0)
        def _():
            each("start")

        body(*refs[:n_in], *refs[outs_at:outs_at + n_out], *refs[scratch_at:scratch_at + n_scratch])

        @pl.when(pl.program_id(0) == steps - 1)
        def _():
            each("wait")

    any_spec = pl.BlockSpec(memory_space=pl.ANY)
    res = pl.pallas_call(
        riding, name=name, grid=(steps,), in_specs=list(in_specs) + [any_spec] * len(operands),
        out_specs=list(out_specs) + [any_spec] * len(landing), out_shape=list(out_shape) + landing,
        scratch_shapes=list(scratch_shapes) + [pltpu.SemaphoreType.DMA(rd.sems) for rd in riders for _ in range(2)],
        input_output_aliases=aliases, compiler_params=params)(*args, *operands)
    return list(res[:n_out]), [list(res[n_out + out_at:n_out + out_at + n_lands]) for _, _, out_at, n_lands in spans]


_SUB_TILE = 256


def _sub_tiles(tm):
    return [pl.ds(r, _SUB_TILE) for r in range(0, tm, _SUB_TILE)]


def _ffn_fwd(x, g_pre, wg_t, wu_t, wd, g_post, target, name, riders=None, tm=512):
    s, d = x.shape
    ff = wd.shape[0]
    with_loss = target is not None

    def body(*refs):
        if with_loss:
            x_ref, gpre_ref, wg_ref, wu_ref, wd_ref, gpost_ref, t_ref, xo_ref, a_ref, b_ref, f_ref, loss_ref = refs
        else:
            x_ref, gpre_ref, wg_ref, wu_ref, wd_ref, gpost_ref, xo_ref, a_ref, b_ref, f_ref = refs
        loss = 0.0
        for rows in _sub_tiles(tm):
            xv = x_ref[rows, :]
            hb = (xv * _inv_rms(xv) * gpre_ref[...]).astype(BF16)
            a = _dot_nt(hb, wg_ref[...])
            b = _dot_nt(hb, wu_ref[...])
            hh = (a * jax.nn.sigmoid(a)) * b
            f = _dot(hh.astype(BF16), wd_ref[...])
            xo = xv + 0.5 * (f * _inv_rms(f) * gpost_ref[...])
            a_ref[rows, :] = a.astype(BF16)
            b_ref[rows, :] = b.astype(BF16)
            f_ref[rows, :] = f
            if with_loss:
                e = xo - t_ref[rows, :]
                xo_ref[rows, :] = e * (1.0 / d)
                loss = loss + 0.5 * jnp.sum(jnp.mean(e * e, axis=-1, keepdims=True))
            else:
                xo_ref[rows, :] = xo
        if with_loss:
            _accumulate(loss_ref, loss)

    in_specs = [_rows(tm, d), _const((1, d)), _resident((ff, d)), _resident((ff, d)), _resident((ff, d)), _const((1, d))]
    args = [x, g_pre, wg_t, wu_t, wd, g_post]
    out_shape = [jax.ShapeDtypeStruct((s, d), F32), jax.ShapeDtypeStruct((s, ff), BF16),
                 jax.ShapeDtypeStruct((s, ff), BF16), jax.ShapeDtypeStruct((s, d), F32)]
    out_specs = [_rows(tm, d), _rows(tm, ff), _rows(tm, ff), _rows(tm, d)]
    if with_loss:
        in_specs.append(_rows(tm, d))
        args.append(target)
        out_shape.append(jax.ShapeDtypeStruct((8, 128), F32))
        out_specs.append(_const((8, 128)))
    return _hosted_call(body, riders, name=name, steps=s // tm, in_specs=in_specs, out_specs=out_specs, out_shape=out_shape, args=args)


def _ffn_bwd(dxo, x, f, a, b, g_pre, g_post, wg_t, wu_t, wd, name, riders=None, tm=256):
    s, d = x.shape
    ff = wd.shape[0]

    def body(dxo_ref, x_ref, f_ref, a_ref, b_ref, gpre_ref, gpost_ref, wg_ref, wu_ref, wd_ref,
             dx_ref, hh_ref, da_ref, db_ref, df_ref, h_ref, dgpre_ref, dgpost_ref):
        dgpre_sum = dgpost_sum = 0.0
        for rows in _sub_tiles(tm):
            dxo_v = dxo_ref[rows, :]
            fv = f_ref[rows, :]
            df, dgpost = _rms_bwd(fv, _inv_rms(fv), gpost_ref[...], 0.5 * dxo_v)
            dfb = df.astype(BF16)
            dhh = _dot_nt(dfb, wd_ref[...])
            av = a_ref[rows, :].astype(F32)
            bv = b_ref[rows, :].astype(F32)
            sig = jax.nn.sigmoid(av)
            sa = av * sig
            da = (dhh * bv * (sig * (1.0 + av * (1.0 - sig)))).astype(BF16)
            db = (dhh * sa).astype(BF16)
            dh = _dot(da, wg_ref[...]) + _dot(db, wu_ref[...])
            xv = x_ref[rows, :]
            inv = _inv_rms(xv)
            dxn, dgpre = _rms_bwd(xv, inv, gpre_ref[...], dh)
            dx_ref[rows, :] = dxo_v + dxn
            hh_ref[rows, :] = (sa * bv).astype(BF16)
            da_ref[rows, :] = da
            db_ref[rows, :] = db
            df_ref[rows, :] = dfb
            h_ref[rows, :] = (xv * inv * gpre_ref[...]).astype(BF16)
            dgpre_sum, dgpost_sum = dgpre_sum + dgpre, dgpost_sum + dgpost
        _accumulate(dgpre_ref, dgpre_sum)
        _accumulate(dgpost_ref, dgpost_sum)

    return _hosted_call(
        body, riders, name=name, steps=s // tm,
        in_specs=[_rows(tm, d), _rows(tm, d), _rows(tm, d), _rows(tm, ff), _rows(tm, ff), _const((1, d)), _const((1, d)),
                  _resident((ff, d)), _resident((ff, d)), _resident((ff, d))],
        out_specs=[_rows(tm, d), _rows(tm, ff), _rows(tm, ff), _rows(tm, ff), _rows(tm, d), _rows(tm, d),
                   _const((1, d)), _const((1, d))],
        out_shape=[jax.ShapeDtypeStruct((s, d), F32), jax.ShapeDtypeStruct((s, ff), BF16), jax.ShapeDtypeStruct((s, ff), BF16),
                   jax.ShapeDtypeStruct((s, ff), BF16), jax.ShapeDtypeStruct((s, d), BF16), jax.ShapeDtypeStruct((s, d), BF16),
                   jax.ShapeDtypeStruct((1, d), F32), jax.ShapeDtypeStruct((1, d), F32)],
        args=[dxo, x, f, a, b, g_pre, g_post, wg_t, wu_t, wd])


def _wgrad(lhs, rhs, name, riders=None, rt=256):
    s, r = lhs.shape
    c = rhs.shape[1]

    def body(l_ref, r_ref, o_ref):
        o_ref[...] = _dot_tn(l_ref[...], r_ref[...])

    (out,), riding = _hosted_call(
        body, riders, name=name, steps=pl.cdiv(r, rt), in_specs=[pl.BlockSpec((s, rt), lambda i: (0, i)), _resident((s, c))],
        out_specs=[pl.BlockSpec((rt, c), lambda i: (i, 0))], out_shape=[jax.ShapeDtypeStruct((r, c), F32)], args=[lhs, rhs])
    return out, riding


def _attn_dtype(dilation):
    return BF16 if dilation == 1 else F32


def _in_fwd(x, g, w_in_t, name, tm=512):
    s, d = x.shape
    d_in = w_in_t.shape[0]
    n_groups = len(DILATIONS)
    dtypes = [_attn_dtype(dil) for dil in DILATIONS] * 3

    def body(x_ref, g_ref, w_ref, h_ref, u_ref, *part_refs):
        xv = x_ref[...]
        hb = (xv * _inv_rms(xv) * g_ref[...]).astype(BF16)
        h_ref[...] = hb
        z = _dot_nt(hb, w_ref[...])
        u_ref[...] = z[:, :POOL_DIM]
        for j, ref in enumerate(part_refs):
            part = z[:, POOL_DIM + GROUP_DIM * j:POOL_DIM + GROUP_DIM * (j + 1)]
            ref[...] = (part * _SCORE_SCALE if j < n_groups else part).astype(ref.dtype)

    return pl.pallas_call(
        body, name=name, grid=(s // tm,), in_specs=[_rows(tm, d), _const((1, d)), _resident((d_in, d))],
        out_specs=[_rows(tm, d), _rows(tm, POOL_DIM)] + [_rows(tm, GROUP_DIM)] * len(dtypes),
        out_shape=[jax.ShapeDtypeStruct((s, d), BF16), jax.ShapeDtypeStruct((s, POOL_DIM), F32)]
        + [jax.ShapeDtypeStruct((s, GROUP_DIM), dt) for dt in dtypes],
        compiler_params=_params(dimension_semantics=("arbitrary",)))(x, g, w_in_t)


def _in_bwd(du, dparts, x, dxo, g, w_in_t, name, riders=None, tm=512):
    s, d = x.shape
    d_in = w_in_t.shape[0]
    n_parts = len(dparts)

    def body(du_ref, *refs):
        part_refs = refs[:n_parts]
        x_ref, dxo_ref, g_ref, w_ref, dx_ref, dz_ref, dg_ref = refs[n_parts:]
        dz = jnp.concatenate([r[...].astype(BF16) for r in (du_ref,) + part_refs], axis=1)
        dz_ref[...] = dz
        dh = _dot(dz, w_ref[...])
        xv = x_ref[...]
        dxn, dg = _rms_bwd(xv, _inv_rms(xv), g_ref[...], dh)
        dx_ref[...] = dxo_ref[...] + dxn
        _accumulate(dg_ref, dg)

    return _hosted_call(
        body, riders, name=name, steps=s // tm,
        in_specs=[_rows(tm, POOL_DIM)] + [_rows(tm, GROUP_DIM)] * n_parts + [_rows(tm, d), _rows(tm, d), _const((1, d)),
                                                                             _resident((d_in, d))],
        out_specs=[_rows(tm, d), _rows(tm, d_in), _const((1, d))],
        out_shape=[jax.ShapeDtypeStruct((s, d), F32), jax.ShapeDtypeStruct((s, d_in), BF16), jax.ShapeDtypeStruct((1, d), F32)],
        args=[du, *dparts, x, dxo, g, w_in_t])


_POOL_HALO = 8


def _pool_chain(v, first_shift):
    n = v.shape[0]
    p2 = v + pltpu.roll(v, first_shift, 0)
    p4 = pltpu.roll(p2, 1, 0) + pltpu.roll(p2, n - 1, 0)
    p8 = pltpu.roll(p4, 2, 0) + pltpu.roll(p4, n - 2, 0)
    p16 = pltpu.roll(p8, 4, 0) + pltpu.roll(p8, n - 4, 0)
    group = lax.broadcasted_iota(jnp.int32, v.shape, 1) // HEAD_DIM
    return jnp.where(group == 0, p2, jnp.where(group == 1, p4, jnp.where(group == 2, p8, p16)))


def _pool_count(t0, rows, s):
    t = t0 + lax.broadcasted_iota(jnp.int32, (rows, POOL_DIM), 0)
    group = lax.broadcasted_iota(jnp.int32, (rows, POOL_DIM), 1) // HEAD_DIM
    half = jnp.where(group == 0, 1, jnp.where(group == 1, 2, jnp.where(group == 2, 4, 8)))
    cnt = jnp.minimum(t + half, s) - jnp.maximum(t - half, 0)
    return jnp.maximum(cnt, 1).astype(F32)


def _pad_rows(ref, pad_ref, s):
    zeros = jnp.zeros((_POOL_HALO, pad_ref.shape[1]), pad_ref.dtype)
    pad_ref[pl.ds(0, _POOL_HALO), :] = zeros
    pad_ref[pl.ds(_POOL_HALO + s, _POOL_HALO), :] = zeros
    pad_ref[pl.ds(_POOL_HALO, s), :] = ref[...]


def _pool_fwd(u, w_bd, scale, name, tm=512):
    s = u.shape[0]
    ext = tm + 2 * _POOL_HALO

    def body(u_ref, w_ref, sc_ref, o_ref, upad):
        _pad_rows(u_ref, upad, s)

        def tile(i, carry):
            t0 = pl.multiple_of(i * tm, tm)
            uv = upad[pl.ds(t0, ext), :]
            win = _pool_chain(uv, 1)[_POOL_HALO:_POOL_HALO + tm]
            y = win / _pool_count(t0, tm, s) - uv[_POOL_HALO:_POOL_HALO + tm]
            o_ref[pl.ds(t0, tm), :] = (_dot(y.astype(BF16), w_ref[...]) * sc_ref[...]).astype(BF16)
            return carry

        lax.fori_loop(0, s // tm, tile, 0)

    return pl.pallas_call(body, name=name, out_shape=jax.ShapeDtypeStruct((s, POOL_DIM), BF16),
                          scratch_shapes=[pltpu.VMEM((s + 2 * _POOL_HALO, POOL_DIM), F32)],
                          compiler_params=_params())(u, w_bd, scale)


def _pool_bwd(u, da, w_bd, scale, name, tm=512):
    s = u.shape[0]
    ext = tm + 2 * _POOL_HALO

    def body(u_ref, da_ref, w_ref, sc_ref, du_ref, dw_ref, dsc_ref, upad, dapad):
        _pad_rows(u_ref, upad, s)
        _pad_rows(da_ref, dapad, s)
        dw_ref[...] = jnp.zeros_like(dw_ref)
        dsc_ref[...] = jnp.zeros_like(dsc_ref)

        def tile(i, carry):
            t0 = pl.multiple_of(i * tm, tm)
            uv = upad[pl.ds(t0, ext), :]
            dav = dapad[pl.ds(t0, ext), :]
            win = _pool_chain(uv, 1)[_POOL_HALO:_POOL_HALO + tm]
            yb = (win / _pool_count(t0, tm, s) - uv[_POOL_HALO:_POOL_HALO + tm]).astype(BF16)
            yl = _dot(yb, w_ref[...])
            da_c = dav[_POOL_HALO:_POOL_HALO + tm]
            dsc_ref[...] += jnp.sum(da_c * yl, axis=0, keepdims=True)
            dyl = (dav * sc_ref[...]).astype(BF16)
            dw_ref[...] += _dot_tn(yb, dyl[_POOL_HALO:_POOL_HALO + tm])
            dy = _dot_nt(dyl, w_ref[...])
            dyc = dy / _pool_count(t0 - _POOL_HALO, ext, s)
            du_ref[pl.ds(t0, tm), :] = (_pool_chain(dyc, ext - 1) - dy)[_POOL_HALO:_POOL_HALO + tm]
            return carry

        lax.fori_loop(0, s // tm, tile, 0)

    pool_cols = pl.BlockSpec((s, POOL_DIM), lambda i: (0, 0), pipeline_mode=pl.Buffered(1))
    return pl.pallas_call(
        body, name=name, grid=(1,),
        in_specs=[pool_cols, pool_cols, _const((POOL_DIM, POOL_DIM)), _const((1, POOL_DIM))],
        out_specs=[_const((s, POOL_DIM)), _const((POOL_DIM, POOL_DIM)), _const((1, POOL_DIM))],
        out_shape=[jax.ShapeDtypeStruct((s, POOL_DIM), F32), jax.ShapeDtypeStruct((POOL_DIM, POOL_DIM), F32),
                   jax.ShapeDtypeStruct((1, POOL_DIM), F32)],
        scratch_shapes=[pltpu.VMEM((s + 2 * _POOL_HALO, POOL_DIM), F32), pltpu.VMEM((s + 2 * _POOL_HALO, POOL_DIM), F32)],
        compiler_params=_params(dimension_semantics=("arbitrary",)))(u, da, w_bd, scale)


_BQ = 128
_KW = _BQ + 2 * N_SIDE
_PAIR = 2 * HEAD_DIM
_NEG = -1e30
_ATTN_UNROLL = 8
_SCORE_SCALE = HEAD_DIM ** -0.5


def _stack_heads(x):
    lane_head = lax.broadcasted_iota(jnp.int32, x.shape, 1) // HEAD_DIM
    zero = jnp.zeros_like(x)
    return jnp.concatenate([jnp.where(lane_head == 0, x, zero), jnp.where(lane_head == 1, x, zero)], axis=0)


def _unstack_heads(x):
    lane_head = lax.broadcasted_iota(jnp.int32, (_BQ, _PAIR), 1) // HEAD_DIM
    return jnp.where(lane_head == 0, x[:_BQ], x[_BQ:])


def _stack_cols(x):
    return jnp.concatenate([x[:, 0:1], x[:, HEAD_DIM:HEAD_DIM + 1]], axis=0)


def _fill_bias(bias_ref, slopes_ref, dilation):
    row = lax.broadcasted_iota(jnp.int32, (2 * _BQ, _KW), 0)
    col = lax.broadcasted_iota(jnp.int32, (2 * _BQ, _KW), 1)
    pair = 2 * pl.program_id(0)
    slope = jnp.where(row < _BQ, slopes_ref[pair], slopes_ref[pair + 1]) * float(dilation)

    @pl.when(pl.program_id(1) == 0)
    def _():
        for j in range(3):
            dist = jnp.abs(col - (row & (_BQ - 1)) - j * N_SIDE)
            bias_ref[j] = jnp.where(dist <= N_SIDE, -slope * dist.astype(F32), _NEG)


def _block_window(i, n_blocks, length):
    q0 = pl.multiple_of(i * _BQ, _BQ)
    ws = pl.multiple_of(jnp.clip(q0 - N_SIDE, 0, length - _KW), N_SIDE)
    return q0, ws, jnp.where(i == 0, 0, jnp.where(i == n_blocks - 1, 2, 1))


_FREE_STRIDE = 4


def _residue_views(dilation, seq, ins, outs, tmps):
    step = pl.program_id(1)
    if dilation <= _FREE_STRIDE:
        def rows(start, count):
            return pl.ds(start, count) if dilation == 1 else pl.ds(start * dilation + step, count, stride=dilation)

        return ins, outs, rows, lambda: None
    inner = dilation // _FREE_STRIDE
    assert inner <= _FREE_STRIDE and len(tmps) == len(ins) + len(outs)
    first, second = step // inner, step % inner
    coarse = pl.ds(first, seq // _FREE_STRIDE, stride=_FREE_STRIDE)
    in_tmps, out_tmps = tmps[:len(ins)], tmps[len(ins):]

    @pl.when(second == 0)
    def _():
        for ref, tmp in zip(ins, in_tmps):
            tmp[...] = ref[coarse, :]

    def flush():
        @pl.when(second == inner - 1)
        def _():
            for ref, tmp in zip(outs, out_tmps):
                ref[coarse, :] = tmp[...]

    return in_tmps, out_tmps, lambda start, count: pl.ds(start * inner + second, count, stride=inner), flush


def _attn_call(body, name, dilation, seq, n_in, out_dtypes, scratch, buffers):
    col = pl.BlockSpec((seq, _PAIR), lambda c, r: (0, c), pipeline_mode=pl.Buffered(buffers))
    tmps = [pltpu.VMEM((seq // _FREE_STRIDE, _PAIR), F32)] * (n_in + len(out_dtypes) if dilation > _FREE_STRIDE else 0)
    return pl.pallas_call(
        body, name=name, grid=(GROUP_DIM // _PAIR, dilation),
        in_specs=[pl.BlockSpec(memory_space=pltpu.SMEM)] + [col] * n_in, out_specs=[col] * len(out_dtypes),
        out_shape=[jax.ShapeDtypeStruct((seq, GROUP_DIM), dt) for dt in out_dtypes], scratch_shapes=scratch + tmps,
        compiler_params=_params(dimension_semantics=("arbitrary", "arbitrary")))


def _staged(dilation, length, rows, sources, scratch):
    if dilation == 1:
        return sources
    for src, dst in zip(sources, scratch):
        dst[...] = src[rows(0, length), :].astype(BF16)
    return scratch


def _attn_fwd(q, k, v, slopes, dilation, name):
    seq = q.shape[0]
    length = seq // dilation
    n_blocks = length // _BQ
    n_stage = 0 if dilation == 1 else 3

    def body(sl_ref, q_ref, k_ref, v_ref, o_ref, lse_ref, *scratch):
        bias_ref, tmps = scratch[n_stage], scratch[n_stage + 1:]
        (q_in, k_in, v_in), (o_out, lse_out), rows, flush = _residue_views(dilation, seq, (q_ref, k_ref, v_ref), (o_ref, lse_ref), tmps)
        qs, ks, vs = _staged(dilation, length, rows, (q_in, k_in, v_in), scratch[:n_stage])
        _fill_bias(bias_ref, sl_ref, dilation)

        def block(i, carry):
            q0, ws, which = _block_window(i, n_blocks, length)
            kw = ks[pl.ds(ws, _KW), :]
            vw = vs[pl.ds(ws, _KW), :]
            sc = _dot_nt(_stack_heads(qs[pl.ds(q0, _BQ), :]), kw) + bias_ref[which]
            m = jnp.max(sc, axis=-1, keepdims=True)
            p = jnp.exp(sc - m)
            den = jnp.sum(p, axis=-1, keepdims=True)
            o_out[rows(q0, _BQ), :] = _unstack_heads(_dot(p.astype(BF16), vw) / den)
            lse_out[rows(q0, _BQ), :] = _unstack_heads(jnp.broadcast_to(m + jnp.log(den), (2 * _BQ, _PAIR)))
            return carry

        lax.fori_loop(0, n_blocks, block, 0, unroll=min(_ATTN_UNROLL, n_blocks))
        flush()

    stage = pltpu.VMEM((length, _PAIR), BF16)
    bias = pltpu.VMEM((3, 2 * _BQ, _KW), F32)
    return _attn_call(body, name, dilation, seq, 3, [F32, F32], [stage] * n_stage + [bias], 2)(slopes, q, k, v)


def _attn_bwd(q, k, v, do, lse, cterm, slopes, dilation, name):
    seq = q.shape[0]
    length = seq // dilation
    n_blocks = length // _BQ
    n_stage = 0 if dilation == 1 else 4

    def body(sl_ref, q_ref, k_ref, v_ref, do_ref, lse_ref, c_ref, dq_ref, dk_ref, dv_ref, *scratch):
        (dk_acc, dv_acc, bias_ref), tmps = scratch[n_stage:n_stage + 3], scratch[n_stage + 3:]
        (q_in, k_in, v_in, do_in, lse_in, c_in), (dq_out, dk_out, dv_out), rows, flush = _residue_views(
            dilation, seq, (q_ref, k_ref, v_ref, do_ref, lse_ref, c_ref), (dq_ref, dk_ref, dv_ref), tmps)
        all_rows = rows(0, length)
        qs, ks, vs, dos = _staged(dilation, length, rows, (q_in, k_in, v_in, do_in), scratch[:n_stage])
        dk_acc[...] = jnp.zeros_like(dk_acc)
        dv_acc[...] = jnp.zeros_like(dv_acc)
        _fill_bias(bias_ref, sl_ref, dilation)

        def block(i, carry):
            q0, ws, which = _block_window(i, n_blocks, length)
            qm = _stack_heads(qs[pl.ds(q0, _BQ), :])
            dom = _stack_heads(dos[pl.ds(q0, _BQ), :])
            kw = ks[pl.ds(ws, _KW), :]
            vw = vs[pl.ds(ws, _KW), :]
            p = jnp.exp(_dot_nt(qm, kw) + bias_ref[which] - _stack_cols(lse_in[rows(q0, _BQ), :]))
            ds = (p * (_dot_nt(dom, vw) + _stack_cols(c_in[rows(q0, _BQ), :]))).astype(BF16)
            dq_out[rows(q0, _BQ), :] = (_unstack_heads(_dot(ds, kw)) * _SCORE_SCALE).astype(dq_out.dtype)
            dk_acc[pl.ds(ws, _KW), :] += _dot_tn(ds, qm)
            dv_acc[pl.ds(ws, _KW), :] += _dot_tn(p.astype(BF16), dom)
            return carry

        lax.fori_loop(0, n_blocks, block, 0, unroll=min(_ATTN_UNROLL, n_blocks))
        dk_out[all_rows, :] = dk_acc[...].astype(dk_out.dtype)
        dv_out[all_rows, :] = dv_acc[...].astype(dv_out.dtype)
        flush()

    stage = pltpu.VMEM((length, _PAIR), BF16)
    acc = pltpu.VMEM((length, _PAIR), F32)
    bias = pltpu.VMEM((3, 2 * _BQ, _KW), F32)
    return _attn_call(body, name, dilation, seq, 6, [_attn_dtype(dilation)] * 3, [stage] * n_stage + [acc] * 2 + [bias],
                      2 if dilation == 1 else 1)(slopes, q, k, v, do, lse, cterm)


def _group_weights(lses):
    m = jnp.maximum(jnp.maximum(lses[0], lses[1]), lses[2])
    es = [jnp.exp(l - m) for l in lses]
    den = es[0] + es[1] + es[2]
    return [e / den for e in es]


def _out_fwd(a_pool, outs, lses, x, w_out, g, name, tm=512):
    s, d = x.shape
    width = POOL_DIM + 3 * GROUP_DIM

    def body(ap_ref, o0, o1, o2, l0, l1, l2, x_ref, w_ref, g_ref, xo_ref, mix_ref, cat_ref):
        alphas = _group_weights([l0[...], l1[...], l2[...]])
        cat = jnp.concatenate([ap_ref[...]] + [(o[...] * al).astype(BF16) for o, al in zip((o0, o1, o2), alphas)], axis=1)
        cat_ref[...] = cat
        mix = _dot(cat, w_ref[...])
        mix_ref[...] = mix
        xo_ref[...] = x_ref[...] + mix * _inv_rms(mix) * g_ref[...]

    return pl.pallas_call(
        body, name=name, grid=(s // tm,),
        in_specs=[_rows(tm, POOL_DIM)] + [_rows(tm, GROUP_DIM)] * 6 + [_rows(tm, d), _resident(w_out.shape), _const((1, d))],
        out_specs=[_rows(tm, d), _rows(tm, d), _rows(tm, width)],
        out_shape=[jax.ShapeDtypeStruct((s, d), F32), jax.ShapeDtypeStruct((s, d), F32), jax.ShapeDtypeStruct((s, width), BF16)],
        compiler_params=_params(dimension_semantics=("arbitrary",)))(a_pool, *outs, *lses, x, w_out, g)


def _out_bwd(dxo, mix, outs, lses, w_out, g, head_ones, name, tm=512):
    s, d = mix.shape

    def body(dxo_ref, mix_ref, o0, o1, o2, l0, l1, l2, w_ref, g_ref, ones_ref, dpool_ref, dmix_ref, do0, do1, do2, c0, c1, c2, dg_ref):
        mv = mix_ref[...]
        dmix, dg = _rms_bwd(mv, _inv_rms(mv), g_ref[...], dxo_ref[...])
        dmb = dmix.astype(BF16)
        dmix_ref[...] = dmb
        _accumulate(dg_ref, dg)
        dcat = _dot_nt(dmb, w_ref[...])
        dpool_ref[...] = dcat[:, :POOL_DIM]
        alphas = _group_weights([l0[...], l1[...], l2[...]])
        das = [dcat[:, POOL_DIM + GROUP_DIM * j:POOL_DIM + GROUP_DIM * (j + 1)] for j in range(3)]
        prod = sum(da * (o[...] * al) for da, o, al in zip(das, (o0, o1, o2), alphas))
        hi = prod.astype(BF16)
        lo = (prod - hi.astype(F32)).astype(BF16)
        total = _dot(hi, ones_ref[...]) + _dot(lo, ones_ref[...])
        for da, al, do_ref, c_ref in zip(das, alphas, (do0, do1, do2), (c0, c1, c2)):
            do_ref[...] = (da * al).astype(do_ref.dtype)
            c_ref[...] = -al * total

    return pl.pallas_call(
        body, name=name, grid=(s // tm,),
        in_specs=[_rows(tm, d), _rows(tm, d)] + [_rows(tm, GROUP_DIM)] * 6 + [_resident(w_out.shape), _const((1, d)),
                                                                             _const((GROUP_DIM, GROUP_DIM))],
        out_specs=[_rows(tm, POOL_DIM), _rows(tm, d)] + [_rows(tm, GROUP_DIM)] * 6 + [_const((1, d))],
        out_shape=[jax.ShapeDtypeStruct((s, POOL_DIM), F32), jax.ShapeDtypeStruct((s, d), BF16)]
        + [jax.ShapeDtypeStruct((s, GROUP_DIM), _attn_dtype(dil)) for dil in DILATIONS]
        + [jax.ShapeDtypeStruct((s, GROUP_DIM), F32)] * 3 + [jax.ShapeDtypeStruct((1, d), F32)],
        compiler_params=_params(dimension_semantics=("arbitrary",)))(dxo, mix, *outs, *lses, w_out, g, head_ones)


def _alibi_slopes():
    return np.array([2.0 ** (-8.0 * (i + 1) / N_ATTN_HEADS) for i in range(N_ATTN_HEADS)], np.float32)


def _block_diag(w_lin):
    n, c, _ = w_lin.shape
    eye = jnp.eye(n, dtype=w_lin.dtype)
    return (eye[:, None, :, None] * w_lin[:, :, None, :]).reshape(n * c, n * c)


class _NoExchange:
    def __init__(self, full):
        self.full, self.grads = full, {}

    def first_weights(self):
        return self.full

    def riders(self, host):
        return []

    def landed(self, host, results):
        pass

    def rest_weights(self):
        return self.full

    def gradient(self, name, grad):
        self.grads[name] = grad


def _local_step(x, target, small, exchange):
    s, d = x.shape
    slopes = _alibi_slopes()
    group_slopes = [jnp.asarray(slopes[4 * g:4 * g + 4]) for g in range(3)]
    w_bd = _block_diag(small["w_pool_lin"]).astype(BF16)
    head_ones = jnp.asarray(np.kron(np.eye(GROUP_DIM // HEAD_DIM), np.ones((HEAD_DIM, HEAD_DIM))), BF16)

    full = exchange.first_weights()
    (x1, a1, b1, f1), riding = _ffn_fwd(x, small["g_ffn1_pre"], full["w1_gate"], full["w1_up"], full["w1_down"],
                                        small["g_ffn1_post"], None, "ffn1_fwd", exchange.riders("ffn1_fwd"))
    exchange.landed("ffn1_fwd", riding)
    full = {**full, **exchange.rest_weights()}
    h2, u, *parts = _in_fwd(x1, small["g_mix_pre"], full["w_in"], "in_fwd")
    qs, ks, vs = parts[0:3], parts[3:6], parts[6:9]
    a_pool = _pool_fwd(u, w_bd, small["pool_scale"], "pool_fwd")
    outs, lses = [], []
    for g, dil in enumerate(DILATIONS):
        o, lse = _attn_fwd(qs[g], ks[g], vs[g], group_slopes[g], dil, f"attn_fwd{g}")
        outs.append(o)
        lses.append(lse)
    x2, mix, cat = _out_fwd(a_pool, outs, lses, x1, full["w_out"], small["g_mix_post"], "out_fwd")
    (dx3, a2, b2, f2, loss_part), _ = _ffn_fwd(x2, small["g_ffn2_pre"], full["w2_gate"], full["w2_up"], full["w2_down"],
                                               small["g_ffn2_post"], target, "ffn2_fwd")

    small_grads = {}

    def hosted(call, host, *args):
        results, riding = call(*args, host, exchange.riders(host))
        exchange.landed(host, riding)
        return results

    def ffn_backward(tag, dxo, x_in, f, a, b):
        n = tag[-1]
        dx, hh, da, db, df, h, dg_pre, dg_post = hosted(
            _ffn_bwd, f"{tag}_bwd", dxo, x_in, f, a, b, small[f"g_{tag}_pre"], small[f"g_{tag}_post"],
            full[f"w{n}_gate"], full[f"w{n}_up"], full[f"w{n}_down"])
        for part, lhs, rhs in (("down", hh, df), ("gate", da, h), ("up", db, h)):
            exchange.gradient(f"w{n}_{part}", hosted(_wgrad, f"{tag}_wgrad_{part}", lhs, rhs))
        small_grads[f"g_{tag}_pre"], small_grads[f"g_{tag}_post"] = dg_pre, dg_post
        return dx

    dx2 = ffn_backward("ffn2", dx3, x2, f2, a2, b2)
    dpool, dmix, *dos_cs, small_grads["g_mix_post"] = _out_bwd(dx2, mix, outs, lses, full["w_out"], small["g_mix_post"],
                                                               head_ones, "out_bwd")
    dos, cs = dos_cs[:3], dos_cs[3:]
    dqs, dks, dvs = [], [], []
    for g, dil in enumerate(DILATIONS):
        dq, dk, dv = _attn_bwd(qs[g], ks[g], vs[g], dos[g], lses[g], cs[g], group_slopes[g], dil, f"attn_bwd{g}")
        dqs.append(dq)
        dks.append(dk)
        dvs.append(dv)
    du, dw_bd, small_grads["pool_scale"] = _pool_bwd(u, dpool, w_bd, small["pool_scale"], "pool_bwd")
    n_pool = len(POOL_HALF_WINDOWS)
    small_grads["w_pool_lin"] = jnp.stack(
        [dw_bd[HEAD_DIM * g:HEAD_DIM * (g + 1), HEAD_DIM * g:HEAD_DIM * (g + 1)] for g in range(n_pool)])
    dx1, dz, small_grads["g_mix_pre"] = hosted(_in_bwd, "in_bwd", du, dqs + dks + dvs, x1, dx2, small["g_mix_pre"], full["w_in"])
    exchange.gradient("w_in", hosted(_wgrad, "wgrad_in", dz, h2))
    dx0 = ffn_backward("ffn1", dx1, x, f1, a1, b1)
    exchange.gradient("w_out", hosted(_wgrad, "wgrad_out", cat, dmix))
    return loss_part[0, 0], dx0, small_grads


SEGMENTS = ("w1_gate", "w1_up", "w1_down", "w_in", "w_out", "w2_gate", "w2_up", "w2_down")
TRANSPOSED = ("w1_gate", "w1_up", "w_in", "w2_gate", "w2_up")
ROWS_OUTSIDE = ("w1_gate", "w1_up", "w2_gate", "w2_up")
HALF = 512


def _place():
    x, y, c = lax.axis_index("x"), lax.axis_index("y"), lax.axis_index("c")
    other_chips = [(1 - x, y), (x, 1 - y), (1 - x, 1 - y)]
    return x, y, c, other_chips


def _chip_rows(chip, rows):
    return pl.ds(pl.multiple_of((2 * chip[0] + chip[1]) * rows, 16), rows)


def _cols(c):
    return pl.ds(pl.multiple_of(c * HALF, HALF), HALF)


def _cast_shards(shards, transposed, place, name):
    n = len(shards)
    rows = [w.shape[1] if t else w.shape[0] for w, t in zip(shards, transposed)]

    def body(place_ref, *refs):
        for w_ref, o_ref, t in zip(refs[:n], refs[n:], transposed):
            o_ref[...] = (w_ref[...].T if t else w_ref[...]).astype(BF16)

    once = pl.Buffered(1)
    return pl.pallas_call(
        body, name=name,
        grid_spec=pltpu.PrefetchScalarGridSpec(
            num_scalar_prefetch=1, grid=(1,),
            in_specs=[pl.BlockSpec(w.shape, lambda i, place: (0, 0), pipeline_mode=once) for w in shards],
            out_specs=[pl.BlockSpec((r, 1024), lambda i, place: (place[0], 0), pipeline_mode=once) for r in rows]),
        out_shape=[jax.ShapeDtypeStruct((N_CHIPS * r, 1024), BF16) for r in rows],
        compiler_params=_params(dimension_semantics=("arbitrary",)))(place, *shards)


def _gather_weights(bufs):
    n = len(bufs)
    rows = [b.shape[0] // N_CHIPS for b in bufs]

    def body(*refs):
        outs = refs[n:2 * n]
        send_sems, recv_sems, fwd_send_sems, fwd_recv_sems = refs[2 * n:]
        x, y, c, chips = _place()
        me = (x, y)

        def ici(j, k, src_chip, to):
            blk = outs[k].at[_chip_rows(src_chip, rows[k]), _cols(c)]
            return pltpu.make_async_remote_copy(src_ref=blk, dst_ref=blk, send_sem=send_sems.at[j, k], recv_sem=recv_sems.at[j, k],
                                                device_id=to, device_id_type=MESH)

        def d2d(j, k, src_chip, half):
            blk = outs[k].at[_chip_rows(src_chip, rows[k]), _cols(half)]
            return pltpu.make_async_remote_copy(src_ref=blk, dst_ref=blk, send_sem=fwd_send_sems.at[j, k],
                                                recv_sem=fwd_recv_sems.at[j, k], device_id=(x, y, 1 - c), device_id_type=MESH)

        sends = [ici(j, k, me, (*chip, c)) for j, chip in enumerate(chips) for k in range(n)]
        for cp in sends:
            cp.start()
        forwards = []
        for j, chip in enumerate(chips):
            for k in range(n):
                ici(j, k, chip, (x, y, c)).wait_recv()
                fw = d2d(j, k, chip, c)
                fw.start()
                forwards.append(fw)
        for j, chip in enumerate(chips):
            for k in range(n):
                d2d(j, k, chip, 1 - c).wait_recv()
        for cp in sends + forwards:
            cp.wait_send()

    any_spec = pl.BlockSpec(memory_space=pl.ANY)
    return pl.pallas_call(
        body, name="gather_weights", in_specs=[any_spec] * n, out_specs=[any_spec] * n,
        out_shape=[jax.ShapeDtypeStruct(b.shape, b.dtype) for b in bufs], input_output_aliases={k: k for k in range(n)},
        scratch_shapes=[pltpu.SemaphoreType.DMA((3, n)), pltpu.SemaphoreType.DMA((3, n)),
                        pltpu.SemaphoreType.DMA((3, n)), pltpu.SemaphoreType.DMA((3, n))])(*bufs)


def _gather_rider(bufs):
    n = len(bufs)
    rows = [b.shape[0] // N_CHIPS for b in bufs]

    def copies(outs, send_sems, recv_sems, inbound):
        x, y, c, chips = _place()
        for j, chip in enumerate(chips):
            for k in range(n):
                src_chip = chip if inbound else (x, y)
                blk = outs[k].at[_chip_rows(src_chip, rows[k]), _cols(c)]
                yield pltpu.make_async_remote_copy(src_ref=blk, dst_ref=blk, send_sem=send_sems.at[j, k], recv_sem=recv_sems.at[j, k],
                                                   device_id=(*chip, c), device_id_type=MESH)

    def start(ins, outs, send_sems, recv_sems):
        for cp in copies(outs, send_sems, recv_sems, False):
            cp.start()

    def wait(ins, outs, send_sems, recv_sems):
        for cp in copies(outs, send_sems, recv_sems, True):
            cp.wait_recv()
        for cp in copies(outs, send_sems, recv_sems, False):
            cp.wait_send()

    return _Rider(list(bufs), None, (3, n), start, wait)


def _forward_halves(bufs, name):
    n = len(bufs)
    rows = [b.shape[0] // N_CHIPS for b in bufs]

    def body(*refs):
        outs = refs[n:2 * n]
        send_sems, recv_sems = refs[2 * n:]
        x, y, c, chips = _place()

        def d2d(j, k, chip, half):
            blk = outs[k].at[_chip_rows(chip, rows[k]), _cols(half)]
            return pltpu.make_async_remote_copy(src_ref=blk, dst_ref=blk, send_sem=send_sems.at[j, k], recv_sem=recv_sems.at[j, k],
                                                device_id=(x, y, 1 - c), device_id_type=MESH)

        forwards = [d2d(j, k, chip, c) for j, chip in enumerate(chips) for k in range(n)]
        for cp in forwards:
            cp.start()
        for j, chip in enumerate(chips):
            for k in range(n):
                d2d(j, k, chip, 1 - c).wait_recv()
        for cp in forwards:
            cp.wait_send()

    any_spec = pl.BlockSpec(memory_space=pl.ANY)
    return pl.pallas_call(
        body, name=name, in_specs=[any_spec] * n, out_specs=[any_spec] * n,
        out_shape=[jax.ShapeDtypeStruct(b.shape, b.dtype) for b in bufs], input_output_aliases={k: k for k in range(n)},
        scratch_shapes=[pltpu.SemaphoreType.DMA((3, n)), pltpu.SemaphoreType.DMA((3, n))])(*bufs)


def _sibling_rider(grads):
    n = len(grads)

    def copies(ins, outs, send_sems, recv_sems):
        x, y, c, _ = _place()
        return [pltpu.make_async_remote_copy(src_ref=ins[k].at[:, pl.ds(1 - c, 1)], dst_ref=outs[k], send_sem=send_sems.at[k],
                                             recv_sem=recv_sems.at[k], device_id=(x, y, 1 - c), device_id_type=MESH)
                for k in range(n)]

    def start(*refs):
        for cp in copies(*refs):
            cp.start()

    def wait(*refs):
        for cp in copies(*refs):
            cp.wait()

    return _Rider(list(grads), [jax.ShapeDtypeStruct((N_CHIPS, 1) + g.shape[2:], F32) for g in grads], (n,), start, wait)


def _alone(rider, name):
    n, n_out = len(rider.operands), len(rider.landing)

    def body(*refs):
        rider.start(refs[:n], refs[n:n + n_out], *refs[n + n_out:])
        rider.wait(refs[:n], refs[n:n + n_out], *refs[n + n_out:])

    any_spec = pl.BlockSpec(memory_space=pl.ANY)
    return pl.pallas_call(body, name=name, in_specs=[any_spec] * n, out_specs=[any_spec] * n_out, out_shape=rider.landing,
                          scratch_shapes=[pltpu.SemaphoreType.DMA(rider.sems)] * 2)(*rider.operands)


def _chip_sum(grad, from_sibling, place, name):
    rh, width = grad.shape[2:]

    def body(place_ref, g_ref, s_ref, own_ref, all_ref):
        total = g_ref[0, 0] + s_ref[0, 0]
        all_ref[0, 0] = total.astype(BF16)

        @pl.when(pl.program_id(0) == place_ref[0])
        def _():
            own_ref[0] = total

    blk = (1, 1, rh, width)
    return pl.pallas_call(
        body, name=name,
        grid_spec=pltpu.PrefetchScalarGridSpec(
            num_scalar_prefetch=1, grid=(N_CHIPS,),
            in_specs=[pl.BlockSpec(blk, lambda p, place: (p, place[1], 0, 0)), pl.BlockSpec(blk, lambda p, place: (p, 0, 0, 0))],
            out_specs=[pl.BlockSpec((1, rh, width), lambda p, place: (0, 0, 0)), pl.BlockSpec(blk, lambda p, place: (p, 0, 0, 0))]),
        out_shape=[jax.ShapeDtypeStruct((1, rh, width), F32), jax.ShapeDtypeStruct((N_CHIPS, 1, rh, width), BF16)],
        compiler_params=_params(dimension_semantics=("arbitrary",)))(place, grad, from_sibling)


def _scatter_rider(sums):
    n = len(sums)

    def copies(ins, outs, send_sems, recv_sems):
        x, y, c, chips = _place()
        return [pltpu.make_async_remote_copy(src_ref=ins[k].at[pl.ds(2 * chip[0] + chip[1], 1)], dst_ref=outs[k].at[pl.ds(j, 1)],
                                             send_sem=send_sems.at[j, k], recv_sem=recv_sems.at[j, k],
                                             device_id=(*chip, c), device_id_type=MESH)
                for j, chip in enumerate(chips) for k in range(n)]

    def start(*refs):
        for cp in copies(*refs):
            cp.start()

    def wait(*refs):
        for cp in copies(*refs):
            cp.wait()

    return _Rider(list(sums), [jax.ShapeDtypeStruct((3,) + sm.shape[1:], BF16) for sm in sums], (3, n), start, wait)


def _total_sums(owns, received, name):
    n = len(owns)

    def body(*refs):
        for o_ref, r_ref, t_ref in zip(refs[:n], refs[n:2 * n], refs[2 * n:]):
            total = o_ref[0]
            for j in range(3):
                total = total + r_ref[j, 0].astype(F32)
            t_ref[0] = total

    return pl.pallas_call(body, name=name, out_shape=[jax.ShapeDtypeStruct(o.shape, F32) for o in owns],
                          compiler_params=_params())(*owns, *received)


def _swap_halves(halves):
    n = len(halves)

    def body(*refs):
        ins, outs = refs[:n], refs[n:2 * n]
        send_sems, recv_sems = refs[2 * n:]
        x, y, c, _ = _place()
        copies = [pltpu.make_async_remote_copy(src_ref=ins[k], dst_ref=outs[k], send_sem=send_sems.at[k],
                                               recv_sem=recv_sems.at[k], device_id=(x, y, 1 - c), device_id_type=MESH)
                  for k in range(n)]
        for cp in copies:
            cp.start()
        for cp in copies:
            cp.wait()

    any_spec = pl.BlockSpec(memory_space=pl.ANY)
    return pl.pallas_call(
        body, name="swap_halves", in_specs=[any_spec] * n, out_specs=[any_spec] * n,
        out_shape=[jax.ShapeDtypeStruct(h.shape, F32) for h in halves],
        scratch_shapes=[pltpu.SemaphoreType.DMA((n,)), pltpu.SemaphoreType.DMA((n,))])(*halves)


N_DEV = 8


def _gather_small(block):
    m_per, width = block.shape

    def body(x_ref, out_ref, send_sems, recv_sems, local_sem):
        x, y, c, chips = _place()
        me, sibling = (x, y, c), (x, y, 1 - c)

        def rows(px, py, pc):
            return out_ref.at[pl.ds((4 * px + 2 * py + pc) * m_per, m_per), :]

        def copy(k, blk, to, src=None):
            return pltpu.make_async_remote_copy(src_ref=rows(*blk) if src is None else src, dst_ref=rows(*blk),
                                                send_sem=send_sems.at[k], recv_sem=recv_sems.at[k], device_id=to, device_id_type=MESH)

        mine = pltpu.make_async_copy(x_ref, rows(*me), local_sem)
        mine.start()
        first = [copy(0, me, sibling, src=x_ref)] + [copy(1 + j, me, (*chip, c), src=x_ref) for j, chip in enumerate(chips)]
        for cp in first:
            cp.start()
        passed = [copy(4 + j, (*chip, c), sibling) for j, chip in enumerate(chips)]
        for j, chip in enumerate(chips):
            copy(1 + j, (*chip, c), me).wait_recv()
            passed[j].start()
        copy(0, sibling, me).wait_recv()
        for j, chip in enumerate(chips):
            copy(4 + j, (*chip, 1 - c), me).wait_recv()
        for cp in first + passed:
            cp.wait_send()
        mine.wait()

    vmem = pl.BlockSpec(memory_space=pltpu.VMEM)
    return pl.pallas_call(body, name="gather_small", out_shape=jax.ShapeDtypeStruct((N_DEV * m_per, width), F32),
                          in_specs=[vmem], out_specs=vmem,
                          scratch_shapes=[pltpu.SemaphoreType.DMA((7,)), pltpu.SemaphoreType.DMA((7,)),
                                          pltpu.SemaphoreType.DMA])(block)


def _adamw_math(w, g, m, v):
    m = ADAM_B1 * m + (1.0 - ADAM_B1) * g
    v = ADAM_B2 * v + (1.0 - ADAM_B2) * (g * g)
    m_hat = m / (1.0 - ADAM_B1 ** ADAM_STEP)
    v_hat = v / (1.0 - ADAM_B2 ** ADAM_STEP)
    delta = -ADAM_LR * (m_hat / (jnp.sqrt(v_hat) + ADAM_EPS) + ADAM_WD * w)
    return delta, m, v


def _adamw(w, mine, siblings, place, m, v, transposed, name):
    if transposed:
        def body(place_ref, w_ref, mine_ref, sib_ref, m_ref, v_ref, go_ref, d_ref, mo_ref, vo_ref):
            first = place_ref[1] == 0
            g = jnp.concatenate([jnp.where(first, mine_ref[0], sib_ref[0]), jnp.where(first, sib_ref[0], mine_ref[0])], axis=0).T
            go_ref[...] = g
            d_ref[...], mo_ref[...], vo_ref[...] = _adamw_math(w_ref[...], g, m_ref[...], v_ref[...])

        vmem = pl.BlockSpec(memory_space=pltpu.VMEM)
        return pl.pallas_call(body, name=name, in_specs=[pl.BlockSpec(memory_space=pltpu.SMEM)] + [vmem] * 5, out_specs=[vmem] * 4,
                              out_shape=[jax.ShapeDtypeStruct(w.shape, F32)] * 4, compiler_params=_params())(
                                  place, w, mine, siblings, m, v)

    rh, width = mine.shape[1:]

    def body(place_ref, w_ref, mine_ref, sib_ref, m_ref, v_ref, go_ref, d_ref, mo_ref, vo_ref):
        g = jnp.where(pl.program_id(0) == place_ref[1], mine_ref[0], sib_ref[0])
        go_ref[...] = g
        d_ref[...], mo_ref[...], vo_ref[...] = _adamw_math(w_ref[...], g, m_ref[...], v_ref[...])

    half = pl.BlockSpec((rh, width), lambda h, place: (h, 0))
    whole = pl.BlockSpec((1, rh, width), lambda h, place: (0, 0, 0))
    return pl.pallas_call(
        body, name=name,
        grid_spec=pltpu.PrefetchScalarGridSpec(num_scalar_prefetch=1, grid=(2,), in_specs=[half, whole, whole, half, half],
                                               out_specs=[half] * 4),
        out_shape=[jax.ShapeDtypeStruct(w.shape, F32)] * 4,
        compiler_params=_params(dimension_semantics=("arbitrary",)))(place, w, mine, siblings, m, v)


def _adamw_small(gathered, w, m, v, name):
    def body(ga_ref, w_ref, m_ref, v_ref, go_ref, d_ref, mo_ref, vo_ref):
        g = ga_ref[0]
        for dev in range(1, N_DEV):
            g = g + ga_ref[dev]
        go_ref[...] = g
        d_ref[...], mo_ref[...], vo_ref[...] = _adamw_math(w_ref[...], g, m_ref[...], v_ref[...])

    return pl.pallas_call(body, name=name, out_shape=[jax.ShapeDtypeStruct(w.shape, F32)] * 4,
                          compiler_params=_params())(gathered, w, m, v)


class _Exchange:
    FIRST = ("w1_gate", "w1_up", "w1_down")
    HOSTS = {"ffn2_wgrad_gate": (("w2_down",), ()), "ffn2_wgrad_up": (("w2_gate",), ("w2_down",)),
             "in_bwd": (("w2_up",), ("w2_gate",)), "wgrad_in": ((), ("w2_up",)),
             "ffn1_wgrad_down": ((), ("w_in",)), "ffn1_wgrad_gate": (("w1_down",), ()), "ffn1_wgrad_up": ((), ("w1_down", "w1_gate")),
             "wgrad_out": ((), ("w1_up",))}
    ALONE = ("w_in", "w1_gate", "w1_up", "w_out")

    def __init__(self, bufs, place):
        self.bufs, self.place = bufs, place
        self.later = [k for k in SEGMENTS if k not in self.FIRST]
        self.split, self.own, self.to_send, self.received = {}, {}, {}, {}

    def first_weights(self):
        return dict(zip(self.FIRST, _gather_weights([self.bufs[k] for k in self.FIRST])))

    def riders(self, host):
        if host == "ffn1_fwd":
            return [_gather_rider([self.bufs[k] for k in self.later])]
        halves, sums = self.HOSTS.get(host, ((), ()))
        return ([_sibling_rider([self.split[k] for k in halves])] if halves else []) + (
            [_scatter_rider([self.to_send[k] for k in sums])] if sums else [])

    def landed(self, host, results):
        if host == "ffn1_fwd":
            self.rest = dict(zip(self.later, _forward_halves(results[0], "gather_rest_forward")))
            return
        halves, sums = self.HOSTS.get(host, ((), ()))
        if halves:
            self._chip_sums(halves, results[0])
        if sums:
            self.received.update(zip(sums, results[-1]))

    def rest_weights(self):
        return self.rest

    def gradient(self, name, grad):
        self.split[name] = grad.reshape(N_CHIPS, 2, grad.shape[0] // (2 * N_CHIPS), grad.shape[1])
        if name in self.ALONE:
            self._chip_sums([name], _alone(_sibling_rider([self.split[name]]), f"reduce_sibling_{name}"))

    def _chip_sums(self, names, from_sibling):
        for k, fs in zip(names, from_sibling):
            self.own[k], self.to_send[k] = _chip_sum(self.split[k], fs, self.place, f"chip_sum_{k}")

    def summed_halves(self):
        late = [k for k in SEGMENTS if k not in self.received]
        self.received.update(zip(late, _alone(_scatter_rider([self.to_send[k] for k in late]), "reduce_chips_last")))
        return _total_sums([self.own[k] for k in SEGMENTS], [self.received[k] for k in SEGMENTS], "total_sums")


SMALL = ("g_ffn1_pre", "g_ffn1_post", "g_mix_pre", "w_pool_lin", "pool_scale", "g_mix_post", "g_ffn2_pre", "g_ffn2_post")
WEIGHTS = ("g_ffn1_pre", "w1_gate", "w1_up", "w1_down", "g_ffn1_post", "g_mix_pre", "w_in", "w_pool_lin", "pool_scale", "w_out",
           "g_mix_post", "g_ffn2_pre", "w2_gate", "w2_up", "w2_down", "g_ffn2_post")
LANES = 128


def _pack_small(tree, extra=0.0):
    flat = jnp.concatenate([tree[k].reshape(-1) for k in SMALL] + [jnp.reshape(extra, (1,)).astype(F32)])
    rows = -(-flat.shape[0] // (8 * LANES)) * 8
    return jnp.pad(flat, (0, rows * LANES - flat.shape[0])).reshape(rows, LANES)


def _unpack_small(packed, like):
    flat, out, at = packed.reshape(-1), {}, 0
    for k in SMALL:
        size = math.prod(like[k].shape)
        out[k] = flat[at:at + size].reshape(like[k].shape)
        at += size
    return out


def kernel(x, g_ffn1_pre, w1_gate, w1_up, w1_down, g_ffn1_post, g_mix_pre, w_in, w_pool_lin, pool_scale, w_out, g_mix_post, g_ffn2_pre, w2_gate, w2_up, w2_down, g_ffn2_post, loss_target, m_g_ffn1_pre, m_w1_gate, m_w1_up, m_w1_down, m_g_ffn1_post, m_g_mix_pre, m_w_in, m_w_pool_lin, m_pool_scale, m_w_out, m_g_mix_post, m_g_ffn2_pre, m_w2_gate, m_w2_up, m_w2_down, m_g_ffn2_post, v_g_ffn1_pre, v_w1_gate, v_w1_up, v_w1_down, v_g_ffn1_post, v_g_mix_pre, v_w_in, v_w_pool_lin, v_pool_scale, v_w_out, v_g_mix_post, v_g_ffn2_pre, v_w2_gate, v_w2_up, v_w2_down, v_g_ffn2_post):
    given = dict(locals())
    w = {k: given[k] for k in WEIGHTS}
    m = {k: given["m_" + k] for k in WEIGHTS}
    v = {k: given["v_" + k] for k in WEIGHTS}
    small = {k: (w[k][0] if k == "w_pool_lin" else w[k].reshape(1, -1)) for k in SMALL}

    place = jnp.stack([2 * lax.axis_index("x") + lax.axis_index("y"), lax.axis_index("c")]).astype(jnp.int32)
    def as_rows(a, k):
        return jnp.swapaxes(a, 1, 2)[0] if k in ROWS_OUTSIDE else a[0]

    def as_given(a, k):
        return jnp.swapaxes(a[None], 1, 2) if k in ROWS_OUTSIDE else a[None]

    in_kernel = [k for k in TRANSPOSED if k not in ROWS_OUTSIDE]
    bufs = {}
    for tag, names in (("first", _Exchange.FIRST), ("rest", [k for k in SEGMENTS if k not in _Exchange.FIRST])):
        bufs.update(zip(names, _cast_shards([as_rows(w[k], k) for k in names], [k in in_kernel for k in names], place, f"cast_{tag}")))
    exchange = _Exchange(bufs, place)
    loss_part, grad_x, small_grads = _local_step(x[0], loss_target[0], small, exchange)

    halves = exchange.summed_halves()
    from_sibling = _swap_halves(halves)

    out_grad, out_delta, out_m, out_v = {}, {}, {}, {}
    for k, mine, sib in zip(SEGMENTS, halves, from_sibling):
        out_grad[k], out_delta[k], out_m[k], out_v[k] = (
            as_given(a, k) for a in _adamw(as_rows(w[k], k), mine, sib, place, as_rows(m[k], k), as_rows(v[k], k),
                                           k in in_kernel, f"adamw_{k}"))

    small_grads["w_pool_lin"] = small_grads["w_pool_lin"][None]
    packed = _pack_small(small_grads, loss_part)
    gathered = _gather_small(packed).reshape(N_DEV, *packed.shape)
    like = {k: w[k] for k in SMALL}
    results = _adamw_small(gathered, _pack_small(like), _pack_small({k: m[k] for k in SMALL}),
                           _pack_small({k: v[k] for k in SMALL}), "adamw_small")
    for tree, res in zip((out_grad, out_delta, out_m, out_v), results):
        tree.update(_unpack_small(res, like))
    loss = results[0].reshape(-1)[sum(math.prod(like[k].shape) for k in SMALL)]

    return (loss, grad_x[None], *[out_grad[k] for k in WEIGHTS], *[out_delta[k] for k in WEIGHTS],
            *[out_m[k] for k in WEIGHTS], *[out_v[k] for k in WEIGHTS])
```

```python
import math
import typing

import numpy as np
import jax
import jax.numpy as jnp
from jax import lax
from jax.experimental import pallas as pl
from jax.experimental.pallas import tpu as pltpu

F32 = jnp.float32
BF16 = jnp.bfloat16
MESH = pl.DeviceIdType.MESH

RMS_EPS = 1e-6
HEAD_DIM = 64
POOL_HALF_WINDOWS = (1, 2, 4, 8)
POOL_DIM = 256
GROUP_DIM = 256
DILATIONS = (1, 4, 16)
N_SIDE = 64
N_ATTN_HEADS = 12
ADAM_LR, ADAM_B1, ADAM_B2, ADAM_EPS, ADAM_WD, ADAM_STEP = 0.001, 0.9, 0.999, 1e-08, 0.01, 10

N_CHIPS = 4
V7X_VMEM_LIMIT = 60 * 1024 * 1024

_NT = (((1,), (1,)), ((), ()))
_TN = (((0,), (0,)), ((), ()))


def _dot(a, b):
    return jnp.dot(a, b, preferred_element_type=F32)


def _dot_nt(a, b):
    return lax.dot_general(a, b, _NT, preferred_element_type=F32)


def _dot_tn(a, b):
    return lax.dot_general(a, b, _TN, preferred_element_type=F32)


def _params(**kw):
    return pltpu.CompilerParams(vmem_limit_bytes=V7X_VMEM_LIMIT, **kw)


def _rows(tm, width):
    return pl.BlockSpec((tm, width), lambda i: (i, 0))


def _resident(shape):
    return pl.BlockSpec(shape, lambda i: (0,) * len(shape), pipeline_mode=pl.Buffered(1))


def _const(shape):
    return pl.BlockSpec(shape, lambda i: (0,) * len(shape))


def _inv_rms(x):
    return lax.rsqrt(jnp.mean(x * x, axis=-1, keepdims=True) + RMS_EPS)


def _rms_bwd(x, inv, g, dy):
    n = x * inv
    dn = dy * g
    dx = inv * (dn - n * jnp.mean(dn * n, axis=-1, keepdims=True))
    return dx, jnp.sum(dy * n, axis=0, keepdims=True)


def _accumulate(ref, value):
    @pl.when(pl.program_id(0) == 0)
    def _():
        ref[...] = jnp.zeros_like(ref)

    ref[...] += value


class _Rider(typing.NamedTuple):
    operands: list
    landing: typing.Optional[list]
    sems: tuple
    start: typing.Callable
    wait: typing.Callable


def _hosted_call(body, riders, *, name, steps, in_specs, out_specs, out_shape, args, scratch_shapes=()):
    params = _params(dimension_semantics=("arbitrary",))
    riders = list(riders or [])
    if not riders:
        res = pl.pallas_call(body, name=name, grid=(steps,), in_specs=in_specs, out_specs=out_specs, out_shape=out_shape,
                             scratch_shapes=list(scratch_shapes), compiler_params=params)(*args)
        return list(res), []
    n_in, n_out, n_scratch = len(in_specs), len(out_specs), len(scratch_shapes)
    operands, landing, aliases, spans = [], [], {}, []
    for rd in riders:
        lands = rd.landing if rd.landing is not None else [jax.ShapeDtypeStruct(a.shape, a.dtype) for a in rd.operands]
        if rd.landing is None:
            aliases.update({n_in + len(operands) + i: n_out + len(landing) + i for i in range(len(lands))})
        spans.append((len(operands), len(rd.operands), len(landing), len(lands)))
        operands += rd.operands
        landing += lands
    outs_at = n_in + len(operands)
    scratch_at = outs_at + n_out + len(landing)

    def riding(*refs):
        def each(action):
            for i, (rd, (in_at, n_ops, out_at, n_lands)) in enumerate(zip(riders, spans)):
                sems = refs[scratch_at + n_scratch + 2 * i:scratch_at + n_scratch + 2 * i + 2]
                getattr(rd, action)(refs[n_in + in_at:n_in + in_at + n_ops],
                                    refs[outs_at + n_out + out_at:outs_at + n_out + out_at + n_lands], *sems)

        @pl.when(pl.program_id(0) == 0)
        def _():
            each("start")

        body(*refs[:n_in], *refs[outs_at:outs_at + n_out], *refs[scratch_at:scratch_at + n_scratch])

        @pl.when(pl.program_id(0) == steps - 1)
        def _():
            each("wait")

    any_spec = pl.BlockSpec(memory_space=pl.ANY)
    res = pl.pallas_call(
        riding, name=name, grid=(steps,), in_specs=list(in_specs) + [any_spec] * len(operands),
        out_specs=list(out_specs) + [any_spec] * len(landing), out_shape=list(out_shape) + landing,
        scratch_shapes=list(scratch_shapes) + [pltpu.SemaphoreType.DMA(rd.sems) for rd in riders for _ in range(2)],
        input_output_aliases=aliases, compiler_params=params)(*args, *operands)
    return list(res[:n_out]), [list(res[n_out + out_at:n_out + out_at + n_lands]) for _, _, out_at, n_lands in spans]


_SUB_TILE = 256


def _sub_tiles(tm):
    return [pl.ds(r, _SUB_TILE) for r in range(0, tm, _SUB_TILE)]


def _ffn_fwd(x, g_pre, wg_t, wu_t, wd, g_post, target, name, riders=None, tm=512):
    s, d = x.shape
    ff = wd.shape[0]
    with_loss = target is not None

    def body(*refs):
        if with_loss:
            x_ref, gpre_ref, wg_ref, wu_ref, wd_ref, gpost_ref, t_ref, xo_ref, a_ref, b_ref, f_ref, loss_ref = refs
        else:
            x_ref, gpre_ref, wg_ref, wu_ref, wd_ref, gpost_ref, xo_ref, a_ref, b_ref, f_ref = refs
        loss = 0.0
        for rows in _sub_tiles(tm):
            xv = x_ref[rows, :]
            hb = (xv * _inv_rms(xv) * gpre_ref[...]).astype(BF16)
            a = _dot_nt(hb, wg_ref[...])
            b = _dot_nt(hb, wu_ref[...])
            hh = (a * jax.nn.sigmoid(a)) * b
            f = _dot(hh.astype(BF16), wd_ref[...])
            xo = xv + 0.5 * (f * _inv_rms(f) * gpost_ref[...])
            a_ref[rows, :] = a.astype(BF16)
            b_ref[rows, :] = b.astype(BF16)
            f_ref[rows, :] = f
            if with_loss:
                e = xo - t_ref[rows, :]
                xo_ref[rows, :] = e * (1.0 / d)
                loss = loss + 0.5 * jnp.sum(jnp.mean(e * e, axis=-1, keepdims=True))
            else:
                xo_ref[rows, :] = xo
        if with_loss:
            _accumulate(loss_ref, loss)

    in_specs = [_rows(tm, d), _const((1, d)), _resident((ff, d)), _resident((ff, d)), _resident((ff, d)), _const((1, d))]
    args = [x, g_pre, wg_t, wu_t, wd, g_post]
    out_shape = [jax.ShapeDtypeStruct((s, d), F32), jax.ShapeDtypeStruct((s, ff), BF16),
                 jax.ShapeDtypeStruct((s, ff), BF16), jax.ShapeDtypeStruct((s, d), F32)]
    out_specs = [_rows(tm, d), _rows(tm, ff), _rows(tm, ff), _rows(tm, d)]
    if with_loss:
        in_specs.append(_rows(tm, d))
        args.append(target)
        out_shape.append(jax.ShapeDtypeStruct((8, 128), F32))
        out_specs.append(_const((8, 128)))
    return _hosted_call(body, riders, name=name, steps=s // tm, in_specs=in_specs, out_specs=out_specs, out_shape=out_shape, args=args)


def _ffn_bwd(dxo, x, f, a, b, g_pre, g_post, wg_t, wu_t, wd, name, riders=None, tm=256):
    s, d = x.shape
    ff = wd.shape[0]

    def body(dxo_ref, x_ref, f_ref, a_ref, b_ref, gpre_ref, gpost_ref, wg_ref, wu_ref, wd_ref,
             dx_ref, hh_ref, da_ref, db_ref, df_ref, h_ref, dgpre_ref, dgpost_ref):
        dgpre_sum = dgpost_sum = 0.0
        for rows in _sub_tiles(tm):
            dxo_v = dxo_ref[rows, :]
            fv = f_ref[rows, :]
            df, dgpost = _rms_bwd(fv, _inv_rms(fv), gpost_ref[...], 0.5 * dxo_v)
            dfb = df.astype(BF16)
            dhh = _dot_nt(dfb, wd_ref[...])
            av = a_ref[rows, :].astype(F32)
            bv = b_ref[rows, :].astype(F32)
            sig = jax.nn.sigmoid(av)
            sa = av * sig
            da = (dhh * bv * (sig * (1.0 + av * (1.0 - sig)))).astype(BF16)
            db = (dhh * sa).astype(BF16)
            dh = _dot(da, wg_ref[...]) + _dot(db, wu_ref[...])
            xv = x_ref[rows, :]
            inv = _inv_rms(xv)
            dxn, dgpre = _rms_bwd(xv, inv, gpre_ref[...], dh)
            dx_ref[rows, :] = dxo_v + dxn
            hh_ref[rows, :] = (sa * bv).astype(BF16)
            da_ref[rows, :] = da
            db_ref[rows, :] = db
            df_ref[rows, :] = dfb
            h_ref[rows, :] = (xv * inv * gpre_ref[...]).astype(BF16)
            dgpre_sum, dgpost_sum = dgpre_sum + dgpre, dgpost_sum + dgpost
        _accumulate(dgpre_ref, dgpre_sum)
        _accumulate(dgpost_ref, dgpost_sum)

    return _hosted_call(
        body, riders, name=name, steps=s // tm,
        in_specs=[_rows(tm, d), _rows(tm, d), _rows(tm, d), _rows(tm, ff), _rows(tm, ff), _const((1, d)), _const((1, d)),
                  _resident((ff, d)), _resident((ff, d)), _resident((ff, d))],
        out_specs=[_rows(tm, d), _rows(tm, ff), _rows(tm, ff), _rows(tm, ff), _rows(tm, d), _rows(tm, d),
                   _const((1, d)), _const((1, d))],
        out_shape=[jax.ShapeDtypeStruct((s, d), F32), jax.ShapeDtypeStruct((s, ff), BF16), jax.ShapeDtypeStruct((s, ff), BF16),
                   jax.ShapeDtypeStruct((s, ff), BF16), jax.ShapeDtypeStruct((s, d), BF16), jax.ShapeDtypeStruct((s, d), BF16),
                   jax.ShapeDtypeStruct((1, d), F32), jax.ShapeDtypeStruct((1, d), F32)],
        args=[dxo, x, f, a, b, g_pre, g_post, wg_t, wu_t, wd])


def _wgrad(lhs, rhs, name, riders=None, rt=256):
    s, r = lhs.shape
    c = rhs.shape[1]

    def body(l_ref, r_ref, o_ref):
        o_ref[...] = _dot_tn(l_ref[...], r_ref[...])

    (out,), riding = _hosted_call(
        body, riders, name=name, steps=pl.cdiv(r, rt), in_specs=[pl.BlockSpec((s, rt), lambda i: (0, i)), _resident((s, c))],
        out_specs=[pl.BlockSpec((rt, c), lambda i: (i, 0))], out_shape=[jax.ShapeDtypeStruct((r, c), F32)], args=[lhs, rhs])
    return out, riding


def _attn_dtype(dilation):
    return BF16 if dilation == 1 else F32


def _in_fwd(x, g, w_in_t, name, riders=None, tm=512):
    s, d = x.shape
    d_in = w_in_t.shape[0]
    n_groups = len(DILATIONS)
    dtypes = [_attn_dtype(dil) for dil in DILATIONS] * 3

    def body(x_ref, g_ref, w_ref, h_ref, u_ref, *part_refs):
        xv = x_ref[...]
        hb = (xv * _inv_rms(xv) * g_ref[...]).astype(BF16)
        h_ref[...] = hb
        z = _dot_nt(hb, w_ref[...])
        u_ref[...] = z[:, :POOL_DIM]
        for j, ref in enumerate(part_refs):
            part = z[:, POOL_DIM + GROUP_DIM * j:POOL_DIM + GROUP_DIM * (j + 1)]
            ref[...] = (part * _SCORE_SCALE if j < n_groups else part).astype(ref.dtype)

    return _hosted_call(
        body, riders, name=name, steps=s // tm, in_specs=[_rows(tm, d), _const((1, d)), _resident((d_in, d))],
        out_specs=[_rows(tm, d), _rows(tm, POOL_DIM)] + [_rows(tm, GROUP_DIM)] * len(dtypes),
        out_shape=[jax.ShapeDtypeStruct((s, d), BF16), jax.ShapeDtypeStruct((s, POOL_DIM), F32)]
        + [jax.ShapeDtypeStruct((s, GROUP_DIM), dt) for dt in dtypes],
        args=[x, g, w_in_t])


def _in_bwd(du, dparts, x, dxo, g, w_in_t, name, riders=None, tm=512):
    s, d = x.shape
    d_in = w_in_t.shape[0]
    n_parts = len(dparts)

    def body(du_ref, *refs):
        part_refs = refs[:n_parts]
        x_ref, dxo_ref, g_ref, w_ref, dx_ref, dz_ref, dg_ref = refs[n_parts:]
        dz = jnp.concatenate([r[...].astype(BF16) for r in (du_ref,) + part_refs], axis=1)
        dz_ref[...] = dz
        dh = _dot(dz, w_ref[...])
        xv = x_ref[...]
        dxn, dg = _rms_bwd(xv, _inv_rms(xv), g_ref[...], dh)
        dx_ref[...] = dxo_ref[...] + dxn
        _accumulate(dg_ref, dg)

    return _hosted_call(
        body, riders, name=name, steps=s // tm,
        in_specs=[_rows(tm, POOL_DIM)] + [_rows(tm, GROUP_DIM)] * n_parts + [_rows(tm, d), _rows(tm, d), _const((1, d)),
                                                                             _resident((d_in, d))],
        out_specs=[_rows(tm, d), _rows(tm, d_in), _const((1, d))],
        out_shape=[jax.ShapeDtypeStruct((s, d), F32), jax.ShapeDtypeStruct((s, d_in), BF16), jax.ShapeDtypeStruct((1, d), F32)],
        args=[du, *dparts, x, dxo, g, w_in_t])


_POOL_HALO = 8


def _pool_chain(v, first_shift):
    n = v.shape[0]
    p2 = v + pltpu.roll(v, first_shift, 0)
    p4 = pltpu.roll(p2, 1, 0) + pltpu.roll(p2, n - 1, 0)
    p8 = pltpu.roll(p4, 2, 0) + pltpu.roll(p4, n - 2, 0)
    p16 = pltpu.roll(p8, 4, 0) + pltpu.roll(p8, n - 4, 0)
    group = lax.broadcasted_iota(jnp.int32, v.shape, 1) // HEAD_DIM
    return jnp.where(group == 0, p2, jnp.where(group == 1, p4, jnp.where(group == 2, p8, p16)))


def _pool_count(t0, rows, s):
    t = t0 + lax.broadcasted_iota(jnp.int32, (rows, POOL_DIM), 0)
    group = lax.broadcasted_iota(jnp.int32, (rows, POOL_DIM), 1) // HEAD_DIM
    half = jnp.where(group == 0, 1, jnp.where(group == 1, 2, jnp.where(group == 2, 4, 8)))
    cnt = jnp.minimum(t + half, s) - jnp.maximum(t - half, 0)
    return jnp.maximum(cnt, 1).astype(F32)


def _pad_rows(ref, pad_ref, s):
    zeros = jnp.zeros((_POOL_HALO, pad_ref.shape[1]), pad_ref.dtype)
    pad_ref[pl.ds(0, _POOL_HALO), :] = zeros
    pad_ref[pl.ds(_POOL_HALO + s, _POOL_HALO), :] = zeros
    pad_ref[pl.ds(_POOL_HALO, s), :] = ref[...]


def _pool_fwd(u, w_bd, scale, name, tm=512):
    s = u.shape[0]
    ext = tm + 2 * _POOL_HALO

    def body(u_ref, w_ref, sc_ref, o_ref, upad):
        _pad_rows(u_ref, upad, s)

        def tile(i, carry):
            t0 = pl.multiple_of(i * tm, tm)
            uv = upad[pl.ds(t0, ext), :]
            win = _pool_chain(uv, 1)[_POOL_HALO:_POOL_HALO + tm]
            y = win / _pool_count(t0, tm, s) - uv[_POOL_HALO:_POOL_HALO + tm]
            o_ref[pl.ds(t0, tm), :] = (_dot(y.astype(BF16), w_ref[...]) * sc_ref[...]).astype(BF16)
            return carry

        lax.fori_loop(0, s // tm, tile, 0)

    return pl.pallas_call(body, name=name, out_shape=jax.ShapeDtypeStruct((s, POOL_DIM), BF16),
                          scratch_shapes=[pltpu.VMEM((s + 2 * _POOL_HALO, POOL_DIM), F32)],
                          compiler_params=_params())(u, w_bd, scale)


def _pool_bwd(u, da, w_bd, scale, name, tm=512):
    s = u.shape[0]
    ext = tm + 2 * _POOL_HALO

    def body(u_ref, da_ref, w_ref, sc_ref, du_ref, dw_ref, dsc_ref, upad, dapad):
        _pad_rows(u_ref, upad, s)
        _pad_rows(da_ref, dapad, s)
        dw_ref[...] = jnp.zeros_like(dw_ref)
        dsc_ref[...] = jnp.zeros_like(dsc_ref)

        def tile(i, carry):
            t0 = pl.multiple_of(i * tm, tm)
            uv = upad[pl.ds(t0, ext), :]
            dav = dapad[pl.ds(t0, ext), :]
            win = _pool_chain(uv, 1)[_POOL_HALO:_POOL_HALO + tm]
            yb = (win / _pool_count(t0, tm, s) - uv[_POOL_HALO:_POOL_HALO + tm]).astype(BF16)
            yl = _dot(yb, w_ref[...])
            da_c = dav[_POOL_HALO:_POOL_HALO + tm]
            dsc_ref[...] += jnp.sum(da_c * yl, axis=0, keepdims=True)
            dyl = (dav * sc_ref[...]).astype(BF16)
            dw_ref[...] += _dot_tn(yb, dyl[_POOL_HALO:_POOL_HALO + tm])
            dy = _dot_nt(dyl, w_ref[...])
            dyc = dy / _pool_count(t0 - _POOL_HALO, ext, s)
            du_ref[pl.ds(t0, tm), :] = (_pool_chain(dyc, ext - 1) - dy)[_POOL_HALO:_POOL_HALO + tm]
            return carry

        lax.fori_loop(0, s // tm, tile, 0)

    pool_cols = pl.BlockSpec((s, POOL_DIM), lambda i: (0, 0), pipeline_mode=pl.Buffered(1))
    return pl.pallas_call(
        body, name=name, grid=(1,),
        in_specs=[pool_cols, pool_cols, _const((POOL_DIM, POOL_DIM)), _const((1, POOL_DIM))],
        out_specs=[_const((s, POOL_DIM)), _const((POOL_DIM, POOL_DIM)), _const((1, POOL_DIM))],
        out_shape=[jax.ShapeDtypeStruct((s, POOL_DIM), F32), jax.ShapeDtypeStruct((POOL_DIM, POOL_DIM), F32),
                   jax.ShapeDtypeStruct((1, POOL_DIM), F32)],
        scratch_shapes=[pltpu.VMEM((s + 2 * _POOL_HALO, POOL_DIM), F32), pltpu.VMEM((s + 2 * _POOL_HALO, POOL_DIM), F32)],
        compiler_params=_params(dimension_semantics=("arbitrary",)))(u, da, w_bd, scale)


_BQ = 128
_KW = _BQ + 2 * N_SIDE
_PAIR = 2 * HEAD_DIM
_NEG = -1e30
_ATTN_UNROLL = 8
_SCORE_SCALE = HEAD_DIM ** -0.5


def _stack_heads(x):
    lane_head = lax.broadcasted_iota(jnp.int32, x.shape, 1) // HEAD_DIM
    zero = jnp.zeros_like(x)
    return jnp.concatenate([jnp.where(lane_head == 0, x, zero), jnp.where(lane_head == 1, x, zero)], axis=0)


def _unstack_heads(x):
    lane_head = lax.broadcasted_iota(jnp.int32, (_BQ, _PAIR), 1) // HEAD_DIM
    return jnp.where(lane_head == 0, x[:_BQ], x[_BQ:])


def _stack_cols(x):
    return jnp.concatenate([x[:, 0:1], x[:, HEAD_DIM:HEAD_DIM + 1]], axis=0)


def _fill_bias(bias_ref, slopes_ref, dilation):
    row = lax.broadcasted_iota(jnp.int32, (2 * _BQ, _KW), 0)
    col = lax.broadcasted_iota(jnp.int32, (2 * _BQ, _KW), 1)
    pair = 2 * pl.program_id(0)
    slope = jnp.where(row < _BQ, slopes_ref[pair], slopes_ref[pair + 1]) * float(dilation)

    @pl.when(pl.program_id(1) == 0)
    def _():
        for j in range(3):
            dist = jnp.abs(col - (row & (_BQ - 1)) - j * N_SIDE)
            bias_ref[j] = jnp.where(dist <= N_SIDE, -slope * dist.astype(F32), _NEG)


def _block_window(i, n_blocks, length):
    q0 = pl.multiple_of(i * _BQ, _BQ)
    ws = pl.multiple_of(jnp.clip(q0 - N_SIDE, 0, length - _KW), N_SIDE)
    return q0, ws, jnp.where(i == 0, 0, jnp.where(i == n_blocks - 1, 2, 1))


_FREE_STRIDE = 4


def _residue_views(dilation, seq, ins, outs, tmps):
    step = pl.program_id(1)
    if dilation <= _FREE_STRIDE:
        def rows(start, count):
            return pl.ds(start, count) if dilation == 1 else pl.ds(start * dilation + step, count, stride=dilation)

        return ins, outs, rows, lambda: None
    inner = dilation // _FREE_STRIDE
    assert inner <= _FREE_STRIDE and len(tmps) == len(ins) + len(outs)
    first, second = step // inner, step % inner
    coarse = pl.ds(first, seq // _FREE_STRIDE, stride=_FREE_STRIDE)
    in_tmps, out_tmps = tmps[:len(ins)], tmps[len(ins):]

    @pl.when(second == 0)
    def _():
        for ref, tmp in zip(ins, in_tmps):
            tmp[...] = ref[coarse, :]

    def flush():
        @pl.when(second == inner - 1)
        def _():
            for ref, tmp in zip(outs, out_tmps):
                ref[coarse, :] = tmp[...]

    return in_tmps, out_tmps, lambda start, count: pl.ds(start * inner + second, count, stride=inner), flush


def _attn_call(body, name, dilation, seq, n_in, out_dtypes, scratch, buffers):
    col = pl.BlockSpec((seq, _PAIR), lambda c, r: (0, c), pipeline_mode=pl.Buffered(buffers))
    tmps = [pltpu.VMEM((seq // _FREE_STRIDE, _PAIR), F32)] * (n_in + len(out_dtypes) if dilation > _FREE_STRIDE else 0)
    return pl.pallas_call(
        body, name=name, grid=(GROUP_DIM // _PAIR, dilation),
        in_specs=[pl.BlockSpec(memory_space=pltpu.SMEM)] + [col] * n_in, out_specs=[col] * len(out_dtypes),
        out_shape=[jax.ShapeDtypeStruct((seq, GROUP_DIM), dt) for dt in out_dtypes], scratch_shapes=scratch + tmps,
        compiler_params=_params(dimension_semantics=("arbitrary", "arbitrary")))


def _staged(dilation, length, rows, sources, scratch):
    if dilation == 1:
        return sources
    for src, dst in zip(sources, scratch):
        dst[...] = src[rows(0, length), :].astype(BF16)
    return scratch


def _attn_fwd(q, k, v, slopes, dilation, name):
    seq = q.shape[0]
    length = seq // dilation
    n_blocks = length // _BQ
    n_stage = 0 if dilation == 1 else 3

    def body(sl_ref, q_ref, k_ref, v_ref, o_ref, lse_ref, *scratch):
        bias_ref, tmps = scratch[n_stage], scratch[n_stage + 1:]
        (q_in, k_in, v_in), (o_out, lse_out), rows, flush = _residue_views(dilation, seq, (q_ref, k_ref, v_ref), (o_ref, lse_ref), tmps)
        qs, ks, vs = _staged(dilation, length, rows, (q_in, k_in, v_in), scratch[:n_stage])
        _fill_bias(bias_ref, sl_ref, dilation)

        def block(i, carry):
            q0, ws, which = _block_window(i, n_blocks, length)
            kw = ks[pl.ds(ws, _KW), :]
            vw = vs[pl.ds(ws, _KW), :]
            sc = _dot_nt(_stack_heads(qs[pl.ds(q0, _BQ), :]), kw) + bias_ref[which]
            m = jnp.max(sc, axis=-1, keepdims=True)
            p = jnp.exp(sc - m)
            den = jnp.sum(p, axis=-1, keepdims=True)
            o_out[rows(q0, _BQ), :] = _unstack_heads(_dot(p.astype(BF16), vw) / den)
            lse_out[rows(q0, _BQ), :] = _unstack_heads(jnp.broadcast_to(m + jnp.log(den), (2 * _BQ, _PAIR)))
            return carry

        lax.fori_loop(0, n_blocks, block, 0, unroll=min(_ATTN_UNROLL, n_blocks))
        flush()

    stage = pltpu.VMEM((length, _PAIR), BF16)
    bias = pltpu.VMEM((3, 2 * _BQ, _KW), F32)
    return _attn_call(body, name, dilation, seq, 3, [F32, F32], [stage] * n_stage + [bias], 2)(slopes, q, k, v)


def _attn_bwd(q, k, v, do, lse, cterm, slopes, dilation, name):
    seq = q.shape[0]
    length = seq // dilation
    n_blocks = length // _BQ
    n_stage = 0 if dilation == 1 else 4

    def body(sl_ref, q_ref, k_ref, v_ref, do_ref, lse_ref, c_ref, dq_ref, dk_ref, dv_ref, *scratch):
        (dk_acc, dv_acc, bias_ref), tmps = scratch[n_stage:n_stage + 3], scratch[n_stage + 3:]
        (q_in, k_in, v_in, do_in, lse_in, c_in), (dq_out, dk_out, dv_out), rows, flush = _residue_views(
            dilation, seq, (q_ref, k_ref, v_ref, do_ref, lse_ref, c_ref), (dq_ref, dk_ref, dv_ref), tmps)
        all_rows = rows(0, length)
        qs, ks, vs, dos = _staged(dilation, length, rows, (q_in, k_in, v_in, do_in), scratch[:n_stage])
        dk_acc[...] = jnp.zeros_like(dk_acc)
        dv_acc[...] = jnp.zeros_like(dv_acc)
        _fill_bias(bias_ref, sl_ref, dilation)

        def block(i, carry):
            q0, ws, which = _block_window(i, n_blocks, length)
            qm = _stack_heads(qs[pl.ds(q0, _BQ), :])
            dom = _stack_heads(dos[pl.ds(q0, _BQ), :])
            kw = ks[pl.ds(ws, _KW), :]
            vw = vs[pl.ds(ws, _KW), :]
            p = jnp.exp(_dot_nt(qm, kw) + bias_ref[which] - _stack_cols(lse_in[rows(q0, _BQ), :]))
            ds = (p * (_dot_nt(dom, vw) + _stack_cols(c_in[rows(q0, _BQ), :]))).astype(BF16)
            dq_out[rows(q0, _BQ), :] = (_unstack_heads(_dot(ds, kw)) * _SCORE_SCALE).astype(dq_out.dtype)
            dk_acc[pl.ds(ws, _KW), :] += _dot_tn(ds, qm)
            dv_acc[pl.ds(ws, _KW), :] += _dot_tn(p.astype(BF16), dom)
            return carry

        lax.fori_loop(0, n_blocks, block, 0, unroll=min(_ATTN_UNROLL, n_blocks))
        dk_out[all_rows, :] = dk_acc[...].astype(dk_out.dtype)
        dv_out[all_rows, :] = dv_acc[...].astype(dv_out.dtype)
        flush()

    stage = pltpu.VMEM((length, _PAIR), BF16)
    acc = pltpu.VMEM((length, _PAIR), F32)
    bias = pltpu.VMEM((3, 2 * _BQ, _KW), F32)
    return _attn_call(body, name, dilation, seq, 6, [_attn_dtype(dilation)] * 3, [stage] * n_stage + [acc] * 2 + [bias],
                      2 if dilation == 1 else 1)(slopes, q, k, v, do, lse, cterm)


def _group_weights(lses):
    m = jnp.maximum(jnp.maximum(lses[0], lses[1]), lses[2])
    es = [jnp.exp(l - m) for l in lses]
    den = es[0] + es[1] + es[2]
    return [e / den for e in es]


def _out_fwd(a_pool, outs, lses, x, w_out, g, name, tm=512):
    s, d = x.shape
    width = POOL_DIM + 3 * GROUP_DIM

    def body(ap_ref, o0, o1, o2, l0, l1, l2, x_ref, w_ref, g_ref, xo_ref, mix_ref, cat_ref):
        alphas = _group_weights([l0[...], l1[...], l2[...]])
        cat = jnp.concatenate([ap_ref[...]] + [(o[...] * al).astype(BF16) for o, al in zip((o0, o1, o2), alphas)], axis=1)
        cat_ref[...] = cat
        mix = _dot(cat, w_ref[...])
        mix_ref[...] = mix
        xo_ref[...] = x_ref[...] + mix * _inv_rms(mix) * g_ref[...]

    return pl.pallas_call(
        body, name=name, grid=(s // tm,),
        in_specs=[_rows(tm, POOL_DIM)] + [_rows(tm, GROUP_DIM)] * 6 + [_rows(tm, d), _resident(w_out.shape), _const((1, d))],
        out_specs=[_rows(tm, d), _rows(tm, d), _rows(tm, width)],
        out_shape=[jax.ShapeDtypeStruct((s, d), F32), jax.ShapeDtypeStruct((s, d), F32), jax.ShapeDtypeStruct((s, width), BF16)],
        compiler_params=_params(dimension_semantics=("arbitrary",)))(a_pool, *outs, *lses, x, w_out, g)


def _out_bwd(dxo, mix, outs, lses, w_out, g, head_ones, name, tm=512):
    s, d = mix.shape

    def body(dxo_ref, mix_ref, o0, o1, o2, l0, l1, l2, w_ref, g_ref, ones_ref, dpool_ref, dmix_ref, do0, do1, do2, c0, c1, c2, dg_ref):
        mv = mix_ref[...]
        dmix, dg = _rms_bwd(mv, _inv_rms(mv), g_ref[...], dxo_ref[...])
        dmb = dmix.astype(BF16)
        dmix_ref[...] = dmb
        _accumulate(dg_ref, dg)
        dcat = _dot_nt(dmb, w_ref[...])
        dpool_ref[...] = dcat[:, :POOL_DIM]
        alphas = _group_weights([l0[...], l1[...], l2[...]])
        das = [dcat[:, POOL_DIM + GROUP_DIM * j:POOL_DIM + GROUP_DIM * (j + 1)] for j in range(3)]
        prod = sum(da * (o[...] * al) for da, o, al in zip(das, (o0, o1, o2), alphas))
        hi = prod.astype(BF16)
        lo = (prod - hi.astype(F32)).astype(BF16)
        total = _dot(hi, ones_ref[...]) + _dot(lo, ones_ref[...])
        for da, al, do_ref, c_ref in zip(das, alphas, (do0, do1, do2), (c0, c1, c2)):
            do_ref[...] = (da * al).astype(do_ref.dtype)
            c_ref[...] = -al * total

    return pl.pallas_call(
        body, name=name, grid=(s // tm,),
        in_specs=[_rows(tm, d), _rows(tm, d)] + [_rows(tm, GROUP_DIM)] * 6 + [_resident(w_out.shape), _const((1, d)),
                                                                             _const((GROUP_DIM, GROUP_DIM))],
        out_specs=[_rows(tm, POOL_DIM), _rows(tm, d)] + [_rows(tm, GROUP_DIM)] * 6 + [_const((1, d))],
        out_shape=[jax.ShapeDtypeStruct((s, POOL_DIM), F32), jax.ShapeDtypeStruct((s, d), BF16)]
        + [jax.ShapeDtypeStruct((s, GROUP_DIM), _attn_dtype(dil)) for dil in DILATIONS]
        + [jax.ShapeDtypeStruct((s, GROUP_DIM), F32)] * 3 + [jax.ShapeDtypeStruct((1, d), F32)],
        compiler_params=_params(dimension_semantics=("arbitrary",)))(dxo, mix, *outs, *lses, w_out, g, head_ones)


def _alibi_slopes():
    return np.array([2.0 ** (-8.0 * (i + 1) / N_ATTN_HEADS) for i in range(N_ATTN_HEADS)], np.float32)


def _block_diag(w_lin):
    n, c, _ = w_lin.shape
    eye = jnp.eye(n, dtype=w_lin.dtype)
    return (eye[:, None, :, None] * w_lin[:, :, None, :]).reshape(n * c, n * c)


class _NoExchange:
    def __init__(self, full):
        self.full, self.grads = full, {}

    def first_weights(self):
        return self.full

    def riders(self, host):
        return []

    def landed(self, host, results):
        return self.full

    def gradient(self, name, grad):
        self.grads[name] = grad


def _local_step(x, target, small, exchange):
    s, d = x.shape
    slopes = _alibi_slopes()
    group_slopes = [jnp.asarray(slopes[4 * g:4 * g + 4]) for g in range(3)]
    w_bd = _block_diag(small["w_pool_lin"]).astype(BF16)
    head_ones = jnp.asarray(np.kron(np.eye(GROUP_DIM // HEAD_DIM), np.ones((HEAD_DIM, HEAD_DIM))), BF16)

    full = dict(exchange.first_weights())

    def hosted(call, host, *args):
        results, riding = call(*args, host, exchange.riders(host))
        full.update(exchange.landed(host, riding) or {})
        return results

    x1, a1, b1, f1 = hosted(_ffn_fwd, "ffn1_fwd", x, small["g_ffn1_pre"], full["w1_gate"], full["w1_up"], full["w1_down"],
                            small["g_ffn1_post"], None)
    h2, u, *parts = hosted(_in_fwd, "in_fwd", x1, small["g_mix_pre"], full["w_in"])
    qs, ks, vs = parts[0:3], parts[3:6], parts[6:9]
    a_pool = _pool_fwd(u, w_bd, small["pool_scale"], "pool_fwd")
    outs, lses = [], []
    for g, dil in enumerate(DILATIONS):
        o, lse = _attn_fwd(qs[g], ks[g], vs[g], group_slopes[g], dil, f"attn_fwd{g}")
        outs.append(o)
        lses.append(lse)
    x2, mix, cat = _out_fwd(a_pool, outs, lses, x1, full["w_out"], small["g_mix_post"], "out_fwd")
    (dx3, a2, b2, f2, loss_part), _ = _ffn_fwd(x2, small["g_ffn2_pre"], full["w2_gate"], full["w2_up"], full["w2_down"],
                                               small["g_ffn2_post"], target, "ffn2_fwd")

    small_grads = {}

    def ffn_backward(tag, dxo, x_in, f, a, b):
        n = tag[-1]
        dx, hh, da, db, df, h, dg_pre, dg_post = hosted(
            _ffn_bwd, f"{tag}_bwd", dxo, x_in, f, a, b, small[f"g_{tag}_pre"], small[f"g_{tag}_post"],
            full[f"w{n}_gate"], full[f"w{n}_up"], full[f"w{n}_down"])
        for part, lhs, rhs in (("down", hh, df), ("gate", da, h), ("up", db, h)):
            exchange.gradient(f"w{n}_{part}", hosted(_wgrad, f"{tag}_wgrad_{part}", lhs, rhs))
        small_grads[f"g_{tag}_pre"], small_grads[f"g_{tag}_post"] = dg_pre, dg_post
        return dx

    dx2 = ffn_backward("ffn2", dx3, x2, f2, a2, b2)
    dpool, dmix, *dos_cs, small_grads["g_mix_post"] = _out_bwd(dx2, mix, outs, lses, full["w_out"], small["g_mix_post"],
                                                               head_ones, "out_bwd")
    dos, cs = dos_cs[:3], dos_cs[3:]
    dqs, dks, dvs = [], [], []
    for g, dil in enumerate(DILATIONS):
        dq, dk, dv = _attn_bwd(qs[g], ks[g], vs[g], dos[g], lses[g], cs[g], group_slopes[g], dil, f"attn_bwd{g}")
        dqs.append(dq)
        dks.append(dk)
        dvs.append(dv)
    du, dw_bd, small_grads["pool_scale"] = _pool_bwd(u, dpool, w_bd, small["pool_scale"], "pool_bwd")
    n_pool = len(POOL_HALF_WINDOWS)
    small_grads["w_pool_lin"] = jnp.stack(
        [dw_bd[HEAD_DIM * g:HEAD_DIM * (g + 1), HEAD_DIM * g:HEAD_DIM * (g + 1)] for g in range(n_pool)])
    dx1, dz, small_grads["g_mix_pre"] = hosted(_in_bwd, "in_bwd", du, dqs + dks + dvs, x1, dx2, small["g_mix_pre"], full["w_in"])
    exchange.gradient("w_in", hosted(_wgrad, "wgrad_in", dz, h2))
    dx0 = ffn_backward("ffn1", dx1, x, f1, a1, b1)
    exchange.gradient("w_out", hosted(_wgrad, "wgrad_out", cat, dmix))
    return loss_part[0, 0], dx0, small_grads


SEGMENTS = ("w1_gate", "w1_up", "w1_down", "w_in", "w_out", "w2_gate", "w2_up", "w2_down")
TRANSPOSED = ("w1_gate", "w1_up", "w_in", "w2_gate", "w2_up")
ROWS_OUTSIDE = ("w1_gate", "w1_up", "w2_gate", "w2_up")
HALF = 512


def _place():
    x, y, c = lax.axis_index("x"), lax.axis_index("y"), lax.axis_index("c")
    other_chips = [(1 - x, y), (x, 1 - y), (1 - x, 1 - y)]
    return x, y, c, other_chips


def _chip_rows(chip, rows):
    return pl.ds(pl.multiple_of((2 * chip[0] + chip[1]) * rows, 16), rows)


def _cols(c):
    return pl.ds(pl.multiple_of(c * HALF, HALF), HALF)


def _cast_shards(shards, transposed, place, name):
    n = len(shards)
    rows = [w.shape[1] if t else w.shape[0] for w, t in zip(shards, transposed)]

    def body(place_ref, *refs):
        for w_ref, o_ref, t in zip(refs[:n], refs[n:], transposed):
            o_ref[...] = (w_ref[...].T if t else w_ref[...]).astype(BF16)

    once = pl.Buffered(1)
    return pl.pallas_call(
        body, name=name,
        grid_spec=pltpu.PrefetchScalarGridSpec(
            num_scalar_prefetch=1, grid=(1,),
            in_specs=[pl.BlockSpec(w.shape, lambda i, place: (0, 0), pipeline_mode=once) for w in shards],
            out_specs=[pl.BlockSpec((r, 1024), lambda i, place: (place[0], 0), pipeline_mode=once) for r in rows]),
        out_shape=[jax.ShapeDtypeStruct((N_CHIPS * r, 1024), BF16) for r in rows],
        compiler_params=_params(dimension_semantics=("arbitrary",)))(place, *shards)


def _gather_weights(bufs):
    n = len(bufs)
    rows = [b.shape[0] // N_CHIPS for b in bufs]

    def body(*refs):
        outs = refs[n:2 * n]
        send_sems, recv_sems, fwd_send_sems, fwd_recv_sems = refs[2 * n:]
        x, y, c, chips = _place()
        me = (x, y)

        def ici(j, k, src_chip, to):
            blk = outs[k].at[_chip_rows(src_chip, rows[k]), _cols(c)]
            return pltpu.make_async_remote_copy(src_ref=blk, dst_ref=blk, send_sem=send_sems.at[j, k], recv_sem=recv_sems.at[j, k],
                                                device_id=to, device_id_type=MESH)

        def d2d(j, k, src_chip, half):
            blk = outs[k].at[_chip_rows(src_chip, rows[k]), _cols(half)]
            return pltpu.make_async_remote_copy(src_ref=blk, dst_ref=blk, send_sem=fwd_send_sems.at[j, k],
                                                recv_sem=fwd_recv_sems.at[j, k], device_id=(x, y, 1 - c), device_id_type=MESH)

        sends = [ici(j, k, me, (*chip, c)) for j, chip in enumerate(chips) for k in range(n)]
        for cp in sends:
            cp.start()
        forwards = []
        for j, chip in enumerate(chips):
            for k in range(n):
                ici(j, k, chip, (x, y, c)).wait_recv()
                fw = d2d(j, k, chip, c)
                fw.start()
                forwards.append(fw)
        for j, chip in enumerate(chips):
            for k in range(n):
                d2d(j, k, chip, 1 - c).wait_recv()
        for cp in sends + forwards:
            cp.wait_send()

    any_spec = pl.BlockSpec(memory_space=pl.ANY)
    return pl.pallas_call(
        body, name="gather_weights", in_specs=[any_spec] * n, out_specs=[any_spec] * n,
        out_shape=[jax.ShapeDtypeStruct(b.shape, b.dtype) for b in bufs], input_output_aliases={k: k for k in range(n)},
        scratch_shapes=[pltpu.SemaphoreType.DMA((3, n)), pltpu.SemaphoreType.DMA((3, n)),
                        pltpu.SemaphoreType.DMA((3, n)), pltpu.SemaphoreType.DMA((3, n))])(*bufs)


def _gather_rider(bufs):
    n = len(bufs)
    rows = [b.shape[0] // N_CHIPS for b in bufs]

    def copies(outs, send_sems, recv_sems, inbound):
        x, y, c, chips = _place()
        for j, chip in enumerate(chips):
            for k in range(n):
                src_chip = chip if inbound else (x, y)
                blk = outs[k].at[_chip_rows(src_chip, rows[k]), _cols(c)]
                yield pltpu.make_async_remote_copy(src_ref=blk, dst_ref=blk, send_sem=send_sems.at[j, k], recv_sem=recv_sems.at[j, k],
                                                   device_id=(*chip, c), device_id_type=MESH)

    def start(ins, outs, send_sems, recv_sems):
        for cp in copies(outs, send_sems, recv_sems, False):
            cp.start()

    def wait(ins, outs, send_sems, recv_sems):
        for cp in copies(outs, send_sems, recv_sems, True):
            cp.wait_recv()
        for cp in copies(outs, send_sems, recv_sems, False):
            cp.wait_send()

    return _Rider(list(bufs), None, (3, n), start, wait)


def _forward_rider(bufs):
    n = len(bufs)
    rows = [b.shape[0] // N_CHIPS for b in bufs]

    def copies(outs, send_sems, recv_sems, half):
        x, y, c, chips = _place()
        for j, chip in enumerate(chips):
            for k in range(n):
                blk = outs[k].at[_chip_rows(chip, rows[k]), _cols(half(c))]
                yield pltpu.make_async_remote_copy(src_ref=blk, dst_ref=blk, send_sem=send_sems.at[j, k], recv_sem=recv_sems.at[j, k],
                                                   device_id=(x, y, 1 - c), device_id_type=MESH)

    def start(ins, outs, send_sems, recv_sems):
        for cp in copies(outs, send_sems, recv_sems, lambda c: c):
            cp.start()

    def wait(ins, outs, send_sems, recv_sems):
        for cp in copies(outs, send_sems, recv_sems, lambda c: 1 - c):
            cp.wait_recv()
        for cp in copies(outs, send_sems, recv_sems, lambda c: c):
            cp.wait_send()

    return _Rider(list(bufs), None, (3, n), start, wait)


def _sibling_rider(grads):
    n = len(grads)

    def copies(ins, outs, send_sems, recv_sems):
        x, y, c, _ = _place()
        return [pltpu.make_async_remote_copy(src_ref=ins[k].at[:, pl.ds(1 - c, 1)], dst_ref=outs[k], send_sem=send_sems.at[k],
                                             recv_sem=recv_sems.at[k], device_id=(x, y, 1 - c), device_id_type=MESH)
                for k in range(n)]

    def start(*refs):
        for cp in copies(*refs):
            cp.start()

    def wait(*refs):
        for cp in copies(*refs):
            cp.wait()

    return _Rider(list(grads), [jax.ShapeDtypeStruct((N_CHIPS, 1) + g.shape[2:], F32) for g in grads], (n,), start, wait)


def _alone(rider, name):
    n = len(rider.operands)
    landing = rider.landing if rider.landing is not None else [jax.ShapeDtypeStruct(a.shape, a.dtype) for a in rider.operands]
    n_out = len(landing)

    def body(*refs):
        rider.start(refs[:n], refs[n:n + n_out], *refs[n + n_out:])
        rider.wait(refs[:n], refs[n:n + n_out], *refs[n + n_out:])

    any_spec = pl.BlockSpec(memory_space=pl.ANY)
    return pl.pallas_call(body, name=name, in_specs=[any_spec] * n, out_specs=[any_spec] * n_out, out_shape=landing,
                          input_output_aliases={i: i for i in range(n)} if rider.landing is None else {},
                          scratch_shapes=[pltpu.SemaphoreType.DMA(rider.sems)] * 2)(*rider.operands)


def _chip_sum(grad, from_sibling, place, name):
    rh, width = grad.shape[2:]

    def body(place_ref, g_ref, s_ref, own_ref, all_ref):
        all_ref[...] = (g_ref[...] + s_ref[...]).astype(BF16)
        mine = place_ref[0]
        own_ref[0] = g_ref[mine, 0] + s_ref[mine, 0]

    blk = (N_CHIPS, 1, rh, width)
    once = pl.Buffered(1)
    return pl.pallas_call(
        body, name=name,
        grid_spec=pltpu.PrefetchScalarGridSpec(
            num_scalar_prefetch=1, grid=(1,),
            in_specs=[pl.BlockSpec(blk, lambda i, place: (0, place[1], 0, 0), pipeline_mode=once),
                      pl.BlockSpec(blk, lambda i, place: (0, 0, 0, 0), pipeline_mode=once)],
            out_specs=[pl.BlockSpec((1, rh, width), lambda i, place: (0, 0, 0), pipeline_mode=once),
                       pl.BlockSpec(blk, lambda i, place: (0, 0, 0, 0), pipeline_mode=once)]),
        out_shape=[jax.ShapeDtypeStruct((1, rh, width), F32), jax.ShapeDtypeStruct((N_CHIPS, 1, rh, width), BF16)],
        compiler_params=_params(dimension_semantics=("arbitrary",)))(place, grad, from_sibling)


def _scatter_rider(sums):
    n = len(sums)

    def copies(ins, outs, send_sems, recv_sems):
        x, y, c, chips = _place()
        return [pltpu.make_async_remote_copy(src_ref=ins[k].at[pl.ds(2 * chip[0] + chip[1], 1)], dst_ref=outs[k].at[pl.ds(j, 1)],
                                             send_sem=send_sems.at[j, k], recv_sem=recv_sems.at[j, k],
                                             device_id=(*chip, c), device_id_type=MESH)
                for j, chip in enumerate(chips) for k in range(n)]

    def start(*refs):
        for cp in copies(*refs):
            cp.start()

    def wait(*refs):
        for cp in copies(*refs):
            cp.wait()

    return _Rider(list(sums), [jax.ShapeDtypeStruct((3,) + sm.shape[1:], BF16) for sm in sums], (3, n), start, wait)


def _total_sums(owns, received, name):
    n = len(owns)

    def body(*refs):
        for o_ref, r_ref, t_ref in zip(refs[:n], refs[n:2 * n], refs[2 * n:]):
            total = o_ref[0]
            for j in range(3):
                total = total + r_ref[j, 0].astype(F32)
            t_ref[0] = total

    return pl.pallas_call(body, name=name, out_shape=[jax.ShapeDtypeStruct(o.shape, F32) for o in owns],
                          compiler_params=_params())(*owns, *received)


def _swap_halves(halves):
    n = len(halves)

    def body(*refs):
        ins, outs = refs[:n], refs[n:2 * n]
        send_sems, recv_sems = refs[2 * n:]
        x, y, c, _ = _place()
        copies = [pltpu.make_async_remote_copy(src_ref=ins[k], dst_ref=outs[k], send_sem=send_sems.at[k],
                                               recv_sem=recv_sems.at[k], device_id=(x, y, 1 - c), device_id_type=MESH)
                  for k in range(n)]
        for cp in copies:
            cp.start()
        for cp in copies:
            cp.wait()

    any_spec = pl.BlockSpec(memory_space=pl.ANY)
    return pl.pallas_call(
        body, name="swap_halves", in_specs=[any_spec] * n, out_specs=[any_spec] * n,
        out_shape=[jax.ShapeDtypeStruct(h.shape, F32) for h in halves],
        scratch_shapes=[pltpu.SemaphoreType.DMA((n,)), pltpu.SemaphoreType.DMA((n,))])(*halves)


N_DEV = 8


def _gather_small(block):
    m_per, width = block.shape

    def body(x_ref, out_ref, send_sems, recv_sems, local_sem):
        x, y, c, chips = _place()
        me, sibling = (x, y, c), (x, y, 1 - c)

        def rows(px, py, pc):
            return out_ref.at[pl.ds((4 * px + 2 * py + pc) * m_per, m_per), :]

        def copy(k, blk, to, src=None):
            return pltpu.make_async_remote_copy(src_ref=rows(*blk) if src is None else src, dst_ref=rows(*blk),
                                                send_sem=send_sems.at[k], recv_sem=recv_sems.at[k], device_id=to, device_id_type=MESH)

        mine = pltpu.make_async_copy(x_ref, rows(*me), local_sem)
        mine.start()
        first = [copy(0, me, sibling, src=x_ref)] + [copy(1 + j, me, (*chip, c), src=x_ref) for j, chip in enumerate(chips)]
        for cp in first:
            cp.start()
        passed = [copy(4 + j, (*chip, c), sibling) for j, chip in enumerate(chips)]
        for j, chip in enumerate(chips):
            copy(1 + j, (*chip, c), me).wait_recv()
            passed[j].start()
        copy(0, sibling, me).wait_recv()
        for j, chip in enumerate(chips):
            copy(4 + j, (*chip, 1 - c), me).wait_recv()
        for cp in first + passed:
            cp.wait_send()
        mine.wait()

    vmem = pl.BlockSpec(memory_space=pltpu.VMEM)
    return pl.pallas_call(body, name="gather_small", out_shape=jax.ShapeDtypeStruct((N_DEV * m_per, width), F32),
                          in_specs=[vmem], out_specs=vmem,
                          scratch_shapes=[pltpu.SemaphoreType.DMA((7,)), pltpu.SemaphoreType.DMA((7,)),
                                          pltpu.SemaphoreType.DMA])(block)


def _adamw_math(w, g, m, v):
    m = ADAM_B1 * m + (1.0 - ADAM_B1) * g
    v = ADAM_B2 * v + (1.0 - ADAM_B2) * (g * g)
    m_hat = m / (1.0 - ADAM_B1 ** ADAM_STEP)
    v_hat = v / (1.0 - ADAM_B2 ** ADAM_STEP)
    delta = -ADAM_LR * (m_hat / (jnp.sqrt(v_hat) + ADAM_EPS) + ADAM_WD * w)
    return delta, m, v


def _adamw(w, mine, siblings, place, m, v, transposed, name):
    if transposed:
        def body(place_ref, w_ref, mine_ref, sib_ref, m_ref, v_ref, go_ref, d_ref, mo_ref, vo_ref):
            first = place_ref[1] == 0
            g = jnp.concatenate([jnp.where(first, mine_ref[0], sib_ref[0]), jnp.where(first, sib_ref[0], mine_ref[0])], axis=0).T
            go_ref[...] = g
            d_ref[...], mo_ref[...], vo_ref[...] = _adamw_math(w_ref[...], g, m_ref[...], v_ref[...])

        vmem = pl.BlockSpec(memory_space=pltpu.VMEM)
        return pl.pallas_call(body, name=name, in_specs=[pl.BlockSpec(memory_space=pltpu.SMEM)] + [vmem] * 5, out_specs=[vmem] * 4,
                              out_shape=[jax.ShapeDtypeStruct(w.shape, F32)] * 4, compiler_params=_params())(
                                  place, w, mine, siblings, m, v)

    rh, width = mine.shape[1:]

    def body(place_ref, w_ref, mine_ref, sib_ref, m_ref, v_ref, go_ref, d_ref, mo_ref, vo_ref):
        g = jnp.where(pl.program_id(0) == place_ref[1], mine_ref[0], sib_ref[0])
        go_ref[...] = g
        d_ref[...], mo_ref[...], vo_ref[...] = _adamw_math(w_ref[...], g, m_ref[...], v_ref[...])

    half = pl.BlockSpec((rh, width), lambda h, place: (h, 0))
    whole = pl.BlockSpec((1, rh, width), lambda h, place: (0, 0, 0))
    return pl.pallas_call(
        body, name=name,
        grid_spec=pltpu.PrefetchScalarGridSpec(num_scalar_prefetch=1, grid=(2,), in_specs=[half, whole, whole, half, half],
                                               out_specs=[half] * 4),
        out_shape=[jax.ShapeDtypeStruct(w.shape, F32)] * 4,
        compiler_params=_params(dimension_semantics=("arbitrary",)))(place, w, mine, siblings, m, v)


def _adamw_small(gathered, w, m, v, name):
    def body(ga_ref, w_ref, m_ref, v_ref, go_ref, d_ref, mo_ref, vo_ref):
        g = ga_ref[0]
        for dev in range(1, N_DEV):
            g = g + ga_ref[dev]
        go_ref[...] = g
        d_ref[...], mo_ref[...], vo_ref[...] = _adamw_math(w_ref[...], g, m_ref[...], v_ref[...])

    return pl.pallas_call(body, name=name, out_shape=[jax.ShapeDtypeStruct(w.shape, F32)] * 4,
                          compiler_params=_params())(gathered, w, m, v)


class _Exchange:
    FIRST = ("w1_gate", "w1_up", "w1_down")
    HOSTS = {"ffn2_wgrad_gate": (("w2_down",), ()), "ffn2_wgrad_up": (("w2_gate",), ("w2_down",)),
             "in_bwd": (("w2_up",), ("w2_gate",)), "wgrad_in": ((), ("w2_up",)),
             "ffn1_wgrad_down": ((), ("w_in",)), "ffn1_wgrad_gate": (("w1_down",), ()), "ffn1_wgrad_up": ((), ("w1_down", "w1_gate")),
             "wgrad_out": ((), ("w1_up",))}
    ALONE = ("w_in", "w1_gate", "w1_up", "w_out")

    def __init__(self, bufs, place):
        self.bufs, self.place = bufs, place
        self.later = [k for k in SEGMENTS if k not in self.FIRST]
        self.split, self.own, self.to_send, self.received = {}, {}, {}, {}

    def first_weights(self):
        return dict(zip(self.FIRST, _gather_weights([self.bufs[k] for k in self.FIRST])))

    def riders(self, host):
        if host == "ffn1_fwd":
            return [_gather_rider([self.bufs[k] for k in self.later])]
        if host == "in_fwd":
            return [_forward_rider([self.bufs[k] for k in self.later[1:]])]
        halves, sums = self.HOSTS.get(host, ((), ()))
        return ([_sibling_rider([self.split[k] for k in halves])] if halves else []) + (
            [_scatter_rider([self.to_send[k] for k in sums])] if sums else [])

    def landed(self, host, results):
        if host == "ffn1_fwd":
            self.bufs.update(zip(self.later, results[0]))
            return dict(zip(self.later[:1], _alone(_forward_rider([self.bufs[self.later[0]]]), "gather_forward_first")))
        if host == "in_fwd":
            return dict(zip(self.later[1:], results[0]))
        halves, sums = self.HOSTS.get(host, ((), ()))
        if halves:
            self._chip_sums(halves, results[0])
        if sums:
            self.received.update(zip(sums, results[-1]))

    def gradient(self, name, grad):
        self.split[name] = grad.reshape(N_CHIPS, 2, grad.shape[0] // (2 * N_CHIPS), grad.shape[1])
        if name in self.ALONE:
            self._chip_sums([name], _alone(_sibling_rider([self.split[name]]), f"reduce_sibling_{name}"))

    def _chip_sums(self, names, from_sibling):
        for k, fs in zip(names, from_sibling):
            self.own[k], self.to_send[k] = _chip_sum(self.split[k], fs, self.place, f"chip_sum_{k}")

    def summed_halves(self):
        late = [k for k in SEGMENTS if k not in self.received]
        self.received.update(zip(late, _alone(_scatter_rider([self.to_send[k] for k in late]), "reduce_chips_last")))
        return _total_sums([self.own[k] for k in SEGMENTS], [self.received[k] for k in SEGMENTS], "total_sums")


SMALL = ("g_ffn1_pre", "g_ffn1_post", "g_mix_pre", "w_pool_lin", "pool_scale", "g_mix_post", "g_ffn2_pre", "g_ffn2_post")
WEIGHTS = ("g_ffn1_pre", "w1_gate", "w1_up", "w1_down", "g_ffn1_post", "g_mix_pre", "w_in", "w_pool_lin", "pool_scale", "w_out",
           "g_mix_post", "g_ffn2_pre", "w2_gate", "w2_up", "w2_down", "g_ffn2_post")
LANES = 128


def _pack_small(tree, extra=0.0):
    flat = jnp.concatenate([tree[k].reshape(-1) for k in SMALL] + [jnp.reshape(extra, (1,)).astype(F32)])
    rows = -(-flat.shape[0] // (8 * LANES)) * 8
    return jnp.pad(flat, (0, rows * LANES - flat.shape[0])).reshape(rows, LANES)


def _unpack_small(packed, like):
    flat, out, at = packed.reshape(-1), {}, 0
    for k in SMALL:
        size = math.prod(like[k].shape)
        out[k] = flat[at:at + size].reshape(like[k].shape)
        at += size
    return out


def kernel(x, g_ffn1_pre, w1_gate, w1_up, w1_down, g_ffn1_post, g_mix_pre, w_in, w_pool_lin, pool_scale, w_out, g_mix_post, g_ffn2_pre, w2_gate, w2_up, w2_down, g_ffn2_post, loss_target, m_g_ffn1_pre, m_w1_gate, m_w1_up, m_w1_down, m_g_ffn1_post, m_g_mix_pre, m_w_in, m_w_pool_lin, m_pool_scale, m_w_out, m_g_mix_post, m_g_ffn2_pre, m_w2_gate, m_w2_up, m_w2_down, m_g_ffn2_post, v_g_ffn1_pre, v_w1_gate, v_w1_up, v_w1_down, v_g_ffn1_post, v_g_mix_pre, v_w_in, v_w_pool_lin, v_pool_scale, v_w_out, v_g_mix_post, v_g_ffn2_pre, v_w2_gate, v_w2_up, v_w2_down, v_g_ffn2_post):
    given = dict(locals())
    w = {k: given[k] for k in WEIGHTS}
    m = {k: given["m_" + k] for k in WEIGHTS}
    v = {k: given["v_" + k] for k in WEIGHTS}
    small = {k: (w[k][0] if k == "w_pool_lin" else w[k].reshape(1, -1)) for k in SMALL}

    place = jnp.stack([2 * lax.axis_index("x") + lax.axis_index("y"), lax.axis_index("c")]).astype(jnp.int32)
    def as_rows(a, k):
        return jnp.swapaxes(a, 1, 2)[0] if k in ROWS_OUTSIDE else a[0]

    def as_given(a, k):
        return jnp.swapaxes(a[None], 1, 2) if k in ROWS_OUTSIDE else a[None]

    in_kernel = [k for k in TRANSPOSED if k not in ROWS_OUTSIDE]
    bufs = {}
    for tag, names in (("first", _Exchange.FIRST), ("rest", [k for k in SEGMENTS if k not in _Exchange.FIRST])):
        bufs.update(zip(names, _cast_shards([as_rows(w[k], k) for k in names], [k in in_kernel for k in names], place, f"cast_{tag}")))
    exchange = _Exchange(bufs, place)
    loss_part, grad_x, small_grads = _local_step(x[0], loss_target[0], small, exchange)

    halves = exchange.summed_halves()
    from_sibling = _swap_halves(halves)

    out_grad, out_delta, out_m, out_v = {}, {}, {}, {}
    for k, mine, sib in zip(SEGMENTS, halves, from_sibling):
        out_grad[k], out_delta[k], out_m[k], out_v[k] = (
            as_given(a, k) for a in _adamw(as_rows(w[k], k), mine, sib, place, as_rows(m[k], k), as_rows(v[k], k),
                                           k in in_kernel, f"adamw_{k}"))

    small_grads["w_pool_lin"] = small_grads["w_pool_lin"][None]
    packed = _pack_small(small_grads, loss_part)
    gathered = _gather_small(packed).reshape(N_DEV, *packed.shape)
    like = {k: w[k] for k in SMALL}
    results = _adamw_small(gathered, _pack_small(like), _pack_small({k: m[k] for k in SMALL}),
                           _pack_small({k: v[k] for k in SMALL}), "adamw_small")
    for tree, res in zip((out_grad, out_delta, out_m, out_v), results):
        tree.update(_unpack_small(res, like))
    loss = results[0].reshape(-1)[sum(math.prod(like[k].shape) for k in SMALL)]

    return (loss, grad_x[None], *[out_grad[k] for k in WEIGHTS], *[out_delta[k] for k in WEIGHTS],
            *[out_m[k] for k in WEIGHTS], *[out_v[k] for k in WEIGHTS])
```

```python
import math
import typing

import numpy as np
import jax
import jax.numpy as jnp
from jax import lax
from jax.experimental import pallas as pl
from jax.experimental.pallas import tpu as pltpu

F32 = jnp.float32
BF16 = jnp.bfloat16
MESH = pl.DeviceIdType.MESH

RMS_EPS = 1e-6
HEAD_DIM = 64
POOL_HALF_WINDOWS = (1, 2, 4, 8)
POOL_DIM = 256
GROUP_DIM = 256
DILATIONS = (1, 4, 16)
N_SIDE = 64
N_ATTN_HEADS = 12
ADAM_LR, ADAM_B1, ADAM_B2, ADAM_EPS, ADAM_WD, ADAM_STEP = 0.001, 0.9, 0.999, 1e-08, 0.01, 10

N_CHIPS = 4
V7X_VMEM_LIMIT = 60 * 1024 * 1024

_NT = (((1,), (1,)), ((), ()))
_TN = (((0,), (0,)), ((), ()))


def _dot(a, b):
    return jnp.dot(a, b, preferred_element_type=F32)


def _dot_nt(a, b):
    return lax.dot_general(a, b, _NT, preferred_element_type=F32)


def _dot_tn(a, b):
    return lax.dot_general(a, b, _TN, preferred_element_type=F32)


def _params(**kw):
    return pltpu.CompilerParams(vmem_limit_bytes=V7X_VMEM_LIMIT, **kw)


def _rows(tm, width):
    return pl.BlockSpec((tm, width), lambda i: (i, 0))


def _resident(shape):
    return pl.BlockSpec(shape, lambda i: (0,) * len(shape), pipeline_mode=pl.Buffered(1))


def _const(shape):
    return pl.BlockSpec(shape, lambda i: (0,) * len(shape))


def _inv_rms(x):
    return lax.rsqrt(jnp.mean(x * x, axis=-1, keepdims=True) + RMS_EPS)


def _rms_bwd(x, inv, g, dy):
    n = x * inv
    dn = dy * g
    dx = inv * (dn - n * jnp.mean(dn * n, axis=-1, keepdims=True))
    return dx, jnp.sum(dy * n, axis=0, keepdims=True)


def _accumulate(ref, value):
    @pl.when(pl.program_id(0) == 0)
    def _():
        ref[...] = jnp.zeros_like(ref)

    ref[...] += value


class _Rider(typing.NamedTuple):
    operands: list
    landing: typing.Optional[list]
    sems: tuple
    start: typing.Callable
    wait: typing.Callable


def _hosted_call(body, riders, *, name, steps, in_specs, out_specs, out_shape, args, scratch_shapes=()):
    params = _params(dimension_semantics=("arbitrary",))
    riders = list(riders or [])
    if not riders:
        res = pl.pallas_call(body, name=name, grid=(steps,), in_specs=in_specs, out_specs=out_specs, out_shape=out_shape,
                             scratch_shapes=list(scratch_shapes), compiler_params=params)(*args)
        return list(res), []
    n_in, n_out, n_scratch = len(in_specs), len(out_specs), len(scratch_shapes)
    operands, landing, aliases, spans = [], [], {}, []
    for rd in riders:
        lands = rd.landing if rd.landing is not None else [jax.ShapeDtypeStruct(a.shape, a.dtype) for a in rd.operands]
        if rd.landing is None:
            aliases.update({n_in + len(operands) + i: n_out + len(landing) + i for i in range(len(lands))})
        spans.append((len(operands), len(rd.operands), len(landing), len(lands)))
        operands += rd.operands
        landing += lands
    outs_at = n_in + len(operands)
    scratch_at = outs_at + n_out + len(landing)

    def riding(*refs):
        def each(action):
            for i, (rd, (in_at, n_ops, out_at, n_lands)) in enumerate(zip(riders, spans)):
                sems = refs[scratch_at + n_scratch + 2 * i:scratch_at + n_scratch + 2 * i + 2]
                getattr(rd, action)(refs[n_in + in_at:n_in + in_at + n_ops],
                                    refs[outs_at + n_out + out_at:outs_at + n_out + out_at + n_lands], *sems)

        @pl.when(pl.program_id(0) == 0)
        def _():
            each("start")

        body(*refs[:n_in], *refs[outs_at:outs_at + n_out], *refs[scratch_at:scratch_at + n_scratch])

        @pl.when(pl.program_id(0) == steps - 1)
        def _():
            each("wait")

    any_spec = pl.BlockSpec(memory_space=pl.ANY)
    res = pl.pallas_call(
        riding, name=name, grid=(steps,), in_specs=list(in_specs) + [any_spec] * len(operands),
        out_specs=list(out_specs) + [any_spec] * len(landing), out_shape=list(out_shape) + landing,
        scratch_shapes=list(scratch_shapes) + [pltpu.SemaphoreType.DMA(rd.sems) for rd in riders for _ in range(2)],
        input_output_aliases=aliases, compiler_params=params)(*args, *operands)
    return list(res[:n_out]), [list(res[n_out + out_at:n_out + out_at + n_lands]) for _, _, out_at, n_lands in spans]


_SUB_TILE = 256


def _sub_tiles(tm):
    return [pl.ds(r, _SUB_TILE) for r in range(0, tm, _SUB_TILE)]


def _ffn_fwd(x, g_pre, wg_t, wu_t, wd, g_post, target, name, riders=None, tm=512):
    s, d = x.shape
    ff = wd.shape[0]
    with_loss = target is not None

    def body(*refs):
        if with_loss:
            x_ref, gpre_ref, wg_ref, wu_ref, wd_ref, gpost_ref, t_ref, xo_ref, a_ref, b_ref, f_ref, loss_ref = refs
        else:
            x_ref, gpre_ref, wg_ref, wu_ref, wd_ref, gpost_ref, xo_ref, a_ref, b_ref, f_ref = refs
        loss = 0.0
        for rows in _sub_tiles(tm):
            xv = x_ref[rows, :]
            hb = (xv * _inv_rms(xv) * gpre_ref[...]).astype(BF16)
            a = _dot_nt(hb, wg_ref[...])
            b = _dot_nt(hb, wu_ref[...])
            hh = (a * jax.nn.sigmoid(a)) * b
            f = _dot(hh.astype(BF16), wd_ref[...])
            xo = xv + 0.5 * (f * _inv_rms(f) * gpost_ref[...])
            a_ref[rows, :] = a.astype(BF16)
            b_ref[rows, :] = b.astype(BF16)
            f_ref[rows, :] = f
            if with_loss:
                e = xo - t_ref[rows, :]
                xo_ref[rows, :] = e * (1.0 / d)
                loss = loss + 0.5 * jnp.sum(jnp.mean(e * e, axis=-1, keepdims=True))
            else:
                xo_ref[rows, :] = xo
        if with_loss:
            _accumulate(loss_ref, loss)

    in_specs = [_rows(tm, d), _const((1, d)), _resident((ff, d)), _resident((ff, d)), _resident((ff, d)), _const((1, d))]
    args = [x, g_pre, wg_t, wu_t, wd, g_post]
    out_shape = [jax.ShapeDtypeStruct((s, d), F32), jax.ShapeDtypeStruct((s, ff), BF16),
                 jax.ShapeDtypeStruct((s, ff), BF16), jax.ShapeDtypeStruct((s, d), F32)]
    out_specs = [_rows(tm, d), _rows(tm, ff), _rows(tm, ff), _rows(tm, d)]
    if with_loss:
        in_specs.append(_rows(tm, d))
        args.append(target)
        out_shape.append(jax.ShapeDtypeStruct((8, 128), F32))
        out_specs.append(_const((8, 128)))
    return _hosted_call(body, riders, name=name, steps=s // tm, in_specs=in_specs, out_specs=out_specs, out_shape=out_shape, args=args)


def _ffn_bwd(dxo, x, f, a, b, g_pre, g_post, wg_t, wu_t, wd, name, riders=None, tm=256):
    s, d = x.shape
    ff = wd.shape[0]

    def body(dxo_ref, x_ref, f_ref, a_ref, b_ref, gpre_ref, gpost_ref, wg_ref, wu_ref, wd_ref,
             dx_ref, hh_ref, da_ref, db_ref, df_ref, h_ref, dgpre_ref, dgpost_ref):
        dgpre_sum = dgpost_sum = 0.0
        for rows in _sub_tiles(tm):
            dxo_v = dxo_ref[rows, :]
            fv = f_ref[rows, :]
            df, dgpost = _rms_bwd(fv, _inv_rms(fv), gpost_ref[...], 0.5 * dxo_v)
            dfb = df.astype(BF16)
            dhh = _dot_nt(dfb, wd_ref[...])
            av = a_ref[rows, :].astype(F32)
            bv = b_ref[rows, :].astype(F32)
            sig = jax.nn.sigmoid(av)
            sa = av * sig
            da = (dhh * bv * (sig * (1.0 + av * (1.0 - sig)))).astype(BF16)
            db = (dhh * sa).astype(BF16)
            dh = _dot(da, wg_ref[...]) + _dot(db, wu_ref[...])
            xv = x_ref[rows, :]
            inv = _inv_rms(xv)
            dxn, dgpre = _rms_bwd(xv, inv, gpre_ref[...], dh)
            dx_ref[rows, :] = dxo_v + dxn
            hh_ref[rows, :] = (sa * bv).astype(BF16)
            da_ref[rows, :] = da
            db_ref[rows, :] = db
            df_ref[rows, :] = dfb
            h_ref[rows, :] = (xv * inv * gpre_ref[...]).astype(BF16)
            dgpre_sum, dgpost_sum = dgpre_sum + dgpre, dgpost_sum + dgpost
        _accumulate(dgpre_ref, dgpre_sum)
        _accumulate(dgpost_ref, dgpost_sum)

    return _hosted_call(
        body, riders, name=name, steps=s // tm,
        in_specs=[_rows(tm, d), _rows(tm, d), _rows(tm, d), _rows(tm, ff), _rows(tm, ff), _const((1, d)), _const((1, d)),
                  _resident((ff, d)), _resident((ff, d)), _resident((ff, d))],
        out_specs=[_rows(tm, d), _rows(tm, ff), _rows(tm, ff), _rows(tm, ff), _rows(tm, d), _rows(tm, d),
                   _const((1, d)), _const((1, d))],
        out_shape=[jax.ShapeDtypeStruct((s, d), F32), jax.ShapeDtypeStruct((s, ff), BF16), jax.ShapeDtypeStruct((s, ff), BF16),
                   jax.ShapeDtypeStruct((s, ff), BF16), jax.ShapeDtypeStruct((s, d), BF16), jax.ShapeDtypeStruct((s, d), BF16),
                   jax.ShapeDtypeStruct((1, d), F32), jax.ShapeDtypeStruct((1, d), F32)],
        args=[dxo, x, f, a, b, g_pre, g_post, wg_t, wu_t, wd])


def _wgrad(lhs, rhs, name, riders=None, rt=256):
    s, r = lhs.shape
    c = rhs.shape[1]

    def body(l_ref, r_ref, o_ref):
        o_ref[...] = _dot_tn(l_ref[...], r_ref[...])

    (out,), riding = _hosted_call(
        body, riders, name=name, steps=pl.cdiv(r, rt), in_specs=[pl.BlockSpec((s, rt), lambda i: (0, i)), _resident((s, c))],
        out_specs=[pl.BlockSpec((rt, c), lambda i: (i, 0))], out_shape=[jax.ShapeDtypeStruct((r, c), F32)], args=[lhs, rhs])
    return out, riding


def _attn_dtype(dilation):
    return BF16 if dilation == 1 else F32


def _in_fwd(x, g, w_in_t, name, riders=None, tm=1024):
    s, d = x.shape
    d_in = w_in_t.shape[0]
    n_groups = len(DILATIONS)
    dtypes = [_attn_dtype(dil) for dil in DILATIONS] * 3

    def body(x_ref, g_ref, w_ref, h_ref, u_ref, *part_refs):
        xv = x_ref[...]
        hb = (xv * _inv_rms(xv) * g_ref[...]).astype(BF16)
        h_ref[...] = hb
        z = _dot_nt(hb, w_ref[...])
        u_ref[...] = z[:, :POOL_DIM]
        for j, ref in enumerate(part_refs):
            part = z[:, POOL_DIM + GROUP_DIM * j:POOL_DIM + GROUP_DIM * (j + 1)]
            ref[...] = (part * _SCORE_SCALE if j < n_groups else part).astype(ref.dtype)

    return _hosted_call(
        body, riders, name=name, steps=s // tm, in_specs=[_rows(tm, d), _const((1, d)), _resident((d_in, d))],
        out_specs=[_rows(tm, d), _rows(tm, POOL_DIM)] + [_rows(tm, GROUP_DIM)] * len(dtypes),
        out_shape=[jax.ShapeDtypeStruct((s, d), BF16), jax.ShapeDtypeStruct((s, POOL_DIM), F32)]
        + [jax.ShapeDtypeStruct((s, GROUP_DIM), dt) for dt in dtypes],
        args=[x, g, w_in_t])


def _in_bwd(du, dparts, x, dxo, g, w_in_t, name, riders=None, tm=512):
    s, d = x.shape
    d_in = w_in_t.shape[0]
    n_parts = len(dparts)

    def body(du_ref, *refs):
        part_refs = refs[:n_parts]
        x_ref, dxo_ref, g_ref, w_ref, dx_ref, dz_ref, dg_ref = refs[n_parts:]
        dz = jnp.concatenate([r[...].astype(BF16) for r in (du_ref,) + part_refs], axis=1)
        dz_ref[...] = dz
        dh = _dot(dz, w_ref[...])
        xv = x_ref[...]
        dxn, dg = _rms_bwd(xv, _inv_rms(xv), g_ref[...], dh)
        dx_ref[...] = dxo_ref[...] + dxn
        _accumulate(dg_ref, dg)

    return _hosted_call(
        body, riders, name=name, steps=s // tm,
        in_specs=[_rows(tm, POOL_DIM)] + [_rows(tm, GROUP_DIM)] * n_parts + [_rows(tm, d), _rows(tm, d), _const((1, d)),
                                                                             _resident((d_in, d))],
        out_specs=[_rows(tm, d), _rows(tm, d_in), _const((1, d))],
        out_shape=[jax.ShapeDtypeStruct((s, d), F32), jax.ShapeDtypeStruct((s, d_in), BF16), jax.ShapeDtypeStruct((1, d), F32)],
        args=[du, *dparts, x, dxo, g, w_in_t])


_POOL_HALO = 8


def _pool_chain(v, first_shift):
    n = v.shape[0]
    p2 = v + pltpu.roll(v, first_shift, 0)
    p4 = pltpu.roll(p2, 1, 0) + pltpu.roll(p2, n - 1, 0)
    p8 = pltpu.roll(p4, 2, 0) + pltpu.roll(p4, n - 2, 0)
    p16 = pltpu.roll(p8, 4, 0) + pltpu.roll(p8, n - 4, 0)
    group = lax.broadcasted_iota(jnp.int32, v.shape, 1) // HEAD_DIM
    return jnp.where(group == 0, p2, jnp.where(group == 1, p4, jnp.where(group == 2, p8, p16)))


def _pool_count(t0, rows, s):
    t = t0 + lax.broadcasted_iota(jnp.int32, (rows, POOL_DIM), 0)
    group = lax.broadcasted_iota(jnp.int32, (rows, POOL_DIM), 1) // HEAD_DIM
    half = jnp.where(group == 0, 1, jnp.where(group == 1, 2, jnp.where(group == 2, 4, 8)))
    cnt = jnp.minimum(t + half, s) - jnp.maximum(t - half, 0)
    return jnp.maximum(cnt, 1).astype(F32)


def _pad_rows(ref, pad_ref, s):
    zeros = jnp.zeros((_POOL_HALO, pad_ref.shape[1]), pad_ref.dtype)
    pad_ref[pl.ds(0, _POOL_HALO), :] = zeros
    pad_ref[pl.ds(_POOL_HALO + s, _POOL_HALO), :] = zeros
    pad_ref[pl.ds(_POOL_HALO, s), :] = ref[...]


def _pool_fwd(u, w_bd, scale, name, tm=512):
    s = u.shape[0]
    ext = tm + 2 * _POOL_HALO

    def body(u_ref, w_ref, sc_ref, o_ref, upad):
        _pad_rows(u_ref, upad, s)

        def tile(i, carry):
            t0 = pl.multiple_of(i * tm, tm)
            uv = upad[pl.ds(t0, ext), :]
            win = _pool_chain(uv, 1)[_POOL_HALO:_POOL_HALO + tm]
            y = win / _pool_count(t0, tm, s) - uv[_POOL_HALO:_POOL_HALO + tm]
            o_ref[pl.ds(t0, tm), :] = (_dot(y.astype(BF16), w_ref[...]) * sc_ref[...]).astype(BF16)
            return carry

        lax.fori_loop(0, s // tm, tile, 0)

    return pl.pallas_call(body, name=name, out_shape=jax.ShapeDtypeStruct((s, POOL_DIM), BF16),
                          scratch_shapes=[pltpu.VMEM((s + 2 * _POOL_HALO, POOL_DIM), F32)],
                          compiler_params=_params())(u, w_bd, scale)


def _pool_bwd(u, da, w_bd, scale, name, tm=512):
    s = u.shape[0]
    ext = tm + 2 * _POOL_HALO

    def body(u_ref, da_ref, w_ref, sc_ref, du_ref, dw_ref, dsc_ref, upad, dapad):
        _pad_rows(u_ref, upad, s)
        _pad_rows(da_ref, dapad, s)
        dw_ref[...] = jnp.zeros_like(dw_ref)
        dsc_ref[...] = jnp.zeros_like(dsc_ref)

        def tile(i, carry):
            t0 = pl.multiple_of(i * tm, tm)
            uv = upad[pl.ds(t0, ext), :]
            dav = dapad[pl.ds(t0, ext), :]
            win = _pool_chain(uv, 1)[_POOL_HALO:_POOL_HALO + tm]
            yb = (win / _pool_count(t0, tm, s) - uv[_POOL_HALO:_POOL_HALO + tm]).astype(BF16)
            yl = _dot(yb, w_ref[...])
            da_c = dav[_POOL_HALO:_POOL_HALO + tm]
            dsc_ref[...] += jnp.sum(da_c * yl, axis=0, keepdims=True)
            dyl = (dav * sc_ref[...]).astype(BF16)
            dw_ref[...] += _dot_tn(yb, dyl[_POOL_HALO:_POOL_HALO + tm])
            dy = _dot_nt(dyl, w_ref[...])
            dyc = dy / _pool_count(t0 - _POOL_HALO, ext, s)
            du_ref[pl.ds(t0, tm), :] = (_pool_chain(dyc, ext - 1) - dy)[_POOL_HALO:_POOL_HALO + tm]
            return carry

        lax.fori_loop(0, s // tm, tile, 0)

    pool_cols = pl.BlockSpec((s, POOL_DIM), lambda i: (0, 0), pipeline_mode=pl.Buffered(1))
    return pl.pallas_call(
        body, name=name, grid=(1,),
        in_specs=[pool_cols, pool_cols, _const((POOL_DIM, POOL_DIM)), _const((1, POOL_DIM))],
        out_specs=[_const((s, POOL_DIM)), _const((POOL_DIM, POOL_DIM)), _const((1, POOL_DIM))],
        out_shape=[jax.ShapeDtypeStruct((s, POOL_DIM), F32), jax.ShapeDtypeStruct((POOL_DIM, POOL_DIM), F32),
                   jax.ShapeDtypeStruct((1, POOL_DIM), F32)],
        scratch_shapes=[pltpu.VMEM((s + 2 * _POOL_HALO, POOL_DIM), F32), pltpu.VMEM((s + 2 * _POOL_HALO, POOL_DIM), F32)],
        compiler_params=_params(dimension_semantics=("arbitrary",)))(u, da, w_bd, scale)


_BQ = 128
_KW = _BQ + 2 * N_SIDE
_PAIR = 2 * HEAD_DIM
_NEG = -1e30
_ATTN_UNROLL = 8
_SCORE_SCALE = HEAD_DIM ** -0.5


def _stack_heads(x):
    lane_head = lax.broadcasted_iota(jnp.int32, x.shape, 1) // HEAD_DIM
    zero = jnp.zeros_like(x)
    return jnp.concatenate([jnp.where(lane_head == 0, x, zero), jnp.where(lane_head == 1, x, zero)], axis=0)


def _unstack_heads(x):
    lane_head = lax.broadcasted_iota(jnp.int32, (_BQ, _PAIR), 1) // HEAD_DIM
    return jnp.where(lane_head == 0, x[:_BQ], x[_BQ:])


def _stack_cols(x):
    return jnp.concatenate([x[:, 0:1], x[:, HEAD_DIM:HEAD_DIM + 1]], axis=0)


def _fill_bias(bias_ref, slopes_ref, dilation):
    row = lax.broadcasted_iota(jnp.int32, (2 * _BQ, _KW), 0)
    col = lax.broadcasted_iota(jnp.int32, (2 * _BQ, _KW), 1)
    pair = 2 * pl.program_id(0)
    slope = jnp.where(row < _BQ, slopes_ref[pair], slopes_ref[pair + 1]) * float(dilation)

    @pl.when(pl.program_id(1) == 0)
    def _():
        for j in range(3):
            dist = jnp.abs(col - (row & (_BQ - 1)) - j * N_SIDE)
            bias_ref[j] = jnp.where(dist <= N_SIDE, -slope * dist.astype(F32), _NEG)


def _block_window(i, n_blocks, length):
    q0 = pl.multiple_of(i * _BQ, _BQ)
    ws = pl.multiple_of(jnp.clip(q0 - N_SIDE, 0, length - _KW), N_SIDE)
    return q0, ws, jnp.where(i == 0, 0, jnp.where(i == n_blocks - 1, 2, 1))


_FREE_STRIDE = 4


def _residue_views(dilation, seq, ins, outs, tmps):
    step = pl.program_id(1)
    if dilation <= _FREE_STRIDE:
        def rows(start, count):
            return pl.ds(start, count) if dilation == 1 else pl.ds(start * dilation + step, count, stride=dilation)

        return ins, outs, rows, lambda: None
    inner = dilation // _FREE_STRIDE
    assert inner <= _FREE_STRIDE and len(tmps) == len(ins) + len(outs)
    first, second = step // inner, step % inner
    coarse = pl.ds(first, seq // _FREE_STRIDE, stride=_FREE_STRIDE)
    in_tmps, out_tmps = tmps[:len(ins)], tmps[len(ins):]

    @pl.when(second == 0)
    def _():
        for ref, tmp in zip(ins, in_tmps):
            tmp[...] = ref[coarse, :]

    def flush():
        @pl.when(second == inner - 1)
        def _():
            for ref, tmp in zip(outs, out_tmps):
                ref[coarse, :] = tmp[...]

    return in_tmps, out_tmps, lambda start, count: pl.ds(start * inner + second, count, stride=inner), flush


def _attn_call(body, name, dilation, seq, n_in, out_dtypes, scratch, buffers):
    col = pl.BlockSpec((seq, _PAIR), lambda c, r: (0, c), pipeline_mode=pl.Buffered(buffers))
    tmps = [pltpu.VMEM((seq // _FREE_STRIDE, _PAIR), F32)] * (n_in + len(out_dtypes) if dilation > _FREE_STRIDE else 0)
    return pl.pallas_call(
        body, name=name, grid=(GROUP_DIM // _PAIR, dilation),
        in_specs=[pl.BlockSpec(memory_space=pltpu.SMEM)] + [col] * n_in, out_specs=[col] * len(out_dtypes),
        out_shape=[jax.ShapeDtypeStruct((seq, GROUP_DIM), dt) for dt in out_dtypes], scratch_shapes=scratch + tmps,
        compiler_params=_params(dimension_semantics=("arbitrary", "arbitrary")))


def _staged(dilation, length, rows, sources, scratch):
    if dilation == 1:
        return sources
    for src, dst in zip(sources, scratch):
        dst[...] = src[rows(0, length), :].astype(BF16)
    return scratch


def _attn_fwd(q, k, v, slopes, dilation, name):
    seq = q.shape[0]
    length = seq // dilation
    n_blocks = length // _BQ
    n_stage = 0 if dilation == 1 else 3

    def body(sl_ref, q_ref, k_ref, v_ref, o_ref, lse_ref, *scratch):
        bias_ref, tmps = scratch[n_stage], scratch[n_stage + 1:]
        (q_in, k_in, v_in), (o_out, lse_out), rows, flush = _residue_views(dilation, seq, (q_ref, k_ref, v_ref), (o_ref, lse_ref), tmps)
        qs, ks, vs = _staged(dilation, length, rows, (q_in, k_in, v_in), scratch[:n_stage])
        _fill_bias(bias_ref, sl_ref, dilation)

        def block(i, carry):
            q0, ws, which = _block_window(i, n_blocks, length)
            kw = ks[pl.ds(ws, _KW), :]
            vw = vs[pl.ds(ws, _KW), :]
            sc = _dot_nt(_stack_heads(qs[pl.ds(q0, _BQ), :]), kw) + bias_ref[which]
            m = jnp.max(sc, axis=-1, keepdims=True)
            p = jnp.exp(sc - m)
            den = jnp.sum(p, axis=-1, keepdims=True)
            o_out[rows(q0, _BQ), :] = _unstack_heads(_dot(p.astype(BF16), vw) / den)
            lse_out[rows(q0, _BQ), :] = _unstack_heads(jnp.broadcast_to(m + jnp.log(den), (2 * _BQ, _PAIR)))
            return carry

        lax.fori_loop(0, n_blocks, block, 0, unroll=min(_ATTN_UNROLL, n_blocks))
        flush()

    stage = pltpu.VMEM((length, _PAIR), BF16)
    bias = pltpu.VMEM((3, 2 * _BQ, _KW), F32)
    return _attn_call(body, name, dilation, seq, 3, [F32, F32], [stage] * n_stage + [bias], 2)(slopes, q, k, v)


def _attn_bwd(q, k, v, do, lse, cterm, slopes, dilation, name):
    seq = q.shape[0]
    length = seq // dilation
    n_blocks = length // _BQ
    n_stage = 0 if dilation == 1 else 4

    def body(sl_ref, q_ref, k_ref, v_ref, do_ref, lse_ref, c_ref, dq_ref, dk_ref, dv_ref, *scratch):
        (dk_acc, dv_acc, bias_ref), tmps = scratch[n_stage:n_stage + 3], scratch[n_stage + 3:]
        (q_in, k_in, v_in, do_in, lse_in, c_in), (dq_out, dk_out, dv_out), rows, flush = _residue_views(
            dilation, seq, (q_ref, k_ref, v_ref, do_ref, lse_ref, c_ref), (dq_ref, dk_ref, dv_ref), tmps)
        all_rows = rows(0, length)
        qs, ks, vs, dos = _staged(dilation, length, rows, (q_in, k_in, v_in, do_in), scratch[:n_stage])
        dk_acc[...] = jnp.zeros_like(dk_acc)
        dv_acc[...] = jnp.zeros_like(dv_acc)
        _fill_bias(bias_ref, sl_ref, dilation)

        def block(i, carry):
            q0, ws, which = _block_window(i, n_blocks, length)
            qm = _stack_heads(qs[pl.ds(q0, _BQ), :])
            dom = _stack_heads(dos[pl.ds(q0, _BQ), :])
            kw = ks[pl.ds(ws, _KW), :]
            vw = vs[pl.ds(ws, _KW), :]
            p = jnp.exp(_dot_nt(qm, kw) + bias_ref[which] - _stack_cols(lse_in[rows(q0, _BQ), :]))
            ds = (p * (_dot_nt(dom, vw) + _stack_cols(c_in[rows(q0, _BQ), :]))).astype(BF16)
            dq_out[rows(q0, _BQ), :] = (_unstack_heads(_dot(ds, kw)) * _SCORE_SCALE).astype(dq_out.dtype)
            dk_acc[pl.ds(ws, _KW), :] += _dot_tn(ds, qm)
            dv_acc[pl.ds(ws, _KW), :] += _dot_tn(p.astype(BF16), dom)
            return carry

        lax.fori_loop(0, n_blocks, block, 0, unroll=min(_ATTN_UNROLL, n_blocks))
        dk_out[all_rows, :] = dk_acc[...].astype(dk_out.dtype)
        dv_out[all_rows, :] = dv_acc[...].astype(dv_out.dtype)
        flush()

    stage = pltpu.VMEM((length, _PAIR), BF16)
    acc = pltpu.VMEM((length, _PAIR), F32)
    bias = pltpu.VMEM((3, 2 * _BQ, _KW), F32)
    return _attn_call(body, name, dilation, seq, 6, [_attn_dtype(dilation)] * 3, [stage] * n_stage + [acc] * 2 + [bias],
                      2 if dilation == 1 else 1)(slopes, q, k, v, do, lse, cterm)


def _group_weights(lses):
    m = jnp.maximum(jnp.maximum(lses[0], lses[1]), lses[2])
    es = [jnp.exp(l - m) for l in lses]
    den = es[0] + es[1] + es[2]
    return [e / den for e in es]


def _out_fwd(a_pool, outs, lses, x, w_out, g, name, tm=1024):
    s, d = x.shape
    width = POOL_DIM + 3 * GROUP_DIM

    def body(ap_ref, o0, o1, o2, l0, l1, l2, x_ref, w_ref, g_ref, xo_ref, mix_ref, cat_ref):
        alphas = _group_weights([l0[...], l1[...], l2[...]])
        cat = jnp.concatenate([ap_ref[...]] + [(o[...] * al).astype(BF16) for o, al in zip((o0, o1, o2), alphas)], axis=1)
        cat_ref[...] = cat
        mix = _dot(cat, w_ref[...])
        mix_ref[...] = mix
        xo_ref[...] = x_ref[...] + mix * _inv_rms(mix) * g_ref[...]

    return pl.pallas_call(
        body, name=name, grid=(s // tm,),
        in_specs=[_rows(tm, POOL_DIM)] + [_rows(tm, GROUP_DIM)] * 6 + [_rows(tm, d), _resident(w_out.shape), _const((1, d))],
        out_specs=[_rows(tm, d), _rows(tm, d), _rows(tm, width)],
        out_shape=[jax.ShapeDtypeStruct((s, d), F32), jax.ShapeDtypeStruct((s, d), F32), jax.ShapeDtypeStruct((s, width), BF16)],
        compiler_params=_params(dimension_semantics=("arbitrary",)))(a_pool, *outs, *lses, x, w_out, g)


def _out_bwd(dxo, mix, outs, lses, w_out, g, head_ones, name, tm=1024):
    s, d = mix.shape

    def body(dxo_ref, mix_ref, o0, o1, o2, l0, l1, l2, w_ref, g_ref, ones_ref, dpool_ref, dmix_ref, do0, do1, do2, c0, c1, c2, dg_ref):
        mv = mix_ref[...]
        dmix, dg = _rms_bwd(mv, _inv_rms(mv), g_ref[...], dxo_ref[...])
        dmb = dmix.astype(BF16)
        dmix_ref[...] = dmb
        _accumulate(dg_ref, dg)
        dcat = _dot_nt(dmb, w_ref[...])
        dpool_ref[...] = dcat[:, :POOL_DIM]
        alphas = _group_weights([l0[...], l1[...], l2[...]])
        das = [dcat[:, POOL_DIM + GROUP_DIM * j:POOL_DIM + GROUP_DIM * (j + 1)] for j in range(3)]
        prod = sum(da * (o[...] * al) for da, o, al in zip(das, (o0, o1, o2), alphas))
        hi = prod.astype(BF16)
        lo = (prod - hi.astype(F32)).astype(BF16)
        total = _dot(hi, ones_ref[...]) + _dot(lo, ones_ref[...])
        for da, al, do_ref, c_ref in zip(das, alphas, (do0, do1, do2), (c0, c1, c2)):
            do_ref[...] = (da * al).astype(do_ref.dtype)
            c_ref[...] = -al * total

    return pl.pallas_call(
        body, name=name, grid=(s // tm,),
        in_specs=[_rows(tm, d), _rows(tm, d)] + [_rows(tm, GROUP_DIM)] * 6 + [_resident(w_out.shape), _const((1, d)),
                                                                             _const((GROUP_DIM, GROUP_DIM))],
        out_specs=[_rows(tm, POOL_DIM), _rows(tm, d)] + [_rows(tm, GROUP_DIM)] * 6 + [_const((1, d))],
        out_shape=[jax.ShapeDtypeStruct((s, POOL_DIM), F32), jax.ShapeDtypeStruct((s, d), BF16)]
        + [jax.ShapeDtypeStruct((s, GROUP_DIM), _attn_dtype(dil)) for dil in DILATIONS]
        + [jax.ShapeDtypeStruct((s, GROUP_DIM), F32)] * 3 + [jax.ShapeDtypeStruct((1, d), F32)],
        compiler_params=_params(dimension_semantics=("arbitrary",)))(dxo, mix, *outs, *lses, w_out, g, head_ones)


def _alibi_slopes():
    return np.array([2.0 ** (-8.0 * (i + 1) / N_ATTN_HEADS) for i in range(N_ATTN_HEADS)], np.float32)


def _block_diag(w_lin):
    n, c, _ = w_lin.shape
    eye = jnp.eye(n, dtype=w_lin.dtype)
    return (eye[:, None, :, None] * w_lin[:, :, None, :]).reshape(n * c, n * c)


class _NoExchange:
    def __init__(self, full):
        self.full, self.grads = full, {}

    def first_weights(self):
        return self.full

    def riders(self, host):
        return []

    def landed(self, host, results):
        return self.full

    def gradient(self, name, grad):
        self.grads[name] = grad


def _local_step(x, target, small, exchange):
    s, d = x.shape
    slopes = _alibi_slopes()
    group_slopes = [jnp.asarray(slopes[4 * g:4 * g + 4]) for g in range(3)]
    w_bd = _block_diag(small["w_pool_lin"]).astype(BF16)
    head_ones = jnp.asarray(np.kron(np.eye(GROUP_DIM // HEAD_DIM), np.ones((HEAD_DIM, HEAD_DIM))), BF16)

    full = dict(exchange.first_weights())

    def hosted(call, host, *args):
        results, riding = call(*args, host, exchange.riders(host))
        full.update(exchange.landed(host, riding) or {})
        return results

    x1, a1, b1, f1 = hosted(_ffn_fwd, "ffn1_fwd", x, small["g_ffn1_pre"], full["w1_gate"], full["w1_up"], full["w1_down"],
                            small["g_ffn1_post"], None)
    h2, u, *parts = hosted(_in_fwd, "in_fwd", x1, small["g_mix_pre"], full["w_in"])
    qs, ks, vs = parts[0:3], parts[3:6], parts[6:9]
    a_pool = _pool_fwd(u, w_bd, small["pool_scale"], "pool_fwd")
    outs, lses = [], []
    for g, dil in enumerate(DILATIONS):
        o, lse = _attn_fwd(qs[g], ks[g], vs[g], group_slopes[g], dil, f"attn_fwd{g}")
        outs.append(o)
        lses.append(lse)
    x2, mix, cat = _out_fwd(a_pool, outs, lses, x1, full["w_out"], small["g_mix_post"], "out_fwd")
    (dx3, a2, b2, f2, loss_part), _ = _ffn_fwd(x2, small["g_ffn2_pre"], full["w2_gate"], full["w2_up"], full["w2_down"],
                                               small["g_ffn2_post"], target, "ffn2_fwd")

    small_grads = {}

    def ffn_backward(tag, dxo, x_in, f, a, b):
        n = tag[-1]
        dx, hh, da, db, df, h, dg_pre, dg_post = hosted(
            _ffn_bwd, f"{tag}_bwd", dxo, x_in, f, a, b, small[f"g_{tag}_pre"], small[f"g_{tag}_post"],
            full[f"w{n}_gate"], full[f"w{n}_up"], full[f"w{n}_down"])
        for part, lhs, rhs in (("down", hh, df), ("gate", da, h), ("up", db, h)):
            exchange.gradient(f"w{n}_{part}", hosted(_wgrad, f"{tag}_wgrad_{part}", lhs, rhs))
        small_grads[f"g_{tag}_pre"], small_grads[f"g_{tag}_post"] = dg_pre, dg_post
        return dx

    dx2 = ffn_backward("ffn2", dx3, x2, f2, a2, b2)
    dpool, dmix, *dos_cs, small_grads["g_mix_post"] = _out_bwd(dx2, mix, outs, lses, full["w_out"], small["g_mix_post"],
                                                               head_ones, "out_bwd")
    dos, cs = dos_cs[:3], dos_cs[3:]
    dqs, dks, dvs = [], [], []
    for g, dil in enumerate(DILATIONS):
        dq, dk, dv = _attn_bwd(qs[g], ks[g], vs[g], dos[g], lses[g], cs[g], group_slopes[g], dil, f"attn_bwd{g}")
        dqs.append(dq)
        dks.append(dk)
        dvs.append(dv)
    du, dw_bd, small_grads["pool_scale"] = _pool_bwd(u, dpool, w_bd, small["pool_scale"], "pool_bwd")
    n_pool = len(POOL_HALF_WINDOWS)
    small_grads["w_pool_lin"] = jnp.stack(
        [dw_bd[HEAD_DIM * g:HEAD_DIM * (g + 1), HEAD_DIM * g:HEAD_DIM * (g + 1)] for g in range(n_pool)])
    dx1, dz, small_grads["g_mix_pre"] = hosted(_in_bwd, "in_bwd", du, dqs + dks + dvs, x1, dx2, small["g_mix_pre"], full["w_in"])
    exchange.gradient("w_in", hosted(_wgrad, "wgrad_in", dz, h2))
    dx0 = ffn_backward("ffn1", dx1, x, f1, a1, b1)
    exchange.gradient("w_out", hosted(_wgrad, "wgrad_out", cat, dmix))
    return loss_part[0, 0], dx0, small_grads


SEGMENTS = ("w1_gate", "w1_up", "w1_down", "w_in", "w_out", "w2_gate", "w2_up", "w2_down")
TRANSPOSED = ("w1_gate", "w1_up", "w_in", "w2_gate", "w2_up")
ROWS_OUTSIDE = ("w1_gate", "w1_up", "w2_gate", "w2_up")
HALF = 512


def _place():
    x, y, c = lax.axis_index("x"), lax.axis_index("y"), lax.axis_index("c")
    other_chips = [(1 - x, y), (x, 1 - y), (1 - x, 1 - y)]
    return x, y, c, other_chips


def _chip_rows(chip, rows):
    return pl.ds(pl.multiple_of((2 * chip[0] + chip[1]) * rows, 16), rows)


def _cols(c):
    return pl.ds(pl.multiple_of(c * HALF, HALF), HALF)


def _cast_shards(shards, transposed, place, name):
    n = len(shards)
    rows = [w.shape[1] if t else w.shape[0] for w, t in zip(shards, transposed)]

    def body(place_ref, *refs):
        for w_ref, o_ref, t in zip(refs[:n], refs[n:], transposed):
            o_ref[...] = (w_ref[...].T if t else w_ref[...]).astype(BF16)

    once = pl.Buffered(1)
    return pl.pallas_call(
        body, name=name,
        grid_spec=pltpu.PrefetchScalarGridSpec(
            num_scalar_prefetch=1, grid=(1,),
            in_specs=[pl.BlockSpec(w.shape, lambda i, place: (0, 0), pipeline_mode=once) for w in shards],
            out_specs=[pl.BlockSpec((r, 1024), lambda i, place: (place[0], 0), pipeline_mode=once) for r in rows]),
        out_shape=[jax.ShapeDtypeStruct((N_CHIPS * r, 1024), BF16) for r in rows],
        compiler_params=_params(dimension_semantics=("arbitrary",)))(place, *shards)


def _gather_weights(bufs):
    n = len(bufs)
    rows = [b.shape[0] // N_CHIPS for b in bufs]

    def body(*refs):
        outs = refs[n:2 * n]
        send_sems, recv_sems, fwd_send_sems, fwd_recv_sems = refs[2 * n:]
        x, y, c, chips = _place()
        me = (x, y)

        def ici(j, k, src_chip, to):
            blk = outs[k].at[_chip_rows(src_chip, rows[k]), _cols(c)]
            return pltpu.make_async_remote_copy(src_ref=blk, dst_ref=blk, send_sem=send_sems.at[j, k], recv_sem=recv_sems.at[j, k],
                                                device_id=to, device_id_type=MESH)

        def d2d(j, k, src_chip, half):
            blk = outs[k].at[_chip_rows(src_chip, rows[k]), _cols(half)]
            return pltpu.make_async_remote_copy(src_ref=blk, dst_ref=blk, send_sem=fwd_send_sems.at[j, k],
                                                recv_sem=fwd_recv_sems.at[j, k], device_id=(x, y, 1 - c), device_id_type=MESH)

        sends = [ici(j, k, me, (*chip, c)) for j, chip in enumerate(chips) for k in range(n)]
        for cp in sends:
            cp.start()
        forwards = []
        for j, chip in enumerate(chips):
            for k in range(n):
                ici(j, k, chip, (x, y, c)).wait_recv()
                fw = d2d(j, k, chip, c)
                fw.start()
                forwards.append(fw)
        for j, chip in enumerate(chips):
            for k in range(n):
                d2d(j, k, chip, 1 - c).wait_recv()
        for cp in sends + forwards:
            cp.wait_send()

    any_spec = pl.BlockSpec(memory_space=pl.ANY)
    return pl.pallas_call(
        body, name="gather_weights", in_specs=[any_spec] * n, out_specs=[any_spec] * n,
        out_shape=[jax.ShapeDtypeStruct(b.shape, b.dtype) for b in bufs], input_output_aliases={k: k for k in range(n)},
        scratch_shapes=[pltpu.SemaphoreType.DMA((3, n)), pltpu.SemaphoreType.DMA((3, n)),
                        pltpu.SemaphoreType.DMA((3, n)), pltpu.SemaphoreType.DMA((3, n))])(*bufs)


def _gather_rider(bufs):
    n = len(bufs)
    rows = [b.shape[0] // N_CHIPS for b in bufs]

    def copies(outs, send_sems, recv_sems, inbound):
        x, y, c, chips = _place()
        for j, chip in enumerate(chips):
            for k in range(n):
                src_chip = chip if inbound else (x, y)
                blk = outs[k].at[_chip_rows(src_chip, rows[k]), _cols(c)]
                yield pltpu.make_async_remote_copy(src_ref=blk, dst_ref=blk, send_sem=send_sems.at[j, k], recv_sem=recv_sems.at[j, k],
                                                   device_id=(*chip, c), device_id_type=MESH)

    def start(ins, outs, send_sems, recv_sems):
        for cp in copies(outs, send_sems, recv_sems, False):
            cp.start()

    def wait(ins, outs, send_sems, recv_sems):
        for cp in copies(outs, send_sems, recv_sems, True):
            cp.wait_recv()
        for cp in copies(outs, send_sems, recv_sems, False):
            cp.wait_send()

    return _Rider(list(bufs), None, (3, n), start, wait)


def _forward_rider(bufs):
    n = len(bufs)
    rows = [b.shape[0] // N_CHIPS for b in bufs]

    def copies(outs, send_sems, recv_sems, half):
        x, y, c, chips = _place()
        for j, chip in enumerate(chips):
            for k in range(n):
                blk = outs[k].at[_chip_rows(chip, rows[k]), _cols(half(c))]
                yield pltpu.make_async_remote_copy(src_ref=blk, dst_ref=blk, send_sem=send_sems.at[j, k], recv_sem=recv_sems.at[j, k],
                                                   device_id=(x, y, 1 - c), device_id_type=MESH)

    def start(ins, outs, send_sems, recv_sems):
        for cp in copies(outs, send_sems, recv_sems, lambda c: c):
            cp.start()

    def wait(ins, outs, send_sems, recv_sems):
        for cp in copies(outs, send_sems, recv_sems, lambda c: 1 - c):
            cp.wait_recv()
        for cp in copies(outs, send_sems, recv_sems, lambda c: c):
            cp.wait_send()

    return _Rider(list(bufs), None, (3, n), start, wait)


def _sibling_rider(grads):
    n = len(grads)

    def copies(ins, outs, send_sems, recv_sems):
        x, y, c, _ = _place()
        return [pltpu.make_async_remote_copy(src_ref=ins[k].at[:, pl.ds(1 - c, 1)], dst_ref=outs[k], send_sem=send_sems.at[k],
                                             recv_sem=recv_sems.at[k], device_id=(x, y, 1 - c), device_id_type=MESH)
                for k in range(n)]

    def start(*refs):
        for cp in copies(*refs):
            cp.start()

    def wait(*refs):
        for cp in copies(*refs):
            cp.wait()

    return _Rider(list(grads), [jax.ShapeDtypeStruct((N_CHIPS, 1) + g.shape[2:], F32) for g in grads], (n,), start, wait)


def _alone(rider, name):
    n = len(rider.operands)
    landing = rider.landing if rider.landing is not None else [jax.ShapeDtypeStruct(a.shape, a.dtype) for a in rider.operands]
    n_out = len(landing)

    def body(*refs):
        rider.start(refs[:n], refs[n:n + n_out], *refs[n + n_out:])
        rider.wait(refs[:n], refs[n:n + n_out], *refs[n + n_out:])

    any_spec = pl.BlockSpec(memory_space=pl.ANY)
    return pl.pallas_call(body, name=name, in_specs=[any_spec] * n, out_specs=[any_spec] * n_out, out_shape=landing,
                          input_output_aliases={i: i for i in range(n)} if rider.landing is None else {},
                          scratch_shapes=[pltpu.SemaphoreType.DMA(rider.sems)] * 2)(*rider.operands)


def _chip_sum(grad, from_sibling, place, name):
    rh, width = grad.shape[2:]

    def body(place_ref, g_ref, s_ref, own_ref, all_ref):
        all_ref[...] = (g_ref[...] + s_ref[...]).astype(BF16)
        mine = place_ref[0]
        own_ref[0] = g_ref[mine, 0] + s_ref[mine, 0]

    blk = (N_CHIPS, 1, rh, width)
    once = pl.Buffered(1)
    return pl.pallas_call(
        body, name=name,
        grid_spec=pltpu.PrefetchScalarGridSpec(
            num_scalar_prefetch=1, grid=(1,),
            in_specs=[pl.BlockSpec(blk, lambda i, place: (0, place[1], 0, 0), pipeline_mode=once),
                      pl.BlockSpec(blk, lambda i, place: (0, 0, 0, 0), pipeline_mode=once)],
            out_specs=[pl.BlockSpec((1, rh, width), lambda i, place: (0, 0, 0), pipeline_mode=once),
                       pl.BlockSpec(blk, lambda i, place: (0, 0, 0, 0), pipeline_mode=once)]),
        out_shape=[jax.ShapeDtypeStruct((1, rh, width), F32), jax.ShapeDtypeStruct((N_CHIPS, 1, rh, width), BF16)],
        compiler_params=_params(dimension_semantics=("arbitrary",)))(place, grad, from_sibling)


def _scatter_rider(sums):
    n = len(sums)

    def copies(ins, outs, send_sems, recv_sems):
        x, y, c, chips = _place()
        return [pltpu.make_async_remote_copy(src_ref=ins[k].at[pl.ds(2 * chip[0] + chip[1], 1)], dst_ref=outs[k].at[pl.ds(j, 1)],
                                             send_sem=send_sems.at[j, k], recv_sem=recv_sems.at[j, k],
                                             device_id=(*chip, c), device_id_type=MESH)
                for j, chip in enumerate(chips) for k in range(n)]

    def start(*refs):
        for cp in copies(*refs):
            cp.start()

    def wait(*refs):
        for cp in copies(*refs):
            cp.wait()

    return _Rider(list(sums), [jax.ShapeDtypeStruct((3,) + sm.shape[1:], BF16) for sm in sums], (3, n), start, wait)


def _total_sums(owns, received, name):
    n = len(owns)

    def body(*refs):
        for o_ref, r_ref, t_ref in zip(refs[:n], refs[n:2 * n], refs[2 * n:]):
            total = o_ref[0]
            for j in range(3):
                total = total + r_ref[j, 0].astype(F32)
            t_ref[0] = total

    return pl.pallas_call(body, name=name, out_shape=[jax.ShapeDtypeStruct(o.shape, F32) for o in owns],
                          compiler_params=_params())(*owns, *received)


def _swap_halves(halves):
    n = len(halves)

    def body(*refs):
        ins, outs = refs[:n], refs[n:2 * n]
        send_sems, recv_sems = refs[2 * n:]
        x, y, c, _ = _place()
        copies = [pltpu.make_async_remote_copy(src_ref=ins[k], dst_ref=outs[k], send_sem=send_sems.at[k],
                                               recv_sem=recv_sems.at[k], device_id=(x, y, 1 - c), device_id_type=MESH)
                  for k in range(n)]
        for cp in copies:
            cp.start()
        for cp in copies:
            cp.wait()

    any_spec = pl.BlockSpec(memory_space=pl.ANY)
    return pl.pallas_call(
        body, name="swap_halves", in_specs=[any_spec] * n, out_specs=[any_spec] * n,
        out_shape=[jax.ShapeDtypeStruct(h.shape, F32) for h in halves],
        scratch_shapes=[pltpu.SemaphoreType.DMA((n,)), pltpu.SemaphoreType.DMA((n,))])(*halves)


N_DEV = 8


def _gather_small(block):
    m_per, width = block.shape

    def body(x_ref, out_ref, send_sems, recv_sems, local_sem):
        x, y, c, chips = _place()
        me, sibling = (x, y, c), (x, y, 1 - c)

        def rows(px, py, pc):
            return out_ref.at[pl.ds((4 * px + 2 * py + pc) * m_per, m_per), :]

        def copy(k, blk, to, src=None):
            return pltpu.make_async_remote_copy(src_ref=rows(*blk) if src is None else src, dst_ref=rows(*blk),
                                                send_sem=send_sems.at[k], recv_sem=recv_sems.at[k], device_id=to, device_id_type=MESH)

        mine = pltpu.make_async_copy(x_ref, rows(*me), local_sem)
        mine.start()
        first = [copy(0, me, sibling, src=x_ref)] + [copy(1 + j, me, (*chip, c), src=x_ref) for j, chip in enumerate(chips)]
        for cp in first:
            cp.start()
        passed = [copy(4 + j, (*chip, c), sibling) for j, chip in enumerate(chips)]
        for j, chip in enumerate(chips):
            copy(1 + j, (*chip, c), me).wait_recv()
            passed[j].start()
        copy(0, sibling, me).wait_recv()
        for j, chip in enumerate(chips):
            copy(4 + j, (*chip, 1 - c), me).wait_recv()
        for cp in first + passed:
            cp.wait_send()
        mine.wait()

    vmem = pl.BlockSpec(memory_space=pltpu.VMEM)
    return pl.pallas_call(body, name="gather_small", out_shape=jax.ShapeDtypeStruct((N_DEV * m_per, width), F32),
                          in_specs=[vmem], out_specs=vmem,
                          scratch_shapes=[pltpu.SemaphoreType.DMA((7,)), pltpu.SemaphoreType.DMA((7,)),
                                          pltpu.SemaphoreType.DMA])(block)


def _adamw_math(w, g, m, v):
    m = ADAM_B1 * m + (1.0 - ADAM_B1) * g
    v = ADAM_B2 * v + (1.0 - ADAM_B2) * (g * g)
    m_hat = m / (1.0 - ADAM_B1 ** ADAM_STEP)
    v_hat = v / (1.0 - ADAM_B2 ** ADAM_STEP)
    delta = -ADAM_LR * (m_hat / (jnp.sqrt(v_hat) + ADAM_EPS) + ADAM_WD * w)
    return delta, m, v


def _adamw(w, mine, siblings, place, m, v, transposed, name):
    if transposed:
        def body(place_ref, w_ref, mine_ref, sib_ref, m_ref, v_ref, go_ref, d_ref, mo_ref, vo_ref):
            first = place_ref[1] == 0
            g = jnp.concatenate([jnp.where(first, mine_ref[0], sib_ref[0]), jnp.where(first, sib_ref[0], mine_ref[0])], axis=0).T
            go_ref[...] = g
            d_ref[...], mo_ref[...], vo_ref[...] = _adamw_math(w_ref[...], g, m_ref[...], v_ref[...])

        vmem = pl.BlockSpec(memory_space=pltpu.VMEM)
        return pl.pallas_call(body, name=name, in_specs=[pl.BlockSpec(memory_space=pltpu.SMEM)] + [vmem] * 5, out_specs=[vmem] * 4,
                              out_shape=[jax.ShapeDtypeStruct(w.shape, F32)] * 4, compiler_params=_params())(
                                  place, w, mine, siblings, m, v)

    rh, width = mine.shape[1:]

    def body(place_ref, w_ref, mine_ref, sib_ref, m_ref, v_ref, go_ref, d_ref, mo_ref, vo_ref):
        g = jnp.where(pl.program_id(0) == place_ref[1], mine_ref[0], sib_ref[0])
        go_ref[...] = g
        d_ref[...], mo_ref[...], vo_ref[...] = _adamw_math(w_ref[...], g, m_ref[...], v_ref[...])

    half = pl.BlockSpec((rh, width), lambda h, place: (h, 0))
    whole = pl.BlockSpec((1, rh, width), lambda h, place: (0, 0, 0))
    return pl.pallas_call(
        body, name=name,
        grid_spec=pltpu.PrefetchScalarGridSpec(num_scalar_prefetch=1, grid=(2,), in_specs=[half, whole, whole, half, half],
                                               out_specs=[half] * 4),
        out_shape=[jax.ShapeDtypeStruct(w.shape, F32)] * 4,
        compiler_params=_params(dimension_semantics=("arbitrary",)))(place, w, mine, siblings, m, v)


def _adamw_small(gathered, w, m, v, name):
    def body(ga_ref, w_ref, m_ref, v_ref, go_ref, d_ref, mo_ref, vo_ref):
        g = ga_ref[0]
        for dev in range(1, N_DEV):
            g = g + ga_ref[dev]
        go_ref[...] = g
        d_ref[...], mo_ref[...], vo_ref[...] = _adamw_math(w_ref[...], g, m_ref[...], v_ref[...])

    return pl.pallas_call(body, name=name, out_shape=[jax.ShapeDtypeStruct(w.shape, F32)] * 4,
                          compiler_params=_params())(gathered, w, m, v)


class _Exchange:
    FIRST = ("w1_gate", "w1_up", "w1_down")
    HOSTS = {"ffn2_wgrad_gate": (("w2_down",), ()), "ffn2_wgrad_up": (("w2_gate",), ("w2_down",)),
             "in_bwd": (("w2_up",), ("w2_gate",)), "wgrad_in": ((), ("w2_up",)),
             "ffn1_wgrad_down": ((), ("w_in",)), "ffn1_wgrad_gate": (("w1_down",), ()), "ffn1_wgrad_up": ((), ("w1_down", "w1_gate")),
             "wgrad_out": ((), ("w1_up",))}
    ALONE = ("w_in", "w1_gate", "w1_up", "w_out")

    def __init__(self, bufs, place):
        self.bufs, self.place = bufs, place
        self.later = [k for k in SEGMENTS if k not in self.FIRST]
        self.split, self.own, self.to_send, self.received = {}, {}, {}, {}

    def first_weights(self):
        return dict(zip(self.FIRST, _gather_weights([self.bufs[k] for k in self.FIRST])))

    def riders(self, host):
        if host == "ffn1_fwd":
            return [_gather_rider([self.bufs[k] for k in self.later])]
        if host == "in_fwd":
            return [_forward_rider([self.bufs[k] for k in self.later[1:]])]
        halves, sums = self.HOSTS.get(host, ((), ()))
        return ([_sibling_rider([self.split[k] for k in halves])] if halves else []) + (
            [_scatter_rider([self.to_send[k] for k in sums])] if sums else [])

    def landed(self, host, results):
        if host == "ffn1_fwd":
            self.bufs.update(zip(self.later, results[0]))
            return dict(zip(self.later[:1], _alone(_forward_rider([self.bufs[self.later[0]]]), "gather_forward_first")))
        if host == "in_fwd":
            return dict(zip(self.later[1:], results[0]))
        halves, sums = self.HOSTS.get(host, ((), ()))
        if halves:
            self._chip_sums(halves, results[0])
        if sums:
            self.received.update(zip(sums, results[-1]))

    def gradient(self, name, grad):
        self.split[name] = grad.reshape(N_CHIPS, 2, grad.shape[0] // (2 * N_CHIPS), grad.shape[1])
        if name in self.ALONE:
            self._chip_sums([name], _alone(_sibling_rider([self.split[name]]), f"reduce_sibling_{name}"))

    def _chip_sums(self, names, from_sibling):
        for k, fs in zip(names, from_sibling):
            self.own[k], self.to_send[k] = _chip_sum(self.split[k], fs, self.place, f"chip_sum_{k}")

    def summed_halves(self):
        late = [k for k in SEGMENTS if k not in self.received]
        self.received.update(zip(late, _alone(_scatter_rider([self.to_send[k] for k in late]), "reduce_chips_last")))
        return _total_sums([self.own[k] for k in SEGMENTS], [self.received[k] for k in SEGMENTS], "total_sums")


SMALL = ("g_ffn1_pre", "g_ffn1_post", "g_mix_pre", "w_pool_lin", "pool_scale", "g_mix_post", "g_ffn2_pre", "g_ffn2_post")
WEIGHTS = ("g_ffn1_pre", "w1_gate", "w1_up", "w1_down", "g_ffn1_post", "g_mix_pre", "w_in", "w_pool_lin", "pool_scale", "w_out",
           "g_mix_post", "g_ffn2_pre", "w2_gate", "w2_up", "w2_down", "g_ffn2_post")
LANES = 128


def _pack_small(tree, extra=0.0):
    flat = jnp.concatenate([tree[k].reshape(-1) for k in SMALL] + [jnp.reshape(extra, (1,)).astype(F32)])
    rows = -(-flat.shape[0] // (8 * LANES)) * 8
    return jnp.pad(flat, (0, rows * LANES - flat.shape[0])).reshape(rows, LANES)


def _unpack_small(packed, like):
    flat, out, at = packed.reshape(-1), {}, 0
    for k in SMALL:
        size = math.prod(like[k].shape)
        out[k] = flat[at:at + size].reshape(like[k].shape)
        at += size
    return out


def kernel(x, g_ffn1_pre, w1_gate, w1_up, w1_down, g_ffn1_post, g_mix_pre, w_in, w_pool_lin, pool_scale, w_out, g_mix_post, g_ffn2_pre, w2_gate, w2_up, w2_down, g_ffn2_post, loss_target, m_g_ffn1_pre, m_w1_gate, m_w1_up, m_w1_down, m_g_ffn1_post, m_g_mix_pre, m_w_in, m_w_pool_lin, m_pool_scale, m_w_out, m_g_mix_post, m_g_ffn2_pre, m_w2_gate, m_w2_up, m_w2_down, m_g_ffn2_post, v_g_ffn1_pre, v_w1_gate, v_w1_up, v_w1_down, v_g_ffn1_post, v_g_mix_pre, v_w_in, v_w_pool_lin, v_pool_scale, v_w_out, v_g_mix_post, v_g_ffn2_pre, v_w2_gate, v_w2_up, v_w2_down, v_g_ffn2_post):
    given = dict(locals())
    w = {k: given[k] for k in WEIGHTS}
    m = {k: given["m_" + k] for k in WEIGHTS}
    v = {k: given["v_" + k] for k in WEIGHTS}
    small = {k: (w[k][0] if k == "w_pool_lin" else w[k].reshape(1, -1)) for k in SMALL}

    place = jnp.stack([2 * lax.axis_index("x") + lax.axis_index("y"), lax.axis_index("c")]).astype(jnp.int32)
    def as_rows(a, k):
        return jnp.swapaxes(a, 1, 2)[0] if k in ROWS_OUTSIDE else a[0]

    def as_given(a, k):
        return jnp.swapaxes(a[None], 1, 2) if k in ROWS_OUTSIDE else a[None]

    in_kernel = [k for k in TRANSPOSED if k not in ROWS_OUTSIDE]
    bufs = {}
    for tag, names in (("first", _Exchange.FIRST), ("rest", [k for k in SEGMENTS if k not in _Exchange.FIRST])):
        bufs.update(zip(names, _cast_shards([as_rows(w[k], k) for k in names], [k in in_kernel for k in names], place, f"cast_{tag}")))
    exchange = _Exchange(bufs, place)
    loss_part, grad_x, small_grads = _local_step(x[0], loss_target[0], small, exchange)

    halves = exchange.summed_halves()
    from_sibling = _swap_halves(halves)

    out_grad, out_delta, out_m, out_v = {}, {}, {}, {}
    for k, mine, sib in zip(SEGMENTS, halves, from_sibling):
        out_grad[k], out_delta[k], out_m[k], out_v[k] = (
            as_given(a, k) for a in _adamw(as_rows(w[k], k), mine, sib, place, as_rows(m[k], k), as_rows(v[k], k),
                                           k in in_kernel, f"adamw_{k}"))

    small_grads["w_pool_lin"] = small_grads["w_pool_lin"][None]
    packed = _pack_small(small_grads, loss_part)
    gathered = _gather_small(packed).reshape(N_DEV, *packed.shape)
    like = {k: w[k] for k in SMALL}
    results = _adamw_small(gathered, _pack_small(like), _pack_small({k: m[k] for k in SMALL}),
                           _pack_small({k: v[k] for k in SMALL}), "adamw_small")
    for tree, res in zip((out_grad, out_delta, out_m, out_v), results):
        tree.update(_unpack_small(res, like))
    loss = results[0].reshape(-1)[sum(math.prod(like[k].shape) for k in SMALL)]

    return (loss, grad_x[None], *[out_grad[k] for k in WEIGHTS], *[out_delta[k] for k in WEIGHTS],
            *[out_m[k] for k in WEIGHTS], *[out_v[k] for k in WEIGHTS])
```

```python
import math
import typing

import numpy as np
import jax
import jax.numpy as jnp
from jax import lax
from jax.experimental import pallas as pl
from jax.experimental.pallas import tpu as pltpu

F32 = jnp.float32
BF16 = jnp.bfloat16
MESH = pl.DeviceIdType.MESH

RMS_EPS = 1e-6
HEAD_DIM = 64
POOL_HALF_WINDOWS = (1, 2, 4, 8)
POOL_DIM = 256
GROUP_DIM = 256
DILATIONS = (1, 4, 16)
N_SIDE = 64
N_ATTN_HEADS = 12
ADAM_LR, ADAM_B1, ADAM_B2, ADAM_EPS, ADAM_WD, ADAM_STEP = 0.001, 0.9, 0.999, 1e-08, 0.01, 10

N_CHIPS = 4
V7X_VMEM_LIMIT = 60 * 1024 * 1024

_NT = (((1,), (1,)), ((), ()))
_TN = (((0,), (0,)), ((), ()))


def _dot(a, b):
    return jnp.dot(a, b, preferred_element_type=F32)


def _dot_nt(a, b):
    return lax.dot_general(a, b, _NT, preferred_element_type=F32)


def _dot_tn(a, b):
    return lax.dot_general(a, b, _TN, preferred_element_type=F32)


def _params(**kw):
    return pltpu.CompilerParams(vmem_limit_bytes=V7X_VMEM_LIMIT, **kw)


def _rows(tm, width):
    return pl.BlockSpec((tm, width), lambda i: (i, 0))


def _resident(shape):
    return pl.BlockSpec(shape, lambda i: (0,) * len(shape), pipeline_mode=pl.Buffered(1))


def _const(shape):
    return pl.BlockSpec(shape, lambda i: (0,) * len(shape))


def _inv_rms(x):
    return lax.rsqrt(jnp.mean(x * x, axis=-1, keepdims=True) + RMS_EPS)


def _rms_bwd(x, inv, g, dy):
    n = x * inv
    dn = dy * g
    dx = inv * (dn - n * jnp.mean(dn * n, axis=-1, keepdims=True))
    return dx, jnp.sum(dy * n, axis=0, keepdims=True)


def _accumulate(ref, value):
    @pl.when(pl.program_id(0) == 0)
    def _():
        ref[...] = jnp.zeros_like(ref)

    ref[...] += value


class _Rider(typing.NamedTuple):
    operands: list
    landing: typing.Optional[list]
    sems: tuple
    start: typing.Callable
    wait: typing.Callable


def _hosted_call(body, riders, *, name, steps, in_specs, out_specs, out_shape, args, scratch_shapes=()):
    params = _params(dimension_semantics=("arbitrary",))
    riders = list(riders or [])
    if not riders:
        res = pl.pallas_call(body, name=name, grid=(steps,), in_specs=in_specs, out_specs=out_specs, out_shape=out_shape,
                             scratch_shapes=list(scratch_shapes), compiler_params=params)(*args)
        return list(res), []
    n_in, n_out, n_scratch = len(in_specs), len(out_specs), len(scratch_shapes)
    operands, landing, aliases, spans = [], [], {}, []
    for rd in riders:
        lands = rd.landing if rd.landing is not None else [jax.ShapeDtypeStruct(a.shape, a.dtype) for a in rd.operands]
        if rd.landing is None:
            aliases.update({n_in + len(operands) + i: n_out + len(landing) + i for i in range(len(lands))})
        spans.append((len(operands), len(rd.operands), len(landing), len(lands)))
        operands += rd.operands
        landing += lands
    outs_at = n_in + len(operands)
    scratch_at = outs_at + n_out + len(landing)

    def riding(*refs):
        def each(action):
            for i, (rd, (in_at, n_ops, out_at, n_lands)) in enumerate(zip(riders, spans)):
                sems = refs[scratch_at + n_scratch + 2 * i:scratch_at + n_scratch + 2 * i + 2]
                getattr(rd, action)(refs[n_in + in_at:n_in + in_at + n_ops],
                                    refs[outs_at + n_out + out_at:outs_at + n_out + out_at + n_lands], *sems)

        @pl.when(pl.program_id(0) == 0)
        def _():
            each("start")

        body(*refs[:n_in], *refs[outs_at:outs_at + n_out], *refs[scratch_at:scratch_at + n_scratch])

        @pl.when(pl.program_id(0) == steps - 1)
        def _():
            each("wait")

    any_spec = pl.BlockSpec(memory_space=pl.ANY)
    res = pl.pallas_call(
        riding, name=name, grid=(steps,), in_specs=list(in_specs) + [any_spec] * len(operands),
        out_specs=list(out_specs) + [any_spec] * len(landing), out_shape=list(out_shape) + landing,
        scratch_shapes=list(scratch_shapes) + [pltpu.SemaphoreType.DMA(rd.sems) for rd in riders for _ in range(2)],
        input_output_aliases=aliases, compiler_params=params)(*args, *operands)
    return list(res[:n_out]), [list(res[n_out + out_at:n_out + out_at + n_lands]) for _, _, out_at, n_lands in spans]


_SUB_TILE = 256


def _sub_tiles(tm):
    return [pl.ds(r, _SUB_TILE) for r in range(0, tm, _SUB_TILE)]


def _ffn_fwd(x, g_pre, wg_t, wu_t, wd, g_post, target, name, riders=None, tm=512):
    s, d = x.shape
    ff = wd.shape[0]
    with_loss = target is not None

    def body(*refs):
        if with_loss:
            x_ref, gpre_ref, wg_ref, wu_ref, wd_ref, gpost_ref, t_ref, xo_ref, a_ref, b_ref, f_ref, loss_ref = refs
        else:
            x_ref, gpre_ref, wg_ref, wu_ref, wd_ref, gpost_ref, xo_ref, a_ref, b_ref, f_ref = refs
        loss = 0.0
        for rows in _sub_tiles(tm):
            xv = x_ref[rows, :]
            hb = (xv * _inv_rms(xv) * gpre_ref[...]).astype(BF16)
            a = _dot_nt(hb, wg_ref[...])
            b = _dot_nt(hb, wu_ref[...])
            hh = (a * jax.nn.sigmoid(a)) * b
            f = _dot(hh.astype(BF16), wd_ref[...])
            xo = xv + 0.5 * (f * _inv_rms(f) * gpost_ref[...])
            a_ref[rows, :] = a.astype(BF16)
            b_ref[rows, :] = b.astype(BF16)
            f_ref[rows, :] = f
            if with_loss:
                e = xo - t_ref[rows, :]
                xo_ref[rows, :] = e * (1.0 / d)
                loss = loss + 0.5 * jnp.sum(jnp.mean(e * e, axis=-1, keepdims=True))
            else:
                xo_ref[rows, :] = xo
        if with_loss:
            _accumulate(loss_ref, loss)

    in_specs = [_rows(tm, d), _const((1, d)), _resident((ff, d)), _resident((ff, d)), _resident((ff, d)), _const((1, d))]
    args = [x, g_pre, wg_t, wu_t, wd, g_post]
    out_shape = [jax.ShapeDtypeStruct((s, d), F32), jax.ShapeDtypeStruct((s, ff), BF16),
                 jax.ShapeDtypeStruct((s, ff), BF16), jax.ShapeDtypeStruct((s, d), F32)]
    out_specs = [_rows(tm, d), _rows(tm, ff), _rows(tm, ff), _rows(tm, d)]
    if with_loss:
        in_specs.append(_rows(tm, d))
        args.append(target)
        out_shape.append(jax.ShapeDtypeStruct((8, 128), F32))
        out_specs.append(_const((8, 128)))
    return _hosted_call(body, riders, name=name, steps=s // tm, in_specs=in_specs, out_specs=out_specs, out_shape=out_shape, args=args)


def _ffn_bwd(dxo, x, f, a, b, g_pre, g_post, wg_t, wu_t, wd, name, riders=None, tm=256):
    s, d = x.shape
    ff = wd.shape[0]

    def body(dxo_ref, x_ref, f_ref, a_ref, b_ref, gpre_ref, gpost_ref, wg_ref, wu_ref, wd_ref,
             dx_ref, hh_ref, da_ref, db_ref, df_ref, h_ref, dgpre_ref, dgpost_ref):
        dgpre_sum = dgpost_sum = 0.0
        for rows in _sub_tiles(tm):
            dxo_v = dxo_ref[rows, :]
            fv = f_ref[rows, :]
            df, dgpost = _rms_bwd(fv, _inv_rms(fv), gpost_ref[...], 0.5 * dxo_v)
            dfb = df.astype(BF16)
            dhh = _dot_nt(dfb, wd_ref[...])
            av = a_ref[rows, :].astype(F32)
            bv = b_ref[rows, :].astype(F32)
            sig = jax.nn.sigmoid(av)
            sa = av * sig
            da = (dhh * bv * (sig * (1.0 + av * (1.0 - sig)))).astype(BF16)
            db = (dhh * sa).astype(BF16)
            dh = _dot(da, wg_ref[...]) + _dot(db, wu_ref[...])
            xv = x_ref[rows, :]
            inv = _inv_rms(xv)
            dxn, dgpre = _rms_bwd(xv, inv, gpre_ref[...], dh)
            dx_ref[rows, :] = dxo_v + dxn
            hh_ref[rows, :] = (sa * bv).astype(BF16)
            da_ref[rows, :] = da
            db_ref[rows, :] = db
            df_ref[rows, :] = dfb
            h_ref[rows, :] = (xv * inv * gpre_ref[...]).astype(BF16)
            dgpre_sum, dgpost_sum = dgpre_sum + dgpre, dgpost_sum + dgpost
        _accumulate(dgpre_ref, dgpre_sum)
        _accumulate(dgpost_ref, dgpost_sum)

    return _hosted_call(
        body, riders, name=name, steps=s // tm,
        in_specs=[_rows(tm, d), _rows(tm, d), _rows(tm, d), _rows(tm, ff), _rows(tm, ff), _const((1, d)), _const((1, d)),
                  _resident((ff, d)), _resident((ff, d)), _resident((ff, d))],
        out_specs=[_rows(tm, d), _rows(tm, ff), _rows(tm, ff), _rows(tm, ff), _rows(tm, d), _rows(tm, d),
                   _const((1, d)), _const((1, d))],
        out_shape=[jax.ShapeDtypeStruct((s, d), F32), jax.ShapeDtypeStruct((s, ff), BF16), jax.ShapeDtypeStruct((s, ff), BF16),
                   jax.ShapeDtypeStruct((s, ff), BF16), jax.ShapeDtypeStruct((s, d), BF16), jax.ShapeDtypeStruct((s, d), BF16),
                   jax.ShapeDtypeStruct((1, d), F32), jax.ShapeDtypeStruct((1, d), F32)],
        args=[dxo, x, f, a, b, g_pre, g_post, wg_t, wu_t, wd])


def _wgrad(lhs, rhs, name, riders=None, rt=256):
    s, r = lhs.shape
    c = rhs.shape[1]

    def body(l_ref, r_ref, o_ref):
        o_ref[...] = _dot_tn(l_ref[...], r_ref[...])

    (out,), riding = _hosted_call(
        body, riders, name=name, steps=pl.cdiv(r, rt), in_specs=[pl.BlockSpec((s, rt), lambda i: (0, i)), _resident((s, c))],
        out_specs=[pl.BlockSpec((rt, c), lambda i: (i, 0))], out_shape=[jax.ShapeDtypeStruct((r, c), F32)], args=[lhs, rhs])
    return out, riding


def _attn_dtype(dilation):
    return BF16 if dilation == 1 else F32


def _in_fwd(x, g, w_in_t, name, riders=None, tm=1024):
    s, d = x.shape
    d_in = w_in_t.shape[0]
    n_groups = len(DILATIONS)
    dtypes = [_attn_dtype(dil) for dil in DILATIONS] * 3

    def body(x_ref, g_ref, w_ref, h_ref, u_ref, *part_refs):
        xv = x_ref[...]
        hb = (xv * _inv_rms(xv) * g_ref[...]).astype(BF16)
        h_ref[...] = hb
        z = _dot_nt(hb, w_ref[...])
        u_ref[...] = z[:, :POOL_DIM]
        for j, ref in enumerate(part_refs):
            part = z[:, POOL_DIM + GROUP_DIM * j:POOL_DIM + GROUP_DIM * (j + 1)]
            ref[...] = (part * _SCORE_SCALE if j < n_groups else part).astype(ref.dtype)

    return _hosted_call(
        body, riders, name=name, steps=s // tm, in_specs=[_rows(tm, d), _const((1, d)), _resident((d_in, d))],
        out_specs=[_rows(tm, d), _rows(tm, POOL_DIM)] + [_rows(tm, GROUP_DIM)] * len(dtypes),
        out_shape=[jax.ShapeDtypeStruct((s, d), BF16), jax.ShapeDtypeStruct((s, POOL_DIM), F32)]
        + [jax.ShapeDtypeStruct((s, GROUP_DIM), dt) for dt in dtypes],
        args=[x, g, w_in_t])


def _in_bwd(du, dparts, x, dxo, g, w_in_t, name, riders=None, tm=512):
    s, d = x.shape
    d_in = w_in_t.shape[0]
    n_parts = len(dparts)

    def body(du_ref, *refs):
        part_refs = refs[:n_parts]
        x_ref, dxo_ref, g_ref, w_ref, dx_ref, dz_ref, dg_ref = refs[n_parts:]
        dz = jnp.concatenate([r[...].astype(BF16) for r in (du_ref,) + part_refs], axis=1)
        dz_ref[...] = dz
        dh = _dot(dz, w_ref[...])
        xv = x_ref[...]
        dxn, dg = _rms_bwd(xv, _inv_rms(xv), g_ref[...], dh)
        dx_ref[...] = dxo_ref[...] + dxn
        _accumulate(dg_ref, dg)

    return _hosted_call(
        body, riders, name=name, steps=s // tm,
        in_specs=[_rows(tm, POOL_DIM)] + [_rows(tm, GROUP_DIM)] * n_parts + [_rows(tm, d), _rows(tm, d), _const((1, d)),
                                                                             _resident((d_in, d))],
        out_specs=[_rows(tm, d), _rows(tm, d_in), _const((1, d))],
        out_shape=[jax.ShapeDtypeStruct((s, d), F32), jax.ShapeDtypeStruct((s, d_in), BF16), jax.ShapeDtypeStruct((1, d), F32)],
        args=[du, *dparts, x, dxo, g, w_in_t])


_POOL_HALO = 8


def _pool_chain(v, first_shift):
    n = v.shape[0]
    p2 = v + pltpu.roll(v, first_shift, 0)
    p4 = pltpu.roll(p2, 1, 0) + pltpu.roll(p2, n - 1, 0)
    p8 = pltpu.roll(p4, 2, 0) + pltpu.roll(p4, n - 2, 0)
    p16 = pltpu.roll(p8, 4, 0) + pltpu.roll(p8, n - 4, 0)
    group = lax.broadcasted_iota(jnp.int32, v.shape, 1) // HEAD_DIM
    return jnp.where(group == 0, p2, jnp.where(group == 1, p4, jnp.where(group == 2, p8, p16)))


def _pool_count(t0, rows, s):
    t = t0 + lax.broadcasted_iota(jnp.int32, (rows, POOL_DIM), 0)
    group = lax.broadcasted_iota(jnp.int32, (rows, POOL_DIM), 1) // HEAD_DIM
    half = jnp.where(group == 0, 1, jnp.where(group == 1, 2, jnp.where(group == 2, 4, 8)))
    cnt = jnp.minimum(t + half, s) - jnp.maximum(t - half, 0)
    return jnp.maximum(cnt, 1).astype(F32)


def _pad_rows(ref, pad_ref, s):
    zeros = jnp.zeros((_POOL_HALO, pad_ref.shape[1]), pad_ref.dtype)
    pad_ref[pl.ds(0, _POOL_HALO), :] = zeros
    pad_ref[pl.ds(_POOL_HALO + s, _POOL_HALO), :] = zeros
    pad_ref[pl.ds(_POOL_HALO, s), :] = ref[...]


def _pool_fwd(u, w_bd, scale, name, tm=512):
    s = u.shape[0]
    ext = tm + 2 * _POOL_HALO

    def body(u_ref, w_ref, sc_ref, o_ref, upad):
        _pad_rows(u_ref, upad, s)

        def tile(i, carry):
            t0 = pl.multiple_of(i * tm, tm)
            uv = upad[pl.ds(t0, ext), :]
            win = _pool_chain(uv, 1)[_POOL_HALO:_POOL_HALO + tm]
            y = win / _pool_count(t0, tm, s) - uv[_POOL_HALO:_POOL_HALO + tm]
            o_ref[pl.ds(t0, tm), :] = (_dot(y.astype(BF16), w_ref[...]) * sc_ref[...]).astype(BF16)
            return carry

        lax.fori_loop(0, s // tm, tile, 0)

    return pl.pallas_call(body, name=name, out_shape=jax.ShapeDtypeStruct((s, POOL_DIM), BF16),
                          scratch_shapes=[pltpu.VMEM((s + 2 * _POOL_HALO, POOL_DIM), F32)],
                          compiler_params=_params())(u, w_bd, scale)


def _pool_bwd(u, da, w_bd, scale, name, tm=512):
    s = u.shape[0]
    ext = tm + 2 * _POOL_HALO

    def body(u_ref, da_ref, w_ref, sc_ref, du_ref, dw_ref, dsc_ref, upad, dapad):
        _pad_rows(u_ref, upad, s)
        _pad_rows(da_ref, dapad, s)
        dw_ref[...] = jnp.zeros_like(dw_ref)
        dsc_ref[...] = jnp.zeros_like(dsc_ref)

        def tile(i, carry):
            t0 = pl.multiple_of(i * tm, tm)
            uv = upad[pl.ds(t0, ext), :]
            dav = dapad[pl.ds(t0, ext), :]
            win = _pool_chain(uv, 1)[_POOL_HALO:_POOL_HALO + tm]
            yb = (win / _pool_count(t0, tm, s) - uv[_POOL_HALO:_POOL_HALO + tm]).astype(BF16)
            yl = _dot(yb, w_ref[...])
            da_c = dav[_POOL_HALO:_POOL_HALO + tm]
            dsc_ref[...] += jnp.sum(da_c * yl, axis=0, keepdims=True)
            dyl = (dav * sc_ref[...]).astype(BF16)
            dw_ref[...] += _dot_tn(yb, dyl[_POOL_HALO:_POOL_HALO + tm])
            dy = _dot_nt(dyl, w_ref[...])
            dyc = dy / _pool_count(t0 - _POOL_HALO, ext, s)
            du_ref[pl.ds(t0, tm), :] = (_pool_chain(dyc, ext - 1) - dy)[_POOL_HALO:_POOL_HALO + tm]
            return carry

        lax.fori_loop(0, s // tm, tile, 0)

    pool_cols = pl.BlockSpec((s, POOL_DIM), lambda i: (0, 0), pipeline_mode=pl.Buffered(1))
    return pl.pallas_call(
        body, name=name, grid=(1,),
        in_specs=[pool_cols, pool_cols, _const((POOL_DIM, POOL_DIM)), _const((1, POOL_DIM))],
        out_specs=[_const((s, POOL_DIM)), _const((POOL_DIM, POOL_DIM)), _const((1, POOL_DIM))],
        out_shape=[jax.ShapeDtypeStruct((s, POOL_DIM), F32), jax.ShapeDtypeStruct((POOL_DIM, POOL_DIM), F32),
                   jax.ShapeDtypeStruct((1, POOL_DIM), F32)],
        scratch_shapes=[pltpu.VMEM((s + 2 * _POOL_HALO, POOL_DIM), F32), pltpu.VMEM((s + 2 * _POOL_HALO, POOL_DIM), F32)],
        compiler_params=_params(dimension_semantics=("arbitrary",)))(u, da, w_bd, scale)


_BQ = 128
_KW = _BQ + 2 * N_SIDE
_PAIR = 2 * HEAD_DIM
_NEG = -1e30
_ATTN_UNROLL = 8
_SCORE_SCALE = HEAD_DIM ** -0.5


def _stack_heads(x):
    lane_head = lax.broadcasted_iota(jnp.int32, x.shape, 1) // HEAD_DIM
    zero = jnp.zeros_like(x)
    return jnp.concatenate([jnp.where(lane_head == 0, x, zero), jnp.where(lane_head == 1, x, zero)], axis=0)


def _unstack_heads(x):
    lane_head = lax.broadcasted_iota(jnp.int32, (_BQ, _PAIR), 1) // HEAD_DIM
    return jnp.where(lane_head == 0, x[:_BQ], x[_BQ:])


def _stack_cols(x):
    return jnp.concatenate([x[:, 0:1], x[:, HEAD_DIM:HEAD_DIM + 1]], axis=0)


def _fill_bias(bias_ref, slopes_ref, dilation):
    row = lax.broadcasted_iota(jnp.int32, (2 * _BQ, _KW), 0)
    col = lax.broadcasted_iota(jnp.int32, (2 * _BQ, _KW), 1)
    pair = 2 * pl.program_id(0)
    slope = jnp.where(row < _BQ, slopes_ref[pair], slopes_ref[pair + 1]) * float(dilation)

    @pl.when(pl.program_id(1) == 0)
    def _():
        for j in range(3):
            dist = jnp.abs(col - (row & (_BQ - 1)) - j * N_SIDE)
            bias_ref[j] = jnp.where(dist <= N_SIDE, -slope * dist.astype(F32), _NEG)


def _block_window(i, n_blocks, length):
    q0 = pl.multiple_of(i * _BQ, _BQ)
    ws = pl.multiple_of(jnp.clip(q0 - N_SIDE, 0, length - _KW), N_SIDE)
    return q0, ws, jnp.where(i == 0, 0, jnp.where(i == n_blocks - 1, 2, 1))


_FREE_STRIDE = 4


def _residue_views(dilation, seq, ins, outs, tmps):
    step = pl.program_id(1)
    if dilation <= _FREE_STRIDE:
        def rows(start, count):
            return pl.ds(start, count) if dilation == 1 else pl.ds(start * dilation + step, count, stride=dilation)

        return ins, outs, rows, lambda: None
    inner = dilation // _FREE_STRIDE
    assert inner <= _FREE_STRIDE and len(tmps) == len(ins) + len(outs)
    first, second = step // inner, step % inner
    coarse = pl.ds(first, seq // _FREE_STRIDE, stride=_FREE_STRIDE)
    in_tmps, out_tmps = tmps[:len(ins)], tmps[len(ins):]

    @pl.when(second == 0)
    def _():
        for ref, tmp in zip(ins, in_tmps):
            tmp[...] = ref[coarse, :]

    def flush():
        @pl.when(second == inner - 1)
        def _():
            for ref, tmp in zip(outs, out_tmps):
                ref[coarse, :] = tmp[...]

    return in_tmps, out_tmps, lambda start, count: pl.ds(start * inner + second, count, stride=inner), flush


def _attn_call(body, name, dilation, seq, n_in, out_dtypes, scratch, buffers):
    col = pl.BlockSpec((seq, _PAIR), lambda c, r: (0, c), pipeline_mode=pl.Buffered(buffers))
    tmps = [pltpu.VMEM((seq // _FREE_STRIDE, _PAIR), F32)] * (n_in + len(out_dtypes) if dilation > _FREE_STRIDE else 0)
    return pl.pallas_call(
        body, name=name, grid=(GROUP_DIM // _PAIR, dilation),
        in_specs=[pl.BlockSpec(memory_space=pltpu.SMEM)] + [col] * n_in, out_specs=[col] * len(out_dtypes),
        out_shape=[jax.ShapeDtypeStruct((seq, GROUP_DIM), dt) for dt in out_dtypes], scratch_shapes=scratch + tmps,
        compiler_params=_params(dimension_semantics=("arbitrary", "arbitrary")))


def _staged(dilation, length, rows, sources, scratch):
    if dilation == 1:
        return sources
    for src, dst in zip(sources, scratch):
        dst[...] = src[rows(0, length), :].astype(BF16)
    return scratch


def _attn_fwd(q, k, v, slopes, dilation, name):
    seq = q.shape[0]
    length = seq // dilation
    n_blocks = length // _BQ
    n_stage = 0 if dilation == 1 else 3

    def body(sl_ref, q_ref, k_ref, v_ref, o_ref, lse_ref, *scratch):
        bias_ref, tmps = scratch[n_stage], scratch[n_stage + 1:]
        (q_in, k_in, v_in), (o_out, lse_out), rows, flush = _residue_views(dilation, seq, (q_ref, k_ref, v_ref), (o_ref, lse_ref), tmps)
        qs, ks, vs = _staged(dilation, length, rows, (q_in, k_in, v_in), scratch[:n_stage])
        _fill_bias(bias_ref, sl_ref, dilation)

        def block(i, carry):
            q0, ws, which = _block_window(i, n_blocks, length)
            kw = ks[pl.ds(ws, _KW), :]
            vw = vs[pl.ds(ws, _KW), :]
            sc = _dot_nt(_stack_heads(qs[pl.ds(q0, _BQ), :]), kw) + bias_ref[which]
            m = jnp.max(sc, axis=-1, keepdims=True)
            p = jnp.exp(sc - m)
            den = jnp.sum(p, axis=-1, keepdims=True)
            o_out[rows(q0, _BQ), :] = _unstack_heads(_dot(p.astype(BF16), vw) / den)
            lse_out[rows(q0, _BQ), :] = _unstack_heads(jnp.broadcast_to(m + jnp.log(den), (2 * _BQ, _PAIR)))
            return carry

        lax.fori_loop(0, n_blocks, block, 0, unroll=min(_ATTN_UNROLL, n_blocks))
        flush()

    stage = pltpu.VMEM((length, _PAIR), BF16)
    bias = pltpu.VMEM((3, 2 * _BQ, _KW), F32)
    return _attn_call(body, name, dilation, seq, 3, [F32, F32], [stage] * n_stage + [bias], 2)(slopes, q, k, v)


def _attn_bwd(q, k, v, do, lse, cterm, slopes, dilation, name):
    seq = q.shape[0]
    length = seq // dilation
    n_blocks = length // _BQ
    n_stage = 0 if dilation == 1 else 4

    def body(sl_ref, q_ref, k_ref, v_ref, do_ref, lse_ref, c_ref, dq_ref, dk_ref, dv_ref, *scratch):
        (dk_acc, dv_acc, bias_ref), tmps = scratch[n_stage:n_stage + 3], scratch[n_stage + 3:]
        (q_in, k_in, v_in, do_in, lse_in, c_in), (dq_out, dk_out, dv_out), rows, flush = _residue_views(
            dilation, seq, (q_ref, k_ref, v_ref, do_ref, lse_ref, c_ref), (dq_ref, dk_ref, dv_ref), tmps)
        all_rows = rows(0, length)
        qs, ks, vs, dos = _staged(dilation, length, rows, (q_in, k_in, v_in, do_in), scratch[:n_stage])
        dk_acc[...] = jnp.zeros_like(dk_acc)
        dv_acc[...] = jnp.zeros_like(dv_acc)
        _fill_bias(bias_ref, sl_ref, dilation)

        def block(i, carry):
            q0, ws, which = _block_window(i, n_blocks, length)
            qm = _stack_heads(qs[pl.ds(q0, _BQ), :])
            dom = _stack_heads(dos[pl.ds(q0, _BQ), :])
            kw = ks[pl.ds(ws, _KW), :]
            vw = vs[pl.ds(ws, _KW), :]
            p = jnp.exp(_dot_nt(qm, kw) + bias_ref[which] - _stack_cols(lse_in[rows(q0, _BQ), :]))
            ds = (p * (_dot_nt(dom, vw) + _stack_cols(c_in[rows(q0, _BQ), :]))).astype(BF16)
            dq_out[rows(q0, _BQ), :] = (_unstack_heads(_dot(ds, kw)) * _SCORE_SCALE).astype(dq_out.dtype)
            dk_acc[pl.ds(ws, _KW), :] += _dot_tn(ds, qm)
            dv_acc[pl.ds(ws, _KW), :] += _dot_tn(p.astype(BF16), dom)
            return carry

        lax.fori_loop(0, n_blocks, block, 0, unroll=min(_ATTN_UNROLL, n_blocks))
        dk_out[all_rows, :] = dk_acc[...].astype(dk_out.dtype)
        dv_out[all_rows, :] = dv_acc[...].astype(dv_out.dtype)
        flush()

    stage = pltpu.VMEM((length, _PAIR), BF16)
    acc = pltpu.VMEM((length, _PAIR), F32)
    bias = pltpu.VMEM((3, 2 * _BQ, _KW), F32)
    return _attn_call(body, name, dilation, seq, 6, [_attn_dtype(dilation)] * 3, [stage] * n_stage + [acc] * 2 + [bias],
                      2 if dilation == 1 else 1)(slopes, q, k, v, do, lse, cterm)


def _group_weights(lses):
    m = jnp.maximum(jnp.maximum(lses[0], lses[1]), lses[2])
    es = [jnp.exp(l - m) for l in lses]
    den = es[0] + es[1] + es[2]
    return [e / den for e in es]


def _out_fwd(a_pool, outs, lses, x, w_out, g, name, tm=1024):
    s, d = x.shape
    width = POOL_DIM + 3 * GROUP_DIM

    def body(ap_ref, o0, o1, o2, l0, l1, l2, x_ref, w_ref, g_ref, xo_ref, mix_ref, cat_ref):
        alphas = _group_weights([l0[...], l1[...], l2[...]])
        cat = jnp.concatenate([ap_ref[...]] + [(o[...] * al).astype(BF16) for o, al in zip((o0, o1, o2), alphas)], axis=1)
        cat_ref[...] = cat
        mix = _dot(cat, w_ref[...])
        mix_ref[...] = mix
        xo_ref[...] = x_ref[...] + mix * _inv_rms(mix) * g_ref[...]

    return pl.pallas_call(
        body, name=name, grid=(s // tm,),
        in_specs=[_rows(tm, POOL_DIM)] + [_rows(tm, GROUP_DIM)] * 6 + [_rows(tm, d), _resident(w_out.shape), _const((1, d))],
        out_specs=[_rows(tm, d), _rows(tm, d), _rows(tm, width)],
        out_shape=[jax.ShapeDtypeStruct((s, d), F32), jax.ShapeDtypeStruct((s, d), F32), jax.ShapeDtypeStruct((s, width), BF16)],
        compiler_params=_params(dimension_semantics=("arbitrary",)))(a_pool, *outs, *lses, x, w_out, g)


def _out_bwd(dxo, mix, outs, lses, w_out, g, head_ones, name, tm=1024):
    s, d = mix.shape

    def body(dxo_ref, mix_ref, o0, o1, o2, l0, l1, l2, w_ref, g_ref, ones_ref, dpool_ref, dmix_ref, do0, do1, do2, c0, c1, c2, dg_ref):
        mv = mix_ref[...]
        dmix, dg = _rms_bwd(mv, _inv_rms(mv), g_ref[...], dxo_ref[...])
        dmb = dmix.astype(BF16)
        dmix_ref[...] = dmb
        _accumulate(dg_ref, dg)
        dcat = _dot_nt(dmb, w_ref[...])
        dpool_ref[...] = dcat[:, :POOL_DIM]
        alphas = _group_weights([l0[...], l1[...], l2[...]])
        das = [dcat[:, POOL_DIM + GROUP_DIM * j:POOL_DIM + GROUP_DIM * (j + 1)] for j in range(3)]
        prod = sum(da * (o[...] * al) for da, o, al in zip(das, (o0, o1, o2), alphas))
        hi = prod.astype(BF16)
        lo = (prod - hi.astype(F32)).astype(BF16)
        total = _dot(hi, ones_ref[...]) + _dot(lo, ones_ref[...])
        for da, al, do_ref, c_ref in zip(das, alphas, (do0, do1, do2), (c0, c1, c2)):
            do_ref[...] = (da * al).astype(do_ref.dtype)
            c_ref[...] = -al * total

    return pl.pallas_call(
        body, name=name, grid=(s // tm,),
        in_specs=[_rows(tm, d), _rows(tm, d)] + [_rows(tm, GROUP_DIM)] * 6 + [_resident(w_out.shape), _const((1, d)),
                                                                             _const((GROUP_DIM, GROUP_DIM))],
        out_specs=[_rows(tm, POOL_DIM), _rows(tm, d)] + [_rows(tm, GROUP_DIM)] * 6 + [_const((1, d))],
        out_shape=[jax.ShapeDtypeStruct((s, POOL_DIM), F32), jax.ShapeDtypeStruct((s, d), BF16)]
        + [jax.ShapeDtypeStruct((s, GROUP_DIM), _attn_dtype(dil)) for dil in DILATIONS]
        + [jax.ShapeDtypeStruct((s, GROUP_DIM), F32)] * 3 + [jax.ShapeDtypeStruct((1, d), F32)],
        compiler_params=_params(dimension_semantics=("arbitrary",)))(dxo, mix, *outs, *lses, w_out, g, head_ones)


def _alibi_slopes():
    return np.array([2.0 ** (-8.0 * (i + 1) / N_ATTN_HEADS) for i in range(N_ATTN_HEADS)], np.float32)


def _block_diag(w_lin):
    n, c, _ = w_lin.shape
    eye = jnp.eye(n, dtype=w_lin.dtype)
    return (eye[:, None, :, None] * w_lin[:, :, None, :]).reshape(n * c, n * c)


class _NoExchange:
    def __init__(self, full):
        self.full, self.grads = full, {}

    def first_weights(self):
        return self.full

    def riders(self, host):
        return []

    def landed(self, host, results):
        return self.full

    def gradient(self, name, grad):
        self.grads[name] = grad


def _local_step(x, target, small, exchange):
    s, d = x.shape
    slopes = _alibi_slopes()
    group_slopes = [jnp.asarray(slopes[4 * g:4 * g + 4]) for g in range(3)]
    w_bd = _block_diag(small["w_pool_lin"]).astype(BF16)
    head_ones = jnp.asarray(np.kron(np.eye(GROUP_DIM // HEAD_DIM), np.ones((HEAD_DIM, HEAD_DIM))), BF16)

    full = dict(exchange.first_weights())

    def hosted(call, host, *args):
        results, riding = call(*args, host, exchange.riders(host))
        full.update(exchange.landed(host, riding) or {})
        return results

    x1, a1, b1, f1 = hosted(_ffn_fwd, "ffn1_fwd", x, small["g_ffn1_pre"], full["w1_gate"], full["w1_up"], full["w1_down"],
                            small["g_ffn1_post"], None)
    h2, u, *parts = hosted(_in_fwd, "in_fwd", x1, small["g_mix_pre"], full["w_in"])
    qs, ks, vs = parts[0:3], parts[3:6], parts[6:9]
    a_pool = _pool_fwd(u, w_bd, small["pool_scale"], "pool_fwd")
    outs, lses = [], []
    for g, dil in enumerate(DILATIONS):
        o, lse = _attn_fwd(qs[g], ks[g], vs[g], group_slopes[g], dil, f"attn_fwd{g}")
        outs.append(o)
        lses.append(lse)
    x2, mix, cat = _out_fwd(a_pool, outs, lses, x1, full["w_out"], small["g_mix_post"], "out_fwd")
    (dx3, a2, b2, f2, loss_part), _ = _ffn_fwd(x2, small["g_ffn2_pre"], full["w2_gate"], full["w2_up"], full["w2_down"],
                                               small["g_ffn2_post"], target, "ffn2_fwd")

    small_grads = {}

    def ffn_backward(tag, dxo, x_in, f, a, b):
        n = tag[-1]
        dx, hh, da, db, df, h, dg_pre, dg_post = hosted(
            _ffn_bwd, f"{tag}_bwd", dxo, x_in, f, a, b, small[f"g_{tag}_pre"], small[f"g_{tag}_post"],
            full[f"w{n}_gate"], full[f"w{n}_up"], full[f"w{n}_down"])
        for part, lhs, rhs in (("down", hh, df), ("gate", da, h), ("up", db, h)):
            exchange.gradient(f"w{n}_{part}", hosted(_wgrad, f"{tag}_wgrad_{part}", lhs, rhs))
        small_grads[f"g_{tag}_pre"], small_grads[f"g_{tag}_post"] = dg_pre, dg_post
        return dx

    dx2 = ffn_backward("ffn2", dx3, x2, f2, a2, b2)
    dpool, dmix, *dos_cs, small_grads["g_mix_post"] = _out_bwd(dx2, mix, outs, lses, full["w_out"], small["g_mix_post"],
                                                               head_ones, "out_bwd")
    dos, cs = dos_cs[:3], dos_cs[3:]
    dqs, dks, dvs = [], [], []
    for g, dil in enumerate(DILATIONS):
        dq, dk, dv = _attn_bwd(qs[g], ks[g], vs[g], dos[g], lses[g], cs[g], group_slopes[g], dil, f"attn_bwd{g}")
        dqs.append(dq)
        dks.append(dk)
        dvs.append(dv)
    du, dw_bd, small_grads["pool_scale"] = _pool_bwd(u, dpool, w_bd, small["pool_scale"], "pool_bwd")
    n_pool = len(POOL_HALF_WINDOWS)
    small_grads["w_pool_lin"] = jnp.stack(
        [dw_bd[HEAD_DIM * g:HEAD_DIM * (g + 1), HEAD_DIM * g:HEAD_DIM * (g + 1)] for g in range(n_pool)])
    dx1, dz, small_grads["g_mix_pre"] = hosted(_in_bwd, "in_bwd", du, dqs + dks + dvs, x1, dx2, small["g_mix_pre"], full["w_in"])
    exchange.gradient("w_in", hosted(_wgrad, "wgrad_in", dz, h2))
    dx0 = ffn_backward("ffn1", dx1, x, f1, a1, b1)
    exchange.gradient("w_out", hosted(_wgrad, "wgrad_out", cat, dmix))
    return loss_part[0, 0], dx0, small_grads


SEGMENTS = ("w1_gate", "w1_up", "w1_down", "w_in", "w_out", "w2_gate", "w2_up", "w2_down")
TRANSPOSED = ("w1_gate", "w1_up", "w_in", "w2_gate", "w2_up")
ROWS_OUTSIDE = ("w1_gate", "w1_up", "w2_gate", "w2_up")
HALF = 512


def _place():
    x, y, c = lax.axis_index("x"), lax.axis_index("y"), lax.axis_index("c")
    other_chips = [(1 - x, y), (x, 1 - y), (1 - x, 1 - y)]
    return x, y, c, other_chips


def _chip_rows(chip, rows):
    return pl.ds(pl.multiple_of((2 * chip[0] + chip[1]) * rows, 16), rows)


def _cols(c):
    return pl.ds(pl.multiple_of(c * HALF, HALF), HALF)


def _cast_shards(shards, transposed, place, name):
    n = len(shards)
    rows = [w.shape[1] if t else w.shape[0] for w, t in zip(shards, transposed)]

    def body(place_ref, *refs):
        for w_ref, o_ref, t in zip(refs[:n], refs[n:], transposed):
            o_ref[...] = (w_ref[...].T if t else w_ref[...]).astype(BF16)

    once = pl.Buffered(1)
    return pl.pallas_call(
        body, name=name,
        grid_spec=pltpu.PrefetchScalarGridSpec(
            num_scalar_prefetch=1, grid=(1,),
            in_specs=[pl.BlockSpec(w.shape, lambda i, place: (0, 0), pipeline_mode=once) for w in shards],
            out_specs=[pl.BlockSpec((r, 1024), lambda i, place: (place[0], 0), pipeline_mode=once) for r in rows]),
        out_shape=[jax.ShapeDtypeStruct((N_CHIPS * r, 1024), BF16) for r in rows],
        compiler_params=_params(dimension_semantics=("arbitrary",)))(place, *shards)


def _gather_weights(bufs):
    n = len(bufs)
    rows = [b.shape[0] // N_CHIPS for b in bufs]

    def halves(r):
        first = -(-r // 32) * 16
        return (0, first), (first, r - first)

    def body(*refs):
        outs = refs[n:2 * n]
        ici_send, ici_recv, d2d_send, d2d_recv = refs[2 * n:]
        x, y, c, _ = _place()
        me, via_x, via_y, diagonal = (x, y), (1 - x, y), (x, 1 - y), (1 - x, 1 - y)

        def piece(chip, k, h, cols):
            start, size = halves(rows[k])[h]
            return outs[k].at[pl.ds(pl.multiple_of((2 * chip[0] + chip[1]) * rows[k] + start, 16), size), _cols(cols)]

        def ici(path, chip, k, h, to):
            blk = piece(chip, k, h, c)
            return pltpu.make_async_remote_copy(src_ref=blk, dst_ref=blk, send_sem=ici_send.at[path, k, h],
                                                recv_sem=ici_recv.at[path, k, h], device_id=(*to, c), device_id_type=MESH)

        def d2d(slot, chip, k, h, cols):
            blk = piece(chip, k, h, cols)
            return pltpu.make_async_remote_copy(src_ref=blk, dst_ref=blk, send_sem=d2d_send.at[slot, k, h],
                                                recv_sem=d2d_recv.at[slot, k, h], device_id=(x, y, 1 - c), device_id_type=MESH)

        started = [ici(0, me, k, h, via_x) for h in (0, 1) for k in range(n)] + [ici(1, me, k, h, via_y) for h in (1, 0) for k in range(n)]
        for cp in started:
            cp.start()

        def landed(path, slot, chip, k, h, pass_on_to=None):
            ici(path, chip, k, h, me).wait_recv()
            more = [d2d(slot, chip, k, h, c)] + ([ici(2, chip, k, h, pass_on_to)] if pass_on_to else [])
            for cp in more:
                cp.start()
            started.extend(more)

        for k in range(n):
            landed(0, 0, via_x, k, 0, pass_on_to=via_y)
            landed(1, 1, via_y, k, 1, pass_on_to=via_x)
        for k in range(n):
            landed(0, 0, via_x, k, 1)
            landed(1, 1, via_y, k, 0)
        for k in range(n):
            for h in range(2):
                landed(2, 2, diagonal, k, h)
        for slot, chip in enumerate((via_x, via_y, diagonal)):
            for k in range(n):
                for h in range(2):
                    d2d(slot, chip, k, h, 1 - c).wait_recv()
        for cp in started:
            cp.wait_send()

    any_spec = pl.BlockSpec(memory_space=pl.ANY)
    return pl.pallas_call(
        body, name="gather_weights", in_specs=[any_spec] * n, out_specs=[any_spec] * n,
        out_shape=[jax.ShapeDtypeStruct(b.shape, b.dtype) for b in bufs], input_output_aliases={k: k for k in range(n)},
        scratch_shapes=[pltpu.SemaphoreType.DMA((3, n, 2))] * 4)(*bufs)


def _gather_rider(bufs):
    n = len(bufs)
    rows = [b.shape[0] // N_CHIPS for b in bufs]

    def copies(outs, send_sems, recv_sems, inbound):
        x, y, c, chips = _place()
        for j, chip in enumerate(chips):
            for k in range(n):
                src_chip = chip if inbound else (x, y)
                blk = outs[k].at[_chip_rows(src_chip, rows[k]), _cols(c)]
                yield pltpu.make_async_remote_copy(src_ref=blk, dst_ref=blk, send_sem=send_sems.at[j, k], recv_sem=recv_sems.at[j, k],
                                                   device_id=(*chip, c), device_id_type=MESH)

    def start(ins, outs, send_sems, recv_sems):
        for cp in copies(outs, send_sems, recv_sems, False):
            cp.start()

    def wait(ins, outs, send_sems, recv_sems):
        for cp in copies(outs, send_sems, recv_sems, True):
            cp.wait_recv()
        for cp in copies(outs, send_sems, recv_sems, False):
            cp.wait_send()

    return _Rider(list(bufs), None, (3, n), start, wait)


def _forward_rider(bufs):
    n = len(bufs)
    rows = [b.shape[0] // N_CHIPS for b in bufs]

    def copies(outs, send_sems, recv_sems, half):
        x, y, c, chips = _place()
        for j, chip in enumerate(chips):
            for k in range(n):
                blk = outs[k].at[_chip_rows(chip, rows[k]), _cols(half(c))]
                yield pltpu.make_async_remote_copy(src_ref=blk, dst_ref=blk, send_sem=send_sems.at[j, k], recv_sem=recv_sems.at[j, k],
                                                   device_id=(x, y, 1 - c), device_id_type=MESH)

    def start(ins, outs, send_sems, recv_sems):
        for cp in copies(outs, send_sems, recv_sems, lambda c: c):
            cp.start()

    def wait(ins, outs, send_sems, recv_sems):
        for cp in copies(outs, send_sems, recv_sems, lambda c: 1 - c):
            cp.wait_recv()
        for cp in copies(outs, send_sems, recv_sems, lambda c: c):
            cp.wait_send()

    return _Rider(list(bufs), None, (3, n), start, wait)


def _sibling_rider(grads):
    n = len(grads)

    def copies(ins, outs, send_sems, recv_sems):
        x, y, c, _ = _place()
        return [pltpu.make_async_remote_copy(src_ref=ins[k].at[:, pl.ds(1 - c, 1)], dst_ref=outs[k], send_sem=send_sems.at[k],
                                             recv_sem=recv_sems.at[k], device_id=(x, y, 1 - c), device_id_type=MESH)
                for k in range(n)]

    def start(*refs):
        for cp in copies(*refs):
            cp.start()

    def wait(*refs):
        for cp in copies(*refs):
            cp.wait()

    return _Rider(list(grads), [jax.ShapeDtypeStruct((N_CHIPS, 1) + g.shape[2:], F32) for g in grads], (n,), start, wait)


def _alone(rider, name):
    n = len(rider.operands)
    landing = rider.landing if rider.landing is not None else [jax.ShapeDtypeStruct(a.shape, a.dtype) for a in rider.operands]
    n_out = len(landing)

    def body(*refs):
        rider.start(refs[:n], refs[n:n + n_out], *refs[n + n_out:])
        rider.wait(refs[:n], refs[n:n + n_out], *refs[n + n_out:])

    any_spec = pl.BlockSpec(memory_space=pl.ANY)
    return pl.pallas_call(body, name=name, in_specs=[any_spec] * n, out_specs=[any_spec] * n_out, out_shape=landing,
                          input_output_aliases={i: i for i in range(n)} if rider.landing is None else {},
                          scratch_shapes=[pltpu.SemaphoreType.DMA(rider.sems)] * 2)(*rider.operands)


def _chip_sum(grad, from_sibling, place, name):
    rh, width = grad.shape[2:]

    def body(place_ref, g_ref, s_ref, own_ref, all_ref):
        all_ref[...] = (g_ref[...] + s_ref[...]).astype(BF16)
        mine = place_ref[0]
        own_ref[0] = g_ref[mine, 0] + s_ref[mine, 0]

    blk = (N_CHIPS, 1, rh, width)
    once = pl.Buffered(1)
    return pl.pallas_call(
        body, name=name,
        grid_spec=pltpu.PrefetchScalarGridSpec(
            num_scalar_prefetch=1, grid=(1,),
            in_specs=[pl.BlockSpec(blk, lambda i, place: (0, place[1], 0, 0), pipeline_mode=once),
                      pl.BlockSpec(blk, lambda i, place: (0, 0, 0, 0), pipeline_mode=once)],
            out_specs=[pl.BlockSpec((1, rh, width), lambda i, place: (0, 0, 0), pipeline_mode=once),
                       pl.BlockSpec(blk, lambda i, place: (0, 0, 0, 0), pipeline_mode=once)]),
        out_shape=[jax.ShapeDtypeStruct((1, rh, width), F32), jax.ShapeDtypeStruct((N_CHIPS, 1, rh, width), BF16)],
        compiler_params=_params(dimension_semantics=("arbitrary",)))(place, grad, from_sibling)


def _scatter_rider(sums):
    n = len(sums)

    def copies(ins, outs, send_sems, recv_sems):
        x, y, c, chips = _place()
        return [pltpu.make_async_remote_copy(src_ref=ins[k].at[pl.ds(2 * chip[0] + chip[1], 1)], dst_ref=outs[k].at[pl.ds(j, 1)],
                                             send_sem=send_sems.at[j, k], recv_sem=recv_sems.at[j, k],
                                             device_id=(*chip, c), device_id_type=MESH)
                for j, chip in enumerate(chips) for k in range(n)]

    def start(*refs):
        for cp in copies(*refs):
            cp.start()

    def wait(*refs):
        for cp in copies(*refs):
            cp.wait()

    return _Rider(list(sums), [jax.ShapeDtypeStruct((3,) + sm.shape[1:], BF16) for sm in sums], (3, n), start, wait)


def _total_sums(owns, received, name):
    n = len(owns)

    def body(*refs):
        for o_ref, r_ref, t_ref in zip(refs[:n], refs[n:2 * n], refs[2 * n:]):
            total = o_ref[0]
            for j in range(3):
                total = total + r_ref[j, 0].astype(F32)
            t_ref[0] = total

    return pl.pallas_call(body, name=name, out_shape=[jax.ShapeDtypeStruct(o.shape, F32) for o in owns],
                          compiler_params=_params())(*owns, *received)


def _swap_halves(halves):
    n = len(halves)

    def body(*refs):
        ins, outs = refs[:n], refs[n:2 * n]
        send_sems, recv_sems = refs[2 * n:]
        x, y, c, _ = _place()
        copies = [pltpu.make_async_remote_copy(src_ref=ins[k], dst_ref=outs[k], send_sem=send_sems.at[k],
                                               recv_sem=recv_sems.at[k], device_id=(x, y, 1 - c), device_id_type=MESH)
                  for k in range(n)]
        for cp in copies:
            cp.start()
        for cp in copies:
            cp.wait()

    any_spec = pl.BlockSpec(memory_space=pl.ANY)
    return pl.pallas_call(
        body, name="swap_halves", in_specs=[any_spec] * n, out_specs=[any_spec] * n,
        out_shape=[jax.ShapeDtypeStruct(h.shape, F32) for h in halves],
        scratch_shapes=[pltpu.SemaphoreType.DMA((n,)), pltpu.SemaphoreType.DMA((n,))])(*halves)


N_DEV = 8


def _gather_small(block):
    m_per, width = block.shape

    def body(x_ref, out_ref, send_sems, recv_sems, local_sem):
        x, y, c, chips = _place()
        me, sibling = (x, y, c), (x, y, 1 - c)

        def rows(px, py, pc):
            return out_ref.at[pl.ds((4 * px + 2 * py + pc) * m_per, m_per), :]

        def copy(k, blk, to, src=None):
            return pltpu.make_async_remote_copy(src_ref=rows(*blk) if src is None else src, dst_ref=rows(*blk),
                                                send_sem=send_sems.at[k], recv_sem=recv_sems.at[k], device_id=to, device_id_type=MESH)

        mine = pltpu.make_async_copy(x_ref, rows(*me), local_sem)
        mine.start()
        first = [copy(0, me, sibling, src=x_ref)] + [copy(1 + j, me, (*chip, c), src=x_ref) for j, chip in enumerate(chips)]
        for cp in first:
            cp.start()
        passed = [copy(4 + j, (*chip, c), sibling) for j, chip in enumerate(chips)]
        for j, chip in enumerate(chips):
            copy(1 + j, (*chip, c), me).wait_recv()
            passed[j].start()
        copy(0, sibling, me).wait_recv()
        for j, chip in enumerate(chips):
            copy(4 + j, (*chip, 1 - c), me).wait_recv()
        for cp in first + passed:
            cp.wait_send()
        mine.wait()

    vmem = pl.BlockSpec(memory_space=pltpu.VMEM)
    return pl.pallas_call(body, name="gather_small", out_shape=jax.ShapeDtypeStruct((N_DEV * m_per, width), F32),
                          in_specs=[vmem], out_specs=vmem,
                          scratch_shapes=[pltpu.SemaphoreType.DMA((7,)), pltpu.SemaphoreType.DMA((7,)),
                                          pltpu.SemaphoreType.DMA])(block)


def _adamw_math(w, g, m, v):
    m = ADAM_B1 * m + (1.0 - ADAM_B1) * g
    v = ADAM_B2 * v + (1.0 - ADAM_B2) * (g * g)
    m_hat = m / (1.0 - ADAM_B1 ** ADAM_STEP)
    v_hat = v / (1.0 - ADAM_B2 ** ADAM_STEP)
    delta = -ADAM_LR * (m_hat / (jnp.sqrt(v_hat) + ADAM_EPS) + ADAM_WD * w)
    return delta, m, v


def _adamw(w, mine, siblings, place, m, v, transposed, name):
    if transposed:
        def body(place_ref, w_ref, mine_ref, sib_ref, m_ref, v_ref, go_ref, d_ref, mo_ref, vo_ref):
            first = place_ref[1] == 0
            g = jnp.concatenate([jnp.where(first, mine_ref[0], sib_ref[0]), jnp.where(first, sib_ref[0], mine_ref[0])], axis=0).T
            go_ref[...] = g
            d_ref[...], mo_ref[...], vo_ref[...] = _adamw_math(w_ref[...], g, m_ref[...], v_ref[...])

        vmem = pl.BlockSpec(memory_space=pltpu.VMEM)
        return pl.pallas_call(body, name=name, in_specs=[pl.BlockSpec(memory_space=pltpu.SMEM)] + [vmem] * 5, out_specs=[vmem] * 4,
                              out_shape=[jax.ShapeDtypeStruct(w.shape, F32)] * 4, compiler_params=_params())(
                                  place, w, mine, siblings, m, v)

    rh, width = mine.shape[1:]

    def body(place_ref, w_ref, mine_ref, sib_ref, m_ref, v_ref, go_ref, d_ref, mo_ref, vo_ref):
        g = jnp.where(pl.program_id(0) == place_ref[1], mine_ref[0], sib_ref[0])
        go_ref[...] = g
        d_ref[...], mo_ref[...], vo_ref[...] = _adamw_math(w_ref[...], g, m_ref[...], v_ref[...])

    half = pl.BlockSpec((rh, width), lambda h, place: (h, 0))
    whole = pl.BlockSpec((1, rh, width), lambda h, place: (0, 0, 0))
    return pl.pallas_call(
        body, name=name,
        grid_spec=pltpu.PrefetchScalarGridSpec(num_scalar_prefetch=1, grid=(2,), in_specs=[half, whole, whole, half, half],
                                               out_specs=[half] * 4),
        out_shape=[jax.ShapeDtypeStruct(w.shape, F32)] * 4,
        compiler_params=_params(dimension_semantics=("arbitrary",)))(place, w, mine, siblings, m, v)


def _adamw_small(gathered, w, m, v, name):
    def body(ga_ref, w_ref, m_ref, v_ref, go_ref, d_ref, mo_ref, vo_ref):
        g = ga_ref[0]
        for dev in range(1, N_DEV):
            g = g + ga_ref[dev]
        go_ref[...] = g
        d_ref[...], mo_ref[...], vo_ref[...] = _adamw_math(w_ref[...], g, m_ref[...], v_ref[...])

    return pl.pallas_call(body, name=name, out_shape=[jax.ShapeDtypeStruct(w.shape, F32)] * 4,
                          compiler_params=_params())(gathered, w, m, v)


class _Exchange:
    FIRST = ("w1_gate", "w1_up", "w1_down")
    HOSTS = {"ffn2_wgrad_gate": (("w2_down",), ()), "ffn2_wgrad_up": (("w2_gate",), ("w2_down",)),
             "in_bwd": (("w2_up",), ("w2_gate",)), "wgrad_in": ((), ("w2_up",)),
             "ffn1_wgrad_down": ((), ("w_in",)), "ffn1_wgrad_gate": (("w1_down",), ()), "ffn1_wgrad_up": ((), ("w1_down", "w1_gate")),
             "wgrad_out": ((), ("w1_up",))}
    ALONE = ("w_in", "w1_gate", "w1_up", "w_out")

    def __init__(self, bufs, place):
        self.bufs, self.place = bufs, place
        self.later = [k for k in SEGMENTS if k not in self.FIRST]
        self.split, self.own, self.to_send, self.received = {}, {}, {}, {}

    def first_weights(self):
        return dict(zip(self.FIRST, _gather_weights([self.bufs[k] for k in self.FIRST])))

    def riders(self, host):
        if host == "ffn1_fwd":
            return [_gather_rider([self.bufs[k] for k in self.later])]
        if host == "in_fwd":
            return [_forward_rider([self.bufs[k] for k in self.later[1:]])]
        halves, sums = self.HOSTS.get(host, ((), ()))
        return ([_sibling_rider([self.split[k] for k in halves])] if halves else []) + (
            [_scatter_rider([self.to_send[k] for k in sums])] if sums else [])

    def landed(self, host, results):
        if host == "ffn1_fwd":
            self.bufs.update(zip(self.later, results[0]))
            return dict(zip(self.later[:1], _alone(_forward_rider([self.bufs[self.later[0]]]), "gather_forward_first")))
        if host == "in_fwd":
            return dict(zip(self.later[1:], results[0]))
        halves, sums = self.HOSTS.get(host, ((), ()))
        if halves:
            self._chip_sums(halves, results[0])
        if sums:
            self.received.update(zip(sums, results[-1]))

    def gradient(self, name, grad):
        self.split[name] = grad.reshape(N_CHIPS, 2, grad.shape[0] // (2 * N_CHIPS), grad.shape[1])
        if name in self.ALONE:
            self._chip_sums([name], _alone(_sibling_rider([self.split[name]]), f"reduce_sibling_{name}"))

    def _chip_sums(self, names, from_sibling):
        for k, fs in zip(names, from_sibling):
            self.own[k], self.to_send[k] = _chip_sum(self.split[k], fs, self.place, f"chip_sum_{k}")

    def summed_halves(self):
        late = [k for k in SEGMENTS if k not in self.received]
        self.received.update(zip(late, _alone(_scatter_rider([self.to_send[k] for k in late]), "reduce_chips_last")))
        return _total_sums([self.own[k] for k in SEGMENTS], [self.received[k] for k in SEGMENTS], "total_sums")


SMALL = ("g_ffn1_pre", "g_ffn1_post", "g_mix_pre", "w_pool_lin", "pool_scale", "g_mix_post", "g_ffn2_pre", "g_ffn2_post")
WEIGHTS = ("g_ffn1_pre", "w1_gate", "w1_up", "w1_down", "g_ffn1_post", "g_mix_pre", "w_in", "w_pool_lin", "pool_scale", "w_out",
           "g_mix_post", "g_ffn2_pre", "w2_gate", "w2_up", "w2_down", "g_ffn2_post")
LANES = 128


def _pack_small(tree, extra=0.0):
    flat = jnp.concatenate([tree[k].reshape(-1) for k in SMALL] + [jnp.reshape(extra, (1,)).astype(F32)])
    rows = -(-flat.shape[0] // (8 * LANES)) * 8
    return jnp.pad(flat, (0, rows * LANES - flat.shape[0])).reshape(rows, LANES)


def _unpack_small(packed, like):
    flat, out, at = packed.reshape(-1), {}, 0
    for k in SMALL:
        size = math.prod(like[k].shape)
        out[k] = flat[at:at + size].reshape(like[k].shape)
        at += size
    return out


def kernel(x, g_ffn1_pre, w1_gate, w1_up, w1_down, g_ffn1_post, g_mix_pre, w_in, w_pool_lin, pool_scale, w_out, g_mix_post, g_ffn2_pre, w2_gate, w2_up, w2_down, g_ffn2_post, loss_target, m_g_ffn1_pre, m_w1_gate, m_w1_up, m_w1_down, m_g_ffn1_post, m_g_mix_pre, m_w_in, m_w_pool_lin, m_pool_scale, m_w_out, m_g_mix_post, m_g_ffn2_pre, m_w2_gate, m_w2_up, m_w2_down, m_g_ffn2_post, v_g_ffn1_pre, v_w1_gate, v_w1_up, v_w1_down, v_g_ffn1_post, v_g_mix_pre, v_w_in, v_w_pool_lin, v_pool_scale, v_w_out, v_g_mix_post, v_g_ffn2_pre, v_w2_gate, v_w2_up, v_w2_down, v_g_ffn2_post):
    given = dict(locals())
    w = {k: given[k] for k in WEIGHTS}
    m = {k: given["m_" + k] for k in WEIGHTS}
    v = {k: given["v_" + k] for k in WEIGHTS}
    small = {k: (w[k][0] if k == "w_pool_lin" else w[k].reshape(1, -1)) for k in SMALL}

    place = jnp.stack([2 * lax.axis_index("x") + lax.axis_index("y"), lax.axis_index("c")]).astype(jnp.int32)
    def as_rows(a, k):
        return jnp.swapaxes(a, 1, 2)[0] if k in ROWS_OUTSIDE else a[0]

    def as_given(a, k):
        return jnp.swapaxes(a[None], 1, 2) if k in ROWS_OUTSIDE else a[None]

    in_kernel = [k for k in TRANSPOSED if k not in ROWS_OUTSIDE]
    bufs = {}
    for tag, names in (("first", _Exchange.FIRST), ("rest", [k for k in SEGMENTS if k not in _Exchange.FIRST])):
        bufs.update(zip(names, _cast_shards([as_rows(w[k], k) for k in names], [k in in_kernel for k in names], place, f"cast_{tag}")))
    exchange = _Exchange(bufs, place)
    loss_part, grad_x, small_grads = _local_step(x[0], loss_target[0], small, exchange)

    halves = exchange.summed_halves()
    from_sibling = _swap_halves(halves)

    out_grad, out_delta, out_m, out_v = {}, {}, {}, {}
    for k, mine, sib in zip(SEGMENTS, halves, from_sibling):
        out_grad[k], out_delta[k], out_m[k], out_v[k] = (
            as_given(a, k) for a in _adamw(as_rows(w[k], k), mine, sib, place, as_rows(m[k], k), as_rows(v[k], k),
                                           k in in_kernel, f"adamw_{k}"))

    small_grads["w_pool_lin"] = small_grads["w_pool_lin"][None]
    packed = _pack_small(small_grads, loss_part)
    gathered = _gather_small(packed).reshape(N_DEV, *packed.shape)
    like = {k: w[k] for k in SMALL}
    results = _adamw_small(gathered, _pack_small(like), _pack_small({k: m[k] for k in SMALL}),
                           _pack_small({k: v[k] for k in SMALL}), "adamw_small")
    for tree, res in zip((out_grad, out_delta, out_m, out_v), results):
        tree.update(_unpack_small(res, like))
    loss = results[0].reshape(-1)[sum(math.prod(like[k].shape) for k in SMALL)]

    return (loss, grad_x[None], *[out_grad[k] for k in WEIGHTS], *[out_delta[k] for k in WEIGHTS],
            *[out_m[k] for k in WEIGHTS], *[out_v[k] for k in WEIGHTS])
```

```python
import math
import typing

import numpy as np
import jax
import jax.numpy as jnp
from jax import lax
from jax.experimental import pallas as pl
from jax.experimental.pallas import tpu as pltpu

F32 = jnp.float32
BF16 = jnp.bfloat16
MESH = pl.DeviceIdType.MESH

RMS_EPS = 1e-6
HEAD_DIM = 64
POOL_HALF_WINDOWS = (1, 2, 4, 8)
POOL_DIM = 256
GROUP_DIM = 256
DILATIONS = (1, 4, 16)
N_SIDE = 64
N_ATTN_HEADS = 12
ADAM_LR, ADAM_B1, ADAM_B2, ADAM_EPS, ADAM_WD, ADAM_STEP = 0.001, 0.9, 0.999, 1e-08, 0.01, 10

N_CHIPS = 4
V7X_VMEM_LIMIT = 60 * 1024 * 1024

_NT = (((1,), (1,)), ((), ()))
_TN = (((0,), (0,)), ((), ()))


def _dot(a, b):
    return jnp.dot(a, b, preferred_element_type=F32)


def _dot_nt(a, b):
    return lax.dot_general(a, b, _NT, preferred_element_type=F32)


def _dot_tn(a, b):
    return lax.dot_general(a, b, _TN, preferred_element_type=F32)


def _params(**kw):
    return pltpu.CompilerParams(vmem_limit_bytes=V7X_VMEM_LIMIT, **kw)


def _rows(tm, width):
    return pl.BlockSpec((tm, width), lambda i: (i, 0))


def _resident(shape):
    return pl.BlockSpec(shape, lambda i: (0,) * len(shape), pipeline_mode=pl.Buffered(1))


def _const(shape):
    return pl.BlockSpec(shape, lambda i: (0,) * len(shape))


def _inv_rms(x):
    return lax.rsqrt(jnp.mean(x * x, axis=-1, keepdims=True) + RMS_EPS)


def _rms_bwd(x, inv, g, dy):
    n = x * inv
    dn = dy * g
    dx = inv * (dn - n * jnp.mean(dn * n, axis=-1, keepdims=True))
    return dx, jnp.sum(dy * n, axis=0, keepdims=True)


def _accumulate(ref, value):
    @pl.when(pl.program_id(0) == 0)
    def _():
        ref[...] = jnp.zeros_like(ref)

    ref[...] += value


class _Rider(typing.NamedTuple):
    operands: list
    landing: typing.Optional[list]
    sems: tuple
    start: typing.Callable
    wait: typing.Callable


def _hosted_call(body, riders, *, name, steps, in_specs, out_specs, out_shape, args, scratch_shapes=()):
    params = _params(dimension_semantics=("arbitrary",))
    riders = list(riders or [])
    if not riders:
        res = pl.pallas_call(body, name=name, grid=(steps,), in_specs=in_specs, out_specs=out_specs, out_shape=out_shape,
                             scratch_shapes=list(scratch_shapes), compiler_params=params)(*args)
        return list(res), []
    n_in, n_out, n_scratch = len(in_specs), len(out_specs), len(scratch_shapes)
    operands, landing, aliases, spans = [], [], {}, []
    for rd in riders:
        lands = rd.landing if rd.landing is not None else [jax.ShapeDtypeStruct(a.shape, a.dtype) for a in rd.operands]
        if rd.landing is None:
            aliases.update({n_in + len(operands) + i: n_out + len(landing) + i for i in range(len(lands))})
        spans.append((len(operands), len(rd.operands), len(landing), len(lands)))
        operands += rd.operands
        landing += lands
    outs_at = n_in + len(operands)
    scratch_at = outs_at + n_out + len(landing)

    def riding(*refs):
        def each(action):
            for i, (rd, (in_at, n_ops, out_at, n_lands)) in enumerate(zip(riders, spans)):
                sems = refs[scratch_at + n_scratch + 2 * i:scratch_at + n_scratch + 2 * i + 2]
                getattr(rd, action)(refs[n_in + in_at:n_in + in_at + n_ops],
                                    refs[outs_at + n_out + out_at:outs_at + n_out + out_at + n_lands], *sems)

        @pl.when(pl.program_id(0) == 0)
        def _():
            each("start")

        body(*refs[:n_in], *refs[outs_at:outs_at + n_out], *refs[scratch_at:scratch_at + n_scratch])

        @pl.when(pl.program_id(0) == steps - 1)
        def _():
            each("wait")

    any_spec = pl.BlockSpec(memory_space=pl.ANY)
    res = pl.pallas_call(
        riding, name=name, grid=(steps,), in_specs=list(in_specs) + [any_spec] * len(operands),
        out_specs=list(out_specs) + [any_spec] * len(landing), out_shape=list(out_shape) + landing,
        scratch_shapes=list(scratch_shapes) + [pltpu.SemaphoreType.DMA(rd.sems) for rd in riders for _ in range(2)],
        input_output_aliases=aliases, compiler_params=params)(*args, *operands)
    return list(res[:n_out]), [list(res[n_out + out_at:n_out + out_at + n_lands]) for _, _, out_at, n_lands in spans]


_SUB_TILE = 256


def _sub_tiles(tm):
    return [pl.ds(r, _SUB_TILE) for r in range(0, tm, _SUB_TILE)]


def _ffn_fwd(x, g_pre, wg_t, wu_t, wd, g_post, target, name, riders=None, tm=512):
    s, d = x.shape
    ff = wd.shape[0]
    with_loss = target is not None

    def body(*refs):
        if with_loss:
            x_ref, gpre_ref, wg_ref, wu_ref, wd_ref, gpost_ref, t_ref, xo_ref, a_ref, b_ref, f_ref, loss_ref = refs
        else:
            x_ref, gpre_ref, wg_ref, wu_ref, wd_ref, gpost_ref, xo_ref, a_ref, b_ref, f_ref = refs
        loss = 0.0
        for rows in _sub_tiles(tm):
            xv = x_ref[rows, :]
            hb = (xv * _inv_rms(xv) * gpre_ref[...]).astype(BF16)
            a = _dot_nt(hb, wg_ref[...])
            b = _dot_nt(hb, wu_ref[...])
            hh = (a * jax.nn.sigmoid(a)) * b
            f = _dot(hh.astype(BF16), wd_ref[...])
            xo = xv + 0.5 * (f * _inv_rms(f) * gpost_ref[...])
            a_ref[rows, :] = a.astype(BF16)
            b_ref[rows, :] = b.astype(BF16)
            f_ref[rows, :] = f
            if with_loss:
                e = xo - t_ref[rows, :]
                xo_ref[rows, :] = e * (1.0 / d)
                loss = loss + 0.5 * jnp.sum(jnp.mean(e * e, axis=-1, keepdims=True))
            else:
                xo_ref[rows, :] = xo
        if with_loss:
            _accumulate(loss_ref, loss)

    in_specs = [_rows(tm, d), _const((1, d)), _resident((ff, d)), _resident((ff, d)), _resident((ff, d)), _const((1, d))]
    args = [x, g_pre, wg_t, wu_t, wd, g_post]
    out_shape = [jax.ShapeDtypeStruct((s, d), F32), jax.ShapeDtypeStruct((s, ff), BF16),
                 jax.ShapeDtypeStruct((s, ff), BF16), jax.ShapeDtypeStruct((s, d), F32)]
    out_specs = [_rows(tm, d), _rows(tm, ff), _rows(tm, ff), _rows(tm, d)]
    if with_loss:
        in_specs.append(_rows(tm, d))
        args.append(target)
        out_shape.append(jax.ShapeDtypeStruct((8, 128), F32))
        out_specs.append(_const((8, 128)))
    return _hosted_call(body, riders, name=name, steps=s // tm, in_specs=in_specs, out_specs=out_specs, out_shape=out_shape, args=args)


def _ffn_bwd(dxo, x, f, a, b, g_pre, g_post, wg_t, wu_t, wd, name, riders=None, tm=256):
    s, d = x.shape
    ff = wd.shape[0]

    def body(dxo_ref, x_ref, f_ref, a_ref, b_ref, gpre_ref, gpost_ref, wg_ref, wu_ref, wd_ref,
             dx_ref, hh_ref, da_ref, db_ref, df_ref, h_ref, dgpre_ref, dgpost_ref):
        dgpre_sum = dgpost_sum = 0.0
        for rows in _sub_tiles(tm):
            dxo_v = dxo_ref[rows, :]
            fv = f_ref[rows, :]
            df, dgpost = _rms_bwd(fv, _inv_rms(fv), gpost_ref[...], 0.5 * dxo_v)
            dfb = df.astype(BF16)
            dhh = _dot_nt(dfb, wd_ref[...])
            av = a_ref[rows, :].astype(F32)
            bv = b_ref[rows, :].astype(F32)
            sig = jax.nn.sigmoid(av)
            sa = av * sig
            da = (dhh * bv * (sig * (1.0 + av * (1.0 - sig)))).astype(BF16)
            db = (dhh * sa).astype(BF16)
            dh = _dot(da, wg_ref[...]) + _dot(db, wu_ref[...])
            xv = x_ref[rows, :]
            inv = _inv_rms(xv)
            dxn, dgpre = _rms_bwd(xv, inv, gpre_ref[...], dh)
            dx_ref[rows, :] = dxo_v + dxn
            hh_ref[rows, :] = (sa * bv).astype(BF16)
            da_ref[rows, :] = da
            db_ref[rows, :] = db
            df_ref[rows, :] = dfb
            h_ref[rows, :] = (xv * inv * gpre_ref[...]).astype(BF16)
            dgpre_sum, dgpost_sum = dgpre_sum + dgpre, dgpost_sum + dgpost
        _accumulate(dgpre_ref, dgpre_sum)
        _accumulate(dgpost_ref, dgpost_sum)

    return _hosted_call(
        body, riders, name=name, steps=s // tm,
        in_specs=[_rows(tm, d), _rows(tm, d), _rows(tm, d), _rows(tm, ff), _rows(tm, ff), _const((1, d)), _const((1, d)),
                  _resident((ff, d)), _resident((ff, d)), _resident((ff, d))],
        out_specs=[_rows(tm, d), _rows(tm, ff), _rows(tm, ff), _rows(tm, ff), _rows(tm, d), _rows(tm, d),
                   _const((1, d)), _const((1, d))],
        out_shape=[jax.ShapeDtypeStruct((s, d), F32), jax.ShapeDtypeStruct((s, ff), BF16), jax.ShapeDtypeStruct((s, ff), BF16),
                   jax.ShapeDtypeStruct((s, ff), BF16), jax.ShapeDtypeStruct((s, d), BF16), jax.ShapeDtypeStruct((s, d), BF16),
                   jax.ShapeDtypeStruct((1, d), F32), jax.ShapeDtypeStruct((1, d), F32)],
        args=[dxo, x, f, a, b, g_pre, g_post, wg_t, wu_t, wd])


def _wgrad(lhs, rhs, name, riders=None, rt=256):
    s, r = lhs.shape
    c = rhs.shape[1]

    def body(l_ref, r_ref, o_ref):
        o_ref[...] = _dot_tn(l_ref[...], r_ref[...])

    (out,), riding = _hosted_call(
        body, riders, name=name, steps=pl.cdiv(r, rt), in_specs=[pl.BlockSpec((s, rt), lambda i: (0, i)), _resident((s, c))],
        out_specs=[pl.BlockSpec((rt, c), lambda i: (i, 0))], out_shape=[jax.ShapeDtypeStruct((r, c), F32)], args=[lhs, rhs])
    return out, riding


def _attn_dtype(dilation):
    return BF16 if dilation == 1 else F32


def _in_fwd(x, g, w_in_t, name, riders=None, tm=1024):
    s, d = x.shape
    d_in = w_in_t.shape[0]
    n_groups = len(DILATIONS)
    dtypes = [_attn_dtype(dil) for dil in DILATIONS] * 3

    def body(x_ref, g_ref, w_ref, h_ref, u_ref, *part_refs):
        xv = x_ref[...]
        hb = (xv * _inv_rms(xv) * g_ref[...]).astype(BF16)
        h_ref[...] = hb
        z = _dot_nt(hb, w_ref[...])
        u_ref[...] = z[:, :POOL_DIM]
        for j, ref in enumerate(part_refs):
            part = z[:, POOL_DIM + GROUP_DIM * j:POOL_DIM + GROUP_DIM * (j + 1)]
            ref[...] = (part * _SCORE_SCALE if j < n_groups else part).astype(ref.dtype)

    return _hosted_call(
        body, riders, name=name, steps=s // tm, in_specs=[_rows(tm, d), _const((1, d)), _resident((d_in, d))],
        out_specs=[_rows(tm, d), _rows(tm, POOL_DIM)] + [_rows(tm, GROUP_DIM)] * len(dtypes),
        out_shape=[jax.ShapeDtypeStruct((s, d), BF16), jax.ShapeDtypeStruct((s, POOL_DIM), F32)]
        + [jax.ShapeDtypeStruct((s, GROUP_DIM), dt) for dt in dtypes],
        args=[x, g, w_in_t])


def _in_bwd(du, dparts, x, dxo, g, w_in_t, name, riders=None, tm=512):
    s, d = x.shape
    d_in = w_in_t.shape[0]
    n_parts = len(dparts)

    def body(du_ref, *refs):
        part_refs = refs[:n_parts]
        x_ref, dxo_ref, g_ref, w_ref, dx_ref, dz_ref, dg_ref = refs[n_parts:]
        dz = jnp.concatenate([r[...].astype(BF16) for r in (du_ref,) + part_refs], axis=1)
        dz_ref[...] = dz
        dh = _dot(dz, w_ref[...])
        xv = x_ref[...]
        dxn, dg = _rms_bwd(xv, _inv_rms(xv), g_ref[...], dh)
        dx_ref[...] = dxo_ref[...] + dxn
        _accumulate(dg_ref, dg)

    return _hosted_call(
        body, riders, name=name, steps=s // tm,
        in_specs=[_rows(tm, POOL_DIM)] + [_rows(tm, GROUP_DIM)] * n_parts + [_rows(tm, d), _rows(tm, d), _const((1, d)),
                                                                             _resident((d_in, d))],
        out_specs=[_rows(tm, d), _rows(tm, d_in), _const((1, d))],
        out_shape=[jax.ShapeDtypeStruct((s, d), F32), jax.ShapeDtypeStruct((s, d_in), BF16), jax.ShapeDtypeStruct((1, d), F32)],
        args=[du, *dparts, x, dxo, g, w_in_t])


_POOL_HALO = 8


def _pool_chain(v, first_shift):
    n = v.shape[0]
    p2 = v + pltpu.roll(v, first_shift, 0)
    p4 = pltpu.roll(p2, 1, 0) + pltpu.roll(p2, n - 1, 0)
    p8 = pltpu.roll(p4, 2, 0) + pltpu.roll(p4, n - 2, 0)
    p16 = pltpu.roll(p8, 4, 0) + pltpu.roll(p8, n - 4, 0)
    group = lax.broadcasted_iota(jnp.int32, v.shape, 1) // HEAD_DIM
    return jnp.where(group == 0, p2, jnp.where(group == 1, p4, jnp.where(group == 2, p8, p16)))


def _pool_count(t0, rows, s):
    t = t0 + lax.broadcasted_iota(jnp.int32, (rows, POOL_DIM), 0)
    group = lax.broadcasted_iota(jnp.int32, (rows, POOL_DIM), 1) // HEAD_DIM
    half = jnp.where(group == 0, 1, jnp.where(group == 1, 2, jnp.where(group == 2, 4, 8)))
    cnt = jnp.minimum(t + half, s) - jnp.maximum(t - half, 0)
    return jnp.maximum(cnt, 1).astype(F32)


def _pad_rows(ref, pad_ref, s):
    zeros = jnp.zeros((_POOL_HALO, pad_ref.shape[1]), pad_ref.dtype)
    pad_ref[pl.ds(0, _POOL_HALO), :] = zeros
    pad_ref[pl.ds(_POOL_HALO + s, _POOL_HALO), :] = zeros
    pad_ref[pl.ds(_POOL_HALO, s), :] = ref[...]


def _pool_fwd(u, w_bd, scale, name, tm=512):
    s = u.shape[0]
    ext = tm + 2 * _POOL_HALO

    def body(u_ref, w_ref, sc_ref, o_ref, upad):
        _pad_rows(u_ref, upad, s)

        def tile(i, carry):
            t0 = pl.multiple_of(i * tm, tm)
            uv = upad[pl.ds(t0, ext), :]
            win = _pool_chain(uv, 1)[_POOL_HALO:_POOL_HALO + tm]
            y = win / _pool_count(t0, tm, s) - uv[_POOL_HALO:_POOL_HALO + tm]
            o_ref[pl.ds(t0, tm), :] = (_dot(y.astype(BF16), w_ref[...]) * sc_ref[...]).astype(BF16)
            return carry

        lax.fori_loop(0, s // tm, tile, 0)

    return pl.pallas_call(body, name=name, out_shape=jax.ShapeDtypeStruct((s, POOL_DIM), BF16),
                          scratch_shapes=[pltpu.VMEM((s + 2 * _POOL_HALO, POOL_DIM), F32)],
                          compiler_params=_params())(u, w_bd, scale)


def _pool_bwd(u, da, w_bd, scale, name, tm=512):
    s = u.shape[0]
    ext = tm + 2 * _POOL_HALO

    def body(u_ref, da_ref, w_ref, sc_ref, du_ref, dw_ref, dsc_ref, upad, dapad):
        _pad_rows(u_ref, upad, s)
        _pad_rows(da_ref, dapad, s)
        dw_ref[...] = jnp.zeros_like(dw_ref)
        dsc_ref[...] = jnp.zeros_like(dsc_ref)

        def tile(i, carry):
            t0 = pl.multiple_of(i * tm, tm)
            uv = upad[pl.ds(t0, ext), :]
            dav = dapad[pl.ds(t0, ext), :]
            win = _pool_chain(uv, 1)[_POOL_HALO:_POOL_HALO + tm]
            yb = (win / _pool_count(t0, tm, s) - uv[_POOL_HALO:_POOL_HALO + tm]).astype(BF16)
            yl = _dot(yb, w_ref[...])
            da_c = dav[_POOL_HALO:_POOL_HALO + tm]
            dsc_ref[...] += jnp.sum(da_c * yl, axis=0, keepdims=True)
            dyl = (dav * sc_ref[...]).astype(BF16)
            dw_ref[...] += _dot_tn(yb, dyl[_POOL_HALO:_POOL_HALO + tm])
            dy = _dot_nt(dyl, w_ref[...])
            dyc = dy / _pool_count(t0 - _POOL_HALO, ext, s)
            du_ref[pl.ds(t0, tm), :] = (_pool_chain(dyc, ext - 1) - dy)[_POOL_HALO:_POOL_HALO + tm]
            return carry

        lax.fori_loop(0, s // tm, tile, 0)

    pool_cols = pl.BlockSpec((s, POOL_DIM), lambda i: (0, 0), pipeline_mode=pl.Buffered(1))
    return pl.pallas_call(
        body, name=name, grid=(1,),
        in_specs=[pool_cols, pool_cols, _const((POOL_DIM, POOL_DIM)), _const((1, POOL_DIM))],
        out_specs=[_const((s, POOL_DIM)), _const((POOL_DIM, POOL_DIM)), _const((1, POOL_DIM))],
        out_shape=[jax.ShapeDtypeStruct((s, POOL_DIM), F32), jax.ShapeDtypeStruct((POOL_DIM, POOL_DIM), F32),
                   jax.ShapeDtypeStruct((1, POOL_DIM), F32)],
        scratch_shapes=[pltpu.VMEM((s + 2 * _POOL_HALO, POOL_DIM), F32), pltpu.VMEM((s + 2 * _POOL_HALO, POOL_DIM), F32)],
        compiler_params=_params(dimension_semantics=("arbitrary",)))(u, da, w_bd, scale)


_BQ = 128
_KW = _BQ + 2 * N_SIDE
_PAIR = 2 * HEAD_DIM
_NEG = -1e30
_ATTN_UNROLL = 8
_SCORE_SCALE = HEAD_DIM ** -0.5


def _stack_heads(x):
    lane_head = lax.broadcasted_iota(jnp.int32, x.shape, 1) // HEAD_DIM
    zero = jnp.zeros_like(x)
    return jnp.concatenate([jnp.where(lane_head == 0, x, zero), jnp.where(lane_head == 1, x, zero)], axis=0)


def _unstack_heads(x):
    lane_head = lax.broadcasted_iota(jnp.int32, (_BQ, _PAIR), 1) // HEAD_DIM
    return jnp.where(lane_head == 0, x[:_BQ], x[_BQ:])


def _stack_cols(x):
    return jnp.concatenate([x[:, 0:1], x[:, HEAD_DIM:HEAD_DIM + 1]], axis=0)


def _fill_bias(bias_ref, slopes_ref, dilation):
    row = lax.broadcasted_iota(jnp.int32, (2 * _BQ, _KW), 0)
    col = lax.broadcasted_iota(jnp.int32, (2 * _BQ, _KW), 1)
    pair = 2 * pl.program_id(0)
    slope = jnp.where(row < _BQ, slopes_ref[pair], slopes_ref[pair + 1]) * float(dilation)

    @pl.when(pl.program_id(1) == 0)
    def _():
        for j in range(3):
            dist = jnp.abs(col - (row & (_BQ - 1)) - j * N_SIDE)
            bias_ref[j] = jnp.where(dist <= N_SIDE, -slope * dist.astype(F32), _NEG)


def _block_window(i, n_blocks, length):
    q0 = pl.multiple_of(i * _BQ, _BQ)
    ws = pl.multiple_of(jnp.clip(q0 - N_SIDE, 0, length - _KW), N_SIDE)
    return q0, ws, jnp.where(i == 0, 0, jnp.where(i == n_blocks - 1, 2, 1))


_FREE_STRIDE = 4


def _residue_views(dilation, seq, ins, outs, tmps):
    step = pl.program_id(1)
    if dilation <= _FREE_STRIDE:
        def rows(start, count):
            return pl.ds(start, count) if dilation == 1 else pl.ds(start * dilation + step, count, stride=dilation)

        return ins, outs, rows, lambda: None
    inner = dilation // _FREE_STRIDE
    assert inner <= _FREE_STRIDE and len(tmps) == len(ins) + len(outs)
    first, second = step // inner, step % inner
    coarse = pl.ds(first, seq // _FREE_STRIDE, stride=_FREE_STRIDE)
    in_tmps, out_tmps = tmps[:len(ins)], tmps[len(ins):]

    @pl.when(second == 0)
    def _():
        for ref, tmp in zip(ins, in_tmps):
            tmp[...] = ref[coarse, :]

    def flush():
        @pl.when(second == inner - 1)
        def _():
            for ref, tmp in zip(outs, out_tmps):
                ref[coarse, :] = tmp[...]

    return in_tmps, out_tmps, lambda start, count: pl.ds(start * inner + second, count, stride=inner), flush


def _attn_call(body, name, dilation, seq, n_in, out_dtypes, scratch, buffers):
    col = pl.BlockSpec((seq, _PAIR), lambda c, r: (0, c), pipeline_mode=pl.Buffered(buffers))
    tmps = [pltpu.VMEM((seq // _FREE_STRIDE, _PAIR), F32)] * (n_in + len(out_dtypes) if dilation > _FREE_STRIDE else 0)
    return pl.pallas_call(
        body, name=name, grid=(GROUP_DIM // _PAIR, dilation),
        in_specs=[pl.BlockSpec(memory_space=pltpu.SMEM)] + [col] * n_in, out_specs=[col] * len(out_dtypes),
        out_shape=[jax.ShapeDtypeStruct((seq, GROUP_DIM), dt) for dt in out_dtypes], scratch_shapes=scratch + tmps,
        compiler_params=_params(dimension_semantics=("arbitrary", "arbitrary")))


def _staged(dilation, length, rows, sources, scratch):
    if dilation == 1:
        return sources
    for src, dst in zip(sources, scratch):
        dst[...] = src[rows(0, length), :].astype(BF16)
    return scratch


def _attn_fwd(q, k, v, slopes, dilation, name):
    seq = q.shape[0]
    length = seq // dilation
    n_blocks = length // _BQ
    n_stage = 0 if dilation == 1 else 3

    def body(sl_ref, q_ref, k_ref, v_ref, o_ref, lse_ref, *scratch):
        bias_ref, tmps = scratch[n_stage], scratch[n_stage + 1:]
        (q_in, k_in, v_in), (o_out, lse_out), rows, flush = _residue_views(dilation, seq, (q_ref, k_ref, v_ref), (o_ref, lse_ref), tmps)
        qs, ks, vs = _staged(dilation, length, rows, (q_in, k_in, v_in), scratch[:n_stage])
        _fill_bias(bias_ref, sl_ref, dilation)

        def block(i, carry):
            q0, ws, which = _block_window(i, n_blocks, length)
            kw = ks[pl.ds(ws, _KW), :]
            vw = vs[pl.ds(ws, _KW), :]
            sc = _dot_nt(_stack_heads(qs[pl.ds(q0, _BQ), :]), kw) + bias_ref[which]
            m = jnp.max(sc, axis=-1, keepdims=True)
            p = jnp.exp(sc - m)
            den = jnp.sum(p, axis=-1, keepdims=True)
            o_out[rows(q0, _BQ), :] = _unstack_heads(_dot(p.astype(BF16), vw) / den)
            lse_out[rows(q0, _BQ), :] = _unstack_heads(jnp.broadcast_to(m + jnp.log(den), (2 * _BQ, _PAIR)))
            return carry

        lax.fori_loop(0, n_blocks, block, 0, unroll=min(_ATTN_UNROLL, n_blocks))
        flush()

    stage = pltpu.VMEM((length, _PAIR), BF16)
    bias = pltpu.VMEM((3, 2 * _BQ, _KW), F32)
    return _attn_call(body, name, dilation, seq, 3, [F32, F32], [stage] * n_stage + [bias], 2)(slopes, q, k, v)


def _attn_bwd(q, k, v, do, lse, cterm, slopes, dilation, name):
    seq = q.shape[0]
    length = seq // dilation
    n_blocks = length // _BQ
    n_stage = 0 if dilation == 1 else 4

    def body(sl_ref, q_ref, k_ref, v_ref, do_ref, lse_ref, c_ref, dq_ref, dk_ref, dv_ref, *scratch):
        (dk_acc, dv_acc, bias_ref), tmps = scratch[n_stage:n_stage + 3], scratch[n_stage + 3:]
        (q_in, k_in, v_in, do_in, lse_in, c_in), (dq_out, dk_out, dv_out), rows, flush = _residue_views(
            dilation, seq, (q_ref, k_ref, v_ref, do_ref, lse_ref, c_ref), (dq_ref, dk_ref, dv_ref), tmps)
        all_rows = rows(0, length)
        qs, ks, vs, dos = _staged(dilation, length, rows, (q_in, k_in, v_in, do_in), scratch[:n_stage])
        dk_acc[...] = jnp.zeros_like(dk_acc)
        dv_acc[...] = jnp.zeros_like(dv_acc)
        _fill_bias(bias_ref, sl_ref, dilation)

        def block(i, carry):
            q0, ws, which = _block_window(i, n_blocks, length)
            qm = _stack_heads(qs[pl.ds(q0, _BQ), :])
            dom = _stack_heads(dos[pl.ds(q0, _BQ), :])
            kw = ks[pl.ds(ws, _KW), :]
            vw = vs[pl.ds(ws, _KW), :]
            p = jnp.exp(_dot_nt(qm, kw) + bias_ref[which] - _stack_cols(lse_in[rows(q0, _BQ), :]))
            ds = (p * (_dot_nt(dom, vw) + _stack_cols(c_in[rows(q0, _BQ), :]))).astype(BF16)
            dq_out[rows(q0, _BQ), :] = (_unstack_heads(_dot(ds, kw)) * _SCORE_SCALE).astype(dq_out.dtype)
            dk_acc[pl.ds(ws, _KW), :] += _dot_tn(ds, qm)
            dv_acc[pl.ds(ws, _KW), :] += _dot_tn(p.astype(BF16), dom)
            return carry

        lax.fori_loop(0, n_blocks, block, 0, unroll=min(_ATTN_UNROLL, n_blocks))
        dk_out[all_rows, :] = dk_acc[...].astype(dk_out.dtype)
        dv_out[all_rows, :] = dv_acc[...].astype(dv_out.dtype)
        flush()

    stage = pltpu.VMEM((length, _PAIR), BF16)
    acc = pltpu.VMEM((length, _PAIR), F32)
    bias = pltpu.VMEM((3, 2 * _BQ, _KW), F32)
    return _attn_call(body, name, dilation, seq, 6, [_attn_dtype(dilation)] * 3, [stage] * n_stage + [acc] * 2 + [bias],
                      2 if dilation == 1 else 1)(slopes, q, k, v, do, lse, cterm)


def _group_weights(lses):
    m = jnp.maximum(jnp.maximum(lses[0], lses[1]), lses[2])
    es = [jnp.exp(l - m) for l in lses]
    den = es[0] + es[1] + es[2]
    return [e / den for e in es]


def _out_fwd(a_pool, outs, lses, x, w_out, g, name, tm=1024):
    s, d = x.shape
    width = POOL_DIM + 3 * GROUP_DIM

    def body(ap_ref, o0, o1, o2, l0, l1, l2, x_ref, w_ref, g_ref, xo_ref, mix_ref, cat_ref):
        alphas = _group_weights([l0[...], l1[...], l2[...]])
        cat = jnp.concatenate([ap_ref[...]] + [(o[...] * al).astype(BF16) for o, al in zip((o0, o1, o2), alphas)], axis=1)
        cat_ref[...] = cat
        mix = _dot(cat, w_ref[...])
        mix_ref[...] = mix
        xo_ref[...] = x_ref[...] + mix * _inv_rms(mix) * g_ref[...]

    return pl.pallas_call(
        body, name=name, grid=(s // tm,),
        in_specs=[_rows(tm, POOL_DIM)] + [_rows(tm, GROUP_DIM)] * 6 + [_rows(tm, d), _resident(w_out.shape), _const((1, d))],
        out_specs=[_rows(tm, d), _rows(tm, d), _rows(tm, width)],
        out_shape=[jax.ShapeDtypeStruct((s, d), F32), jax.ShapeDtypeStruct((s, d), F32), jax.ShapeDtypeStruct((s, width), BF16)],
        compiler_params=_params(dimension_semantics=("arbitrary",)))(a_pool, *outs, *lses, x, w_out, g)


def _out_bwd(dxo, mix, outs, lses, w_out, g, head_ones, name, tm=1024):
    s, d = mix.shape

    def body(dxo_ref, mix_ref, o0, o1, o2, l0, l1, l2, w_ref, g_ref, ones_ref, dpool_ref, dmix_ref, do0, do1, do2, c0, c1, c2, dg_ref):
        mv = mix_ref[...]
        dmix, dg = _rms_bwd(mv, _inv_rms(mv), g_ref[...], dxo_ref[...])
        dmb = dmix.astype(BF16)
        dmix_ref[...] = dmb
        _accumulate(dg_ref, dg)
        dcat = _dot_nt(dmb, w_ref[...])
        dpool_ref[...] = dcat[:, :POOL_DIM]
        alphas = _group_weights([l0[...], l1[...], l2[...]])
        das = [dcat[:, POOL_DIM + GROUP_DIM * j:POOL_DIM + GROUP_DIM * (j + 1)] for j in range(3)]
        prod = sum(da * (o[...] * al) for da, o, al in zip(das, (o0, o1, o2), alphas))
        hi = prod.astype(BF16)
        lo = (prod - hi.astype(F32)).astype(BF16)
        total = _dot(hi, ones_ref[...]) + _dot(lo, ones_ref[...])
        for da, al, do_ref, c_ref in zip(das, alphas, (do0, do1, do2), (c0, c1, c2)):
            do_ref[...] = (da * al).astype(do_ref.dtype)
            c_ref[...] = -al * total

    return pl.pallas_call(
        body, name=name, grid=(s // tm,),
        in_specs=[_rows(tm, d), _rows(tm, d)] + [_rows(tm, GROUP_DIM)] * 6 + [_resident(w_out.shape), _const((1, d)),
                                                                             _const((GROUP_DIM, GROUP_DIM))],
        out_specs=[_rows(tm, POOL_DIM), _rows(tm, d)] + [_rows(tm, GROUP_DIM)] * 6 + [_const((1, d))],
        out_shape=[jax.ShapeDtypeStruct((s, POOL_DIM), F32), jax.ShapeDtypeStruct((s, d), BF16)]
        + [jax.ShapeDtypeStruct((s, GROUP_DIM), _attn_dtype(dil)) for dil in DILATIONS]
        + [jax.ShapeDtypeStruct((s, GROUP_DIM), F32)] * 3 + [jax.ShapeDtypeStruct((1, d), F32)],
        compiler_params=_params(dimension_semantics=("arbitrary",)))(dxo, mix, *outs, *lses, w_out, g, head_ones)


def _alibi_slopes():
    return np.array([2.0 ** (-8.0 * (i + 1) / N_ATTN_HEADS) for i in range(N_ATTN_HEADS)], np.float32)


def _block_diag(w_lin):
    n, c, _ = w_lin.shape
    eye = jnp.eye(n, dtype=w_lin.dtype)
    return (eye[:, None, :, None] * w_lin[:, :, None, :]).reshape(n * c, n * c)


class _NoExchange:
    def __init__(self, full):
        self.full, self.grads = full, {}

    def first_weights(self):
        return self.full

    def riders(self, host):
        return []

    def landed(self, host, results):
        return self.full

    def gradient(self, name, grad):
        self.grads[name] = grad


def _local_step(x, target, small, exchange):
    s, d = x.shape
    slopes = _alibi_slopes()
    group_slopes = [jnp.asarray(slopes[4 * g:4 * g + 4]) for g in range(3)]
    w_bd = _block_diag(small["w_pool_lin"]).astype(BF16)
    head_ones = jnp.asarray(np.kron(np.eye(GROUP_DIM // HEAD_DIM), np.ones((HEAD_DIM, HEAD_DIM))), BF16)

    full = dict(exchange.first_weights())

    def hosted(call, host, *args):
        results, riding = call(*args, host, exchange.riders(host))
        full.update(exchange.landed(host, riding) or {})
        return results

    x1, a1, b1, f1 = hosted(_ffn_fwd, "ffn1_fwd", x, small["g_ffn1_pre"], full["w1_gate"], full["w1_up"], full["w1_down"],
                            small["g_ffn1_post"], None)
    h2, u, *parts = hosted(_in_fwd, "in_fwd", x1, small["g_mix_pre"], full["w_in"])
    qs, ks, vs = parts[0:3], parts[3:6], parts[6:9]
    a_pool = _pool_fwd(u, w_bd, small["pool_scale"], "pool_fwd")
    outs, lses = [], []
    for g, dil in enumerate(DILATIONS):
        o, lse = _attn_fwd(qs[g], ks[g], vs[g], group_slopes[g], dil, f"attn_fwd{g}")
        outs.append(o)
        lses.append(lse)
    x2, mix, cat = _out_fwd(a_pool, outs, lses, x1, full["w_out"], small["g_mix_post"], "out_fwd")
    (dx3, a2, b2, f2, loss_part), _ = _ffn_fwd(x2, small["g_ffn2_pre"], full["w2_gate"], full["w2_up"], full["w2_down"],
                                               small["g_ffn2_post"], target, "ffn2_fwd")

    small_grads = {}

    def ffn_backward(tag, dxo, x_in, f, a, b):
        n = tag[-1]
        dx, hh, da, db, df, h, dg_pre, dg_post = hosted(
            _ffn_bwd, f"{tag}_bwd", dxo, x_in, f, a, b, small[f"g_{tag}_pre"], small[f"g_{tag}_post"],
            full[f"w{n}_gate"], full[f"w{n}_up"], full[f"w{n}_down"])
        for part, lhs, rhs in (("down", hh, df), ("gate", da, h), ("up", db, h)):
            exchange.gradient(f"w{n}_{part}", hosted(_wgrad, f"{tag}_wgrad_{part}", lhs, rhs))
        small_grads[f"g_{tag}_pre"], small_grads[f"g_{tag}_post"] = dg_pre, dg_post
        return dx

    dx2 = ffn_backward("ffn2", dx3, x2, f2, a2, b2)
    dpool, dmix, *dos_cs, small_grads["g_mix_post"] = _out_bwd(dx2, mix, outs, lses, full["w_out"], small["g_mix_post"],
                                                               head_ones, "out_bwd")
    dos, cs = dos_cs[:3], dos_cs[3:]
    dqs, dks, dvs = [], [], []
    for g, dil in enumerate(DILATIONS):
        dq, dk, dv = _attn_bwd(qs[g], ks[g], vs[g], dos[g], lses[g], cs[g], group_slopes[g], dil, f"attn_bwd{g}")
        dqs.append(dq)
        dks.append(dk)
        dvs.append(dv)
    du, dw_bd, small_grads["pool_scale"] = _pool_bwd(u, dpool, w_bd, small["pool_scale"], "pool_bwd")
    n_pool = len(POOL_HALF_WINDOWS)
    small_grads["w_pool_lin"] = jnp.stack(
        [dw_bd[HEAD_DIM * g:HEAD_DIM * (g + 1), HEAD_DIM * g:HEAD_DIM * (g + 1)] for g in range(n_pool)])
    dx1, dz, small_grads["g_mix_pre"] = hosted(_in_bwd, "in_bwd", du, dqs + dks + dvs, x1, dx2, small["g_mix_pre"], full["w_in"])
    exchange.gradient("w_in", hosted(_wgrad, "wgrad_in", dz, h2))
    dx0 = ffn_backward("ffn1", dx1, x, f1, a1, b1)
    exchange.gradient("w_out", hosted(_wgrad, "wgrad_out", cat, dmix))
    return loss_part[0, 0], dx0, small_grads


SEGMENTS = ("w1_gate", "w1_up", "w1_down", "w_in", "w_out", "w2_gate", "w2_up", "w2_down")
TRANSPOSED = ("w1_gate", "w1_up", "w_in", "w2_gate", "w2_up")
ROWS_OUTSIDE = ("w1_gate", "w1_up", "w2_gate", "w2_up")
HALF = 512


def _place():
    x, y, c = lax.axis_index("x"), lax.axis_index("y"), lax.axis_index("c")
    other_chips = [(1 - x, y), (x, 1 - y), (1 - x, 1 - y)]
    return x, y, c, other_chips


def _chip_rows(chip, rows):
    return pl.ds(pl.multiple_of((2 * chip[0] + chip[1]) * rows, 16), rows)


def _cols(c):
    return pl.ds(pl.multiple_of(c * HALF, HALF), HALF)


def _cast_shards(shards, transposed, place, name):
    n = len(shards)
    rows = [w.shape[1] if t else w.shape[0] for w, t in zip(shards, transposed)]

    def body(place_ref, *refs):
        for w_ref, o_ref, t in zip(refs[:n], refs[n:], transposed):
            o_ref[...] = (w_ref[...].T if t else w_ref[...]).astype(BF16)

    once = pl.Buffered(1)
    return pl.pallas_call(
        body, name=name,
        grid_spec=pltpu.PrefetchScalarGridSpec(
            num_scalar_prefetch=1, grid=(1,),
            in_specs=[pl.BlockSpec(w.shape, lambda i, place: (0, 0), pipeline_mode=once) for w in shards],
            out_specs=[pl.BlockSpec((r, 1024), lambda i, place: (place[0], 0), pipeline_mode=once) for r in rows]),
        out_shape=[jax.ShapeDtypeStruct((N_CHIPS * r, 1024), BF16) for r in rows],
        compiler_params=_params(dimension_semantics=("arbitrary",)))(place, *shards)


def _gather_weights(bufs):
    n = len(bufs)
    rows = [b.shape[0] // N_CHIPS for b in bufs]

    def halves(r):
        first = -(-r // 32) * 16
        return (0, first), (first, r - first)

    def body(*refs):
        outs = refs[n:2 * n]
        ici_send, ici_recv, d2d_send, d2d_recv = refs[2 * n:]
        x, y, c, _ = _place()
        me, via_x, via_y, diagonal = (x, y), (1 - x, y), (x, 1 - y), (1 - x, 1 - y)

        def piece(chip, k, h, cols):
            start, size = halves(rows[k])[h]
            return outs[k].at[pl.ds(pl.multiple_of((2 * chip[0] + chip[1]) * rows[k] + start, 16), size), _cols(cols)]

        def ici(path, chip, k, h, to):
            blk = piece(chip, k, h, c)
            return pltpu.make_async_remote_copy(src_ref=blk, dst_ref=blk, send_sem=ici_send.at[path, k, h],
                                                recv_sem=ici_recv.at[path, k, h], device_id=(*to, c), device_id_type=MESH)

        def d2d(slot, chip, k, h, cols):
            blk = piece(chip, k, h, cols)
            return pltpu.make_async_remote_copy(src_ref=blk, dst_ref=blk, send_sem=d2d_send.at[slot, k, h],
                                                recv_sem=d2d_recv.at[slot, k, h], device_id=(x, y, 1 - c), device_id_type=MESH)

        started = [ici(0, me, k, h, via_x) for h in (0, 1) for k in range(n)] + [ici(1, me, k, h, via_y) for h in (1, 0) for k in range(n)]
        for cp in started:
            cp.start()

        def landed(path, slot, chip, k, h, pass_on_to=None):
            ici(path, chip, k, h, me).wait_recv()
            more = [d2d(slot, chip, k, h, c)] + ([ici(2, chip, k, h, pass_on_to)] if pass_on_to else [])
            for cp in more:
                cp.start()
            started.extend(more)

        for k in range(n):
            landed(0, 0, via_x, k, 0, pass_on_to=via_y)
            landed(1, 1, via_y, k, 1, pass_on_to=via_x)
        for k in range(n):
            landed(0, 0, via_x, k, 1)
            landed(1, 1, via_y, k, 0)
        for k in range(n):
            for h in range(2):
                landed(2, 2, diagonal, k, h)
        for slot, chip in enumerate((via_x, via_y, diagonal)):
            for k in range(n):
                for h in range(2):
                    d2d(slot, chip, k, h, 1 - c).wait_recv()
        for cp in started:
            cp.wait_send()

    any_spec = pl.BlockSpec(memory_space=pl.ANY)
    return pl.pallas_call(
        body, name="gather_weights", in_specs=[any_spec] * n, out_specs=[any_spec] * n,
        out_shape=[jax.ShapeDtypeStruct(b.shape, b.dtype) for b in bufs], input_output_aliases={k: k for k in range(n)},
        scratch_shapes=[pltpu.SemaphoreType.DMA((3, n, 2))] * 4)(*bufs)


def _gather_rider(bufs):
    n = len(bufs)
    rows = [b.shape[0] // N_CHIPS for b in bufs]

    def copies(outs, send_sems, recv_sems, inbound):
        x, y, c, chips = _place()
        for j, chip in enumerate(chips):
            for k in range(n):
                src_chip = chip if inbound else (x, y)
                blk = outs[k].at[_chip_rows(src_chip, rows[k]), _cols(c)]
                yield pltpu.make_async_remote_copy(src_ref=blk, dst_ref=blk, send_sem=send_sems.at[j, k], recv_sem=recv_sems.at[j, k],
                                                   device_id=(*chip, c), device_id_type=MESH)

    def start(ins, outs, send_sems, recv_sems):
        for cp in copies(outs, send_sems, recv_sems, False):
            cp.start()

    def wait(ins, outs, send_sems, recv_sems):
        for cp in copies(outs, send_sems, recv_sems, True):
            cp.wait_recv()
        for cp in copies(outs, send_sems, recv_sems, False):
            cp.wait_send()

    return _Rider(list(bufs), None, (3, n), start, wait)


def _forward_rider(bufs):
    n = len(bufs)
    rows = [b.shape[0] // N_CHIPS for b in bufs]

    def copies(outs, send_sems, recv_sems, half):
        x, y, c, chips = _place()
        for j, chip in enumerate(chips):
            for k in range(n):
                blk = outs[k].at[_chip_rows(chip, rows[k]), _cols(half(c))]
                yield pltpu.make_async_remote_copy(src_ref=blk, dst_ref=blk, send_sem=send_sems.at[j, k], recv_sem=recv_sems.at[j, k],
                                                   device_id=(x, y, 1 - c), device_id_type=MESH)

    def start(ins, outs, send_sems, recv_sems):
        for cp in copies(outs, send_sems, recv_sems, lambda c: c):
            cp.start()

    def wait(ins, outs, send_sems, recv_sems):
        for cp in copies(outs, send_sems, recv_sems, lambda c: 1 - c):
            cp.wait_recv()
        for cp in copies(outs, send_sems, recv_sems, lambda c: c):
            cp.wait_send()

    return _Rider(list(bufs), None, (3, n), start, wait)


def _sibling_rider(grads):
    n = len(grads)

    def copies(ins, outs, send_sems, recv_sems):
        x, y, c, _ = _place()
        return [pltpu.make_async_remote_copy(src_ref=ins[k].at[:, pl.ds(1 - c, 1)], dst_ref=outs[k], send_sem=send_sems.at[k],
                                             recv_sem=recv_sems.at[k], device_id=(x, y, 1 - c), device_id_type=MESH)
                for k in range(n)]

    def start(*refs):
        for cp in copies(*refs):
            cp.start()

    def wait(*refs):
        for cp in copies(*refs):
            cp.wait()

    return _Rider(list(grads), [jax.ShapeDtypeStruct((N_CHIPS, 1) + g.shape[2:], F32) for g in grads], (n,), start, wait)


def _alone(rider, name):
    n = len(rider.operands)
    landing = rider.landing if rider.landing is not None else [jax.ShapeDtypeStruct(a.shape, a.dtype) for a in rider.operands]
    n_out = len(landing)

    def body(*refs):
        rider.start(refs[:n], refs[n:n + n_out], *refs[n + n_out:])
        rider.wait(refs[:n], refs[n:n + n_out], *refs[n + n_out:])

    any_spec = pl.BlockSpec(memory_space=pl.ANY)
    return pl.pallas_call(body, name=name, in_specs=[any_spec] * n, out_specs=[any_spec] * n_out, out_shape=landing,
                          input_output_aliases={i: i for i in range(n)} if rider.landing is None else {},
                          scratch_shapes=[pltpu.SemaphoreType.DMA(rider.sems)] * 2)(*rider.operands)


def _chip_sum(grad, from_sibling, place, name):
    rh, width = grad.shape[2:]

    def body(place_ref, g_ref, s_ref, own_ref, all_ref):
        all_ref[...] = (g_ref[...] + s_ref[...]).astype(BF16)
        mine = place_ref[0]
        own_ref[0] = g_ref[mine, 0] + s_ref[mine, 0]

    blk = (N_CHIPS, 1, rh, width)
    once = pl.Buffered(1)
    return pl.pallas_call(
        body, name=name,
        grid_spec=pltpu.PrefetchScalarGridSpec(
            num_scalar_prefetch=1, grid=(1,),
            in_specs=[pl.BlockSpec(blk, lambda i, place: (0, place[1], 0, 0), pipeline_mode=once),
                      pl.BlockSpec(blk, lambda i, place: (0, 0, 0, 0), pipeline_mode=once)],
            out_specs=[pl.BlockSpec((1, rh, width), lambda i, place: (0, 0, 0), pipeline_mode=once),
                       pl.BlockSpec(blk, lambda i, place: (0, 0, 0, 0), pipeline_mode=once)]),
        out_shape=[jax.ShapeDtypeStruct((1, rh, width), F32), jax.ShapeDtypeStruct((N_CHIPS, 1, rh, width), BF16)],
        compiler_params=_params(dimension_semantics=("arbitrary",)))(place, grad, from_sibling)


def _scatter_rider(sums):
    n = len(sums)

    def copies(ins, outs, send_sems, recv_sems):
        x, y, c, chips = _place()
        return [pltpu.make_async_remote_copy(src_ref=ins[k].at[pl.ds(2 * chip[0] + chip[1], 1)], dst_ref=outs[k].at[pl.ds(j, 1)],
                                             send_sem=send_sems.at[j, k], recv_sem=recv_sems.at[j, k],
                                             device_id=(*chip, c), device_id_type=MESH)
                for j, chip in enumerate(chips) for k in range(n)]

    def start(*refs):
        for cp in copies(*refs):
            cp.start()

    def wait(*refs):
        for cp in copies(*refs):
            cp.wait()

    return _Rider(list(sums), [jax.ShapeDtypeStruct((3,) + sm.shape[1:], BF16) for sm in sums], (3, n), start, wait)


def _total_sums(owns, received, name, riders=None):
    n = len(owns)

    def body(*refs):
        for o_ref, r_ref, t_ref in zip(refs[:n], refs[n:2 * n], refs[2 * n:]):
            total = o_ref[0]
            for j in range(3):
                total = total + r_ref[j, 0].astype(F32)
            t_ref[0] = total

    return _hosted_call(body, riders, name=name, steps=1, in_specs=[_resident(a.shape) for a in owns + received],
                        out_specs=[_resident(o.shape) for o in owns], out_shape=[jax.ShapeDtypeStruct(o.shape, F32) for o in owns],
                        args=owns + received)


def _swap_rider(halves):
    n = len(halves)

    def copies(ins, outs, send_sems, recv_sems):
        x, y, c, _ = _place()
        return [pltpu.make_async_remote_copy(src_ref=ins[k], dst_ref=outs[k], send_sem=send_sems.at[k], recv_sem=recv_sems.at[k],
                                             device_id=(x, y, 1 - c), device_id_type=MESH) for k in range(n)]

    def start(*refs):
        for cp in copies(*refs):
            cp.start()

    def wait(*refs):
        for cp in copies(*refs):
            cp.wait()

    return _Rider(list(halves), [jax.ShapeDtypeStruct(h.shape, F32) for h in halves], (n,), start, wait)


N_DEV = 8


def _gather_small(block):
    m_per, width = block.shape

    def body(x_ref, out_ref, send_sems, recv_sems, local_sem):
        x, y, c, chips = _place()
        me, sibling = (x, y, c), (x, y, 1 - c)

        def rows(px, py, pc):
            return out_ref.at[pl.ds((4 * px + 2 * py + pc) * m_per, m_per), :]

        def copy(k, blk, to, src=None):
            return pltpu.make_async_remote_copy(src_ref=rows(*blk) if src is None else src, dst_ref=rows(*blk),
                                                send_sem=send_sems.at[k], recv_sem=recv_sems.at[k], device_id=to, device_id_type=MESH)

        mine = pltpu.make_async_copy(x_ref, rows(*me), local_sem)
        mine.start()
        first = [copy(0, me, sibling, src=x_ref)] + [copy(1 + j, me, (*chip, c), src=x_ref) for j, chip in enumerate(chips)]
        for cp in first:
            cp.start()
        passed = [copy(4 + j, (*chip, c), sibling) for j, chip in enumerate(chips)]
        for j, chip in enumerate(chips):
            copy(1 + j, (*chip, c), me).wait_recv()
            passed[j].start()
        copy(0, sibling, me).wait_recv()
        for j, chip in enumerate(chips):
            copy(4 + j, (*chip, 1 - c), me).wait_recv()
        for cp in first + passed:
            cp.wait_send()
        mine.wait()

    vmem = pl.BlockSpec(memory_space=pltpu.VMEM)
    return pl.pallas_call(body, name="gather_small", out_shape=jax.ShapeDtypeStruct((N_DEV * m_per, width), F32),
                          in_specs=[vmem], out_specs=vmem,
                          scratch_shapes=[pltpu.SemaphoreType.DMA((7,)), pltpu.SemaphoreType.DMA((7,)),
                                          pltpu.SemaphoreType.DMA])(block)


def _adamw_math(w, g, m, v):
    m = ADAM_B1 * m + (1.0 - ADAM_B1) * g
    v = ADAM_B2 * v + (1.0 - ADAM_B2) * (g * g)
    m_hat = m / (1.0 - ADAM_B1 ** ADAM_STEP)
    v_hat = v / (1.0 - ADAM_B2 ** ADAM_STEP)
    delta = -ADAM_LR * (m_hat / (jnp.sqrt(v_hat) + ADAM_EPS) + ADAM_WD * w)
    return delta, m, v


def _adamw(w, mine, siblings, place, m, v, transposed, name, riders=None):
    rh, width = mine.shape[1:]
    place_spec = pl.BlockSpec(memory_space=pltpu.SMEM)
    halves = [_const((1, rh, width))] * 2
    out_shape = [jax.ShapeDtypeStruct(w.shape, F32)] * 4
    if transposed:
        def body(place_ref, w_ref, mine_ref, sib_ref, m_ref, v_ref, go_ref, d_ref, mo_ref, vo_ref):
            first = place_ref[1] == 0
            g = jnp.concatenate([jnp.where(first, mine_ref[0], sib_ref[0]), jnp.where(first, sib_ref[0], mine_ref[0])], axis=0).T
            go_ref[...] = g
            d_ref[...], mo_ref[...], vo_ref[...] = _adamw_math(w_ref[...], g, m_ref[...], v_ref[...])

        whole = _resident(w.shape)
        return _hosted_call(body, riders, name=name, steps=1, in_specs=[place_spec, whole] + halves + [whole, whole],
                            out_specs=[whole] * 4, out_shape=out_shape, args=[place, w, mine, siblings, m, v])

    def body(place_ref, w_ref, mine_ref, sib_ref, m_ref, v_ref, go_ref, d_ref, mo_ref, vo_ref):
        g = jnp.where(pl.program_id(0) == place_ref[1], mine_ref[0], sib_ref[0])
        go_ref[...] = g
        d_ref[...], mo_ref[...], vo_ref[...] = _adamw_math(w_ref[...], g, m_ref[...], v_ref[...])

    half = _rows(rh, width)
    return _hosted_call(body, riders, name=name, steps=2, in_specs=[place_spec, half] + halves + [half, half],
                        out_specs=[half] * 4, out_shape=out_shape, args=[place, w, mine, siblings, m, v])


def _adamw_small(gathered, w, m, v, name):
    def body(ga_ref, w_ref, m_ref, v_ref, go_ref, d_ref, mo_ref, vo_ref):
        g = ga_ref[0]
        for dev in range(1, N_DEV):
            g = g + ga_ref[dev]
        go_ref[...] = g
        d_ref[...], mo_ref[...], vo_ref[...] = _adamw_math(w_ref[...], g, m_ref[...], v_ref[...])

    return pl.pallas_call(body, name=name, out_shape=[jax.ShapeDtypeStruct(w.shape, F32)] * 4,
                          compiler_params=_params())(gathered, w, m, v)


class _Exchange:
    FIRST = ("w1_gate", "w1_up", "w1_down")
    HOSTS = {"ffn2_wgrad_gate": (("w2_down",), ()), "ffn2_wgrad_up": (("w2_gate",), ("w2_down",)),
             "in_bwd": (("w2_up",), ("w2_gate",)), "wgrad_in": ((), ("w2_up",)),
             "ffn1_wgrad_down": ((), ("w_in",)), "ffn1_wgrad_gate": (("w1_down",), ()), "ffn1_wgrad_up": ((), ("w1_down", "w1_gate")),
             "wgrad_out": ((), ("w1_up",))}
    ALONE = ("w_in", "w1_gate", "w1_up", "w_out")

    def __init__(self, bufs, place):
        self.bufs, self.place = bufs, place
        self.later = [k for k in SEGMENTS if k not in self.FIRST]
        self.split, self.own, self.to_send, self.received = {}, {}, {}, {}

    def first_weights(self):
        return dict(zip(self.FIRST, _gather_weights([self.bufs[k] for k in self.FIRST])))

    def riders(self, host):
        if host == "ffn1_fwd":
            return [_gather_rider([self.bufs[k] for k in self.later])]
        if host == "in_fwd":
            return [_forward_rider([self.bufs[k] for k in self.later[1:]])]
        halves, sums = self.HOSTS.get(host, ((), ()))
        return ([_sibling_rider([self.split[k] for k in halves])] if halves else []) + (
            [_scatter_rider([self.to_send[k] for k in sums])] if sums else [])

    def landed(self, host, results):
        if host == "ffn1_fwd":
            self.bufs.update(zip(self.later, results[0]))
            return dict(zip(self.later[:1], _alone(_forward_rider([self.bufs[self.later[0]]]), "gather_forward_first")))
        if host == "in_fwd":
            return dict(zip(self.later[1:], results[0]))
        halves, sums = self.HOSTS.get(host, ((), ()))
        if halves:
            self._chip_sums(halves, results[0])
        if sums:
            self.received.update(zip(sums, results[-1]))

    def gradient(self, name, grad):
        self.split[name] = grad.reshape(N_CHIPS, 2, grad.shape[0] // (2 * N_CHIPS), grad.shape[1])
        if name in self.ALONE:
            self._chip_sums([name], _alone(_sibling_rider([self.split[name]]), f"reduce_sibling_{name}"))

    def _chip_sums(self, names, from_sibling):
        for k, fs in zip(names, from_sibling):
            self.own[k], self.to_send[k] = _chip_sum(self.split[k], fs, self.place, f"chip_sum_{k}")

    def finish(self, adamw):
        late = [k for k in SEGMENTS if k not in self.received]
        early = [k for k in SEGMENTS if k in self.received]
        a, b = early[:len(early) // 2], early[len(early) // 2:]
        totals_a, (landed_late,) = _total_sums([self.own[k] for k in a], [self.received[k] for k in a], "total_sums_a",
                                               [_scatter_rider([self.to_send[k] for k in late])])
        self.received.update(zip(late, landed_late))
        totals_b, (siblings_a,) = _total_sums([self.own[k] for k in b + late], [self.received[k] for k in b + late], "total_sums_b",
                                              [_swap_rider(totals_a)])
        mine = dict(zip(a + b + late, totals_a + totals_b))
        siblings = dict(zip(a, siblings_a))
        results = {}
        results[a[0]], (siblings_b,) = adamw(a[0], mine[a[0]], siblings[a[0]], [_swap_rider(totals_b)])
        siblings.update(zip(b + late, siblings_b))
        for k in a[1:] + b + late:
            results[k], _ = adamw(k, mine[k], siblings[k], [])
        return results


SMALL = ("g_ffn1_pre", "g_ffn1_post", "g_mix_pre", "w_pool_lin", "pool_scale", "g_mix_post", "g_ffn2_pre", "g_ffn2_post")
WEIGHTS = ("g_ffn1_pre", "w1_gate", "w1_up", "w1_down", "g_ffn1_post", "g_mix_pre", "w_in", "w_pool_lin", "pool_scale", "w_out",
           "g_mix_post", "g_ffn2_pre", "w2_gate", "w2_up", "w2_down", "g_ffn2_post")
LANES = 128


def _pack_small(tree, extra=0.0):
    flat = jnp.concatenate([tree[k].reshape(-1) for k in SMALL] + [jnp.reshape(extra, (1,)).astype(F32)])
    rows = -(-flat.shape[0] // (8 * LANES)) * 8
    return jnp.pad(flat, (0, rows * LANES - flat.shape[0])).reshape(rows, LANES)


def _unpack_small(packed, like):
    flat, out, at = packed.reshape(-1), {}, 0
    for k in SMALL:
        size = math.prod(like[k].shape)
        out[k] = flat[at:at + size].reshape(like[k].shape)
        at += size
    return out


def kernel(x, g_ffn1_pre, w1_gate, w1_up, w1_down, g_ffn1_post, g_mix_pre, w_in, w_pool_lin, pool_scale, w_out, g_mix_post, g_ffn2_pre, w2_gate, w2_up, w2_down, g_ffn2_post, loss_target, m_g_ffn1_pre, m_w1_gate, m_w1_up, m_w1_down, m_g_ffn1_post, m_g_mix_pre, m_w_in, m_w_pool_lin, m_pool_scale, m_w_out, m_g_mix_post, m_g_ffn2_pre, m_w2_gate, m_w2_up, m_w2_down, m_g_ffn2_post, v_g_ffn1_pre, v_w1_gate, v_w1_up, v_w1_down, v_g_ffn1_post, v_g_mix_pre, v_w_in, v_w_pool_lin, v_pool_scale, v_w_out, v_g_mix_post, v_g_ffn2_pre, v_w2_gate, v_w2_up, v_w2_down, v_g_ffn2_post):
    given = dict(locals())
    w = {k: given[k] for k in WEIGHTS}
    m = {k: given["m_" + k] for k in WEIGHTS}
    v = {k: given["v_" + k] for k in WEIGHTS}
    small = {k: (w[k][0] if k == "w_pool_lin" else w[k].reshape(1, -1)) for k in SMALL}

    place = jnp.stack([2 * lax.axis_index("x") + lax.axis_index("y"), lax.axis_index("c")]).astype(jnp.int32)
    def as_rows(a, k):
        return jnp.swapaxes(a, 1, 2)[0] if k in ROWS_OUTSIDE else a[0]

    def as_given(a, k):
        return jnp.swapaxes(a[None], 1, 2) if k in ROWS_OUTSIDE else a[None]

    in_kernel = [k for k in TRANSPOSED if k not in ROWS_OUTSIDE]
    bufs = {}
    for tag, names in (("first", _Exchange.FIRST), ("rest", [k for k in SEGMENTS if k not in _Exchange.FIRST])):
        bufs.update(zip(names, _cast_shards([as_rows(w[k], k) for k in names], [k in in_kernel for k in names], place, f"cast_{tag}")))
    exchange = _Exchange(bufs, place)
    loss_part, grad_x, small_grads = _local_step(x[0], loss_target[0], small, exchange)

    def adamw(k, mine, siblings, riders):
        return _adamw(as_rows(w[k], k), mine, siblings, place, as_rows(m[k], k), as_rows(v[k], k), k in in_kernel, f"adamw_{k}", riders)

    out_grad, out_delta, out_m, out_v = {}, {}, {}, {}
    for k, results in exchange.finish(adamw).items():
        out_grad[k], out_delta[k], out_m[k], out_v[k] = (as_given(a, k) for a in results)

    small_grads["w_pool_lin"] = small_grads["w_pool_lin"][None]
    packed = _pack_small(small_grads, loss_part)
    gathered = _gather_small(packed).reshape(N_DEV, *packed.shape)
    like = {k: w[k] for k in SMALL}
    results = _adamw_small(gathered, _pack_small(like), _pack_small({k: m[k] for k in SMALL}),
                           _pack_small({k: v[k] for k in SMALL}), "adamw_small")
    for tree, res in zip((out_grad, out_delta, out_m, out_v), results):
        tree.update(_unpack_small(res, like))
    loss = results[0].reshape(-1)[sum(math.prod(like[k].shape) for k in SMALL)]

    return (loss, grad_x[None], *[out_grad[k] for k in WEIGHTS], *[out_delta[k] for k in WEIGHTS],
            *[out_m[k] for k in WEIGHTS], *[out_v[k] for k in WEIGHTS])
```

```python
import math
import typing

import numpy as np
import jax
import jax.numpy as jnp
from jax import lax
from jax.experimental import pallas as pl
from jax.experimental.pallas import tpu as pltpu

F32 = jnp.float32
BF16 = jnp.bfloat16
MESH = pl.DeviceIdType.MESH

RMS_EPS = 1e-6
HEAD_DIM = 64
POOL_HALF_WINDOWS = (1, 2, 4, 8)
POOL_DIM = 256
GROUP_DIM = 256
DILATIONS = (1, 4, 16)
N_SIDE = 64
N_ATTN_HEADS = 12
ADAM_LR, ADAM_B1, ADAM_B2, ADAM_EPS, ADAM_WD, ADAM_STEP = 0.001, 0.9, 0.999, 1e-08, 0.01, 10

N_CHIPS = 4
V7X_VMEM_LIMIT = 60 * 1024 * 1024

_NT = (((1,), (1,)), ((), ()))
_TN = (((0,), (0,)), ((), ()))


def _dot(a, b):
    return jnp.dot(a, b, preferred_element_type=F32)


def _dot_nt(a, b):
    return lax.dot_general(a, b, _NT, preferred_element_type=F32)


def _dot_tn(a, b):
    return lax.dot_general(a, b, _TN, preferred_element_type=F32)


def _params(**kw):
    return pltpu.CompilerParams(vmem_limit_bytes=V7X_VMEM_LIMIT, **kw)


def _rows(tm, width):
    return pl.BlockSpec((tm, width), lambda i: (i, 0))


def _resident(shape):
    return pl.BlockSpec(shape, lambda i: (0,) * len(shape), pipeline_mode=pl.Buffered(1))


def _const(shape):
    return pl.BlockSpec(shape, lambda i: (0,) * len(shape))


def _inv_rms(x):
    return lax.rsqrt(jnp.mean(x * x, axis=-1, keepdims=True) + RMS_EPS)


def _rms_bwd(x, inv, g, dy):
    n = x * inv
    dn = dy * g
    dx = inv * (dn - n * jnp.mean(dn * n, axis=-1, keepdims=True))
    return dx, jnp.sum(dy * n, axis=0, keepdims=True)


def _accumulate(ref, value):
    @pl.when(pl.program_id(0) == 0)
    def _():
        ref[...] = jnp.zeros_like(ref)

    ref[...] += value


class _Rider(typing.NamedTuple):
    operands: list
    landing: typing.Optional[list]
    sems: tuple
    start: typing.Callable
    wait: typing.Callable


def _hosted_call(body, riders, *, name, steps, in_specs, out_specs, out_shape, args, scratch_shapes=()):
    params = _params(dimension_semantics=("arbitrary",))
    riders = list(riders or [])
    if not riders:
        res = pl.pallas_call(body, name=name, grid=(steps,), in_specs=in_specs, out_specs=out_specs, out_shape=out_shape,
                             scratch_shapes=list(scratch_shapes), compiler_params=params)(*args)
        return list(res), []
    n_in, n_out, n_scratch = len(in_specs), len(out_specs), len(scratch_shapes)
    operands, landing, aliases, spans = [], [], {}, []
    for rd in riders:
        lands = rd.landing if rd.landing is not None else [jax.ShapeDtypeStruct(a.shape, a.dtype) for a in rd.operands]
        if rd.landing is None:
            aliases.update({n_in + len(operands) + i: n_out + len(landing) + i for i in range(len(lands))})
        spans.append((len(operands), len(rd.operands), len(landing), len(lands)))
        operands += rd.operands
        landing += lands
    outs_at = n_in + len(operands)
    scratch_at = outs_at + n_out + len(landing)

    def riding(*refs):
        def each(action):
            for i, (rd, (in_at, n_ops, out_at, n_lands)) in enumerate(zip(riders, spans)):
                sems = refs[scratch_at + n_scratch + 2 * i:scratch_at + n_scratch + 2 * i + 2]
                getattr(rd, action)(refs[n_in + in_at:n_in + in_at + n_ops],
                                    refs[outs_at + n_out + out_at:outs_at + n_out + out_at + n_lands], *sems)

        @pl.when(pl.program_id(0) == 0)
        def _():
            each("start")

        body(*refs[:n_in], *refs[outs_at:outs_at + n_out], *refs[scratch_at:scratch_at + n_scratch])

        @pl.when(pl.program_id(0) == steps - 1)
        def _():
            each("wait")

    any_spec = pl.BlockSpec(memory_space=pl.ANY)
    res = pl.pallas_call(
        riding, name=name, grid=(steps,), in_specs=list(in_specs) + [any_spec] * len(operands),
        out_specs=list(out_specs) + [any_spec] * len(landing), out_shape=list(out_shape) + landing,
        scratch_shapes=list(scratch_shapes) + [pltpu.SemaphoreType.DMA(rd.sems) for rd in riders for _ in range(2)],
        input_output_aliases=aliases, compiler_params=params)(*args, *operands)
    return list(res[:n_out]), [list(res[n_out + out_at:n_out + out_at + n_lands]) for _, _, out_at, n_lands in spans]


_SUB_TILE = 256


def _sub_tiles(tm):
    return [pl.ds(r, _SUB_TILE) for r in range(0, tm, _SUB_TILE)]


def _ffn_fwd(x, g_pre, wg_t, wu_t, wd, g_post, target, name, riders=None, tm=512):
    s, d = x.shape
    ff = wd.shape[0]
    with_loss = target is not None

    def body(*refs):
        if with_loss:
            x_ref, gpre_ref, wg_ref, wu_ref, wd_ref, gpost_ref, t_ref, xo_ref, a_ref, b_ref, f_ref, loss_ref = refs
        else:
            x_ref, gpre_ref, wg_ref, wu_ref, wd_ref, gpost_ref, xo_ref, a_ref, b_ref, f_ref = refs
        loss = 0.0
        for rows in _sub_tiles(tm):
            xv = x_ref[rows, :]
            hb = (xv * _inv_rms(xv) * gpre_ref[...]).astype(BF16)
            a = _dot_nt(hb, wg_ref[...])
            b = _dot_nt(hb, wu_ref[...])
            hh = (a * jax.nn.sigmoid(a)) * b
            f = _dot(hh.astype(BF16), wd_ref[...])
            xo = xv + 0.5 * (f * _inv_rms(f) * gpost_ref[...])
            a_ref[rows, :] = a.astype(BF16)
            b_ref[rows, :] = b.astype(BF16)
            f_ref[rows, :] = f
            if with_loss:
                e = xo - t_ref[rows, :]
                xo_ref[rows, :] = e * (1.0 / d)
                loss = loss + 0.5 * jnp.sum(jnp.mean(e * e, axis=-1, keepdims=True))
            else:
                xo_ref[rows, :] = xo
        if with_loss:
            _accumulate(loss_ref, loss)

    in_specs = [_rows(tm, d), _const((1, d)), _resident((ff, d)), _resident((ff, d)), _resident((ff, d)), _const((1, d))]
    args = [x, g_pre, wg_t, wu_t, wd, g_post]
    out_shape = [jax.ShapeDtypeStruct((s, d), F32), jax.ShapeDtypeStruct((s, ff), BF16),
                 jax.ShapeDtypeStruct((s, ff), BF16), jax.ShapeDtypeStruct((s, d), F32)]
    out_specs = [_rows(tm, d), _rows(tm, ff), _rows(tm, ff), _rows(tm, d)]
    if with_loss:
        in_specs.append(_rows(tm, d))
        args.append(target)
        out_shape.append(jax.ShapeDtypeStruct((8, 128), F32))
        out_specs.append(_const((8, 128)))
    return _hosted_call(body, riders, name=name, steps=s // tm, in_specs=in_specs, out_specs=out_specs, out_shape=out_shape, args=args)


def _ffn_bwd(dxo, x, f, a, b, g_pre, g_post, wg_t, wu_t, wd, name, riders=None, tm=256):
    s, d = x.shape
    ff = wd.shape[0]

    def body(dxo_ref, x_ref, f_ref, a_ref, b_ref, gpre_ref, gpost_ref, wg_ref, wu_ref, wd_ref,
             dx_ref, hh_ref, da_ref, db_ref, df_ref, h_ref, dgpre_ref, dgpost_ref):
        dgpre_sum = dgpost_sum = 0.0
        for rows in _sub_tiles(tm):
            dxo_v = dxo_ref[rows, :]
            fv = f_ref[rows, :]
            df, dgpost = _rms_bwd(fv, _inv_rms(fv), gpost_ref[...], 0.5 * dxo_v)
            dfb = df.astype(BF16)
            dhh = _dot_nt(dfb, wd_ref[...])
            av = a_ref[rows, :].astype(F32)
            bv = b_ref[rows, :].astype(F32)
            sig = jax.nn.sigmoid(av)
            sa = av * sig
            da = (dhh * bv * (sig * (1.0 + av * (1.0 - sig)))).astype(BF16)
            db = (dhh * sa).astype(BF16)
            dh = _dot(da, wg_ref[...]) + _dot(db, wu_ref[...])
            xv = x_ref[rows, :]
            inv = _inv_rms(xv)
            dxn, dgpre = _rms_bwd(xv, inv, gpre_ref[...], dh)
            dx_ref[rows, :] = dxo_v + dxn
            hh_ref[rows, :] = (sa * bv).astype(BF16)
            da_ref[rows, :] = da
            db_ref[rows, :] = db
            df_ref[rows, :] = dfb
            h_ref[rows, :] = (xv * inv * gpre_ref[...]).astype(BF16)
            dgpre_sum, dgpost_sum = dgpre_sum + dgpre, dgpost_sum + dgpost
        _accumulate(dgpre_ref, dgpre_sum)
        _accumulate(dgpost_ref, dgpost_sum)

    return _hosted_call(
        body, riders, name=name, steps=s // tm,
        in_specs=[_rows(tm, d), _rows(tm, d), _rows(tm, d), _rows(tm, ff), _rows(tm, ff), _const((1, d)), _const((1, d)),
                  _resident((ff, d)), _resident((ff, d)), _resident((ff, d))],
        out_specs=[_rows(tm, d), _rows(tm, ff), _rows(tm, ff), _rows(tm, ff), _rows(tm, d), _rows(tm, d),
                   _const((1, d)), _const((1, d))],
        out_shape=[jax.ShapeDtypeStruct((s, d), F32), jax.ShapeDtypeStruct((s, ff), BF16), jax.ShapeDtypeStruct((s, ff), BF16),
                   jax.ShapeDtypeStruct((s, ff), BF16), jax.ShapeDtypeStruct((s, d), BF16), jax.ShapeDtypeStruct((s, d), BF16),
                   jax.ShapeDtypeStruct((1, d), F32), jax.ShapeDtypeStruct((1, d), F32)],
        args=[dxo, x, f, a, b, g_pre, g_post, wg_t, wu_t, wd])


def _wgrad(lhs, rhs, name, riders=None, rt=256):
    s, r = lhs.shape
    c = rhs.shape[1]

    def body(l_ref, r_ref, o_ref):
        o_ref[...] = _dot_tn(l_ref[...], r_ref[...])

    (out,), riding = _hosted_call(
        body, riders, name=name, steps=pl.cdiv(r, rt), in_specs=[pl.BlockSpec((s, rt), lambda i: (0, i)), _resident((s, c))],
        out_specs=[pl.BlockSpec((rt, c), lambda i: (i, 0))], out_shape=[jax.ShapeDtypeStruct((r, c), F32)], args=[lhs, rhs])
    return out, riding


def _attn_dtype(dilation):
    return BF16 if dilation == 1 else F32


def _in_fwd(x, g, w_in_t, name, riders=None, tm=1024):
    s, d = x.shape
    d_in = w_in_t.shape[0]
    n_groups = len(DILATIONS)
    dtypes = [_attn_dtype(dil) for dil in DILATIONS] * 3

    def body(x_ref, g_ref, w_ref, h_ref, u_ref, *part_refs):
        xv = x_ref[...]
        hb = (xv * _inv_rms(xv) * g_ref[...]).astype(BF16)
        h_ref[...] = hb
        z = _dot_nt(hb, w_ref[...])
        u_ref[...] = z[:, :POOL_DIM]
        for j, ref in enumerate(part_refs):
            part = z[:, POOL_DIM + GROUP_DIM * j:POOL_DIM + GROUP_DIM * (j + 1)]
            ref[...] = (part * _SCORE_SCALE if j < n_groups else part).astype(ref.dtype)

    return _hosted_call(
        body, riders, name=name, steps=s // tm, in_specs=[_rows(tm, d), _const((1, d)), _resident((d_in, d))],
        out_specs=[_rows(tm, d), _rows(tm, POOL_DIM)] + [_rows(tm, GROUP_DIM)] * len(dtypes),
        out_shape=[jax.ShapeDtypeStruct((s, d), BF16), jax.ShapeDtypeStruct((s, POOL_DIM), F32)]
        + [jax.ShapeDtypeStruct((s, GROUP_DIM), dt) for dt in dtypes],
        args=[x, g, w_in_t])


def _in_bwd(du, dparts, x, dxo, g, w_in_t, name, riders=None, tm=512):
    s, d = x.shape
    d_in = w_in_t.shape[0]
    n_parts = len(dparts)

    def body(du_ref, *refs):
        part_refs = refs[:n_parts]
        x_ref, dxo_ref, g_ref, w_ref, dx_ref, dz_ref, dg_ref = refs[n_parts:]
        dz = jnp.concatenate([r[...].astype(BF16) for r in (du_ref,) + part_refs], axis=1)
        dz_ref[...] = dz
        dh = _dot(dz, w_ref[...])
        xv = x_ref[...]
        dxn, dg = _rms_bwd(xv, _inv_rms(xv), g_ref[...], dh)
        dx_ref[...] = dxo_ref[...] + dxn
        _accumulate(dg_ref, dg)

    return _hosted_call(
        body, riders, name=name, steps=s // tm,
        in_specs=[_rows(tm, POOL_DIM)] + [_rows(tm, GROUP_DIM)] * n_parts + [_rows(tm, d), _rows(tm, d), _const((1, d)),
                                                                             _resident((d_in, d))],
        out_specs=[_rows(tm, d), _rows(tm, d_in), _const((1, d))],
        out_shape=[jax.ShapeDtypeStruct((s, d), F32), jax.ShapeDtypeStruct((s, d_in), BF16), jax.ShapeDtypeStruct((1, d), F32)],
        args=[du, *dparts, x, dxo, g, w_in_t])


_POOL_HALO = 8


def _pool_chain(v, first_shift):
    n = v.shape[0]
    p2 = v + pltpu.roll(v, first_shift, 0)
    p4 = pltpu.roll(p2, 1, 0) + pltpu.roll(p2, n - 1, 0)
    p8 = pltpu.roll(p4, 2, 0) + pltpu.roll(p4, n - 2, 0)
    p16 = pltpu.roll(p8, 4, 0) + pltpu.roll(p8, n - 4, 0)
    group = lax.broadcasted_iota(jnp.int32, v.shape, 1) // HEAD_DIM
    return jnp.where(group == 0, p2, jnp.where(group == 1, p4, jnp.where(group == 2, p8, p16)))


def _pool_count(t0, rows, s):
    t = t0 + lax.broadcasted_iota(jnp.int32, (rows, POOL_DIM), 0)
    group = lax.broadcasted_iota(jnp.int32, (rows, POOL_DIM), 1) // HEAD_DIM
    half = jnp.where(group == 0, 1, jnp.where(group == 1, 2, jnp.where(group == 2, 4, 8)))
    cnt = jnp.minimum(t + half, s) - jnp.maximum(t - half, 0)
    return jnp.maximum(cnt, 1).astype(F32)


def _pad_rows(ref, pad_ref, s):
    zeros = jnp.zeros((_POOL_HALO, pad_ref.shape[1]), pad_ref.dtype)
    pad_ref[pl.ds(0, _POOL_HALO), :] = zeros
    pad_ref[pl.ds(_POOL_HALO + s, _POOL_HALO), :] = zeros
    pad_ref[pl.ds(_POOL_HALO, s), :] = ref[...]


def _pool_fwd(u, w_bd, scale, name, tm=512):
    s = u.shape[0]
    ext = tm + 2 * _POOL_HALO

    def body(u_ref, w_ref, sc_ref, o_ref, upad):
        _pad_rows(u_ref, upad, s)

        def tile(i, carry):
            t0 = pl.multiple_of(i * tm, tm)
            uv = upad[pl.ds(t0, ext), :]
            win = _pool_chain(uv, 1)[_POOL_HALO:_POOL_HALO + tm]
            y = win / _pool_count(t0, tm, s) - uv[_POOL_HALO:_POOL_HALO + tm]
            o_ref[pl.ds(t0, tm), :] = (_dot(y.astype(BF16), w_ref[...]) * sc_ref[...]).astype(BF16)
            return carry

        lax.fori_loop(0, s // tm, tile, 0)

    return pl.pallas_call(body, name=name, out_shape=jax.ShapeDtypeStruct((s, POOL_DIM), BF16),
                          scratch_shapes=[pltpu.VMEM((s + 2 * _POOL_HALO, POOL_DIM), F32)],
                          compiler_params=_params())(u, w_bd, scale)


def _pool_bwd(u, da, w_bd, scale, name, tm=512):
    s = u.shape[0]
    ext = tm + 2 * _POOL_HALO

    def body(u_ref, da_ref, w_ref, sc_ref, du_ref, dw_ref, dsc_ref, upad, dapad):
        _pad_rows(u_ref, upad, s)
        _pad_rows(da_ref, dapad, s)
        dw_ref[...] = jnp.zeros_like(dw_ref)
        dsc_ref[...] = jnp.zeros_like(dsc_ref)

        def tile(i, carry):
            t0 = pl.multiple_of(i * tm, tm)
            uv = upad[pl.ds(t0, ext), :]
            dav = dapad[pl.ds(t0, ext), :]
            win = _pool_chain(uv, 1)[_POOL_HALO:_POOL_HALO + tm]
            yb = (win / _pool_count(t0, tm, s) - uv[_POOL_HALO:_POOL_HALO + tm]).astype(BF16)
            yl = _dot(yb, w_ref[...])
            da_c = dav[_POOL_HALO:_POOL_HALO + tm]
            dsc_ref[...] += jnp.sum(da_c * yl, axis=0, keepdims=True)
            dyl = (dav * sc_ref[...]).astype(BF16)
            dw_ref[...] += _dot_tn(yb, dyl[_POOL_HALO:_POOL_HALO + tm])
            dy = _dot_nt(dyl, w_ref[...])
            dyc = dy / _pool_count(t0 - _POOL_HALO, ext, s)
            du_ref[pl.ds(t0, tm), :] = (_pool_chain(dyc, ext - 1) - dy)[_POOL_HALO:_POOL_HALO + tm]
            return carry

        lax.fori_loop(0, s // tm, tile, 0)

    pool_cols = pl.BlockSpec((s, POOL_DIM), lambda i: (0, 0), pipeline_mode=pl.Buffered(1))
    return pl.pallas_call(
        body, name=name, grid=(1,),
        in_specs=[pool_cols, pool_cols, _const((POOL_DIM, POOL_DIM)), _const((1, POOL_DIM))],
        out_specs=[_const((s, POOL_DIM)), _const((POOL_DIM, POOL_DIM)), _const((1, POOL_DIM))],
        out_shape=[jax.ShapeDtypeStruct((s, POOL_DIM), F32), jax.ShapeDtypeStruct((POOL_DIM, POOL_DIM), F32),
                   jax.ShapeDtypeStruct((1, POOL_DIM), F32)],
        scratch_shapes=[pltpu.VMEM((s + 2 * _POOL_HALO, POOL_DIM), F32), pltpu.VMEM((s + 2 * _POOL_HALO, POOL_DIM), F32)],
        compiler_params=_params(dimension_semantics=("arbitrary",)))(u, da, w_bd, scale)


_BQ = 128
_KW = _BQ + 2 * N_SIDE
_PAIR = 2 * HEAD_DIM
_NEG = -1e30
_ATTN_UNROLL = 8
_SCORE_SCALE = HEAD_DIM ** -0.5


def _stack_heads(x):
    lane_head = lax.broadcasted_iota(jnp.int32, x.shape, 1) // HEAD_DIM
    zero = jnp.zeros_like(x)
    return jnp.concatenate([jnp.where(lane_head == 0, x, zero), jnp.where(lane_head == 1, x, zero)], axis=0)


def _unstack_heads(x):
    lane_head = lax.broadcasted_iota(jnp.int32, (_BQ, _PAIR), 1) // HEAD_DIM
    return jnp.where(lane_head == 0, x[:_BQ], x[_BQ:])


def _stack_cols(x):
    return jnp.concatenate([x[:, 0:1], x[:, HEAD_DIM:HEAD_DIM + 1]], axis=0)


def _fill_bias(bias_ref, slopes_ref, dilation):
    row = lax.broadcasted_iota(jnp.int32, (2 * _BQ, _KW), 0)
    col = lax.broadcasted_iota(jnp.int32, (2 * _BQ, _KW), 1)
    pair = 2 * pl.program_id(0)
    slope = jnp.where(row < _BQ, slopes_ref[pair], slopes_ref[pair + 1]) * float(dilation)

    @pl.when(pl.program_id(1) == 0)
    def _():
        for j in range(3):
            dist = jnp.abs(col - (row & (_BQ - 1)) - j * N_SIDE)
            bias_ref[j] = jnp.where(dist <= N_SIDE, -slope * dist.astype(F32), _NEG)


def _block_window(i, n_blocks, length):
    q0 = pl.multiple_of(i * _BQ, _BQ)
    ws = pl.multiple_of(jnp.clip(q0 - N_SIDE, 0, length - _KW), N_SIDE)
    return q0, ws, jnp.where(i == 0, 0, jnp.where(i == n_blocks - 1, 2, 1))


_FREE_STRIDE = 4


def _residue_views(dilation, seq, ins, outs, tmps):
    step = pl.program_id(1)
    if dilation <= _FREE_STRIDE:
        def rows(start, count):
            return pl.ds(start, count) if dilation == 1 else pl.ds(start * dilation + step, count, stride=dilation)

        return ins, outs, rows, lambda: None
    inner = dilation // _FREE_STRIDE
    assert inner <= _FREE_STRIDE and len(tmps) == len(ins) + len(outs)
    first, second = step // inner, step % inner
    coarse = pl.ds(first, seq // _FREE_STRIDE, stride=_FREE_STRIDE)
    in_tmps, out_tmps = tmps[:len(ins)], tmps[len(ins):]

    @pl.when(second == 0)
    def _():
        for ref, tmp in zip(ins, in_tmps):
            tmp[...] = ref[coarse, :]

    def flush():
        @pl.when(second == inner - 1)
        def _():
            for ref, tmp in zip(outs, out_tmps):
                ref[coarse, :] = tmp[...]

    return in_tmps, out_tmps, lambda start, count: pl.ds(start * inner + second, count, stride=inner), flush


def _attn_call(body, name, dilation, seq, n_in, out_dtypes, scratch, buffers):
    col = pl.BlockSpec((seq, _PAIR), lambda c, r: (0, c), pipeline_mode=pl.Buffered(buffers))
    tmps = [pltpu.VMEM((seq // _FREE_STRIDE, _PAIR), F32)] * (n_in + len(out_dtypes) if dilation > _FREE_STRIDE else 0)
    return pl.pallas_call(
        body, name=name, grid=(GROUP_DIM // _PAIR, dilation),
        in_specs=[pl.BlockSpec(memory_space=pltpu.SMEM)] + [col] * n_in, out_specs=[col] * len(out_dtypes),
        out_shape=[jax.ShapeDtypeStruct((seq, GROUP_DIM), dt) for dt in out_dtypes], scratch_shapes=scratch + tmps,
        compiler_params=_params(dimension_semantics=("arbitrary", "arbitrary")))


def _staged(dilation, length, rows, sources, scratch):
    if dilation == 1:
        return sources
    for src, dst in zip(sources, scratch):
        dst[...] = src[rows(0, length), :].astype(BF16)
    return scratch


def _attn_fwd(q, k, v, slopes, dilation, name):
    seq = q.shape[0]
    length = seq // dilation
    n_blocks = length // _BQ
    n_stage = 0 if dilation == 1 else 3

    def body(sl_ref, q_ref, k_ref, v_ref, o_ref, lse_ref, *scratch):
        bias_ref, tmps = scratch[n_stage], scratch[n_stage + 1:]
        (q_in, k_in, v_in), (o_out, lse_out), rows, flush = _residue_views(dilation, seq, (q_ref, k_ref, v_ref), (o_ref, lse_ref), tmps)
        qs, ks, vs = _staged(dilation, length, rows, (q_in, k_in, v_in), scratch[:n_stage])
        _fill_bias(bias_ref, sl_ref, dilation)

        def block(i, carry):
            q0, ws, which = _block_window(i, n_blocks, length)
            kw = ks[pl.ds(ws, _KW), :]
            vw = vs[pl.ds(ws, _KW), :]
            sc = _dot_nt(_stack_heads(qs[pl.ds(q0, _BQ), :]), kw) + bias_ref[which]
            m = jnp.max(sc, axis=-1, keepdims=True)
            p = jnp.exp(sc - m)
            den = jnp.sum(p, axis=-1, keepdims=True)
            o_out[rows(q0, _BQ), :] = _unstack_heads(_dot(p.astype(BF16), vw) / den)
            lse_out[rows(q0, _BQ), :] = _unstack_heads(jnp.broadcast_to(m + jnp.log(den), (2 * _BQ, _PAIR)))
            return carry

        lax.fori_loop(0, n_blocks, block, 0, unroll=min(_ATTN_UNROLL, n_blocks))
        flush()

    stage = pltpu.VMEM((length, _PAIR), BF16)
    bias = pltpu.VMEM((3, 2 * _BQ, _KW), F32)
    return _attn_call(body, name, dilation, seq, 3, [F32, F32], [stage] * n_stage + [bias], 2)(slopes, q, k, v)


def _attn_bwd(q, k, v, do, lse, cterm, slopes, dilation, name):
    seq = q.shape[0]
    length = seq // dilation
    n_blocks = length // _BQ
    n_stage = 0 if dilation == 1 else 4

    def body(sl_ref, q_ref, k_ref, v_ref, do_ref, lse_ref, c_ref, dq_ref, dk_ref, dv_ref, *scratch):
        (dk_acc, dv_acc, bias_ref), tmps = scratch[n_stage:n_stage + 3], scratch[n_stage + 3:]
        (q_in, k_in, v_in, do_in, lse_in, c_in), (dq_out, dk_out, dv_out), rows, flush = _residue_views(
            dilation, seq, (q_ref, k_ref, v_ref, do_ref, lse_ref, c_ref), (dq_ref, dk_ref, dv_ref), tmps)
        all_rows = rows(0, length)
        qs, ks, vs, dos = _staged(dilation, length, rows, (q_in, k_in, v_in, do_in), scratch[:n_stage])
        dk_acc[...] = jnp.zeros_like(dk_acc)
        dv_acc[...] = jnp.zeros_like(dv_acc)
        _fill_bias(bias_ref, sl_ref, dilation)

        def block(i, carry):
            q0, ws, which = _block_window(i, n_blocks, length)
            qm = _stack_heads(qs[pl.ds(q0, _BQ), :])
            dom = _stack_heads(dos[pl.ds(q0, _BQ), :])
            kw = ks[pl.ds(ws, _KW), :]
            vw = vs[pl.ds(ws, _KW), :]
            p = jnp.exp(_dot_nt(qm, kw) + bias_ref[which] - _stack_cols(lse_in[rows(q0, _BQ), :]))
            ds = (p * (_dot_nt(dom, vw) + _stack_cols(c_in[rows(q0, _BQ), :]))).astype(BF16)
            dq_out[rows(q0, _BQ), :] = (_unstack_heads(_dot(ds, kw)) * _SCORE_SCALE).astype(dq_out.dtype)
            dk_acc[pl.ds(ws, _KW), :] += _dot_tn(ds, qm)
            dv_acc[pl.ds(ws, _KW), :] += _dot_tn(p.astype(BF16), dom)
            return carry

        lax.fori_loop(0, n_blocks, block, 0, unroll=min(_ATTN_UNROLL, n_blocks))
        dk_out[all_rows, :] = dk_acc[...].astype(dk_out.dtype)
        dv_out[all_rows, :] = dv_acc[...].astype(dv_out.dtype)
        flush()

    stage = pltpu.VMEM((length, _PAIR), BF16)
    acc = pltpu.VMEM((length, _PAIR), F32)
    bias = pltpu.VMEM((3, 2 * _BQ, _KW), F32)
    return _attn_call(body, name, dilation, seq, 6, [_attn_dtype(dilation)] * 3, [stage] * n_stage + [acc] * 2 + [bias],
                      2 if dilation == 1 else 1)(slopes, q, k, v, do, lse, cterm)


def _group_weights(lses):
    m = jnp.maximum(jnp.maximum(lses[0], lses[1]), lses[2])
    es = [jnp.exp(l - m) for l in lses]
    den = es[0] + es[1] + es[2]
    return [e / den for e in es]


def _out_fwd(a_pool, outs, lses, x, w_out, g, name, tm=1024):
    s, d = x.shape
    width = POOL_DIM + 3 * GROUP_DIM

    def body(ap_ref, o0, o1, o2, l0, l1, l2, x_ref, w_ref, g_ref, xo_ref, cat_ref):
        alphas = _group_weights([l0[...], l1[...], l2[...]])
        cat = jnp.concatenate([ap_ref[...]] + [(o[...] * al).astype(BF16) for o, al in zip((o0, o1, o2), alphas)], axis=1)
        cat_ref[...] = cat
        mix = _dot(cat, w_ref[...])
        xo_ref[...] = x_ref[...] + mix * _inv_rms(mix) * g_ref[...]

    return pl.pallas_call(
        body, name=name, grid=(s // tm,),
        in_specs=[_rows(tm, POOL_DIM)] + [_rows(tm, GROUP_DIM)] * 6 + [_rows(tm, d), _resident(w_out.shape), _const((1, d))],
        out_specs=[_rows(tm, d), _rows(tm, width)],
        out_shape=[jax.ShapeDtypeStruct((s, d), F32), jax.ShapeDtypeStruct((s, width), BF16)],
        compiler_params=_params(dimension_semantics=("arbitrary",)))(a_pool, *outs, *lses, x, w_out, g)


def _out_bwd(dxo, cat, outs, lses, w_out, g, head_ones, name, tm=1024):
    s, d = dxo.shape

    def body(dxo_ref, cat_ref, o0, o1, o2, l0, l1, l2, w_ref, g_ref, ones_ref, dpool_ref, dmix_ref, do0, do1, do2, c0, c1, c2, dg_ref):
        mv = _dot(cat_ref[...], w_ref[...])
        dmix, dg = _rms_bwd(mv, _inv_rms(mv), g_ref[...], dxo_ref[...])
        dmb = dmix.astype(BF16)
        dmix_ref[...] = dmb
        _accumulate(dg_ref, dg)
        dcat = _dot_nt(dmb, w_ref[...])
        dpool_ref[...] = dcat[:, :POOL_DIM]
        alphas = _group_weights([l0[...], l1[...], l2[...]])
        das = [dcat[:, POOL_DIM + GROUP_DIM * j:POOL_DIM + GROUP_DIM * (j + 1)] for j in range(3)]
        prod = sum(da * (o[...] * al) for da, o, al in zip(das, (o0, o1, o2), alphas))
        hi = prod.astype(BF16)
        lo = (prod - hi.astype(F32)).astype(BF16)
        total = _dot(hi, ones_ref[...]) + _dot(lo, ones_ref[...])
        for da, al, do_ref, c_ref in zip(das, alphas, (do0, do1, do2), (c0, c1, c2)):
            do_ref[...] = (da * al).astype(do_ref.dtype)
            c_ref[...] = -al * total

    return pl.pallas_call(
        body, name=name, grid=(s // tm,),
        in_specs=[_rows(tm, d), _rows(tm, cat.shape[1])] + [_rows(tm, GROUP_DIM)] * 6 + [_resident(w_out.shape), _const((1, d)),
                                                                                        _const((GROUP_DIM, GROUP_DIM))],
        out_specs=[_rows(tm, POOL_DIM), _rows(tm, d)] + [_rows(tm, GROUP_DIM)] * 6 + [_const((1, d))],
        out_shape=[jax.ShapeDtypeStruct((s, POOL_DIM), F32), jax.ShapeDtypeStruct((s, d), BF16)]
        + [jax.ShapeDtypeStruct((s, GROUP_DIM), _attn_dtype(dil)) for dil in DILATIONS]
        + [jax.ShapeDtypeStruct((s, GROUP_DIM), F32)] * 3 + [jax.ShapeDtypeStruct((1, d), F32)],
        compiler_params=_params(dimension_semantics=("arbitrary",)))(dxo, cat, *outs, *lses, w_out, g, head_ones)


def _alibi_slopes():
    return np.array([2.0 ** (-8.0 * (i + 1) / N_ATTN_HEADS) for i in range(N_ATTN_HEADS)], np.float32)


def _block_diag(w_lin):
    n, c, _ = w_lin.shape
    eye = jnp.eye(n, dtype=w_lin.dtype)
    return (eye[:, None, :, None] * w_lin[:, :, None, :]).reshape(n * c, n * c)


class _NoExchange:
    def __init__(self, full):
        self.full, self.grads = full, {}

    def first_weights(self):
        return self.full

    def riders(self, host):
        return []

    def landed(self, host, results):
        return self.full

    def gradient(self, name, grad):
        self.grads[name] = grad


def _local_step(x, target, small, exchange):
    s, d = x.shape
    slopes = _alibi_slopes()
    group_slopes = [jnp.asarray(slopes[4 * g:4 * g + 4]) for g in range(3)]
    w_bd = _block_diag(small["w_pool_lin"]).astype(BF16)
    head_ones = jnp.asarray(np.kron(np.eye(GROUP_DIM // HEAD_DIM), np.ones((HEAD_DIM, HEAD_DIM))), BF16)

    full = dict(exchange.first_weights())

    def hosted(call, host, *args):
        results, riding = call(*args, host, exchange.riders(host))
        full.update(exchange.landed(host, riding) or {})
        return results

    x1, a1, b1, f1 = hosted(_ffn_fwd, "ffn1_fwd", x, small["g_ffn1_pre"], full["w1_gate"], full["w1_up"], full["w1_down"],
                            small["g_ffn1_post"], None)
    h2, u, *parts = hosted(_in_fwd, "in_fwd", x1, small["g_mix_pre"], full["w_in"])
    qs, ks, vs = parts[0:3], parts[3:6], parts[6:9]
    a_pool = _pool_fwd(u, w_bd, small["pool_scale"], "pool_fwd")
    outs, lses = [], []
    for g, dil in enumerate(DILATIONS):
        o, lse = _attn_fwd(qs[g], ks[g], vs[g], group_slopes[g], dil, f"attn_fwd{g}")
        outs.append(o)
        lses.append(lse)
    x2, cat = _out_fwd(a_pool, outs, lses, x1, full["w_out"], small["g_mix_post"], "out_fwd")
    (dx3, a2, b2, f2, loss_part), _ = _ffn_fwd(x2, small["g_ffn2_pre"], full["w2_gate"], full["w2_up"], full["w2_down"],
                                               small["g_ffn2_post"], target, "ffn2_fwd")

    small_grads = {}

    def ffn_backward(tag, dxo, x_in, f, a, b):
        n = tag[-1]
        dx, hh, da, db, df, h, dg_pre, dg_post = hosted(
            _ffn_bwd, f"{tag}_bwd", dxo, x_in, f, a, b, small[f"g_{tag}_pre"], small[f"g_{tag}_post"],
            full[f"w{n}_gate"], full[f"w{n}_up"], full[f"w{n}_down"])
        for part, lhs, rhs in (("down", hh, df), ("gate", da, h), ("up", db, h)):
            exchange.gradient(f"w{n}_{part}", hosted(_wgrad, f"{tag}_wgrad_{part}", lhs, rhs))
        small_grads[f"g_{tag}_pre"], small_grads[f"g_{tag}_post"] = dg_pre, dg_post
        return dx

    dx2 = ffn_backward("ffn2", dx3, x2, f2, a2, b2)
    dpool, dmix, *dos_cs, small_grads["g_mix_post"] = _out_bwd(dx2, cat, outs, lses, full["w_out"], small["g_mix_post"],
                                                               head_ones, "out_bwd")
    dos, cs = dos_cs[:3], dos_cs[3:]
    dqs, dks, dvs = [], [], []
    for g, dil in enumerate(DILATIONS):
        dq, dk, dv = _attn_bwd(qs[g], ks[g], vs[g], dos[g], lses[g], cs[g], group_slopes[g], dil, f"attn_bwd{g}")
        dqs.append(dq)
        dks.append(dk)
        dvs.append(dv)
    du, dw_bd, small_grads["pool_scale"] = _pool_bwd(u, dpool, w_bd, small["pool_scale"], "pool_bwd")
    n_pool = len(POOL_HALF_WINDOWS)
    small_grads["w_pool_lin"] = jnp.stack(
        [dw_bd[HEAD_DIM * g:HEAD_DIM * (g + 1), HEAD_DIM * g:HEAD_DIM * (g + 1)] for g in range(n_pool)])
    dx1, dz, small_grads["g_mix_pre"] = hosted(_in_bwd, "in_bwd", du, dqs + dks + dvs, x1, dx2, small["g_mix_pre"], full["w_in"])
    exchange.gradient("w_in", hosted(_wgrad, "wgrad_in", dz, h2))
    dx0 = ffn_backward("ffn1", dx1, x, f1, a1, b1)
    exchange.gradient("w_out", hosted(_wgrad, "wgrad_out", cat, dmix))
    return loss_part[0, 0], dx0, small_grads


SEGMENTS = ("w1_gate", "w1_up", "w1_down", "w_in", "w_out", "w2_gate", "w2_up", "w2_down")
TRANSPOSED = ("w1_gate", "w1_up", "w_in", "w2_gate", "w2_up")
ROWS_OUTSIDE = ("w1_gate", "w1_up", "w2_gate", "w2_up")
HALF = 512


def _place():
    x, y, c = lax.axis_index("x"), lax.axis_index("y"), lax.axis_index("c")
    other_chips = [(1 - x, y), (x, 1 - y), (1 - x, 1 - y)]
    return x, y, c, other_chips


def _chip_rows(chip, rows):
    return pl.ds(pl.multiple_of((2 * chip[0] + chip[1]) * rows, 16), rows)


def _cols(c):
    return pl.ds(pl.multiple_of(c * HALF, HALF), HALF)


def _cast_shards(shards, transposed, place, name):
    n = len(shards)
    rows = [w.shape[1] if t else w.shape[0] for w, t in zip(shards, transposed)]

    def body(place_ref, *refs):
        for w_ref, o_ref, t in zip(refs[:n], refs[n:], transposed):
            o_ref[...] = (w_ref[...].T if t else w_ref[...]).astype(BF16)

    once = pl.Buffered(1)
    return pl.pallas_call(
        body, name=name,
        grid_spec=pltpu.PrefetchScalarGridSpec(
            num_scalar_prefetch=1, grid=(1,),
            in_specs=[pl.BlockSpec(w.shape, lambda i, place: (0, 0), pipeline_mode=once) for w in shards],
            out_specs=[pl.BlockSpec((r, 1024), lambda i, place: (place[0], 0), pipeline_mode=once) for r in rows]),
        out_shape=[jax.ShapeDtypeStruct((N_CHIPS * r, 1024), BF16) for r in rows],
        compiler_params=_params(dimension_semantics=("arbitrary",)))(place, *shards)


def _gather_weights(bufs):
    n = len(bufs)
    rows = [b.shape[0] // N_CHIPS for b in bufs]

    def halves(r):
        first = -(-r // 32) * 16
        return (0, first), (first, r - first)

    def body(*refs):
        outs = refs[n:2 * n]
        ici_send, ici_recv, d2d_send, d2d_recv = refs[2 * n:]
        x, y, c, _ = _place()
        me, via_x, via_y, diagonal = (x, y), (1 - x, y), (x, 1 - y), (1 - x, 1 - y)

        def piece(chip, k, h, cols):
            start, size = halves(rows[k])[h]
            return outs[k].at[pl.ds(pl.multiple_of((2 * chip[0] + chip[1]) * rows[k] + start, 16), size), _cols(cols)]

        def ici(path, chip, k, h, to):
            blk = piece(chip, k, h, c)
            return pltpu.make_async_remote_copy(src_ref=blk, dst_ref=blk, send_sem=ici_send.at[path, k, h],
                                                recv_sem=ici_recv.at[path, k, h], device_id=(*to, c), device_id_type=MESH)

        def d2d(slot, chip, k, h, cols):
            blk = piece(chip, k, h, cols)
            return pltpu.make_async_remote_copy(src_ref=blk, dst_ref=blk, send_sem=d2d_send.at[slot, k, h],
                                                recv_sem=d2d_recv.at[slot, k, h], device_id=(x, y, 1 - c), device_id_type=MESH)

        started = [ici(0, me, k, h, via_x) for h in (0, 1) for k in range(n)] + [ici(1, me, k, h, via_y) for h in (1, 0) for k in range(n)]
        for cp in started:
            cp.start()

        def landed(path, slot, chip, k, h, pass_on_to=None):
            ici(path, chip, k, h, me).wait_recv()
            more = [d2d(slot, chip, k, h, c)] + ([ici(2, chip, k, h, pass_on_to)] if pass_on_to else [])
            for cp in more:
                cp.start()
            started.extend(more)

        for k in range(n):
            landed(0, 0, via_x, k, 0, pass_on_to=via_y)
            landed(1, 1, via_y, k, 1, pass_on_to=via_x)
        for k in range(n):
            landed(0, 0, via_x, k, 1)
            landed(1, 1, via_y, k, 0)
        for k in range(n):
            for h in range(2):
                landed(2, 2, diagonal, k, h)
        for slot, chip in enumerate((via_x, via_y, diagonal)):
            for k in range(n):
                for h in range(2):
                    d2d(slot, chip, k, h, 1 - c).wait_recv()
        for cp in started:
            cp.wait_send()

    any_spec = pl.BlockSpec(memory_space=pl.ANY)
    return pl.pallas_call(
        body, name="gather_weights", in_specs=[any_spec] * n, out_specs=[any_spec] * n,
        out_shape=[jax.ShapeDtypeStruct(b.shape, b.dtype) for b in bufs], input_output_aliases={k: k for k in range(n)},
        scratch_shapes=[pltpu.SemaphoreType.DMA((3, n, 2))] * 4)(*bufs)


def _gather_rider(bufs):
    n = len(bufs)
    rows = [b.shape[0] // N_CHIPS for b in bufs]

    def copies(outs, send_sems, recv_sems, inbound):
        x, y, c, chips = _place()
        for j, chip in enumerate(chips):
            for k in range(n):
                src_chip = chip if inbound else (x, y)
                blk = outs[k].at[_chip_rows(src_chip, rows[k]), _cols(c)]
                yield pltpu.make_async_remote_copy(src_ref=blk, dst_ref=blk, send_sem=send_sems.at[j, k], recv_sem=recv_sems.at[j, k],
                                                   device_id=(*chip, c), device_id_type=MESH)

    def start(ins, outs, send_sems, recv_sems):
        for cp in copies(outs, send_sems, recv_sems, False):
            cp.start()

    def wait(ins, outs, send_sems, recv_sems):
        for cp in copies(outs, send_sems, recv_sems, True):
            cp.wait_recv()
        for cp in copies(outs, send_sems, recv_sems, False):
            cp.wait_send()

    return _Rider(list(bufs), None, (3, n), start, wait)


def _forward_rider(bufs):
    n = len(bufs)
    rows = [b.shape[0] // N_CHIPS for b in bufs]

    def copies(outs, send_sems, recv_sems, half):
        x, y, c, chips = _place()
        for j, chip in enumerate(chips):
            for k in range(n):
                blk = outs[k].at[_chip_rows(chip, rows[k]), _cols(half(c))]
                yield pltpu.make_async_remote_copy(src_ref=blk, dst_ref=blk, send_sem=send_sems.at[j, k], recv_sem=recv_sems.at[j, k],
                                                   device_id=(x, y, 1 - c), device_id_type=MESH)

    def start(ins, outs, send_sems, recv_sems):
        for cp in copies(outs, send_sems, recv_sems, lambda c: c):
            cp.start()

    def wait(ins, outs, send_sems, recv_sems):
        for cp in copies(outs, send_sems, recv_sems, lambda c: 1 - c):
            cp.wait_recv()
        for cp in copies(outs, send_sems, recv_sems, lambda c: c):
            cp.wait_send()

    return _Rider(list(bufs), None, (3, n), start, wait)


def _sibling_rider(grads):
    n = len(grads)

    def copies(ins, outs, send_sems, recv_sems):
        x, y, c, _ = _place()
        return [pltpu.make_async_remote_copy(src_ref=ins[k].at[:, pl.ds(1 - c, 1)], dst_ref=outs[k], send_sem=send_sems.at[k],
                                             recv_sem=recv_sems.at[k], device_id=(x, y, 1 - c), device_id_type=MESH)
                for k in range(n)]

    def start(*refs):
        for cp in copies(*refs):
            cp.start()

    def wait(*refs):
        for cp in copies(*refs):
            cp.wait()

    return _Rider(list(grads), [jax.ShapeDtypeStruct((N_CHIPS, 1) + g.shape[2:], F32) for g in grads], (n,), start, wait)


def _alone(rider, name):
    n = len(rider.operands)
    landing = rider.landing if rider.landing is not None else [jax.ShapeDtypeStruct(a.shape, a.dtype) for a in rider.operands]
    n_out = len(landing)

    def body(*refs):
        rider.start(refs[:n], refs[n:n + n_out], *refs[n + n_out:])
        rider.wait(refs[:n], refs[n:n + n_out], *refs[n + n_out:])

    any_spec = pl.BlockSpec(memory_space=pl.ANY)
    return pl.pallas_call(body, name=name, in_specs=[any_spec] * n, out_specs=[any_spec] * n_out, out_shape=landing,
                          input_output_aliases={i: i for i in range(n)} if rider.landing is None else {},
                          scratch_shapes=[pltpu.SemaphoreType.DMA(rider.sems)] * 2)(*rider.operands)


def _chip_sum(grad, from_sibling, place, name):
    rh, width = grad.shape[2:]

    def body(place_ref, g_ref, s_ref, own_ref, all_ref):
        all_ref[...] = (g_ref[...] + s_ref[...]).astype(BF16)
        mine = place_ref[0]
        own_ref[0] = g_ref[mine, 0] + s_ref[mine, 0]

    blk = (N_CHIPS, 1, rh, width)
    once = pl.Buffered(1)
    return pl.pallas_call(
        body, name=name,
        grid_spec=pltpu.PrefetchScalarGridSpec(
            num_scalar_prefetch=1, grid=(1,),
            in_specs=[pl.BlockSpec(blk, lambda i, place: (0, place[1], 0, 0), pipeline_mode=once),
                      pl.BlockSpec(blk, lambda i, place: (0, 0, 0, 0), pipeline_mode=once)],
            out_specs=[pl.BlockSpec((1, rh, width), lambda i, place: (0, 0, 0), pipeline_mode=once),
                       pl.BlockSpec(blk, lambda i, place: (0, 0, 0, 0), pipeline_mode=once)]),
        out_shape=[jax.ShapeDtypeStruct((1, rh, width), F32), jax.ShapeDtypeStruct((N_CHIPS, 1, rh, width), BF16)],
        compiler_params=_params(dimension_semantics=("arbitrary",)))(place, grad, from_sibling)


def _scatter_rider(sums):
    n = len(sums)

    def copies(ins, outs, send_sems, recv_sems):
        x, y, c, chips = _place()
        return [pltpu.make_async_remote_copy(src_ref=ins[k].at[pl.ds(2 * chip[0] + chip[1], 1)], dst_ref=outs[k].at[pl.ds(j, 1)],
                                             send_sem=send_sems.at[j, k], recv_sem=recv_sems.at[j, k],
                                             device_id=(*chip, c), device_id_type=MESH)
                for j, chip in enumerate(chips) for k in range(n)]

    def start(*refs):
        for cp in copies(*refs):
            cp.start()

    def wait(*refs):
        for cp in copies(*refs):
            cp.wait()

    return _Rider(list(sums), [jax.ShapeDtypeStruct((3,) + sm.shape[1:], BF16) for sm in sums], (3, n), start, wait)


def _total_sums(owns, received, name, riders=None):
    n = len(owns)

    def body(*refs):
        for o_ref, r_ref, t_ref in zip(refs[:n], refs[n:2 * n], refs[2 * n:]):
            total = o_ref[0]
            for j in range(3):
                total = total + r_ref[j, 0].astype(F32)
            t_ref[0] = total

    return _hosted_call(body, riders, name=name, steps=1, in_specs=[_resident(a.shape) for a in owns + received],
                        out_specs=[_resident(o.shape) for o in owns], out_shape=[jax.ShapeDtypeStruct(o.shape, F32) for o in owns],
                        args=owns + received)


def _swap_rider(halves):
    n = len(halves)

    def copies(ins, outs, send_sems, recv_sems):
        x, y, c, _ = _place()
        return [pltpu.make_async_remote_copy(src_ref=ins[k], dst_ref=outs[k], send_sem=send_sems.at[k], recv_sem=recv_sems.at[k],
                                             device_id=(x, y, 1 - c), device_id_type=MESH) for k in range(n)]

    def start(*refs):
        for cp in copies(*refs):
            cp.start()

    def wait(*refs):
        for cp in copies(*refs):
            cp.wait()

    return _Rider(list(halves), [jax.ShapeDtypeStruct(h.shape, F32) for h in halves], (n,), start, wait)


N_DEV = 8


def _gather_small(block):
    m_per, width = block.shape

    def body(x_ref, out_ref, send_sems, recv_sems, local_sem):
        x, y, c, chips = _place()
        me, sibling = (x, y, c), (x, y, 1 - c)

        def rows(px, py, pc):
            return out_ref.at[pl.ds((4 * px + 2 * py + pc) * m_per, m_per), :]

        def copy(k, blk, to, src=None):
            return pltpu.make_async_remote_copy(src_ref=rows(*blk) if src is None else src, dst_ref=rows(*blk),
                                                send_sem=send_sems.at[k], recv_sem=recv_sems.at[k], device_id=to, device_id_type=MESH)

        mine = pltpu.make_async_copy(x_ref, rows(*me), local_sem)
        mine.start()
        first = [copy(0, me, sibling, src=x_ref)] + [copy(1 + j, me, (*chip, c), src=x_ref) for j, chip in enumerate(chips)]
        for cp in first:
            cp.start()
        passed = [copy(4 + j, (*chip, c), sibling) for j, chip in enumerate(chips)]
        for j, chip in enumerate(chips):
            copy(1 + j, (*chip, c), me).wait_recv()
            passed[j].start()
        copy(0, sibling, me).wait_recv()
        for j, chip in enumerate(chips):
            copy(4 + j, (*chip, 1 - c), me).wait_recv()
        for cp in first + passed:
            cp.wait_send()
        mine.wait()

    vmem = pl.BlockSpec(memory_space=pltpu.VMEM)
    return pl.pallas_call(body, name="gather_small", out_shape=jax.ShapeDtypeStruct((N_DEV * m_per, width), F32),
                          in_specs=[vmem], out_specs=vmem,
                          scratch_shapes=[pltpu.SemaphoreType.DMA((7,)), pltpu.SemaphoreType.DMA((7,)),
                                          pltpu.SemaphoreType.DMA])(block)


def _adamw_math(w, g, m, v):
    m = ADAM_B1 * m + (1.0 - ADAM_B1) * g
    v = ADAM_B2 * v + (1.0 - ADAM_B2) * (g * g)
    m_hat = m / (1.0 - ADAM_B1 ** ADAM_STEP)
    v_hat = v / (1.0 - ADAM_B2 ** ADAM_STEP)
    delta = -ADAM_LR * (m_hat / (jnp.sqrt(v_hat) + ADAM_EPS) + ADAM_WD * w)
    return delta, m, v


def _adamw(w, mine, siblings, place, m, v, transposed, name, riders=None):
    rh, width = mine.shape[1:]
    place_spec = pl.BlockSpec(memory_space=pltpu.SMEM)
    halves = [_const((1, rh, width))] * 2
    out_shape = [jax.ShapeDtypeStruct(w.shape, F32)] * 4
    if transposed:
        def body(place_ref, w_ref, mine_ref, sib_ref, m_ref, v_ref, go_ref, d_ref, mo_ref, vo_ref):
            first = place_ref[1] == 0
            g = jnp.concatenate([jnp.where(first, mine_ref[0], sib_ref[0]), jnp.where(first, sib_ref[0], mine_ref[0])], axis=0).T
            go_ref[...] = g
            d_ref[...], mo_ref[...], vo_ref[...] = _adamw_math(w_ref[...], g, m_ref[...], v_ref[...])

        whole = _resident(w.shape)
        return _hosted_call(body, riders, name=name, steps=1, in_specs=[place_spec, whole] + halves + [whole, whole],
                            out_specs=[whole] * 4, out_shape=out_shape, args=[place, w, mine, siblings, m, v])

    def body(place_ref, w_ref, mine_ref, sib_ref, m_ref, v_ref, go_ref, d_ref, mo_ref, vo_ref):
        g = jnp.where(pl.program_id(0) == place_ref[1], mine_ref[0], sib_ref[0])
        go_ref[...] = g
        d_ref[...], mo_ref[...], vo_ref[...] = _adamw_math(w_ref[...], g, m_ref[...], v_ref[...])

    half = _rows(rh, width)
    return _hosted_call(body, riders, name=name, steps=2, in_specs=[place_spec, half] + halves + [half, half],
                        out_specs=[half] * 4, out_shape=out_shape, args=[place, w, mine, siblings, m, v])


def _adamw_small(gathered, w, m, v, name):
    def body(ga_ref, w_ref, m_ref, v_ref, go_ref, d_ref, mo_ref, vo_ref):
        g = ga_ref[0]
        for dev in range(1, N_DEV):
            g = g + ga_ref[dev]
        go_ref[...] = g
        d_ref[...], mo_ref[...], vo_ref[...] = _adamw_math(w_ref[...], g, m_ref[...], v_ref[...])

    return pl.pallas_call(body, name=name, out_shape=[jax.ShapeDtypeStruct(w.shape, F32)] * 4,
                          compiler_params=_params())(gathered, w, m, v)


class _Exchange:
    FIRST = ("w1_gate", "w1_up", "w1_down")
    HOSTS = {"ffn2_wgrad_gate": (("w2_down",), ()), "ffn2_wgrad_up": (("w2_gate",), ("w2_down",)),
             "in_bwd": (("w2_up",), ("w2_gate",)), "wgrad_in": ((), ("w2_up",)),
             "ffn1_wgrad_down": ((), ("w_in",)), "ffn1_wgrad_gate": (("w1_down",), ()), "ffn1_wgrad_up": ((), ("w1_down", "w1_gate")),
             "wgrad_out": ((), ("w1_up",))}
    ALONE = ("w_in", "w1_gate", "w1_up", "w_out")

    def __init__(self, bufs, place):
        self.bufs, self.place = bufs, place
        self.later = [k for k in SEGMENTS if k not in self.FIRST]
        self.split, self.own, self.to_send, self.received = {}, {}, {}, {}

    def first_weights(self):
        return dict(zip(self.FIRST, _gather_weights([self.bufs[k] for k in self.FIRST])))

    def riders(self, host):
        if host == "ffn1_fwd":
            return [_gather_rider([self.bufs[k] for k in self.later])]
        if host == "in_fwd":
            return [_forward_rider([self.bufs[k] for k in self.later[1:]])]
        halves, sums = self.HOSTS.get(host, ((), ()))
        return ([_sibling_rider([self.split[k] for k in halves])] if halves else []) + (
            [_scatter_rider([self.to_send[k] for k in sums])] if sums else [])

    def landed(self, host, results):
        if host == "ffn1_fwd":
            self.bufs.update(zip(self.later, results[0]))
            return dict(zip(self.later[:1], _alone(_forward_rider([self.bufs[self.later[0]]]), "gather_forward_first")))
        if host == "in_fwd":
            return dict(zip(self.later[1:], results[0]))
        halves, sums = self.HOSTS.get(host, ((), ()))
        if halves:
            self._chip_sums(halves, results[0])
        if sums:
            self.received.update(zip(sums, results[-1]))

    def gradient(self, name, grad):
        self.split[name] = grad.reshape(N_CHIPS, 2, grad.shape[0] // (2 * N_CHIPS), grad.shape[1])
        if name in self.ALONE:
            self._chip_sums([name], _alone(_sibling_rider([self.split[name]]), f"reduce_sibling_{name}"))

    def _chip_sums(self, names, from_sibling):
        for k, fs in zip(names, from_sibling):
            self.own[k], self.to_send[k] = _chip_sum(self.split[k], fs, self.place, f"chip_sum_{k}")

    def summed_halves(self):
        late = [k for k in SEGMENTS if k not in self.received]
        self.received.update(zip(late, _alone(_scatter_rider([self.to_send[k] for k in late]), "reduce_chips_last")))
        mine, _ = _total_sums([self.own[k] for k in SEGMENTS], [self.received[k] for k in SEGMENTS], "total_sums")
        return mine, _alone(_swap_rider(mine), "swap_halves")


SMALL = ("g_ffn1_pre", "g_ffn1_post", "g_mix_pre", "w_pool_lin", "pool_scale", "g_mix_post", "g_ffn2_pre", "g_ffn2_post")
WEIGHTS = ("g_ffn1_pre", "w1_gate", "w1_up", "w1_down", "g_ffn1_post", "g_mix_pre", "w_in", "w_pool_lin", "pool_scale", "w_out",
           "g_mix_post", "g_ffn2_pre", "w2_gate", "w2_up", "w2_down", "g_ffn2_post")
LANES = 128


def _pack_small(tree, extra=0.0):
    flat = jnp.concatenate([tree[k].reshape(-1) for k in SMALL] + [jnp.reshape(extra, (1,)).astype(F32)])
    rows = -(-flat.shape[0] // (8 * LANES)) * 8
    return jnp.pad(flat, (0, rows * LANES - flat.shape[0])).reshape(rows, LANES)


def _unpack_small(packed, like):
    flat, out, at = packed.reshape(-1), {}, 0
    for k in SMALL:
        size = math.prod(like[k].shape)
        out[k] = flat[at:at + size].reshape(like[k].shape)
        at += size
    return out


def kernel(x, g_ffn1_pre, w1_gate, w1_up, w1_down, g_ffn1_post, g_mix_pre, w_in, w_pool_lin, pool_scale, w_out, g_mix_post, g_ffn2_pre, w2_gate, w2_up, w2_down, g_ffn2_post, loss_target, m_g_ffn1_pre, m_w1_gate, m_w1_up, m_w1_down, m_g_ffn1_post, m_g_mix_pre, m_w_in, m_w_pool_lin, m_pool_scale, m_w_out, m_g_mix_post, m_g_ffn2_pre, m_w2_gate, m_w2_up, m_w2_down, m_g_ffn2_post, v_g_ffn1_pre, v_w1_gate, v_w1_up, v_w1_down, v_g_ffn1_post, v_g_mix_pre, v_w_in, v_w_pool_lin, v_pool_scale, v_w_out, v_g_mix_post, v_g_ffn2_pre, v_w2_gate, v_w2_up, v_w2_down, v_g_ffn2_post):
    given = dict(locals())
    w = {k: given[k] for k in WEIGHTS}
    m = {k: given["m_" + k] for k in WEIGHTS}
    v = {k: given["v_" + k] for k in WEIGHTS}
    small = {k: (w[k][0] if k == "w_pool_lin" else w[k].reshape(1, -1)) for k in SMALL}

    place = jnp.stack([2 * lax.axis_index("x") + lax.axis_index("y"), lax.axis_index("c")]).astype(jnp.int32)
    def as_rows(a, k):
        return jnp.swapaxes(a, 1, 2)[0] if k in ROWS_OUTSIDE else a[0]

    def as_given(a, k):
        return jnp.swapaxes(a[None], 1, 2) if k in ROWS_OUTSIDE else a[None]

    in_kernel = [k for k in TRANSPOSED if k not in ROWS_OUTSIDE]
    bufs = {}
    for tag, names in (("first", _Exchange.FIRST), ("rest", [k for k in SEGMENTS if k not in _Exchange.FIRST])):
        bufs.update(zip(names, _cast_shards([as_rows(w[k], k) for k in names], [k in in_kernel for k in names], place, f"cast_{tag}")))
    exchange = _Exchange(bufs, place)
    loss_part, grad_x, small_grads = _local_step(x[0], loss_target[0], small, exchange)

    out_grad, out_delta, out_m, out_v = {}, {}, {}, {}
    for k, mine, siblings in zip(SEGMENTS, *exchange.summed_halves()):
        results, _ = _adamw(as_rows(w[k], k), mine, siblings, place, as_rows(m[k], k), as_rows(v[k], k), k in in_kernel, f"adamw_{k}")
        out_grad[k], out_delta[k], out_m[k], out_v[k] = (as_given(a, k) for a in results)

    small_grads["w_pool_lin"] = small_grads["w_pool_lin"][None]
    packed = _pack_small(small_grads, loss_part)
    gathered = _gather_small(packed).reshape(N_DEV, *packed.shape)
    like = {k: w[k] for k in SMALL}
    results = _adamw_small(gathered, _pack_small(like), _pack_small({k: m[k] for k in SMALL}),
                           _pack_small({k: v[k] for k in SMALL}), "adamw_small")
    for tree, res in zip((out_grad, out_delta, out_m, out_v), results):
        tree.update(_unpack_small(res, like))
    loss = results[0].reshape(-1)[sum(math.prod(like[k].shape) for k in SMALL)]

    return (loss, grad_x[None], *[out_grad[k] for k in WEIGHTS], *[out_delta[k] for k in WEIGHTS],
            *[out_m[k] for k in WEIGHTS], *[out_v[k] for k in WEIGHTS])
```

```python
import math
import typing

import numpy as np
import jax
import jax.numpy as jnp
from jax import lax
from jax.experimental import pallas as pl
from jax.experimental.pallas import tpu as pltpu

F32 = jnp.float32
BF16 = jnp.bfloat16
MESH = pl.DeviceIdType.MESH

RMS_EPS = 1e-6
HEAD_DIM = 64
POOL_HALF_WINDOWS = (1, 2, 4, 8)
POOL_DIM = 256
GROUP_DIM = 256
DILATIONS = (1, 4, 16)
N_SIDE = 64
N_ATTN_HEADS = 12
ADAM_LR, ADAM_B1, ADAM_B2, ADAM_EPS, ADAM_WD, ADAM_STEP = 0.001, 0.9, 0.999, 1e-08, 0.01, 10

N_CHIPS = 4
V7X_VMEM_LIMIT = 60 * 1024 * 1024

_NT = (((1,), (1,)), ((), ()))
_TN = (((0,), (0,)), ((), ()))


def _dot(a, b):
    return jnp.dot(a, b, preferred_element_type=F32)


def _dot_nt(a, b):
    return lax.dot_general(a, b, _NT, preferred_element_type=F32)


def _dot_tn(a, b):
    return lax.dot_general(a, b, _TN, preferred_element_type=F32)


def _params(**kw):
    return pltpu.CompilerParams(vmem_limit_bytes=V7X_VMEM_LIMIT, **kw)


def _rows(tm, width):
    return pl.BlockSpec((tm, width), lambda i: (i, 0))


def _resident(shape):
    return pl.BlockSpec(shape, lambda i: (0,) * len(shape), pipeline_mode=pl.Buffered(1))


def _const(shape):
    return pl.BlockSpec(shape, lambda i: (0,) * len(shape))


def _inv_rms(x):
    return lax.rsqrt(jnp.mean(x * x, axis=-1, keepdims=True) + RMS_EPS)


def _rms_bwd(x, inv, g, dy):
    n = x * inv
    dn = dy * g
    dx = inv * (dn - n * jnp.mean(dn * n, axis=-1, keepdims=True))
    return dx, jnp.sum(dy * n, axis=0, keepdims=True)


def _accumulate(ref, value):
    @pl.when(pl.program_id(0) == 0)
    def _():
        ref[...] = jnp.zeros_like(ref)

    ref[...] += value


class _Rider(typing.NamedTuple):
    operands: list
    landing: typing.Optional[list]
    sems: tuple
    start: typing.Callable
    wait: typing.Callable


def _hosted_call(body, riders, *, name, steps, in_specs, out_specs, out_shape, args, scratch_shapes=()):
    params = _params(dimension_semantics=("arbitrary",))
    riders = list(riders or [])
    if not riders:
        res = pl.pallas_call(body, name=name, grid=(steps,), in_specs=in_specs, out_specs=out_specs, out_shape=out_shape,
                             scratch_shapes=list(scratch_shapes), compiler_params=params)(*args)
        return list(res), []
    n_in, n_out, n_scratch = len(in_specs), len(out_specs), len(scratch_shapes)
    operands, landing, aliases, spans = [], [], {}, []
    for rd in riders:
        lands = rd.landing if rd.landing is not None else [jax.ShapeDtypeStruct(a.shape, a.dtype) for a in rd.operands]
        if rd.landing is None:
            aliases.update({n_in + len(operands) + i: n_out + len(landing) + i for i in range(len(lands))})
        spans.append((len(operands), len(rd.operands), len(landing), len(lands)))
        operands += rd.operands
        landing += lands
    outs_at = n_in + len(operands)
    scratch_at = outs_at + n_out + len(landing)

    def riding(*refs):
        def each(action):
            for i, (rd, (in_at, n_ops, out_at, n_lands)) in enumerate(zip(riders, spans)):
                sems = refs[scratch_at + n_scratch + 2 * i:scratch_at + n_scratch + 2 * i + 2]
                getattr(rd, action)(refs[n_in + in_at:n_in + in_at + n_ops],
                                    refs[outs_at + n_out + out_at:outs_at + n_out + out_at + n_lands], *sems)

        @pl.when(pl.program_id(0) == 0)
        def _():
            each("start")

        body(*refs[:n_in], *refs[outs_at:outs_at + n_out], *refs[scratch_at:scratch_at + n_scratch])

        @pl.when(pl.program_id(0) == steps - 1)
        def _():
            each("wait")

    any_spec = pl.BlockSpec(memory_space=pl.ANY)
    res = pl.pallas_call(
        riding, name=name, grid=(steps,), in_specs=list(in_specs) + [any_spec] * len(operands),
        out_specs=list(out_specs) + [any_spec] * len(landing), out_shape=list(out_shape) + landing,
        scratch_shapes=list(scratch_shapes) + [pltpu.SemaphoreType.DMA(rd.sems) for rd in riders for _ in range(2)],
        input_output_aliases=aliases, compiler_params=params)(*args, *operands)
    return list(res[:n_out]), [list(res[n_out + out_at:n_out + out_at + n_lands]) for _, _, out_at, n_lands in spans]


_SUB_TILE = 256


def _sub_tiles(tm):
    return [pl.ds(r, _SUB_TILE) for r in range(0, tm, _SUB_TILE)]


def _ffn_fwd(x, g_pre, wg_t, wu_t, wd, g_post, target, name, riders=None, tm=512):
    s, d = x.shape
    ff = wd.shape[0]
    with_loss = target is not None

    def body(*refs):
        if with_loss:
            x_ref, gpre_ref, wg_ref, wu_ref, wd_ref, gpost_ref, t_ref, xo_ref, a_ref, b_ref, f_ref, loss_ref = refs
        else:
            x_ref, gpre_ref, wg_ref, wu_ref, wd_ref, gpost_ref, xo_ref, a_ref, b_ref, f_ref = refs
        loss = 0.0
        for rows in _sub_tiles(tm):
            xv = x_ref[rows, :]
            hb = (xv * _inv_rms(xv) * gpre_ref[...]).astype(BF16)
            a = _dot_nt(hb, wg_ref[...])
            b = _dot_nt(hb, wu_ref[...])
            hh = (a * jax.nn.sigmoid(a)) * b
            f = _dot(hh.astype(BF16), wd_ref[...])
            xo = xv + 0.5 * (f * _inv_rms(f) * gpost_ref[...])
            a_ref[rows, :] = a.astype(BF16)
            b_ref[rows, :] = b.astype(BF16)
            f_ref[rows, :] = f
            if with_loss:
                e = xo - t_ref[rows, :]
                xo_ref[rows, :] = e * (1.0 / d)
                loss = loss + 0.5 * jnp.sum(jnp.mean(e * e, axis=-1, keepdims=True))
            else:
                xo_ref[rows, :] = xo
        if with_loss:
            _accumulate(loss_ref, loss)

    in_specs = [_rows(tm, d), _const((1, d)), _resident((ff, d)), _resident((ff, d)), _resident((ff, d)), _const((1, d))]
    args = [x, g_pre, wg_t, wu_t, wd, g_post]
    out_shape = [jax.ShapeDtypeStruct((s, d), F32), jax.ShapeDtypeStruct((s, ff), BF16),
                 jax.ShapeDtypeStruct((s, ff), BF16), jax.ShapeDtypeStruct((s, d), F32)]
    out_specs = [_rows(tm, d), _rows(tm, ff), _rows(tm, ff), _rows(tm, d)]
    if with_loss:
        in_specs.append(_rows(tm, d))
        args.append(target)
        out_shape.append(jax.ShapeDtypeStruct((8, 128), F32))
        out_specs.append(_const((8, 128)))
    return _hosted_call(body, riders, name=name, steps=s // tm, in_specs=in_specs, out_specs=out_specs, out_shape=out_shape, args=args)


def _ffn_bwd(dxo, x, f, a, b, g_pre, g_post, wg_t, wu_t, wd, name, riders=None, tm=256):
    s, d = x.shape
    ff = wd.shape[0]

    def body(dxo_ref, x_ref, f_ref, a_ref, b_ref, gpre_ref, gpost_ref, wg_ref, wu_ref, wd_ref,
             dx_ref, hh_ref, da_ref, db_ref, df_ref, h_ref, dgpre_ref, dgpost_ref):
        dgpre_sum = dgpost_sum = 0.0
        for rows in _sub_tiles(tm):
            dxo_v = dxo_ref[rows, :]
            fv = f_ref[rows, :]
            df, dgpost = _rms_bwd(fv, _inv_rms(fv), gpost_ref[...], 0.5 * dxo_v)
            dfb = df.astype(BF16)
            dhh = _dot_nt(dfb, wd_ref[...])
            av = a_ref[rows, :].astype(F32)
            bv = b_ref[rows, :].astype(F32)
            sig = jax.nn.sigmoid(av)
            sa = av * sig
            da = (dhh * bv * (sig * (1.0 + av * (1.0 - sig)))).astype(BF16)
            db = (dhh * sa).astype(BF16)
            dh = _dot(da, wg_ref[...]) + _dot(db, wu_ref[...])
            xv = x_ref[rows, :]
            inv = _inv_rms(xv)
            dxn, dgpre = _rms_bwd(xv, inv, gpre_ref[...], dh)
            dx_ref[rows, :] = dxo_v + dxn
            hh_ref[rows, :] = (sa * bv).astype(BF16)
            da_ref[rows, :] = da
            db_ref[rows, :] = db
            df_ref[rows, :] = dfb
            h_ref[rows, :] = (xv * inv * gpre_ref[...]).astype(BF16)
            dgpre_sum, dgpost_sum = dgpre_sum + dgpre, dgpost_sum + dgpost
        _accumulate(dgpre_ref, dgpre_sum)
        _accumulate(dgpost_ref, dgpost_sum)

    return _hosted_call(
        body, riders, name=name, steps=s // tm,
        in_specs=[_rows(tm, d), _rows(tm, d), _rows(tm, d), _rows(tm, ff), _rows(tm, ff), _const((1, d)), _const((1, d)),
                  _resident((ff, d)), _resident((ff, d)), _resident((ff, d))],
        out_specs=[_rows(tm, d), _rows(tm, ff), _rows(tm, ff), _rows(tm, ff), _rows(tm, d), _rows(tm, d),
                   _const((1, d)), _const((1, d))],
        out_shape=[jax.ShapeDtypeStruct((s, d), F32), jax.ShapeDtypeStruct((s, ff), BF16), jax.ShapeDtypeStruct((s, ff), BF16),
                   jax.ShapeDtypeStruct((s, ff), BF16), jax.ShapeDtypeStruct((s, d), BF16), jax.ShapeDtypeStruct((s, d), BF16),
                   jax.ShapeDtypeStruct((1, d), F32), jax.ShapeDtypeStruct((1, d), F32)],
        args=[dxo, x, f, a, b, g_pre, g_post, wg_t, wu_t, wd])


def _wgrad(lhs, rhs, name, riders=None, rt=256):
    s, r = lhs.shape
    c = rhs.shape[1]

    def body(l_ref, r_ref, o_ref):
        o_ref[...] = _dot_tn(l_ref[...], r_ref[...])

    (out,), riding = _hosted_call(
        body, riders, name=name, steps=pl.cdiv(r, rt), in_specs=[pl.BlockSpec((s, rt), lambda i: (0, i)), _resident((s, c))],
        out_specs=[pl.BlockSpec((rt, c), lambda i: (i, 0))], out_shape=[jax.ShapeDtypeStruct((r, c), F32)], args=[lhs, rhs])
    return out, riding


def _attn_dtype(dilation):
    return BF16 if dilation == 1 else F32


def _in_fwd(x, g, w_in_t, name, riders=None, tm=1024):
    s, d = x.shape
    d_in = w_in_t.shape[0]
    n_groups = len(DILATIONS)
    dtypes = [_attn_dtype(dil) for dil in DILATIONS] * 3

    def body(x_ref, g_ref, w_ref, h_ref, u_ref, *part_refs):
        xv = x_ref[...]
        hb = (xv * _inv_rms(xv) * g_ref[...]).astype(BF16)
        h_ref[...] = hb
        z = _dot_nt(hb, w_ref[...])
        u_ref[...] = z[:, :POOL_DIM]
        for j, ref in enumerate(part_refs):
            part = z[:, POOL_DIM + GROUP_DIM * j:POOL_DIM + GROUP_DIM * (j + 1)]
            ref[...] = (part * _SCORE_SCALE if j < n_groups else part).astype(ref.dtype)

    return _hosted_call(
        body, riders, name=name, steps=s // tm, in_specs=[_rows(tm, d), _const((1, d)), _resident((d_in, d))],
        out_specs=[_rows(tm, d), _rows(tm, POOL_DIM)] + [_rows(tm, GROUP_DIM)] * len(dtypes),
        out_shape=[jax.ShapeDtypeStruct((s, d), BF16), jax.ShapeDtypeStruct((s, POOL_DIM), F32)]
        + [jax.ShapeDtypeStruct((s, GROUP_DIM), dt) for dt in dtypes],
        args=[x, g, w_in_t])


def _in_bwd(du, dparts, x, dxo, g, w_in_t, name, riders=None, tm=512):
    s, d = x.shape
    d_in = w_in_t.shape[0]
    n_parts = len(dparts)

    def body(du_ref, *refs):
        part_refs = refs[:n_parts]
        x_ref, dxo_ref, g_ref, w_ref, dx_ref, dz_ref, dg_ref = refs[n_parts:]
        dz = jnp.concatenate([r[...].astype(BF16) for r in (du_ref,) + part_refs], axis=1)
        dz_ref[...] = dz
        dh = _dot(dz, w_ref[...])
        xv = x_ref[...]
        dxn, dg = _rms_bwd(xv, _inv_rms(xv), g_ref[...], dh)
        dx_ref[...] = dxo_ref[...] + dxn
        _accumulate(dg_ref, dg)

    return _hosted_call(
        body, riders, name=name, steps=s // tm,
        in_specs=[_rows(tm, POOL_DIM)] + [_rows(tm, GROUP_DIM)] * n_parts + [_rows(tm, d), _rows(tm, d), _const((1, d)),
                                                                             _resident((d_in, d))],
        out_specs=[_rows(tm, d), _rows(tm, d_in), _const((1, d))],
        out_shape=[jax.ShapeDtypeStruct((s, d), F32), jax.ShapeDtypeStruct((s, d_in), BF16), jax.ShapeDtypeStruct((1, d), F32)],
        args=[du, *dparts, x, dxo, g, w_in_t])


_POOL_HALO = 8


def _pool_chain(v, first_shift):
    n = v.shape[0]
    p2 = v + pltpu.roll(v, first_shift, 0)
    p4 = pltpu.roll(p2, 1, 0) + pltpu.roll(p2, n - 1, 0)
    p8 = pltpu.roll(p4, 2, 0) + pltpu.roll(p4, n - 2, 0)
    p16 = pltpu.roll(p8, 4, 0) + pltpu.roll(p8, n - 4, 0)
    group = lax.broadcasted_iota(jnp.int32, v.shape, 1) // HEAD_DIM
    return jnp.where(group == 0, p2, jnp.where(group == 1, p4, jnp.where(group == 2, p8, p16)))


def _pool_count(t0, rows, s):
    t = t0 + lax.broadcasted_iota(jnp.int32, (rows, POOL_DIM), 0)
    group = lax.broadcasted_iota(jnp.int32, (rows, POOL_DIM), 1) // HEAD_DIM
    half = jnp.where(group == 0, 1, jnp.where(group == 1, 2, jnp.where(group == 2, 4, 8)))
    cnt = jnp.minimum(t + half, s) - jnp.maximum(t - half, 0)
    return jnp.maximum(cnt, 1).astype(F32)


def _pad_rows(ref, pad_ref, s):
    zeros = jnp.zeros((_POOL_HALO, pad_ref.shape[1]), pad_ref.dtype)
    pad_ref[pl.ds(0, _POOL_HALO), :] = zeros
    pad_ref[pl.ds(_POOL_HALO + s, _POOL_HALO), :] = zeros
    pad_ref[pl.ds(_POOL_HALO, s), :] = ref[...]


def _pool_fwd(u, w_bd, scale, name, tm=512):
    s = u.shape[0]
    ext = tm + 2 * _POOL_HALO

    def body(u_ref, w_ref, sc_ref, o_ref, upad):
        _pad_rows(u_ref, upad, s)

        def tile(i, carry):
            t0 = pl.multiple_of(i * tm, tm)
            uv = upad[pl.ds(t0, ext), :]
            win = _pool_chain(uv, 1)[_POOL_HALO:_POOL_HALO + tm]
            y = win / _pool_count(t0, tm, s) - uv[_POOL_HALO:_POOL_HALO + tm]
            o_ref[pl.ds(t0, tm), :] = (_dot(y.astype(BF16), w_ref[...]) * sc_ref[...]).astype(BF16)
            return carry

        lax.fori_loop(0, s // tm, tile, 0)

    return pl.pallas_call(body, name=name, out_shape=jax.ShapeDtypeStruct((s, POOL_DIM), BF16),
                          scratch_shapes=[pltpu.VMEM((s + 2 * _POOL_HALO, POOL_DIM), F32)],
                          compiler_params=_params())(u, w_bd, scale)


def _pool_bwd(u, da, w_bd, scale, name, tm=512):
    s = u.shape[0]
    ext = tm + 2 * _POOL_HALO

    def body(u_ref, da_ref, w_ref, sc_ref, du_ref, dw_ref, dsc_ref, upad, dapad):
        _pad_rows(u_ref, upad, s)
        _pad_rows(da_ref, dapad, s)
        dw_ref[...] = jnp.zeros_like(dw_ref)
        dsc_ref[...] = jnp.zeros_like(dsc_ref)

        def tile(i, carry):
            t0 = pl.multiple_of(i * tm, tm)
            uv = upad[pl.ds(t0, ext), :]
            dav = dapad[pl.ds(t0, ext), :]
            win = _pool_chain(uv, 1)[_POOL_HALO:_POOL_HALO + tm]
            yb = (win / _pool_count(t0, tm, s) - uv[_POOL_HALO:_POOL_HALO + tm]).astype(BF16)
            yl = _dot(yb, w_ref[...])
            da_c = dav[_POOL_HALO:_POOL_HALO + tm]
            dsc_ref[...] += jnp.sum(da_c * yl, axis=0, keepdims=True)
            dyl = (dav * sc_ref[...]).astype(BF16)
            dw_ref[...] += _dot_tn(yb, dyl[_POOL_HALO:_POOL_HALO + tm])
            dy = _dot_nt(dyl, w_ref[...])
            dyc = dy / _pool_count(t0 - _POOL_HALO, ext, s)
            du_ref[pl.ds(t0, tm), :] = (_pool_chain(dyc, ext - 1) - dy)[_POOL_HALO:_POOL_HALO + tm]
            return carry

        lax.fori_loop(0, s // tm, tile, 0)

    pool_cols = pl.BlockSpec((s, POOL_DIM), lambda i: (0, 0), pipeline_mode=pl.Buffered(1))
    return pl.pallas_call(
        body, name=name, grid=(1,),
        in_specs=[pool_cols, pool_cols, _const((POOL_DIM, POOL_DIM)), _const((1, POOL_DIM))],
        out_specs=[_const((s, POOL_DIM)), _const((POOL_DIM, POOL_DIM)), _const((1, POOL_DIM))],
        out_shape=[jax.ShapeDtypeStruct((s, POOL_DIM), F32), jax.ShapeDtypeStruct((POOL_DIM, POOL_DIM), F32),
                   jax.ShapeDtypeStruct((1, POOL_DIM), F32)],
        scratch_shapes=[pltpu.VMEM((s + 2 * _POOL_HALO, POOL_DIM), F32), pltpu.VMEM((s + 2 * _POOL_HALO, POOL_DIM), F32)],
        compiler_params=_params(dimension_semantics=("arbitrary",)))(u, da, w_bd, scale)


_BQ = 128
_KW = _BQ + 2 * N_SIDE
_PAIR = 2 * HEAD_DIM
_NEG = -1e30
_ATTN_UNROLL = 8
_SCORE_SCALE = HEAD_DIM ** -0.5


def _stack_heads(x):
    lane_head = lax.broadcasted_iota(jnp.int32, x.shape, 1) // HEAD_DIM
    zero = jnp.zeros_like(x)
    return jnp.concatenate([jnp.where(lane_head == 0, x, zero), jnp.where(lane_head == 1, x, zero)], axis=0)


def _unstack_heads(x):
    lane_head = lax.broadcasted_iota(jnp.int32, (_BQ, _PAIR), 1) // HEAD_DIM
    return jnp.where(lane_head == 0, x[:_BQ], x[_BQ:])


def _stack_cols(x):
    return jnp.concatenate([x[:, 0:1], x[:, HEAD_DIM:HEAD_DIM + 1]], axis=0)


def _fill_bias(bias_ref, slopes_ref, dilation):
    row = lax.broadcasted_iota(jnp.int32, (2 * _BQ, _KW), 0)
    col = lax.broadcasted_iota(jnp.int32, (2 * _BQ, _KW), 1)
    pair = 2 * pl.program_id(0)
    slope = jnp.where(row < _BQ, slopes_ref[pair], slopes_ref[pair + 1]) * float(dilation)

    @pl.when(pl.program_id(1) == 0)
    def _():
        for j in range(3):
            dist = jnp.abs(col - (row & (_BQ - 1)) - j * N_SIDE)
            bias_ref[j] = jnp.where(dist <= N_SIDE, -slope * dist.astype(F32), _NEG)


def _block_window(i, n_blocks, length):
    q0 = pl.multiple_of(i * _BQ, _BQ)
    ws = pl.multiple_of(jnp.clip(q0 - N_SIDE, 0, length - _KW), N_SIDE)
    return q0, ws, jnp.where(i == 0, 0, jnp.where(i == n_blocks - 1, 2, 1))


_FREE_STRIDE = 4


def _residue_views(dilation, seq, ins, outs, tmps):
    step = pl.program_id(1)
    if dilation <= _FREE_STRIDE:
        def rows(start, count):
            return pl.ds(start, count) if dilation == 1 else pl.ds(start * dilation + step, count, stride=dilation)

        return ins, outs, rows, lambda: None
    inner = dilation // _FREE_STRIDE
    assert inner <= _FREE_STRIDE and len(tmps) == len(ins) + len(outs)
    first, second = step // inner, step % inner
    coarse = pl.ds(first, seq // _FREE_STRIDE, stride=_FREE_STRIDE)
    in_tmps, out_tmps = tmps[:len(ins)], tmps[len(ins):]

    @pl.when(second == 0)
    def _():
        for ref, tmp in zip(ins, in_tmps):
            tmp[...] = ref[coarse, :]

    def flush():
        @pl.when(second == inner - 1)
        def _():
            for ref, tmp in zip(outs, out_tmps):
                ref[coarse, :] = tmp[...]

    return in_tmps, out_tmps, lambda start, count: pl.ds(start * inner + second, count, stride=inner), flush


def _attn_call(body, name, dilation, seq, n_in, out_dtypes, scratch, buffers):
    col = pl.BlockSpec((seq, _PAIR), lambda c, r: (0, c), pipeline_mode=pl.Buffered(buffers))
    tmps = [pltpu.VMEM((seq // _FREE_STRIDE, _PAIR), F32)] * (n_in + len(out_dtypes) if dilation > _FREE_STRIDE else 0)
    return pl.pallas_call(
        body, name=name, grid=(GROUP_DIM // _PAIR, dilation),
        in_specs=[pl.BlockSpec(memory_space=pltpu.SMEM)] + [col] * n_in, out_specs=[col] * len(out_dtypes),
        out_shape=[jax.ShapeDtypeStruct((seq, GROUP_DIM), dt) for dt in out_dtypes], scratch_shapes=scratch + tmps,
        compiler_params=_params(dimension_semantics=("arbitrary", "arbitrary")))


def _staged(dilation, length, rows, sources, scratch):
    if dilation == 1:
        return sources
    for src, dst in zip(sources, scratch):
        dst[...] = src[rows(0, length), :].astype(BF16)
    return scratch


def _attn_fwd(q, k, v, slopes, dilation, name):
    seq = q.shape[0]
    length = seq // dilation
    n_blocks = length // _BQ
    n_stage = 0 if dilation == 1 else 3

    def body(sl_ref, q_ref, k_ref, v_ref, o_ref, lse_ref, *scratch):
        bias_ref, tmps = scratch[n_stage], scratch[n_stage + 1:]
        (q_in, k_in, v_in), (o_out, lse_out), rows, flush = _residue_views(dilation, seq, (q_ref, k_ref, v_ref), (o_ref, lse_ref), tmps)
        qs, ks, vs = _staged(dilation, length, rows, (q_in, k_in, v_in), scratch[:n_stage])
        _fill_bias(bias_ref, sl_ref, dilation)

        def block(i, carry):
            q0, ws, which = _block_window(i, n_blocks, length)
            kw = ks[pl.ds(ws, _KW), :]
            vw = vs[pl.ds(ws, _KW), :]
            sc = _dot_nt(_stack_heads(qs[pl.ds(q0, _BQ), :]), kw) + bias_ref[which]
            m = jnp.max(sc, axis=-1, keepdims=True)
            p = jnp.exp(sc - m)
            den = jnp.sum(p, axis=-1, keepdims=True)
            o_out[rows(q0, _BQ), :] = _unstack_heads(_dot(p.astype(BF16), vw) / den)
            lse_out[rows(q0, _BQ), :] = _unstack_heads(jnp.broadcast_to(m + jnp.log(den), (2 * _BQ, _PAIR)))
            return carry

        lax.fori_loop(0, n_blocks, block, 0, unroll=min(_ATTN_UNROLL, n_blocks))
        flush()

    stage = pltpu.VMEM((length, _PAIR), BF16)
    bias = pltpu.VMEM((3, 2 * _BQ, _KW), F32)
    return _attn_call(body, name, dilation, seq, 3, [F32, F32], [stage] * n_stage + [bias], 2)(slopes, q, k, v)


def _attn_bwd(q, k, v, do, lse, cterm, slopes, dilation, name):
    seq = q.shape[0]
    length = seq // dilation
    n_blocks = length // _BQ
    n_stage, n_whole = (0, 0) if dilation == 1 else (4, 3)

    def body(sl_ref, q_ref, k_ref, v_ref, do_ref, lse_ref, c_ref, dq_ref, dk_ref, dv_ref, *scratch):
        dk_acc, dv_acc, bias_ref = scratch[n_stage:n_stage + 3]
        whole, tmps = scratch[n_stage + 3:n_stage + 3 + n_whole], scratch[n_stage + 3 + n_whole:]
        (q_in, k_in, v_in, do_in, lse_in, c_in), (dq_out, dk_out, dv_out), rows, flush = _residue_views(
            dilation, seq, (q_ref, k_ref, v_ref, do_ref, lse_ref, c_ref), whole or (dq_ref, dk_ref, dv_ref), tmps)
        all_rows = rows(0, length)
        qs, ks, vs, dos = _staged(dilation, length, rows, (q_in, k_in, v_in, do_in), scratch[:n_stage])
        dk_acc[...] = jnp.zeros_like(dk_acc)
        dv_acc[...] = jnp.zeros_like(dv_acc)
        _fill_bias(bias_ref, sl_ref, dilation)

        def block(i, carry):
            q0, ws, which = _block_window(i, n_blocks, length)
            qm = _stack_heads(qs[pl.ds(q0, _BQ), :])
            dom = _stack_heads(dos[pl.ds(q0, _BQ), :])
            kw = ks[pl.ds(ws, _KW), :]
            vw = vs[pl.ds(ws, _KW), :]
            p = jnp.exp(_dot_nt(qm, kw) + bias_ref[which] - _stack_cols(lse_in[rows(q0, _BQ), :]))
            ds = (p * (_dot_nt(dom, vw) + _stack_cols(c_in[rows(q0, _BQ), :]))).astype(BF16)
            dq_out[rows(q0, _BQ), :] = (_unstack_heads(_dot(ds, kw)) * _SCORE_SCALE).astype(dq_out.dtype)
            dk_acc[pl.ds(ws, _KW), :] += _dot_tn(ds, qm)
            dv_acc[pl.ds(ws, _KW), :] += _dot_tn(p.astype(BF16), dom)
            return carry

        lax.fori_loop(0, n_blocks, block, 0, unroll=min(_ATTN_UNROLL, n_blocks))
        dk_out[all_rows, :] = dk_acc[...].astype(dk_out.dtype)
        dv_out[all_rows, :] = dv_acc[...].astype(dv_out.dtype)
        flush()
        if whole:
            @pl.when(pl.program_id(1) == dilation - 1)
            def _():
                for ref, collected in zip((dq_ref, dk_ref, dv_ref), whole):
                    ref[...] = collected[...].astype(BF16)

    stage = pltpu.VMEM((length, _PAIR), BF16)
    acc = pltpu.VMEM((length, _PAIR), F32)
    bias = pltpu.VMEM((3, 2 * _BQ, _KW), F32)
    collect = pltpu.VMEM((seq, _PAIR), F32)
    return _attn_call(body, name, dilation, seq, 6, [BF16] * 3, [stage] * n_stage + [acc] * 2 + [bias] + [collect] * n_whole,
                      2 if dilation == 1 else 1)(slopes, q, k, v, do, lse, cterm)


def _group_weights(lses):
    m = jnp.maximum(jnp.maximum(lses[0], lses[1]), lses[2])
    es = [jnp.exp(l - m) for l in lses]
    den = es[0] + es[1] + es[2]
    return [e / den for e in es]


def _out_fwd(a_pool, outs, lses, x, w_out, g, name, tm=1024):
    s, d = x.shape
    width = POOL_DIM + 3 * GROUP_DIM

    def body(ap_ref, o0, o1, o2, l0, l1, l2, x_ref, w_ref, g_ref, xo_ref, cat_ref):
        alphas = _group_weights([l0[...], l1[...], l2[...]])
        cat = jnp.concatenate([ap_ref[...]] + [(o[...] * al).astype(BF16) for o, al in zip((o0, o1, o2), alphas)], axis=1)
        cat_ref[...] = cat
        mix = _dot(cat, w_ref[...])
        xo_ref[...] = x_ref[...] + mix * _inv_rms(mix) * g_ref[...]

    return pl.pallas_call(
        body, name=name, grid=(s // tm,),
        in_specs=[_rows(tm, POOL_DIM)] + [_rows(tm, GROUP_DIM)] * 6 + [_rows(tm, d), _resident(w_out.shape), _const((1, d))],
        out_specs=[_rows(tm, d), _rows(tm, width)],
        out_shape=[jax.ShapeDtypeStruct((s, d), F32), jax.ShapeDtypeStruct((s, width), BF16)],
        compiler_params=_params(dimension_semantics=("arbitrary",)))(a_pool, *outs, *lses, x, w_out, g)


def _out_bwd(dxo, cat, outs, lses, w_out, g, head_ones, name, tm=1024):
    s, d = dxo.shape

    def body(dxo_ref, cat_ref, o0, o1, o2, l0, l1, l2, w_ref, g_ref, ones_ref, dpool_ref, dmix_ref, do0, do1, do2, c0, c1, c2, dg_ref):
        mv = _dot(cat_ref[...], w_ref[...])
        dmix, dg = _rms_bwd(mv, _inv_rms(mv), g_ref[...], dxo_ref[...])
        dmb = dmix.astype(BF16)
        dmix_ref[...] = dmb
        _accumulate(dg_ref, dg)
        dcat = _dot_nt(dmb, w_ref[...])
        dpool_ref[...] = dcat[:, :POOL_DIM]
        alphas = _group_weights([l0[...], l1[...], l2[...]])
        das = [dcat[:, POOL_DIM + GROUP_DIM * j:POOL_DIM + GROUP_DIM * (j + 1)] for j in range(3)]
        prod = sum(da * (o[...] * al) for da, o, al in zip(das, (o0, o1, o2), alphas))
        hi = prod.astype(BF16)
        lo = (prod - hi.astype(F32)).astype(BF16)
        total = _dot(hi, ones_ref[...]) + _dot(lo, ones_ref[...])
        for da, al, do_ref, c_ref in zip(das, alphas, (do0, do1, do2), (c0, c1, c2)):
            do_ref[...] = (da * al).astype(do_ref.dtype)
            c_ref[...] = -al * total

    return pl.pallas_call(
        body, name=name, grid=(s // tm,),
        in_specs=[_rows(tm, d), _rows(tm, cat.shape[1])] + [_rows(tm, GROUP_DIM)] * 6 + [_resident(w_out.shape), _const((1, d)),
                                                                                        _const((GROUP_DIM, GROUP_DIM))],
        out_specs=[_rows(tm, POOL_DIM), _rows(tm, d)] + [_rows(tm, GROUP_DIM)] * 6 + [_const((1, d))],
        out_shape=[jax.ShapeDtypeStruct((s, POOL_DIM), F32), jax.ShapeDtypeStruct((s, d), BF16)]
        + [jax.ShapeDtypeStruct((s, GROUP_DIM), _attn_dtype(dil)) for dil in DILATIONS]
        + [jax.ShapeDtypeStruct((s, GROUP_DIM), F32)] * 3 + [jax.ShapeDtypeStruct((1, d), F32)],
        compiler_params=_params(dimension_semantics=("arbitrary",)))(dxo, cat, *outs, *lses, w_out, g, head_ones)


def _alibi_slopes():
    return np.array([2.0 ** (-8.0 * (i + 1) / N_ATTN_HEADS) for i in range(N_ATTN_HEADS)], np.float32)


def _block_diag(w_lin):
    n, c, _ = w_lin.shape
    eye = jnp.eye(n, dtype=w_lin.dtype)
    return (eye[:, None, :, None] * w_lin[:, :, None, :]).reshape(n * c, n * c)


class _NoExchange:
    def __init__(self, full):
        self.full, self.grads = full, {}

    def first_weights(self):
        return self.full

    def riders(self, host):
        return []

    def landed(self, host, results):
        return self.full

    def gradient(self, name, grad):
        self.grads[name] = grad


def _local_step(x, target, small, exchange):
    s, d = x.shape
    slopes = _alibi_slopes()
    group_slopes = [jnp.asarray(slopes[4 * g:4 * g + 4]) for g in range(3)]
    w_bd = _block_diag(small["w_pool_lin"]).astype(BF16)
    head_ones = jnp.asarray(np.kron(np.eye(GROUP_DIM // HEAD_DIM), np.ones((HEAD_DIM, HEAD_DIM))), BF16)

    full = dict(exchange.first_weights())

    def hosted(call, host, *args):
        results, riding = call(*args, host, exchange.riders(host))
        full.update(exchange.landed(host, riding) or {})
        return results

    x1, a1, b1, f1 = hosted(_ffn_fwd, "ffn1_fwd", x, small["g_ffn1_pre"], full["w1_gate"], full["w1_up"], full["w1_down"],
                            small["g_ffn1_post"], None)
    h2, u, *parts = hosted(_in_fwd, "in_fwd", x1, small["g_mix_pre"], full["w_in"])
    qs, ks, vs = parts[0:3], parts[3:6], parts[6:9]
    a_pool = _pool_fwd(u, w_bd, small["pool_scale"], "pool_fwd")
    outs, lses = [], []
    for g, dil in enumerate(DILATIONS):
        o, lse = _attn_fwd(qs[g], ks[g], vs[g], group_slopes[g], dil, f"attn_fwd{g}")
        outs.append(o)
        lses.append(lse)
    x2, cat = _out_fwd(a_pool, outs, lses, x1, full["w_out"], small["g_mix_post"], "out_fwd")
    (dx3, a2, b2, f2, loss_part), _ = _ffn_fwd(x2, small["g_ffn2_pre"], full["w2_gate"], full["w2_up"], full["w2_down"],
                                               small["g_ffn2_post"], target, "ffn2_fwd")

    small_grads = {}

    def ffn_backward(tag, dxo, x_in, f, a, b):
        n = tag[-1]
        dx, hh, da, db, df, h, dg_pre, dg_post = hosted(
            _ffn_bwd, f"{tag}_bwd", dxo, x_in, f, a, b, small[f"g_{tag}_pre"], small[f"g_{tag}_post"],
            full[f"w{n}_gate"], full[f"w{n}_up"], full[f"w{n}_down"])
        for part, lhs, rhs in (("down", hh, df), ("gate", da, h), ("up", db, h)):
            exchange.gradient(f"w{n}_{part}", hosted(_wgrad, f"{tag}_wgrad_{part}", lhs, rhs))
        small_grads[f"g_{tag}_pre"], small_grads[f"g_{tag}_post"] = dg_pre, dg_post
        return dx

    dx2 = ffn_backward("ffn2", dx3, x2, f2, a2, b2)
    dpool, dmix, *dos_cs, small_grads["g_mix_post"] = _out_bwd(dx2, cat, outs, lses, full["w_out"], small["g_mix_post"],
                                                               head_ones, "out_bwd")
    dos, cs = dos_cs[:3], dos_cs[3:]
    dqs, dks, dvs = [], [], []
    for g, dil in enumerate(DILATIONS):
        dq, dk, dv = _attn_bwd(qs[g], ks[g], vs[g], dos[g], lses[g], cs[g], group_slopes[g], dil, f"attn_bwd{g}")
        dqs.append(dq)
        dks.append(dk)
        dvs.append(dv)
    du, dw_bd, small_grads["pool_scale"] = _pool_bwd(u, dpool, w_bd, small["pool_scale"], "pool_bwd")
    n_pool = len(POOL_HALF_WINDOWS)
    small_grads["w_pool_lin"] = jnp.stack(
        [dw_bd[HEAD_DIM * g:HEAD_DIM * (g + 1), HEAD_DIM * g:HEAD_DIM * (g + 1)] for g in range(n_pool)])
    dx1, dz, small_grads["g_mix_pre"] = hosted(_in_bwd, "in_bwd", du, dqs + dks + dvs, x1, dx2, small["g_mix_pre"], full["w_in"])
    exchange.gradient("w_in", hosted(_wgrad, "wgrad_in", dz, h2))
    dx0 = ffn_backward("ffn1", dx1, x, f1, a1, b1)
    exchange.gradient("w_out", hosted(_wgrad, "wgrad_out", cat, dmix))
    return loss_part[0, 0], dx0, small_grads


SEGMENTS = ("w1_gate", "w1_up", "w1_down", "w_in", "w_out", "w2_gate", "w2_up", "w2_down")
TRANSPOSED = ("w1_gate", "w1_up", "w_in", "w2_gate", "w2_up")
ROWS_OUTSIDE = ("w1_gate", "w1_up", "w2_gate", "w2_up")
HALF = 512


def _place():
    x, y, c = lax.axis_index("x"), lax.axis_index("y"), lax.axis_index("c")
    other_chips = [(1 - x, y), (x, 1 - y), (1 - x, 1 - y)]
    return x, y, c, other_chips


def _chip_rows(chip, rows):
    return pl.ds(pl.multiple_of((2 * chip[0] + chip[1]) * rows, 16), rows)


def _cols(c):
    return pl.ds(pl.multiple_of(c * HALF, HALF), HALF)


def _cast_shards(shards, transposed, place, name):
    n = len(shards)
    rows = [w.shape[1] if t else w.shape[0] for w, t in zip(shards, transposed)]

    def body(place_ref, *refs):
        for w_ref, o_ref, t in zip(refs[:n], refs[n:], transposed):
            o_ref[...] = (w_ref[...].T if t else w_ref[...]).astype(BF16)

    once = pl.Buffered(1)
    return pl.pallas_call(
        body, name=name,
        grid_spec=pltpu.PrefetchScalarGridSpec(
            num_scalar_prefetch=1, grid=(1,),
            in_specs=[pl.BlockSpec(w.shape, lambda i, place: (0, 0), pipeline_mode=once) for w in shards],
            out_specs=[pl.BlockSpec((r, 1024), lambda i, place: (place[0], 0), pipeline_mode=once) for r in rows]),
        out_shape=[jax.ShapeDtypeStruct((N_CHIPS * r, 1024), BF16) for r in rows],
        compiler_params=_params(dimension_semantics=("arbitrary",)))(place, *shards)


def _gather_weights(bufs):
    n = len(bufs)
    rows = [b.shape[0] // N_CHIPS for b in bufs]

    def halves(r):
        first = -(-r // 32) * 16
        return (0, first), (first, r - first)

    def body(*refs):
        outs = refs[n:2 * n]
        ici_send, ici_recv, d2d_send, d2d_recv = refs[2 * n:]
        x, y, c, _ = _place()
        me, via_x, via_y, diagonal = (x, y), (1 - x, y), (x, 1 - y), (1 - x, 1 - y)

        def piece(chip, k, h, cols):
            start, size = halves(rows[k])[h]
            return outs[k].at[pl.ds(pl.multiple_of((2 * chip[0] + chip[1]) * rows[k] + start, 16), size), _cols(cols)]

        def ici(path, chip, k, h, to):
            blk = piece(chip, k, h, c)
            return pltpu.make_async_remote_copy(src_ref=blk, dst_ref=blk, send_sem=ici_send.at[path, k, h],
                                                recv_sem=ici_recv.at[path, k, h], device_id=(*to, c), device_id_type=MESH)

        def d2d(slot, chip, k, h, cols):
            blk = piece(chip, k, h, cols)
            return pltpu.make_async_remote_copy(src_ref=blk, dst_ref=blk, send_sem=d2d_send.at[slot, k, h],
                                                recv_sem=d2d_recv.at[slot, k, h], device_id=(x, y, 1 - c), device_id_type=MESH)

        started = [ici(0, me, k, h, via_x) for h in (0, 1) for k in range(n)] + [ici(1, me, k, h, via_y) for h in (1, 0) for k in range(n)]
        for cp in started:
            cp.start()

        def landed(path, slot, chip, k, h, pass_on_to=None):
            ici(path, chip, k, h, me).wait_recv()
            more = [d2d(slot, chip, k, h, c)] + ([ici(2, chip, k, h, pass_on_to)] if pass_on_to else [])
            for cp in more:
                cp.start()
            started.extend(more)

        for k in range(n):
            landed(0, 0, via_x, k, 0, pass_on_to=via_y)
            landed(1, 1, via_y, k, 1, pass_on_to=via_x)
        for k in range(n):
            landed(0, 0, via_x, k, 1)
            landed(1, 1, via_y, k, 0)
        for k in range(n):
            for h in range(2):
                landed(2, 2, diagonal, k, h)
        for slot, chip in enumerate((via_x, via_y, diagonal)):
            for k in range(n):
                for h in range(2):
                    d2d(slot, chip, k, h, 1 - c).wait_recv()
        for cp in started:
            cp.wait_send()

    any_spec = pl.BlockSpec(memory_space=pl.ANY)
    return pl.pallas_call(
        body, name="gather_weights", in_specs=[any_spec] * n, out_specs=[any_spec] * n,
        out_shape=[jax.ShapeDtypeStruct(b.shape, b.dtype) for b in bufs], input_output_aliases={k: k for k in range(n)},
        scratch_shapes=[pltpu.SemaphoreType.DMA((3, n, 2))] * 4)(*bufs)


def _gather_rider(bufs):
    n = len(bufs)
    rows = [b.shape[0] // N_CHIPS for b in bufs]

    def copies(outs, send_sems, recv_sems, inbound):
        x, y, c, chips = _place()
        for j, chip in enumerate(chips):
            for k in range(n):
                src_chip = chip if inbound else (x, y)
                blk = outs[k].at[_chip_rows(src_chip, rows[k]), _cols(c)]
                yield pltpu.make_async_remote_copy(src_ref=blk, dst_ref=blk, send_sem=send_sems.at[j, k], recv_sem=recv_sems.at[j, k],
                                                   device_id=(*chip, c), device_id_type=MESH)

    def start(ins, outs, send_sems, recv_sems):
        for cp in copies(outs, send_sems, recv_sems, False):
            cp.start()

    def wait(ins, outs, send_sems, recv_sems):
        for cp in copies(outs, send_sems, recv_sems, True):
            cp.wait_recv()
        for cp in copies(outs, send_sems, recv_sems, False):
            cp.wait_send()

    return _Rider(list(bufs), None, (3, n), start, wait)


def _forward_rider(bufs):
    n = len(bufs)
    rows = [b.shape[0] // N_CHIPS for b in bufs]

    def copies(outs, send_sems, recv_sems, half):
        x, y, c, chips = _place()
        for j, chip in enumerate(chips):
            for k in range(n):
                blk = outs[k].at[_chip_rows(chip, rows[k]), _cols(half(c))]
                yield pltpu.make_async_remote_copy(src_ref=blk, dst_ref=blk, send_sem=send_sems.at[j, k], recv_sem=recv_sems.at[j, k],
                                                   device_id=(x, y, 1 - c), device_id_type=MESH)

    def start(ins, outs, send_sems, recv_sems):
        for cp in copies(outs, send_sems, recv_sems, lambda c: c):
            cp.start()

    def wait(ins, outs, send_sems, recv_sems):
        for cp in copies(outs, send_sems, recv_sems, lambda c: 1 - c):
            cp.wait_recv()
        for cp in copies(outs, send_sems, recv_sems, lambda c: c):
            cp.wait_send()

    return _Rider(list(bufs), None, (3, n), start, wait)


def _sibling_rider(grads):
    n = len(grads)

    def copies(ins, outs, send_sems, recv_sems):
        x, y, c, _ = _place()
        return [pltpu.make_async_remote_copy(src_ref=ins[k].at[:, pl.ds(1 - c, 1)], dst_ref=outs[k], send_sem=send_sems.at[k],
                                             recv_sem=recv_sems.at[k], device_id=(x, y, 1 - c), device_id_type=MESH)
                for k in range(n)]

    def start(*refs):
        for cp in copies(*refs):
            cp.start()

    def wait(*refs):
        for cp in copies(*refs):
            cp.wait()

    return _Rider(list(grads), [jax.ShapeDtypeStruct((N_CHIPS, 1) + g.shape[2:], F32) for g in grads], (n,), start, wait)


def _alone(rider, name):
    n = len(rider.operands)
    landing = rider.landing if rider.landing is not None else [jax.ShapeDtypeStruct(a.shape, a.dtype) for a in rider.operands]
    n_out = len(landing)

    def body(*refs):
        rider.start(refs[:n], refs[n:n + n_out], *refs[n + n_out:])
        rider.wait(refs[:n], refs[n:n + n_out], *refs[n + n_out:])

    any_spec = pl.BlockSpec(memory_space=pl.ANY)
    return pl.pallas_call(body, name=name, in_specs=[any_spec] * n, out_specs=[any_spec] * n_out, out_shape=landing,
                          input_output_aliases={i: i for i in range(n)} if rider.landing is None else {},
                          scratch_shapes=[pltpu.SemaphoreType.DMA(rider.sems)] * 2)(*rider.operands)


def _chip_sum(grad, from_sibling, place, name):
    rh, width = grad.shape[2:]

    def body(place_ref, g_ref, s_ref, own_ref, all_ref):
        all_ref[...] = (g_ref[...] + s_ref[...]).astype(BF16)
        mine = place_ref[0]
        own_ref[0] = g_ref[mine, 0] + s_ref[mine, 0]

    blk = (N_CHIPS, 1, rh, width)
    once = pl.Buffered(1)
    return pl.pallas_call(
        body, name=name,
        grid_spec=pltpu.PrefetchScalarGridSpec(
            num_scalar_prefetch=1, grid=(1,),
            in_specs=[pl.BlockSpec(blk, lambda i, place: (0, place[1], 0, 0), pipeline_mode=once),
                      pl.BlockSpec(blk, lambda i, place: (0, 0, 0, 0), pipeline_mode=once)],
            out_specs=[pl.BlockSpec((1, rh, width), lambda i, place: (0, 0, 0), pipeline_mode=once),
                       pl.BlockSpec(blk, lambda i, place: (0, 0, 0, 0), pipeline_mode=once)]),
        out_shape=[jax.ShapeDtypeStruct((1, rh, width), F32), jax.ShapeDtypeStruct((N_CHIPS, 1, rh, width), BF16)],
        compiler_params=_params(dimension_semantics=("arbitrary",)))(place, grad, from_sibling)


def _scatter_rider(sums):
    n = len(sums)

    def copies(ins, outs, send_sems, recv_sems):
        x, y, c, chips = _place()
        return [pltpu.make_async_remote_copy(src_ref=ins[k].at[pl.ds(2 * chip[0] + chip[1], 1)], dst_ref=outs[k].at[pl.ds(j, 1)],
                                             send_sem=send_sems.at[j, k], recv_sem=recv_sems.at[j, k],
                                             device_id=(*chip, c), device_id_type=MESH)
                for j, chip in enumerate(chips) for k in range(n)]

    def start(*refs):
        for cp in copies(*refs):
            cp.start()

    def wait(*refs):
        for cp in copies(*refs):
            cp.wait()

    return _Rider(list(sums), [jax.ShapeDtypeStruct((3,) + sm.shape[1:], BF16) for sm in sums], (3, n), start, wait)


def _total_sums(owns, received, name, riders=None):
    n = len(owns)

    def body(*refs):
        for o_ref, r_ref, t_ref in zip(refs[:n], refs[n:2 * n], refs[2 * n:]):
            total = o_ref[0]
            for j in range(3):
                total = total + r_ref[j, 0].astype(F32)
            t_ref[0] = total

    return _hosted_call(body, riders, name=name, steps=1, in_specs=[_resident(a.shape) for a in owns + received],
                        out_specs=[_resident(o.shape) for o in owns], out_shape=[jax.ShapeDtypeStruct(o.shape, F32) for o in owns],
                        args=owns + received)


def _swap_rider(halves):
    n = len(halves)

    def copies(ins, outs, send_sems, recv_sems):
        x, y, c, _ = _place()
        return [pltpu.make_async_remote_copy(src_ref=ins[k], dst_ref=outs[k], send_sem=send_sems.at[k], recv_sem=recv_sems.at[k],
                                             device_id=(x, y, 1 - c), device_id_type=MESH) for k in range(n)]

    def start(*refs):
        for cp in copies(*refs):
            cp.start()

    def wait(*refs):
        for cp in copies(*refs):
            cp.wait()

    return _Rider(list(halves), [jax.ShapeDtypeStruct(h.shape, F32) for h in halves], (n,), start, wait)


N_DEV = 8


def _gather_small(block):
    m_per, width = block.shape

    def body(x_ref, out_ref, send_sems, recv_sems, local_sem):
        x, y, c, chips = _place()
        me, sibling = (x, y, c), (x, y, 1 - c)

        def rows(px, py, pc):
            return out_ref.at[pl.ds((4 * px + 2 * py + pc) * m_per, m_per), :]

        def copy(k, blk, to, src=None):
            return pltpu.make_async_remote_copy(src_ref=rows(*blk) if src is None else src, dst_ref=rows(*blk),
                                                send_sem=send_sems.at[k], recv_sem=recv_sems.at[k], device_id=to, device_id_type=MESH)

        mine = pltpu.make_async_copy(x_ref, rows(*me), local_sem)
        mine.start()
        first = [copy(0, me, sibling, src=x_ref)] + [copy(1 + j, me, (*chip, c), src=x_ref) for j, chip in enumerate(chips)]
        for cp in first:
            cp.start()
        passed = [copy(4 + j, (*chip, c), sibling) for j, chip in enumerate(chips)]
        for j, chip in enumerate(chips):
            copy(1 + j, (*chip, c), me).wait_recv()
            passed[j].start()
        copy(0, sibling, me).wait_recv()
        for j, chip in enumerate(chips):
            copy(4 + j, (*chip, 1 - c), me).wait_recv()
        for cp in first + passed:
            cp.wait_send()
        mine.wait()

    vmem = pl.BlockSpec(memory_space=pltpu.VMEM)
    return pl.pallas_call(body, name="gather_small", out_shape=jax.ShapeDtypeStruct((N_DEV * m_per, width), F32),
                          in_specs=[vmem], out_specs=vmem,
                          scratch_shapes=[pltpu.SemaphoreType.DMA((7,)), pltpu.SemaphoreType.DMA((7,)),
                                          pltpu.SemaphoreType.DMA])(block)


def _adamw_math(w, g, m, v):
    m = ADAM_B1 * m + (1.0 - ADAM_B1) * g
    v = ADAM_B2 * v + (1.0 - ADAM_B2) * (g * g)
    m_hat = m / (1.0 - ADAM_B1 ** ADAM_STEP)
    v_hat = v / (1.0 - ADAM_B2 ** ADAM_STEP)
    delta = -ADAM_LR * (m_hat / (jnp.sqrt(v_hat) + ADAM_EPS) + ADAM_WD * w)
    return delta, m, v


def _adamw(w, mine, siblings, place, m, v, transposed, name, riders=None):
    rh, width = mine.shape[1:]
    place_spec = pl.BlockSpec(memory_space=pltpu.SMEM)
    halves = [_const((1, rh, width))] * 2
    out_shape = [jax.ShapeDtypeStruct(w.shape, F32)] * 4
    if transposed:
        def body(place_ref, w_ref, mine_ref, sib_ref, m_ref, v_ref, go_ref, d_ref, mo_ref, vo_ref):
            first = place_ref[1] == 0
            g = jnp.concatenate([jnp.where(first, mine_ref[0], sib_ref[0]), jnp.where(first, sib_ref[0], mine_ref[0])], axis=0).T
            go_ref[...] = g
            d_ref[...], mo_ref[...], vo_ref[...] = _adamw_math(w_ref[...], g, m_ref[...], v_ref[...])

        whole = _resident(w.shape)
        return _hosted_call(body, riders, name=name, steps=1, in_specs=[place_spec, whole] + halves + [whole, whole],
                            out_specs=[whole] * 4, out_shape=out_shape, args=[place, w, mine, siblings, m, v])

    def body(place_ref, w_ref, mine_ref, sib_ref, m_ref, v_ref, go_ref, d_ref, mo_ref, vo_ref):
        g = jnp.where(pl.program_id(0) == place_ref[1], mine_ref[0], sib_ref[0])
        go_ref[...] = g
        d_ref[...], mo_ref[...], vo_ref[...] = _adamw_math(w_ref[...], g, m_ref[...], v_ref[...])

    half = _rows(rh, width)
    return _hosted_call(body, riders, name=name, steps=2, in_specs=[place_spec, half] + halves + [half, half],
                        out_specs=[half] * 4, out_shape=out_shape, args=[place, w, mine, siblings, m, v])


def _adamw_small(gathered, w, m, v, name):
    def body(ga_ref, w_ref, m_ref, v_ref, go_ref, d_ref, mo_ref, vo_ref):
        g = ga_ref[0]
        for dev in range(1, N_DEV):
            g = g + ga_ref[dev]
        go_ref[...] = g
        d_ref[...], mo_ref[...], vo_ref[...] = _adamw_math(w_ref[...], g, m_ref[...], v_ref[...])

    return pl.pallas_call(body, name=name, out_shape=[jax.ShapeDtypeStruct(w.shape, F32)] * 4,
                          compiler_params=_params())(gathered, w, m, v)


class _Exchange:
    FIRST = ("w1_gate", "w1_up", "w1_down")
    HOSTS = {"ffn2_wgrad_gate": (("w2_down",), ()), "ffn2_wgrad_up": (("w2_gate",), ("w2_down",)),
             "in_bwd": (("w2_up",), ("w2_gate",)), "wgrad_in": ((), ("w2_up",)),
             "ffn1_wgrad_down": ((), ("w_in",)), "ffn1_wgrad_gate": (("w1_down",), ()), "ffn1_wgrad_up": ((), ("w1_down", "w1_gate")),
             "wgrad_out": ((), ("w1_up",))}
    ALONE = ("w_in", "w1_gate", "w1_up", "w_out")

    def __init__(self, bufs, place):
        self.bufs, self.place = bufs, place
        self.later = [k for k in SEGMENTS if k not in self.FIRST]
        self.split, self.own, self.to_send, self.received = {}, {}, {}, {}

    def first_weights(self):
        return dict(zip(self.FIRST, _gather_weights([self.bufs[k] for k in self.FIRST])))

    def riders(self, host):
        if host == "ffn1_fwd":
            return [_gather_rider([self.bufs[k] for k in self.later])]
        if host == "in_fwd":
            return [_forward_rider([self.bufs[k] for k in self.later[1:]])]
        halves, sums = self.HOSTS.get(host, ((), ()))
        return ([_sibling_rider([self.split[k] for k in halves])] if halves else []) + (
            [_scatter_rider([self.to_send[k] for k in sums])] if sums else [])

    def landed(self, host, results):
        if host == "ffn1_fwd":
            self.bufs.update(zip(self.later, results[0]))
            return dict(zip(self.later[:1], _alone(_forward_rider([self.bufs[self.later[0]]]), "gather_forward_first")))
        if host == "in_fwd":
            return dict(zip(self.later[1:], results[0]))
        halves, sums = self.HOSTS.get(host, ((), ()))
        if halves:
            self._chip_sums(halves, results[0])
        if sums:
            self.received.update(zip(sums, results[-1]))

    def gradient(self, name, grad):
        self.split[name] = grad.reshape(N_CHIPS, 2, grad.shape[0] // (2 * N_CHIPS), grad.shape[1])
        if name in self.ALONE:
            self._chip_sums([name], _alone(_sibling_rider([self.split[name]]), f"reduce_sibling_{name}"))

    def _chip_sums(self, names, from_sibling):
        for k, fs in zip(names, from_sibling):
            self.own[k], self.to_send[k] = _chip_sum(self.split[k], fs, self.place, f"chip_sum_{k}")

    def summed_halves(self):
        late = [k for k in SEGMENTS if k not in self.received]
        self.received.update(zip(late, _alone(_scatter_rider([self.to_send[k] for k in late]), "reduce_chips_last")))
        mine, _ = _total_sums([self.own[k] for k in SEGMENTS], [self.received[k] for k in SEGMENTS], "total_sums")
        return mine, _alone(_swap_rider(mine), "swap_halves")


SMALL = ("g_ffn1_pre", "g_ffn1_post", "g_mix_pre", "w_pool_lin", "pool_scale", "g_mix_post", "g_ffn2_pre", "g_ffn2_post")
WEIGHTS = ("g_ffn1_pre", "w1_gate", "w1_up", "w1_down", "g_ffn1_post", "g_mix_pre", "w_in", "w_pool_lin", "pool_scale", "w_out",
           "g_mix_post", "g_ffn2_pre", "w2_gate", "w2_up", "w2_down", "g_ffn2_post")
LANES = 128


def _pack_small(tree, extra=0.0):
    flat = jnp.concatenate([tree[k].reshape(-1) for k in SMALL] + [jnp.reshape(extra, (1,)).astype(F32)])
    rows = -(-flat.shape[0] // (8 * LANES)) * 8
    return jnp.pad(flat, (0, rows * LANES - flat.shape[0])).reshape(rows, LANES)


def _unpack_small(packed, like):
    flat, out, at = packed.reshape(-1), {}, 0
    for k in SMALL:
        size = math.prod(like[k].shape)
        out[k] = flat[at:at + size].reshape(like[k].shape)
        at += size
    return out


def kernel(x, g_ffn1_pre, w1_gate, w1_up, w1_down, g_ffn1_post, g_mix_pre, w_in, w_pool_lin, pool_scale, w_out, g_mix_post, g_ffn2_pre, w2_gate, w2_up, w2_down, g_ffn2_post, loss_target, m_g_ffn1_pre, m_w1_gate, m_w1_up, m_w1_down, m_g_ffn1_post, m_g_mix_pre, m_w_in, m_w_pool_lin, m_pool_scale, m_w_out, m_g_mix_post, m_g_ffn2_pre, m_w2_gate, m_w2_up, m_w2_down, m_g_ffn2_post, v_g_ffn1_pre, v_w1_gate, v_w1_up, v_w1_down, v_g_ffn1_post, v_g_mix_pre, v_w_in, v_w_pool_lin, v_pool_scale, v_w_out, v_g_mix_post, v_g_ffn2_pre, v_w2_gate, v_w2_up, v_w2_down, v_g_ffn2_post):
    given = dict(locals())
    w = {k: given[k] for k in WEIGHTS}
    m = {k: given["m_" + k] for k in WEIGHTS}
    v = {k: given["v_" + k] for k in WEIGHTS}
    small = {k: (w[k][0] if k == "w_pool_lin" else w[k].reshape(1, -1)) for k in SMALL}

    place = jnp.stack([2 * lax.axis_index("x") + lax.axis_index("y"), lax.axis_index("c")]).astype(jnp.int32)
    def as_rows(a, k):
        return jnp.swapaxes(a, 1, 2)[0] if k in ROWS_OUTSIDE else a[0]

    def as_given(a, k):
        return jnp.swapaxes(a[None], 1, 2) if k in ROWS_OUTSIDE else a[None]

    in_kernel = [k for k in TRANSPOSED if k not in ROWS_OUTSIDE]
    bufs = {}
    for tag, names in (("first", _Exchange.FIRST), ("rest", [k for k in SEGMENTS if k not in _Exchange.FIRST])):
        bufs.update(zip(names, _cast_shards([as_rows(w[k], k) for k in names], [k in in_kernel for k in names], place, f"cast_{tag}")))
    exchange = _Exchange(bufs, place)
    loss_part, grad_x, small_grads = _local_step(x[0], loss_target[0], small, exchange)

    out_grad, out_delta, out_m, out_v = {}, {}, {}, {}
    for k, mine, siblings in zip(SEGMENTS, *exchange.summed_halves()):
        results, _ = _adamw(as_rows(w[k], k), mine, siblings, place, as_rows(m[k], k), as_rows(v[k], k), k in in_kernel, f"adamw_{k}")
        out_grad[k], out_delta[k], out_m[k], out_v[k] = (as_given(a, k) for a in results)

    small_grads["w_pool_lin"] = small_grads["w_pool_lin"][None]
    packed = _pack_small(small_grads, loss_part)
    gathered = _gather_small(packed).reshape(N_DEV, *packed.shape)
    like = {k: w[k] for k in SMALL}
    results = _adamw_small(gathered, _pack_small(like), _pack_small({k: m[k] for k in SMALL}),
                           _pack_small({k: v[k] for k in SMALL}), "adamw_small")
    for tree, res in zip((out_grad, out_delta, out_m, out_v), results):
        tree.update(_unpack_small(res, like))
    loss = results[0].reshape(-1)[sum(math.prod(like[k].shape) for k in SMALL)]

    return (loss, grad_x[None], *[out_grad[k] for k in WEIGHTS], *[out_delta[k] for k in WEIGHTS],
            *[out_m[k] for k in WEIGHTS], *[out_v[k] for k in WEIGHTS])
```

```python
import math
import typing

import numpy as np
import jax
import jax.numpy as jnp
from jax import lax
from jax.experimental import pallas as pl
from jax.experimental.pallas import tpu as pltpu

F32 = jnp.float32
BF16 = jnp.bfloat16
MESH = pl.DeviceIdType.MESH

RMS_EPS = 1e-6
HEAD_DIM = 64
POOL_HALF_WINDOWS = (1, 2, 4, 8)
POOL_DIM = 256
GROUP_DIM = 256
DILATIONS = (1, 4, 16)
N_SIDE = 64
N_ATTN_HEADS = 12
ADAM_LR, ADAM_B1, ADAM_B2, ADAM_EPS, ADAM_WD, ADAM_STEP = 0.001, 0.9, 0.999, 1e-08, 0.01, 10

N_CHIPS = 4
V7X_VMEM_LIMIT = 60 * 1024 * 1024

_NT = (((1,), (1,)), ((), ()))
_TN = (((0,), (0,)), ((), ()))


def _dot(a, b):
    return jnp.dot(a, b, preferred_element_type=F32)


def _dot_nt(a, b):
    return lax.dot_general(a, b, _NT, preferred_element_type=F32)


def _dot_tn(a, b):
    return lax.dot_general(a, b, _TN, preferred_element_type=F32)


def _params(**kw):
    return pltpu.CompilerParams(vmem_limit_bytes=V7X_VMEM_LIMIT, **kw)


def _rows(tm, width):
    return pl.BlockSpec((tm, width), lambda i: (i, 0))


def _resident(shape):
    return pl.BlockSpec(shape, lambda i: (0,) * len(shape), pipeline_mode=pl.Buffered(1))


def _const(shape):
    return pl.BlockSpec(shape, lambda i: (0,) * len(shape))


def _inv_rms(x):
    return lax.rsqrt(jnp.mean(x * x, axis=-1, keepdims=True) + RMS_EPS)


def _rms_bwd(x, inv, g, dy):
    n = x * inv
    dn = dy * g
    dx = inv * (dn - n * jnp.mean(dn * n, axis=-1, keepdims=True))
    return dx, jnp.sum(dy * n, axis=0, keepdims=True)


def _accumulate(ref, value):
    @pl.when(pl.program_id(0) == 0)
    def _():
        ref[...] = jnp.zeros_like(ref)

    ref[...] += value


class _Rider(typing.NamedTuple):
    operands: list
    landing: typing.Optional[list]
    sems: tuple
    start: typing.Callable
    wait: typing.Callable


def _hosted_call(body, riders, *, name, steps, in_specs, out_specs, out_shape, args, scratch_shapes=()):
    params = _params(dimension_semantics=("arbitrary",))
    riders = list(riders or [])
    if not riders:
        res = pl.pallas_call(body, name=name, grid=(steps,), in_specs=in_specs, out_specs=out_specs, out_shape=out_shape,
                             scratch_shapes=list(scratch_shapes), compiler_params=params)(*args)
        return list(res), []
    n_in, n_out, n_scratch = len(in_specs), len(out_specs), len(scratch_shapes)
    operands, landing, aliases, spans = [], [], {}, []
    for rd in riders:
        lands = rd.landing if rd.landing is not None else [jax.ShapeDtypeStruct(a.shape, a.dtype) for a in rd.operands]
        if rd.landing is None:
            aliases.update({n_in + len(operands) + i: n_out + len(landing) + i for i in range(len(lands))})
        spans.append((len(operands), len(rd.operands), len(landing), len(lands)))
        operands += rd.operands
        landing += lands
    outs_at = n_in + len(operands)
    scratch_at = outs_at + n_out + len(landing)

    def riding(*refs):
        def each(action):
            for i, (rd, (in_at, n_ops, out_at, n_lands)) in enumerate(zip(riders, spans)):
                sems = refs[scratch_at + n_scratch + 2 * i:scratch_at + n_scratch + 2 * i + 2]
                getattr(rd, action)(refs[n_in + in_at:n_in + in_at + n_ops],
                                    refs[outs_at + n_out + out_at:outs_at + n_out + out_at + n_lands], *sems)

        @pl.when(pl.program_id(0) == 0)
        def _():
            each("start")

        body(*refs[:n_in], *refs[outs_at:outs_at + n_out], *refs[scratch_at:scratch_at + n_scratch])

        @pl.when(pl.program_id(0) == steps - 1)
        def _():
            each("wait")

    any_spec = pl.BlockSpec(memory_space=pl.ANY)
    res = pl.pallas_call(
        riding, name=name, grid=(steps,), in_specs=list(in_specs) + [any_spec] * len(operands),
        out_specs=list(out_specs) + [any_spec] * len(landing), out_shape=list(out_shape) + landing,
        scratch_shapes=list(scratch_shapes) + [pltpu.SemaphoreType.DMA(rd.sems) for rd in riders for _ in range(2)],
        input_output_aliases=aliases, compiler_params=params)(*args, *operands)
    return list(res[:n_out]), [list(res[n_out + out_at:n_out + out_at + n_lands]) for _, _, out_at, n_lands in spans]


_SUB_TILE = 256


def _sub_tiles(tm):
    return [pl.ds(r, _SUB_TILE) for r in range(0, tm, _SUB_TILE)]


def _ffn_fwd(x, g_pre, wg_t, wu_t, wd, g_post, target, name, riders=None, tm=512):
    s, d = x.shape
    ff = wd.shape[0]
    with_loss = target is not None

    def body(*refs):
        if with_loss:
            x_ref, gpre_ref, wg_ref, wu_ref, wd_ref, gpost_ref, t_ref, xo_ref, a_ref, b_ref, f_ref, loss_ref = refs
        else:
            x_ref, gpre_ref, wg_ref, wu_ref, wd_ref, gpost_ref, xo_ref, a_ref, b_ref, f_ref = refs
        loss = 0.0
        for rows in _sub_tiles(tm):
            xv = x_ref[rows, :]
            hb = (xv * _inv_rms(xv) * gpre_ref[...]).astype(BF16)
            a = _dot_nt(hb, wg_ref[...])
            b = _dot_nt(hb, wu_ref[...])
            hh = (a * jax.nn.sigmoid(a)) * b
            f = _dot(hh.astype(BF16), wd_ref[...])
            xo = xv + 0.5 * (f * _inv_rms(f) * gpost_ref[...])
            a_ref[rows, :] = a.astype(BF16)
            b_ref[rows, :] = b.astype(BF16)
            f_ref[rows, :] = f
            if with_loss:
                e = xo - t_ref[rows, :]
                xo_ref[rows, :] = e * (1.0 / d)
                loss = loss + 0.5 * jnp.sum(jnp.mean(e * e, axis=-1, keepdims=True))
            else:
                xo_ref[rows, :] = xo
        if with_loss:
            _accumulate(loss_ref, loss)

    in_specs = [_rows(tm, d), _const((1, d)), _resident((ff, d)), _resident((ff, d)), _resident((ff, d)), _const((1, d))]
    args = [x, g_pre, wg_t, wu_t, wd, g_post]
    out_shape = [jax.ShapeDtypeStruct((s, d), F32), jax.ShapeDtypeStruct((s, ff), BF16),
                 jax.ShapeDtypeStruct((s, ff), BF16), jax.ShapeDtypeStruct((s, d), F32)]
    out_specs = [_rows(tm, d), _rows(tm, ff), _rows(tm, ff), _rows(tm, d)]
    if with_loss:
        in_specs.append(_rows(tm, d))
        args.append(target)
        out_shape.append(jax.ShapeDtypeStruct((8, 128), F32))
        out_specs.append(_const((8, 128)))
    return _hosted_call(body, riders, name=name, steps=s // tm, in_specs=in_specs, out_specs=out_specs, out_shape=out_shape, args=args)


def _ffn_bwd(dxo, x, f, a, b, g_pre, g_post, wg_t, wu_t, wd, name, riders=None, tm=256):
    s, d = x.shape
    ff = wd.shape[0]

    def body(dxo_ref, x_ref, f_ref, a_ref, b_ref, gpre_ref, gpost_ref, wg_ref, wu_ref, wd_ref,
             dx_ref, hh_ref, da_ref, db_ref, df_ref, h_ref, dgpre_ref, dgpost_ref):
        dgpre_sum = dgpost_sum = 0.0
        for rows in _sub_tiles(tm):
            dxo_v = dxo_ref[rows, :]
            fv = f_ref[rows, :]
            df, dgpost = _rms_bwd(fv, _inv_rms(fv), gpost_ref[...], 0.5 * dxo_v)
            dfb = df.astype(BF16)
            dhh = _dot_nt(dfb, wd_ref[...])
            av = a_ref[rows, :].astype(F32)
            bv = b_ref[rows, :].astype(F32)
            sig = jax.nn.sigmoid(av)
            sa = av * sig
            da = (dhh * bv * (sig * (1.0 + av * (1.0 - sig)))).astype(BF16)
            db = (dhh * sa).astype(BF16)
            dh = _dot(da, wg_ref[...]) + _dot(db, wu_ref[...])
            xv = x_ref[rows, :]
            inv = _inv_rms(xv)
            dxn, dgpre = _rms_bwd(xv, inv, gpre_ref[...], dh)
            dx_ref[rows, :] = dxo_v + dxn
            hh_ref[rows, :] = (sa * bv).astype(BF16)
            da_ref[rows, :] = da
            db_ref[rows, :] = db
            df_ref[rows, :] = dfb
            h_ref[rows, :] = (xv * inv * gpre_ref[...]).astype(BF16)
            dgpre_sum, dgpost_sum = dgpre_sum + dgpre, dgpost_sum + dgpost
        _accumulate(dgpre_ref, dgpre_sum)
        _accumulate(dgpost_ref, dgpost_sum)

    return _hosted_call(
        body, riders, name=name, steps=s // tm,
        in_specs=[_rows(tm, d), _rows(tm, d), _rows(tm, d), _rows(tm, ff), _rows(tm, ff), _const((1, d)), _const((1, d)),
                  _resident((ff, d)), _resident((ff, d)), _resident((ff, d))],
        out_specs=[_rows(tm, d), _rows(tm, ff), _rows(tm, ff), _rows(tm, ff), _rows(tm, d), _rows(tm, d),
                   _const((1, d)), _const((1, d))],
        out_shape=[jax.ShapeDtypeStruct((s, d), F32), jax.ShapeDtypeStruct((s, ff), BF16), jax.ShapeDtypeStruct((s, ff), BF16),
                   jax.ShapeDtypeStruct((s, ff), BF16), jax.ShapeDtypeStruct((s, d), BF16), jax.ShapeDtypeStruct((s, d), BF16),
                   jax.ShapeDtypeStruct((1, d), F32), jax.ShapeDtypeStruct((1, d), F32)],
        args=[dxo, x, f, a, b, g_pre, g_post, wg_t, wu_t, wd])


def _wgrad(lhs, rhs, name, riders=None, rt=256):
    s, r = lhs.shape
    c = rhs.shape[1]

    def body(l_ref, r_ref, o_ref):
        o_ref[...] = _dot_tn(l_ref[...], r_ref[...])

    (out,), riding = _hosted_call(
        body, riders, name=name, steps=pl.cdiv(r, rt), in_specs=[pl.BlockSpec((s, rt), lambda i: (0, i)), _resident((s, c))],
        out_specs=[pl.BlockSpec((rt, c), lambda i: (i, 0))], out_shape=[jax.ShapeDtypeStruct((r, c), F32)], args=[lhs, rhs])
    return out, riding


def _attn_dtype(dilation):
    return BF16 if dilation == 1 else F32


def _in_fwd(x, g, w_in_t, name, riders=None, tm=1024):
    s, d = x.shape
    d_in = w_in_t.shape[0]
    n_groups = len(DILATIONS)
    dtypes = [_attn_dtype(dil) for dil in DILATIONS] * 3

    def body(x_ref, g_ref, w_ref, h_ref, u_ref, *part_refs):
        xv = x_ref[...]
        hb = (xv * _inv_rms(xv) * g_ref[...]).astype(BF16)
        h_ref[...] = hb
        z = _dot_nt(hb, w_ref[...])
        u_ref[...] = z[:, :POOL_DIM]
        for j, ref in enumerate(part_refs):
            part = z[:, POOL_DIM + GROUP_DIM * j:POOL_DIM + GROUP_DIM * (j + 1)]
            ref[...] = (part * _SCORE_SCALE if j < n_groups else part).astype(ref.dtype)

    return _hosted_call(
        body, riders, name=name, steps=s // tm, in_specs=[_rows(tm, d), _const((1, d)), _resident((d_in, d))],
        out_specs=[_rows(tm, d), _rows(tm, POOL_DIM)] + [_rows(tm, GROUP_DIM)] * len(dtypes),
        out_shape=[jax.ShapeDtypeStruct((s, d), BF16), jax.ShapeDtypeStruct((s, POOL_DIM), F32)]
        + [jax.ShapeDtypeStruct((s, GROUP_DIM), dt) for dt in dtypes],
        args=[x, g, w_in_t])


def _in_bwd(du, dparts, x, dxo, g, w_in_t, name, riders=None, tm=512):
    s, d = x.shape
    d_in = w_in_t.shape[0]
    n_parts = len(dparts)

    def body(du_ref, *refs):
        part_refs = refs[:n_parts]
        x_ref, dxo_ref, g_ref, w_ref, dx_ref, dz_ref, dg_ref = refs[n_parts:]
        dz = jnp.concatenate([r[...].astype(BF16) for r in (du_ref,) + part_refs], axis=1)
        dz_ref[...] = dz
        dh = _dot(dz, w_ref[...])
        xv = x_ref[...]
        dxn, dg = _rms_bwd(xv, _inv_rms(xv), g_ref[...], dh)
        dx_ref[...] = dxo_ref[...] + dxn
        _accumulate(dg_ref, dg)

    return _hosted_call(
        body, riders, name=name, steps=s // tm,
        in_specs=[_rows(tm, POOL_DIM)] + [_rows(tm, GROUP_DIM)] * n_parts + [_rows(tm, d), _rows(tm, d), _const((1, d)),
                                                                             _resident((d_in, d))],
        out_specs=[_rows(tm, d), _rows(tm, d_in), _const((1, d))],
        out_shape=[jax.ShapeDtypeStruct((s, d), F32), jax.ShapeDtypeStruct((s, d_in), BF16), jax.ShapeDtypeStruct((1, d), F32)],
        args=[du, *dparts, x, dxo, g, w_in_t])


_POOL_HALO = 8


def _pool_chain(v, first_shift):
    n = v.shape[0]
    p2 = v + pltpu.roll(v, first_shift, 0)
    p4 = pltpu.roll(p2, 1, 0) + pltpu.roll(p2, n - 1, 0)
    p8 = pltpu.roll(p4, 2, 0) + pltpu.roll(p4, n - 2, 0)
    p16 = pltpu.roll(p8, 4, 0) + pltpu.roll(p8, n - 4, 0)
    group = lax.broadcasted_iota(jnp.int32, v.shape, 1) // HEAD_DIM
    return jnp.where(group == 0, p2, jnp.where(group == 1, p4, jnp.where(group == 2, p8, p16)))


def _pool_count(t0, rows, s):
    t = t0 + lax.broadcasted_iota(jnp.int32, (rows, POOL_DIM), 0)
    group = lax.broadcasted_iota(jnp.int32, (rows, POOL_DIM), 1) // HEAD_DIM
    half = jnp.where(group == 0, 1, jnp.where(group == 1, 2, jnp.where(group == 2, 4, 8)))
    cnt = jnp.minimum(t + half, s) - jnp.maximum(t - half, 0)
    return jnp.maximum(cnt, 1).astype(F32)


def _pad_rows(ref, pad_ref, s):
    zeros = jnp.zeros((_POOL_HALO, pad_ref.shape[1]), pad_ref.dtype)
    pad_ref[pl.ds(0, _POOL_HALO), :] = zeros
    pad_ref[pl.ds(_POOL_HALO + s, _POOL_HALO), :] = zeros
    pad_ref[pl.ds(_POOL_HALO, s), :] = ref[...]


def _pool_fwd(u, w_bd, scale, name, tm=512):
    s = u.shape[0]
    ext = tm + 2 * _POOL_HALO

    def body(u_ref, w_ref, sc_ref, o_ref, upad):
        _pad_rows(u_ref, upad, s)

        def tile(i, carry):
            t0 = pl.multiple_of(i * tm, tm)
            uv = upad[pl.ds(t0, ext), :]
            win = _pool_chain(uv, 1)[_POOL_HALO:_POOL_HALO + tm]
            y = win / _pool_count(t0, tm, s) - uv[_POOL_HALO:_POOL_HALO + tm]
            o_ref[pl.ds(t0, tm), :] = (_dot(y.astype(BF16), w_ref[...]) * sc_ref[...]).astype(BF16)
            return carry

        lax.fori_loop(0, s // tm, tile, 0)

    return pl.pallas_call(body, name=name, out_shape=jax.ShapeDtypeStruct((s, POOL_DIM), BF16),
                          scratch_shapes=[pltpu.VMEM((s + 2 * _POOL_HALO, POOL_DIM), F32)],
                          compiler_params=_params())(u, w_bd, scale)


def _pool_bwd(u, da, w_bd, scale, name, tm=512):
    s = u.shape[0]
    ext = tm + 2 * _POOL_HALO

    def body(u_ref, da_ref, w_ref, sc_ref, du_ref, dw_ref, dsc_ref, upad, dapad):
        _pad_rows(u_ref, upad, s)
        _pad_rows(da_ref, dapad, s)
        dw_ref[...] = jnp.zeros_like(dw_ref)
        dsc_ref[...] = jnp.zeros_like(dsc_ref)

        def tile(i, carry):
            t0 = pl.multiple_of(i * tm, tm)
            uv = upad[pl.ds(t0, ext), :]
            dav = dapad[pl.ds(t0, ext), :]
            win = _pool_chain(uv, 1)[_POOL_HALO:_POOL_HALO + tm]
            yb = (win / _pool_count(t0, tm, s) - uv[_POOL_HALO:_POOL_HALO + tm]).astype(BF16)
            yl = _dot(yb, w_ref[...])
            da_c = dav[_POOL_HALO:_POOL_HALO + tm]
            dsc_ref[...] += jnp.sum(da_c * yl, axis=0, keepdims=True)
            dyl = (dav * sc_ref[...]).astype(BF16)
            dw_ref[...] += _dot_tn(yb, dyl[_POOL_HALO:_POOL_HALO + tm])
            dy = _dot_nt(dyl, w_ref[...])
            dyc = dy / _pool_count(t0 - _POOL_HALO, ext, s)
            du_ref[pl.ds(t0, tm), :] = (_pool_chain(dyc, ext - 1) - dy)[_POOL_HALO:_POOL_HALO + tm].astype(BF16)
            return carry

        lax.fori_loop(0, s // tm, tile, 0)

    pool_cols = pl.BlockSpec((s, POOL_DIM), lambda i: (0, 0), pipeline_mode=pl.Buffered(1))
    return pl.pallas_call(
        body, name=name, grid=(1,),
        in_specs=[pool_cols, pool_cols, _const((POOL_DIM, POOL_DIM)), _const((1, POOL_DIM))],
        out_specs=[_const((s, POOL_DIM)), _const((POOL_DIM, POOL_DIM)), _const((1, POOL_DIM))],
        out_shape=[jax.ShapeDtypeStruct((s, POOL_DIM), BF16), jax.ShapeDtypeStruct((POOL_DIM, POOL_DIM), F32),
                   jax.ShapeDtypeStruct((1, POOL_DIM), F32)],
        scratch_shapes=[pltpu.VMEM((s + 2 * _POOL_HALO, POOL_DIM), F32), pltpu.VMEM((s + 2 * _POOL_HALO, POOL_DIM), F32)],
        compiler_params=_params(dimension_semantics=("arbitrary",)))(u, da, w_bd, scale)


_BQ = 128
_KW = _BQ + 2 * N_SIDE
_PAIR = 2 * HEAD_DIM
_NEG = -1e30
_ATTN_UNROLL = 8
_SCORE_SCALE = HEAD_DIM ** -0.5


def _stack_heads(x):
    lane_head = lax.broadcasted_iota(jnp.int32, x.shape, 1) // HEAD_DIM
    zero = jnp.zeros_like(x)
    return jnp.concatenate([jnp.where(lane_head == 0, x, zero), jnp.where(lane_head == 1, x, zero)], axis=0)


def _unstack_heads(x):
    lane_head = lax.broadcasted_iota(jnp.int32, (_BQ, _PAIR), 1) // HEAD_DIM
    return jnp.where(lane_head == 0, x[:_BQ], x[_BQ:])


def _stack_cols(x):
    return jnp.concatenate([x[:, 0:1], x[:, HEAD_DIM:HEAD_DIM + 1]], axis=0)


def _fill_bias(bias_ref, slopes_ref, dilation):
    row = lax.broadcasted_iota(jnp.int32, (2 * _BQ, _KW), 0)
    col = lax.broadcasted_iota(jnp.int32, (2 * _BQ, _KW), 1)
    pair = 2 * pl.program_id(0)
    slope = jnp.where(row < _BQ, slopes_ref[pair], slopes_ref[pair + 1]) * float(dilation)

    @pl.when(pl.program_id(1) == 0)
    def _():
        for j in range(3):
            dist = jnp.abs(col - (row & (_BQ - 1)) - j * N_SIDE)
            bias_ref[j] = jnp.where(dist <= N_SIDE, -slope * dist.astype(F32), _NEG)


def _block_window(i, n_blocks, length):
    q0 = pl.multiple_of(i * _BQ, _BQ)
    ws = pl.multiple_of(jnp.clip(q0 - N_SIDE, 0, length - _KW), N_SIDE)
    return q0, ws, jnp.where(i == 0, 0, jnp.where(i == n_blocks - 1, 2, 1))


_FREE_STRIDE = 4


def _residue_views(dilation, seq, ins, outs, tmps):
    step = pl.program_id(1)
    if dilation <= _FREE_STRIDE:
        def rows(start, count):
            return pl.ds(start, count) if dilation == 1 else pl.ds(start * dilation + step, count, stride=dilation)

        return ins, outs, rows, lambda: None
    inner = dilation // _FREE_STRIDE
    assert inner <= _FREE_STRIDE and len(tmps) == len(ins) + len(outs)
    first, second = step // inner, step % inner
    coarse = pl.ds(first, seq // _FREE_STRIDE, stride=_FREE_STRIDE)
    in_tmps, out_tmps = tmps[:len(ins)], tmps[len(ins):]

    @pl.when(second == 0)
    def _():
        for ref, tmp in zip(ins, in_tmps):
            tmp[...] = ref[coarse, :]

    def flush():
        @pl.when(second == inner - 1)
        def _():
            for ref, tmp in zip(outs, out_tmps):
                ref[coarse, :] = tmp[...]

    return in_tmps, out_tmps, lambda start, count: pl.ds(start * inner + second, count, stride=inner), flush


def _attn_call(body, name, dilation, seq, n_in, out_dtypes, scratch, buffers):
    col = pl.BlockSpec((seq, _PAIR), lambda c, r: (0, c), pipeline_mode=pl.Buffered(buffers))
    tmps = [pltpu.VMEM((seq // _FREE_STRIDE, _PAIR), F32)] * (n_in + len(out_dtypes) if dilation > _FREE_STRIDE else 0)
    return pl.pallas_call(
        body, name=name, grid=(GROUP_DIM // _PAIR, dilation),
        in_specs=[pl.BlockSpec(memory_space=pltpu.SMEM)] + [col] * n_in, out_specs=[col] * len(out_dtypes),
        out_shape=[jax.ShapeDtypeStruct((seq, GROUP_DIM), dt) for dt in out_dtypes], scratch_shapes=scratch + tmps,
        compiler_params=_params(dimension_semantics=("arbitrary", "arbitrary")))


def _staged(dilation, length, rows, sources, scratch):
    if dilation == 1:
        return sources
    for src, dst in zip(sources, scratch):
        dst[...] = src[rows(0, length), :].astype(BF16)
    return scratch


def _attn_fwd(q, k, v, slopes, dilation, name):
    seq = q.shape[0]
    length = seq // dilation
    n_blocks = length // _BQ
    n_stage = 0 if dilation == 1 else 3

    def body(sl_ref, q_ref, k_ref, v_ref, o_ref, lse_ref, *scratch):
        bias_ref, tmps = scratch[n_stage], scratch[n_stage + 1:]
        (q_in, k_in, v_in), (o_out, lse_out), rows, flush = _residue_views(dilation, seq, (q_ref, k_ref, v_ref), (o_ref, lse_ref), tmps)
        qs, ks, vs = _staged(dilation, length, rows, (q_in, k_in, v_in), scratch[:n_stage])
        _fill_bias(bias_ref, sl_ref, dilation)

        def block(i, carry):
            q0, ws, which = _block_window(i, n_blocks, length)
            kw = ks[pl.ds(ws, _KW), :]
            vw = vs[pl.ds(ws, _KW), :]
            sc = _dot_nt(_stack_heads(qs[pl.ds(q0, _BQ), :]), kw) + bias_ref[which]
            m = jnp.max(sc, axis=-1, keepdims=True)
            p = jnp.exp(sc - m)
            den = jnp.sum(p, axis=-1, keepdims=True)
            o_out[rows(q0, _BQ), :] = _unstack_heads(_dot(p.astype(BF16), vw) / den)
            lse_out[rows(q0, _BQ), :] = _unstack_heads(jnp.broadcast_to(m + jnp.log(den), (2 * _BQ, _PAIR)))
            return carry

        lax.fori_loop(0, n_blocks, block, 0, unroll=min(_ATTN_UNROLL, n_blocks))
        flush()

    stage = pltpu.VMEM((length, _PAIR), BF16)
    bias = pltpu.VMEM((3, 2 * _BQ, _KW), F32)
    return _attn_call(body, name, dilation, seq, 3, [F32, F32], [stage] * n_stage + [bias], 2)(slopes, q, k, v)


def _attn_bwd(q, k, v, do, lse, cterm, slopes, dilation, name):
    seq = q.shape[0]
    length = seq // dilation
    n_blocks = length // _BQ
    n_stage, n_whole = (0, 0) if dilation == 1 else (4, 3)

    def body(sl_ref, q_ref, k_ref, v_ref, do_ref, lse_ref, c_ref, dq_ref, dk_ref, dv_ref, *scratch):
        dk_acc, dv_acc, bias_ref = scratch[n_stage:n_stage + 3]
        whole, tmps = scratch[n_stage + 3:n_stage + 3 + n_whole], scratch[n_stage + 3 + n_whole:]
        (q_in, k_in, v_in, do_in, lse_in, c_in), (dq_out, dk_out, dv_out), rows, flush = _residue_views(
            dilation, seq, (q_ref, k_ref, v_ref, do_ref, lse_ref, c_ref), whole or (dq_ref, dk_ref, dv_ref), tmps)
        all_rows = rows(0, length)
        qs, ks, vs, dos = _staged(dilation, length, rows, (q_in, k_in, v_in, do_in), scratch[:n_stage])
        dk_acc[...] = jnp.zeros_like(dk_acc)
        dv_acc[...] = jnp.zeros_like(dv_acc)
        _fill_bias(bias_ref, sl_ref, dilation)

        def block(i, carry):
            q0, ws, which = _block_window(i, n_blocks, length)
            qm = _stack_heads(qs[pl.ds(q0, _BQ), :])
            dom = _stack_heads(dos[pl.ds(q0, _BQ), :])
            kw = ks[pl.ds(ws, _KW), :]
            vw = vs[pl.ds(ws, _KW), :]
            p = jnp.exp(_dot_nt(qm, kw) + bias_ref[which] - _stack_cols(lse_in[rows(q0, _BQ), :]))
            ds = (p * (_dot_nt(dom, vw) + _stack_cols(c_in[rows(q0, _BQ), :]))).astype(BF16)
            dq_out[rows(q0, _BQ), :] = (_unstack_heads(_dot(ds, kw)) * _SCORE_SCALE).astype(dq_out.dtype)
            dk_acc[pl.ds(ws, _KW), :] += _dot_tn(ds, qm)
            dv_acc[pl.ds(ws, _KW), :] += _dot_tn(p.astype(BF16), dom)
            return carry

        lax.fori_loop(0, n_blocks, block, 0, unroll=min(_ATTN_UNROLL, n_blocks))
        dk_out[all_rows, :] = dk_acc[...].astype(dk_out.dtype)
        dv_out[all_rows, :] = dv_acc[...].astype(dv_out.dtype)
        flush()
        if whole:
            @pl.when(pl.program_id(1) == dilation - 1)
            def _():
                for ref, collected in zip((dq_ref, dk_ref, dv_ref), whole):
                    ref[...] = collected[...].astype(BF16)

    stage = pltpu.VMEM((length, _PAIR), BF16)
    acc = pltpu.VMEM((length, _PAIR), F32)
    bias = pltpu.VMEM((3, 2 * _BQ, _KW), F32)
    collect = pltpu.VMEM((seq, _PAIR), F32)
    return _attn_call(body, name, dilation, seq, 6, [BF16] * 3, [stage] * n_stage + [acc] * 2 + [bias] + [collect] * n_whole,
                      2 if dilation == 1 else 1)(slopes, q, k, v, do, lse, cterm)


def _group_weights(lses):
    m = jnp.maximum(jnp.maximum(lses[0], lses[1]), lses[2])
    es = [jnp.exp(l - m) for l in lses]
    den = es[0] + es[1] + es[2]
    return [e / den for e in es]


def _out_fwd(a_pool, outs, lses, x, w_out, g, name, tm=1024):
    s, d = x.shape
    width = POOL_DIM + 3 * GROUP_DIM

    def body(ap_ref, o0, o1, o2, l0, l1, l2, x_ref, w_ref, g_ref, xo_ref, cat_ref):
        alphas = _group_weights([l0[...], l1[...], l2[...]])
        cat = jnp.concatenate([ap_ref[...]] + [(o[...] * al).astype(BF16) for o, al in zip((o0, o1, o2), alphas)], axis=1)
        cat_ref[...] = cat
        mix = _dot(cat, w_ref[...])
        xo_ref[...] = x_ref[...] + mix * _inv_rms(mix) * g_ref[...]

    return pl.pallas_call(
        body, name=name, grid=(s // tm,),
        in_specs=[_rows(tm, POOL_DIM)] + [_rows(tm, GROUP_DIM)] * 6 + [_rows(tm, d), _resident(w_out.shape), _const((1, d))],
        out_specs=[_rows(tm, d), _rows(tm, width)],
        out_shape=[jax.ShapeDtypeStruct((s, d), F32), jax.ShapeDtypeStruct((s, width), BF16)],
        compiler_params=_params(dimension_semantics=("arbitrary",)))(a_pool, *outs, *lses, x, w_out, g)


def _out_bwd(dxo, cat, outs, lses, w_out, g, head_ones, name, tm=1024):
    s, d = dxo.shape

    def body(dxo_ref, cat_ref, o0, o1, o2, l0, l1, l2, w_ref, g_ref, ones_ref, dpool_ref, dmix_ref, do0, do1, do2, c0, c1, c2, dg_ref):
        mv = _dot(cat_ref[...], w_ref[...])
        dmix, dg = _rms_bwd(mv, _inv_rms(mv), g_ref[...], dxo_ref[...])
        dmb = dmix.astype(BF16)
        dmix_ref[...] = dmb
        _accumulate(dg_ref, dg)
        dcat = _dot_nt(dmb, w_ref[...])
        dpool_ref[...] = dcat[:, :POOL_DIM]
        alphas = _group_weights([l0[...], l1[...], l2[...]])
        das = [dcat[:, POOL_DIM + GROUP_DIM * j:POOL_DIM + GROUP_DIM * (j + 1)] for j in range(3)]
        prod = sum(da * (o[...] * al) for da, o, al in zip(das, (o0, o1, o2), alphas))
        hi = prod.astype(BF16)
        lo = (prod - hi.astype(F32)).astype(BF16)
        total = _dot(hi, ones_ref[...]) + _dot(lo, ones_ref[...])
        for da, al, do_ref, c_ref in zip(das, alphas, (do0, do1, do2), (c0, c1, c2)):
            do_ref[...] = (da * al).astype(do_ref.dtype)
            c_ref[...] = -al * total

    return pl.pallas_call(
        body, name=name, grid=(s // tm,),
        in_specs=[_rows(tm, d), _rows(tm, cat.shape[1])] + [_rows(tm, GROUP_DIM)] * 6 + [_resident(w_out.shape), _const((1, d)),
                                                                                        _const((GROUP_DIM, GROUP_DIM))],
        out_specs=[_rows(tm, POOL_DIM), _rows(tm, d)] + [_rows(tm, GROUP_DIM)] * 6 + [_const((1, d))],
        out_shape=[jax.ShapeDtypeStruct((s, POOL_DIM), F32), jax.ShapeDtypeStruct((s, d), BF16)]
        + [jax.ShapeDtypeStruct((s, GROUP_DIM), _attn_dtype(dil)) for dil in DILATIONS]
        + [jax.ShapeDtypeStruct((s, GROUP_DIM), F32)] * 3 + [jax.ShapeDtypeStruct((1, d), F32)],
        compiler_params=_params(dimension_semantics=("arbitrary",)))(dxo, cat, *outs, *lses, w_out, g, head_ones)


def _alibi_slopes():
    return np.array([2.0 ** (-8.0 * (i + 1) / N_ATTN_HEADS) for i in range(N_ATTN_HEADS)], np.float32)


def _block_diag(w_lin):
    n, c, _ = w_lin.shape
    eye = jnp.eye(n, dtype=w_lin.dtype)
    return (eye[:, None, :, None] * w_lin[:, :, None, :]).reshape(n * c, n * c)


class _NoExchange:
    def __init__(self, full):
        self.full, self.grads = full, {}

    def first_weights(self):
        return self.full

    def riders(self, host):
        return []

    def landed(self, host, results):
        return self.full

    def gradient(self, name, grad):
        self.grads[name] = grad


def _local_step(x, target, small, exchange):
    s, d = x.shape
    slopes = _alibi_slopes()
    group_slopes = [jnp.asarray(slopes[4 * g:4 * g + 4]) for g in range(3)]
    w_bd = _block_diag(small["w_pool_lin"]).astype(BF16)
    head_ones = jnp.asarray(np.kron(np.eye(GROUP_DIM // HEAD_DIM), np.ones((HEAD_DIM, HEAD_DIM))), BF16)

    full = dict(exchange.first_weights())

    def hosted(call, host, *args):
        results, riding = call(*args, host, exchange.riders(host))
        full.update(exchange.landed(host, riding) or {})
        return results

    x1, a1, b1, f1 = hosted(_ffn_fwd, "ffn1_fwd", x, small["g_ffn1_pre"], full["w1_gate"], full["w1_up"], full["w1_down"],
                            small["g_ffn1_post"], None)
    h2, u, *parts = hosted(_in_fwd, "in_fwd", x1, small["g_mix_pre"], full["w_in"])
    qs, ks, vs = parts[0:3], parts[3:6], parts[6:9]
    a_pool = _pool_fwd(u, w_bd, small["pool_scale"], "pool_fwd")
    outs, lses = [], []
    for g, dil in enumerate(DILATIONS):
        o, lse = _attn_fwd(qs[g], ks[g], vs[g], group_slopes[g], dil, f"attn_fwd{g}")
        outs.append(o)
        lses.append(lse)
    x2, cat = _out_fwd(a_pool, outs, lses, x1, full["w_out"], small["g_mix_post"], "out_fwd")
    (dx3, a2, b2, f2, loss_part), _ = _ffn_fwd(x2, small["g_ffn2_pre"], full["w2_gate"], full["w2_up"], full["w2_down"],
                                               small["g_ffn2_post"], target, "ffn2_fwd")

    small_grads = {}

    def ffn_backward(tag, dxo, x_in, f, a, b):
        n = tag[-1]
        dx, hh, da, db, df, h, dg_pre, dg_post = hosted(
            _ffn_bwd, f"{tag}_bwd", dxo, x_in, f, a, b, small[f"g_{tag}_pre"], small[f"g_{tag}_post"],
            full[f"w{n}_gate"], full[f"w{n}_up"], full[f"w{n}_down"])
        for part, lhs, rhs in (("down", hh, df), ("gate", da, h), ("up", db, h)):
            exchange.gradient(f"w{n}_{part}", hosted(_wgrad, f"{tag}_wgrad_{part}", lhs, rhs))
        small_grads[f"g_{tag}_pre"], small_grads[f"g_{tag}_post"] = dg_pre, dg_post
        return dx

    dx2 = ffn_backward("ffn2", dx3, x2, f2, a2, b2)
    dpool, dmix, *dos_cs, small_grads["g_mix_post"] = _out_bwd(dx2, cat, outs, lses, full["w_out"], small["g_mix_post"],
                                                               head_ones, "out_bwd")
    dos, cs = dos_cs[:3], dos_cs[3:]
    dqs, dks, dvs = [], [], []
    for g, dil in enumerate(DILATIONS):
        dq, dk, dv = _attn_bwd(qs[g], ks[g], vs[g], dos[g], lses[g], cs[g], group_slopes[g], dil, f"attn_bwd{g}")
        dqs.append(dq)
        dks.append(dk)
        dvs.append(dv)
    du, dw_bd, small_grads["pool_scale"] = _pool_bwd(u, dpool, w_bd, small["pool_scale"], "pool_bwd")
    n_pool = len(POOL_HALF_WINDOWS)
    small_grads["w_pool_lin"] = jnp.stack(
        [dw_bd[HEAD_DIM * g:HEAD_DIM * (g + 1), HEAD_DIM * g:HEAD_DIM * (g + 1)] for g in range(n_pool)])
    dx1, dz, small_grads["g_mix_pre"] = hosted(_in_bwd, "in_bwd", du, dqs + dks + dvs, x1, dx2, small["g_mix_pre"], full["w_in"])
    exchange.gradient("w_in", hosted(_wgrad, "wgrad_in", dz, h2))
    dx0 = ffn_backward("ffn1", dx1, x, f1, a1, b1)
    exchange.gradient("w_out", hosted(_wgrad, "wgrad_out", cat, dmix))
    return loss_part[0, 0], dx0, small_grads


SEGMENTS = ("w1_gate", "w1_up", "w1_down", "w_in", "w_out", "w2_gate", "w2_up", "w2_down")
TRANSPOSED = ("w1_gate", "w1_up", "w_in", "w2_gate", "w2_up")
ROWS_OUTSIDE = ("w1_gate", "w1_up", "w2_gate", "w2_up")
HALF = 512


def _place():
    x, y, c = lax.axis_index("x"), lax.axis_index("y"), lax.axis_index("c")
    other_chips = [(1 - x, y), (x, 1 - y), (1 - x, 1 - y)]
    return x, y, c, other_chips


def _chip_rows(chip, rows):
    return pl.ds(pl.multiple_of((2 * chip[0] + chip[1]) * rows, 16), rows)


def _cols(c):
    return pl.ds(pl.multiple_of(c * HALF, HALF), HALF)


def _cast_shards(shards, transposed, place, name):
    n = len(shards)
    rows = [w.shape[1] if t else w.shape[0] for w, t in zip(shards, transposed)]

    def body(place_ref, *refs):
        for w_ref, o_ref, t in zip(refs[:n], refs[n:], transposed):
            o_ref[...] = (w_ref[...].T if t else w_ref[...]).astype(BF16)

    once = pl.Buffered(1)
    return pl.pallas_call(
        body, name=name,
        grid_spec=pltpu.PrefetchScalarGridSpec(
            num_scalar_prefetch=1, grid=(1,),
            in_specs=[pl.BlockSpec(w.shape, lambda i, place: (0, 0), pipeline_mode=once) for w in shards],
            out_specs=[pl.BlockSpec((r, 1024), lambda i, place: (place[0], 0), pipeline_mode=once) for r in rows]),
        out_shape=[jax.ShapeDtypeStruct((N_CHIPS * r, 1024), BF16) for r in rows],
        compiler_params=_params(dimension_semantics=("arbitrary",)))(place, *shards)


def _gather_weights(bufs):
    n = len(bufs)
    rows = [b.shape[0] // N_CHIPS for b in bufs]

    def halves(r):
        first = -(-r // 32) * 16
        return (0, first), (first, r - first)

    def body(*refs):
        outs = refs[n:2 * n]
        ici_send, ici_recv, d2d_send, d2d_recv = refs[2 * n:]
        x, y, c, _ = _place()
        me, via_x, via_y, diagonal = (x, y), (1 - x, y), (x, 1 - y), (1 - x, 1 - y)

        def piece(chip, k, h, cols):
            start, size = halves(rows[k])[h]
            return outs[k].at[pl.ds(pl.multiple_of((2 * chip[0] + chip[1]) * rows[k] + start, 16), size), _cols(cols)]

        def ici(path, chip, k, h, to):
            blk = piece(chip, k, h, c)
            return pltpu.make_async_remote_copy(src_ref=blk, dst_ref=blk, send_sem=ici_send.at[path, k, h],
                                                recv_sem=ici_recv.at[path, k, h], device_id=(*to, c), device_id_type=MESH)

        def d2d(slot, chip, k, h, cols):
            blk = piece(chip, k, h, cols)
            return pltpu.make_async_remote_copy(src_ref=blk, dst_ref=blk, send_sem=d2d_send.at[slot, k, h],
                                                recv_sem=d2d_recv.at[slot, k, h], device_id=(x, y, 1 - c), device_id_type=MESH)

        started = [ici(0, me, k, h, via_x) for h in (0, 1) for k in range(n)] + [ici(1, me, k, h, via_y) for h in (1, 0) for k in range(n)]
        for cp in started:
            cp.start()

        def landed(path, slot, chip, k, h, pass_on_to=None):
            ici(path, chip, k, h, me).wait_recv()
            more = [d2d(slot, chip, k, h, c)] + ([ici(2, chip, k, h, pass_on_to)] if pass_on_to else [])
            for cp in more:
                cp.start()
            started.extend(more)

        for k in range(n):
            landed(0, 0, via_x, k, 0, pass_on_to=via_y)
            landed(1, 1, via_y, k, 1, pass_on_to=via_x)
        for k in range(n):
            landed(0, 0, via_x, k, 1)
            landed(1, 1, via_y, k, 0)
        for k in range(n):
            for h in range(2):
                landed(2, 2, diagonal, k, h)
        for slot, chip in enumerate((via_x, via_y, diagonal)):
            for k in range(n):
                for h in range(2):
                    d2d(slot, chip, k, h, 1 - c).wait_recv()
        for cp in started:
            cp.wait_send()

    any_spec = pl.BlockSpec(memory_space=pl.ANY)
    return pl.pallas_call(
        body, name="gather_weights", in_specs=[any_spec] * n, out_specs=[any_spec] * n,
        out_shape=[jax.ShapeDtypeStruct(b.shape, b.dtype) for b in bufs], input_output_aliases={k: k for k in range(n)},
        scratch_shapes=[pltpu.SemaphoreType.DMA((3, n, 2))] * 4)(*bufs)


def _gather_rider(bufs):
    n = len(bufs)
    rows = [b.shape[0] // N_CHIPS for b in bufs]

    def copies(outs, send_sems, recv_sems, inbound):
        x, y, c, chips = _place()
        for j, chip in enumerate(chips):
            for k in range(n):
                src_chip = chip if inbound else (x, y)
                blk = outs[k].at[_chip_rows(src_chip, rows[k]), _cols(c)]
                yield pltpu.make_async_remote_copy(src_ref=blk, dst_ref=blk, send_sem=send_sems.at[j, k], recv_sem=recv_sems.at[j, k],
                                                   device_id=(*chip, c), device_id_type=MESH)

    def start(ins, outs, send_sems, recv_sems):
        for cp in copies(outs, send_sems, recv_sems, False):
            cp.start()

    def wait(ins, outs, send_sems, recv_sems):
        for cp in copies(outs, send_sems, recv_sems, True):
            cp.wait_recv()
        for cp in copies(outs, send_sems, recv_sems, False):
            cp.wait_send()

    return _Rider(list(bufs), None, (3, n), start, wait)


def _forward_rider(bufs):
    n = len(bufs)
    rows = [b.shape[0] // N_CHIPS for b in bufs]

    def copies(outs, send_sems, recv_sems, half):
        x, y, c, chips = _place()
        for j, chip in enumerate(chips):
            for k in range(n):
                blk = outs[k].at[_chip_rows(chip, rows[k]), _cols(half(c))]
                yield pltpu.make_async_remote_copy(src_ref=blk, dst_ref=blk, send_sem=send_sems.at[j, k], recv_sem=recv_sems.at[j, k],
                                                   device_id=(x, y, 1 - c), device_id_type=MESH)

    def start(ins, outs, send_sems, recv_sems):
        for cp in copies(outs, send_sems, recv_sems, lambda c: c):
            cp.start()

    def wait(ins, outs, send_sems, recv_sems):
        for cp in copies(outs, send_sems, recv_sems, lambda c: 1 - c):
            cp.wait_recv()
        for cp in copies(outs, send_sems, recv_sems, lambda c: c):
            cp.wait_send()

    return _Rider(list(bufs), None, (3, n), start, wait)


def _sibling_rider(grads):
    n = len(grads)

    def copies(ins, outs, send_sems, recv_sems):
        x, y, c, _ = _place()
        return [pltpu.make_async_remote_copy(src_ref=ins[k].at[:, pl.ds(1 - c, 1)], dst_ref=outs[k], send_sem=send_sems.at[k],
                                             recv_sem=recv_sems.at[k], device_id=(x, y, 1 - c), device_id_type=MESH)
                for k in range(n)]

    def start(*refs):
        for cp in copies(*refs):
            cp.start()

    def wait(*refs):
        for cp in copies(*refs):
            cp.wait()

    return _Rider(list(grads), [jax.ShapeDtypeStruct((N_CHIPS, 1) + g.shape[2:], F32) for g in grads], (n,), start, wait)


def _alone(rider, name):
    n = len(rider.operands)
    landing = rider.landing if rider.landing is not None else [jax.ShapeDtypeStruct(a.shape, a.dtype) for a in rider.operands]
    n_out = len(landing)

    def body(*refs):
        rider.start(refs[:n], refs[n:n + n_out], *refs[n + n_out:])
        rider.wait(refs[:n], refs[n:n + n_out], *refs[n + n_out:])

    any_spec = pl.BlockSpec(memory_space=pl.ANY)
    return pl.pallas_call(body, name=name, in_specs=[any_spec] * n, out_specs=[any_spec] * n_out, out_shape=landing,
                          input_output_aliases={i: i for i in range(n)} if rider.landing is None else {},
                          scratch_shapes=[pltpu.SemaphoreType.DMA(rider.sems)] * 2)(*rider.operands)


def _chip_sum(grad, from_sibling, place, name):
    rh, width = grad.shape[2:]

    def body(place_ref, g_ref, s_ref, own_ref, all_ref):
        all_ref[...] = (g_ref[...] + s_ref[...]).astype(BF16)
        mine = place_ref[0]
        own_ref[0] = g_ref[mine, 0] + s_ref[mine, 0]

    blk = (N_CHIPS, 1, rh, width)
    once = pl.Buffered(1)
    return pl.pallas_call(
        body, name=name,
        grid_spec=pltpu.PrefetchScalarGridSpec(
            num_scalar_prefetch=1, grid=(1,),
            in_specs=[pl.BlockSpec(blk, lambda i, place: (0, place[1], 0, 0), pipeline_mode=once),
                      pl.BlockSpec(blk, lambda i, place: (0, 0, 0, 0), pipeline_mode=once)],
            out_specs=[pl.BlockSpec((1, rh, width), lambda i, place: (0, 0, 0), pipeline_mode=once),
                       pl.BlockSpec(blk, lambda i, place: (0, 0, 0, 0), pipeline_mode=once)]),
        out_shape=[jax.ShapeDtypeStruct((1, rh, width), F32), jax.ShapeDtypeStruct((N_CHIPS, 1, rh, width), BF16)],
        compiler_params=_params(dimension_semantics=("arbitrary",)))(place, grad, from_sibling)


def _scatter_rider(sums):
    n = len(sums)

    def copies(ins, outs, send_sems, recv_sems):
        x, y, c, chips = _place()
        return [pltpu.make_async_remote_copy(src_ref=ins[k].at[pl.ds(2 * chip[0] + chip[1], 1)], dst_ref=outs[k].at[pl.ds(j, 1)],
                                             send_sem=send_sems.at[j, k], recv_sem=recv_sems.at[j, k],
                                             device_id=(*chip, c), device_id_type=MESH)
                for j, chip in enumerate(chips) for k in range(n)]

    def start(*refs):
        for cp in copies(*refs):
            cp.start()

    def wait(*refs):
        for cp in copies(*refs):
            cp.wait()

    return _Rider(list(sums), [jax.ShapeDtypeStruct((3,) + sm.shape[1:], BF16) for sm in sums], (3, n), start, wait)


def _total_sums(owns, received, name):
    n = len(owns)

    def body(*refs):
        for o_ref, r_ref, t_ref in zip(refs[:n], refs[n:2 * n], refs[2 * n:]):
            total = o_ref[0]
            for j in range(3):
                total = total + r_ref[j, 0].astype(F32)
            t_ref[0] = total

    return _hosted_call(body, None, name=name, steps=1, in_specs=[_resident(a.shape) for a in owns + received],
                        out_specs=[_resident(o.shape) for o in owns], out_shape=[jax.ShapeDtypeStruct(o.shape, F32) for o in owns],
                        args=owns + received)[0]


def _swap_rider(halves):
    n = len(halves)

    def copies(ins, outs, send_sems, recv_sems):
        x, y, c, _ = _place()
        return [pltpu.make_async_remote_copy(src_ref=ins[k], dst_ref=outs[k], send_sem=send_sems.at[k], recv_sem=recv_sems.at[k],
                                             device_id=(x, y, 1 - c), device_id_type=MESH) for k in range(n)]

    def start(*refs):
        for cp in copies(*refs):
            cp.start()

    def wait(*refs):
        for cp in copies(*refs):
            cp.wait()

    return _Rider(list(halves), [jax.ShapeDtypeStruct(h.shape, F32) for h in halves], (n,), start, wait)


N_DEV = 8


def _gather_small(block):
    m_per, width = block.shape

    def body(x_ref, out_ref, send_sems, recv_sems, local_sem):
        x, y, c, chips = _place()
        me, sibling = (x, y, c), (x, y, 1 - c)

        def rows(px, py, pc):
            return out_ref.at[pl.ds((4 * px + 2 * py + pc) * m_per, m_per), :]

        def copy(k, blk, to, src=None):
            return pltpu.make_async_remote_copy(src_ref=rows(*blk) if src is None else src, dst_ref=rows(*blk),
                                                send_sem=send_sems.at[k], recv_sem=recv_sems.at[k], device_id=to, device_id_type=MESH)

        mine = pltpu.make_async_copy(x_ref, rows(*me), local_sem)
        mine.start()
        first = [copy(0, me, sibling, src=x_ref)] + [copy(1 + j, me, (*chip, c), src=x_ref) for j, chip in enumerate(chips)]
        for cp in first:
            cp.start()
        passed = [copy(4 + j, (*chip, c), sibling) for j, chip in enumerate(chips)]
        for j, chip in enumerate(chips):
            copy(1 + j, (*chip, c), me).wait_recv()
            passed[j].start()
        copy(0, sibling, me).wait_recv()
        for j, chip in enumerate(chips):
            copy(4 + j, (*chip, 1 - c), me).wait_recv()
        for cp in first + passed:
            cp.wait_send()
        mine.wait()

    vmem = pl.BlockSpec(memory_space=pltpu.VMEM)
    return pl.pallas_call(body, name="gather_small", out_shape=jax.ShapeDtypeStruct((N_DEV * m_per, width), F32),
                          in_specs=[vmem], out_specs=vmem,
                          scratch_shapes=[pltpu.SemaphoreType.DMA((7,)), pltpu.SemaphoreType.DMA((7,)),
                                          pltpu.SemaphoreType.DMA])(block)


def _adamw_math(w, g, m, v):
    m = ADAM_B1 * m + (1.0 - ADAM_B1) * g
    v = ADAM_B2 * v + (1.0 - ADAM_B2) * (g * g)
    m_hat = m / (1.0 - ADAM_B1 ** ADAM_STEP)
    v_hat = v / (1.0 - ADAM_B2 ** ADAM_STEP)
    delta = -ADAM_LR * (m_hat / (jnp.sqrt(v_hat) + ADAM_EPS) + ADAM_WD * w)
    return delta, m, v


def _adamw(w, mine, siblings, place, m, v, transposed, name):
    rh, width = mine.shape[1:]
    place_spec = pl.BlockSpec(memory_space=pltpu.SMEM)
    halves = [_const((1, rh, width))] * 2
    out_shape = [jax.ShapeDtypeStruct(w.shape, F32)] * 4
    if transposed:
        def body(place_ref, w_ref, mine_ref, sib_ref, m_ref, v_ref, go_ref, d_ref, mo_ref, vo_ref):
            first = place_ref[1] == 0
            g = jnp.concatenate([jnp.where(first, mine_ref[0], sib_ref[0]), jnp.where(first, sib_ref[0], mine_ref[0])], axis=0).T
            go_ref[...] = g
            d_ref[...], mo_ref[...], vo_ref[...] = _adamw_math(w_ref[...], g, m_ref[...], v_ref[...])

        whole = _resident(w.shape)
        return _hosted_call(body, None, name=name, steps=1, in_specs=[place_spec, whole] + halves + [whole, whole],
                            out_specs=[whole] * 4, out_shape=out_shape, args=[place, w, mine, siblings, m, v])[0]

    def body(place_ref, w_ref, mine_ref, sib_ref, m_ref, v_ref, go_ref, d_ref, mo_ref, vo_ref):
        g = jnp.where(pl.program_id(0) == place_ref[1], mine_ref[0], sib_ref[0])
        go_ref[...] = g
        d_ref[...], mo_ref[...], vo_ref[...] = _adamw_math(w_ref[...], g, m_ref[...], v_ref[...])

    half = _rows(rh, width)
    return _hosted_call(body, None, name=name, steps=2, in_specs=[place_spec, half] + halves + [half, half],
                        out_specs=[half] * 4, out_shape=out_shape, args=[place, w, mine, siblings, m, v])[0]


def _adamw_small(gathered, w, m, v, name):
    def body(ga_ref, w_ref, m_ref, v_ref, go_ref, d_ref, mo_ref, vo_ref):
        g = ga_ref[0]
        for dev in range(1, N_DEV):
            g = g + ga_ref[dev]
        go_ref[...] = g
        d_ref[...], mo_ref[...], vo_ref[...] = _adamw_math(w_ref[...], g, m_ref[...], v_ref[...])

    return pl.pallas_call(body, name=name, out_shape=[jax.ShapeDtypeStruct(w.shape, F32)] * 4,
                          compiler_params=_params())(gathered, w, m, v)


class _Exchange:
    FIRST = ("w1_gate", "w1_up", "w1_down")
    HOSTS = {"ffn2_wgrad_gate": (("w2_down",), ()), "ffn2_wgrad_up": (("w2_gate",), ("w2_down",)),
             "in_bwd": (("w2_up",), ("w2_gate",)), "wgrad_in": ((), ("w2_up",)),
             "ffn1_wgrad_down": ((), ("w_in",)), "ffn1_wgrad_gate": (("w1_down",), ()), "ffn1_wgrad_up": ((), ("w1_down", "w1_gate")),
             "wgrad_out": ((), ("w1_up",))}
    ALONE = ("w_in", "w1_gate", "w1_up", "w_out")

    def __init__(self, bufs, place):
        self.bufs, self.place = bufs, place
        self.later = [k for k in SEGMENTS if k not in self.FIRST]
        self.split, self.own, self.to_send, self.received = {}, {}, {}, {}

    def first_weights(self):
        return dict(zip(self.FIRST, _gather_weights([self.bufs[k] for k in self.FIRST])))

    def riders(self, host):
        if host == "ffn1_fwd":
            return [_gather_rider([self.bufs[k] for k in self.later])]
        if host == "in_fwd":
            return [_forward_rider([self.bufs[k] for k in self.later[1:]])]
        halves, sums = self.HOSTS.get(host, ((), ()))
        return ([_sibling_rider([self.split[k] for k in halves])] if halves else []) + (
            [_scatter_rider([self.to_send[k] for k in sums])] if sums else [])

    def landed(self, host, results):
        if host == "ffn1_fwd":
            self.bufs.update(zip(self.later, results[0]))
            return dict(zip(self.later[:1], _alone(_forward_rider([self.bufs[self.later[0]]]), "gather_forward_first")))
        if host == "in_fwd":
            return dict(zip(self.later[1:], results[0]))
        halves, sums = self.HOSTS.get(host, ((), ()))
        if halves:
            self._chip_sums(halves, results[0])
        if sums:
            self.received.update(zip(sums, results[-1]))

    def gradient(self, name, grad):
        self.split[name] = grad.reshape(N_CHIPS, 2, grad.shape[0] // (2 * N_CHIPS), grad.shape[1])
        if name in self.ALONE:
            self._chip_sums([name], _alone(_sibling_rider([self.split[name]]), f"reduce_sibling_{name}"))

    def _chip_sums(self, names, from_sibling):
        for k, fs in zip(names, from_sibling):
            self.own[k], self.to_send[k] = _chip_sum(self.split[k], fs, self.place, f"chip_sum_{k}")

    def summed_halves(self):
        late = [k for k in SEGMENTS if k not in self.received]
        self.received.update(zip(late, _alone(_scatter_rider([self.to_send[k] for k in late]), "reduce_chips_last")))
        mine = _total_sums([self.own[k] for k in SEGMENTS], [self.received[k] for k in SEGMENTS], "total_sums")
        return mine, _alone(_swap_rider(mine), "swap_halves")


SMALL = ("g_ffn1_pre", "g_ffn1_post", "g_mix_pre", "w_pool_lin", "pool_scale", "g_mix_post", "g_ffn2_pre", "g_ffn2_post")
WEIGHTS = ("g_ffn1_pre", "w1_gate", "w1_up", "w1_down", "g_ffn1_post", "g_mix_pre", "w_in", "w_pool_lin", "pool_scale", "w_out",
           "g_mix_post", "g_ffn2_pre", "w2_gate", "w2_up", "w2_down", "g_ffn2_post")
LANES = 128


def _pack_small(tree, extra=0.0):
    flat = jnp.concatenate([tree[k].reshape(-1) for k in SMALL] + [jnp.reshape(extra, (1,)).astype(F32)])
    rows = -(-flat.shape[0] // (8 * LANES)) * 8
    return jnp.pad(flat, (0, rows * LANES - flat.shape[0])).reshape(rows, LANES)


def _unpack_small(packed, like):
    flat, out, at = packed.reshape(-1), {}, 0
    for k in SMALL:
        size = math.prod(like[k].shape)
        out[k] = flat[at:at + size].reshape(like[k].shape)
        at += size
    return out


def kernel(x, g_ffn1_pre, w1_gate, w1_up, w1_down, g_ffn1_post, g_mix_pre, w_in, w_pool_lin, pool_scale, w_out, g_mix_post, g_ffn2_pre, w2_gate, w2_up, w2_down, g_ffn2_post, loss_target, m_g_ffn1_pre, m_w1_gate, m_w1_up, m_w1_down, m_g_ffn1_post, m_g_mix_pre, m_w_in, m_w_pool_lin, m_pool_scale, m_w_out, m_g_mix_post, m_g_ffn2_pre, m_w2_gate, m_w2_up, m_w2_down, m_g_ffn2_post, v_g_ffn1_pre, v_w1_gate, v_w1_up, v_w1_down, v_g_ffn1_post, v_g_mix_pre, v_w_in, v_w_pool_lin, v_pool_scale, v_w_out, v_g_mix_post, v_g_ffn2_pre, v_w2_gate, v_w2_up, v_w2_down, v_g_ffn2_post):
    given = dict(locals())
    w = {k: given[k] for k in WEIGHTS}
    m = {k: given["m_" + k] for k in WEIGHTS}
    v = {k: given["v_" + k] for k in WEIGHTS}
    small = {k: (w[k][0] if k == "w_pool_lin" else w[k].reshape(1, -1)) for k in SMALL}

    place = jnp.stack([2 * lax.axis_index("x") + lax.axis_index("y"), lax.axis_index("c")]).astype(jnp.int32)
    def as_rows(a, k):
        return jnp.swapaxes(a, 1, 2)[0] if k in ROWS_OUTSIDE else a[0]

    def as_given(a, k):
        return jnp.swapaxes(a[None], 1, 2) if k in ROWS_OUTSIDE else a[None]

    in_kernel = [k for k in TRANSPOSED if k not in ROWS_OUTSIDE]
    bufs = {}
    for tag, names in (("first", _Exchange.FIRST), ("rest", [k for k in SEGMENTS if k not in _Exchange.FIRST])):
        bufs.update(zip(names, _cast_shards([as_rows(w[k], k) for k in names], [k in in_kernel for k in names], place, f"cast_{tag}")))
    exchange = _Exchange(bufs, place)
    loss_part, grad_x, small_grads = _local_step(x[0], loss_target[0], small, exchange)

    out_grad, out_delta, out_m, out_v = {}, {}, {}, {}
    for k, mine, siblings in zip(SEGMENTS, *exchange.summed_halves()):
        results = _adamw(as_rows(w[k], k), mine, siblings, place, as_rows(m[k], k), as_rows(v[k], k), k in in_kernel, f"adamw_{k}")
        out_grad[k], out_delta[k], out_m[k], out_v[k] = (as_given(a, k) for a in results)

    small_grads["w_pool_lin"] = small_grads["w_pool_lin"][None]
    packed = _pack_small(small_grads, loss_part)
    gathered = _gather_small(packed).reshape(N_DEV, *packed.shape)
    like = {k: w[k] for k in SMALL}
    results = _adamw_small(gathered, _pack_small(like), _pack_small({k: m[k] for k in SMALL}),
                           _pack_small({k: v[k] for k in SMALL}), "adamw_small")
    for tree, res in zip((out_grad, out_delta, out_m, out_v), results):
        tree.update(_unpack_small(res, like))
    loss = results[0].reshape(-1)[sum(math.prod(like[k].shape) for k in SMALL)]

    return (loss, grad_x[None], *[out_grad[k] for k in WEIGHTS], *[out_delta[k] for k in WEIGHTS],
            *[out_m[k] for k in WEIGHTS], *[out_v[k] for k in WEIGHTS])
```

```python
import math
import typing

import numpy as np
import jax
import jax.numpy as jnp
from jax import lax
from jax.experimental import pallas as pl
from jax.experimental.pallas import tpu as pltpu

F32 = jnp.float32
BF16 = jnp.bfloat16
MESH = pl.DeviceIdType.MESH

RMS_EPS = 1e-6
HEAD_DIM = 64
POOL_HALF_WINDOWS = (1, 2, 4, 8)
POOL_DIM = 256
GROUP_DIM = 256
DILATIONS = (1, 4, 16)
N_SIDE = 64
N_ATTN_HEADS = 12
ADAM_LR, ADAM_B1, ADAM_B2, ADAM_EPS, ADAM_WD, ADAM_STEP = 0.001, 0.9, 0.999, 1e-08, 0.01, 10

N_CHIPS = 4
V7X_VMEM_LIMIT = 60 * 1024 * 1024

_NT = (((1,), (1,)), ((), ()))
_TN = (((0,), (0,)), ((), ()))


def _dot(a, b):
    return jnp.dot(a, b, preferred_element_type=F32)


def _dot_nt(a, b):
    return lax.dot_general(a, b, _NT, preferred_element_type=F32)


def _dot_tn(a, b):
    return lax.dot_general(a, b, _TN, preferred_element_type=F32)


def _params(**kw):
    return pltpu.CompilerParams(vmem_limit_bytes=V7X_VMEM_LIMIT, **kw)


def _rows(tm, width):
    return pl.BlockSpec((tm, width), lambda i: (i, 0))


def _resident(shape):
    return pl.BlockSpec(shape, lambda i: (0,) * len(shape), pipeline_mode=pl.Buffered(1))


def _const(shape):
    return pl.BlockSpec(shape, lambda i: (0,) * len(shape))


def _inv_rms(x):
    return lax.rsqrt(jnp.mean(x * x, axis=-1, keepdims=True) + RMS_EPS)


def _rms_bwd(x, inv, g, dy):
    n = x * inv
    dn = dy * g
    dx = inv * (dn - n * jnp.mean(dn * n, axis=-1, keepdims=True))
    return dx, jnp.sum(dy * n, axis=0, keepdims=True)


def _accumulate(ref, value):
    @pl.when(pl.program_id(0) == 0)
    def _():
        ref[...] = jnp.zeros_like(ref)

    ref[...] += value


class _Rider(typing.NamedTuple):
    operands: list
    landing: typing.Optional[list]
    sems: tuple
    start: typing.Callable
    wait: typing.Callable


def _hosted_call(body, riders, *, name, steps, in_specs, out_specs, out_shape, args, scratch_shapes=()):
    params = _params(dimension_semantics=("arbitrary",))
    riders = list(riders or [])
    if not riders:
        res = pl.pallas_call(body, name=name, grid=(steps,), in_specs=in_specs, out_specs=out_specs, out_shape=out_shape,
                             scratch_shapes=list(scratch_shapes), compiler_params=params)(*args)
        return list(res), []
    n_in, n_out, n_scratch = len(in_specs), len(out_specs), len(scratch_shapes)
    operands, landing, aliases, spans = [], [], {}, []
    for rd in riders:
        lands = rd.landing if rd.landing is not None else [jax.ShapeDtypeStruct(a.shape, a.dtype) for a in rd.operands]
        if rd.landing is None:
            aliases.update({n_in + len(operands) + i: n_out + len(landing) + i for i in range(len(lands))})
        spans.append((len(operands), len(rd.operands), len(landing), len(lands)))
        operands += rd.operands
        landing += lands
    outs_at = n_in + len(operands)
    scratch_at = outs_at + n_out + len(landing)

    def riding(*refs):
        def each(action):
            for i, (rd, (in_at, n_ops, out_at, n_lands)) in enumerate(zip(riders, spans)):
                sems = refs[scratch_at + n_scratch + 2 * i:scratch_at + n_scratch + 2 * i + 2]
                getattr(rd, action)(refs[n_in + in_at:n_in + in_at + n_ops],
                                    refs[outs_at + n_out + out_at:outs_at + n_out + out_at + n_lands], *sems)

        @pl.when(pl.program_id(0) == 0)
        def _():
            each("start")

        body(*refs[:n_in], *refs[outs_at:outs_at + n_out], *refs[scratch_at:scratch_at + n_scratch])

        @pl.when(pl.program_id(0) == steps - 1)
        def _():
            each("wait")

    any_spec = pl.BlockSpec(memory_space=pl.ANY)
    res = pl.pallas_call(
        riding, name=name, grid=(steps,), in_specs=list(in_specs) + [any_spec] * len(operands),
        out_specs=list(out_specs) + [any_spec] * len(landing), out_shape=list(out_shape) + landing,
        scratch_shapes=list(scratch_shapes) + [pltpu.SemaphoreType.DMA(rd.sems) for rd in riders for _ in range(2)],
        input_output_aliases=aliases, compiler_params=params)(*args, *operands)
    return list(res[:n_out]), [list(res[n_out + out_at:n_out + out_at + n_lands]) for _, _, out_at, n_lands in spans]


_SUB_TILE = 256


def _sub_tiles(tm):
    return [pl.ds(r, _SUB_TILE) for r in range(0, tm, _SUB_TILE)]


def _ffn_fwd(x, g_pre, wg_t, wu_t, wd, g_post, target, name, riders=None, tm=512):
    s, d = x.shape
    ff = wd.shape[0]
    with_loss = target is not None

    def body(*refs):
        if with_loss:
            x_ref, gpre_ref, wg_ref, wu_ref, wd_ref, gpost_ref, t_ref, xo_ref, a_ref, b_ref, f_ref, loss_ref = refs
        else:
            x_ref, gpre_ref, wg_ref, wu_ref, wd_ref, gpost_ref, xo_ref, a_ref, b_ref, f_ref = refs
        loss = 0.0
        for rows in _sub_tiles(tm):
            xv = x_ref[rows, :]
            hb = (xv * _inv_rms(xv) * gpre_ref[...]).astype(BF16)
            a = _dot_nt(hb, wg_ref[...])
            b = _dot_nt(hb, wu_ref[...])
            hh = (a * jax.nn.sigmoid(a)) * b
            f = _dot(hh.astype(BF16), wd_ref[...])
            xo = xv + 0.5 * (f * _inv_rms(f) * gpost_ref[...])
            a_ref[rows, :] = a.astype(BF16)
            b_ref[rows, :] = b.astype(BF16)
            f_ref[rows, :] = f
            if with_loss:
                e = xo - t_ref[rows, :]
                xo_ref[rows, :] = e * (1.0 / d)
                loss = loss + 0.5 * jnp.sum(jnp.mean(e * e, axis=-1, keepdims=True))
            else:
                xo_ref[rows, :] = xo
        if with_loss:
            _accumulate(loss_ref, loss)

    in_specs = [_rows(tm, d), _const((1, d)), _resident((ff, d)), _resident((ff, d)), _resident((ff, d)), _const((1, d))]
    args = [x, g_pre, wg_t, wu_t, wd, g_post]
    out_shape = [jax.ShapeDtypeStruct((s, d), F32), jax.ShapeDtypeStruct((s, ff), BF16),
                 jax.ShapeDtypeStruct((s, ff), BF16), jax.ShapeDtypeStruct((s, d), F32)]
    out_specs = [_rows(tm, d), _rows(tm, ff), _rows(tm, ff), _rows(tm, d)]
    if with_loss:
        in_specs.append(_rows(tm, d))
        args.append(target)
        out_shape.append(jax.ShapeDtypeStruct((8, 128), F32))
        out_specs.append(_const((8, 128)))
    return _hosted_call(body, riders, name=name, steps=s // tm, in_specs=in_specs, out_specs=out_specs, out_shape=out_shape, args=args)


def _ffn_bwd(dxo, x, f, a, b, g_pre, g_post, wg_t, wu_t, wd, name, riders=None, tm=256):
    s, d = x.shape
    ff = wd.shape[0]

    def body(dxo_ref, x_ref, f_ref, a_ref, b_ref, gpre_ref, gpost_ref, wg_ref, wu_ref, wd_ref,
             dx_ref, hh_ref, da_ref, db_ref, df_ref, h_ref, dgpre_ref, dgpost_ref):
        dgpre_sum = dgpost_sum = 0.0
        for rows in _sub_tiles(tm):
            dxo_v = dxo_ref[rows, :]
            fv = f_ref[rows, :]
            df, dgpost = _rms_bwd(fv, _inv_rms(fv), gpost_ref[...], 0.5 * dxo_v)
            dfb = df.astype(BF16)
            dhh = _dot_nt(dfb, wd_ref[...])
            av = a_ref[rows, :].astype(F32)
            bv = b_ref[rows, :].astype(F32)
            sig = jax.nn.sigmoid(av)
            sa = av * sig
            da = (dhh * bv * (sig * (1.0 + av * (1.0 - sig)))).astype(BF16)
            db = (dhh * sa).astype(BF16)
            dh = _dot(da, wg_ref[...]) + _dot(db, wu_ref[...])
            xv = x_ref[rows, :]
            inv = _inv_rms(xv)
            dxn, dgpre = _rms_bwd(xv, inv, gpre_ref[...], dh)
            dx_ref[rows, :] = dxo_v + dxn
            hh_ref[rows, :] = (sa * bv).astype(BF16)
            da_ref[rows, :] = da
            db_ref[rows, :] = db
            df_ref[rows, :] = dfb
            h_ref[rows, :] = (xv * inv * gpre_ref[...]).astype(BF16)
            dgpre_sum, dgpost_sum = dgpre_sum + dgpre, dgpost_sum + dgpost
        _accumulate(dgpre_ref, dgpre_sum)
        _accumulate(dgpost_ref, dgpost_sum)

    return _hosted_call(
        body, riders, name=name, steps=s // tm,
        in_specs=[_rows(tm, d), _rows(tm, d), _rows(tm, d), _rows(tm, ff), _rows(tm, ff), _const((1, d)), _const((1, d)),
                  _resident((ff, d)), _resident((ff, d)), _resident((ff, d))],
        out_specs=[_rows(tm, d), _rows(tm, ff), _rows(tm, ff), _rows(tm, ff), _rows(tm, d), _rows(tm, d),
                   _const((1, d)), _const((1, d))],
        out_shape=[jax.ShapeDtypeStruct((s, d), F32), jax.ShapeDtypeStruct((s, ff), BF16), jax.ShapeDtypeStruct((s, ff), BF16),
                   jax.ShapeDtypeStruct((s, ff), BF16), jax.ShapeDtypeStruct((s, d), BF16), jax.ShapeDtypeStruct((s, d), BF16),
                   jax.ShapeDtypeStruct((1, d), F32), jax.ShapeDtypeStruct((1, d), F32)],
        args=[dxo, x, f, a, b, g_pre, g_post, wg_t, wu_t, wd])


def _wgrad(lhs, rhs, name, riders=None, rt=256):
    s, r = lhs.shape
    c = rhs.shape[1]

    def body(l_ref, r_ref, o_ref):
        o_ref[...] = _dot_tn(l_ref[...], r_ref[...])

    (out,), riding = _hosted_call(
        body, riders, name=name, steps=pl.cdiv(r, rt), in_specs=[pl.BlockSpec((s, rt), lambda i: (0, i)), _resident((s, c))],
        out_specs=[pl.BlockSpec((rt, c), lambda i: (i, 0))], out_shape=[jax.ShapeDtypeStruct((r, c), F32)], args=[lhs, rhs])
    return out, riding


def _attn_dtype(dilation):
    return BF16 if dilation == 1 else F32


def _in_fwd(x, g, w_in_t, name, riders=None, tm=1024):
    s, d = x.shape
    d_in = w_in_t.shape[0]
    n_groups = len(DILATIONS)
    dtypes = [_attn_dtype(dil) for dil in DILATIONS] * 3

    def body(x_ref, g_ref, w_ref, h_ref, u_ref, *part_refs):
        xv = x_ref[...]
        hb = (xv * _inv_rms(xv) * g_ref[...]).astype(BF16)
        h_ref[...] = hb
        z = _dot_nt(hb, w_ref[...])
        u_ref[...] = z[:, :POOL_DIM]
        for j, ref in enumerate(part_refs):
            part = z[:, POOL_DIM + GROUP_DIM * j:POOL_DIM + GROUP_DIM * (j + 1)]
            ref[...] = (part * _SCORE_SCALE if j < n_groups else part).astype(ref.dtype)

    return _hosted_call(
        body, riders, name=name, steps=s // tm, in_specs=[_rows(tm, d), _const((1, d)), _resident((d_in, d))],
        out_specs=[_rows(tm, d), _rows(tm, POOL_DIM)] + [_rows(tm, GROUP_DIM)] * len(dtypes),
        out_shape=[jax.ShapeDtypeStruct((s, d), BF16), jax.ShapeDtypeStruct((s, POOL_DIM), F32)]
        + [jax.ShapeDtypeStruct((s, GROUP_DIM), dt) for dt in dtypes],
        args=[x, g, w_in_t])


def _in_bwd(du, dparts, x, dxo, g, w_in_t, name, riders=None, tm=512):
    s, d = x.shape
    d_in = w_in_t.shape[0]
    n_parts = len(dparts)

    def body(du_ref, *refs):
        part_refs = refs[:n_parts]
        x_ref, dxo_ref, g_ref, w_ref, dx_ref, dg_ref = refs[n_parts:]
        dh = _dot(jnp.concatenate([r[...] for r in (du_ref,) + part_refs], axis=1), w_ref[...])
        xv = x_ref[...]
        dxn, dg = _rms_bwd(xv, _inv_rms(xv), g_ref[...], dh)
        dx_ref[...] = dxo_ref[...] + dxn
        _accumulate(dg_ref, dg)

    return _hosted_call(
        body, riders, name=name, steps=s // tm,
        in_specs=[_rows(tm, POOL_DIM)] + [_rows(tm, GROUP_DIM)] * n_parts + [_rows(tm, d), _rows(tm, d), _const((1, d)),
                                                                             _resident((d_in, d))],
        out_specs=[_rows(tm, d), _const((1, d))],
        out_shape=[jax.ShapeDtypeStruct((s, d), F32), jax.ShapeDtypeStruct((1, d), F32)],
        args=[du, *dparts, x, dxo, g, w_in_t])


def _wgrad_parts(parts, rhs, name, riders=None):
    n = len(parts)
    s, rt = parts[0].shape
    c = rhs.shape[1]

    def body(*refs):
        part_refs, r_ref, o_ref, buf, sems = refs[:n], refs[n], refs[n + 1], refs[n + 2], refs[n + 3]

        def fetch(i):
            return pltpu.make_async_copy(part_refs[i], buf.at[i % 2], sems.at[i % 2])

        fetch(0).start()
        for i in range(n):
            if i + 1 < n:
                fetch(i + 1).start()
            fetch(i).wait()
            o_ref[pl.ds(i * rt, rt), :] = _dot_tn(buf[i % 2], r_ref[...])

    (out,), riding = _hosted_call(
        body, riders, name=name, steps=1, in_specs=[pl.BlockSpec(memory_space=pl.ANY)] * n + [_resident((s, c))],
        out_specs=[_resident((n * rt, c))], out_shape=[jax.ShapeDtypeStruct((n * rt, c), F32)], args=[*parts, rhs],
        scratch_shapes=[pltpu.VMEM((2, s, rt), BF16), pltpu.SemaphoreType.DMA((2,))])
    return out, riding


_POOL_HALO = 8


def _pool_chain(v, first_shift):
    n = v.shape[0]
    p2 = v + pltpu.roll(v, first_shift, 0)
    p4 = pltpu.roll(p2, 1, 0) + pltpu.roll(p2, n - 1, 0)
    p8 = pltpu.roll(p4, 2, 0) + pltpu.roll(p4, n - 2, 0)
    p16 = pltpu.roll(p8, 4, 0) + pltpu.roll(p8, n - 4, 0)
    group = lax.broadcasted_iota(jnp.int32, v.shape, 1) // HEAD_DIM
    return jnp.where(group == 0, p2, jnp.where(group == 1, p4, jnp.where(group == 2, p8, p16)))


def _pool_count(t0, rows, s):
    t = t0 + lax.broadcasted_iota(jnp.int32, (rows, POOL_DIM), 0)
    group = lax.broadcasted_iota(jnp.int32, (rows, POOL_DIM), 1) // HEAD_DIM
    half = jnp.where(group == 0, 1, jnp.where(group == 1, 2, jnp.where(group == 2, 4, 8)))
    cnt = jnp.minimum(t + half, s) - jnp.maximum(t - half, 0)
    return jnp.maximum(cnt, 1).astype(F32)


def _pad_rows(ref, pad_ref, s):
    zeros = jnp.zeros((_POOL_HALO, pad_ref.shape[1]), pad_ref.dtype)
    pad_ref[pl.ds(0, _POOL_HALO), :] = zeros
    pad_ref[pl.ds(_POOL_HALO + s, _POOL_HALO), :] = zeros
    pad_ref[pl.ds(_POOL_HALO, s), :] = ref[...]


def _pool_fwd(u, w_bd, scale, name, tm=512):
    s = u.shape[0]
    ext = tm + 2 * _POOL_HALO

    def body(u_ref, w_ref, sc_ref, o_ref, upad):
        _pad_rows(u_ref, upad, s)

        def tile(i, carry):
            t0 = pl.multiple_of(i * tm, tm)
            uv = upad[pl.ds(t0, ext), :]
            win = _pool_chain(uv, 1)[_POOL_HALO:_POOL_HALO + tm]
            y = win / _pool_count(t0, tm, s) - uv[_POOL_HALO:_POOL_HALO + tm]
            o_ref[pl.ds(t0, tm), :] = (_dot(y.astype(BF16), w_ref[...]) * sc_ref[...]).astype(BF16)
            return carry

        lax.fori_loop(0, s // tm, tile, 0)

    return pl.pallas_call(body, name=name, out_shape=jax.ShapeDtypeStruct((s, POOL_DIM), BF16),
                          scratch_shapes=[pltpu.VMEM((s + 2 * _POOL_HALO, POOL_DIM), F32)],
                          compiler_params=_params())(u, w_bd, scale)


def _pool_bwd(u, da, w_bd, scale, name, tm=512):
    s = u.shape[0]
    ext = tm + 2 * _POOL_HALO

    def body(u_ref, da_ref, w_ref, sc_ref, du_ref, dw_ref, dsc_ref, upad, dapad):
        _pad_rows(u_ref, upad, s)
        _pad_rows(da_ref, dapad, s)
        dw_ref[...] = jnp.zeros_like(dw_ref)
        dsc_ref[...] = jnp.zeros_like(dsc_ref)

        def tile(i, carry):
            t0 = pl.multiple_of(i * tm, tm)
            uv = upad[pl.ds(t0, ext), :]
            dav = dapad[pl.ds(t0, ext), :]
            win = _pool_chain(uv, 1)[_POOL_HALO:_POOL_HALO + tm]
            yb = (win / _pool_count(t0, tm, s) - uv[_POOL_HALO:_POOL_HALO + tm]).astype(BF16)
            yl = _dot(yb, w_ref[...])
            da_c = dav[_POOL_HALO:_POOL_HALO + tm]
            dsc_ref[...] += jnp.sum(da_c * yl, axis=0, keepdims=True)
            dyl = (dav * sc_ref[...]).astype(BF16)
            dw_ref[...] += _dot_tn(yb, dyl[_POOL_HALO:_POOL_HALO + tm])
            dy = _dot_nt(dyl, w_ref[...])
            dyc = dy / _pool_count(t0 - _POOL_HALO, ext, s)
            du_ref[pl.ds(t0, tm), :] = (_pool_chain(dyc, ext - 1) - dy)[_POOL_HALO:_POOL_HALO + tm].astype(BF16)
            return carry

        lax.fori_loop(0, s // tm, tile, 0)

    pool_cols = pl.BlockSpec((s, POOL_DIM), lambda i: (0, 0), pipeline_mode=pl.Buffered(1))
    return pl.pallas_call(
        body, name=name, grid=(1,),
        in_specs=[pool_cols, pool_cols, _const((POOL_DIM, POOL_DIM)), _const((1, POOL_DIM))],
        out_specs=[_const((s, POOL_DIM)), _const((POOL_DIM, POOL_DIM)), _const((1, POOL_DIM))],
        out_shape=[jax.ShapeDtypeStruct((s, POOL_DIM), BF16), jax.ShapeDtypeStruct((POOL_DIM, POOL_DIM), F32),
                   jax.ShapeDtypeStruct((1, POOL_DIM), F32)],
        scratch_shapes=[pltpu.VMEM((s + 2 * _POOL_HALO, POOL_DIM), F32), pltpu.VMEM((s + 2 * _POOL_HALO, POOL_DIM), F32)],
        compiler_params=_params(dimension_semantics=("arbitrary",)))(u, da, w_bd, scale)


_BQ = 128
_KW = _BQ + 2 * N_SIDE
_PAIR = 2 * HEAD_DIM
_NEG = -1e30
_ATTN_UNROLL = 8
_SCORE_SCALE = HEAD_DIM ** -0.5


def _stack_heads(x):
    lane_head = lax.broadcasted_iota(jnp.int32, x.shape, 1) // HEAD_DIM
    zero = jnp.zeros_like(x)
    return jnp.concatenate([jnp.where(lane_head == 0, x, zero), jnp.where(lane_head == 1, x, zero)], axis=0)


def _unstack_heads(x):
    lane_head = lax.broadcasted_iota(jnp.int32, (_BQ, _PAIR), 1) // HEAD_DIM
    return jnp.where(lane_head == 0, x[:_BQ], x[_BQ:])


def _stack_cols(x):
    return jnp.concatenate([x[:, 0:1], x[:, HEAD_DIM:HEAD_DIM + 1]], axis=0)


def _fill_bias(bias_ref, slopes_ref, dilation):
    row = lax.broadcasted_iota(jnp.int32, (2 * _BQ, _KW), 0)
    col = lax.broadcasted_iota(jnp.int32, (2 * _BQ, _KW), 1)
    pair = 2 * pl.program_id(0)
    slope = jnp.where(row < _BQ, slopes_ref[pair], slopes_ref[pair + 1]) * float(dilation)

    @pl.when(pl.program_id(1) == 0)
    def _():
        for j in range(3):
            dist = jnp.abs(col - (row & (_BQ - 1)) - j * N_SIDE)
            bias_ref[j] = jnp.where(dist <= N_SIDE, -slope * dist.astype(F32), _NEG)


def _block_window(i, n_blocks, length):
    q0 = pl.multiple_of(i * _BQ, _BQ)
    ws = pl.multiple_of(jnp.clip(q0 - N_SIDE, 0, length - _KW), N_SIDE)
    return q0, ws, jnp.where(i == 0, 0, jnp.where(i == n_blocks - 1, 2, 1))


_FREE_STRIDE = 4


def _residue_views(dilation, seq, ins, outs, tmps):
    step = pl.program_id(1)
    if dilation <= _FREE_STRIDE:
        def rows(start, count):
            return pl.ds(start, count) if dilation == 1 else pl.ds(start * dilation + step, count, stride=dilation)

        return ins, outs, rows, lambda: None
    inner = dilation // _FREE_STRIDE
    assert inner <= _FREE_STRIDE and len(tmps) == len(ins) + len(outs)
    first, second = step // inner, step % inner
    coarse = pl.ds(first, seq // _FREE_STRIDE, stride=_FREE_STRIDE)
    in_tmps, out_tmps = tmps[:len(ins)], tmps[len(ins):]

    @pl.when(second == 0)
    def _():
        for ref, tmp in zip(ins, in_tmps):
            tmp[...] = ref[coarse, :]

    def flush():
        @pl.when(second == inner - 1)
        def _():
            for ref, tmp in zip(outs, out_tmps):
                ref[coarse, :] = tmp[...]

    return in_tmps, out_tmps, lambda start, count: pl.ds(start * inner + second, count, stride=inner), flush


def _attn_call(body, name, dilation, seq, n_in, out_dtypes, scratch, buffers):
    col = pl.BlockSpec((seq, _PAIR), lambda c, r: (0, c), pipeline_mode=pl.Buffered(buffers))
    tmps = [pltpu.VMEM((seq // _FREE_STRIDE, _PAIR), F32)] * (n_in + len(out_dtypes) if dilation > _FREE_STRIDE else 0)
    return pl.pallas_call(
        body, name=name, grid=(GROUP_DIM // _PAIR, dilation),
        in_specs=[pl.BlockSpec(memory_space=pltpu.SMEM)] + [col] * n_in, out_specs=[col] * len(out_dtypes),
        out_shape=[jax.ShapeDtypeStruct((seq, GROUP_DIM), dt) for dt in out_dtypes], scratch_shapes=scratch + tmps,
        compiler_params=_params(dimension_semantics=("arbitrary", "arbitrary")))


def _staged(dilation, length, rows, sources, scratch):
    if dilation == 1:
        return sources
    for src, dst in zip(sources, scratch):
        dst[...] = src[rows(0, length), :].astype(BF16)
    return scratch


def _attn_fwd(q, k, v, slopes, dilation, name):
    seq = q.shape[0]
    length = seq // dilation
    n_blocks = length // _BQ
    n_stage = 0 if dilation == 1 else 3

    def body(sl_ref, q_ref, k_ref, v_ref, o_ref, lse_ref, *scratch):
        bias_ref, tmps = scratch[n_stage], scratch[n_stage + 1:]
        (q_in, k_in, v_in), (o_out, lse_out), rows, flush = _residue_views(dilation, seq, (q_ref, k_ref, v_ref), (o_ref, lse_ref), tmps)
        qs, ks, vs = _staged(dilation, length, rows, (q_in, k_in, v_in), scratch[:n_stage])
        _fill_bias(bias_ref, sl_ref, dilation)

        def block(i, carry):
            q0, ws, which = _block_window(i, n_blocks, length)
            kw = ks[pl.ds(ws, _KW), :]
            vw = vs[pl.ds(ws, _KW), :]
            sc = _dot_nt(_stack_heads(qs[pl.ds(q0, _BQ), :]), kw) + bias_ref[which]
            m = jnp.max(sc, axis=-1, keepdims=True)
            p = jnp.exp(sc - m)
            den = jnp.sum(p, axis=-1, keepdims=True)
            o_out[rows(q0, _BQ), :] = _unstack_heads(_dot(p.astype(BF16), vw) / den)
            lse_out[rows(q0, _BQ), :] = _unstack_heads(jnp.broadcast_to(m + jnp.log(den), (2 * _BQ, _PAIR)))
            return carry

        lax.fori_loop(0, n_blocks, block, 0, unroll=min(_ATTN_UNROLL, n_blocks))
        flush()

    stage = pltpu.VMEM((length, _PAIR), BF16)
    bias = pltpu.VMEM((3, 2 * _BQ, _KW), F32)
    return _attn_call(body, name, dilation, seq, 3, [F32, F32], [stage] * n_stage + [bias], 2)(slopes, q, k, v)


def _attn_bwd(q, k, v, do, lse, cterm, slopes, dilation, name):
    seq = q.shape[0]
    length = seq // dilation
    n_blocks = length // _BQ
    n_stage, n_whole = (0, 0) if dilation == 1 else (4, 3)

    def body(sl_ref, q_ref, k_ref, v_ref, do_ref, lse_ref, c_ref, dq_ref, dk_ref, dv_ref, *scratch):
        dk_acc, dv_acc, bias_ref = scratch[n_stage:n_stage + 3]
        whole, tmps = scratch[n_stage + 3:n_stage + 3 + n_whole], scratch[n_stage + 3 + n_whole:]
        (q_in, k_in, v_in, do_in, lse_in, c_in), (dq_out, dk_out, dv_out), rows, flush = _residue_views(
            dilation, seq, (q_ref, k_ref, v_ref, do_ref, lse_ref, c_ref), whole or (dq_ref, dk_ref, dv_ref), tmps)
        all_rows = rows(0, length)
        qs, ks, vs, dos = _staged(dilation, length, rows, (q_in, k_in, v_in, do_in), scratch[:n_stage])
        dk_acc[...] = jnp.zeros_like(dk_acc)
        dv_acc[...] = jnp.zeros_like(dv_acc)
        _fill_bias(bias_ref, sl_ref, dilation)

        def block(i, carry):
            q0, ws, which = _block_window(i, n_blocks, length)
            qm = _stack_heads(qs[pl.ds(q0, _BQ), :])
            dom = _stack_heads(dos[pl.ds(q0, _BQ), :])
            kw = ks[pl.ds(ws, _KW), :]
            vw = vs[pl.ds(ws, _KW), :]
            p = jnp.exp(_dot_nt(qm, kw) + bias_ref[which] - _stack_cols(lse_in[rows(q0, _BQ), :]))
            ds = (p * (_dot_nt(dom, vw) + _stack_cols(c_in[rows(q0, _BQ), :]))).astype(BF16)
            dq_out[rows(q0, _BQ), :] = (_unstack_heads(_dot(ds, kw)) * _SCORE_SCALE).astype(dq_out.dtype)
            dk_acc[pl.ds(ws, _KW), :] += _dot_tn(ds, qm)
            dv_acc[pl.ds(ws, _KW), :] += _dot_tn(p.astype(BF16), dom)
            return carry

        lax.fori_loop(0, n_blocks, block, 0, unroll=min(_ATTN_UNROLL, n_blocks))
        dk_out[all_rows, :] = dk_acc[...].astype(dk_out.dtype)
        dv_out[all_rows, :] = dv_acc[...].astype(dv_out.dtype)
        flush()
        if whole:
            @pl.when(pl.program_id(1) == dilation - 1)
            def _():
                for ref, collected in zip((dq_ref, dk_ref, dv_ref), whole):
                    ref[...] = collected[...].astype(BF16)

    stage = pltpu.VMEM((length, _PAIR), BF16)
    acc = pltpu.VMEM((length, _PAIR), F32)
    bias = pltpu.VMEM((3, 2 * _BQ, _KW), F32)
    collect = pltpu.VMEM((seq, _PAIR), F32)
    return _attn_call(body, name, dilation, seq, 6, [BF16] * 3, [stage] * n_stage + [acc] * 2 + [bias] + [collect] * n_whole,
                      2 if dilation == 1 else 1)(slopes, q, k, v, do, lse, cterm)


def _group_weights(lses):
    m = jnp.maximum(jnp.maximum(lses[0], lses[1]), lses[2])
    es = [jnp.exp(l - m) for l in lses]
    den = es[0] + es[1] + es[2]
    return [e / den for e in es]


def _out_fwd(a_pool, outs, lses, x, w_out, g, name, tm=1024):
    s, d = x.shape
    width = POOL_DIM + 3 * GROUP_DIM

    def body(ap_ref, o0, o1, o2, l0, l1, l2, x_ref, w_ref, g_ref, xo_ref, cat_ref):
        alphas = _group_weights([l0[...], l1[...], l2[...]])
        cat = jnp.concatenate([ap_ref[...]] + [(o[...] * al).astype(BF16) for o, al in zip((o0, o1, o2), alphas)], axis=1)
        cat_ref[...] = cat
        mix = _dot(cat, w_ref[...])
        xo_ref[...] = x_ref[...] + mix * _inv_rms(mix) * g_ref[...]

    return pl.pallas_call(
        body, name=name, grid=(s // tm,),
        in_specs=[_rows(tm, POOL_DIM)] + [_rows(tm, GROUP_DIM)] * 6 + [_rows(tm, d), _resident(w_out.shape), _const((1, d))],
        out_specs=[_rows(tm, d), _rows(tm, width)],
        out_shape=[jax.ShapeDtypeStruct((s, d), F32), jax.ShapeDtypeStruct((s, width), BF16)],
        compiler_params=_params(dimension_semantics=("arbitrary",)))(a_pool, *outs, *lses, x, w_out, g)


def _out_bwd(dxo, cat, outs, lses, w_out, g, head_ones, name, tm=1024):
    s, d = dxo.shape

    def body(dxo_ref, cat_ref, o0, o1, o2, l0, l1, l2, w_ref, g_ref, ones_ref, dpool_ref, dmix_ref, do0, do1, do2, c0, c1, c2, dg_ref):
        mv = _dot(cat_ref[...], w_ref[...])
        dmix, dg = _rms_bwd(mv, _inv_rms(mv), g_ref[...], dxo_ref[...])
        dmb = dmix.astype(BF16)
        dmix_ref[...] = dmb
        _accumulate(dg_ref, dg)
        dcat = _dot_nt(dmb, w_ref[...])
        dpool_ref[...] = dcat[:, :POOL_DIM]
        alphas = _group_weights([l0[...], l1[...], l2[...]])
        das = [dcat[:, POOL_DIM + GROUP_DIM * j:POOL_DIM + GROUP_DIM * (j + 1)] for j in range(3)]
        prod = sum(da * (o[...] * al) for da, o, al in zip(das, (o0, o1, o2), alphas))
        hi = prod.astype(BF16)
        lo = (prod - hi.astype(F32)).astype(BF16)
        total = _dot(hi, ones_ref[...]) + _dot(lo, ones_ref[...])
        for da, al, do_ref, c_ref in zip(das, alphas, (do0, do1, do2), (c0, c1, c2)):
            do_ref[...] = (da * al).astype(do_ref.dtype)
            c_ref[...] = -al * total

    return pl.pallas_call(
        body, name=name, grid=(s // tm,),
        in_specs=[_rows(tm, d), _rows(tm, cat.shape[1])] + [_rows(tm, GROUP_DIM)] * 6 + [_resident(w_out.shape), _const((1, d)),
                                                                                        _const((GROUP_DIM, GROUP_DIM))],
        out_specs=[_rows(tm, POOL_DIM), _rows(tm, d)] + [_rows(tm, GROUP_DIM)] * 6 + [_const((1, d))],
        out_shape=[jax.ShapeDtypeStruct((s, POOL_DIM), F32), jax.ShapeDtypeStruct((s, d), BF16)]
        + [jax.ShapeDtypeStruct((s, GROUP_DIM), _attn_dtype(dil)) for dil in DILATIONS]
        + [jax.ShapeDtypeStruct((s, GROUP_DIM), F32)] * 3 + [jax.ShapeDtypeStruct((1, d), F32)],
        compiler_params=_params(dimension_semantics=("arbitrary",)))(dxo, cat, *outs, *lses, w_out, g, head_ones)


def _alibi_slopes():
    return np.array([2.0 ** (-8.0 * (i + 1) / N_ATTN_HEADS) for i in range(N_ATTN_HEADS)], np.float32)


def _block_diag(w_lin):
    n, c, _ = w_lin.shape
    eye = jnp.eye(n, dtype=w_lin.dtype)
    return (eye[:, None, :, None] * w_lin[:, :, None, :]).reshape(n * c, n * c)


class _NoExchange:
    def __init__(self, full):
        self.full, self.grads = full, {}

    def first_weights(self):
        return self.full

    def riders(self, host):
        return []

    def landed(self, host, results):
        return self.full

    def gradient(self, name, grad):
        self.grads[name] = grad


def _local_step(x, target, small, exchange):
    s, d = x.shape
    slopes = _alibi_slopes()
    group_slopes = [jnp.asarray(slopes[4 * g:4 * g + 4]) for g in range(3)]
    w_bd = _block_diag(small["w_pool_lin"]).astype(BF16)
    head_ones = jnp.asarray(np.kron(np.eye(GROUP_DIM // HEAD_DIM), np.ones((HEAD_DIM, HEAD_DIM))), BF16)

    full = dict(exchange.first_weights())

    def hosted(call, host, *args):
        results, riding = call(*args, host, exchange.riders(host))
        full.update(exchange.landed(host, riding) or {})
        return results

    x1, a1, b1, f1 = hosted(_ffn_fwd, "ffn1_fwd", x, small["g_ffn1_pre"], full["w1_gate"], full["w1_up"], full["w1_down"],
                            small["g_ffn1_post"], None)
    h2, u, *parts = hosted(_in_fwd, "in_fwd", x1, small["g_mix_pre"], full["w_in"])
    qs, ks, vs = parts[0:3], parts[3:6], parts[6:9]
    a_pool = _pool_fwd(u, w_bd, small["pool_scale"], "pool_fwd")
    outs, lses = [], []
    for g, dil in enumerate(DILATIONS):
        o, lse = _attn_fwd(qs[g], ks[g], vs[g], group_slopes[g], dil, f"attn_fwd{g}")
        outs.append(o)
        lses.append(lse)
    x2, cat = _out_fwd(a_pool, outs, lses, x1, full["w_out"], small["g_mix_post"], "out_fwd")
    (dx3, a2, b2, f2, loss_part), _ = _ffn_fwd(x2, small["g_ffn2_pre"], full["w2_gate"], full["w2_up"], full["w2_down"],
                                               small["g_ffn2_post"], target, "ffn2_fwd")

    small_grads = {}

    def ffn_backward(tag, dxo, x_in, f, a, b):
        n = tag[-1]
        dx, hh, da, db, df, h, dg_pre, dg_post = hosted(
            _ffn_bwd, f"{tag}_bwd", dxo, x_in, f, a, b, small[f"g_{tag}_pre"], small[f"g_{tag}_post"],
            full[f"w{n}_gate"], full[f"w{n}_up"], full[f"w{n}_down"])
        for part, lhs, rhs in (("down", hh, df), ("gate", da, h), ("up", db, h)):
            exchange.gradient(f"w{n}_{part}", hosted(_wgrad, f"{tag}_wgrad_{part}", lhs, rhs))
        small_grads[f"g_{tag}_pre"], small_grads[f"g_{tag}_post"] = dg_pre, dg_post
        return dx

    dx2 = ffn_backward("ffn2", dx3, x2, f2, a2, b2)
    dpool, dmix, *dos_cs, small_grads["g_mix_post"] = _out_bwd(dx2, cat, outs, lses, full["w_out"], small["g_mix_post"],
                                                               head_ones, "out_bwd")
    dos, cs = dos_cs[:3], dos_cs[3:]
    dqs, dks, dvs = [], [], []
    for g, dil in enumerate(DILATIONS):
        dq, dk, dv = _attn_bwd(qs[g], ks[g], vs[g], dos[g], lses[g], cs[g], group_slopes[g], dil, f"attn_bwd{g}")
        dqs.append(dq)
        dks.append(dk)
        dvs.append(dv)
    du, dw_bd, small_grads["pool_scale"] = _pool_bwd(u, dpool, w_bd, small["pool_scale"], "pool_bwd")
    n_pool = len(POOL_HALF_WINDOWS)
    small_grads["w_pool_lin"] = jnp.stack(
        [dw_bd[HEAD_DIM * g:HEAD_DIM * (g + 1), HEAD_DIM * g:HEAD_DIM * (g + 1)] for g in range(n_pool)])
    dz_parts = dqs + dks + dvs
    dx1, small_grads["g_mix_pre"] = hosted(_in_bwd, "in_bwd", du, dz_parts, x1, dx2, small["g_mix_pre"], full["w_in"])
    exchange.gradient("w_in", hosted(_wgrad_parts, "wgrad_in", [du] + dz_parts, h2))
    dx0 = ffn_backward("ffn1", dx1, x, f1, a1, b1)
    exchange.gradient("w_out", hosted(_wgrad, "wgrad_out", cat, dmix))
    return loss_part[0, 0], dx0, small_grads


SEGMENTS = ("w1_gate", "w1_up", "w1_down", "w_in", "w_out", "w2_gate", "w2_up", "w2_down")
TRANSPOSED = ("w1_gate", "w1_up", "w_in", "w2_gate", "w2_up")
ROWS_OUTSIDE = ("w1_gate", "w1_up", "w2_gate", "w2_up")
HALF = 512


def _place():
    x, y, c = lax.axis_index("x"), lax.axis_index("y"), lax.axis_index("c")
    other_chips = [(1 - x, y), (x, 1 - y), (1 - x, 1 - y)]
    return x, y, c, other_chips


def _chip_rows(chip, rows):
    return pl.ds(pl.multiple_of((2 * chip[0] + chip[1]) * rows, 16), rows)


def _cols(c):
    return pl.ds(pl.multiple_of(c * HALF, HALF), HALF)


def _cast_shards(shards, transposed, place, name):
    n = len(shards)
    rows = [w.shape[1] if t else w.shape[0] for w, t in zip(shards, transposed)]

    def body(place_ref, *refs):
        for w_ref, o_ref, t in zip(refs[:n], refs[n:], transposed):
            o_ref[...] = (w_ref[...].T if t else w_ref[...]).astype(BF16)

    once = pl.Buffered(1)
    return pl.pallas_call(
        body, name=name,
        grid_spec=pltpu.PrefetchScalarGridSpec(
            num_scalar_prefetch=1, grid=(1,),
            in_specs=[pl.BlockSpec(w.shape, lambda i, place: (0, 0), pipeline_mode=once) for w in shards],
            out_specs=[pl.BlockSpec((r, 1024), lambda i, place: (place[0], 0), pipeline_mode=once) for r in rows]),
        out_shape=[jax.ShapeDtypeStruct((N_CHIPS * r, 1024), BF16) for r in rows],
        compiler_params=_params(dimension_semantics=("arbitrary",)))(place, *shards)


def _gather_weights(bufs):
    n = len(bufs)
    rows = [b.shape[0] // N_CHIPS for b in bufs]

    def halves(r):
        first = -(-r // 32) * 16
        return (0, first), (first, r - first)

    def body(*refs):
        outs = refs[n:2 * n]
        ici_send, ici_recv, d2d_send, d2d_recv = refs[2 * n:]
        x, y, c, _ = _place()
        me, via_x, via_y, diagonal = (x, y), (1 - x, y), (x, 1 - y), (1 - x, 1 - y)

        def piece(chip, k, h, cols):
            start, size = halves(rows[k])[h]
            return outs[k].at[pl.ds(pl.multiple_of((2 * chip[0] + chip[1]) * rows[k] + start, 16), size), _cols(cols)]

        def ici(path, chip, k, h, to):
            blk = piece(chip, k, h, c)
            return pltpu.make_async_remote_copy(src_ref=blk, dst_ref=blk, send_sem=ici_send.at[path, k, h],
                                                recv_sem=ici_recv.at[path, k, h], device_id=(*to, c), device_id_type=MESH)

        def d2d(slot, chip, k, h, cols):
            blk = piece(chip, k, h, cols)
            return pltpu.make_async_remote_copy(src_ref=blk, dst_ref=blk, send_sem=d2d_send.at[slot, k, h],
                                                recv_sem=d2d_recv.at[slot, k, h], device_id=(x, y, 1 - c), device_id_type=MESH)

        started = [ici(0, me, k, h, via_x) for h in (0, 1) for k in range(n)] + [ici(1, me, k, h, via_y) for h in (1, 0) for k in range(n)]
        for cp in started:
            cp.start()

        def landed(path, slot, chip, k, h, pass_on_to=None):
            ici(path, chip, k, h, me).wait_recv()
            more = [d2d(slot, chip, k, h, c)] + ([ici(2, chip, k, h, pass_on_to)] if pass_on_to else [])
            for cp in more:
                cp.start()
            started.extend(more)

        for k in range(n):
            landed(0, 0, via_x, k, 0, pass_on_to=via_y)
            landed(1, 1, via_y, k, 1, pass_on_to=via_x)
        for k in range(n):
            landed(0, 0, via_x, k, 1)
            landed(1, 1, via_y, k, 0)
        for k in range(n):
            for h in range(2):
                landed(2, 2, diagonal, k, h)
        for slot, chip in enumerate((via_x, via_y, diagonal)):
            for k in range(n):
                for h in range(2):
                    d2d(slot, chip, k, h, 1 - c).wait_recv()
        for cp in started:
            cp.wait_send()

    any_spec = pl.BlockSpec(memory_space=pl.ANY)
    return pl.pallas_call(
        body, name="gather_weights", in_specs=[any_spec] * n, out_specs=[any_spec] * n,
        out_shape=[jax.ShapeDtypeStruct(b.shape, b.dtype) for b in bufs], input_output_aliases={k: k for k in range(n)},
        scratch_shapes=[pltpu.SemaphoreType.DMA((3, n, 2))] * 4)(*bufs)


def _gather_rider(bufs):
    n = len(bufs)
    rows = [b.shape[0] // N_CHIPS for b in bufs]

    def copies(outs, send_sems, recv_sems, inbound):
        x, y, c, chips = _place()
        for j, chip in enumerate(chips):
            for k in range(n):
                src_chip = chip if inbound else (x, y)
                blk = outs[k].at[_chip_rows(src_chip, rows[k]), _cols(c)]
                yield pltpu.make_async_remote_copy(src_ref=blk, dst_ref=blk, send_sem=send_sems.at[j, k], recv_sem=recv_sems.at[j, k],
                                                   device_id=(*chip, c), device_id_type=MESH)

    def start(ins, outs, send_sems, recv_sems):
        for cp in copies(outs, send_sems, recv_sems, False):
            cp.start()

    def wait(ins, outs, send_sems, recv_sems):
        for cp in copies(outs, send_sems, recv_sems, True):
            cp.wait_recv()
        for cp in copies(outs, send_sems, recv_sems, False):
            cp.wait_send()

    return _Rider(list(bufs), None, (3, n), start, wait)


def _forward_rider(bufs):
    n = len(bufs)
    rows = [b.shape[0] // N_CHIPS for b in bufs]

    def copies(outs, send_sems, recv_sems, half):
        x, y, c, chips = _place()
        for j, chip in enumerate(chips):
            for k in range(n):
                blk = outs[k].at[_chip_rows(chip, rows[k]), _cols(half(c))]
                yield pltpu.make_async_remote_copy(src_ref=blk, dst_ref=blk, send_sem=send_sems.at[j, k], recv_sem=recv_sems.at[j, k],
                                                   device_id=(x, y, 1 - c), device_id_type=MESH)

    def start(ins, outs, send_sems, recv_sems):
        for cp in copies(outs, send_sems, recv_sems, lambda c: c):
            cp.start()

    def wait(ins, outs, send_sems, recv_sems):
        for cp in copies(outs, send_sems, recv_sems, lambda c: 1 - c):
            cp.wait_recv()
        for cp in copies(outs, send_sems, recv_sems, lambda c: c):
            cp.wait_send()

    return _Rider(list(bufs), None, (3, n), start, wait)


def _sibling_rider(grads):
    n = len(grads)

    def copies(ins, outs, send_sems, recv_sems):
        x, y, c, _ = _place()
        return [pltpu.make_async_remote_copy(src_ref=ins[k].at[:, pl.ds(1 - c, 1)], dst_ref=outs[k], send_sem=send_sems.at[k],
                                             recv_sem=recv_sems.at[k], device_id=(x, y, 1 - c), device_id_type=MESH)
                for k in range(n)]

    def start(*refs):
        for cp in copies(*refs):
            cp.start()

    def wait(*refs):
        for cp in copies(*refs):
            cp.wait()

    return _Rider(list(grads), [jax.ShapeDtypeStruct((N_CHIPS, 1) + g.shape[2:], F32) for g in grads], (n,), start, wait)


def _alone(rider, name):
    n = len(rider.operands)
    landing = rider.landing if rider.landing is not None else [jax.ShapeDtypeStruct(a.shape, a.dtype) for a in rider.operands]
    n_out = len(landing)

    def body(*refs):
        rider.start(refs[:n], refs[n:n + n_out], *refs[n + n_out:])
        rider.wait(refs[:n], refs[n:n + n_out], *refs[n + n_out:])

    any_spec = pl.BlockSpec(memory_space=pl.ANY)
    return pl.pallas_call(body, name=name, in_specs=[any_spec] * n, out_specs=[any_spec] * n_out, out_shape=landing,
                          input_output_aliases={i: i for i in range(n)} if rider.landing is None else {},
                          scratch_shapes=[pltpu.SemaphoreType.DMA(rider.sems)] * 2)(*rider.operands)


def _chip_sum(grad, from_sibling, place, name):
    rh, width = grad.shape[2:]

    def body(place_ref, g_ref, s_ref, own_ref, all_ref):
        all_ref[...] = (g_ref[...] + s_ref[...]).astype(BF16)
        mine = place_ref[0]
        own_ref[0] = g_ref[mine, 0] + s_ref[mine, 0]

    blk = (N_CHIPS, 1, rh, width)
    once = pl.Buffered(1)
    return pl.pallas_call(
        body, name=name,
        grid_spec=pltpu.PrefetchScalarGridSpec(
            num_scalar_prefetch=1, grid=(1,),
            in_specs=[pl.BlockSpec(blk, lambda i, place: (0, place[1], 0, 0), pipeline_mode=once),
                      pl.BlockSpec(blk, lambda i, place: (0, 0, 0, 0), pipeline_mode=once)],
            out_specs=[pl.BlockSpec((1, rh, width), lambda i, place: (0, 0, 0), pipeline_mode=once),
                       pl.BlockSpec(blk, lambda i, place: (0, 0, 0, 0), pipeline_mode=once)]),
        out_shape=[jax.ShapeDtypeStruct((1, rh, width), F32), jax.ShapeDtypeStruct((N_CHIPS, 1, rh, width), BF16)],
        compiler_params=_params(dimension_semantics=("arbitrary",)))(place, grad, from_sibling)


def _scatter_rider(sums):
    n = len(sums)

    def copies(ins, outs, send_sems, recv_sems):
        x, y, c, chips = _place()
        return [pltpu.make_async_remote_copy(src_ref=ins[k].at[pl.ds(2 * chip[0] + chip[1], 1)], dst_ref=outs[k].at[pl.ds(j, 1)],
                                             send_sem=send_sems.at[j, k], recv_sem=recv_sems.at[j, k],
                                             device_id=(*chip, c), device_id_type=MESH)
                for j, chip in enumerate(chips) for k in range(n)]

    def start(*refs):
        for cp in copies(*refs):
            cp.start()

    def wait(*refs):
        for cp in copies(*refs):
            cp.wait()

    return _Rider(list(sums), [jax.ShapeDtypeStruct((3,) + sm.shape[1:], BF16) for sm in sums], (3, n), start, wait)


def _total_sums(owns, received, name):
    n = len(owns)

    def body(*refs):
        for o_ref, r_ref, t_ref in zip(refs[:n], refs[n:2 * n], refs[2 * n:]):
            total = o_ref[0]
            for j in range(3):
                total = total + r_ref[j, 0].astype(F32)
            t_ref[0] = total

    return _hosted_call(body, None, name=name, steps=1, in_specs=[_resident(a.shape) for a in owns + received],
                        out_specs=[_resident(o.shape) for o in owns], out_shape=[jax.ShapeDtypeStruct(o.shape, F32) for o in owns],
                        args=owns + received)[0]


def _swap_rider(halves):
    n = len(halves)

    def copies(ins, outs, send_sems, recv_sems):
        x, y, c, _ = _place()
        return [pltpu.make_async_remote_copy(src_ref=ins[k], dst_ref=outs[k], send_sem=send_sems.at[k], recv_sem=recv_sems.at[k],
                                             device_id=(x, y, 1 - c), device_id_type=MESH) for k in range(n)]

    def start(*refs):
        for cp in copies(*refs):
            cp.start()

    def wait(*refs):
        for cp in copies(*refs):
            cp.wait()

    return _Rider(list(halves), [jax.ShapeDtypeStruct(h.shape, F32) for h in halves], (n,), start, wait)


N_DEV = 8


def _gather_small(block):
    m_per, width = block.shape

    def body(x_ref, out_ref, send_sems, recv_sems, local_sem):
        x, y, c, chips = _place()
        me, sibling = (x, y, c), (x, y, 1 - c)

        def rows(px, py, pc):
            return out_ref.at[pl.ds((4 * px + 2 * py + pc) * m_per, m_per), :]

        def copy(k, blk, to, src=None):
            return pltpu.make_async_remote_copy(src_ref=rows(*blk) if src is None else src, dst_ref=rows(*blk),
                                                send_sem=send_sems.at[k], recv_sem=recv_sems.at[k], device_id=to, device_id_type=MESH)

        mine = pltpu.make_async_copy(x_ref, rows(*me), local_sem)
        mine.start()
        first = [copy(0, me, sibling, src=x_ref)] + [copy(1 + j, me, (*chip, c), src=x_ref) for j, chip in enumerate(chips)]
        for cp in first:
            cp.start()
        passed = [copy(4 + j, (*chip, c), sibling) for j, chip in enumerate(chips)]
        for j, chip in enumerate(chips):
            copy(1 + j, (*chip, c), me).wait_recv()
            passed[j].start()
        copy(0, sibling, me).wait_recv()
        for j, chip in enumerate(chips):
            copy(4 + j, (*chip, 1 - c), me).wait_recv()
        for cp in first + passed:
            cp.wait_send()
        mine.wait()

    vmem = pl.BlockSpec(memory_space=pltpu.VMEM)
    return pl.pallas_call(body, name="gather_small", out_shape=jax.ShapeDtypeStruct((N_DEV * m_per, width), F32),
                          in_specs=[vmem], out_specs=vmem,
                          scratch_shapes=[pltpu.SemaphoreType.DMA((7,)), pltpu.SemaphoreType.DMA((7,)),
                                          pltpu.SemaphoreType.DMA])(block)


def _adamw_math(w, g, m, v):
    m = ADAM_B1 * m + (1.0 - ADAM_B1) * g
    v = ADAM_B2 * v + (1.0 - ADAM_B2) * (g * g)
    m_hat = m / (1.0 - ADAM_B1 ** ADAM_STEP)
    v_hat = v / (1.0 - ADAM_B2 ** ADAM_STEP)
    delta = -ADAM_LR * (m_hat / (jnp.sqrt(v_hat) + ADAM_EPS) + ADAM_WD * w)
    return delta, m, v


def _adamw(w, mine, siblings, place, m, v, transposed, name):
    rh, width = mine.shape[1:]
    place_spec = pl.BlockSpec(memory_space=pltpu.SMEM)
    halves = [_const((1, rh, width))] * 2
    out_shape = [jax.ShapeDtypeStruct(w.shape, F32)] * 4
    if transposed:
        def body(place_ref, w_ref, mine_ref, sib_ref, m_ref, v_ref, go_ref, d_ref, mo_ref, vo_ref):
            first = place_ref[1] == 0
            g = jnp.concatenate([jnp.where(first, mine_ref[0], sib_ref[0]), jnp.where(first, sib_ref[0], mine_ref[0])], axis=0).T
            go_ref[...] = g
            d_ref[...], mo_ref[...], vo_ref[...] = _adamw_math(w_ref[...], g, m_ref[...], v_ref[...])

        whole = _resident(w.shape)
        return _hosted_call(body, None, name=name, steps=1, in_specs=[place_spec, whole] + halves + [whole, whole],
                            out_specs=[whole] * 4, out_shape=out_shape, args=[place, w, mine, siblings, m, v])[0]

    def body(place_ref, w_ref, mine_ref, sib_ref, m_ref, v_ref, go_ref, d_ref, mo_ref, vo_ref):
        g = jnp.where(pl.program_id(0) == place_ref[1], mine_ref[0], sib_ref[0])
        go_ref[...] = g
        d_ref[...], mo_ref[...], vo_ref[...] = _adamw_math(w_ref[...], g, m_ref[...], v_ref[...])

    half = _rows(rh, width)
    return _hosted_call(body, None, name=name, steps=2, in_specs=[place_spec, half] + halves + [half, half],
                        out_specs=[half] * 4, out_shape=out_shape, args=[place, w, mine, siblings, m, v])[0]


def _adamw_small(gathered, w, m, v, name):
    def body(ga_ref, w_ref, m_ref, v_ref, go_ref, d_ref, mo_ref, vo_ref):
        g = ga_ref[0]
        for dev in range(1, N_DEV):
            g = g + ga_ref[dev]
        go_ref[...] = g
        d_ref[...], mo_ref[...], vo_ref[...] = _adamw_math(w_ref[...], g, m_ref[...], v_ref[...])

    return pl.pallas_call(body, name=name, out_shape=[jax.ShapeDtypeStruct(w.shape, F32)] * 4,
                          compiler_params=_params())(gathered, w, m, v)


class _Exchange:
    FIRST = ("w1_gate", "w1_up", "w1_down")
    HOSTS = {"ffn2_wgrad_gate": (("w2_down",), ()), "ffn2_wgrad_up": (("w2_gate",), ("w2_down",)),
             "in_bwd": (("w2_up",), ("w2_gate",)), "wgrad_in": ((), ("w2_up",)),
             "ffn1_wgrad_down": ((), ("w_in",)), "ffn1_wgrad_gate": (("w1_down",), ()), "ffn1_wgrad_up": ((), ("w1_down", "w1_gate")),
             "wgrad_out": ((), ("w1_up",))}
    ALONE = ("w_in", "w1_gate", "w1_up", "w_out")

    def __init__(self, bufs, place):
        self.bufs, self.place = bufs, place
        self.later = [k for k in SEGMENTS if k not in self.FIRST]
        self.split, self.own, self.to_send, self.received = {}, {}, {}, {}

    def first_weights(self):
        return dict(zip(self.FIRST, _gather_weights([self.bufs[k] for k in self.FIRST])))

    def riders(self, host):
        if host == "ffn1_fwd":
            return [_gather_rider([self.bufs[k] for k in self.later])]
        if host == "in_fwd":
            return [_forward_rider([self.bufs[k] for k in self.later[1:]])]
        halves, sums = self.HOSTS.get(host, ((), ()))
        return ([_sibling_rider([self.split[k] for k in halves])] if halves else []) + (
            [_scatter_rider([self.to_send[k] for k in sums])] if sums else [])

    def landed(self, host, results):
        if host == "ffn1_fwd":
            self.bufs.update(zip(self.later, results[0]))
            return dict(zip(self.later[:1], _alone(_forward_rider([self.bufs[self.later[0]]]), "gather_forward_first")))
        if host == "in_fwd":
            return dict(zip(self.later[1:], results[0]))
        halves, sums = self.HOSTS.get(host, ((), ()))
        if halves:
            self._chip_sums(halves, results[0])
        if sums:
            self.received.update(zip(sums, results[-1]))

    def gradient(self, name, grad):
        self.split[name] = grad.reshape(N_CHIPS, 2, grad.shape[0] // (2 * N_CHIPS), grad.shape[1])
        if name in self.ALONE:
            self._chip_sums([name], _alone(_sibling_rider([self.split[name]]), f"reduce_sibling_{name}"))

    def _chip_sums(self, names, from_sibling):
        for k, fs in zip(names, from_sibling):
            self.own[k], self.to_send[k] = _chip_sum(self.split[k], fs, self.place, f"chip_sum_{k}")

    def summed_halves(self):
        late = [k for k in SEGMENTS if k not in self.received]
        self.received.update(zip(late, _alone(_scatter_rider([self.to_send[k] for k in late]), "reduce_chips_last")))
        mine = _total_sums([self.own[k] for k in SEGMENTS], [self.received[k] for k in SEGMENTS], "total_sums")
        return mine, _alone(_swap_rider(mine), "swap_halves")


SMALL = ("g_ffn1_pre", "g_ffn1_post", "g_mix_pre", "w_pool_lin", "pool_scale", "g_mix_post", "g_ffn2_pre", "g_ffn2_post")
WEIGHTS = ("g_ffn1_pre", "w1_gate", "w1_up", "w1_down", "g_ffn1_post", "g_mix_pre", "w_in", "w_pool_lin", "pool_scale", "w_out",
           "g_mix_post", "g_ffn2_pre", "w2_gate", "w2_up", "w2_down", "g_ffn2_post")
LANES = 128


def _pack_small(tree, extra=0.0):
    flat = jnp.concatenate([tree[k].reshape(-1) for k in SMALL] + [jnp.reshape(extra, (1,)).astype(F32)])
    rows = -(-flat.shape[0] // (8 * LANES)) * 8
    return jnp.pad(flat, (0, rows * LANES - flat.shape[0])).reshape(rows, LANES)


def _unpack_small(packed, like):
    flat, out, at = packed.reshape(-1), {}, 0
    for k in SMALL:
        size = math.prod(like[k].shape)
        out[k] = flat[at:at + size].reshape(like[k].shape)
        at += size
    return out


def kernel(x, g_ffn1_pre, w1_gate, w1_up, w1_down, g_ffn1_post, g_mix_pre, w_in, w_pool_lin, pool_scale, w_out, g_mix_post, g_ffn2_pre, w2_gate, w2_up, w2_down, g_ffn2_post, loss_target, m_g_ffn1_pre, m_w1_gate, m_w1_up, m_w1_down, m_g_ffn1_post, m_g_mix_pre, m_w_in, m_w_pool_lin, m_pool_scale, m_w_out, m_g_mix_post, m_g_ffn2_pre, m_w2_gate, m_w2_up, m_w2_down, m_g_ffn2_post, v_g_ffn1_pre, v_w1_gate, v_w1_up, v_w1_down, v_g_ffn1_post, v_g_mix_pre, v_w_in, v_w_pool_lin, v_pool_scale, v_w_out, v_g_mix_post, v_g_ffn2_pre, v_w2_gate, v_w2_up, v_w2_down, v_g_ffn2_post):
    given = dict(locals())
    w = {k: given[k] for k in WEIGHTS}
    m = {k: given["m_" + k] for k in WEIGHTS}
    v = {k: given["v_" + k] for k in WEIGHTS}
    small = {k: (w[k][0] if k == "w_pool_lin" else w[k].reshape(1, -1)) for k in SMALL}

    place = jnp.stack([2 * lax.axis_index("x") + lax.axis_index("y"), lax.axis_index("c")]).astype(jnp.int32)
    def as_rows(a, k):
        return jnp.swapaxes(a, 1, 2)[0] if k in ROWS_OUTSIDE else a[0]

    def as_given(a, k):
        return jnp.swapaxes(a[None], 1, 2) if k in ROWS_OUTSIDE else a[None]

    in_kernel = [k for k in TRANSPOSED if k not in ROWS_OUTSIDE]
    bufs = {}
    for tag, names in (("first", _Exchange.FIRST), ("rest", [k for k in SEGMENTS if k not in _Exchange.FIRST])):
        bufs.update(zip(names, _cast_shards([as_rows(w[k], k) for k in names], [k in in_kernel for k in names], place, f"cast_{tag}")))
    exchange = _Exchange(bufs, place)
    loss_part, grad_x, small_grads = _local_step(x[0], loss_target[0], small, exchange)

    out_grad, out_delta, out_m, out_v = {}, {}, {}, {}
    for k, mine, siblings in zip(SEGMENTS, *exchange.summed_halves()):
        results = _adamw(as_rows(w[k], k), mine, siblings, place, as_rows(m[k], k), as_rows(v[k], k), k in in_kernel, f"adamw_{k}")
        out_grad[k], out_delta[k], out_m[k], out_v[k] = (as_given(a, k) for a in results)

    small_grads["w_pool_lin"] = small_grads["w_pool_lin"][None]
    packed = _pack_small(small_grads, loss_part)
    gathered = _gather_small(packed).reshape(N_DEV, *packed.shape)
    like = {k: w[k] for k in SMALL}
    results = _adamw_small(gathered, _pack_small(like), _pack_small({k: m[k] for k in SMALL}),
                           _pack_small({k: v[k] for k in SMALL}), "adamw_small")
    for tree, res in zip((out_grad, out_delta, out_m, out_v), results):
        tree.update(_unpack_small(res, like))
    loss = results[0].reshape(-1)[sum(math.prod(like[k].shape) for k in SMALL)]

    return (loss, grad_x[None], *[out_grad[k] for k in WEIGHTS], *[out_delta[k] for k in WEIGHTS],
            *[out_m[k] for k in WEIGHTS], *[out_v[k] for k in WEIGHTS])
```

```python
import math
import typing

import numpy as np
import jax
import jax.numpy as jnp
from jax import lax
from jax.experimental import pallas as pl
from jax.experimental.pallas import tpu as pltpu

F32 = jnp.float32
BF16 = jnp.bfloat16
MESH = pl.DeviceIdType.MESH

RMS_EPS = 1e-6
HEAD_DIM = 64
POOL_HALF_WINDOWS = (1, 2, 4, 8)
POOL_DIM = 256
GROUP_DIM = 256
DILATIONS = (1, 4, 16)
N_SIDE = 64
N_ATTN_HEADS = 12
ADAM_LR, ADAM_B1, ADAM_B2, ADAM_EPS, ADAM_WD, ADAM_STEP = 0.001, 0.9, 0.999, 1e-08, 0.01, 10

N_CHIPS = 4
V7X_VMEM_LIMIT = 60 * 1024 * 1024

_NT = (((1,), (1,)), ((), ()))
_TN = (((0,), (0,)), ((), ()))


def _dot(a, b):
    return jnp.dot(a, b, preferred_element_type=F32)


def _dot_nt(a, b):
    return lax.dot_general(a, b, _NT, preferred_element_type=F32)


def _dot_tn(a, b):
    return lax.dot_general(a, b, _TN, preferred_element_type=F32)


def _params(**kw):
    return pltpu.CompilerParams(vmem_limit_bytes=V7X_VMEM_LIMIT, **kw)


def _rows(tm, width):
    return pl.BlockSpec((tm, width), lambda i: (i, 0))


def _resident(shape):
    return pl.BlockSpec(shape, lambda i: (0,) * len(shape), pipeline_mode=pl.Buffered(1))


def _const(shape):
    return pl.BlockSpec(shape, lambda i: (0,) * len(shape))


def _inv_rms(x):
    return lax.rsqrt(jnp.mean(x * x, axis=-1, keepdims=True) + RMS_EPS)


def _rms_bwd(x, inv, g, dy):
    n = x * inv
    dn = dy * g
    dx = inv * (dn - n * jnp.mean(dn * n, axis=-1, keepdims=True))
    return dx, jnp.sum(dy * n, axis=0, keepdims=True)


def _accumulate(ref, value):
    @pl.when(pl.program_id(0) == 0)
    def _():
        ref[...] = jnp.zeros_like(ref)

    ref[...] += value


class _Rider(typing.NamedTuple):
    operands: list
    landing: typing.Optional[list]
    sems: tuple
    start: typing.Callable
    wait: typing.Callable


def _hosted_call(body, riders, *, name, steps, in_specs, out_specs, out_shape, args, scratch_shapes=()):
    params = _params(dimension_semantics=("arbitrary",))
    riders = list(riders or [])
    if not riders:
        res = pl.pallas_call(body, name=name, grid=(steps,), in_specs=in_specs, out_specs=out_specs, out_shape=out_shape,
                             scratch_shapes=list(scratch_shapes), compiler_params=params)(*args)
        return list(res), []
    n_in, n_out, n_scratch = len(in_specs), len(out_specs), len(scratch_shapes)
    operands, landing, aliases, spans = [], [], {}, []
    for rd in riders:
        lands = rd.landing if rd.landing is not None else [jax.ShapeDtypeStruct(a.shape, a.dtype) for a in rd.operands]
        if rd.landing is None:
            aliases.update({n_in + len(operands) + i: n_out + len(landing) + i for i in range(len(lands))})
        spans.append((len(operands), len(rd.operands), len(landing), len(lands)))
        operands += rd.operands
        landing += lands
    outs_at = n_in + len(operands)
    scratch_at = outs_at + n_out + len(landing)

    def riding(*refs):
        def each(action):
            for i, (rd, (in_at, n_ops, out_at, n_lands)) in enumerate(zip(riders, spans)):
                sems = refs[scratch_at + n_scratch + 2 * i:scratch_at + n_scratch + 2 * i + 2]
                getattr(rd, action)(refs[n_in + in_at:n_in + in_at + n_ops],
                                    refs[outs_at + n_out + out_at:outs_at + n_out + out_at + n_lands], *sems)

        @pl.when(pl.program_id(0) == 0)
        def _():
            each("start")

        body(*refs[:n_in], *refs[outs_at:outs_at + n_out], *refs[scratch_at:scratch_at + n_scratch])

        @pl.when(pl.program_id(0) == steps - 1)
        def _():
            each("wait")

    any_spec = pl.BlockSpec(memory_space=pl.ANY)
    res = pl.pallas_call(
        riding, name=name, grid=(steps,), in_specs=list(in_specs) + [any_spec] * len(operands),
        out_specs=list(out_specs) + [any_spec] * len(landing), out_shape=list(out_shape) + landing,
        scratch_shapes=list(scratch_shapes) + [pltpu.SemaphoreType.DMA(rd.sems) for rd in riders for _ in range(2)],
        input_output_aliases=aliases, compiler_params=params)(*args, *operands)
    return list(res[:n_out]), [list(res[n_out + out_at:n_out + out_at + n_lands]) for _, _, out_at, n_lands in spans]


_SUB_TILE = 256


def _sub_tiles(tm):
    return [pl.ds(r, _SUB_TILE) for r in range(0, tm, _SUB_TILE)]


def _ffn_fwd(x, g_pre, wg_t, wu_t, wd, g_post, target, name, riders=None, tm=512):
    s, d = x.shape
    ff = wd.shape[0]
    with_loss = target is not None

    def body(*refs):
        if with_loss:
            x_ref, gpre_ref, wg_ref, wu_ref, wd_ref, gpost_ref, t_ref, xo_ref, a_ref, b_ref, f_ref, loss_ref = refs
        else:
            x_ref, gpre_ref, wg_ref, wu_ref, wd_ref, gpost_ref, xo_ref, a_ref, b_ref, f_ref = refs
        loss = 0.0
        for rows in _sub_tiles(tm):
            xv = x_ref[rows, :]
            hb = (xv * _inv_rms(xv) * gpre_ref[...]).astype(BF16)
            a = _dot_nt(hb, wg_ref[...])
            b = _dot_nt(hb, wu_ref[...])
            hh = (a * jax.nn.sigmoid(a)) * b
            f = _dot(hh.astype(BF16), wd_ref[...])
            xo = xv + 0.5 * (f * _inv_rms(f) * gpost_ref[...])
            a_ref[rows, :] = a.astype(BF16)
            b_ref[rows, :] = b.astype(BF16)
            f_ref[rows, :] = f
            if with_loss:
                e = xo - t_ref[rows, :]
                xo_ref[rows, :] = e * (1.0 / d)
                loss = loss + 0.5 * jnp.sum(jnp.mean(e * e, axis=-1, keepdims=True))
            else:
                xo_ref[rows, :] = xo
        if with_loss:
            _accumulate(loss_ref, loss)

    in_specs = [_rows(tm, d), _const((1, d)), _resident((ff, d)), _resident((ff, d)), _resident((ff, d)), _const((1, d))]
    args = [x, g_pre, wg_t, wu_t, wd, g_post]
    out_shape = [jax.ShapeDtypeStruct((s, d), F32), jax.ShapeDtypeStruct((s, ff), BF16),
                 jax.ShapeDtypeStruct((s, ff), BF16), jax.ShapeDtypeStruct((s, d), F32)]
    out_specs = [_rows(tm, d), _rows(tm, ff), _rows(tm, ff), _rows(tm, d)]
    if with_loss:
        in_specs.append(_rows(tm, d))
        args.append(target)
        out_shape.append(jax.ShapeDtypeStruct((8, 128), F32))
        out_specs.append(_const((8, 128)))
    return _hosted_call(body, riders, name=name, steps=s // tm, in_specs=in_specs, out_specs=out_specs, out_shape=out_shape, args=args)


def _ffn_bwd(dxo, x, f, a, b, g_pre, g_post, wg_t, wu_t, wd, name, riders=None, tm=256):
    s, d = x.shape
    ff = wd.shape[0]

    def body(dxo_ref, x_ref, f_ref, a_ref, b_ref, gpre_ref, gpost_ref, wg_ref, wu_ref, wd_ref,
             dx_ref, hh_ref, da_ref, db_ref, df_ref, h_ref, dgpre_ref, dgpost_ref):
        dgpre_sum = dgpost_sum = 0.0
        for rows in _sub_tiles(tm):
            dxo_v = dxo_ref[rows, :]
            fv = f_ref[rows, :]
            df, dgpost = _rms_bwd(fv, _inv_rms(fv), gpost_ref[...], 0.5 * dxo_v)
            dfb = df.astype(BF16)
            dhh = _dot_nt(dfb, wd_ref[...])
            av = a_ref[rows, :].astype(F32)
            bv = b_ref[rows, :].astype(F32)
            sig = jax.nn.sigmoid(av)
            sa = av * sig
            da = (dhh * bv * (sig * (1.0 + av * (1.0 - sig)))).astype(BF16)
            db = (dhh * sa).astype(BF16)
            dh = _dot(da, wg_ref[...]) + _dot(db, wu_ref[...])
            xv = x_ref[rows, :]
            inv = _inv_rms(xv)
            dxn, dgpre = _rms_bwd(xv, inv, gpre_ref[...], dh)
            dx_ref[rows, :] = dxo_v + dxn
            hh_ref[rows, :] = (sa * bv).astype(BF16)
            da_ref[rows, :] = da
            db_ref[rows, :] = db
            df_ref[rows, :] = dfb
            h_ref[rows, :] = (xv * inv * gpre_ref[...]).astype(BF16)
            dgpre_sum, dgpost_sum = dgpre_sum + dgpre, dgpost_sum + dgpost
        _accumulate(dgpre_ref, dgpre_sum)
        _accumulate(dgpost_ref, dgpost_sum)

    return _hosted_call(
        body, riders, name=name, steps=s // tm,
        in_specs=[_rows(tm, d), _rows(tm, d), _rows(tm, d), _rows(tm, ff), _rows(tm, ff), _const((1, d)), _const((1, d)),
                  _resident((ff, d)), _resident((ff, d)), _resident((ff, d))],
        out_specs=[_rows(tm, d), _rows(tm, ff), _rows(tm, ff), _rows(tm, ff), _rows(tm, d), _rows(tm, d),
                   _const((1, d)), _const((1, d))],
        out_shape=[jax.ShapeDtypeStruct((s, d), F32), jax.ShapeDtypeStruct((s, ff), BF16), jax.ShapeDtypeStruct((s, ff), BF16),
                   jax.ShapeDtypeStruct((s, ff), BF16), jax.ShapeDtypeStruct((s, d), BF16), jax.ShapeDtypeStruct((s, d), BF16),
                   jax.ShapeDtypeStruct((1, d), F32), jax.ShapeDtypeStruct((1, d), F32)],
        args=[dxo, x, f, a, b, g_pre, g_post, wg_t, wu_t, wd])


def _wgrad(lhs, rhs, name, riders=None, rt=256):
    s, r = lhs.shape
    c = rhs.shape[1]

    def body(l_ref, r_ref, o_ref):
        o_ref[...] = _dot_tn(l_ref[...], r_ref[...])

    (out,), riding = _hosted_call(
        body, riders, name=name, steps=pl.cdiv(r, rt), in_specs=[pl.BlockSpec((s, rt), lambda i: (0, i)), _resident((s, c))],
        out_specs=[pl.BlockSpec((rt, c), lambda i: (i, 0))], out_shape=[jax.ShapeDtypeStruct((r, c), F32)], args=[lhs, rhs])
    return out, riding


def _attn_dtype(dilation):
    return BF16 if dilation == 1 else F32


def _in_fwd(x, g, w_in_t, name, riders=None, tm=1024):
    s, d = x.shape
    d_in = w_in_t.shape[0]
    n_groups = len(DILATIONS)
    dtypes = [_attn_dtype(dil) for dil in DILATIONS] * 3

    def body(x_ref, g_ref, w_ref, h_ref, u_ref, *part_refs):
        xv = x_ref[...]
        hb = (xv * _inv_rms(xv) * g_ref[...]).astype(BF16)
        h_ref[...] = hb
        z = _dot_nt(hb, w_ref[...])
        u_ref[...] = z[:, :POOL_DIM]
        for j, ref in enumerate(part_refs):
            part = z[:, POOL_DIM + GROUP_DIM * j:POOL_DIM + GROUP_DIM * (j + 1)]
            ref[...] = (part * _SCORE_SCALE if j < n_groups else part).astype(ref.dtype)

    return _hosted_call(
        body, riders, name=name, steps=s // tm, in_specs=[_rows(tm, d), _const((1, d)), _resident((d_in, d))],
        out_specs=[_rows(tm, d), _rows(tm, POOL_DIM)] + [_rows(tm, GROUP_DIM)] * len(dtypes),
        out_shape=[jax.ShapeDtypeStruct((s, d), BF16), jax.ShapeDtypeStruct((s, POOL_DIM), F32)]
        + [jax.ShapeDtypeStruct((s, GROUP_DIM), dt) for dt in dtypes],
        args=[x, g, w_in_t])


def _in_bwd(du, dparts, x, dxo, g, w_in_t, name, riders=None, tm=512):
    s, d = x.shape
    d_in = w_in_t.shape[0]
    n_parts = len(dparts)

    def body(du_ref, *refs):
        part_refs = refs[:n_parts]
        x_ref, dxo_ref, g_ref, w_ref, dx_ref, dg_ref = refs[n_parts:]
        dh = _dot(jnp.concatenate([r[...] for r in (du_ref,) + part_refs], axis=1), w_ref[...])
        xv = x_ref[...]
        dxn, dg = _rms_bwd(xv, _inv_rms(xv), g_ref[...], dh)
        dx_ref[...] = dxo_ref[...] + dxn
        _accumulate(dg_ref, dg)

    return _hosted_call(
        body, riders, name=name, steps=s // tm,
        in_specs=[_rows(tm, POOL_DIM)] + [_rows(tm, GROUP_DIM)] * n_parts + [_rows(tm, d), _rows(tm, d), _const((1, d)),
                                                                             _resident((d_in, d))],
        out_specs=[_rows(tm, d), _const((1, d))],
        out_shape=[jax.ShapeDtypeStruct((s, d), F32), jax.ShapeDtypeStruct((1, d), F32)],
        args=[du, *dparts, x, dxo, g, w_in_t])


def _wgrad_parts(parts, rhs, name, riders=None):
    n = len(parts)
    s, rt = parts[0].shape
    c = rhs.shape[1]

    def body(*refs):
        part_refs, r_ref, o_ref, buf, sems = refs[:n], refs[n], refs[n + 1], refs[n + 2], refs[n + 3]

        def fetch(i):
            return pltpu.make_async_copy(part_refs[i], buf.at[i % 2], sems.at[i % 2])

        fetch(0).start()
        for i in range(n):
            if i + 1 < n:
                fetch(i + 1).start()
            fetch(i).wait()
            o_ref[pl.ds(i * rt, rt), :] = _dot_tn(buf[i % 2], r_ref[...])

    (out,), riding = _hosted_call(
        body, riders, name=name, steps=1, in_specs=[pl.BlockSpec(memory_space=pl.ANY)] * n + [_resident((s, c))],
        out_specs=[_resident((n * rt, c))], out_shape=[jax.ShapeDtypeStruct((n * rt, c), F32)], args=[*parts, rhs],
        scratch_shapes=[pltpu.VMEM((2, s, rt), BF16), pltpu.SemaphoreType.DMA((2,))])
    return out, riding


_POOL_HALO = 8


def _pool_chain(v, first_shift):
    n = v.shape[0]
    p2 = v + pltpu.roll(v, first_shift, 0)
    p4 = pltpu.roll(p2, 1, 0) + pltpu.roll(p2, n - 1, 0)
    p8 = pltpu.roll(p4, 2, 0) + pltpu.roll(p4, n - 2, 0)
    p16 = pltpu.roll(p8, 4, 0) + pltpu.roll(p8, n - 4, 0)
    group = lax.broadcasted_iota(jnp.int32, v.shape, 1) // HEAD_DIM
    return jnp.where(group == 0, p2, jnp.where(group == 1, p4, jnp.where(group == 2, p8, p16)))


def _pool_count(t0, rows, s):
    t = t0 + lax.broadcasted_iota(jnp.int32, (rows, POOL_DIM), 0)
    group = lax.broadcasted_iota(jnp.int32, (rows, POOL_DIM), 1) // HEAD_DIM
    half = jnp.where(group == 0, 1, jnp.where(group == 1, 2, jnp.where(group == 2, 4, 8)))
    cnt = jnp.minimum(t + half, s) - jnp.maximum(t - half, 0)
    return jnp.maximum(cnt, 1).astype(F32)


def _pad_rows(ref, pad_ref, s):
    zeros = jnp.zeros((_POOL_HALO, pad_ref.shape[1]), pad_ref.dtype)
    pad_ref[pl.ds(0, _POOL_HALO), :] = zeros
    pad_ref[pl.ds(_POOL_HALO + s, _POOL_HALO), :] = zeros
    pad_ref[pl.ds(_POOL_HALO, s), :] = ref[...]


def _pool_fwd(u, w_bd, scale, name, tm=512):
    s = u.shape[0]
    ext = tm + 2 * _POOL_HALO

    def body(u_ref, w_ref, sc_ref, o_ref, upad):
        _pad_rows(u_ref, upad, s)

        def tile(i, carry):
            t0 = pl.multiple_of(i * tm, tm)
            uv = upad[pl.ds(t0, ext), :]
            win = _pool_chain(uv, 1)[_POOL_HALO:_POOL_HALO + tm]
            y = win / _pool_count(t0, tm, s) - uv[_POOL_HALO:_POOL_HALO + tm]
            o_ref[pl.ds(t0, tm), :] = (_dot(y.astype(BF16), w_ref[...]) * sc_ref[...]).astype(BF16)
            return carry

        lax.fori_loop(0, s // tm, tile, 0)

    return pl.pallas_call(body, name=name, out_shape=jax.ShapeDtypeStruct((s, POOL_DIM), BF16),
                          scratch_shapes=[pltpu.VMEM((s + 2 * _POOL_HALO, POOL_DIM), F32)],
                          compiler_params=_params())(u, w_bd, scale)


def _pool_bwd(u, da, w_bd, scale, name, tm=512):
    s = u.shape[0]
    ext = tm + 2 * _POOL_HALO

    def body(u_ref, da_ref, w_ref, sc_ref, du_ref, dw_ref, dsc_ref, upad, dapad):
        _pad_rows(u_ref, upad, s)
        _pad_rows(da_ref, dapad, s)
        dw_ref[...] = jnp.zeros_like(dw_ref)
        dsc_ref[...] = jnp.zeros_like(dsc_ref)

        def tile(i, carry):
            t0 = pl.multiple_of(i * tm, tm)
            uv = upad[pl.ds(t0, ext), :]
            dav = dapad[pl.ds(t0, ext), :]
            win = _pool_chain(uv, 1)[_POOL_HALO:_POOL_HALO + tm]
            yb = (win / _pool_count(t0, tm, s) - uv[_POOL_HALO:_POOL_HALO + tm]).astype(BF16)
            yl = _dot(yb, w_ref[...])
            da_c = dav[_POOL_HALO:_POOL_HALO + tm]
            dsc_ref[...] += jnp.sum(da_c * yl, axis=0, keepdims=True)
            dyl = (dav * sc_ref[...]).astype(BF16)
            dw_ref[...] += _dot_tn(yb, dyl[_POOL_HALO:_POOL_HALO + tm])
            dy = _dot_nt(dyl, w_ref[...])
            dyc = dy / _pool_count(t0 - _POOL_HALO, ext, s)
            du_ref[pl.ds(t0, tm), :] = (_pool_chain(dyc, ext - 1) - dy)[_POOL_HALO:_POOL_HALO + tm].astype(BF16)
            return carry

        lax.fori_loop(0, s // tm, tile, 0)

    pool_cols = pl.BlockSpec((s, POOL_DIM), lambda i: (0, 0), pipeline_mode=pl.Buffered(1))
    return pl.pallas_call(
        body, name=name, grid=(1,),
        in_specs=[pool_cols, pool_cols, _const((POOL_DIM, POOL_DIM)), _const((1, POOL_DIM))],
        out_specs=[_const((s, POOL_DIM)), _const((POOL_DIM, POOL_DIM)), _const((1, POOL_DIM))],
        out_shape=[jax.ShapeDtypeStruct((s, POOL_DIM), BF16), jax.ShapeDtypeStruct((POOL_DIM, POOL_DIM), F32),
                   jax.ShapeDtypeStruct((1, POOL_DIM), F32)],
        scratch_shapes=[pltpu.VMEM((s + 2 * _POOL_HALO, POOL_DIM), F32), pltpu.VMEM((s + 2 * _POOL_HALO, POOL_DIM), F32)],
        compiler_params=_params(dimension_semantics=("arbitrary",)))(u, da, w_bd, scale)


_BQ = 128
_KW = _BQ + 2 * N_SIDE
_PAIR = 2 * HEAD_DIM
_NEG = -1e30
_ATTN_UNROLL = 8
_SCORE_SCALE = HEAD_DIM ** -0.5


def _stack_heads(x):
    lane_head = lax.broadcasted_iota(jnp.int32, x.shape, 1) // HEAD_DIM
    zero = jnp.zeros_like(x)
    return jnp.concatenate([jnp.where(lane_head == 0, x, zero), jnp.where(lane_head == 1, x, zero)], axis=0)


def _unstack_heads(x):
    lane_head = lax.broadcasted_iota(jnp.int32, (_BQ, _PAIR), 1) // HEAD_DIM
    return jnp.where(lane_head == 0, x[:_BQ], x[_BQ:])


def _stack_cols(x):
    return jnp.concatenate([x[:, 0:1], x[:, HEAD_DIM:HEAD_DIM + 1]], axis=0)


def _fill_bias(bias_ref, slopes_ref, dilation):
    row = lax.broadcasted_iota(jnp.int32, (2 * _BQ, _KW), 0)
    col = lax.broadcasted_iota(jnp.int32, (2 * _BQ, _KW), 1)
    pair = 2 * pl.program_id(0)
    slope = jnp.where(row < _BQ, slopes_ref[pair], slopes_ref[pair + 1]) * float(dilation)

    @pl.when(pl.program_id(1) == 0)
    def _():
        for j in range(3):
            dist = jnp.abs(col - (row & (_BQ - 1)) - j * N_SIDE)
            bias_ref[j] = jnp.where(dist <= N_SIDE, -slope * dist.astype(F32), _NEG)


def _block_window(i, n_blocks, length):
    q0 = pl.multiple_of(i * _BQ, _BQ)
    ws = pl.multiple_of(jnp.clip(q0 - N_SIDE, 0, length - _KW), N_SIDE)
    return q0, ws, jnp.where(i == 0, 0, jnp.where(i == n_blocks - 1, 2, 1))


_FREE_STRIDE = 4


def _residues_per_step(dilation):
    return 2 if dilation > _FREE_STRIDE else 1


def _residue_views(dilation, seq, ins, outs, tmps):
    step = pl.program_id(1)
    if dilation <= _FREE_STRIDE:
        def rows(start, count, sub=0):
            return pl.ds(start, count) if dilation == 1 else pl.ds(start * dilation + step, count, stride=dilation)

        return ins, outs, rows, lambda: None
    inner, per = dilation // _FREE_STRIDE, _residues_per_step(dilation)
    assert inner <= _FREE_STRIDE and inner % per == 0 and len(tmps) == len(ins) + len(outs)
    first, turn = step // (inner // per), step % (inner // per)
    coarse = pl.ds(first, seq // _FREE_STRIDE, stride=_FREE_STRIDE)
    in_tmps, out_tmps = tmps[:len(ins)], tmps[len(ins):]

    @pl.when(turn == 0)
    def _():
        for ref, tmp in zip(ins, in_tmps):
            tmp[...] = ref[coarse, :]

    def flush():
        @pl.when(turn == inner // per - 1)
        def _():
            for ref, tmp in zip(outs, out_tmps):
                ref[coarse, :] = tmp[...]

    return in_tmps, out_tmps, lambda start, count, sub=0: pl.ds(start * inner + turn * per + sub, count, stride=inner), flush


def _of_sub(ref, sub):
    return ref.at[sub] if len(ref.shape) == 3 else ref


def _attn_call(body, name, dilation, seq, n_in, out_dtypes, scratch, buffers):
    col = pl.BlockSpec((seq, _PAIR), lambda c, r: (0, c), pipeline_mode=pl.Buffered(buffers))
    tmps = [pltpu.VMEM((seq // _FREE_STRIDE, _PAIR), F32)] * (n_in + len(out_dtypes) if dilation > _FREE_STRIDE else 0)
    return pl.pallas_call(
        body, name=name, grid=(GROUP_DIM // _PAIR, dilation // _residues_per_step(dilation)),
        in_specs=[pl.BlockSpec(memory_space=pltpu.SMEM)] + [col] * n_in, out_specs=[col] * len(out_dtypes),
        out_shape=[jax.ShapeDtypeStruct((seq, GROUP_DIM), dt) for dt in out_dtypes], scratch_shapes=scratch + tmps,
        compiler_params=_params(dimension_semantics=("arbitrary", "arbitrary")))


def _staged(dilation, length, rows, sources, scratch):
    if dilation == 1:
        return sources
    for src, dst in zip(sources, scratch):
        for sub in range(_residues_per_step(dilation)):
            dst[sub] = src[rows(0, length, sub), :].astype(BF16)
    return scratch


def _sub_and_block(i, dilation, n_blocks):
    return (0, i) if _residues_per_step(dilation) == 1 else (i // n_blocks, i % n_blocks)


def _attn_fwd(q, k, v, slopes, dilation, name):
    seq = q.shape[0]
    length = seq // dilation
    n_blocks = length // _BQ
    n_stage = 0 if dilation == 1 else 3

    def body(sl_ref, q_ref, k_ref, v_ref, o_ref, lse_ref, *scratch):
        bias_ref, tmps = scratch[n_stage], scratch[n_stage + 1:]
        (q_in, k_in, v_in), (o_out, lse_out), rows, flush = _residue_views(dilation, seq, (q_ref, k_ref, v_ref), (o_ref, lse_ref), tmps)
        qs, ks, vs = _staged(dilation, length, rows, (q_in, k_in, v_in), scratch[:n_stage])
        _fill_bias(bias_ref, sl_ref, dilation)

        def block(i, carry):
            sub, j = _sub_and_block(i, dilation, n_blocks)
            q0, ws, which = _block_window(j, n_blocks, length)
            kw = _of_sub(ks, sub)[pl.ds(ws, _KW), :]
            vw = _of_sub(vs, sub)[pl.ds(ws, _KW), :]
            sc = _dot_nt(_stack_heads(_of_sub(qs, sub)[pl.ds(q0, _BQ), :]), kw) + bias_ref[which]
            m = jnp.max(sc, axis=-1, keepdims=True)
            p = jnp.exp(sc - m)
            den = jnp.sum(p, axis=-1, keepdims=True)
            o_out[rows(q0, _BQ, sub), :] = _unstack_heads(_dot(p.astype(BF16), vw) / den)
            lse_out[rows(q0, _BQ, sub), :] = _unstack_heads(jnp.broadcast_to(m + jnp.log(den), (2 * _BQ, _PAIR)))
            return carry

        lax.fori_loop(0, trips, block, 0, unroll=min(_ATTN_UNROLL, trips))
        flush()

    trips = _residues_per_step(dilation) * n_blocks
    stage = pltpu.VMEM((_residues_per_step(dilation), length, _PAIR), BF16)
    bias = pltpu.VMEM((3, 2 * _BQ, _KW), F32)
    return _attn_call(body, name, dilation, seq, 3, [F32, F32], [stage] * n_stage + [bias], 2)(slopes, q, k, v)


def _attn_bwd(q, k, v, do, lse, cterm, slopes, dilation, name):
    seq = q.shape[0]
    length = seq // dilation
    n_blocks = length // _BQ
    n_stage, n_whole = (0, 0) if dilation == 1 else (4, 3)

    def body(sl_ref, q_ref, k_ref, v_ref, do_ref, lse_ref, c_ref, dq_ref, dk_ref, dv_ref, *scratch):
        dk_acc, dv_acc, bias_ref = scratch[n_stage:n_stage + 3]
        whole, tmps = scratch[n_stage + 3:n_stage + 3 + n_whole], scratch[n_stage + 3 + n_whole:]
        (q_in, k_in, v_in, do_in, lse_in, c_in), (dq_out, dk_out, dv_out), rows, flush = _residue_views(
            dilation, seq, (q_ref, k_ref, v_ref, do_ref, lse_ref, c_ref), whole or (dq_ref, dk_ref, dv_ref), tmps)
        qs, ks, vs, dos = _staged(dilation, length, rows, (q_in, k_in, v_in, do_in), scratch[:n_stage])
        dk_acc[...] = jnp.zeros_like(dk_acc)
        dv_acc[...] = jnp.zeros_like(dv_acc)
        _fill_bias(bias_ref, sl_ref, dilation)

        def block(i, carry):
            sub, j = _sub_and_block(i, dilation, n_blocks)
            q0, ws, which = _block_window(j, n_blocks, length)
            qm = _stack_heads(_of_sub(qs, sub)[pl.ds(q0, _BQ), :])
            dom = _stack_heads(_of_sub(dos, sub)[pl.ds(q0, _BQ), :])
            kw = _of_sub(ks, sub)[pl.ds(ws, _KW), :]
            vw = _of_sub(vs, sub)[pl.ds(ws, _KW), :]
            p = jnp.exp(_dot_nt(qm, kw) + bias_ref[which] - _stack_cols(lse_in[rows(q0, _BQ, sub), :]))
            ds = (p * (_dot_nt(dom, vw) + _stack_cols(c_in[rows(q0, _BQ, sub), :]))).astype(BF16)
            dq_out[rows(q0, _BQ, sub), :] = (_unstack_heads(_dot(ds, kw)) * _SCORE_SCALE).astype(dq_out.dtype)
            dk_acc[sub, pl.ds(ws, _KW), :] += _dot_tn(ds, qm)
            dv_acc[sub, pl.ds(ws, _KW), :] += _dot_tn(p.astype(BF16), dom)
            return carry

        lax.fori_loop(0, trips, block, 0, unroll=min(_ATTN_UNROLL, trips))
        for sub in range(per):
            dk_out[rows(0, length, sub), :] = dk_acc[sub].astype(dk_out.dtype)
            dv_out[rows(0, length, sub), :] = dv_acc[sub].astype(dv_out.dtype)
        flush()
        if whole:
            @pl.when(pl.program_id(1) == dilation // per - 1)
            def _():
                for ref, collected in zip((dq_ref, dk_ref, dv_ref), whole):
                    ref[...] = collected[...].astype(BF16)

    per = _residues_per_step(dilation)
    trips = per * n_blocks
    stage = pltpu.VMEM((per, length, _PAIR), BF16)
    acc = pltpu.VMEM((per, length, _PAIR), F32)
    bias = pltpu.VMEM((3, 2 * _BQ, _KW), F32)
    collect = pltpu.VMEM((seq, _PAIR), F32)
    return _attn_call(body, name, dilation, seq, 6, [BF16] * 3, [stage] * n_stage + [acc] * 2 + [bias] + [collect] * n_whole,
                      2 if dilation == 1 else 1)(slopes, q, k, v, do, lse, cterm)


def _group_weights(lses):
    m = jnp.maximum(jnp.maximum(lses[0], lses[1]), lses[2])
    es = [jnp.exp(l - m) for l in lses]
    den = es[0] + es[1] + es[2]
    return [e / den for e in es]


def _out_fwd(a_pool, outs, lses, x, w_out, g, name, tm=1024):
    s, d = x.shape
    width = POOL_DIM + 3 * GROUP_DIM

    def body(ap_ref, o0, o1, o2, l0, l1, l2, x_ref, w_ref, g_ref, xo_ref, cat_ref):
        alphas = _group_weights([l0[...], l1[...], l2[...]])
        cat = jnp.concatenate([ap_ref[...]] + [(o[...] * al).astype(BF16) for o, al in zip((o0, o1, o2), alphas)], axis=1)
        cat_ref[...] = cat
        mix = _dot(cat, w_ref[...])
        xo_ref[...] = x_ref[...] + mix * _inv_rms(mix) * g_ref[...]

    return pl.pallas_call(
        body, name=name, grid=(s // tm,),
        in_specs=[_rows(tm, POOL_DIM)] + [_rows(tm, GROUP_DIM)] * 6 + [_rows(tm, d), _resident(w_out.shape), _const((1, d))],
        out_specs=[_rows(tm, d), _rows(tm, width)],
        out_shape=[jax.ShapeDtypeStruct((s, d), F32), jax.ShapeDtypeStruct((s, width), BF16)],
        compiler_params=_params(dimension_semantics=("arbitrary",)))(a_pool, *outs, *lses, x, w_out, g)


def _out_bwd(dxo, cat, outs, lses, w_out, g, head_ones, name, tm=1024):
    s, d = dxo.shape

    def body(dxo_ref, cat_ref, o0, o1, o2, l0, l1, l2, w_ref, g_ref, ones_ref, dpool_ref, dmix_ref, do0, do1, do2, c0, c1, c2, dg_ref):
        mv = _dot(cat_ref[...], w_ref[...])
        dmix, dg = _rms_bwd(mv, _inv_rms(mv), g_ref[...], dxo_ref[...])
        dmb = dmix.astype(BF16)
        dmix_ref[...] = dmb
        _accumulate(dg_ref, dg)
        dcat = _dot_nt(dmb, w_ref[...])
        dpool_ref[...] = dcat[:, :POOL_DIM]
        alphas = _group_weights([l0[...], l1[...], l2[...]])
        das = [dcat[:, POOL_DIM + GROUP_DIM * j:POOL_DIM + GROUP_DIM * (j + 1)] for j in range(3)]
        prod = sum(da * (o[...] * al) for da, o, al in zip(das, (o0, o1, o2), alphas))
        hi = prod.astype(BF16)
        lo = (prod - hi.astype(F32)).astype(BF16)
        total = _dot(hi, ones_ref[...]) + _dot(lo, ones_ref[...])
        for da, al, do_ref, c_ref in zip(das, alphas, (do0, do1, do2), (c0, c1, c2)):
            do_ref[...] = (da * al).astype(do_ref.dtype)
            c_ref[...] = -al * total

    return pl.pallas_call(
        body, name=name, grid=(s // tm,),
        in_specs=[_rows(tm, d), _rows(tm, cat.shape[1])] + [_rows(tm, GROUP_DIM)] * 6 + [_resident(w_out.shape), _const((1, d)),
                                                                                        _const((GROUP_DIM, GROUP_DIM))],
        out_specs=[_rows(tm, POOL_DIM), _rows(tm, d)] + [_rows(tm, GROUP_DIM)] * 6 + [_const((1, d))],
        out_shape=[jax.ShapeDtypeStruct((s, POOL_DIM), F32), jax.ShapeDtypeStruct((s, d), BF16)]
        + [jax.ShapeDtypeStruct((s, GROUP_DIM), _attn_dtype(dil)) for dil in DILATIONS]
        + [jax.ShapeDtypeStruct((s, GROUP_DIM), F32)] * 3 + [jax.ShapeDtypeStruct((1, d), F32)],
        compiler_params=_params(dimension_semantics=("arbitrary",)))(dxo, cat, *outs, *lses, w_out, g, head_ones)


def _alibi_slopes():
    return np.array([2.0 ** (-8.0 * (i + 1) / N_ATTN_HEADS) for i in range(N_ATTN_HEADS)], np.float32)


def _block_diag(w_lin):
    n, c, _ = w_lin.shape
    eye = jnp.eye(n, dtype=w_lin.dtype)
    return (eye[:, None, :, None] * w_lin[:, :, None, :]).reshape(n * c, n * c)


class _NoExchange:
    def __init__(self, full):
        self.full, self.grads = full, {}

    def first_weights(self):
        return self.full

    def riders(self, host):
        return []

    def landed(self, host, results):
        return self.full

    def gradient(self, name, grad):
        self.grads[name] = grad


def _local_step(x, target, small, exchange):
    s, d = x.shape
    slopes = _alibi_slopes()
    group_slopes = [jnp.asarray(slopes[4 * g:4 * g + 4]) for g in range(3)]
    w_bd = _block_diag(small["w_pool_lin"]).astype(BF16)
    head_ones = jnp.asarray(np.kron(np.eye(GROUP_DIM // HEAD_DIM), np.ones((HEAD_DIM, HEAD_DIM))), BF16)

    full = dict(exchange.first_weights())

    def hosted(call, host, *args):
        results, riding = call(*args, host, exchange.riders(host))
        full.update(exchange.landed(host, riding) or {})
        return results

    x1, a1, b1, f1 = hosted(_ffn_fwd, "ffn1_fwd", x, small["g_ffn1_pre"], full["w1_gate"], full["w1_up"], full["w1_down"],
                            small["g_ffn1_post"], None)
    h2, u, *parts = hosted(_in_fwd, "in_fwd", x1, small["g_mix_pre"], full["w_in"])
    qs, ks, vs = parts[0:3], parts[3:6], parts[6:9]
    a_pool = _pool_fwd(u, w_bd, small["pool_scale"], "pool_fwd")
    outs, lses = [], []
    for g, dil in enumerate(DILATIONS):
        o, lse = _attn_fwd(qs[g], ks[g], vs[g], group_slopes[g], dil, f"attn_fwd{g}")
        outs.append(o)
        lses.append(lse)
    x2, cat = _out_fwd(a_pool, outs, lses, x1, full["w_out"], small["g_mix_post"], "out_fwd")
    (dx3, a2, b2, f2, loss_part), _ = _ffn_fwd(x2, small["g_ffn2_pre"], full["w2_gate"], full["w2_up"], full["w2_down"],
                                               small["g_ffn2_post"], target, "ffn2_fwd")

    small_grads = {}

    def ffn_backward(tag, dxo, x_in, f, a, b):
        n = tag[-1]
        dx, hh, da, db, df, h, dg_pre, dg_post = hosted(
            _ffn_bwd, f"{tag}_bwd", dxo, x_in, f, a, b, small[f"g_{tag}_pre"], small[f"g_{tag}_post"],
            full[f"w{n}_gate"], full[f"w{n}_up"], full[f"w{n}_down"])
        for part, lhs, rhs in (("down", hh, df), ("gate", da, h), ("up", db, h)):
            exchange.gradient(f"w{n}_{part}", hosted(_wgrad, f"{tag}_wgrad_{part}", lhs, rhs))
        small_grads[f"g_{tag}_pre"], small_grads[f"g_{tag}_post"] = dg_pre, dg_post
        return dx

    dx2 = ffn_backward("ffn2", dx3, x2, f2, a2, b2)
    dpool, dmix, *dos_cs, small_grads["g_mix_post"] = _out_bwd(dx2, cat, outs, lses, full["w_out"], small["g_mix_post"],
                                                               head_ones, "out_bwd")
    dos, cs = dos_cs[:3], dos_cs[3:]
    dqs, dks, dvs = [], [], []
    for g, dil in enumerate(DILATIONS):
        dq, dk, dv = _attn_bwd(qs[g], ks[g], vs[g], dos[g], lses[g], cs[g], group_slopes[g], dil, f"attn_bwd{g}")
        dqs.append(dq)
        dks.append(dk)
        dvs.append(dv)
    du, dw_bd, small_grads["pool_scale"] = _pool_bwd(u, dpool, w_bd, small["pool_scale"], "pool_bwd")
    n_pool = len(POOL_HALF_WINDOWS)
    small_grads["w_pool_lin"] = jnp.stack(
        [dw_bd[HEAD_DIM * g:HEAD_DIM * (g + 1), HEAD_DIM * g:HEAD_DIM * (g + 1)] for g in range(n_pool)])
    dz_parts = dqs + dks + dvs
    dx1, small_grads["g_mix_pre"] = hosted(_in_bwd, "in_bwd", du, dz_parts, x1, dx2, small["g_mix_pre"], full["w_in"])
    exchange.gradient("w_in", hosted(_wgrad_parts, "wgrad_in", [du] + dz_parts, h2))
    dx0 = ffn_backward("ffn1", dx1, x, f1, a1, b1)
    exchange.gradient("w_out", hosted(_wgrad, "wgrad_out", cat, dmix))
    return loss_part[0, 0], dx0, small_grads


SEGMENTS = ("w1_gate", "w1_up", "w1_down", "w_in", "w_out", "w2_gate", "w2_up", "w2_down")
TRANSPOSED = ("w1_gate", "w1_up", "w_in", "w2_gate", "w2_up")
ROWS_OUTSIDE = ("w1_gate", "w1_up", "w2_gate", "w2_up")
HALF = 512


def _place():
    x, y, c = lax.axis_index("x"), lax.axis_index("y"), lax.axis_index("c")
    other_chips = [(1 - x, y), (x, 1 - y), (1 - x, 1 - y)]
    return x, y, c, other_chips


def _chip_rows(chip, rows):
    return pl.ds(pl.multiple_of((2 * chip[0] + chip[1]) * rows, 16), rows)


def _cols(c):
    return pl.ds(pl.multiple_of(c * HALF, HALF), HALF)


def _cast_shards(shards, transposed, place, name):
    n = len(shards)
    rows = [w.shape[1] if t else w.shape[0] for w, t in zip(shards, transposed)]

    def body(place_ref, *refs):
        for w_ref, o_ref, t in zip(refs[:n], refs[n:], transposed):
            o_ref[...] = (w_ref[...].T if t else w_ref[...]).astype(BF16)

    once = pl.Buffered(1)
    return pl.pallas_call(
        body, name=name,
        grid_spec=pltpu.PrefetchScalarGridSpec(
            num_scalar_prefetch=1, grid=(1,),
            in_specs=[pl.BlockSpec(w.shape, lambda i, place: (0, 0), pipeline_mode=once) for w in shards],
            out_specs=[pl.BlockSpec((r, 1024), lambda i, place: (place[0], 0), pipeline_mode=once) for r in rows]),
        out_shape=[jax.ShapeDtypeStruct((N_CHIPS * r, 1024), BF16) for r in rows],
        compiler_params=_params(dimension_semantics=("arbitrary",)))(place, *shards)


def _gather_weights(bufs):
    n = len(bufs)
    rows = [b.shape[0] // N_CHIPS for b in bufs]

    def halves(r):
        first = -(-r // 32) * 16
        return (0, first), (first, r - first)

    def body(*refs):
        outs = refs[n:2 * n]
        ici_send, ici_recv, d2d_send, d2d_recv = refs[2 * n:]
        x, y, c, _ = _place()
        me, via_x, via_y, diagonal = (x, y), (1 - x, y), (x, 1 - y), (1 - x, 1 - y)

        def piece(chip, k, h, cols):
            start, size = halves(rows[k])[h]
            return outs[k].at[pl.ds(pl.multiple_of((2 * chip[0] + chip[1]) * rows[k] + start, 16), size), _cols(cols)]

        def ici(path, chip, k, h, to):
            blk = piece(chip, k, h, c)
            return pltpu.make_async_remote_copy(src_ref=blk, dst_ref=blk, send_sem=ici_send.at[path, k, h],
                                                recv_sem=ici_recv.at[path, k, h], device_id=(*to, c), device_id_type=MESH)

        def d2d(slot, chip, k, h, cols):
            blk = piece(chip, k, h, cols)
            return pltpu.make_async_remote_copy(src_ref=blk, dst_ref=blk, send_sem=d2d_send.at[slot, k, h],
                                                recv_sem=d2d_recv.at[slot, k, h], device_id=(x, y, 1 - c), device_id_type=MESH)

        started = [ici(0, me, k, h, via_x) for h in (0, 1) for k in range(n)] + [ici(1, me, k, h, via_y) for h in (1, 0) for k in range(n)]
        for cp in started:
            cp.start()

        def landed(path, slot, chip, k, h, pass_on_to=None):
            ici(path, chip, k, h, me).wait_recv()
            more = [d2d(slot, chip, k, h, c)] + ([ici(2, chip, k, h, pass_on_to)] if pass_on_to else [])
            for cp in more:
                cp.start()
            started.extend(more)

        for k in range(n):
            landed(0, 0, via_x, k, 0, pass_on_to=via_y)
            landed(1, 1, via_y, k, 1, pass_on_to=via_x)
        for k in range(n):
            landed(0, 0, via_x, k, 1)
            landed(1, 1, via_y, k, 0)
        for k in range(n):
            for h in range(2):
                landed(2, 2, diagonal, k, h)
        for slot, chip in enumerate((via_x, via_y, diagonal)):
            for k in range(n):
                for h in range(2):
                    d2d(slot, chip, k, h, 1 - c).wait_recv()
        for cp in started:
            cp.wait_send()

    any_spec = pl.BlockSpec(memory_space=pl.ANY)
    return pl.pallas_call(
        body, name="gather_weights", in_specs=[any_spec] * n, out_specs=[any_spec] * n,
        out_shape=[jax.ShapeDtypeStruct(b.shape, b.dtype) for b in bufs], input_output_aliases={k: k for k in range(n)},
        scratch_shapes=[pltpu.SemaphoreType.DMA((3, n, 2))] * 4)(*bufs)


def _gather_rider(bufs):
    n = len(bufs)
    rows = [b.shape[0] // N_CHIPS for b in bufs]

    def copies(outs, send_sems, recv_sems, inbound):
        x, y, c, chips = _place()
        for j, chip in enumerate(chips):
            for k in range(n):
                src_chip = chip if inbound else (x, y)
                blk = outs[k].at[_chip_rows(src_chip, rows[k]), _cols(c)]
                yield pltpu.make_async_remote_copy(src_ref=blk, dst_ref=blk, send_sem=send_sems.at[j, k], recv_sem=recv_sems.at[j, k],
                                                   device_id=(*chip, c), device_id_type=MESH)

    def start(ins, outs, send_sems, recv_sems):
        for cp in copies(outs, send_sems, recv_sems, False):
            cp.start()

    def wait(ins, outs, send_sems, recv_sems):
        for cp in copies(outs, send_sems, recv_sems, True):
            cp.wait_recv()
        for cp in copies(outs, send_sems, recv_sems, False):
            cp.wait_send()

    return _Rider(list(bufs), None, (3, n), start, wait)


def _forward_rider(bufs):
    n = len(bufs)
    rows = [b.shape[0] // N_CHIPS for b in bufs]

    def copies(outs, send_sems, recv_sems, half):
        x, y, c, chips = _place()
        for j, chip in enumerate(chips):
            for k in range(n):
                blk = outs[k].at[_chip_rows(chip, rows[k]), _cols(half(c))]
                yield pltpu.make_async_remote_copy(src_ref=blk, dst_ref=blk, send_sem=send_sems.at[j, k], recv_sem=recv_sems.at[j, k],
                                                   device_id=(x, y, 1 - c), device_id_type=MESH)

    def start(ins, outs, send_sems, recv_sems):
        for cp in copies(outs, send_sems, recv_sems, lambda c: c):
            cp.start()

    def wait(ins, outs, send_sems, recv_sems):
        for cp in copies(outs, send_sems, recv_sems, lambda c: 1 - c):
            cp.wait_recv()
        for cp in copies(outs, send_sems, recv_sems, lambda c: c):
            cp.wait_send()

    return _Rider(list(bufs), None, (3, n), start, wait)


def _sibling_rider(grads):
    n = len(grads)

    def copies(ins, outs, send_sems, recv_sems):
        x, y, c, _ = _place()
        return [pltpu.make_async_remote_copy(src_ref=ins[k].at[:, pl.ds(1 - c, 1)], dst_ref=outs[k], send_sem=send_sems.at[k],
                                             recv_sem=recv_sems.at[k], device_id=(x, y, 1 - c), device_id_type=MESH)
                for k in range(n)]

    def start(*refs):
        for cp in copies(*refs):
            cp.start()

    def wait(*refs):
        for cp in copies(*refs):
            cp.wait()

    return _Rider(list(grads), [jax.ShapeDtypeStruct((N_CHIPS, 1) + g.shape[2:], F32) for g in grads], (n,), start, wait)


def _alone(rider, name):
    n = len(rider.operands)
    landing = rider.landing if rider.landing is not None else [jax.ShapeDtypeStruct(a.shape, a.dtype) for a in rider.operands]
    n_out = len(landing)

    def body(*refs):
        rider.start(refs[:n], refs[n:n + n_out], *refs[n + n_out:])
        rider.wait(refs[:n], refs[n:n + n_out], *refs[n + n_out:])

    any_spec = pl.BlockSpec(memory_space=pl.ANY)
    return pl.pallas_call(body, name=name, in_specs=[any_spec] * n, out_specs=[any_spec] * n_out, out_shape=landing,
                          input_output_aliases={i: i for i in range(n)} if rider.landing is None else {},
                          scratch_shapes=[pltpu.SemaphoreType.DMA(rider.sems)] * 2)(*rider.operands)


def _chip_sum(grad, from_sibling, place, name):
    rh, width = grad.shape[2:]

    def body(place_ref, g_ref, s_ref, own_ref, all_ref):
        all_ref[...] = (g_ref[...] + s_ref[...]).astype(BF16)
        mine = place_ref[0]
        own_ref[0] = g_ref[mine, 0] + s_ref[mine, 0]

    blk = (N_CHIPS, 1, rh, width)
    once = pl.Buffered(1)
    return pl.pallas_call(
        body, name=name,
        grid_spec=pltpu.PrefetchScalarGridSpec(
            num_scalar_prefetch=1, grid=(1,),
            in_specs=[pl.BlockSpec(blk, lambda i, place: (0, place[1], 0, 0), pipeline_mode=once),
                      pl.BlockSpec(blk, lambda i, place: (0, 0, 0, 0), pipeline_mode=once)],
            out_specs=[pl.BlockSpec((1, rh, width), lambda i, place: (0, 0, 0), pipeline_mode=once),
                       pl.BlockSpec(blk, lambda i, place: (0, 0, 0, 0), pipeline_mode=once)]),
        out_shape=[jax.ShapeDtypeStruct((1, rh, width), F32), jax.ShapeDtypeStruct((N_CHIPS, 1, rh, width), BF16)],
        compiler_params=_params(dimension_semantics=("arbitrary",)))(place, grad, from_sibling)


def _scatter_rider(sums):
    n = len(sums)

    def copies(ins, outs, send_sems, recv_sems):
        x, y, c, chips = _place()
        return [pltpu.make_async_remote_copy(src_ref=ins[k].at[pl.ds(2 * chip[0] + chip[1], 1)], dst_ref=outs[k].at[pl.ds(j, 1)],
                                             send_sem=send_sems.at[j, k], recv_sem=recv_sems.at[j, k],
                                             device_id=(*chip, c), device_id_type=MESH)
                for j, chip in enumerate(chips) for k in range(n)]

    def start(*refs):
        for cp in copies(*refs):
            cp.start()

    def wait(*refs):
        for cp in copies(*refs):
            cp.wait()

    return _Rider(list(sums), [jax.ShapeDtypeStruct((3,) + sm.shape[1:], BF16) for sm in sums], (3, n), start, wait)


def _total_sums(owns, received, name):
    n = len(owns)

    def body(*refs):
        for o_ref, r_ref, t_ref in zip(refs[:n], refs[n:2 * n], refs[2 * n:]):
            total = o_ref[0]
            for j in range(3):
                total = total + r_ref[j, 0].astype(F32)
            t_ref[0] = total

    return _hosted_call(body, None, name=name, steps=1, in_specs=[_resident(a.shape) for a in owns + received],
                        out_specs=[_resident(o.shape) for o in owns], out_shape=[jax.ShapeDtypeStruct(o.shape, F32) for o in owns],
                        args=owns + received)[0]


def _swap_rider(halves):
    n = len(halves)

    def copies(ins, outs, send_sems, recv_sems):
        x, y, c, _ = _place()
        return [pltpu.make_async_remote_copy(src_ref=ins[k], dst_ref=outs[k], send_sem=send_sems.at[k], recv_sem=recv_sems.at[k],
                                             device_id=(x, y, 1 - c), device_id_type=MESH) for k in range(n)]

    def start(*refs):
        for cp in copies(*refs):
            cp.start()

    def wait(*refs):
        for cp in copies(*refs):
            cp.wait()

    return _Rider(list(halves), [jax.ShapeDtypeStruct(h.shape, F32) for h in halves], (n,), start, wait)


N_DEV = 8


def _gather_small(block):
    m_per, width = block.shape

    def body(x_ref, out_ref, send_sems, recv_sems, local_sem):
        x, y, c, chips = _place()
        me, sibling = (x, y, c), (x, y, 1 - c)

        def rows(px, py, pc):
            return out_ref.at[pl.ds((4 * px + 2 * py + pc) * m_per, m_per), :]

        def copy(k, blk, to, src=None):
            return pltpu.make_async_remote_copy(src_ref=rows(*blk) if src is None else src, dst_ref=rows(*blk),
                                                send_sem=send_sems.at[k], recv_sem=recv_sems.at[k], device_id=to, device_id_type=MESH)

        mine = pltpu.make_async_copy(x_ref, rows(*me), local_sem)
        mine.start()
        first = [copy(0, me, sibling, src=x_ref)] + [copy(1 + j, me, (*chip, c), src=x_ref) for j, chip in enumerate(chips)]
        for cp in first:
            cp.start()
        passed = [copy(4 + j, (*chip, c), sibling) for j, chip in enumerate(chips)]
        for j, chip in enumerate(chips):
            copy(1 + j, (*chip, c), me).wait_recv()
            passed[j].start()
        copy(0, sibling, me).wait_recv()
        for j, chip in enumerate(chips):
            copy(4 + j, (*chip, 1 - c), me).wait_recv()
        for cp in first + passed:
            cp.wait_send()
        mine.wait()

    vmem = pl.BlockSpec(memory_space=pltpu.VMEM)
    return pl.pallas_call(body, name="gather_small", out_shape=jax.ShapeDtypeStruct((N_DEV * m_per, width), F32),
                          in_specs=[vmem], out_specs=vmem,
                          scratch_shapes=[pltpu.SemaphoreType.DMA((7,)), pltpu.SemaphoreType.DMA((7,)),
                                          pltpu.SemaphoreType.DMA])(block)


def _adamw_math(w, g, m, v):
    m = ADAM_B1 * m + (1.0 - ADAM_B1) * g
    v = ADAM_B2 * v + (1.0 - ADAM_B2) * (g * g)
    m_hat = m / (1.0 - ADAM_B1 ** ADAM_STEP)
    v_hat = v / (1.0 - ADAM_B2 ** ADAM_STEP)
    delta = -ADAM_LR * (m_hat / (jnp.sqrt(v_hat) + ADAM_EPS) + ADAM_WD * w)
    return delta, m, v


def _adamw(w, mine, siblings, place, m, v, transposed, name):
    rh, width = mine.shape[1:]
    place_spec = pl.BlockSpec(memory_space=pltpu.SMEM)
    halves = [_const((1, rh, width))] * 2
    out_shape = [jax.ShapeDtypeStruct(w.shape, F32)] * 4
    if transposed:
        def body(place_ref, w_ref, mine_ref, sib_ref, m_ref, v_ref, go_ref, d_ref, mo_ref, vo_ref):
            first = place_ref[1] == 0
            g = jnp.concatenate([jnp.where(first, mine_ref[0], sib_ref[0]), jnp.where(first, sib_ref[0], mine_ref[0])], axis=0).T
            go_ref[...] = g
            d_ref[...], mo_ref[...], vo_ref[...] = _adamw_math(w_ref[...], g, m_ref[...], v_ref[...])

        whole = _resident(w.shape)
        return _hosted_call(body, None, name=name, steps=1, in_specs=[place_spec, whole] + halves + [whole, whole],
                            out_specs=[whole] * 4, out_shape=out_shape, args=[place, w, mine, siblings, m, v])[0]

    def body(place_ref, w_ref, mine_ref, sib_ref, m_ref, v_ref, go_ref, d_ref, mo_ref, vo_ref):
        g = jnp.where(pl.program_id(0) == place_ref[1], mine_ref[0], sib_ref[0])
        go_ref[...] = g
        d_ref[...], mo_ref[...], vo_ref[...] = _adamw_math(w_ref[...], g, m_ref[...], v_ref[...])

    half = _rows(rh, width)
    return _hosted_call(body, None, name=name, steps=2, in_specs=[place_spec, half] + halves + [half, half],
                        out_specs=[half] * 4, out_shape=out_shape, args=[place, w, mine, siblings, m, v])[0]


def _adamw_small(gathered, w, m, v, name):
    def body(ga_ref, w_ref, m_ref, v_ref, go_ref, d_ref, mo_ref, vo_ref):
        g = ga_ref[0]
        for dev in range(1, N_DEV):
            g = g + ga_ref[dev]
        go_ref[...] = g
        d_ref[...], mo_ref[...], vo_ref[...] = _adamw_math(w_ref[...], g, m_ref[...], v_ref[...])

    return pl.pallas_call(body, name=name, out_shape=[jax.ShapeDtypeStruct(w.shape, F32)] * 4,
                          compiler_params=_params())(gathered, w, m, v)


class _Exchange:
    FIRST = ("w1_gate", "w1_up", "w1_down")
    HOSTS = {"ffn2_wgrad_gate": (("w2_down",), ()), "ffn2_wgrad_up": (("w2_gate",), ("w2_down",)),
             "in_bwd": (("w2_up",), ("w2_gate",)), "wgrad_in": ((), ("w2_up",)),
             "ffn1_wgrad_down": ((), ("w_in",)), "ffn1_wgrad_gate": (("w1_down",), ()), "ffn1_wgrad_up": ((), ("w1_down", "w1_gate")),
             "wgrad_out": ((), ("w1_up",))}
    ALONE = ("w_in", "w1_gate", "w1_up", "w_out")

    def __init__(self, bufs, place):
        self.bufs, self.place = bufs, place
        self.later = [k for k in SEGMENTS if k not in self.FIRST]
        self.split, self.own, self.to_send, self.received = {}, {}, {}, {}

    def first_weights(self):
        return dict(zip(self.FIRST, _gather_weights([self.bufs[k] for k in self.FIRST])))

    def riders(self, host):
        if host == "ffn1_fwd":
            return [_gather_rider([self.bufs[k] for k in self.later])]
        if host == "in_fwd":
            return [_forward_rider([self.bufs[k] for k in self.later[1:]])]
        halves, sums = self.HOSTS.get(host, ((), ()))
        return ([_sibling_rider([self.split[k] for k in halves])] if halves else []) + (
            [_scatter_rider([self.to_send[k] for k in sums])] if sums else [])

    def landed(self, host, results):
        if host == "ffn1_fwd":
            self.bufs.update(zip(self.later, results[0]))
            return dict(zip(self.later[:1], _alone(_forward_rider([self.bufs[self.later[0]]]), "gather_forward_first")))
        if host == "in_fwd":
            return dict(zip(self.later[1:], results[0]))
        halves, sums = self.HOSTS.get(host, ((), ()))
        if halves:
            self._chip_sums(halves, results[0])
        if sums:
            self.received.update(zip(sums, results[-1]))

    def gradient(self, name, grad):
        self.split[name] = grad.reshape(N_CHIPS, 2, grad.shape[0] // (2 * N_CHIPS), grad.shape[1])
        if name in self.ALONE:
            self._chip_sums([name], _alone(_sibling_rider([self.split[name]]), f"reduce_sibling_{name}"))

    def _chip_sums(self, names, from_sibling):
        for k, fs in zip(names, from_sibling):
            self.own[k], self.to_send[k] = _chip_sum(self.split[k], fs, self.place, f"chip_sum_{k}")

    def summed_halves(self):
        late = [k for k in SEGMENTS if k not in self.received]
        self.received.update(zip(late, _alone(_scatter_rider([self.to_send[k] for k in late]), "reduce_chips_last")))
        mine = _total_sums([self.own[k] for k in SEGMENTS], [self.received[k] for k in SEGMENTS], "total_sums")
        return mine, _alone(_swap_rider(mine), "swap_halves")


SMALL = ("g_ffn1_pre", "g_ffn1_post", "g_mix_pre", "w_pool_lin", "pool_scale", "g_mix_post", "g_ffn2_pre", "g_ffn2_post")
WEIGHTS = ("g_ffn1_pre", "w1_gate", "w1_up", "w1_down", "g_ffn1_post", "g_mix_pre", "w_in", "w_pool_lin", "pool_scale", "w_out",
           "g_mix_post", "g_ffn2_pre", "w2_gate", "w2_up", "w2_down", "g_ffn2_post")
LANES = 128


def _pack_small(tree, extra=0.0):
    flat = jnp.concatenate([tree[k].reshape(-1) for k in SMALL] + [jnp.reshape(extra, (1,)).astype(F32)])
    rows = -(-flat.shape[0] // (8 * LANES)) * 8
    return jnp.pad(flat, (0, rows * LANES - flat.shape[0])).reshape(rows, LANES)


def _unpack_small(packed, like):
    flat, out, at = packed.reshape(-1), {}, 0
    for k in SMALL:
        size = math.prod(like[k].shape)
        out[k] = flat[at:at + size].reshape(like[k].shape)
        at += size
    return out


def kernel(x, g_ffn1_pre, w1_gate, w1_up, w1_down, g_ffn1_post, g_mix_pre, w_in, w_pool_lin, pool_scale, w_out, g_mix_post, g_ffn2_pre, w2_gate, w2_up, w2_down, g_ffn2_post, loss_target, m_g_ffn1_pre, m_w1_gate, m_w1_up, m_w1_down, m_g_ffn1_post, m_g_mix_pre, m_w_in, m_w_pool_lin, m_pool_scale, m_w_out, m_g_mix_post, m_g_ffn2_pre, m_w2_gate, m_w2_up, m_w2_down, m_g_ffn2_post, v_g_ffn1_pre, v_w1_gate, v_w1_up, v_w1_down, v_g_ffn1_post, v_g_mix_pre, v_w_in, v_w_pool_lin, v_pool_scale, v_w_out, v_g_mix_post, v_g_ffn2_pre, v_w2_gate, v_w2_up, v_w2_down, v_g_ffn2_post):
    given = dict(locals())
    w = {k: given[k] for k in WEIGHTS}
    m = {k: given["m_" + k] for k in WEIGHTS}
    v = {k: given["v_" + k] for k in WEIGHTS}
    small = {k: (w[k][0] if k == "w_pool_lin" else w[k].reshape(1, -1)) for k in SMALL}

    place = jnp.stack([2 * lax.axis_index("x") + lax.axis_index("y"), lax.axis_index("c")]).astype(jnp.int32)
    def as_rows(a, k):
        return jnp.swapaxes(a, 1, 2)[0] if k in ROWS_OUTSIDE else a[0]

    def as_given(a, k):
        return jnp.swapaxes(a[None], 1, 2) if k in ROWS_OUTSIDE else a[None]

    in_kernel = [k for k in TRANSPOSED if k not in ROWS_OUTSIDE]
    bufs = {}
    for tag, names in (("first", _Exchange.FIRST), ("rest", [k for k in SEGMENTS if k not in _Exchange.FIRST])):
        bufs.update(zip(names, _cast_shards([as_rows(w[k], k) for k in names], [k in in_kernel for k in names], place, f"cast_{tag}")))
    exchange = _Exchange(bufs, place)
    loss_part, grad_x, small_grads = _local_step(x[0], loss_target[0], small, exchange)

    out_grad, out_delta, out_m, out_v = {}, {}, {}, {}
    for k, mine, siblings in zip(SEGMENTS, *exchange.summed_halves()):
        results = _adamw(as_rows(w[k], k), mine, siblings, place, as_rows(m[k], k), as_rows(v[k], k), k in in_kernel, f"adamw_{k}")
        out_grad[k], out_delta[k], out_m[k], out_v[k] = (as_given(a, k) for a in results)

    small_grads["w_pool_lin"] = small_grads["w_pool_lin"][None]
    packed = _pack_small(small_grads, loss_part)
    gathered = _gather_small(packed).reshape(N_DEV, *packed.shape)
    like = {k: w[k] for k in SMALL}
    results = _adamw_small(gathered, _pack_small(like), _pack_small({k: m[k] for k in SMALL}),
                           _pack_small({k: v[k] for k in SMALL}), "adamw_small")
    for tree, res in zip((out_grad, out_delta, out_m, out_v), results):
        tree.update(_unpack_small(res, like))
    loss = results[0].reshape(-1)[sum(math.prod(like[k].shape) for k in SMALL)]

    return (loss, grad_x[None], *[out_grad[k] for k in WEIGHTS], *[out_delta[k] for k in WEIGHTS],
            *[out_m[k] for k in WEIGHTS], *[out_v[k] for k in WEIGHTS])
```

```python
import math
import typing

import numpy as np
import jax
import jax.numpy as jnp
from jax import lax
from jax.experimental import pallas as pl
from jax.experimental.pallas import tpu as pltpu

F32 = jnp.float32
BF16 = jnp.bfloat16
MESH = pl.DeviceIdType.MESH

RMS_EPS = 1e-6
HEAD_DIM = 64
POOL_HALF_WINDOWS = (1, 2, 4, 8)
POOL_DIM = 256
GROUP_DIM = 256
DILATIONS = (1, 4, 16)
N_SIDE = 64
N_ATTN_HEADS = 12
ADAM_LR, ADAM_B1, ADAM_B2, ADAM_EPS, ADAM_WD, ADAM_STEP = 0.001, 0.9, 0.999, 1e-08, 0.01, 10

N_CHIPS = 4
V7X_VMEM_LIMIT = 60 * 1024 * 1024

_NT = (((1,), (1,)), ((), ()))
_TN = (((0,), (0,)), ((), ()))


def _dot(a, b):
    return jnp.dot(a, b, preferred_element_type=F32)


def _dot_nt(a, b):
    return lax.dot_general(a, b, _NT, preferred_element_type=F32)


def _dot_tn(a, b):
    return lax.dot_general(a, b, _TN, preferred_element_type=F32)


def _params(**kw):
    return pltpu.CompilerParams(vmem_limit_bytes=V7X_VMEM_LIMIT, **kw)


def _rows(tm, width):
    return pl.BlockSpec((tm, width), lambda i: (i, 0))


def _resident(shape):
    return pl.BlockSpec(shape, lambda i: (0,) * len(shape), pipeline_mode=pl.Buffered(1))


def _const(shape):
    return pl.BlockSpec(shape, lambda i: (0,) * len(shape))


def _inv_rms(x):
    return lax.rsqrt(jnp.mean(x * x, axis=-1, keepdims=True) + RMS_EPS)


def _rms_bwd(x, inv, g, dy):
    n = x * inv
    dn = dy * g
    dx = inv * (dn - n * jnp.mean(dn * n, axis=-1, keepdims=True))
    return dx, jnp.sum(dy * n, axis=0, keepdims=True)


def _accumulate(ref, value):
    @pl.when(pl.program_id(0) == 0)
    def _():
        ref[...] = jnp.zeros_like(ref)

    ref[...] += value


class _Rider(typing.NamedTuple):
    operands: list
    landing: typing.Optional[list]
    sems: tuple
    start: typing.Callable
    wait: typing.Callable


def _hosted_call(body, riders, *, name, steps, in_specs, out_specs, out_shape, args, scratch_shapes=()):
    params = _params(dimension_semantics=("arbitrary",))
    riders = list(riders or [])
    if not riders:
        res = pl.pallas_call(body, name=name, grid=(steps,), in_specs=in_specs, out_specs=out_specs, out_shape=out_shape,
                             scratch_shapes=list(scratch_shapes), compiler_params=params)(*args)
        return list(res), []
    n_in, n_out, n_scratch = len(in_specs), len(out_specs), len(scratch_shapes)
    operands, landing, aliases, spans = [], [], {}, []
    for rd in riders:
        lands = rd.landing if rd.landing is not None else [jax.ShapeDtypeStruct(a.shape, a.dtype) for a in rd.operands]
        if rd.landing is None:
            aliases.update({n_in + len(operands) + i: n_out + len(landing) + i for i in range(len(lands))})
        spans.append((len(operands), len(rd.operands), len(landing), len(lands)))
        operands += rd.operands
        landing += lands
    outs_at = n_in + len(operands)
    scratch_at = outs_at + n_out + len(landing)

    def riding(*refs):
        def each(action):
            for i, (rd, (in_at, n_ops, out_at, n_lands)) in enumerate(zip(riders, spans)):
                sems = refs[scratch_at + n_scratch + 2 * i:scratch_at + n_scratch + 2 * i + 2]
                getattr(rd, action)(refs[n_in + in_at:n_in + in_at + n_ops],
                                    refs[outs_at + n_out + out_at:outs_at + n_out + out_at + n_lands], *sems)

        @pl.when(pl.program_id(0) == 0)
        def _():
            each("start")

        body(*refs[:n_in], *refs[outs_at:outs_at + n_out], *refs[scratch_at:scratch_at + n_scratch])

        @pl.when(pl.program_id(0) == steps - 1)
        def _():
            each("wait")

    any_spec = pl.BlockSpec(memory_space=pl.ANY)
    res = pl.pallas_call(
        riding, name=name, grid=(steps,), in_specs=list(in_specs) + [any_spec] * len(operands),
        out_specs=list(out_specs) + [any_spec] * len(landing), out_shape=list(out_shape) + landing,
        scratch_shapes=list(scratch_shapes) + [pltpu.SemaphoreType.DMA(rd.sems) for rd in riders for _ in range(2)],
        input_output_aliases=aliases, compiler_params=params)(*args, *operands)
    return list(res[:n_out]), [list(res[n_out + out_at:n_out + out_at + n_lands]) for _, _, out_at, n_lands in spans]


_SUB_TILE = 256


def _sub_tiles(tm):
    return [pl.ds(r, _SUB_TILE) for r in range(0, tm, _SUB_TILE)]


def _ffn_fwd(x, g_pre, wg_t, wu_t, wd, g_post, target, name, riders=None, tm=512):
    s, d = x.shape
    ff = wd.shape[0]
    with_loss = target is not None

    def body(*refs):
        if with_loss:
            x_ref, gpre_ref, wg_ref, wu_ref, wd_ref, gpost_ref, t_ref, xo_ref, a_ref, b_ref, f_ref, loss_ref = refs
        else:
            x_ref, gpre_ref, wg_ref, wu_ref, wd_ref, gpost_ref, xo_ref, a_ref, b_ref, f_ref = refs
        loss = 0.0
        for rows in _sub_tiles(tm):
            xv = x_ref[rows, :]
            hb = (xv * _inv_rms(xv) * gpre_ref[...]).astype(BF16)
            a = _dot_nt(hb, wg_ref[...])
            b = _dot_nt(hb, wu_ref[...])
            hh = (a * jax.nn.sigmoid(a)) * b
            f = _dot(hh.astype(BF16), wd_ref[...])
            xo = xv + 0.5 * (f * _inv_rms(f) * gpost_ref[...])
            a_ref[rows, :] = a.astype(BF16)
            b_ref[rows, :] = b.astype(BF16)
            f_ref[rows, :] = f
            if with_loss:
                e = xo - t_ref[rows, :]
                xo_ref[rows, :] = e * (1.0 / d)
                loss = loss + 0.5 * jnp.sum(jnp.mean(e * e, axis=-1, keepdims=True))
            else:
                xo_ref[rows, :] = xo
        if with_loss:
            _accumulate(loss_ref, loss)

    in_specs = [_rows(tm, d), _const((1, d)), _resident((ff, d)), _resident((ff, d)), _resident((ff, d)), _const((1, d))]
    args = [x, g_pre, wg_t, wu_t, wd, g_post]
    out_shape = [jax.ShapeDtypeStruct((s, d), F32), jax.ShapeDtypeStruct((s, ff), BF16),
                 jax.ShapeDtypeStruct((s, ff), BF16), jax.ShapeDtypeStruct((s, d), F32)]
    out_specs = [_rows(tm, d), _rows(tm, ff), _rows(tm, ff), _rows(tm, d)]
    if with_loss:
        in_specs.append(_rows(tm, d))
        args.append(target)
        out_shape.append(jax.ShapeDtypeStruct((8, 128), F32))
        out_specs.append(_const((8, 128)))
    return _hosted_call(body, riders, name=name, steps=s // tm, in_specs=in_specs, out_specs=out_specs, out_shape=out_shape, args=args)


def _ffn_bwd(dxo, x, f, a, b, g_pre, g_post, wg_t, wu_t, wd, name, riders=None, tm=256):
    s, d = x.shape
    ff = wd.shape[0]

    def body(dxo_ref, x_ref, f_ref, a_ref, b_ref, gpre_ref, gpost_ref, wg_ref, wu_ref, wd_ref,
             dx_ref, hh_ref, da_ref, db_ref, df_ref, h_ref, dgpre_ref, dgpost_ref):
        dgpre_sum = dgpost_sum = 0.0
        for rows in _sub_tiles(tm):
            dxo_v = dxo_ref[rows, :]
            fv = f_ref[rows, :]
            df, dgpost = _rms_bwd(fv, _inv_rms(fv), gpost_ref[...], 0.5 * dxo_v)
            dfb = df.astype(BF16)
            dhh = _dot_nt(dfb, wd_ref[...])
            av = a_ref[rows, :].astype(F32)
            bv = b_ref[rows, :].astype(F32)
            sig = jax.nn.sigmoid(av)
            sa = av * sig
            da = (dhh * bv * (sig * (1.0 + av * (1.0 - sig)))).astype(BF16)
            db = (dhh * sa).astype(BF16)
            dh = _dot(da, wg_ref[...]) + _dot(db, wu_ref[...])
            xv = x_ref[rows, :]
            inv = _inv_rms(xv)
            dxn, dgpre = _rms_bwd(xv, inv, gpre_ref[...], dh)
            dx_ref[rows, :] = dxo_v + dxn
            hh_ref[rows, :] = (sa * bv).astype(BF16)
            da_ref[rows, :] = da
            db_ref[rows, :] = db
            df_ref[rows, :] = dfb
            h_ref[rows, :] = (xv * inv * gpre_ref[...]).astype(BF16)
            dgpre_sum, dgpost_sum = dgpre_sum + dgpre, dgpost_sum + dgpost
        _accumulate(dgpre_ref, dgpre_sum)
        _accumulate(dgpost_ref, dgpost_sum)

    return _hosted_call(
        body, riders, name=name, steps=s // tm,
        in_specs=[_rows(tm, d), _rows(tm, d), _rows(tm, d), _rows(tm, ff), _rows(tm, ff), _const((1, d)), _const((1, d)),
                  _resident((ff, d)), _resident((ff, d)), _resident((ff, d))],
        out_specs=[_rows(tm, d), _rows(tm, ff), _rows(tm, ff), _rows(tm, ff), _rows(tm, d), _rows(tm, d),
                   _const((1, d)), _const((1, d))],
        out_shape=[jax.ShapeDtypeStruct((s, d), F32), jax.ShapeDtypeStruct((s, ff), BF16), jax.ShapeDtypeStruct((s, ff), BF16),
                   jax.ShapeDtypeStruct((s, ff), BF16), jax.ShapeDtypeStruct((s, d), BF16), jax.ShapeDtypeStruct((s, d), BF16),
                   jax.ShapeDtypeStruct((1, d), F32), jax.ShapeDtypeStruct((1, d), F32)],
        args=[dxo, x, f, a, b, g_pre, g_post, wg_t, wu_t, wd])


def _wgrad(lhs, rhs, name, riders=None, rt=256):
    s, r = lhs.shape
    c = rhs.shape[1]

    def body(l_ref, r_ref, o_ref):
        o_ref[...] = _dot_tn(l_ref[...], r_ref[...])

    (out,), riding = _hosted_call(
        body, riders, name=name, steps=pl.cdiv(r, rt), in_specs=[pl.BlockSpec((s, rt), lambda i: (0, i)), _resident((s, c))],
        out_specs=[pl.BlockSpec((rt, c), lambda i: (i, 0))], out_shape=[jax.ShapeDtypeStruct((r, c), F32)], args=[lhs, rhs])
    return out, riding


def _attn_dtype(dilation):
    return BF16 if dilation == 1 else F32


def _in_fwd(x, g, w_in_t, name, riders=None, tm=1024):
    s, d = x.shape
    d_in = w_in_t.shape[0]
    n_groups = len(DILATIONS)
    dtypes = [_attn_dtype(dil) for dil in DILATIONS] * 3

    def body(x_ref, g_ref, w_ref, h_ref, u_ref, *part_refs):
        xv = x_ref[...]
        hb = (xv * _inv_rms(xv) * g_ref[...]).astype(BF16)
        h_ref[...] = hb
        z = _dot_nt(hb, w_ref[...])
        u_ref[...] = z[:, :POOL_DIM]
        for j, ref in enumerate(part_refs):
            part = z[:, POOL_DIM + GROUP_DIM * j:POOL_DIM + GROUP_DIM * (j + 1)]
            ref[...] = (part * _SCORE_SCALE if j < n_groups else part).astype(ref.dtype)

    return _hosted_call(
        body, riders, name=name, steps=s // tm, in_specs=[_rows(tm, d), _const((1, d)), _resident((d_in, d))],
        out_specs=[_rows(tm, d), _rows(tm, POOL_DIM)] + [_rows(tm, GROUP_DIM)] * len(dtypes),
        out_shape=[jax.ShapeDtypeStruct((s, d), BF16), jax.ShapeDtypeStruct((s, POOL_DIM), F32)]
        + [jax.ShapeDtypeStruct((s, GROUP_DIM), dt) for dt in dtypes],
        args=[x, g, w_in_t])


def _in_bwd(du, dparts, x, dxo, g, w_in_t, name, riders=None, tm=512):
    s, d = x.shape
    d_in = w_in_t.shape[0]
    n_parts = len(dparts)

    def body(du_ref, *refs):
        part_refs = refs[:n_parts]
        x_ref, dxo_ref, g_ref, w_ref, dx_ref, dg_ref = refs[n_parts:]
        dh = _dot(jnp.concatenate([r[...] for r in (du_ref,) + part_refs], axis=1), w_ref[...])
        xv = x_ref[...]
        dxn, dg = _rms_bwd(xv, _inv_rms(xv), g_ref[...], dh)
        dx_ref[...] = dxo_ref[...] + dxn
        _accumulate(dg_ref, dg)

    return _hosted_call(
        body, riders, name=name, steps=s // tm,
        in_specs=[_rows(tm, POOL_DIM)] + [_rows(tm, GROUP_DIM)] * n_parts + [_rows(tm, d), _rows(tm, d), _const((1, d)),
                                                                             _resident((d_in, d))],
        out_specs=[_rows(tm, d), _const((1, d))],
        out_shape=[jax.ShapeDtypeStruct((s, d), F32), jax.ShapeDtypeStruct((1, d), F32)],
        args=[du, *dparts, x, dxo, g, w_in_t])


def _wgrad_parts(parts, rhs, name, riders=None):
    n = len(parts)
    s, rt = parts[0].shape
    c = rhs.shape[1]

    def body(*refs):
        part_refs, r_ref, o_ref, buf, sems = refs[:n], refs[n], refs[n + 1], refs[n + 2], refs[n + 3]

        def fetch(i):
            return pltpu.make_async_copy(part_refs[i], buf.at[i % 2], sems.at[i % 2])

        fetch(0).start()
        for i in range(n):
            if i + 1 < n:
                fetch(i + 1).start()
            fetch(i).wait()
            o_ref[pl.ds(i * rt, rt), :] = _dot_tn(buf[i % 2], r_ref[...])

    (out,), riding = _hosted_call(
        body, riders, name=name, steps=1, in_specs=[pl.BlockSpec(memory_space=pl.ANY)] * n + [_resident((s, c))],
        out_specs=[_resident((n * rt, c))], out_shape=[jax.ShapeDtypeStruct((n * rt, c), F32)], args=[*parts, rhs],
        scratch_shapes=[pltpu.VMEM((2, s, rt), BF16), pltpu.SemaphoreType.DMA((2,))])
    return out, riding


_POOL_HALO = 8


def _pool_chain(v, first_shift):
    n = v.shape[0]
    p2 = v + pltpu.roll(v, first_shift, 0)
    p4 = pltpu.roll(p2, 1, 0) + pltpu.roll(p2, n - 1, 0)
    p8 = pltpu.roll(p4, 2, 0) + pltpu.roll(p4, n - 2, 0)
    p16 = pltpu.roll(p8, 4, 0) + pltpu.roll(p8, n - 4, 0)
    group = lax.broadcasted_iota(jnp.int32, v.shape, 1) // HEAD_DIM
    return jnp.where(group == 0, p2, jnp.where(group == 1, p4, jnp.where(group == 2, p8, p16)))


def _pool_count(t0, rows, s):
    t = t0 + lax.broadcasted_iota(jnp.int32, (rows, POOL_DIM), 0)
    group = lax.broadcasted_iota(jnp.int32, (rows, POOL_DIM), 1) // HEAD_DIM
    half = jnp.where(group == 0, 1, jnp.where(group == 1, 2, jnp.where(group == 2, 4, 8)))
    cnt = jnp.minimum(t + half, s) - jnp.maximum(t - half, 0)
    return jnp.maximum(cnt, 1).astype(F32)


def _pad_rows(ref, pad_ref, s):
    zeros = jnp.zeros((_POOL_HALO, pad_ref.shape[1]), pad_ref.dtype)
    pad_ref[pl.ds(0, _POOL_HALO), :] = zeros
    pad_ref[pl.ds(_POOL_HALO + s, _POOL_HALO), :] = zeros
    pad_ref[pl.ds(_POOL_HALO, s), :] = ref[...]


def _pool_fwd(u, w_bd, scale, name, tm=512):
    s = u.shape[0]
    ext = tm + 2 * _POOL_HALO

    def body(u_ref, w_ref, sc_ref, o_ref, upad):
        _pad_rows(u_ref, upad, s)

        def tile(i, carry):
            t0 = pl.multiple_of(i * tm, tm)
            uv = upad[pl.ds(t0, ext), :]
            win = _pool_chain(uv, 1)[_POOL_HALO:_POOL_HALO + tm]
            y = win / _pool_count(t0, tm, s) - uv[_POOL_HALO:_POOL_HALO + tm]
            o_ref[pl.ds(t0, tm), :] = (_dot(y.astype(BF16), w_ref[...]) * sc_ref[...]).astype(BF16)
            return carry

        lax.fori_loop(0, s // tm, tile, 0)

    return pl.pallas_call(body, name=name, out_shape=jax.ShapeDtypeStruct((s, POOL_DIM), BF16),
                          scratch_shapes=[pltpu.VMEM((s + 2 * _POOL_HALO, POOL_DIM), F32)],
                          compiler_params=_params())(u, w_bd, scale)


def _pool_bwd(u, da, w_bd, scale, name, tm=512):
    s = u.shape[0]
    ext = tm + 2 * _POOL_HALO

    def body(u_ref, da_ref, w_ref, sc_ref, du_ref, dw_ref, dsc_ref, upad, dapad):
        _pad_rows(u_ref, upad, s)
        _pad_rows(da_ref, dapad, s)
        dw_ref[...] = jnp.zeros_like(dw_ref)
        dsc_ref[...] = jnp.zeros_like(dsc_ref)

        def tile(i, carry):
            t0 = pl.multiple_of(i * tm, tm)
            uv = upad[pl.ds(t0, ext), :]
            dav = dapad[pl.ds(t0, ext), :]
            win = _pool_chain(uv, 1)[_POOL_HALO:_POOL_HALO + tm]
            yb = (win / _pool_count(t0, tm, s) - uv[_POOL_HALO:_POOL_HALO + tm]).astype(BF16)
            yl = _dot(yb, w_ref[...])
            da_c = dav[_POOL_HALO:_POOL_HALO + tm]
            dsc_ref[...] += jnp.sum(da_c * yl, axis=0, keepdims=True)
            dyl = (dav * sc_ref[...]).astype(BF16)
            dw_ref[...] += _dot_tn(yb, dyl[_POOL_HALO:_POOL_HALO + tm])
            dy = _dot_nt(dyl, w_ref[...])
            dyc = dy / _pool_count(t0 - _POOL_HALO, ext, s)
            du_ref[pl.ds(t0, tm), :] = (_pool_chain(dyc, ext - 1) - dy)[_POOL_HALO:_POOL_HALO + tm].astype(BF16)
            return carry

        lax.fori_loop(0, s // tm, tile, 0)

    pool_cols = pl.BlockSpec((s, POOL_DIM), lambda i: (0, 0), pipeline_mode=pl.Buffered(1))
    return pl.pallas_call(
        body, name=name, grid=(1,),
        in_specs=[pool_cols, pool_cols, _const((POOL_DIM, POOL_DIM)), _const((1, POOL_DIM))],
        out_specs=[_const((s, POOL_DIM)), _const((POOL_DIM, POOL_DIM)), _const((1, POOL_DIM))],
        out_shape=[jax.ShapeDtypeStruct((s, POOL_DIM), BF16), jax.ShapeDtypeStruct((POOL_DIM, POOL_DIM), F32),
                   jax.ShapeDtypeStruct((1, POOL_DIM), F32)],
        scratch_shapes=[pltpu.VMEM((s + 2 * _POOL_HALO, POOL_DIM), F32), pltpu.VMEM((s + 2 * _POOL_HALO, POOL_DIM), F32)],
        compiler_params=_params(dimension_semantics=("arbitrary",)))(u, da, w_bd, scale)


_BQ = 128
_KW = _BQ + 2 * N_SIDE
_PAIR = 2 * HEAD_DIM
_NEG = -1e30
_ATTN_UNROLL = 8
_SCORE_SCALE = HEAD_DIM ** -0.5


def _stack_heads(x):
    lane_head = lax.broadcasted_iota(jnp.int32, x.shape, 1) // HEAD_DIM
    zero = jnp.zeros_like(x)
    return jnp.concatenate([jnp.where(lane_head == 0, x, zero), jnp.where(lane_head == 1, x, zero)], axis=0)


def _unstack_heads(x):
    lane_head = lax.broadcasted_iota(jnp.int32, (_BQ, _PAIR), 1) // HEAD_DIM
    return jnp.where(lane_head == 0, x[:_BQ], x[_BQ:])


def _stack_cols(x):
    return jnp.concatenate([x[:, 0:1], x[:, HEAD_DIM:HEAD_DIM + 1]], axis=0)


def _fill_bias(bias_ref, slopes_ref, dilation):
    row = lax.broadcasted_iota(jnp.int32, (2 * _BQ, _KW), 0)
    col = lax.broadcasted_iota(jnp.int32, (2 * _BQ, _KW), 1)
    pair = 2 * pl.program_id(0)
    slope = jnp.where(row < _BQ, slopes_ref[pair], slopes_ref[pair + 1]) * float(dilation)

    @pl.when(pl.program_id(1) == 0)
    def _():
        for j in range(3):
            dist = jnp.abs(col - (row & (_BQ - 1)) - j * N_SIDE)
            bias_ref[j] = jnp.where(dist <= N_SIDE, -slope * dist.astype(F32), _NEG)


def _block_window(i, n_blocks, length):
    q0 = pl.multiple_of(i * _BQ, _BQ)
    ws = pl.multiple_of(jnp.clip(q0 - N_SIDE, 0, length - _KW), N_SIDE)
    return q0, ws, jnp.where(i == 0, 0, jnp.where(i == n_blocks - 1, 2, 1))


_FREE_STRIDE = 4


def _residues_per_step(dilation):
    return max(dilation // _FREE_STRIDE, 1)


def _residue_views(dilation, seq, ins, outs, tmps):
    step = pl.program_id(1)
    if dilation <= _FREE_STRIDE:
        def rows(start, count, sub=0):
            return pl.ds(start, count) if dilation == 1 else pl.ds(start * dilation + step, count, stride=dilation)

        return ins, outs, rows, lambda: None
    inner = _residues_per_step(dilation)
    assert inner <= _FREE_STRIDE and len(tmps) == len(ins) + len(outs)
    coarse = pl.ds(step, seq // _FREE_STRIDE, stride=_FREE_STRIDE)
    in_tmps, out_tmps = tmps[:len(ins)], tmps[len(ins):]
    for ref, tmp in zip(ins, in_tmps):
        tmp[...] = ref[coarse, :]

    def flush():
        for ref, tmp in zip(outs, out_tmps):
            ref[coarse, :] = tmp[...]

    return in_tmps, out_tmps, lambda start, count, sub=0: pl.ds(start * inner + sub, count, stride=inner), flush


def _of_sub(ref, sub):
    return ref.at[sub] if len(ref.shape) == 3 else ref


def _attn_call(body, name, dilation, seq, n_in, out_dtypes, scratch, buffers):
    col = pl.BlockSpec((seq, _PAIR), lambda c, r: (0, c), pipeline_mode=pl.Buffered(buffers))
    tmps = [pltpu.VMEM((seq // _FREE_STRIDE, _PAIR), F32)] * (n_in + len(out_dtypes) if dilation > _FREE_STRIDE else 0)
    return pl.pallas_call(
        body, name=name, grid=(GROUP_DIM // _PAIR, dilation // _residues_per_step(dilation)),
        in_specs=[pl.BlockSpec(memory_space=pltpu.SMEM)] + [col] * n_in, out_specs=[col] * len(out_dtypes),
        out_shape=[jax.ShapeDtypeStruct((seq, GROUP_DIM), dt) for dt in out_dtypes], scratch_shapes=scratch + tmps,
        compiler_params=_params(dimension_semantics=("arbitrary", "arbitrary")))


def _staged(dilation, length, rows, sources, scratch):
    if dilation == 1:
        return sources
    for src, dst in zip(sources, scratch):
        for sub in range(_residues_per_step(dilation)):
            dst[sub] = src[rows(0, length, sub), :].astype(BF16)
    return scratch


def _sub_and_block(i, dilation, n_blocks):
    return (0, i) if _residues_per_step(dilation) == 1 else (i // n_blocks, i % n_blocks)


def _attn_fwd(q, k, v, slopes, dilation, name):
    seq = q.shape[0]
    length = seq // dilation
    n_blocks = length // _BQ
    n_stage = 0 if dilation == 1 else 3

    def body(sl_ref, q_ref, k_ref, v_ref, o_ref, lse_ref, *scratch):
        bias_ref, tmps = scratch[n_stage], scratch[n_stage + 1:]
        (q_in, k_in, v_in), (o_out, lse_out), rows, flush = _residue_views(dilation, seq, (q_ref, k_ref, v_ref), (o_ref, lse_ref), tmps)
        qs, ks, vs = _staged(dilation, length, rows, (q_in, k_in, v_in), scratch[:n_stage])
        _fill_bias(bias_ref, sl_ref, dilation)

        def block(i, carry):
            sub, j = _sub_and_block(i, dilation, n_blocks)
            q0, ws, which = _block_window(j, n_blocks, length)
            kw = _of_sub(ks, sub)[pl.ds(ws, _KW), :]
            vw = _of_sub(vs, sub)[pl.ds(ws, _KW), :]
            sc = _dot_nt(_stack_heads(_of_sub(qs, sub)[pl.ds(q0, _BQ), :]), kw) + bias_ref[which]
            m = jnp.max(sc, axis=-1, keepdims=True)
            p = jnp.exp(sc - m)
            den = jnp.sum(p, axis=-1, keepdims=True)
            o_out[rows(q0, _BQ, sub), :] = _unstack_heads(_dot(p.astype(BF16), vw) / den)
            lse_out[rows(q0, _BQ, sub), :] = _unstack_heads(jnp.broadcast_to(m + jnp.log(den), (2 * _BQ, _PAIR)))
            return carry

        lax.fori_loop(0, trips, block, 0, unroll=min(_ATTN_UNROLL, trips))
        flush()

    trips = _residues_per_step(dilation) * n_blocks
    stage = pltpu.VMEM((_residues_per_step(dilation), length, _PAIR), BF16)
    bias = pltpu.VMEM((3, 2 * _BQ, _KW), F32)
    return _attn_call(body, name, dilation, seq, 3, [F32, F32], [stage] * n_stage + [bias], 2)(slopes, q, k, v)


def _attn_bwd(q, k, v, do, lse, cterm, slopes, dilation, name):
    seq = q.shape[0]
    length = seq // dilation
    n_blocks = length // _BQ
    n_stage, n_whole = (0, 0) if dilation == 1 else (4, 3)

    def body(sl_ref, q_ref, k_ref, v_ref, do_ref, lse_ref, c_ref, dq_ref, dk_ref, dv_ref, *scratch):
        dk_acc, dv_acc, bias_ref = scratch[n_stage:n_stage + 3]
        whole, tmps = scratch[n_stage + 3:n_stage + 3 + n_whole], scratch[n_stage + 3 + n_whole:]
        (q_in, k_in, v_in, do_in, lse_in, c_in), (dq_out, dk_out, dv_out), rows, flush = _residue_views(
            dilation, seq, (q_ref, k_ref, v_ref, do_ref, lse_ref, c_ref), whole or (dq_ref, dk_ref, dv_ref), tmps)
        qs, ks, vs, dos = _staged(dilation, length, rows, (q_in, k_in, v_in, do_in), scratch[:n_stage])
        dk_acc[...] = jnp.zeros_like(dk_acc)
        dv_acc[...] = jnp.zeros_like(dv_acc)
        _fill_bias(bias_ref, sl_ref, dilation)

        def block(i, carry):
            sub, j = _sub_and_block(i, dilation, n_blocks)
            q0, ws, which = _block_window(j, n_blocks, length)
            qm = _stack_heads(_of_sub(qs, sub)[pl.ds(q0, _BQ), :])
            dom = _stack_heads(_of_sub(dos, sub)[pl.ds(q0, _BQ), :])
            kw = _of_sub(ks, sub)[pl.ds(ws, _KW), :]
            vw = _of_sub(vs, sub)[pl.ds(ws, _KW), :]
            p = jnp.exp(_dot_nt(qm, kw) + bias_ref[which] - _stack_cols(lse_in[rows(q0, _BQ, sub), :]))
            ds = (p * (_dot_nt(dom, vw) + _stack_cols(c_in[rows(q0, _BQ, sub), :]))).astype(BF16)
            dq_out[rows(q0, _BQ, sub), :] = (_unstack_heads(_dot(ds, kw)) * _SCORE_SCALE).astype(dq_out.dtype)
            dk_acc[sub, pl.ds(ws, _KW), :] += _dot_tn(ds, qm)
            dv_acc[sub, pl.ds(ws, _KW), :] += _dot_tn(p.astype(BF16), dom)
            return carry

        lax.fori_loop(0, trips, block, 0, unroll=min(_ATTN_UNROLL, trips))
        for sub in range(per):
            dk_out[rows(0, length, sub), :] = dk_acc[sub].astype(dk_out.dtype)
            dv_out[rows(0, length, sub), :] = dv_acc[sub].astype(dv_out.dtype)
        flush()
        if whole:
            @pl.when(pl.program_id(1) == dilation // per - 1)
            def _():
                for ref, collected in zip((dq_ref, dk_ref, dv_ref), whole):
                    ref[...] = collected[...].astype(BF16)

    per = _residues_per_step(dilation)
    trips = per * n_blocks
    stage = pltpu.VMEM((per, length, _PAIR), BF16)
    acc = pltpu.VMEM((per, length, _PAIR), F32)
    bias = pltpu.VMEM((3, 2 * _BQ, _KW), F32)
    collect = pltpu.VMEM((seq, _PAIR), F32)
    return _attn_call(body, name, dilation, seq, 6, [BF16] * 3, [stage] * n_stage + [acc] * 2 + [bias] + [collect] * n_whole,
                      2 if dilation == 1 else 1)(slopes, q, k, v, do, lse, cterm)


def _group_weights(lses):
    m = jnp.maximum(jnp.maximum(lses[0], lses[1]), lses[2])
    es = [jnp.exp(l - m) for l in lses]
    den = es[0] + es[1] + es[2]
    return [e / den for e in es]


def _out_fwd(a_pool, outs, lses, x, w_out, g, name, tm=1024):
    s, d = x.shape
    width = POOL_DIM + 3 * GROUP_DIM

    def body(ap_ref, o0, o1, o2, l0, l1, l2, x_ref, w_ref, g_ref, xo_ref, cat_ref):
        alphas = _group_weights([l0[...], l1[...], l2[...]])
        cat = jnp.concatenate([ap_ref[...]] + [(o[...] * al).astype(BF16) for o, al in zip((o0, o1, o2), alphas)], axis=1)
        cat_ref[...] = cat
        mix = _dot(cat, w_ref[...])
        xo_ref[...] = x_ref[...] + mix * _inv_rms(mix) * g_ref[...]

    return pl.pallas_call(
        body, name=name, grid=(s // tm,),
        in_specs=[_rows(tm, POOL_DIM)] + [_rows(tm, GROUP_DIM)] * 6 + [_rows(tm, d), _resident(w_out.shape), _const((1, d))],
        out_specs=[_rows(tm, d), _rows(tm, width)],
        out_shape=[jax.ShapeDtypeStruct((s, d), F32), jax.ShapeDtypeStruct((s, width), BF16)],
        compiler_params=_params(dimension_semantics=("arbitrary",)))(a_pool, *outs, *lses, x, w_out, g)


def _out_bwd(dxo, cat, outs, lses, w_out, g, head_ones, name, tm=1024):
    s, d = dxo.shape

    def body(dxo_ref, cat_ref, o0, o1, o2, l0, l1, l2, w_ref, g_ref, ones_ref, dpool_ref, dmix_ref, do0, do1, do2, c0, c1, c2, dg_ref):
        mv = _dot(cat_ref[...], w_ref[...])
        dmix, dg = _rms_bwd(mv, _inv_rms(mv), g_ref[...], dxo_ref[...])
        dmb = dmix.astype(BF16)
        dmix_ref[...] = dmb
        _accumulate(dg_ref, dg)
        dcat = _dot_nt(dmb, w_ref[...])
        dpool_ref[...] = dcat[:, :POOL_DIM]
        alphas = _group_weights([l0[...], l1[...], l2[...]])
        das = [dcat[:, POOL_DIM + GROUP_DIM * j:POOL_DIM + GROUP_DIM * (j + 1)] for j in range(3)]
        prod = sum(da * (o[...] * al) for da, o, al in zip(das, (o0, o1, o2), alphas))
        hi = prod.astype(BF16)
        lo = (prod - hi.astype(F32)).astype(BF16)
        total = _dot(hi, ones_ref[...]) + _dot(lo, ones_ref[...])
        for da, al, do_ref, c_ref in zip(das, alphas, (do0, do1, do2), (c0, c1, c2)):
            do_ref[...] = (da * al).astype(do_ref.dtype)
            c_ref[...] = -al * total

    return pl.pallas_call(
        body, name=name, grid=(s // tm,),
        in_specs=[_rows(tm, d), _rows(tm, cat.shape[1])] + [_rows(tm, GROUP_DIM)] * 6 + [_resident(w_out.shape), _const((1, d)),
                                                                                        _const((GROUP_DIM, GROUP_DIM))],
        out_specs=[_rows(tm, POOL_DIM), _rows(tm, d)] + [_rows(tm, GROUP_DIM)] * 6 + [_const((1, d))],
        out_shape=[jax.ShapeDtypeStruct((s, POOL_DIM), F32), jax.ShapeDtypeStruct((s, d), BF16)]
        + [jax.ShapeDtypeStruct((s, GROUP_DIM), _attn_dtype(dil)) for dil in DILATIONS]
        + [jax.ShapeDtypeStruct((s, GROUP_DIM), F32)] * 3 + [jax.ShapeDtypeStruct((1, d), F32)],
        compiler_params=_params(dimension_semantics=("arbitrary",)))(dxo, cat, *outs, *lses, w_out, g, head_ones)


def _alibi_slopes():
    return np.array([2.0 ** (-8.0 * (i + 1) / N_ATTN_HEADS) for i in range(N_ATTN_HEADS)], np.float32)


def _block_diag(w_lin):
    n, c, _ = w_lin.shape
    eye = jnp.eye(n, dtype=w_lin.dtype)
    return (eye[:, None, :, None] * w_lin[:, :, None, :]).reshape(n * c, n * c)


class _NoExchange:
    def __init__(self, full):
        self.full, self.grads = full, {}

    def first_weights(self):
        return self.full

    def riders(self, host):
        return []

    def landed(self, host, results):
        return self.full

    def gradient(self, name, grad):
        self.grads[name] = grad


def _local_step(x, target, small, exchange):
    s, d = x.shape
    slopes = _alibi_slopes()
    group_slopes = [jnp.asarray(slopes[4 * g:4 * g + 4]) for g in range(3)]
    w_bd = _block_diag(small["w_pool_lin"]).astype(BF16)
    head_ones = jnp.asarray(np.kron(np.eye(GROUP_DIM // HEAD_DIM), np.ones((HEAD_DIM, HEAD_DIM))), BF16)

    full = dict(exchange.first_weights())

    def hosted(call, host, *args):
        results, riding = call(*args, host, exchange.riders(host))
        full.update(exchange.landed(host, riding) or {})
        return results

    x1, a1, b1, f1 = hosted(_ffn_fwd, "ffn1_fwd", x, small["g_ffn1_pre"], full["w1_gate"], full["w1_up"], full["w1_down"],
                            small["g_ffn1_post"], None)
    h2, u, *parts = hosted(_in_fwd, "in_fwd", x1, small["g_mix_pre"], full["w_in"])
    qs, ks, vs = parts[0:3], parts[3:6], parts[6:9]
    a_pool = _pool_fwd(u, w_bd, small["pool_scale"], "pool_fwd")
    outs, lses = [], []
    for g, dil in enumerate(DILATIONS):
        o, lse = _attn_fwd(qs[g], ks[g], vs[g], group_slopes[g], dil, f"attn_fwd{g}")
        outs.append(o)
        lses.append(lse)
    x2, cat = _out_fwd(a_pool, outs, lses, x1, full["w_out"], small["g_mix_post"], "out_fwd")
    (dx3, a2, b2, f2, loss_part), _ = _ffn_fwd(x2, small["g_ffn2_pre"], full["w2_gate"], full["w2_up"], full["w2_down"],
                                               small["g_ffn2_post"], target, "ffn2_fwd")

    small_grads = {}

    def ffn_backward(tag, dxo, x_in, f, a, b):
        n = tag[-1]
        dx, hh, da, db, df, h, dg_pre, dg_post = hosted(
            _ffn_bwd, f"{tag}_bwd", dxo, x_in, f, a, b, small[f"g_{tag}_pre"], small[f"g_{tag}_post"],
            full[f"w{n}_gate"], full[f"w{n}_up"], full[f"w{n}_down"])
        for part, lhs, rhs in (("down", hh, df), ("gate", da, h), ("up", db, h)):
            exchange.gradient(f"w{n}_{part}", hosted(_wgrad, f"{tag}_wgrad_{part}", lhs, rhs))
        small_grads[f"g_{tag}_pre"], small_grads[f"g_{tag}_post"] = dg_pre, dg_post
        return dx

    dx2 = ffn_backward("ffn2", dx3, x2, f2, a2, b2)
    dpool, dmix, *dos_cs, small_grads["g_mix_post"] = _out_bwd(dx2, cat, outs, lses, full["w_out"], small["g_mix_post"],
                                                               head_ones, "out_bwd")
    dos, cs = dos_cs[:3], dos_cs[3:]
    dqs, dks, dvs = [], [], []
    for g, dil in enumerate(DILATIONS):
        dq, dk, dv = _attn_bwd(qs[g], ks[g], vs[g], dos[g], lses[g], cs[g], group_slopes[g], dil, f"attn_bwd{g}")
        dqs.append(dq)
        dks.append(dk)
        dvs.append(dv)
    du, dw_bd, small_grads["pool_scale"] = _pool_bwd(u, dpool, w_bd, small["pool_scale"], "pool_bwd")
    n_pool = len(POOL_HALF_WINDOWS)
    small_grads["w_pool_lin"] = jnp.stack(
        [dw_bd[HEAD_DIM * g:HEAD_DIM * (g + 1), HEAD_DIM * g:HEAD_DIM * (g + 1)] for g in range(n_pool)])
    dz_parts = dqs + dks + dvs
    dx1, small_grads["g_mix_pre"] = hosted(_in_bwd, "in_bwd", du, dz_parts, x1, dx2, small["g_mix_pre"], full["w_in"])
    exchange.gradient("w_in", hosted(_wgrad_parts, "wgrad_in", [du] + dz_parts, h2))
    dx0 = ffn_backward("ffn1", dx1, x, f1, a1, b1)
    exchange.gradient("w_out", hosted(_wgrad, "wgrad_out", cat, dmix))
    return loss_part[0, 0], dx0, small_grads


SEGMENTS = ("w1_gate", "w1_up", "w1_down", "w_in", "w_out", "w2_gate", "w2_up", "w2_down")
TRANSPOSED = ("w1_gate", "w1_up", "w_in", "w2_gate", "w2_up")
ROWS_OUTSIDE = ("w1_gate", "w1_up", "w2_gate", "w2_up")
HALF = 512


def _place():
    x, y, c = lax.axis_index("x"), lax.axis_index("y"), lax.axis_index("c")
    other_chips = [(1 - x, y), (x, 1 - y), (1 - x, 1 - y)]
    return x, y, c, other_chips


def _chip_rows(chip, rows):
    return pl.ds(pl.multiple_of((2 * chip[0] + chip[1]) * rows, 16), rows)


def _cols(c):
    return pl.ds(pl.multiple_of(c * HALF, HALF), HALF)


def _cast_shards(shards, transposed, place, name):
    n = len(shards)
    rows = [w.shape[1] if t else w.shape[0] for w, t in zip(shards, transposed)]

    def body(place_ref, *refs):
        for w_ref, o_ref, t in zip(refs[:n], refs[n:], transposed):
            o_ref[...] = (w_ref[...].T if t else w_ref[...]).astype(BF16)

    once = pl.Buffered(1)
    return pl.pallas_call(
        body, name=name,
        grid_spec=pltpu.PrefetchScalarGridSpec(
            num_scalar_prefetch=1, grid=(1,),
            in_specs=[pl.BlockSpec(w.shape, lambda i, place: (0, 0), pipeline_mode=once) for w in shards],
            out_specs=[pl.BlockSpec((r, 1024), lambda i, place: (place[0], 0), pipeline_mode=once) for r in rows]),
        out_shape=[jax.ShapeDtypeStruct((N_CHIPS * r, 1024), BF16) for r in rows],
        compiler_params=_params(dimension_semantics=("arbitrary",)))(place, *shards)


def _gather_weights(bufs):
    n = len(bufs)
    rows = [b.shape[0] // N_CHIPS for b in bufs]

    def halves(r):
        first = -(-r // 32) * 16
        return (0, first), (first, r - first)

    def body(*refs):
        outs = refs[n:2 * n]
        ici_send, ici_recv, d2d_send, d2d_recv = refs[2 * n:]
        x, y, c, _ = _place()
        me, via_x, via_y, diagonal = (x, y), (1 - x, y), (x, 1 - y), (1 - x, 1 - y)

        def piece(chip, k, h, cols):
            start, size = halves(rows[k])[h]
            return outs[k].at[pl.ds(pl.multiple_of((2 * chip[0] + chip[1]) * rows[k] + start, 16), size), _cols(cols)]

        def ici(path, chip, k, h, to):
            blk = piece(chip, k, h, c)
            return pltpu.make_async_remote_copy(src_ref=blk, dst_ref=blk, send_sem=ici_send.at[path, k, h],
                                                recv_sem=ici_recv.at[path, k, h], device_id=(*to, c), device_id_type=MESH)

        def d2d(slot, chip, k, h, cols):
            blk = piece(chip, k, h, cols)
            return pltpu.make_async_remote_copy(src_ref=blk, dst_ref=blk, send_sem=d2d_send.at[slot, k, h],
                                                recv_sem=d2d_recv.at[slot, k, h], device_id=(x, y, 1 - c), device_id_type=MESH)

        started = [ici(0, me, k, h, via_x) for h in (0, 1) for k in range(n)] + [ici(1, me, k, h, via_y) for h in (1, 0) for k in range(n)]
        for cp in started:
            cp.start()

        def landed(path, slot, chip, k, h, pass_on_to=None):
            ici(path, chip, k, h, me).wait_recv()
            more = [d2d(slot, chip, k, h, c)] + ([ici(2, chip, k, h, pass_on_to)] if pass_on_to else [])
            for cp in more:
                cp.start()
            started.extend(more)

        for k in range(n):
            landed(0, 0, via_x, k, 0, pass_on_to=via_y)
            landed(1, 1, via_y, k, 1, pass_on_to=via_x)
        for k in range(n):
            landed(0, 0, via_x, k, 1)
            landed(1, 1, via_y, k, 0)
        for k in range(n):
            for h in range(2):
                landed(2, 2, diagonal, k, h)
        for slot, chip in enumerate((via_x, via_y, diagonal)):
            for k in range(n):
                for h in range(2):
                    d2d(slot, chip, k, h, 1 - c).wait_recv()
        for cp in started:
            cp.wait_send()

    any_spec = pl.BlockSpec(memory_space=pl.ANY)
    return pl.pallas_call(
        body, name="gather_weights", in_specs=[any_spec] * n, out_specs=[any_spec] * n,
        out_shape=[jax.ShapeDtypeStruct(b.shape, b.dtype) for b in bufs], input_output_aliases={k: k for k in range(n)},
        scratch_shapes=[pltpu.SemaphoreType.DMA((3, n, 2))] * 4)(*bufs)


def _gather_rider(bufs):
    n = len(bufs)
    rows = [b.shape[0] // N_CHIPS for b in bufs]

    def copies(outs, send_sems, recv_sems, inbound):
        x, y, c, chips = _place()
        for j, chip in enumerate(chips):
            for k in range(n):
                src_chip = chip if inbound else (x, y)
                blk = outs[k].at[_chip_rows(src_chip, rows[k]), _cols(c)]
                yield pltpu.make_async_remote_copy(src_ref=blk, dst_ref=blk, send_sem=send_sems.at[j, k], recv_sem=recv_sems.at[j, k],
                                                   device_id=(*chip, c), device_id_type=MESH)

    def start(ins, outs, send_sems, recv_sems):
        for cp in copies(outs, send_sems, recv_sems, False):
            cp.start()

    def wait(ins, outs, send_sems, recv_sems):
        for cp in copies(outs, send_sems, recv_sems, True):
            cp.wait_recv()
        for cp in copies(outs, send_sems, recv_sems, False):
            cp.wait_send()

    return _Rider(list(bufs), None, (3, n), start, wait)


def _forward_rider(bufs):
    n = len(bufs)
    rows = [b.shape[0] // N_CHIPS for b in bufs]

    def copies(outs, send_sems, recv_sems, half):
        x, y, c, chips = _place()
        for j, chip in enumerate(chips):
            for k in range(n):
                blk = outs[k].at[_chip_rows(chip, rows[k]), _cols(half(c))]
                yield pltpu.make_async_remote_copy(src_ref=blk, dst_ref=blk, send_sem=send_sems.at[j, k], recv_sem=recv_sems.at[j, k],
                                                   device_id=(x, y, 1 - c), device_id_type=MESH)

    def start(ins, outs, send_sems, recv_sems):
        for cp in copies(outs, send_sems, recv_sems, lambda c: c):
            cp.start()

    def wait(ins, outs, send_sems, recv_sems):
        for cp in copies(outs, send_sems, recv_sems, lambda c: 1 - c):
            cp.wait_recv()
        for cp in copies(outs, send_sems, recv_sems, lambda c: c):
            cp.wait_send()

    return _Rider(list(bufs), None, (3, n), start, wait)


def _sibling_rider(grads):
    n = len(grads)

    def copies(ins, outs, send_sems, recv_sems):
        x, y, c, _ = _place()
        return [pltpu.make_async_remote_copy(src_ref=ins[k].at[:, pl.ds(1 - c, 1)], dst_ref=outs[k], send_sem=send_sems.at[k],
                                             recv_sem=recv_sems.at[k], device_id=(x, y, 1 - c), device_id_type=MESH)
                for k in range(n)]

    def start(*refs):
        for cp in copies(*refs):
            cp.start()

    def wait(*refs):
        for cp in copies(*refs):
            cp.wait()

    return _Rider(list(grads), [jax.ShapeDtypeStruct((N_CHIPS, 1) + g.shape[2:], F32) for g in grads], (n,), start, wait)


def _alone(rider, name):
    n = len(rider.operands)
    landing = rider.landing if rider.landing is not None else [jax.ShapeDtypeStruct(a.shape, a.dtype) for a in rider.operands]
    n_out = len(landing)

    def body(*refs):
        rider.start(refs[:n], refs[n:n + n_out], *refs[n + n_out:])
        rider.wait(refs[:n], refs[n:n + n_out], *refs[n + n_out:])

    any_spec = pl.BlockSpec(memory_space=pl.ANY)
    return pl.pallas_call(body, name=name, in_specs=[any_spec] * n, out_specs=[any_spec] * n_out, out_shape=landing,
                          input_output_aliases={i: i for i in range(n)} if rider.landing is None else {},
                          scratch_shapes=[pltpu.SemaphoreType.DMA(rider.sems)] * 2)(*rider.operands)


def _chip_sum(grad, from_sibling, place, name):
    rh, width = grad.shape[2:]

    def body(place_ref, g_ref, s_ref, own_ref, all_ref):
        all_ref[...] = (g_ref[...] + s_ref[...]).astype(BF16)
        mine = place_ref[0]
        own_ref[0] = g_ref[mine, 0] + s_ref[mine, 0]

    blk = (N_CHIPS, 1, rh, width)
    once = pl.Buffered(1)
    return pl.pallas_call(
        body, name=name,
        grid_spec=pltpu.PrefetchScalarGridSpec(
            num_scalar_prefetch=1, grid=(1,),
            in_specs=[pl.BlockSpec(blk, lambda i, place: (0, place[1], 0, 0), pipeline_mode=once),
                      pl.BlockSpec(blk, lambda i, place: (0, 0, 0, 0), pipeline_mode=once)],
            out_specs=[pl.BlockSpec((1, rh, width), lambda i, place: (0, 0, 0), pipeline_mode=once),
                       pl.BlockSpec(blk, lambda i, place: (0, 0, 0, 0), pipeline_mode=once)]),
        out_shape=[jax.ShapeDtypeStruct((1, rh, width), F32), jax.ShapeDtypeStruct((N_CHIPS, 1, rh, width), BF16)],
        compiler_params=_params(dimension_semantics=("arbitrary",)))(place, grad, from_sibling)


def _scatter_rider(sums):
    n = len(sums)

    def copies(ins, outs, send_sems, recv_sems):
        x, y, c, chips = _place()
        return [pltpu.make_async_remote_copy(src_ref=ins[k].at[pl.ds(2 * chip[0] + chip[1], 1)], dst_ref=outs[k].at[pl.ds(j, 1)],
                                             send_sem=send_sems.at[j, k], recv_sem=recv_sems.at[j, k],
                                             device_id=(*chip, c), device_id_type=MESH)
                for j, chip in enumerate(chips) for k in range(n)]

    def start(*refs):
        for cp in copies(*refs):
            cp.start()

    def wait(*refs):
        for cp in copies(*refs):
            cp.wait()

    return _Rider(list(sums), [jax.ShapeDtypeStruct((3,) + sm.shape[1:], BF16) for sm in sums], (3, n), start, wait)


def _total_sums(owns, received, name):
    n = len(owns)

    def body(*refs):
        for o_ref, r_ref, t_ref in zip(refs[:n], refs[n:2 * n], refs[2 * n:]):
            total = o_ref[0]
            for j in range(3):
                total = total + r_ref[j, 0].astype(F32)
            t_ref[0] = total

    return _hosted_call(body, None, name=name, steps=1, in_specs=[_resident(a.shape) for a in owns + received],
                        out_specs=[_resident(o.shape) for o in owns], out_shape=[jax.ShapeDtypeStruct(o.shape, F32) for o in owns],
                        args=owns + received)[0]


def _swap_rider(halves):
    n = len(halves)

    def copies(ins, outs, send_sems, recv_sems):
        x, y, c, _ = _place()
        return [pltpu.make_async_remote_copy(src_ref=ins[k], dst_ref=outs[k], send_sem=send_sems.at[k], recv_sem=recv_sems.at[k],
                                             device_id=(x, y, 1 - c), device_id_type=MESH) for k in range(n)]

    def start(*refs):
        for cp in copies(*refs):
            cp.start()

    def wait(*refs):
        for cp in copies(*refs):
            cp.wait()

    return _Rider(list(halves), [jax.ShapeDtypeStruct(h.shape, F32) for h in halves], (n,), start, wait)


N_DEV = 8


def _gather_small(block):
    m_per, width = block.shape

    def body(x_ref, out_ref, send_sems, recv_sems, local_sem):
        x, y, c, chips = _place()
        me, sibling = (x, y, c), (x, y, 1 - c)

        def rows(px, py, pc):
            return out_ref.at[pl.ds((4 * px + 2 * py + pc) * m_per, m_per), :]

        def copy(k, blk, to, src=None):
            return pltpu.make_async_remote_copy(src_ref=rows(*blk) if src is None else src, dst_ref=rows(*blk),
                                                send_sem=send_sems.at[k], recv_sem=recv_sems.at[k], device_id=to, device_id_type=MESH)

        mine = pltpu.make_async_copy(x_ref, rows(*me), local_sem)
        mine.start()
        first = [copy(0, me, sibling, src=x_ref)] + [copy(1 + j, me, (*chip, c), src=x_ref) for j, chip in enumerate(chips)]
        for cp in first:
            cp.start()
        passed = [copy(4 + j, (*chip, c), sibling) for j, chip in enumerate(chips)]
        for j, chip in enumerate(chips):
            copy(1 + j, (*chip, c), me).wait_recv()
            passed[j].start()
        copy(0, sibling, me).wait_recv()
        for j, chip in enumerate(chips):
            copy(4 + j, (*chip, 1 - c), me).wait_recv()
        for cp in first + passed:
            cp.wait_send()
        mine.wait()

    vmem = pl.BlockSpec(memory_space=pltpu.VMEM)
    return pl.pallas_call(body, name="gather_small", out_shape=jax.ShapeDtypeStruct((N_DEV * m_per, width), F32),
                          in_specs=[vmem], out_specs=vmem,
                          scratch_shapes=[pltpu.SemaphoreType.DMA((7,)), pltpu.SemaphoreType.DMA((7,)),
                                          pltpu.SemaphoreType.DMA])(block)


def _adamw_math(w, g, m, v):
    m = ADAM_B1 * m + (1.0 - ADAM_B1) * g
    v = ADAM_B2 * v + (1.0 - ADAM_B2) * (g * g)
    m_hat = m / (1.0 - ADAM_B1 ** ADAM_STEP)
    v_hat = v / (1.0 - ADAM_B2 ** ADAM_STEP)
    delta = -ADAM_LR * (m_hat / (jnp.sqrt(v_hat) + ADAM_EPS) + ADAM_WD * w)
    return delta, m, v


def _adamw(w, mine, siblings, place, m, v, transposed, name):
    rh, width = mine.shape[1:]
    place_spec = pl.BlockSpec(memory_space=pltpu.SMEM)
    halves = [_const((1, rh, width))] * 2
    out_shape = [jax.ShapeDtypeStruct(w.shape, F32)] * 4
    if transposed:
        def body(place_ref, w_ref, mine_ref, sib_ref, m_ref, v_ref, go_ref, d_ref, mo_ref, vo_ref):
            first = place_ref[1] == 0
            g = jnp.concatenate([jnp.where(first, mine_ref[0], sib_ref[0]), jnp.where(first, sib_ref[0], mine_ref[0])], axis=0).T
            go_ref[...] = g
            d_ref[...], mo_ref[...], vo_ref[...] = _adamw_math(w_ref[...], g, m_ref[...], v_ref[...])

        whole = _resident(w.shape)
        return _hosted_call(body, None, name=name, steps=1, in_specs=[place_spec, whole] + halves + [whole, whole],
                            out_specs=[whole] * 4, out_shape=out_shape, args=[place, w, mine, siblings, m, v])[0]

    def body(place_ref, w_ref, mine_ref, sib_ref, m_ref, v_ref, go_ref, d_ref, mo_ref, vo_ref):
        g = jnp.where(pl.program_id(0) == place_ref[1], mine_ref[0], sib_ref[0])
        go_ref[...] = g
        d_ref[...], mo_ref[...], vo_ref[...] = _adamw_math(w_ref[...], g, m_ref[...], v_ref[...])

    half = _rows(rh, width)
    return _hosted_call(body, None, name=name, steps=2, in_specs=[place_spec, half] + halves + [half, half],
                        out_specs=[half] * 4, out_shape=out_shape, args=[place, w, mine, siblings, m, v])[0]


def _adamw_small(gathered, w, m, v, name):
    def body(ga_ref, w_ref, m_ref, v_ref, go_ref, d_ref, mo_ref, vo_ref):
        g = ga_ref[0]
        for dev in range(1, N_DEV):
            g = g + ga_ref[dev]
        go_ref[...] = g
        d_ref[...], mo_ref[...], vo_ref[...] = _adamw_math(w_ref[...], g, m_ref[...], v_ref[...])

    return pl.pallas_call(body, name=name, out_shape=[jax.ShapeDtypeStruct(w.shape, F32)] * 4,
                          compiler_params=_params())(gathered, w, m, v)


class _Exchange:
    FIRST = ("w1_gate", "w1_up", "w1_down")
    HOSTS = {"ffn2_wgrad_gate": (("w2_down",), ()), "ffn2_wgrad_up": (("w2_gate",), ("w2_down",)),
             "in_bwd": (("w2_up",), ("w2_gate",)), "wgrad_in": ((), ("w2_up",)),
             "ffn1_wgrad_down": ((), ("w_in",)), "ffn1_wgrad_gate": (("w1_down",), ()), "ffn1_wgrad_up": ((), ("w1_down", "w1_gate")),
             "wgrad_out": ((), ("w1_up",))}
    ALONE = ("w_in", "w1_gate", "w1_up", "w_out")

    def __init__(self, bufs, place):
        self.bufs, self.place = bufs, place
        self.later = [k for k in SEGMENTS if k not in self.FIRST]
        self.split, self.own, self.to_send, self.received = {}, {}, {}, {}

    def first_weights(self):
        return dict(zip(self.FIRST, _gather_weights([self.bufs[k] for k in self.FIRST])))

    def riders(self, host):
        if host == "ffn1_fwd":
            return [_gather_rider([self.bufs[k] for k in self.later])]
        if host == "in_fwd":
            return [_forward_rider([self.bufs[k] for k in self.later[1:]])]
        halves, sums = self.HOSTS.get(host, ((), ()))
        return ([_sibling_rider([self.split[k] for k in halves])] if halves else []) + (
            [_scatter_rider([self.to_send[k] for k in sums])] if sums else [])

    def landed(self, host, results):
        if host == "ffn1_fwd":
            self.bufs.update(zip(self.later, results[0]))
            return dict(zip(self.later[:1], _alone(_forward_rider([self.bufs[self.later[0]]]), "gather_forward_first")))
        if host == "in_fwd":
            return dict(zip(self.later[1:], results[0]))
        halves, sums = self.HOSTS.get(host, ((), ()))
        if halves:
            self._chip_sums(halves, results[0])
        if sums:
            self.received.update(zip(sums, results[-1]))

    def gradient(self, name, grad):
        self.split[name] = grad.reshape(N_CHIPS, 2, grad.shape[0] // (2 * N_CHIPS), grad.shape[1])
        if name in self.ALONE:
            self._chip_sums([name], _alone(_sibling_rider([self.split[name]]), f"reduce_sibling_{name}"))

    def _chip_sums(self, names, from_sibling):
        for k, fs in zip(names, from_sibling):
            self.own[k], self.to_send[k] = _chip_sum(self.split[k], fs, self.place, f"chip_sum_{k}")

    def summed_halves(self):
        late = [k for k in SEGMENTS if k not in self.received]
        self.received.update(zip(late, _alone(_scatter_rider([self.to_send[k] for k in late]), "reduce_chips_last")))
        mine = _total_sums([self.own[k] for k in SEGMENTS], [self.received[k] for k in SEGMENTS], "total_sums")
        return mine, _alone(_swap_rider(mine), "swap_halves")


SMALL = ("g_ffn1_pre", "g_ffn1_post", "g_mix_pre", "w_pool_lin", "pool_scale", "g_mix_post", "g_ffn2_pre", "g_ffn2_post")
WEIGHTS = ("g_ffn1_pre", "w1_gate", "w1_up", "w1_down", "g_ffn1_post", "g_mix_pre", "w_in", "w_pool_lin", "pool_scale", "w_out",
           "g_mix_post", "g_ffn2_pre", "w2_gate", "w2_up", "w2_down", "g_ffn2_post")
LANES = 128


def _pack_small(tree, extra=0.0):
    flat = jnp.concatenate([tree[k].reshape(-1) for k in SMALL] + [jnp.reshape(extra, (1,)).astype(F32)])
    rows = -(-flat.shape[0] // (8 * LANES)) * 8
    return jnp.pad(flat, (0, rows * LANES - flat.shape[0])).reshape(rows, LANES)


def _unpack_small(packed, like):
    flat, out, at = packed.reshape(-1), {}, 0
    for k in SMALL:
        size = math.prod(like[k].shape)
        out[k] = flat[at:at + size].reshape(like[k].shape)
        at += size
    return out


def kernel(x, g_ffn1_pre, w1_gate, w1_up, w1_down, g_ffn1_post, g_mix_pre, w_in, w_pool_lin, pool_scale, w_out, g_mix_post, g_ffn2_pre, w2_gate, w2_up, w2_down, g_ffn2_post, loss_target, m_g_ffn1_pre, m_w1_gate, m_w1_up, m_w1_down, m_g_ffn1_post, m_g_mix_pre, m_w_in, m_w_pool_lin, m_pool_scale, m_w_out, m_g_mix_post, m_g_ffn2_pre, m_w2_gate, m_w2_up, m_w2_down, m_g_ffn2_post, v_g_ffn1_pre, v_w1_gate, v_w1_up, v_w1_down, v_g_ffn1_post, v_g_mix_pre, v_w_in, v_w_pool_lin, v_pool_scale, v_w_out, v_g_mix_post, v_g_ffn2_pre, v_w2_gate, v_w2_up, v_w2_down, v_g_ffn2_post):
    given = dict(locals())
    w = {k: given[k] for k in WEIGHTS}
    m = {k: given["m_" + k] for k in WEIGHTS}
    v = {k: given["v_" + k] for k in WEIGHTS}
    small = {k: (w[k][0] if k == "w_pool_lin" else w[k].reshape(1, -1)) for k in SMALL}

    place = jnp.stack([2 * lax.axis_index("x") + lax.axis_index("y"), lax.axis_index("c")]).astype(jnp.int32)
    def as_rows(a, k):
        return jnp.swapaxes(a, 1, 2)[0] if k in ROWS_OUTSIDE else a[0]

    def as_given(a, k):
        return jnp.swapaxes(a[None], 1, 2) if k in ROWS_OUTSIDE else a[None]

    in_kernel = [k for k in TRANSPOSED if k not in ROWS_OUTSIDE]
    bufs = {}
    for tag, names in (("first", _Exchange.FIRST), ("rest", [k for k in SEGMENTS if k not in _Exchange.FIRST])):
        bufs.update(zip(names, _cast_shards([as_rows(w[k], k) for k in names], [k in in_kernel for k in names], place, f"cast_{tag}")))
    exchange = _Exchange(bufs, place)
    loss_part, grad_x, small_grads = _local_step(x[0], loss_target[0], small, exchange)

    out_grad, out_delta, out_m, out_v = {}, {}, {}, {}
    for k, mine, siblings in zip(SEGMENTS, *exchange.summed_halves()):
        results = _adamw(as_rows(w[k], k), mine, siblings, place, as_rows(m[k], k), as_rows(v[k], k), k in in_kernel, f"adamw_{k}")
        out_grad[k], out_delta[k], out_m[k], out_v[k] = (as_given(a, k) for a in results)

    small_grads["w_pool_lin"] = small_grads["w_pool_lin"][None]
    packed = _pack_small(small_grads, loss_part)
    gathered = _gather_small(packed).reshape(N_DEV, *packed.shape)
    like = {k: w[k] for k in SMALL}
    results = _adamw_small(gathered, _pack_small(like), _pack_small({k: m[k] for k in SMALL}),
                           _pack_small({k: v[k] for k in SMALL}), "adamw_small")
    for tree, res in zip((out_grad, out_delta, out_m, out_v), results):
        tree.update(_unpack_small(res, like))
    loss = results[0].reshape(-1)[sum(math.prod(like[k].shape) for k in SMALL)]

    return (loss, grad_x[None], *[out_grad[k] for k in WEIGHTS], *[out_delta[k] for k in WEIGHTS],
            *[out_m[k] for k in WEIGHTS], *[out_v[k] for k in WEIGHTS])
```

```python
import math
import typing

import numpy as np
import jax
import jax.numpy as jnp
from jax import lax
from jax.experimental import pallas as pl
from jax.experimental.pallas import tpu as pltpu

F32 = jnp.float32
BF16 = jnp.bfloat16
MESH = pl.DeviceIdType.MESH

RMS_EPS = 1e-6
HEAD_DIM = 64
POOL_HALF_WINDOWS = (1, 2, 4, 8)
POOL_DIM = 256
GROUP_DIM = 256
DILATIONS = (1, 4, 16)
N_SIDE = 64
N_ATTN_HEADS = 12
ADAM_LR, ADAM_B1, ADAM_B2, ADAM_EPS, ADAM_WD, ADAM_STEP = 0.001, 0.9, 0.999, 1e-08, 0.01, 10

N_CHIPS = 4
V7X_VMEM_LIMIT = 60 * 1024 * 1024

_NT = (((1,), (1,)), ((), ()))
_TN = (((0,), (0,)), ((), ()))


def _dot(a, b):
    return jnp.dot(a, b, preferred_element_type=F32)


def _dot_nt(a, b):
    return lax.dot_general(a, b, _NT, preferred_element_type=F32)


def _dot_tn(a, b):
    return lax.dot_general(a, b, _TN, preferred_element_type=F32)


def _params(**kw):
    return pltpu.CompilerParams(vmem_limit_bytes=V7X_VMEM_LIMIT, **kw)


def _rows(tm, width):
    return pl.BlockSpec((tm, width), lambda i: (i, 0))


def _resident(shape):
    return pl.BlockSpec(shape, lambda i: (0,) * len(shape), pipeline_mode=pl.Buffered(1))


def _const(shape):
    return pl.BlockSpec(shape, lambda i: (0,) * len(shape))


def _inv_rms(x):
    return lax.rsqrt(jnp.mean(x * x, axis=-1, keepdims=True) + RMS_EPS)


def _rms_bwd(x, inv, g, dy):
    n = x * inv
    dn = dy * g
    dx = inv * (dn - n * jnp.mean(dn * n, axis=-1, keepdims=True))
    return dx, jnp.sum(dy * n, axis=0, keepdims=True)


def _accumulate(ref, value):
    @pl.when(pl.program_id(0) == 0)
    def _():
        ref[...] = jnp.zeros_like(ref)

    ref[...] += value


class _Rider(typing.NamedTuple):
    operands: list
    landing: typing.Optional[list]
    sems: tuple
    start: typing.Callable
    wait: typing.Callable


def _hosted_call(body, riders, *, name, steps, in_specs, out_specs, out_shape, args, scratch_shapes=()):
    params = _params(dimension_semantics=("arbitrary",))
    riders = list(riders or [])
    if not riders:
        res = pl.pallas_call(body, name=name, grid=(steps,), in_specs=in_specs, out_specs=out_specs, out_shape=out_shape,
                             scratch_shapes=list(scratch_shapes), compiler_params=params)(*args)
        return list(res), []
    n_in, n_out, n_scratch = len(in_specs), len(out_specs), len(scratch_shapes)
    operands, landing, aliases, spans = [], [], {}, []
    for rd in riders:
        lands = rd.landing if rd.landing is not None else [jax.ShapeDtypeStruct(a.shape, a.dtype) for a in rd.operands]
        if rd.landing is None:
            aliases.update({n_in + len(operands) + i: n_out + len(landing) + i for i in range(len(lands))})
        spans.append((len(operands), len(rd.operands), len(landing), len(lands)))
        operands += rd.operands
        landing += lands
    outs_at = n_in + len(operands)
    scratch_at = outs_at + n_out + len(landing)

    def riding(*refs):
        def each(action):
            for i, (rd, (in_at, n_ops, out_at, n_lands)) in enumerate(zip(riders, spans)):
                sems = refs[scratch_at + n_scratch + 2 * i:scratch_at + n_scratch + 2 * i + 2]
                getattr(rd, action)(refs[n_in + in_at:n_in + in_at + n_ops],
                                    refs[outs_at + n_out + out_at:outs_at + n_out + out_at + n_lands], *sems)

        @pl.when(pl.program_id(0) == 0)
        def _():
            each("start")

        body(*refs[:n_in], *refs[outs_at:outs_at + n_out], *refs[scratch_at:scratch_at + n_scratch])

        @pl.when(pl.program_id(0) == steps - 1)
        def _():
            each("wait")

    any_spec = pl.BlockSpec(memory_space=pl.ANY)
    res = pl.pallas_call(
        riding, name=name, grid=(steps,), in_specs=list(in_specs) + [any_spec] * len(operands),
        out_specs=list(out_specs) + [any_spec] * len(landing), out_shape=list(out_shape) + landing,
        scratch_shapes=list(scratch_shapes) + [pltpu.SemaphoreType.DMA(rd.sems) for rd in riders for _ in range(2)],
        input_output_aliases=aliases, compiler_params=params)(*args, *operands)
    return list(res[:n_out]), [list(res[n_out + out_at:n_out + out_at + n_lands]) for _, _, out_at, n_lands in spans]


_SUB_TILE = 256


def _sub_tiles(tm):
    return [pl.ds(r, _SUB_TILE) for r in range(0, tm, _SUB_TILE)]


def _ffn_fwd(x, g_pre, wg_t, wu_t, wd, g_post, target, name, riders=None, tm=512):
    s, d = x.shape
    ff = wd.shape[0]
    with_loss = target is not None

    def body(*refs):
        if with_loss:
            x_ref, gpre_ref, wg_ref, wu_ref, wd_ref, gpost_ref, t_ref, xo_ref, a_ref, b_ref, f_ref, loss_ref = refs
        else:
            x_ref, gpre_ref, wg_ref, wu_ref, wd_ref, gpost_ref, xo_ref, a_ref, b_ref, f_ref = refs
        loss = 0.0
        for rows in _sub_tiles(tm):
            xv = x_ref[rows, :]
            hb = (xv * _inv_rms(xv) * gpre_ref[...]).astype(BF16)
            a = _dot_nt(hb, wg_ref[...])
            b = _dot_nt(hb, wu_ref[...])
            hh = (a * jax.nn.sigmoid(a)) * b
            f = _dot(hh.astype(BF16), wd_ref[...])
            xo = xv + 0.5 * (f * _inv_rms(f) * gpost_ref[...])
            a_ref[rows, :] = a.astype(BF16)
            b_ref[rows, :] = b.astype(BF16)
            f_ref[rows, :] = f
            if with_loss:
                e = xo - t_ref[rows, :]
                xo_ref[rows, :] = e * (1.0 / d)
                loss = loss + 0.5 * jnp.sum(jnp.mean(e * e, axis=-1, keepdims=True))
            else:
                xo_ref[rows, :] = xo
        if with_loss:
            _accumulate(loss_ref, loss)

    in_specs = [_rows(tm, d), _const((1, d)), _resident((ff, d)), _resident((ff, d)), _resident((ff, d)), _const((1, d))]
    args = [x, g_pre, wg_t, wu_t, wd, g_post]
    out_shape = [jax.ShapeDtypeStruct((s, d), F32), jax.ShapeDtypeStruct((s, ff), BF16),
                 jax.ShapeDtypeStruct((s, ff), BF16), jax.ShapeDtypeStruct((s, d), F32)]
    out_specs = [_rows(tm, d), _rows(tm, ff), _rows(tm, ff), _rows(tm, d)]
    if with_loss:
        in_specs.append(_rows(tm, d))
        args.append(target)
        out_shape.append(jax.ShapeDtypeStruct((8, 128), F32))
        out_specs.append(_const((8, 128)))
    return _hosted_call(body, riders, name=name, steps=s // tm, in_specs=in_specs, out_specs=out_specs, out_shape=out_shape, args=args)


def _ffn_bwd(dxo, x, f, a, b, g_pre, g_post, wg_t, wu_t, wd, name, riders=None, tm=256):
    s, d = x.shape
    ff = wd.shape[0]

    def body(dxo_ref, x_ref, f_ref, a_ref, b_ref, gpre_ref, gpost_ref, wg_ref, wu_ref, wd_ref,
             dx_ref, hh_ref, da_ref, db_ref, df_ref, h_ref, dgpre_ref, dgpost_ref):
        dgpre_sum = dgpost_sum = 0.0
        for rows in _sub_tiles(tm):
            dxo_v = dxo_ref[rows, :]
            fv = f_ref[rows, :]
            df, dgpost = _rms_bwd(fv, _inv_rms(fv), gpost_ref[...], 0.5 * dxo_v)
            dfb = df.astype(BF16)
            dhh = _dot_nt(dfb, wd_ref[...])
            av = a_ref[rows, :].astype(F32)
            bv = b_ref[rows, :].astype(F32)
            sig = jax.nn.sigmoid(av)
            sa = av * sig
            da = (dhh * bv * (sig * (1.0 + av * (1.0 - sig)))).astype(BF16)
            db = (dhh * sa).astype(BF16)
            dh = _dot(da, wg_ref[...]) + _dot(db, wu_ref[...])
            xv = x_ref[rows, :]
            inv = _inv_rms(xv)
            dxn, dgpre = _rms_bwd(xv, inv, gpre_ref[...], dh)
            dx_ref[rows, :] = dxo_v + dxn
            hh_ref[rows, :] = (sa * bv).astype(BF16)
            da_ref[rows, :] = da
            db_ref[rows, :] = db
            df_ref[rows, :] = dfb
            h_ref[rows, :] = (xv * inv * gpre_ref[...]).astype(BF16)
            dgpre_sum, dgpost_sum = dgpre_sum + dgpre, dgpost_sum + dgpost
        _accumulate(dgpre_ref, dgpre_sum)
        _accumulate(dgpost_ref, dgpost_sum)

    return _hosted_call(
        body, riders, name=name, steps=s // tm,
        in_specs=[_rows(tm, d), _rows(tm, d), _rows(tm, d), _rows(tm, ff), _rows(tm, ff), _const((1, d)), _const((1, d)),
                  _resident((ff, d)), _resident((ff, d)), _resident((ff, d))],
        out_specs=[_rows(tm, d), _rows(tm, ff), _rows(tm, ff), _rows(tm, ff), _rows(tm, d), _rows(tm, d),
                   _const((1, d)), _const((1, d))],
        out_shape=[jax.ShapeDtypeStruct((s, d), F32), jax.ShapeDtypeStruct((s, ff), BF16), jax.ShapeDtypeStruct((s, ff), BF16),
                   jax.ShapeDtypeStruct((s, ff), BF16), jax.ShapeDtypeStruct((s, d), BF16), jax.ShapeDtypeStruct((s, d), BF16),
                   jax.ShapeDtypeStruct((1, d), F32), jax.ShapeDtypeStruct((1, d), F32)],
        args=[dxo, x, f, a, b, g_pre, g_post, wg_t, wu_t, wd])


def _wgrad(lhs, rhs, name, riders=None, rt=256):
    s, r = lhs.shape
    c = rhs.shape[1]

    def body(l_ref, r_ref, o_ref):
        o_ref[...] = _dot_tn(l_ref[...], r_ref[...])

    (out,), riding = _hosted_call(
        body, riders, name=name, steps=pl.cdiv(r, rt), in_specs=[pl.BlockSpec((s, rt), lambda i: (0, i)), _resident((s, c))],
        out_specs=[pl.BlockSpec((rt, c), lambda i: (i, 0))], out_shape=[jax.ShapeDtypeStruct((r, c), F32)], args=[lhs, rhs])
    return out, riding


def _attn_dtype(dilation):
    return BF16 if dilation == 1 else F32


def _in_fwd(x, g, w_in_t, name, riders=None, tm=1024):
    s, d = x.shape
    d_in = w_in_t.shape[0]
    n_groups = len(DILATIONS)
    dtypes = [_attn_dtype(dil) for dil in DILATIONS] * 3

    def body(x_ref, g_ref, w_ref, h_ref, u_ref, *part_refs):
        xv = x_ref[...]
        hb = (xv * _inv_rms(xv) * g_ref[...]).astype(BF16)
        h_ref[...] = hb
        z = _dot_nt(hb, w_ref[...])
        u_ref[...] = z[:, :POOL_DIM]
        for j, ref in enumerate(part_refs):
            part = z[:, POOL_DIM + GROUP_DIM * j:POOL_DIM + GROUP_DIM * (j + 1)]
            ref[...] = (part * _SCORE_SCALE if j < n_groups else part).astype(ref.dtype)

    return _hosted_call(
        body, riders, name=name, steps=s // tm, in_specs=[_rows(tm, d), _const((1, d)), _resident((d_in, d))],
        out_specs=[_rows(tm, d), _rows(tm, POOL_DIM)] + [_rows(tm, GROUP_DIM)] * len(dtypes),
        out_shape=[jax.ShapeDtypeStruct((s, d), BF16), jax.ShapeDtypeStruct((s, POOL_DIM), F32)]
        + [jax.ShapeDtypeStruct((s, GROUP_DIM), dt) for dt in dtypes],
        args=[x, g, w_in_t])


def _in_bwd(du, dparts, x, dxo, g, w_in_t, name, riders=None, tm=512):
    s, d = x.shape
    d_in = w_in_t.shape[0]
    n_parts = len(dparts)

    def body(du_ref, *refs):
        part_refs = refs[:n_parts]
        x_ref, dxo_ref, g_ref, w_ref, dx_ref, dg_ref = refs[n_parts:]
        dh = _dot(jnp.concatenate([r[...] for r in (du_ref,) + part_refs], axis=1), w_ref[...])
        xv = x_ref[...]
        dxn, dg = _rms_bwd(xv, _inv_rms(xv), g_ref[...], dh)
        dx_ref[...] = dxo_ref[...] + dxn
        _accumulate(dg_ref, dg)

    return _hosted_call(
        body, riders, name=name, steps=s // tm,
        in_specs=[_rows(tm, POOL_DIM)] + [_rows(tm, GROUP_DIM)] * n_parts + [_rows(tm, d), _rows(tm, d), _const((1, d)),
                                                                             _resident((d_in, d))],
        out_specs=[_rows(tm, d), _const((1, d))],
        out_shape=[jax.ShapeDtypeStruct((s, d), F32), jax.ShapeDtypeStruct((1, d), F32)],
        args=[du, *dparts, x, dxo, g, w_in_t])


def _wgrad_parts(parts, rhs, name, riders=None):
    n = len(parts)
    s, rt = parts[0].shape
    c = rhs.shape[1]

    def body(*refs):
        part_refs, r_ref, o_ref, buf, sems = refs[:n], refs[n], refs[n + 1], refs[n + 2], refs[n + 3]

        def fetch(i):
            return pltpu.make_async_copy(part_refs[i], buf.at[i % 2], sems.at[i % 2])

        fetch(0).start()
        for i in range(n):
            if i + 1 < n:
                fetch(i + 1).start()
            fetch(i).wait()
            o_ref[pl.ds(i * rt, rt), :] = _dot_tn(buf[i % 2], r_ref[...])

    (out,), riding = _hosted_call(
        body, riders, name=name, steps=1, in_specs=[pl.BlockSpec(memory_space=pl.ANY)] * n + [_resident((s, c))],
        out_specs=[_resident((n * rt, c))], out_shape=[jax.ShapeDtypeStruct((n * rt, c), F32)], args=[*parts, rhs],
        scratch_shapes=[pltpu.VMEM((2, s, rt), BF16), pltpu.SemaphoreType.DMA((2,))])
    return out, riding


_POOL_HALO = 8


def _pool_chain(v, first_shift):
    n = v.shape[0]
    p2 = v + pltpu.roll(v, first_shift, 0)
    p4 = pltpu.roll(p2, 1, 0) + pltpu.roll(p2, n - 1, 0)
    p8 = pltpu.roll(p4, 2, 0) + pltpu.roll(p4, n - 2, 0)
    p16 = pltpu.roll(p8, 4, 0) + pltpu.roll(p8, n - 4, 0)
    group = lax.broadcasted_iota(jnp.int32, v.shape, 1) // HEAD_DIM
    return jnp.where(group == 0, p2, jnp.where(group == 1, p4, jnp.where(group == 2, p8, p16)))


def _pool_count(t0, rows, s):
    t = t0 + lax.broadcasted_iota(jnp.int32, (rows, POOL_DIM), 0)
    group = lax.broadcasted_iota(jnp.int32, (rows, POOL_DIM), 1) // HEAD_DIM
    half = jnp.where(group == 0, 1, jnp.where(group == 1, 2, jnp.where(group == 2, 4, 8)))
    cnt = jnp.minimum(t + half, s) - jnp.maximum(t - half, 0)
    return jnp.maximum(cnt, 1).astype(F32)


def _pad_rows(ref, pad_ref, s):
    zeros = jnp.zeros((_POOL_HALO, pad_ref.shape[1]), pad_ref.dtype)
    pad_ref[pl.ds(0, _POOL_HALO), :] = zeros
    pad_ref[pl.ds(_POOL_HALO + s, _POOL_HALO), :] = zeros
    pad_ref[pl.ds(_POOL_HALO, s), :] = ref[...]


def _pool_fwd(u, w_bd, scale, name, tm=512):
    s = u.shape[0]
    ext = tm + 2 * _POOL_HALO

    def body(u_ref, w_ref, sc_ref, o_ref, upad):
        _pad_rows(u_ref, upad, s)

        def tile(i, carry):
            t0 = pl.multiple_of(i * tm, tm)
            uv = upad[pl.ds(t0, ext), :]
            win = _pool_chain(uv, 1)[_POOL_HALO:_POOL_HALO + tm]
            y = win / _pool_count(t0, tm, s) - uv[_POOL_HALO:_POOL_HALO + tm]
            o_ref[pl.ds(t0, tm), :] = (_dot(y.astype(BF16), w_ref[...]) * sc_ref[...]).astype(BF16)
            return carry

        lax.fori_loop(0, s // tm, tile, 0)

    return pl.pallas_call(body, name=name, out_shape=jax.ShapeDtypeStruct((s, POOL_DIM), BF16),
                          scratch_shapes=[pltpu.VMEM((s + 2 * _POOL_HALO, POOL_DIM), F32)],
                          compiler_params=_params())(u, w_bd, scale)


def _pool_bwd(u, da, w_bd, scale, name, tm=512):
    s = u.shape[0]
    ext = tm + 2 * _POOL_HALO

    def body(u_ref, da_ref, w_ref, sc_ref, du_ref, dw_ref, dsc_ref, upad, dapad):
        _pad_rows(u_ref, upad, s)
        _pad_rows(da_ref, dapad, s)
        dw_ref[...] = jnp.zeros_like(dw_ref)
        dsc_ref[...] = jnp.zeros_like(dsc_ref)

        def tile(i, carry):
            t0 = pl.multiple_of(i * tm, tm)
            uv = upad[pl.ds(t0, ext), :]
            dav = dapad[pl.ds(t0, ext), :]
            win = _pool_chain(uv, 1)[_POOL_HALO:_POOL_HALO + tm]
            yb = (win / _pool_count(t0, tm, s) - uv[_POOL_HALO:_POOL_HALO + tm]).astype(BF16)
            yl = _dot(yb, w_ref[...])
            da_c = dav[_POOL_HALO:_POOL_HALO + tm]
            dsc_ref[...] += jnp.sum(da_c * yl, axis=0, keepdims=True)
            dyl = (dav * sc_ref[...]).astype(BF16)
            dw_ref[...] += _dot_tn(yb, dyl[_POOL_HALO:_POOL_HALO + tm])
            dy = _dot_nt(dyl, w_ref[...])
            dyc = dy / _pool_count(t0 - _POOL_HALO, ext, s)
            du_ref[pl.ds(t0, tm), :] = (_pool_chain(dyc, ext - 1) - dy)[_POOL_HALO:_POOL_HALO + tm].astype(BF16)
            return carry

        lax.fori_loop(0, s // tm, tile, 0)

    pool_cols = pl.BlockSpec((s, POOL_DIM), lambda i: (0, 0), pipeline_mode=pl.Buffered(1))
    return pl.pallas_call(
        body, name=name, grid=(1,),
        in_specs=[pool_cols, pool_cols, _const((POOL_DIM, POOL_DIM)), _const((1, POOL_DIM))],
        out_specs=[_const((s, POOL_DIM)), _const((POOL_DIM, POOL_DIM)), _const((1, POOL_DIM))],
        out_shape=[jax.ShapeDtypeStruct((s, POOL_DIM), BF16), jax.ShapeDtypeStruct((POOL_DIM, POOL_DIM), F32),
                   jax.ShapeDtypeStruct((1, POOL_DIM), F32)],
        scratch_shapes=[pltpu.VMEM((s + 2 * _POOL_HALO, POOL_DIM), F32), pltpu.VMEM((s + 2 * _POOL_HALO, POOL_DIM), F32)],
        compiler_params=_params(dimension_semantics=("arbitrary",)))(u, da, w_bd, scale)


_BQ = 128
_KW = _BQ + 2 * N_SIDE
_PAIR = 2 * HEAD_DIM
_NEG = -1e30
_ATTN_UNROLL = 8
_SCORE_SCALE = HEAD_DIM ** -0.5


def _stack_heads(x):
    lane_head = lax.broadcasted_iota(jnp.int32, x.shape, 1) // HEAD_DIM
    zero = jnp.zeros_like(x)
    return jnp.concatenate([jnp.where(lane_head == 0, x, zero), jnp.where(lane_head == 1, x, zero)], axis=0)


def _unstack_heads(x):
    lane_head = lax.broadcasted_iota(jnp.int32, (_BQ, _PAIR), 1) // HEAD_DIM
    return jnp.where(lane_head == 0, x[:_BQ], x[_BQ:])


def _stack_cols(x):
    return jnp.concatenate([x[:, 0:1], x[:, HEAD_DIM:HEAD_DIM + 1]], axis=0)


def _fill_bias(bias_ref, slopes_ref, dilation):
    row = lax.broadcasted_iota(jnp.int32, (2 * _BQ, _KW), 0)
    col = lax.broadcasted_iota(jnp.int32, (2 * _BQ, _KW), 1)
    pair = 2 * pl.program_id(0)
    slope = jnp.where(row < _BQ, slopes_ref[pair], slopes_ref[pair + 1]) * float(dilation)

    @pl.when(pl.program_id(1) == 0)
    def _():
        for j in range(3):
            dist = jnp.abs(col - (row & (_BQ - 1)) - j * N_SIDE)
            bias_ref[j] = jnp.where(dist <= N_SIDE, -slope * dist.astype(F32), _NEG)


def _block_window(i, n_blocks, length):
    q0 = pl.multiple_of(i * _BQ, _BQ)
    ws = pl.multiple_of(jnp.clip(q0 - N_SIDE, 0, length - _KW), N_SIDE)
    return q0, ws, jnp.where(i == 0, 0, jnp.where(i == n_blocks - 1, 2, 1))


_FREE_STRIDE = 4


def _residues_per_step(dilation):
    return max(dilation // _FREE_STRIDE, 1)


def _residue_views(dilation, seq, ins, outs, tmps):
    step = pl.program_id(1)
    if dilation <= _FREE_STRIDE:
        def rows(start, count, sub=0):
            return pl.ds(start, count) if dilation == 1 else pl.ds(start * dilation + step, count, stride=dilation)

        return ins, outs, rows, lambda: None
    inner = _residues_per_step(dilation)
    assert inner <= _FREE_STRIDE and len(tmps) == len(ins) + len(outs)
    coarse = pl.ds(step, seq // _FREE_STRIDE, stride=_FREE_STRIDE)
    in_tmps, out_tmps = tmps[:len(ins)], tmps[len(ins):]
    for ref, tmp in zip(ins, in_tmps):
        tmp[...] = ref[coarse, :]

    def flush():
        for ref, tmp in zip(outs, out_tmps):
            ref[coarse, :] = tmp[...]

    return in_tmps, out_tmps, lambda start, count, sub=0: pl.ds(start * inner + sub, count, stride=inner), flush


def _of_sub(ref, sub):
    return ref.at[sub] if len(ref.shape) == 3 else ref


def _attn_call(body, name, dilation, seq, n_in, out_dtypes, scratch, buffers):
    col = pl.BlockSpec((seq, _PAIR), lambda c, r: (0, c), pipeline_mode=pl.Buffered(buffers))
    tmps = [pltpu.VMEM((seq // _FREE_STRIDE, _PAIR), F32)] * (n_in + len(out_dtypes) if dilation > _FREE_STRIDE else 0)
    return pl.pallas_call(
        body, name=name, grid=(GROUP_DIM // _PAIR, dilation // _residues_per_step(dilation)),
        in_specs=[pl.BlockSpec(memory_space=pltpu.SMEM)] + [col] * n_in, out_specs=[col] * len(out_dtypes),
        out_shape=[jax.ShapeDtypeStruct((seq, GROUP_DIM), dt) for dt in out_dtypes], scratch_shapes=scratch + tmps,
        compiler_params=_params(dimension_semantics=("arbitrary", "arbitrary")))


def _staged(dilation, length, rows, sources, scratch):
    if dilation == 1:
        return sources
    for src, dst in zip(sources, scratch):
        for sub in range(_residues_per_step(dilation)):
            dst[sub] = src[rows(0, length, sub), :].astype(BF16)
    return scratch


def _sub_and_block(i, dilation, n_blocks):
    return (0, i) if _residues_per_step(dilation) == 1 else (i // n_blocks, i % n_blocks)


def _attn_fwd(q, k, v, slopes, dilation, name):
    seq = q.shape[0]
    length = seq // dilation
    n_blocks = length // _BQ
    n_stage = 0 if dilation == 1 else 3

    def body(sl_ref, q_ref, k_ref, v_ref, o_ref, lse_ref, *scratch):
        bias_ref, tmps = scratch[n_stage], scratch[n_stage + 1:]
        (q_in, k_in, v_in), (o_out, lse_out), rows, flush = _residue_views(dilation, seq, (q_ref, k_ref, v_ref), (o_ref, lse_ref), tmps)
        qs, ks, vs = _staged(dilation, length, rows, (q_in, k_in, v_in), scratch[:n_stage])
        _fill_bias(bias_ref, sl_ref, dilation)

        def block(i, carry):
            sub, j = _sub_and_block(i, dilation, n_blocks)
            q0, ws, which = _block_window(j, n_blocks, length)
            kw = _of_sub(ks, sub)[pl.ds(ws, _KW), :]
            vw = _of_sub(vs, sub)[pl.ds(ws, _KW), :]
            sc = _dot_nt(_stack_heads(_of_sub(qs, sub)[pl.ds(q0, _BQ), :]), kw) + bias_ref[which]
            m = jnp.max(sc, axis=-1, keepdims=True)
            p = jnp.exp(sc - m)
            den = jnp.sum(p, axis=-1, keepdims=True)
            o_out[rows(q0, _BQ, sub), :] = _unstack_heads(_dot(p.astype(BF16), vw) / den)
            lse_out[rows(q0, _BQ, sub), :] = _unstack_heads(jnp.broadcast_to(m + jnp.log(den), (2 * _BQ, _PAIR)))
            return carry

        lax.fori_loop(0, trips, block, 0, unroll=min(_ATTN_UNROLL, trips))
        flush()

    trips = _residues_per_step(dilation) * n_blocks
    stage = pltpu.VMEM((_residues_per_step(dilation), length, _PAIR), BF16)
    bias = pltpu.VMEM((3, 2 * _BQ, _KW), F32)
    return _attn_call(body, name, dilation, seq, 3, [F32, F32], [stage] * n_stage + [bias], 2)(slopes, q, k, v)


def _attn_bwd(q, k, v, do, lse, cterm, slopes, dilation, name):
    seq = q.shape[0]
    length = seq // dilation
    n_blocks = length // _BQ
    n_stage, n_whole = (0, 0) if dilation == 1 else (4, 3)

    def body(sl_ref, q_ref, k_ref, v_ref, do_ref, lse_ref, c_ref, dq_ref, dk_ref, dv_ref, *scratch):
        dk_acc, dv_acc, bias_ref = scratch[n_stage:n_stage + 3]
        whole, tmps = scratch[n_stage + 3:n_stage + 3 + n_whole], scratch[n_stage + 3 + n_whole:]
        (q_in, k_in, v_in, do_in, lse_in, c_in), (dq_out, dk_out, dv_out), rows, flush = _residue_views(
            dilation, seq, (q_ref, k_ref, v_ref, do_ref, lse_ref, c_ref), whole or (dq_ref, dk_ref, dv_ref), tmps)
        qs, ks, vs, dos = _staged(dilation, length, rows, (q_in, k_in, v_in, do_in), scratch[:n_stage])
        dk_acc[...] = jnp.zeros_like(dk_acc)
        dv_acc[...] = jnp.zeros_like(dv_acc)
        _fill_bias(bias_ref, sl_ref, dilation)

        def block(i, carry):
            sub, j = _sub_and_block(i, dilation, n_blocks)
            q0, ws, which = _block_window(j, n_blocks, length)
            qm = _stack_heads(_of_sub(qs, sub)[pl.ds(q0, _BQ), :])
            dom = _stack_heads(_of_sub(dos, sub)[pl.ds(q0, _BQ), :])
            kw = _of_sub(ks, sub)[pl.ds(ws, _KW), :]
            vw = _of_sub(vs, sub)[pl.ds(ws, _KW), :]
            p = jnp.exp(_dot_nt(qm, kw) + bias_ref[which] - _stack_cols(lse_in[rows(q0, _BQ, sub), :]))
            ds = (p * (_dot_nt(dom, vw) + _stack_cols(c_in[rows(q0, _BQ, sub), :]))).astype(BF16)
            dq_out[rows(q0, _BQ, sub), :] = (_unstack_heads(_dot(ds, kw)) * _SCORE_SCALE).astype(dq_out.dtype)
            dk_acc[sub, pl.ds(ws, _KW), :] += _dot_tn(ds, qm)
            dv_acc[sub, pl.ds(ws, _KW), :] += _dot_tn(p.astype(BF16), dom)
            return carry

        lax.fori_loop(0, trips, block, 0, unroll=min(_ATTN_UNROLL, trips))
        for sub in range(per):
            dk_out[rows(0, length, sub), :] = dk_acc[sub].astype(dk_out.dtype)
            dv_out[rows(0, length, sub), :] = dv_acc[sub].astype(dv_out.dtype)
        flush()
        if whole:
            @pl.when(pl.program_id(1) == dilation // per - 1)
            def _():
                for ref, collected in zip((dq_ref, dk_ref, dv_ref), whole):
                    ref[...] = collected[...].astype(BF16)

    per = _residues_per_step(dilation)
    trips = per * n_blocks
    stage = pltpu.VMEM((per, length, _PAIR), BF16)
    acc = pltpu.VMEM((per, length, _PAIR), F32)
    bias = pltpu.VMEM((3, 2 * _BQ, _KW), F32)
    collect = pltpu.VMEM((seq, _PAIR), F32)
    return _attn_call(body, name, dilation, seq, 6, [BF16] * 3, [stage] * n_stage + [acc] * 2 + [bias] + [collect] * n_whole,
                      2 if dilation == 1 else 1)(slopes, q, k, v, do, lse, cterm)


def _group_weights(lses):
    m = jnp.maximum(jnp.maximum(lses[0], lses[1]), lses[2])
    es = [jnp.exp(l - m) for l in lses]
    den = es[0] + es[1] + es[2]
    return [e / den for e in es]


def _out_fwd(a_pool, outs, lses, x, w_out, g, name, tm=1024):
    s, d = x.shape
    width = POOL_DIM + 3 * GROUP_DIM

    def body(ap_ref, o0, o1, o2, l0, l1, l2, x_ref, w_ref, g_ref, xo_ref, cat_ref):
        alphas = _group_weights([l0[...], l1[...], l2[...]])
        cat = jnp.concatenate([ap_ref[...]] + [(o[...] * al).astype(BF16) for o, al in zip((o0, o1, o2), alphas)], axis=1)
        cat_ref[...] = cat
        mix = _dot(cat, w_ref[...])
        xo_ref[...] = x_ref[...] + mix * _inv_rms(mix) * g_ref[...]

    return pl.pallas_call(
        body, name=name, grid=(s // tm,),
        in_specs=[_rows(tm, POOL_DIM)] + [_rows(tm, GROUP_DIM)] * 6 + [_rows(tm, d), _resident(w_out.shape), _const((1, d))],
        out_specs=[_rows(tm, d), _rows(tm, width)],
        out_shape=[jax.ShapeDtypeStruct((s, d), F32), jax.ShapeDtypeStruct((s, width), BF16)],
        compiler_params=_params(dimension_semantics=("arbitrary",)))(a_pool, *outs, *lses, x, w_out, g)


def _out_bwd(dxo, cat, outs, lses, w_out, g, head_ones, name, tm=1024):
    s, d = dxo.shape

    def body(dxo_ref, cat_ref, o0, o1, o2, l0, l1, l2, w_ref, g_ref, ones_ref, dpool_ref, dmix_ref, do0, do1, do2, c0, c1, c2, dg_ref):
        mv = _dot(cat_ref[...], w_ref[...])
        dmix, dg = _rms_bwd(mv, _inv_rms(mv), g_ref[...], dxo_ref[...])
        dmb = dmix.astype(BF16)
        dmix_ref[...] = dmb
        _accumulate(dg_ref, dg)
        dcat = _dot_nt(dmb, w_ref[...])
        dpool_ref[...] = dcat[:, :POOL_DIM]
        alphas = _group_weights([l0[...], l1[...], l2[...]])
        das = [dcat[:, POOL_DIM + GROUP_DIM * j:POOL_DIM + GROUP_DIM * (j + 1)] for j in range(3)]
        prod = sum(da * (o[...] * al) for da, o, al in zip(das, (o0, o1, o2), alphas))
        hi = prod.astype(BF16)
        lo = (prod - hi.astype(F32)).astype(BF16)
        total = _dot(hi, ones_ref[...]) + _dot(lo, ones_ref[...])
        for da, al, do_ref, c_ref in zip(das, alphas, (do0, do1, do2), (c0, c1, c2)):
            do_ref[...] = (da * al).astype(do_ref.dtype)
            c_ref[...] = -al * total

    return pl.pallas_call(
        body, name=name, grid=(s // tm,),
        in_specs=[_rows(tm, d), _rows(tm, cat.shape[1])] + [_rows(tm, GROUP_DIM)] * 6 + [_resident(w_out.shape), _const((1, d)),
                                                                                        _const((GROUP_DIM, GROUP_DIM))],
        out_specs=[_rows(tm, POOL_DIM), _rows(tm, d)] + [_rows(tm, GROUP_DIM)] * 6 + [_const((1, d))],
        out_shape=[jax.ShapeDtypeStruct((s, POOL_DIM), F32), jax.ShapeDtypeStruct((s, d), BF16)]
        + [jax.ShapeDtypeStruct((s, GROUP_DIM), _attn_dtype(dil)) for dil in DILATIONS]
        + [jax.ShapeDtypeStruct((s, GROUP_DIM), F32)] * 3 + [jax.ShapeDtypeStruct((1, d), F32)],
        compiler_params=_params(dimension_semantics=("arbitrary",)))(dxo, cat, *outs, *lses, w_out, g, head_ones)


def _alibi_slopes():
    return np.array([2.0 ** (-8.0 * (i + 1) / N_ATTN_HEADS) for i in range(N_ATTN_HEADS)], np.float32)


def _block_diag(w_lin):
    n, c, _ = w_lin.shape
    eye = jnp.eye(n, dtype=w_lin.dtype)
    return (eye[:, None, :, None] * w_lin[:, :, None, :]).reshape(n * c, n * c)


class _NoExchange:
    def __init__(self, full):
        self.full, self.grads = full, {}

    def first_weights(self):
        return self.full

    def riders(self, host):
        return []

    def landed(self, host, results):
        return self.full

    def gradient(self, name, grad):
        self.grads[name] = grad


def _local_step(x, target, small, exchange):
    s, d = x.shape
    slopes = _alibi_slopes()
    group_slopes = [jnp.asarray(slopes[4 * g:4 * g + 4]) for g in range(3)]
    w_bd = _block_diag(small["w_pool_lin"]).astype(BF16)
    head_ones = jnp.asarray(np.kron(np.eye(GROUP_DIM // HEAD_DIM), np.ones((HEAD_DIM, HEAD_DIM))), BF16)

    full = dict(exchange.first_weights())

    def hosted(call, host, *args):
        results, riding = call(*args, host, exchange.riders(host))
        full.update(exchange.landed(host, riding) or {})
        return results

    x1, a1, b1, f1 = hosted(_ffn_fwd, "ffn1_fwd", x, small["g_ffn1_pre"], full["w1_gate"], full["w1_up"], full["w1_down"],
                            small["g_ffn1_post"], None)
    h2, u, *parts = hosted(_in_fwd, "in_fwd", x1, small["g_mix_pre"], full["w_in"])
    qs, ks, vs = parts[0:3], parts[3:6], parts[6:9]
    a_pool = _pool_fwd(u, w_bd, small["pool_scale"], "pool_fwd")
    outs, lses = [], []
    for g, dil in enumerate(DILATIONS):
        o, lse = _attn_fwd(qs[g], ks[g], vs[g], group_slopes[g], dil, f"attn_fwd{g}")
        outs.append(o)
        lses.append(lse)
    x2, cat = _out_fwd(a_pool, outs, lses, x1, full["w_out"], small["g_mix_post"], "out_fwd")
    (dx3, a2, b2, f2, loss_part), _ = _ffn_fwd(x2, small["g_ffn2_pre"], full["w2_gate"], full["w2_up"], full["w2_down"],
                                               small["g_ffn2_post"], target, "ffn2_fwd")

    small_grads = {}

    def ffn_backward(tag, dxo, x_in, f, a, b):
        n = tag[-1]
        dx, hh, da, db, df, h, dg_pre, dg_post = hosted(
            _ffn_bwd, f"{tag}_bwd", dxo, x_in, f, a, b, small[f"g_{tag}_pre"], small[f"g_{tag}_post"],
            full[f"w{n}_gate"], full[f"w{n}_up"], full[f"w{n}_down"])
        for part, lhs, rhs in (("down", hh, df), ("gate", da, h), ("up", db, h)):
            exchange.gradient(f"w{n}_{part}", hosted(_wgrad, f"{tag}_wgrad_{part}", lhs, rhs))
        small_grads[f"g_{tag}_pre"], small_grads[f"g_{tag}_post"] = dg_pre, dg_post
        return dx

    dx2 = ffn_backward("ffn2", dx3, x2, f2, a2, b2)
    dpool, dmix, *dos_cs, small_grads["g_mix_post"] = _out_bwd(dx2, cat, outs, lses, full["w_out"], small["g_mix_post"],
                                                               head_ones, "out_bwd")
    dos, cs = dos_cs[:3], dos_cs[3:]
    dqs, dks, dvs = [], [], []
    for g, dil in enumerate(DILATIONS):
        dq, dk, dv = _attn_bwd(qs[g], ks[g], vs[g], dos[g], lses[g], cs[g], group_slopes[g], dil, f"attn_bwd{g}")
        dqs.append(dq)
        dks.append(dk)
        dvs.append(dv)
    du, dw_bd, small_grads["pool_scale"] = _pool_bwd(u, dpool, w_bd, small["pool_scale"], "pool_bwd")
    n_pool = len(POOL_HALF_WINDOWS)
    small_grads["w_pool_lin"] = jnp.stack(
        [dw_bd[HEAD_DIM * g:HEAD_DIM * (g + 1), HEAD_DIM * g:HEAD_DIM * (g + 1)] for g in range(n_pool)])
    dz_parts = dqs + dks + dvs
    dx1, small_grads["g_mix_pre"] = hosted(_in_bwd, "in_bwd", du, dz_parts, x1, dx2, small["g_mix_pre"], full["w_in"])
    exchange.gradient("w_in", hosted(_wgrad_parts, "wgrad_in", [du] + dz_parts, h2))
    dx0 = ffn_backward("ffn1", dx1, x, f1, a1, b1)
    exchange.gradient("w_out", hosted(_wgrad, "wgrad_out", cat, dmix))
    return loss_part[0, 0], dx0, small_grads


SEGMENTS = ("w1_gate", "w1_up", "w1_down", "w_in", "w_out", "w2_gate", "w2_up", "w2_down")
TRANSPOSED = ("w1_gate", "w1_up", "w_in", "w2_gate", "w2_up")
ROWS_OUTSIDE = ("w1_gate", "w1_up", "w2_gate", "w2_up")
HALF = 512


def _place():
    x, y, c = lax.axis_index("x"), lax.axis_index("y"), lax.axis_index("c")
    other_chips = [(1 - x, y), (x, 1 - y), (1 - x, 1 - y)]
    return x, y, c, other_chips


def _chip_rows(chip, rows):
    return pl.ds(pl.multiple_of((2 * chip[0] + chip[1]) * rows, 16), rows)


def _cols(c):
    return pl.ds(pl.multiple_of(c * HALF, HALF), HALF)


def _cast_shards(shards, transposed, place, name):
    n = len(shards)
    rows = [w.shape[1] if t else w.shape[0] for w, t in zip(shards, transposed)]

    def body(place_ref, *refs):
        for w_ref, o_ref, t in zip(refs[:n], refs[n:], transposed):
            o_ref[...] = (w_ref[...].T if t else w_ref[...]).astype(BF16)

    once = pl.Buffered(1)
    return pl.pallas_call(
        body, name=name,
        grid_spec=pltpu.PrefetchScalarGridSpec(
            num_scalar_prefetch=1, grid=(1,),
            in_specs=[pl.BlockSpec(w.shape, lambda i, place: (0, 0), pipeline_mode=once) for w in shards],
            out_specs=[pl.BlockSpec((r, 1024), lambda i, place: (place[0], 0), pipeline_mode=once) for r in rows]),
        out_shape=[jax.ShapeDtypeStruct((N_CHIPS * r, 1024), BF16) for r in rows],
        compiler_params=_params(dimension_semantics=("arbitrary",)))(place, *shards)


def _gather_weights(bufs):
    n = len(bufs)
    rows = [b.shape[0] // N_CHIPS for b in bufs]

    def halves(r):
        first = -(-r // 32) * 16
        return (0, first), (first, r - first)

    def body(*refs):
        outs = refs[n:2 * n]
        ici_send, ici_recv, d2d_send, d2d_recv = refs[2 * n:]
        x, y, c, _ = _place()
        me, via_x, via_y, diagonal = (x, y), (1 - x, y), (x, 1 - y), (1 - x, 1 - y)

        def piece(chip, k, h, cols):
            start, size = halves(rows[k])[h]
            return outs[k].at[pl.ds(pl.multiple_of((2 * chip[0] + chip[1]) * rows[k] + start, 16), size), _cols(cols)]

        def ici(path, chip, k, h, to):
            blk = piece(chip, k, h, c)
            return pltpu.make_async_remote_copy(src_ref=blk, dst_ref=blk, send_sem=ici_send.at[path, k, h],
                                                recv_sem=ici_recv.at[path, k, h], device_id=(*to, c), device_id_type=MESH)

        def d2d(slot, chip, k, h, cols):
            blk = piece(chip, k, h, cols)
            return pltpu.make_async_remote_copy(src_ref=blk, dst_ref=blk, send_sem=d2d_send.at[slot, k, h],
                                                recv_sem=d2d_recv.at[slot, k, h], device_id=(x, y, 1 - c), device_id_type=MESH)

        started = [ici(0, me, k, h, via_x) for h in (0, 1) for k in range(n)] + [ici(1, me, k, h, via_y) for h in (1, 0) for k in range(n)]
        for cp in started:
            cp.start()

        def landed(path, slot, chip, k, h, pass_on_to=None):
            ici(path, chip, k, h, me).wait_recv()
            more = [d2d(slot, chip, k, h, c)] + ([ici(2, chip, k, h, pass_on_to)] if pass_on_to else [])
            for cp in more:
                cp.start()
            started.extend(more)

        for k in range(n):
            landed(0, 0, via_x, k, 0, pass_on_to=via_y)
            landed(1, 1, via_y, k, 1, pass_on_to=via_x)
        for k in range(n):
            landed(0, 0, via_x, k, 1)
            landed(1, 1, via_y, k, 0)
        for k in range(n):
            for h in range(2):
                landed(2, 2, diagonal, k, h)
        for slot, chip in enumerate((via_x, via_y, diagonal)):
            for k in range(n):
                for h in range(2):
                    d2d(slot, chip, k, h, 1 - c).wait_recv()
        for cp in started:
            cp.wait_send()

    any_spec = pl.BlockSpec(memory_space=pl.ANY)
    return pl.pallas_call(
        body, name="gather_weights", in_specs=[any_spec] * n, out_specs=[any_spec] * n,
        out_shape=[jax.ShapeDtypeStruct(b.shape, b.dtype) for b in bufs], input_output_aliases={k: k for k in range(n)},
        scratch_shapes=[pltpu.SemaphoreType.DMA((3, n, 2))] * 4)(*bufs)


def _gather_rider(bufs):
    n = len(bufs)
    rows = [b.shape[0] // N_CHIPS for b in bufs]

    def copies(outs, send_sems, recv_sems, inbound):
        x, y, c, chips = _place()
        for j, chip in enumerate(chips):
            for k in range(n):
                src_chip = chip if inbound else (x, y)
                blk = outs[k].at[_chip_rows(src_chip, rows[k]), _cols(c)]
                yield pltpu.make_async_remote_copy(src_ref=blk, dst_ref=blk, send_sem=send_sems.at[j, k], recv_sem=recv_sems.at[j, k],
                                                   device_id=(*chip, c), device_id_type=MESH)

    def start(ins, outs, send_sems, recv_sems):
        for cp in copies(outs, send_sems, recv_sems, False):
            cp.start()

    def wait(ins, outs, send_sems, recv_sems):
        for cp in copies(outs, send_sems, recv_sems, True):
            cp.wait_recv()
        for cp in copies(outs, send_sems, recv_sems, False):
            cp.wait_send()

    return _Rider(list(bufs), None, (3, n), start, wait)


def _forward_rider(bufs):
    n = len(bufs)
    rows = [b.shape[0] // N_CHIPS for b in bufs]

    def copies(outs, send_sems, recv_sems, half):
        x, y, c, chips = _place()
        for j, chip in enumerate(chips):
            for k in range(n):
                blk = outs[k].at[_chip_rows(chip, rows[k]), _cols(half(c))]
                yield pltpu.make_async_remote_copy(src_ref=blk, dst_ref=blk, send_sem=send_sems.at[j, k], recv_sem=recv_sems.at[j, k],
                                                   device_id=(x, y, 1 - c), device_id_type=MESH)

    def start(ins, outs, send_sems, recv_sems):
        for cp in copies(outs, send_sems, recv_sems, lambda c: c):
            cp.start()

    def wait(ins, outs, send_sems, recv_sems):
        for cp in copies(outs, send_sems, recv_sems, lambda c: 1 - c):
            cp.wait_recv()
        for cp in copies(outs, send_sems, recv_sems, lambda c: c):
            cp.wait_send()

    return _Rider(list(bufs), None, (3, n), start, wait)


def _sibling_rider(grads):
    n = len(grads)

    def copies(ins, outs, send_sems, recv_sems):
        x, y, c, _ = _place()
        return [pltpu.make_async_remote_copy(src_ref=ins[k].at[:, pl.ds(1 - c, 1)], dst_ref=outs[k], send_sem=send_sems.at[k],
                                             recv_sem=recv_sems.at[k], device_id=(x, y, 1 - c), device_id_type=MESH)
                for k in range(n)]

    def start(*refs):
        for cp in copies(*refs):
            cp.start()

    def wait(*refs):
        for cp in copies(*refs):
            cp.wait()

    return _Rider(list(grads), [jax.ShapeDtypeStruct((N_CHIPS, 1) + g.shape[2:], F32) for g in grads], (n,), start, wait)


def _alone(rider, name):
    n = len(rider.operands)
    landing = rider.landing if rider.landing is not None else [jax.ShapeDtypeStruct(a.shape, a.dtype) for a in rider.operands]
    n_out = len(landing)

    def body(*refs):
        rider.start(refs[:n], refs[n:n + n_out], *refs[n + n_out:])
        rider.wait(refs[:n], refs[n:n + n_out], *refs[n + n_out:])

    any_spec = pl.BlockSpec(memory_space=pl.ANY)
    return pl.pallas_call(body, name=name, in_specs=[any_spec] * n, out_specs=[any_spec] * n_out, out_shape=landing,
                          input_output_aliases={i: i for i in range(n)} if rider.landing is None else {},
                          scratch_shapes=[pltpu.SemaphoreType.DMA(rider.sems)] * 2)(*rider.operands)


def _chip_sum(grad, from_sibling, place, name):
    rh, width = grad.shape[2:]

    def body(place_ref, g_ref, s_ref, own_ref, all_ref):
        all_ref[...] = (g_ref[...] + s_ref[...]).astype(BF16)
        mine = place_ref[0]
        own_ref[0] = g_ref[mine, 0] + s_ref[mine, 0]

    blk = (N_CHIPS, 1, rh, width)
    once = pl.Buffered(1)
    return pl.pallas_call(
        body, name=name,
        grid_spec=pltpu.PrefetchScalarGridSpec(
            num_scalar_prefetch=1, grid=(1,),
            in_specs=[pl.BlockSpec(blk, lambda i, place: (0, place[1], 0, 0), pipeline_mode=once),
                      pl.BlockSpec(blk, lambda i, place: (0, 0, 0, 0), pipeline_mode=once)],
            out_specs=[pl.BlockSpec((1, rh, width), lambda i, place: (0, 0, 0), pipeline_mode=once),
                       pl.BlockSpec(blk, lambda i, place: (0, 0, 0, 0), pipeline_mode=once)]),
        out_shape=[jax.ShapeDtypeStruct((1, rh, width), F32), jax.ShapeDtypeStruct((N_CHIPS, 1, rh, width), BF16)],
        compiler_params=_params(dimension_semantics=("arbitrary",)))(place, grad, from_sibling)


def _scatter_rider(sums):
    n = len(sums)

    def copies(ins, outs, send_sems, recv_sems):
        x, y, c, chips = _place()
        return [pltpu.make_async_remote_copy(src_ref=ins[k].at[pl.ds(2 * chip[0] + chip[1], 1)], dst_ref=outs[k].at[pl.ds(j, 1)],
                                             send_sem=send_sems.at[j, k], recv_sem=recv_sems.at[j, k],
                                             device_id=(*chip, c), device_id_type=MESH)
                for j, chip in enumerate(chips) for k in range(n)]

    def start(*refs):
        for cp in copies(*refs):
            cp.start()

    def wait(*refs):
        for cp in copies(*refs):
            cp.wait()

    return _Rider(list(sums), [jax.ShapeDtypeStruct((3,) + sm.shape[1:], BF16) for sm in sums], (3, n), start, wait)


def _total_sums(owns, received, name):
    n = len(owns)

    def body(*refs):
        for o_ref, r_ref, t_ref in zip(refs[:n], refs[n:2 * n], refs[2 * n:]):
            total = o_ref[0]
            for j in range(3):
                total = total + r_ref[j, 0].astype(F32)
            t_ref[0] = total

    return _hosted_call(body, None, name=name, steps=1, in_specs=[_resident(a.shape) for a in owns + received],
                        out_specs=[_resident(o.shape) for o in owns], out_shape=[jax.ShapeDtypeStruct(o.shape, F32) for o in owns],
                        args=owns + received)[0]


def _swap_rider(halves):
    n = len(halves)

    def copies(ins, outs, send_sems, recv_sems):
        x, y, c, _ = _place()
        return [pltpu.make_async_remote_copy(src_ref=ins[k], dst_ref=outs[k], send_sem=send_sems.at[k], recv_sem=recv_sems.at[k],
                                             device_id=(x, y, 1 - c), device_id_type=MESH) for k in range(n)]

    def start(*refs):
        for cp in copies(*refs):
            cp.start()

    def wait(*refs):
        for cp in copies(*refs):
            cp.wait()

    return _Rider(list(halves), [jax.ShapeDtypeStruct(h.shape, F32) for h in halves], (n,), start, wait)


N_DEV = 8


def _gather_small(block):
    m_per, width = block.shape

    def body(x_ref, out_ref, send_sems, recv_sems, local_sem):
        x, y, c, chips = _place()
        me, sibling = (x, y, c), (x, y, 1 - c)

        def rows(px, py, pc):
            return out_ref.at[pl.ds((4 * px + 2 * py + pc) * m_per, m_per), :]

        def copy(k, blk, to, src=None):
            return pltpu.make_async_remote_copy(src_ref=rows(*blk) if src is None else src, dst_ref=rows(*blk),
                                                send_sem=send_sems.at[k], recv_sem=recv_sems.at[k], device_id=to, device_id_type=MESH)

        mine = pltpu.make_async_copy(x_ref, rows(*me), local_sem)
        mine.start()
        first = [copy(0, me, sibling, src=x_ref)] + [copy(1 + j, me, (*chip, c), src=x_ref) for j, chip in enumerate(chips)]
        for cp in first:
            cp.start()
        passed = [copy(4 + j, (*chip, c), sibling) for j, chip in enumerate(chips)]
        for j, chip in enumerate(chips):
            copy(1 + j, (*chip, c), me).wait_recv()
            passed[j].start()
        copy(0, sibling, me).wait_recv()
        for j, chip in enumerate(chips):
            copy(4 + j, (*chip, 1 - c), me).wait_recv()
        for cp in first + passed:
            cp.wait_send()
        mine.wait()

    vmem = pl.BlockSpec(memory_space=pltpu.VMEM)
    return pl.pallas_call(body, name="gather_small", out_shape=jax.ShapeDtypeStruct((N_DEV * m_per, width), F32),
                          in_specs=[vmem], out_specs=vmem,
                          scratch_shapes=[pltpu.SemaphoreType.DMA((7,)), pltpu.SemaphoreType.DMA((7,)),
                                          pltpu.SemaphoreType.DMA])(block)


def _adamw_math(w, g, m, v):
    m = ADAM_B1 * m + (1.0 - ADAM_B1) * g
    v = ADAM_B2 * v + (1.0 - ADAM_B2) * (g * g)
    m_hat = m / (1.0 - ADAM_B1 ** ADAM_STEP)
    v_hat = v / (1.0 - ADAM_B2 ** ADAM_STEP)
    delta = -ADAM_LR * (m_hat / (jnp.sqrt(v_hat) + ADAM_EPS) + ADAM_WD * w)
    return delta, m, v


def _adamw(w, mine, siblings, place, m, v, transposed, name):
    rh, width = mine.shape[1:]
    place_spec = pl.BlockSpec(memory_space=pltpu.SMEM)
    halves = [_const((1, rh, width))] * 2
    out_shape = [jax.ShapeDtypeStruct(w.shape, F32)] * 4
    if transposed:
        def body(place_ref, w_ref, mine_ref, sib_ref, m_ref, v_ref, go_ref, d_ref, mo_ref, vo_ref):
            first = place_ref[1] == 0
            g = jnp.concatenate([jnp.where(first, mine_ref[0], sib_ref[0]), jnp.where(first, sib_ref[0], mine_ref[0])], axis=0).T
            go_ref[...] = g
            d_ref[...], mo_ref[...], vo_ref[...] = _adamw_math(w_ref[...], g, m_ref[...], v_ref[...])

        whole = _resident(w.shape)
        return _hosted_call(body, None, name=name, steps=1, in_specs=[place_spec, whole] + halves + [whole, whole],
                            out_specs=[whole] * 4, out_shape=out_shape, args=[place, w, mine, siblings, m, v])[0]

    def body(place_ref, w_ref, mine_ref, sib_ref, m_ref, v_ref, go_ref, d_ref, mo_ref, vo_ref):
        g = jnp.where(pl.program_id(0) == place_ref[1], mine_ref[0], sib_ref[0])
        go_ref[...] = g
        d_ref[...], mo_ref[...], vo_ref[...] = _adamw_math(w_ref[...], g, m_ref[...], v_ref[...])

    half = _rows(rh, width)
    return _hosted_call(body, None, name=name, steps=2, in_specs=[place_spec, half] + halves + [half, half],
                        out_specs=[half] * 4, out_shape=out_shape, args=[place, w, mine, siblings, m, v])[0]


def _adamw_small(gathered, w, m, v, name):
    def body(ga_ref, w_ref, m_ref, v_ref, go_ref, d_ref, mo_ref, vo_ref):
        g = ga_ref[0]
        for dev in range(1, N_DEV):
            g = g + ga_ref[dev]
        go_ref[...] = g
        d_ref[...], mo_ref[...], vo_ref[...] = _adamw_math(w_ref[...], g, m_ref[...], v_ref[...])

    return pl.pallas_call(body, name=name, out_shape=[jax.ShapeDtypeStruct(w.shape, F32)] * 4,
                          compiler_params=_params())(gathered, w, m, v)


class _Exchange:
    FIRST = ("w1_gate", "w1_up", "w1_down")
    HOSTS = {"ffn2_wgrad_gate": (("w2_down",), ()), "ffn2_wgrad_up": (("w2_gate",), ("w2_down",)),
             "in_bwd": (("w2_up",), ("w2_gate",)), "wgrad_in": ((), ("w2_up",)),
             "ffn1_wgrad_down": ((), ("w_in",)), "ffn1_wgrad_gate": (("w1_down",), ()), "ffn1_wgrad_up": ((), ("w1_down", "w1_gate")),
             "wgrad_out": ((), ("w1_up",))}
    ALONE = ("w_in", "w1_gate", "w1_up", "w_out")
    SWAP_HOST = "wgrad_out"

    def __init__(self, bufs, place):
        self.bufs, self.place = bufs, place
        self.later = [k for k in SEGMENTS if k not in self.FIRST]
        self.split, self.own, self.to_send, self.received = {}, {}, {}, {}

    def first_weights(self):
        return dict(zip(self.FIRST, _gather_weights([self.bufs[k] for k in self.FIRST])))

    def riders(self, host):
        if host == "ffn1_fwd":
            return [_gather_rider([self.bufs[k] for k in self.later])]
        if host == "in_fwd":
            return [_forward_rider([self.bufs[k] for k in self.later[1:]])]
        halves, sums = self.HOSTS.get(host, ((), ()))
        riders = ([_sibling_rider([self.split[k] for k in halves])] if halves else []) + (
            [_scatter_rider([self.to_send[k] for k in sums])] if sums else [])
        if host == self.SWAP_HOST:
            self.early = [k for k in SEGMENTS if k in self.received]
            self.mine = dict(zip(self.early, self._totals(self.early, "total_sums_early")))
            riders.append(_swap_rider([self.mine[k] for k in self.early]))
        return riders

    def landed(self, host, results):
        if host == "ffn1_fwd":
            self.bufs.update(zip(self.later, results[0]))
            return dict(zip(self.later[:1], _alone(_forward_rider([self.bufs[self.later[0]]]), "gather_forward_first")))
        if host == "in_fwd":
            return dict(zip(self.later[1:], results[0]))
        if host == self.SWAP_HOST:
            results = list(results)
            self.siblings = dict(zip(self.early, results.pop()))
        halves, sums = self.HOSTS.get(host, ((), ()))
        if halves:
            self._chip_sums(halves, results[0])
        if sums:
            self.received.update(zip(sums, results[-1]))

    def gradient(self, name, grad):
        self.split[name] = grad.reshape(N_CHIPS, 2, grad.shape[0] // (2 * N_CHIPS), grad.shape[1])
        if name in self.ALONE:
            self._chip_sums([name], _alone(_sibling_rider([self.split[name]]), f"reduce_sibling_{name}"))

    def _chip_sums(self, names, from_sibling):
        for k, fs in zip(names, from_sibling):
            self.own[k], self.to_send[k] = _chip_sum(self.split[k], fs, self.place, f"chip_sum_{k}")

    def _totals(self, names, call_name):
        return _total_sums([self.own[k] for k in names], [self.received[k] for k in names], call_name)

    def summed_halves(self):
        late = [k for k in SEGMENTS if k not in self.received]
        self.received.update(zip(late, _alone(_scatter_rider([self.to_send[k] for k in late]), "reduce_chips_last")))
        rest = [k for k in SEGMENTS if k not in self.early]
        mine = self._totals(rest, "total_sums")
        self.mine.update(zip(rest, mine))
        self.siblings.update(zip(rest, _alone(_swap_rider(mine), "swap_halves")))
        return [self.mine[k] for k in SEGMENTS], [self.siblings[k] for k in SEGMENTS]


SMALL = ("g_ffn1_pre", "g_ffn1_post", "g_mix_pre", "w_pool_lin", "pool_scale", "g_mix_post", "g_ffn2_pre", "g_ffn2_post")
WEIGHTS = ("g_ffn1_pre", "w1_gate", "w1_up", "w1_down", "g_ffn1_post", "g_mix_pre", "w_in", "w_pool_lin", "pool_scale", "w_out",
           "g_mix_post", "g_ffn2_pre", "w2_gate", "w2_up", "w2_down", "g_ffn2_post")
LANES = 128


def _pack_small(tree, extra=0.0):
    flat = jnp.concatenate([tree[k].reshape(-1) for k in SMALL] + [jnp.reshape(extra, (1,)).astype(F32)])
    rows = -(-flat.shape[0] // (8 * LANES)) * 8
    return jnp.pad(flat, (0, rows * LANES - flat.shape[0])).reshape(rows, LANES)


def _unpack_small(packed, like):
    flat, out, at = packed.reshape(-1), {}, 0
    for k in SMALL:
        size = math.prod(like[k].shape)
        out[k] = flat[at:at + size].reshape(like[k].shape)
        at += size
    return out


def kernel(x, g_ffn1_pre, w1_gate, w1_up, w1_down, g_ffn1_post, g_mix_pre, w_in, w_pool_lin, pool_scale, w_out, g_mix_post, g_ffn2_pre, w2_gate, w2_up, w2_down, g_ffn2_post, loss_target, m_g_ffn1_pre, m_w1_gate, m_w1_up, m_w1_down, m_g_ffn1_post, m_g_mix_pre, m_w_in, m_w_pool_lin, m_pool_scale, m_w_out, m_g_mix_post, m_g_ffn2_pre, m_w2_gate, m_w2_up, m_w2_down, m_g_ffn2_post, v_g_ffn1_pre, v_w1_gate, v_w1_up, v_w1_down, v_g_ffn1_post, v_g_mix_pre, v_w_in, v_w_pool_lin, v_pool_scale, v_w_out, v_g_mix_post, v_g_ffn2_pre, v_w2_gate, v_w2_up, v_w2_down, v_g_ffn2_post):
    given = dict(locals())
    w = {k: given[k] for k in WEIGHTS}
    m = {k: given["m_" + k] for k in WEIGHTS}
    v = {k: given["v_" + k] for k in WEIGHTS}
    small = {k: (w[k][0] if k == "w_pool_lin" else w[k].reshape(1, -1)) for k in SMALL}

    place = jnp.stack([2 * lax.axis_index("x") + lax.axis_index("y"), lax.axis_index("c")]).astype(jnp.int32)
    def as_rows(a, k):
        return jnp.swapaxes(a, 1, 2)[0] if k in ROWS_OUTSIDE else a[0]

    def as_given(a, k):
        return jnp.swapaxes(a[None], 1, 2) if k in ROWS_OUTSIDE else a[None]

    in_kernel = [k for k in TRANSPOSED if k not in ROWS_OUTSIDE]
    bufs = {}
    for tag, names in (("first", _Exchange.FIRST), ("rest", [k for k in SEGMENTS if k not in _Exchange.FIRST])):
        bufs.update(zip(names, _cast_shards([as_rows(w[k], k) for k in names], [k in in_kernel for k in names], place, f"cast_{tag}")))
    exchange = _Exchange(bufs, place)
    loss_part, grad_x, small_grads = _local_step(x[0], loss_target[0], small, exchange)

    out_grad, out_delta, out_m, out_v = {}, {}, {}, {}
    for k, mine, siblings in zip(SEGMENTS, *exchange.summed_halves()):
        results = _adamw(as_rows(w[k], k), mine, siblings, place, as_rows(m[k], k), as_rows(v[k], k), k in in_kernel, f"adamw_{k}")
        out_grad[k], out_delta[k], out_m[k], out_v[k] = (as_given(a, k) for a in results)

    small_grads["w_pool_lin"] = small_grads["w_pool_lin"][None]
    packed = _pack_small(small_grads, loss_part)
    gathered = _gather_small(packed).reshape(N_DEV, *packed.shape)
    like = {k: w[k] for k in SMALL}
    results = _adamw_small(gathered, _pack_small(like), _pack_small({k: m[k] for k in SMALL}),
                           _pack_small({k: v[k] for k in SMALL}), "adamw_small")
    for tree, res in zip((out_grad, out_delta, out_m, out_v), results):
        tree.update(_unpack_small(res, like))
    loss = results[0].reshape(-1)[sum(math.prod(like[k].shape) for k in SMALL)]

    return (loss, grad_x[None], *[out_grad[k] for k in WEIGHTS], *[out_delta[k] for k in WEIGHTS],
            *[out_m[k] for k in WEIGHTS], *[out_v[k] for k in WEIGHTS])
```

```python
import math
import typing

import numpy as np
import jax
import jax.numpy as jnp
from jax import lax
from jax.experimental import pallas as pl
from jax.experimental.pallas import tpu as pltpu

F32 = jnp.float32
BF16 = jnp.bfloat16
MESH = pl.DeviceIdType.MESH

RMS_EPS = 1e-6
HEAD_DIM = 64
POOL_HALF_WINDOWS = (1, 2, 4, 8)
POOL_DIM = 256
GROUP_DIM = 256
DILATIONS = (1, 4, 16)
N_SIDE = 64
N_ATTN_HEADS = 12
ADAM_LR, ADAM_B1, ADAM_B2, ADAM_EPS, ADAM_WD, ADAM_STEP = 0.001, 0.9, 0.999, 1e-08, 0.01, 10

N_CHIPS = 4
V7X_VMEM_LIMIT = 60 * 1024 * 1024

_NT = (((1,), (1,)), ((), ()))
_TN = (((0,), (0,)), ((), ()))


def _dot(a, b):
    return jnp.dot(a, b, preferred_element_type=F32)


def _dot_nt(a, b):
    return lax.dot_general(a, b, _NT, preferred_element_type=F32)


def _dot_tn(a, b):
    return lax.dot_general(a, b, _TN, preferred_element_type=F32)


def _params(**kw):
    return pltpu.CompilerParams(vmem_limit_bytes=V7X_VMEM_LIMIT, **kw)


def _rows(tm, width):
    return pl.BlockSpec((tm, width), lambda i: (i, 0))


def _resident(shape):
    return pl.BlockSpec(shape, lambda i: (0,) * len(shape), pipeline_mode=pl.Buffered(1))


def _const(shape):
    return pl.BlockSpec(shape, lambda i: (0,) * len(shape))


def _inv_rms(x):
    return lax.rsqrt(jnp.mean(x * x, axis=-1, keepdims=True) + RMS_EPS)


def _rms_bwd(x, inv, g, dy):
    n = x * inv
    dn = dy * g
    dx = inv * (dn - n * jnp.mean(dn * n, axis=-1, keepdims=True))
    return dx, jnp.sum(dy * n, axis=0, keepdims=True)


def _accumulate(ref, value):
    @pl.when(pl.program_id(0) == 0)
    def _():
        ref[...] = jnp.zeros_like(ref)

    ref[...] += value


class _Rider(typing.NamedTuple):
    operands: list
    landing: typing.Optional[list]
    sems: tuple
    start: typing.Callable
    wait: typing.Callable


def _hosted_call(body, riders, *, name, steps, in_specs, out_specs, out_shape, args, scratch_shapes=()):
    params = _params(dimension_semantics=("arbitrary",))
    riders = list(riders or [])
    if not riders:
        res = pl.pallas_call(body, name=name, grid=(steps,), in_specs=in_specs, out_specs=out_specs, out_shape=out_shape,
                             scratch_shapes=list(scratch_shapes), compiler_params=params)(*args)
        return list(res), []
    n_in, n_out, n_scratch = len(in_specs), len(out_specs), len(scratch_shapes)
    operands, landing, aliases, spans = [], [], {}, []
    for rd in riders:
        lands = rd.landing if rd.landing is not None else [jax.ShapeDtypeStruct(a.shape, a.dtype) for a in rd.operands]
        if rd.landing is None:
            aliases.update({n_in + len(operands) + i: n_out + len(landing) + i for i in range(len(lands))})
        spans.append((len(operands), len(rd.operands), len(landing), len(lands)))
        operands += rd.operands
        landing += lands
    outs_at = n_in + len(operands)
    scratch_at = outs_at + n_out + len(landing)

    def riding(*refs):
        def each(action):
            for i, (rd, (in_at, n_ops, out_at, n_lands)) in enumerate(zip(riders, spans)):
                sems = refs[scratch_at + n_scratch + 2 * i:scratch_at + n_scratch + 2 * i + 2]
                getattr(rd, action)(refs[n_in + in_at:n_in + in_at + n_ops],
                                    refs[outs_at + n_out + out_at:outs_at + n_out + out_at + n_lands], *sems)

        @pl.when(pl.program_id(0) == 0)
        def _():
            each("start")

        body(*refs[:n_in], *refs[outs_at:outs_at + n_out], *refs[scratch_at:scratch_at + n_scratch])

        @pl.when(pl.program_id(0) == steps - 1)
        def _():
            each("wait")

    any_spec = pl.BlockSpec(memory_space=pl.ANY)
    res = pl.pallas_call(
        riding, name=name, grid=(steps,), in_specs=list(in_specs) + [any_spec] * len(operands),
        out_specs=list(out_specs) + [any_spec] * len(landing), out_shape=list(out_shape) + landing,
        scratch_shapes=list(scratch_shapes) + [pltpu.SemaphoreType.DMA(rd.sems) for rd in riders for _ in range(2)],
        input_output_aliases=aliases, compiler_params=params)(*args, *operands)
    return list(res[:n_out]), [list(res[n_out + out_at:n_out + out_at + n_lands]) for _, _, out_at, n_lands in spans]


_SUB_TILE = 256


def _sub_tiles(tm):
    return [pl.ds(r, _SUB_TILE) for r in range(0, tm, _SUB_TILE)]


def _ffn_fwd(x, g_pre, wg_t, wu_t, wd, g_post, target, name, riders=None, tm=512):
    s, d = x.shape
    ff = wd.shape[0]
    with_loss = target is not None

    def body(*refs):
        if with_loss:
            x_ref, gpre_ref, wg_ref, wu_ref, wd_ref, gpost_ref, t_ref, xo_ref, a_ref, b_ref, f_ref, loss_ref = refs
        else:
            x_ref, gpre_ref, wg_ref, wu_ref, wd_ref, gpost_ref, xo_ref, a_ref, b_ref, f_ref = refs
        loss = 0.0
        for rows in _sub_tiles(tm):
            xv = x_ref[rows, :]
            hb = (xv * _inv_rms(xv) * gpre_ref[...]).astype(BF16)
            a = _dot_nt(hb, wg_ref[...])
            b = _dot_nt(hb, wu_ref[...])
            hh = (a * jax.nn.sigmoid(a)) * b
            f = _dot(hh.astype(BF16), wd_ref[...])
            xo = xv + 0.5 * (f * _inv_rms(f) * gpost_ref[...])
            a_ref[rows, :] = a.astype(BF16)
            b_ref[rows, :] = b.astype(BF16)
            f_ref[rows, :] = f
            if with_loss:
                e = xo - t_ref[rows, :]
                xo_ref[rows, :] = e * (1.0 / d)
                loss = loss + 0.5 * jnp.sum(jnp.mean(e * e, axis=-1, keepdims=True))
            else:
                xo_ref[rows, :] = xo
        if with_loss:
            _accumulate(loss_ref, loss)

    in_specs = [_rows(tm, d), _const((1, d)), _resident((ff, d)), _resident((ff, d)), _resident((ff, d)), _const((1, d))]
    args = [x, g_pre, wg_t, wu_t, wd, g_post]
    out_shape = [jax.ShapeDtypeStruct((s, d), F32), jax.ShapeDtypeStruct((s, ff), BF16),
                 jax.ShapeDtypeStruct((s, ff), BF16), jax.ShapeDtypeStruct((s, d), F32)]
    out_specs = [_rows(tm, d), _rows(tm, ff), _rows(tm, ff), _rows(tm, d)]
    if with_loss:
        in_specs.append(_rows(tm, d))
        args.append(target)
        out_shape.append(jax.ShapeDtypeStruct((8, 128), F32))
        out_specs.append(_const((8, 128)))
    return _hosted_call(body, riders, name=name, steps=s // tm, in_specs=in_specs, out_specs=out_specs, out_shape=out_shape, args=args)


def _ffn_bwd(dxo, x, f, a, b, g_pre, g_post, wg_t, wu_t, wd, name, riders=None, tm=256):
    s, d = x.shape
    ff = wd.shape[0]

    def body(dxo_ref, x_ref, f_ref, a_ref, b_ref, gpre_ref, gpost_ref, wg_ref, wu_ref, wd_ref,
             dx_ref, hh_ref, da_ref, db_ref, df_ref, h_ref, dgpre_ref, dgpost_ref):
        dgpre_sum = dgpost_sum = 0.0
        for rows in _sub_tiles(tm):
            dxo_v = dxo_ref[rows, :]
            fv = f_ref[rows, :]
            df, dgpost = _rms_bwd(fv, _inv_rms(fv), gpost_ref[...], 0.5 * dxo_v)
            dfb = df.astype(BF16)
            dhh = _dot_nt(dfb, wd_ref[...])
            av = a_ref[rows, :].astype(F32)
            bv = b_ref[rows, :].astype(F32)
            sig = jax.nn.sigmoid(av)
            sa = av * sig
            da = (dhh * bv * (sig * (1.0 + av * (1.0 - sig)))).astype(BF16)
            db = (dhh * sa).astype(BF16)
            dh = _dot(da, wg_ref[...]) + _dot(db, wu_ref[...])
            xv = x_ref[rows, :]
            inv = _inv_rms(xv)
            dxn, dgpre = _rms_bwd(xv, inv, gpre_ref[...], dh)
            dx_ref[rows, :] = dxo_v + dxn
            hh_ref[rows, :] = (sa * bv).astype(BF16)
            da_ref[rows, :] = da
            db_ref[rows, :] = db
            df_ref[rows, :] = dfb
            h_ref[rows, :] = (xv * inv * gpre_ref[...]).astype(BF16)
            dgpre_sum, dgpost_sum = dgpre_sum + dgpre, dgpost_sum + dgpost
        _accumulate(dgpre_ref, dgpre_sum)
        _accumulate(dgpost_ref, dgpost_sum)

    return _hosted_call(
        body, riders, name=name, steps=s // tm,
        in_specs=[_rows(tm, d), _rows(tm, d), _rows(tm, d), _rows(tm, ff), _rows(tm, ff), _const((1, d)), _const((1, d)),
                  _resident((ff, d)), _resident((ff, d)), _resident((ff, d))],
        out_specs=[_rows(tm, d), _rows(tm, ff), _rows(tm, ff), _rows(tm, ff), _rows(tm, d), _rows(tm, d),
                   _const((1, d)), _const((1, d))],
        out_shape=[jax.ShapeDtypeStruct((s, d), F32), jax.ShapeDtypeStruct((s, ff), BF16), jax.ShapeDtypeStruct((s, ff), BF16),
                   jax.ShapeDtypeStruct((s, ff), BF16), jax.ShapeDtypeStruct((s, d), BF16), jax.ShapeDtypeStruct((s, d), BF16),
                   jax.ShapeDtypeStruct((1, d), F32), jax.ShapeDtypeStruct((1, d), F32)],
        args=[dxo, x, f, a, b, g_pre, g_post, wg_t, wu_t, wd])


def _wgrad(lhs, rhs, name, riders=None, rt=256):
    s, r = lhs.shape
    c = rhs.shape[1]

    def body(l_ref, r_ref, o_ref):
        o_ref[...] = _dot_tn(l_ref[...], r_ref[...])

    (out,), riding = _hosted_call(
        body, riders, name=name, steps=pl.cdiv(r, rt), in_specs=[pl.BlockSpec((s, rt), lambda i: (0, i)), _resident((s, c))],
        out_specs=[pl.BlockSpec((rt, c), lambda i: (i, 0))], out_shape=[jax.ShapeDtypeStruct((r, c), F32)], args=[lhs, rhs])
    return out, riding


def _attn_dtype(dilation):
    return BF16 if dilation == 1 else F32


def _in_fwd(x, g, w_in_t, name, riders=None, tm=1024):
    s, d = x.shape
    d_in = w_in_t.shape[0]
    n_groups = len(DILATIONS)
    dtypes = [_attn_dtype(dil) for dil in DILATIONS] * 3

    def body(x_ref, g_ref, w_ref, h_ref, u_ref, *part_refs):
        xv = x_ref[...]
        hb = (xv * _inv_rms(xv) * g_ref[...]).astype(BF16)
        h_ref[...] = hb
        z = _dot_nt(hb, w_ref[...])
        u_ref[...] = z[:, :POOL_DIM]
        for j, ref in enumerate(part_refs):
            part = z[:, POOL_DIM + GROUP_DIM * j:POOL_DIM + GROUP_DIM * (j + 1)]
            ref[...] = (part * _SCORE_SCALE if j < n_groups else part).astype(ref.dtype)

    return _hosted_call(
        body, riders, name=name, steps=s // tm, in_specs=[_rows(tm, d), _const((1, d)), _resident((d_in, d))],
        out_specs=[_rows(tm, d), _rows(tm, POOL_DIM)] + [_rows(tm, GROUP_DIM)] * len(dtypes),
        out_shape=[jax.ShapeDtypeStruct((s, d), BF16), jax.ShapeDtypeStruct((s, POOL_DIM), F32)]
        + [jax.ShapeDtypeStruct((s, GROUP_DIM), dt) for dt in dtypes],
        args=[x, g, w_in_t])


def _in_bwd(du, dparts, x, dxo, g, w_in_t, name, riders=None, tm=512):
    s, d = x.shape
    d_in = w_in_t.shape[0]
    n_parts = len(dparts)

    def body(du_ref, *refs):
        part_refs = refs[:n_parts]
        x_ref, dxo_ref, g_ref, w_ref, dx_ref, dg_ref = refs[n_parts:]
        dh = _dot(jnp.concatenate([r[...] for r in (du_ref,) + part_refs], axis=1), w_ref[...])
        xv = x_ref[...]
        dxn, dg = _rms_bwd(xv, _inv_rms(xv), g_ref[...], dh)
        dx_ref[...] = dxo_ref[...] + dxn
        _accumulate(dg_ref, dg)

    return _hosted_call(
        body, riders, name=name, steps=s // tm,
        in_specs=[_rows(tm, POOL_DIM)] + [_rows(tm, GROUP_DIM)] * n_parts + [_rows(tm, d), _rows(tm, d), _const((1, d)),
                                                                             _resident((d_in, d))],
        out_specs=[_rows(tm, d), _const((1, d))],
        out_shape=[jax.ShapeDtypeStruct((s, d), F32), jax.ShapeDtypeStruct((1, d), F32)],
        args=[du, *dparts, x, dxo, g, w_in_t])


def _wgrad_parts(parts, rhs, name, riders=None):
    n = len(parts)
    s, rt = parts[0].shape
    c = rhs.shape[1]

    def body(*refs):
        part_refs, r_ref, o_ref, buf, sems = refs[:n], refs[n], refs[n + 1], refs[n + 2], refs[n + 3]

        def fetch(i):
            return pltpu.make_async_copy(part_refs[i], buf.at[i % 2], sems.at[i % 2])

        fetch(0).start()
        for i in range(n):
            if i + 1 < n:
                fetch(i + 1).start()
            fetch(i).wait()
            o_ref[pl.ds(i * rt, rt), :] = _dot_tn(buf[i % 2], r_ref[...])

    (out,), riding = _hosted_call(
        body, riders, name=name, steps=1, in_specs=[pl.BlockSpec(memory_space=pl.ANY)] * n + [_resident((s, c))],
        out_specs=[_resident((n * rt, c))], out_shape=[jax.ShapeDtypeStruct((n * rt, c), F32)], args=[*parts, rhs],
        scratch_shapes=[pltpu.VMEM((2, s, rt), BF16), pltpu.SemaphoreType.DMA((2,))])
    return out, riding


_POOL_HALO = 8


def _pool_chain(v, first_shift):
    n = v.shape[0]
    p2 = v + pltpu.roll(v, first_shift, 0)
    p4 = pltpu.roll(p2, 1, 0) + pltpu.roll(p2, n - 1, 0)
    p8 = pltpu.roll(p4, 2, 0) + pltpu.roll(p4, n - 2, 0)
    p16 = pltpu.roll(p8, 4, 0) + pltpu.roll(p8, n - 4, 0)
    group = lax.broadcasted_iota(jnp.int32, v.shape, 1) // HEAD_DIM
    return jnp.where(group == 0, p2, jnp.where(group == 1, p4, jnp.where(group == 2, p8, p16)))


def _pool_count(t0, rows, s):
    t = t0 + lax.broadcasted_iota(jnp.int32, (rows, POOL_DIM), 0)
    group = lax.broadcasted_iota(jnp.int32, (rows, POOL_DIM), 1) // HEAD_DIM
    half = jnp.where(group == 0, 1, jnp.where(group == 1, 2, jnp.where(group == 2, 4, 8)))
    cnt = jnp.minimum(t + half, s) - jnp.maximum(t - half, 0)
    return jnp.maximum(cnt, 1).astype(F32)


def _pad_rows(ref, pad_ref, s):
    zeros = jnp.zeros((_POOL_HALO, pad_ref.shape[1]), pad_ref.dtype)
    pad_ref[pl.ds(0, _POOL_HALO), :] = zeros
    pad_ref[pl.ds(_POOL_HALO + s, _POOL_HALO), :] = zeros
    pad_ref[pl.ds(_POOL_HALO, s), :] = ref[...]


def _pool_fwd(u, w_bd, scale, name, tm=512):
    s = u.shape[0]
    ext = tm + 2 * _POOL_HALO

    def body(u_ref, w_ref, sc_ref, o_ref, upad):
        _pad_rows(u_ref, upad, s)

        def tile(i, carry):
            t0 = pl.multiple_of(i * tm, tm)
            uv = upad[pl.ds(t0, ext), :]
            win = _pool_chain(uv, 1)[_POOL_HALO:_POOL_HALO + tm]
            y = win / _pool_count(t0, tm, s) - uv[_POOL_HALO:_POOL_HALO + tm]
            o_ref[pl.ds(t0, tm), :] = (_dot(y.astype(BF16), w_ref[...]) * sc_ref[...]).astype(BF16)
            return carry

        lax.fori_loop(0, s // tm, tile, 0)

    return pl.pallas_call(body, name=name, out_shape=jax.ShapeDtypeStruct((s, POOL_DIM), BF16),
                          scratch_shapes=[pltpu.VMEM((s + 2 * _POOL_HALO, POOL_DIM), F32)],
                          compiler_params=_params())(u, w_bd, scale)


def _pool_bwd(u, da, w_bd, scale, name, tm=512):
    s = u.shape[0]
    ext = tm + 2 * _POOL_HALO

    def body(u_ref, da_ref, w_ref, sc_ref, du_ref, dw_ref, dsc_ref, upad, dapad):
        _pad_rows(u_ref, upad, s)
        _pad_rows(da_ref, dapad, s)
        dw_ref[...] = jnp.zeros_like(dw_ref)
        dsc_ref[...] = jnp.zeros_like(dsc_ref)

        def tile(i, carry):
            t0 = pl.multiple_of(i * tm, tm)
            uv = upad[pl.ds(t0, ext), :]
            dav = dapad[pl.ds(t0, ext), :]
            win = _pool_chain(uv, 1)[_POOL_HALO:_POOL_HALO + tm]
            yb = (win / _pool_count(t0, tm, s) - uv[_POOL_HALO:_POOL_HALO + tm]).astype(BF16)
            yl = _dot(yb, w_ref[...])
            da_c = dav[_POOL_HALO:_POOL_HALO + tm]
            dsc_ref[...] += jnp.sum(da_c * yl, axis=0, keepdims=True)
            dyl = (dav * sc_ref[...]).astype(BF16)
            dw_ref[...] += _dot_tn(yb, dyl[_POOL_HALO:_POOL_HALO + tm])
            dy = _dot_nt(dyl, w_ref[...])
            dyc = dy / _pool_count(t0 - _POOL_HALO, ext, s)
            du_ref[pl.ds(t0, tm), :] = (_pool_chain(dyc, ext - 1) - dy)[_POOL_HALO:_POOL_HALO + tm].astype(BF16)
            return carry

        lax.fori_loop(0, s // tm, tile, 0)

    pool_cols = pl.BlockSpec((s, POOL_DIM), lambda i: (0, 0), pipeline_mode=pl.Buffered(1))
    return pl.pallas_call(
        body, name=name, grid=(1,),
        in_specs=[pool_cols, pool_cols, _const((POOL_DIM, POOL_DIM)), _const((1, POOL_DIM))],
        out_specs=[_const((s, POOL_DIM)), _const((POOL_DIM, POOL_DIM)), _const((1, POOL_DIM))],
        out_shape=[jax.ShapeDtypeStruct((s, POOL_DIM), BF16), jax.ShapeDtypeStruct((POOL_DIM, POOL_DIM), F32),
                   jax.ShapeDtypeStruct((1, POOL_DIM), F32)],
        scratch_shapes=[pltpu.VMEM((s + 2 * _POOL_HALO, POOL_DIM), F32), pltpu.VMEM((s + 2 * _POOL_HALO, POOL_DIM), F32)],
        compiler_params=_params(dimension_semantics=("arbitrary",)))(u, da, w_bd, scale)


_BQ = 128
_KW = _BQ + 2 * N_SIDE
_PAIR = 2 * HEAD_DIM
_NEG = -1e30
_ATTN_UNROLL = 8
_SCORE_SCALE = HEAD_DIM ** -0.5


def _stack_heads(x):
    lane_head = lax.broadcasted_iota(jnp.int32, x.shape, 1) // HEAD_DIM
    zero = jnp.zeros_like(x)
    return jnp.concatenate([jnp.where(lane_head == 0, x, zero), jnp.where(lane_head == 1, x, zero)], axis=0)


def _unstack_heads(x):
    lane_head = lax.broadcasted_iota(jnp.int32, (_BQ, _PAIR), 1) // HEAD_DIM
    return jnp.where(lane_head == 0, x[:_BQ], x[_BQ:])


def _stack_cols(x):
    return jnp.concatenate([x[:, 0:1], x[:, HEAD_DIM:HEAD_DIM + 1]], axis=0)


def _fill_bias(bias_ref, slopes_ref, dilation):
    row = lax.broadcasted_iota(jnp.int32, (2 * _BQ, _KW), 0)
    col = lax.broadcasted_iota(jnp.int32, (2 * _BQ, _KW), 1)
    pair = 2 * pl.program_id(0)
    slope = jnp.where(row < _BQ, slopes_ref[pair], slopes_ref[pair + 1]) * float(dilation)

    @pl.when(pl.program_id(1) == 0)
    def _():
        for j in range(3):
            dist = jnp.abs(col - (row & (_BQ - 1)) - j * N_SIDE)
            bias_ref[j] = jnp.where(dist <= N_SIDE, -slope * dist.astype(F32), _NEG)


def _block_window(i, n_blocks, length):
    q0 = pl.multiple_of(i * _BQ, _BQ)
    ws = pl.multiple_of(jnp.clip(q0 - N_SIDE, 0, length - _KW), N_SIDE)
    return q0, ws, jnp.where(i == 0, 0, jnp.where(i == n_blocks - 1, 2, 1))


_FREE_STRIDE = 4


def _residues_per_step(dilation):
    return max(dilation // _FREE_STRIDE, 1)


def _residue_views(dilation, seq, ins, outs, tmps):
    step = pl.program_id(1)
    if dilation <= _FREE_STRIDE:
        def rows(start, count, sub=0):
            return pl.ds(start, count) if dilation == 1 else pl.ds(start * dilation + step, count, stride=dilation)

        return ins, outs, rows, lambda: None
    inner = _residues_per_step(dilation)
    assert inner <= _FREE_STRIDE and len(tmps) == len(ins) + len(outs)
    coarse = pl.ds(step, seq // _FREE_STRIDE, stride=_FREE_STRIDE)
    in_tmps, out_tmps = tmps[:len(ins)], tmps[len(ins):]
    for ref, tmp in zip(ins, in_tmps):
        tmp[...] = ref[coarse, :]

    def flush():
        for ref, tmp in zip(outs, out_tmps):
            ref[coarse, :] = tmp[...]

    return in_tmps, out_tmps, lambda start, count, sub=0: pl.ds(start * inner + sub, count, stride=inner), flush


def _of_sub(ref, sub):
    return ref.at[sub] if len(ref.shape) == 3 else ref


def _attn_call(body, name, dilation, seq, n_in, out_dtypes, scratch, buffers):
    col = pl.BlockSpec((seq, _PAIR), lambda c, r: (0, c), pipeline_mode=pl.Buffered(buffers))
    tmps = [pltpu.VMEM((seq // _FREE_STRIDE, _PAIR), F32)] * (n_in + len(out_dtypes) if dilation > _FREE_STRIDE else 0)
    return pl.pallas_call(
        body, name=name, grid=(GROUP_DIM // _PAIR, dilation // _residues_per_step(dilation)),
        in_specs=[pl.BlockSpec(memory_space=pltpu.SMEM)] + [col] * n_in, out_specs=[col] * len(out_dtypes),
        out_shape=[jax.ShapeDtypeStruct((seq, GROUP_DIM), dt) for dt in out_dtypes], scratch_shapes=scratch + tmps,
        compiler_params=_params(dimension_semantics=("arbitrary", "arbitrary")))


def _staged(dilation, length, rows, sources, scratch):
    if dilation == 1:
        return sources
    for src, dst in zip(sources, scratch):
        for sub in range(_residues_per_step(dilation)):
            dst[sub] = src[rows(0, length, sub), :].astype(BF16)
    return scratch


def _sub_and_block(i, dilation, n_blocks):
    return (0, i) if _residues_per_step(dilation) == 1 else (i // n_blocks, i % n_blocks)


def _attn_fwd(q, k, v, slopes, dilation, name):
    seq = q.shape[0]
    length = seq // dilation
    n_blocks = length // _BQ
    n_stage = 0 if dilation == 1 else 3

    def body(sl_ref, q_ref, k_ref, v_ref, o_ref, lse_ref, *scratch):
        bias_ref, tmps = scratch[n_stage], scratch[n_stage + 1:]
        (q_in, k_in, v_in), (o_out, lse_out), rows, flush = _residue_views(dilation, seq, (q_ref, k_ref, v_ref), (o_ref, lse_ref), tmps)
        qs, ks, vs = _staged(dilation, length, rows, (q_in, k_in, v_in), scratch[:n_stage])
        _fill_bias(bias_ref, sl_ref, dilation)

        def block(i, carry):
            sub, j = _sub_and_block(i, dilation, n_blocks)
            q0, ws, which = _block_window(j, n_blocks, length)
            kw = _of_sub(ks, sub)[pl.ds(ws, _KW), :]
            vw = _of_sub(vs, sub)[pl.ds(ws, _KW), :]
            sc = _dot_nt(_stack_heads(_of_sub(qs, sub)[pl.ds(q0, _BQ), :]), kw) + bias_ref[which]
            m = jnp.max(sc, axis=-1, keepdims=True)
            p = jnp.exp(sc - m)
            den = jnp.sum(p, axis=-1, keepdims=True)
            o_out[rows(q0, _BQ, sub), :] = _unstack_heads(_dot(p.astype(BF16), vw) / den)
            lse_out[rows(q0, _BQ, sub), :] = _unstack_heads(jnp.broadcast_to(m + jnp.log(den), (2 * _BQ, _PAIR)))
            return carry

        lax.fori_loop(0, trips, block, 0, unroll=min(_ATTN_UNROLL, trips))
        flush()

    trips = _residues_per_step(dilation) * n_blocks
    stage = pltpu.VMEM((_residues_per_step(dilation), length, _PAIR), BF16)
    bias = pltpu.VMEM((3, 2 * _BQ, _KW), F32)
    return _attn_call(body, name, dilation, seq, 3, [F32, F32], [stage] * n_stage + [bias], 2)(slopes, q, k, v)


def _attn_bwd(q, k, v, do, lse, cterm, slopes, dilation, name):
    seq = q.shape[0]
    length = seq // dilation
    n_blocks = length // _BQ
    n_stage, n_whole = (0, 0) if dilation == 1 else (4, 3)

    def body(sl_ref, q_ref, k_ref, v_ref, do_ref, lse_ref, c_ref, dq_ref, dk_ref, dv_ref, *scratch):
        dk_acc, dv_acc, bias_ref = scratch[n_stage:n_stage + 3]
        whole, tmps = scratch[n_stage + 3:n_stage + 3 + n_whole], scratch[n_stage + 3 + n_whole:]
        (q_in, k_in, v_in, do_in, lse_in, c_in), (dq_out, dk_out, dv_out), rows, flush = _residue_views(
            dilation, seq, (q_ref, k_ref, v_ref, do_ref, lse_ref, c_ref), whole or (dq_ref, dk_ref, dv_ref), tmps)
        qs, ks, vs, dos = _staged(dilation, length, rows, (q_in, k_in, v_in, do_in), scratch[:n_stage])
        dk_acc[...] = jnp.zeros_like(dk_acc)
        dv_acc[...] = jnp.zeros_like(dv_acc)
        _fill_bias(bias_ref, sl_ref, dilation)

        def block(i, carry):
            sub, j = _sub_and_block(i, dilation, n_blocks)
            q0, ws, which = _block_window(j, n_blocks, length)
            qm = _stack_heads(_of_sub(qs, sub)[pl.ds(q0, _BQ), :])
            dom = _stack_heads(_of_sub(dos, sub)[pl.ds(q0, _BQ), :])
            kw = _of_sub(ks, sub)[pl.ds(ws, _KW), :]
            vw = _of_sub(vs, sub)[pl.ds(ws, _KW), :]
            p = jnp.exp(_dot_nt(qm, kw) + bias_ref[which] - _stack_cols(lse_in[rows(q0, _BQ, sub), :]))
            ds = (p * (_dot_nt(dom, vw) + _stack_cols(c_in[rows(q0, _BQ, sub), :]))).astype(BF16)
            dq_out[rows(q0, _BQ, sub), :] = (_unstack_heads(_dot(ds, kw)) * _SCORE_SCALE).astype(dq_out.dtype)
            dk_acc[sub, pl.ds(ws, _KW), :] += _dot_tn(ds, qm)
            dv_acc[sub, pl.ds(ws, _KW), :] += _dot_tn(p.astype(BF16), dom)
            return carry

        lax.fori_loop(0, trips, block, 0, unroll=min(_ATTN_UNROLL, trips))
        for sub in range(per):
            dk_out[rows(0, length, sub), :] = dk_acc[sub].astype(dk_out.dtype)
            dv_out[rows(0, length, sub), :] = dv_acc[sub].astype(dv_out.dtype)
        flush()
        if whole:
            @pl.when(pl.program_id(1) == dilation // per - 1)
            def _():
                for ref, collected in zip((dq_ref, dk_ref, dv_ref), whole):
                    ref[...] = collected[...].astype(BF16)

    per = _residues_per_step(dilation)
    trips = per * n_blocks
    stage = pltpu.VMEM((per, length, _PAIR), BF16)
    acc = pltpu.VMEM((per, length, _PAIR), F32)
    bias = pltpu.VMEM((3, 2 * _BQ, _KW), F32)
    collect = pltpu.VMEM((seq, _PAIR), F32)
    return _attn_call(body, name, dilation, seq, 6, [BF16] * 3, [stage] * n_stage + [acc] * 2 + [bias] + [collect] * n_whole,
                      2 if dilation == 1 else 1)(slopes, q, k, v, do, lse, cterm)


def _group_weights(lses):
    m = jnp.maximum(jnp.maximum(lses[0], lses[1]), lses[2])
    es = [jnp.exp(l - m) for l in lses]
    den = es[0] + es[1] + es[2]
    return [e / den for e in es]


def _out_fwd(a_pool, outs, lses, x, w_out, g, name, tm=1024):
    s, d = x.shape
    width = POOL_DIM + 3 * GROUP_DIM

    def body(ap_ref, o0, o1, o2, l0, l1, l2, x_ref, w_ref, g_ref, xo_ref, cat_ref):
        alphas = _group_weights([l0[...], l1[...], l2[...]])
        cat = jnp.concatenate([ap_ref[...]] + [(o[...] * al).astype(BF16) for o, al in zip((o0, o1, o2), alphas)], axis=1)
        cat_ref[...] = cat
        mix = _dot(cat, w_ref[...])
        xo_ref[...] = x_ref[...] + mix * _inv_rms(mix) * g_ref[...]

    return pl.pallas_call(
        body, name=name, grid=(s // tm,),
        in_specs=[_rows(tm, POOL_DIM)] + [_rows(tm, GROUP_DIM)] * 6 + [_rows(tm, d), _resident(w_out.shape), _const((1, d))],
        out_specs=[_rows(tm, d), _rows(tm, width)],
        out_shape=[jax.ShapeDtypeStruct((s, d), F32), jax.ShapeDtypeStruct((s, width), BF16)],
        compiler_params=_params(dimension_semantics=("arbitrary",)))(a_pool, *outs, *lses, x, w_out, g)


def _out_bwd(dxo, cat, outs, lses, w_out, g, head_ones, name, tm=1024):
    s, d = dxo.shape

    def body(dxo_ref, cat_ref, o0, o1, o2, l0, l1, l2, w_ref, g_ref, ones_ref, dpool_ref, dmix_ref, do0, do1, do2, c0, c1, c2, dg_ref):
        mv = _dot(cat_ref[...], w_ref[...])
        dmix, dg = _rms_bwd(mv, _inv_rms(mv), g_ref[...], dxo_ref[...])
        dmb = dmix.astype(BF16)
        dmix_ref[...] = dmb
        _accumulate(dg_ref, dg)
        dcat = _dot_nt(dmb, w_ref[...])
        dpool_ref[...] = dcat[:, :POOL_DIM]
        alphas = _group_weights([l0[...], l1[...], l2[...]])
        das = [dcat[:, POOL_DIM + GROUP_DIM * j:POOL_DIM + GROUP_DIM * (j + 1)] for j in range(3)]
        prod = sum(da * (o[...] * al) for da, o, al in zip(das, (o0, o1, o2), alphas))
        hi = prod.astype(BF16)
        lo = (prod - hi.astype(F32)).astype(BF16)
        total = _dot(hi, ones_ref[...]) + _dot(lo, ones_ref[...])
        for da, al, do_ref, c_ref in zip(das, alphas, (do0, do1, do2), (c0, c1, c2)):
            do_ref[...] = (da * al).astype(do_ref.dtype)
            c_ref[...] = -al * total

    return pl.pallas_call(
        body, name=name, grid=(s // tm,),
        in_specs=[_rows(tm, d), _rows(tm, cat.shape[1])] + [_rows(tm, GROUP_DIM)] * 6 + [_resident(w_out.shape), _const((1, d)),
                                                                                        _const((GROUP_DIM, GROUP_DIM))],
        out_specs=[_rows(tm, POOL_DIM), _rows(tm, d)] + [_rows(tm, GROUP_DIM)] * 6 + [_const((1, d))],
        out_shape=[jax.ShapeDtypeStruct((s, POOL_DIM), F32), jax.ShapeDtypeStruct((s, d), BF16)]
        + [jax.ShapeDtypeStruct((s, GROUP_DIM), _attn_dtype(dil)) for dil in DILATIONS]
        + [jax.ShapeDtypeStruct((s, GROUP_DIM), F32)] * 3 + [jax.ShapeDtypeStruct((1, d), F32)],
        compiler_params=_params(dimension_semantics=("arbitrary",)))(dxo, cat, *outs, *lses, w_out, g, head_ones)


def _alibi_slopes():
    return np.array([2.0 ** (-8.0 * (i + 1) / N_ATTN_HEADS) for i in range(N_ATTN_HEADS)], np.float32)


def _block_diag(w_lin):
    n, c, _ = w_lin.shape
    eye = jnp.eye(n, dtype=w_lin.dtype)
    return (eye[:, None, :, None] * w_lin[:, :, None, :]).reshape(n * c, n * c)


class _NoExchange:
    def __init__(self, full):
        self.full, self.grads = full, {}

    def first_weights(self):
        return self.full

    def riders(self, host):
        return []

    def landed(self, host, results):
        return self.full

    def gradient(self, name, grad):
        self.grads[name] = grad

    def small_gradients(self, packed):
        self.small_packed = packed


def _local_step(x, target, small, exchange):
    s, d = x.shape
    slopes = _alibi_slopes()
    group_slopes = [jnp.asarray(slopes[4 * g:4 * g + 4]) for g in range(3)]
    w_bd = _block_diag(small["w_pool_lin"]).astype(BF16)
    head_ones = jnp.asarray(np.kron(np.eye(GROUP_DIM // HEAD_DIM), np.ones((HEAD_DIM, HEAD_DIM))), BF16)

    full = dict(exchange.first_weights())

    def hosted(call, host, *args):
        results, riding = call(*args, host, exchange.riders(host))
        full.update(exchange.landed(host, riding) or {})
        return results

    x1, a1, b1, f1 = hosted(_ffn_fwd, "ffn1_fwd", x, small["g_ffn1_pre"], full["w1_gate"], full["w1_up"], full["w1_down"],
                            small["g_ffn1_post"], None)
    h2, u, *parts = hosted(_in_fwd, "in_fwd", x1, small["g_mix_pre"], full["w_in"])
    qs, ks, vs = parts[0:3], parts[3:6], parts[6:9]
    a_pool = _pool_fwd(u, w_bd, small["pool_scale"], "pool_fwd")
    outs, lses = [], []
    for g, dil in enumerate(DILATIONS):
        o, lse = _attn_fwd(qs[g], ks[g], vs[g], group_slopes[g], dil, f"attn_fwd{g}")
        outs.append(o)
        lses.append(lse)
    x2, cat = _out_fwd(a_pool, outs, lses, x1, full["w_out"], small["g_mix_post"], "out_fwd")
    (dx3, a2, b2, f2, loss_part), _ = _ffn_fwd(x2, small["g_ffn2_pre"], full["w2_gate"], full["w2_up"], full["w2_down"],
                                               small["g_ffn2_post"], target, "ffn2_fwd")

    small_grads = {}

    def ffn_backward(tag, dxo, x_in, f, a, b):
        n = tag[-1]
        dx, hh, da, db, df, h, dg_pre, dg_post = hosted(
            _ffn_bwd, f"{tag}_bwd", dxo, x_in, f, a, b, small[f"g_{tag}_pre"], small[f"g_{tag}_post"],
            full[f"w{n}_gate"], full[f"w{n}_up"], full[f"w{n}_down"])
        small_grads[f"g_{tag}_pre"], small_grads[f"g_{tag}_post"] = dg_pre, dg_post
        if len(small_grads) == len(SMALL):
            exchange.small_gradients(_pack_small(small_grads, loss_part[0, 0]))
        for part, lhs, rhs in (("down", hh, df), ("gate", da, h), ("up", db, h)):
            exchange.gradient(f"w{n}_{part}", hosted(_wgrad, f"{tag}_wgrad_{part}", lhs, rhs))
        return dx

    dx2 = ffn_backward("ffn2", dx3, x2, f2, a2, b2)
    dpool, dmix, *dos_cs, small_grads["g_mix_post"] = _out_bwd(dx2, cat, outs, lses, full["w_out"], small["g_mix_post"],
                                                               head_ones, "out_bwd")
    dos, cs = dos_cs[:3], dos_cs[3:]
    dqs, dks, dvs = [], [], []
    for g, dil in enumerate(DILATIONS):
        dq, dk, dv = _attn_bwd(qs[g], ks[g], vs[g], dos[g], lses[g], cs[g], group_slopes[g], dil, f"attn_bwd{g}")
        dqs.append(dq)
        dks.append(dk)
        dvs.append(dv)
    du, dw_bd, small_grads["pool_scale"] = _pool_bwd(u, dpool, w_bd, small["pool_scale"], "pool_bwd")
    n_pool = len(POOL_HALF_WINDOWS)
    small_grads["w_pool_lin"] = jnp.stack(
        [dw_bd[HEAD_DIM * g:HEAD_DIM * (g + 1), HEAD_DIM * g:HEAD_DIM * (g + 1)] for g in range(n_pool)])
    dz_parts = dqs + dks + dvs
    dx1, small_grads["g_mix_pre"] = hosted(_in_bwd, "in_bwd", du, dz_parts, x1, dx2, small["g_mix_pre"], full["w_in"])
    exchange.gradient("w_in", hosted(_wgrad_parts, "wgrad_in", [du] + dz_parts, h2))
    dx0 = ffn_backward("ffn1", dx1, x, f1, a1, b1)
    exchange.gradient("w_out", hosted(_wgrad, "wgrad_out", cat, dmix))
    return loss_part[0, 0], dx0, small_grads


SEGMENTS = ("w1_gate", "w1_up", "w1_down", "w_in", "w_out", "w2_gate", "w2_up", "w2_down")
TRANSPOSED = ("w1_gate", "w1_up", "w_in", "w2_gate", "w2_up")
ROWS_OUTSIDE = ("w1_gate", "w1_up", "w2_gate", "w2_up")
HALF = 512


def _place():
    x, y, c = lax.axis_index("x"), lax.axis_index("y"), lax.axis_index("c")
    other_chips = [(1 - x, y), (x, 1 - y), (1 - x, 1 - y)]
    return x, y, c, other_chips


def _chip_rows(chip, rows):
    return pl.ds(pl.multiple_of((2 * chip[0] + chip[1]) * rows, 16), rows)


def _cols(c):
    return pl.ds(pl.multiple_of(c * HALF, HALF), HALF)


def _cast_shards(shards, transposed, place, name):
    n = len(shards)
    rows = [w.shape[1] if t else w.shape[0] for w, t in zip(shards, transposed)]

    def body(place_ref, *refs):
        for w_ref, o_ref, t in zip(refs[:n], refs[n:], transposed):
            o_ref[...] = (w_ref[...].T if t else w_ref[...]).astype(BF16)

    once = pl.Buffered(1)
    return pl.pallas_call(
        body, name=name,
        grid_spec=pltpu.PrefetchScalarGridSpec(
            num_scalar_prefetch=1, grid=(1,),
            in_specs=[pl.BlockSpec(w.shape, lambda i, place: (0, 0), pipeline_mode=once) for w in shards],
            out_specs=[pl.BlockSpec((r, 1024), lambda i, place: (place[0], 0), pipeline_mode=once) for r in rows]),
        out_shape=[jax.ShapeDtypeStruct((N_CHIPS * r, 1024), BF16) for r in rows],
        compiler_params=_params(dimension_semantics=("arbitrary",)))(place, *shards)


def _gather_weights(bufs):
    n = len(bufs)
    rows = [b.shape[0] // N_CHIPS for b in bufs]

    def halves(r):
        first = -(-r // 32) * 16
        return (0, first), (first, r - first)

    def body(*refs):
        outs = refs[n:2 * n]
        ici_send, ici_recv, d2d_send, d2d_recv = refs[2 * n:]
        x, y, c, _ = _place()
        me, via_x, via_y, diagonal = (x, y), (1 - x, y), (x, 1 - y), (1 - x, 1 - y)

        def piece(chip, k, h, cols):
            start, size = halves(rows[k])[h]
            return outs[k].at[pl.ds(pl.multiple_of((2 * chip[0] + chip[1]) * rows[k] + start, 16), size), _cols(cols)]

        def ici(path, chip, k, h, to):
            blk = piece(chip, k, h, c)
            return pltpu.make_async_remote_copy(src_ref=blk, dst_ref=blk, send_sem=ici_send.at[path, k, h],
                                                recv_sem=ici_recv.at[path, k, h], device_id=(*to, c), device_id_type=MESH)

        def d2d(slot, chip, k, h, cols):
            blk = piece(chip, k, h, cols)
            return pltpu.make_async_remote_copy(src_ref=blk, dst_ref=blk, send_sem=d2d_send.at[slot, k, h],
                                                recv_sem=d2d_recv.at[slot, k, h], device_id=(x, y, 1 - c), device_id_type=MESH)

        started = [ici(0, me, k, h, via_x) for h in (0, 1) for k in range(n)] + [ici(1, me, k, h, via_y) for h in (1, 0) for k in range(n)]
        for cp in started:
            cp.start()

        def landed(path, slot, chip, k, h, pass_on_to=None):
            ici(path, chip, k, h, me).wait_recv()
            more = [d2d(slot, chip, k, h, c)] + ([ici(2, chip, k, h, pass_on_to)] if pass_on_to else [])
            for cp in more:
                cp.start()
            started.extend(more)

        for k in range(n):
            landed(0, 0, via_x, k, 0, pass_on_to=via_y)
            landed(1, 1, via_y, k, 1, pass_on_to=via_x)
        for k in range(n):
            landed(0, 0, via_x, k, 1)
            landed(1, 1, via_y, k, 0)
        for k in range(n):
            for h in range(2):
                landed(2, 2, diagonal, k, h)
        for slot, chip in enumerate((via_x, via_y, diagonal)):
            for k in range(n):
                for h in range(2):
                    d2d(slot, chip, k, h, 1 - c).wait_recv()
        for cp in started:
            cp.wait_send()

    any_spec = pl.BlockSpec(memory_space=pl.ANY)
    return pl.pallas_call(
        body, name="gather_weights", in_specs=[any_spec] * n, out_specs=[any_spec] * n,
        out_shape=[jax.ShapeDtypeStruct(b.shape, b.dtype) for b in bufs], input_output_aliases={k: k for k in range(n)},
        scratch_shapes=[pltpu.SemaphoreType.DMA((3, n, 2))] * 4)(*bufs)


def _gather_rider(bufs):
    n = len(bufs)
    rows = [b.shape[0] // N_CHIPS for b in bufs]

    def copies(outs, send_sems, recv_sems, inbound):
        x, y, c, chips = _place()
        for j, chip in enumerate(chips):
            for k in range(n):
                src_chip = chip if inbound else (x, y)
                blk = outs[k].at[_chip_rows(src_chip, rows[k]), _cols(c)]
                yield pltpu.make_async_remote_copy(src_ref=blk, dst_ref=blk, send_sem=send_sems.at[j, k], recv_sem=recv_sems.at[j, k],
                                                   device_id=(*chip, c), device_id_type=MESH)

    def start(ins, outs, send_sems, recv_sems):
        for cp in copies(outs, send_sems, recv_sems, False):
            cp.start()

    def wait(ins, outs, send_sems, recv_sems):
        for cp in copies(outs, send_sems, recv_sems, True):
            cp.wait_recv()
        for cp in copies(outs, send_sems, recv_sems, False):
            cp.wait_send()

    return _Rider(list(bufs), None, (3, n), start, wait)


def _forward_rider(bufs):
    n = len(bufs)
    rows = [b.shape[0] // N_CHIPS for b in bufs]

    def copies(outs, send_sems, recv_sems, half):
        x, y, c, chips = _place()
        for j, chip in enumerate(chips):
            for k in range(n):
                blk = outs[k].at[_chip_rows(chip, rows[k]), _cols(half(c))]
                yield pltpu.make_async_remote_copy(src_ref=blk, dst_ref=blk, send_sem=send_sems.at[j, k], recv_sem=recv_sems.at[j, k],
                                                   device_id=(x, y, 1 - c), device_id_type=MESH)

    def start(ins, outs, send_sems, recv_sems):
        for cp in copies(outs, send_sems, recv_sems, lambda c: c):
            cp.start()

    def wait(ins, outs, send_sems, recv_sems):
        for cp in copies(outs, send_sems, recv_sems, lambda c: 1 - c):
            cp.wait_recv()
        for cp in copies(outs, send_sems, recv_sems, lambda c: c):
            cp.wait_send()

    return _Rider(list(bufs), None, (3, n), start, wait)


def _sibling_rider(grads):
    n = len(grads)

    def copies(ins, outs, send_sems, recv_sems):
        x, y, c, _ = _place()
        return [pltpu.make_async_remote_copy(src_ref=ins[k].at[:, pl.ds(1 - c, 1)], dst_ref=outs[k], send_sem=send_sems.at[k],
                                             recv_sem=recv_sems.at[k], device_id=(x, y, 1 - c), device_id_type=MESH)
                for k in range(n)]

    def start(*refs):
        for cp in copies(*refs):
            cp.start()

    def wait(*refs):
        for cp in copies(*refs):
            cp.wait()

    return _Rider(list(grads), [jax.ShapeDtypeStruct((N_CHIPS, 1) + g.shape[2:], F32) for g in grads], (n,), start, wait)


def _alone(rider, name):
    n = len(rider.operands)
    landing = rider.landing if rider.landing is not None else [jax.ShapeDtypeStruct(a.shape, a.dtype) for a in rider.operands]
    n_out = len(landing)

    def body(*refs):
        rider.start(refs[:n], refs[n:n + n_out], *refs[n + n_out:])
        rider.wait(refs[:n], refs[n:n + n_out], *refs[n + n_out:])

    any_spec = pl.BlockSpec(memory_space=pl.ANY)
    return pl.pallas_call(body, name=name, in_specs=[any_spec] * n, out_specs=[any_spec] * n_out, out_shape=landing,
                          input_output_aliases={i: i for i in range(n)} if rider.landing is None else {},
                          scratch_shapes=[pltpu.SemaphoreType.DMA(rider.sems)] * 2)(*rider.operands)


def _chip_sum(grad, from_sibling, place, name):
    rh, width = grad.shape[2:]

    def body(place_ref, g_ref, s_ref, own_ref, all_ref):
        all_ref[...] = (g_ref[...] + s_ref[...]).astype(BF16)
        mine = place_ref[0]
        own_ref[0] = g_ref[mine, 0] + s_ref[mine, 0]

    blk = (N_CHIPS, 1, rh, width)
    once = pl.Buffered(1)
    return pl.pallas_call(
        body, name=name,
        grid_spec=pltpu.PrefetchScalarGridSpec(
            num_scalar_prefetch=1, grid=(1,),
            in_specs=[pl.BlockSpec(blk, lambda i, place: (0, place[1], 0, 0), pipeline_mode=once),
                      pl.BlockSpec(blk, lambda i, place: (0, 0, 0, 0), pipeline_mode=once)],
            out_specs=[pl.BlockSpec((1, rh, width), lambda i, place: (0, 0, 0), pipeline_mode=once),
                       pl.BlockSpec(blk, lambda i, place: (0, 0, 0, 0), pipeline_mode=once)]),
        out_shape=[jax.ShapeDtypeStruct((1, rh, width), F32), jax.ShapeDtypeStruct((N_CHIPS, 1, rh, width), BF16)],
        compiler_params=_params(dimension_semantics=("arbitrary",)))(place, grad, from_sibling)


def _scatter_rider(sums):
    n = len(sums)

    def copies(ins, outs, send_sems, recv_sems):
        x, y, c, chips = _place()
        return [pltpu.make_async_remote_copy(src_ref=ins[k].at[pl.ds(2 * chip[0] + chip[1], 1)], dst_ref=outs[k].at[pl.ds(j, 1)],
                                             send_sem=send_sems.at[j, k], recv_sem=recv_sems.at[j, k],
                                             device_id=(*chip, c), device_id_type=MESH)
                for j, chip in enumerate(chips) for k in range(n)]

    def start(*refs):
        for cp in copies(*refs):
            cp.start()

    def wait(*refs):
        for cp in copies(*refs):
            cp.wait()

    return _Rider(list(sums), [jax.ShapeDtypeStruct((3,) + sm.shape[1:], BF16) for sm in sums], (3, n), start, wait)


def _total_sums(owns, received, name):
    n = len(owns)

    def body(*refs):
        for o_ref, r_ref, t_ref in zip(refs[:n], refs[n:2 * n], refs[2 * n:]):
            total = o_ref[0]
            for j in range(3):
                total = total + r_ref[j, 0].astype(F32)
            t_ref[0] = total

    return _hosted_call(body, None, name=name, steps=1, in_specs=[_resident(a.shape) for a in owns + received],
                        out_specs=[_resident(o.shape) for o in owns], out_shape=[jax.ShapeDtypeStruct(o.shape, F32) for o in owns],
                        args=owns + received)[0]


def _swap_rider(halves):
    n = len(halves)

    def copies(ins, outs, send_sems, recv_sems):
        x, y, c, _ = _place()
        return [pltpu.make_async_remote_copy(src_ref=ins[k], dst_ref=outs[k], send_sem=send_sems.at[k], recv_sem=recv_sems.at[k],
                                             device_id=(x, y, 1 - c), device_id_type=MESH) for k in range(n)]

    def start(*refs):
        for cp in copies(*refs):
            cp.start()

    def wait(*refs):
        for cp in copies(*refs):
            cp.wait()

    return _Rider(list(halves), [jax.ShapeDtypeStruct(h.shape, F32) for h in halves], (n,), start, wait)


N_DEV = 8


def _small_rider(block):
    m_per, width = block.shape

    def copies(ins, outs, send_sems, recv_sems):
        (x_ref,), (out_ref,) = ins, outs
        x, y, c, chips = _place()
        me, sibling = (x, y, c), (x, y, 1 - c)

        def rows(px, py, pc):
            return out_ref.at[pl.ds((4 * px + 2 * py + pc) * m_per, m_per), :]

        def copy(k, blk, to, src=None):
            return pltpu.make_async_remote_copy(src_ref=rows(*blk) if src is None else src, dst_ref=rows(*blk),
                                                send_sem=send_sems.at[k], recv_sem=recv_sems.at[k], device_id=to, device_id_type=MESH)

        mine = pltpu.make_async_copy(x_ref, rows(*me), send_sems.at[7])
        first = [copy(0, me, sibling, src=x_ref)] + [copy(1 + j, me, (*chip, c), src=x_ref) for j, chip in enumerate(chips)]

        def second_hop():
            arriving = [copy(1 + j, (*chip, c), me) for j, chip in enumerate(chips)]
            passed = [copy(4 + j, (*chip, c), sibling) for j, chip in enumerate(chips)]
            last = [copy(0, sibling, me)] + [copy(4 + j, (*chip, 1 - c), me) for j, chip in enumerate(chips)]
            return arriving, passed, last

        return mine, first, second_hop

    def start(*refs):
        mine, first, _ = copies(*refs)
        mine.start()
        for cp in first:
            cp.start()

    def wait(*refs):
        mine, first, second_hop = copies(*refs)
        arriving, passed, last = second_hop()
        for arrived, onward in zip(arriving, passed):
            arrived.wait_recv()
            onward.start()
        for cp in last:
            cp.wait_recv()
        for cp in first + passed:
            cp.wait_send()
        mine.wait()

    return _Rider([block], [jax.ShapeDtypeStruct((N_DEV * m_per, width), F32)], (8,), start, wait)


def _adamw_math(w, g, m, v):
    m = ADAM_B1 * m + (1.0 - ADAM_B1) * g
    v = ADAM_B2 * v + (1.0 - ADAM_B2) * (g * g)
    m_hat = m / (1.0 - ADAM_B1 ** ADAM_STEP)
    v_hat = v / (1.0 - ADAM_B2 ** ADAM_STEP)
    delta = -ADAM_LR * (m_hat / (jnp.sqrt(v_hat) + ADAM_EPS) + ADAM_WD * w)
    return delta, m, v


def _adamw(w, mine, siblings, place, m, v, transposed, name):
    rh, width = mine.shape[1:]
    place_spec = pl.BlockSpec(memory_space=pltpu.SMEM)
    halves = [_const((1, rh, width))] * 2
    out_shape = [jax.ShapeDtypeStruct(w.shape, F32)] * 4
    if transposed:
        def body(place_ref, w_ref, mine_ref, sib_ref, m_ref, v_ref, go_ref, d_ref, mo_ref, vo_ref):
            first = place_ref[1] == 0
            g = jnp.concatenate([jnp.where(first, mine_ref[0], sib_ref[0]), jnp.where(first, sib_ref[0], mine_ref[0])], axis=0).T
            go_ref[...] = g
            d_ref[...], mo_ref[...], vo_ref[...] = _adamw_math(w_ref[...], g, m_ref[...], v_ref[...])

        whole = _resident(w.shape)
        return _hosted_call(body, None, name=name, steps=1, in_specs=[place_spec, whole] + halves + [whole, whole],
                            out_specs=[whole] * 4, out_shape=out_shape, args=[place, w, mine, siblings, m, v])[0]

    def body(place_ref, w_ref, mine_ref, sib_ref, m_ref, v_ref, go_ref, d_ref, mo_ref, vo_ref):
        g = jnp.where(pl.program_id(0) == place_ref[1], mine_ref[0], sib_ref[0])
        go_ref[...] = g
        d_ref[...], mo_ref[...], vo_ref[...] = _adamw_math(w_ref[...], g, m_ref[...], v_ref[...])

    half = _rows(rh, width)
    return _hosted_call(body, None, name=name, steps=2, in_specs=[place_spec, half] + halves + [half, half],
                        out_specs=[half] * 4, out_shape=out_shape, args=[place, w, mine, siblings, m, v])[0]


def _adamw_small(gathered, w, m, v, name):
    def body(ga_ref, w_ref, m_ref, v_ref, go_ref, d_ref, mo_ref, vo_ref):
        g = ga_ref[0]
        for dev in range(1, N_DEV):
            g = g + ga_ref[dev]
        go_ref[...] = g
        d_ref[...], mo_ref[...], vo_ref[...] = _adamw_math(w_ref[...], g, m_ref[...], v_ref[...])

    return pl.pallas_call(body, name=name, out_shape=[jax.ShapeDtypeStruct(w.shape, F32)] * 4,
                          compiler_params=_params())(gathered, w, m, v)


class _Exchange:
    FIRST = ("w1_gate", "w1_up", "w1_down")
    HOSTS = {"ffn2_wgrad_gate": (("w2_down",), ()), "ffn2_wgrad_up": (("w2_gate",), ("w2_down",)),
             "in_bwd": (("w2_up",), ("w2_gate",)), "wgrad_in": ((), ("w2_up",)),
             "ffn1_wgrad_down": ((), ("w_in",)), "ffn1_wgrad_gate": (("w1_down",), ()), "ffn1_wgrad_up": ((), ("w1_down", "w1_gate")),
             "wgrad_out": ((), ("w1_up",))}
    ALONE = ("w_in", "w1_gate", "w1_up", "w_out")
    SWAP_HOST = "wgrad_out"
    SMALL_HOST = "ffn1_wgrad_gate"

    def __init__(self, bufs, place):
        self.bufs, self.place = bufs, place
        self.later = [k for k in SEGMENTS if k not in self.FIRST]
        self.split, self.own, self.to_send, self.received = {}, {}, {}, {}

    def first_weights(self):
        return dict(zip(self.FIRST, _gather_weights([self.bufs[k] for k in self.FIRST])))

    def riders(self, host):
        if host == "ffn1_fwd":
            return [_gather_rider([self.bufs[k] for k in self.later])]
        if host == "in_fwd":
            return [_forward_rider([self.bufs[k] for k in self.later[1:]])]
        halves, sums = self.HOSTS.get(host, ((), ()))
        riders = ([_sibling_rider([self.split[k] for k in halves])] if halves else []) + (
            [_scatter_rider([self.to_send[k] for k in sums])] if sums else [])
        if host == self.SWAP_HOST:
            self.early = [k for k in SEGMENTS if k in self.received]
            self.mine = dict(zip(self.early, self._totals(self.early, "total_sums_early")))
            riders.append(_swap_rider([self.mine[k] for k in self.early]))
        if host == self.SMALL_HOST:
            riders.append(_small_rider(self.small_packed))
        return riders

    def small_gradients(self, packed):
        self.small_packed = packed

    def landed(self, host, results):
        if host == "ffn1_fwd":
            self.bufs.update(zip(self.later, results[0]))
            return dict(zip(self.later[:1], _alone(_forward_rider([self.bufs[self.later[0]]]), "gather_forward_first")))
        if host == "in_fwd":
            return dict(zip(self.later[1:], results[0]))
        results = list(results)
        if host == self.SWAP_HOST:
            self.siblings = dict(zip(self.early, results.pop()))
        if host == self.SMALL_HOST:
            (self.small_gathered,) = results.pop()
        halves, sums = self.HOSTS.get(host, ((), ()))
        if halves:
            self._chip_sums(halves, results[0])
        if sums:
            self.received.update(zip(sums, results[-1]))

    def gradient(self, name, grad):
        self.split[name] = grad.reshape(N_CHIPS, 2, grad.shape[0] // (2 * N_CHIPS), grad.shape[1])
        if name in self.ALONE:
            self._chip_sums([name], _alone(_sibling_rider([self.split[name]]), f"reduce_sibling_{name}"))

    def _chip_sums(self, names, from_sibling):
        for k, fs in zip(names, from_sibling):
            self.own[k], self.to_send[k] = _chip_sum(self.split[k], fs, self.place, f"chip_sum_{k}")

    def _totals(self, names, call_name):
        return _total_sums([self.own[k] for k in names], [self.received[k] for k in names], call_name)

    def summed_halves(self):
        late = [k for k in SEGMENTS if k not in self.received]
        self.received.update(zip(late, _alone(_scatter_rider([self.to_send[k] for k in late]), "reduce_chips_last")))
        rest = [k for k in SEGMENTS if k not in self.early]
        mine = self._totals(rest, "total_sums")
        self.mine.update(zip(rest, mine))
        self.siblings.update(zip(rest, _alone(_swap_rider(mine), "swap_halves")))
        return [self.mine[k] for k in SEGMENTS], [self.siblings[k] for k in SEGMENTS]


SMALL = ("g_ffn1_pre", "g_ffn1_post", "g_mix_pre", "w_pool_lin", "pool_scale", "g_mix_post", "g_ffn2_pre", "g_ffn2_post")
WEIGHTS = ("g_ffn1_pre", "w1_gate", "w1_up", "w1_down", "g_ffn1_post", "g_mix_pre", "w_in", "w_pool_lin", "pool_scale", "w_out",
           "g_mix_post", "g_ffn2_pre", "w2_gate", "w2_up", "w2_down", "g_ffn2_post")
LANES = 128


def _pack_small(tree, extra=0.0):
    flat = jnp.concatenate([tree[k].reshape(-1) for k in SMALL] + [jnp.reshape(extra, (1,)).astype(F32)])
    rows = -(-flat.shape[0] // (8 * LANES)) * 8
    return jnp.pad(flat, (0, rows * LANES - flat.shape[0])).reshape(rows, LANES)


def _unpack_small(packed, like):
    flat, out, at = packed.reshape(-1), {}, 0
    for k in SMALL:
        size = math.prod(like[k].shape)
        out[k] = flat[at:at + size].reshape(like[k].shape)
        at += size
    return out


def kernel(x, g_ffn1_pre, w1_gate, w1_up, w1_down, g_ffn1_post, g_mix_pre, w_in, w_pool_lin, pool_scale, w_out, g_mix_post, g_ffn2_pre, w2_gate, w2_up, w2_down, g_ffn2_post, loss_target, m_g_ffn1_pre, m_w1_gate, m_w1_up, m_w1_down, m_g_ffn1_post, m_g_mix_pre, m_w_in, m_w_pool_lin, m_pool_scale, m_w_out, m_g_mix_post, m_g_ffn2_pre, m_w2_gate, m_w2_up, m_w2_down, m_g_ffn2_post, v_g_ffn1_pre, v_w1_gate, v_w1_up, v_w1_down, v_g_ffn1_post, v_g_mix_pre, v_w_in, v_w_pool_lin, v_pool_scale, v_w_out, v_g_mix_post, v_g_ffn2_pre, v_w2_gate, v_w2_up, v_w2_down, v_g_ffn2_post):
    given = dict(locals())
    w = {k: given[k] for k in WEIGHTS}
    m = {k: given["m_" + k] for k in WEIGHTS}
    v = {k: given["v_" + k] for k in WEIGHTS}
    small = {k: (w[k][0] if k == "w_pool_lin" else w[k].reshape(1, -1)) for k in SMALL}

    place = jnp.stack([2 * lax.axis_index("x") + lax.axis_index("y"), lax.axis_index("c")]).astype(jnp.int32)
    def as_rows(a, k):
        return jnp.swapaxes(a, 1, 2)[0] if k in ROWS_OUTSIDE else a[0]

    def as_given(a, k):
        return jnp.swapaxes(a[None], 1, 2) if k in ROWS_OUTSIDE else a[None]

    in_kernel = [k for k in TRANSPOSED if k not in ROWS_OUTSIDE]
    bufs = {}
    for tag, names in (("first", _Exchange.FIRST), ("rest", [k for k in SEGMENTS if k not in _Exchange.FIRST])):
        bufs.update(zip(names, _cast_shards([as_rows(w[k], k) for k in names], [k in in_kernel for k in names], place, f"cast_{tag}")))
    exchange = _Exchange(bufs, place)
    _, grad_x, _ = _local_step(x[0], loss_target[0], small, exchange)

    out_grad, out_delta, out_m, out_v = {}, {}, {}, {}
    for k, mine, siblings in zip(SEGMENTS, *exchange.summed_halves()):
        results = _adamw(as_rows(w[k], k), mine, siblings, place, as_rows(m[k], k), as_rows(v[k], k), k in in_kernel, f"adamw_{k}")
        out_grad[k], out_delta[k], out_m[k], out_v[k] = (as_given(a, k) for a in results)

    gathered = exchange.small_gathered.reshape(N_DEV, -1, LANES)
    like = {k: w[k] for k in SMALL}
    results = _adamw_small(gathered, _pack_small(like), _pack_small({k: m[k] for k in SMALL}),
                           _pack_small({k: v[k] for k in SMALL}), "adamw_small")
    for tree, res in zip((out_grad, out_delta, out_m, out_v), results):
        tree.update(_unpack_small(res, like))
    loss = results[0].reshape(-1)[sum(math.prod(like[k].shape) for k in SMALL)]

    return (loss, grad_x[None], *[out_grad[k] for k in WEIGHTS], *[out_delta[k] for k in WEIGHTS],
            *[out_m[k] for k in WEIGHTS], *[out_v[k] for k in WEIGHTS])
```

```python
import math
import typing

import numpy as np
import jax
import jax.numpy as jnp
from jax import lax
from jax.experimental import pallas as pl
from jax.experimental.pallas import tpu as pltpu

F32 = jnp.float32
BF16 = jnp.bfloat16
MESH = pl.DeviceIdType.MESH

RMS_EPS = 1e-6
HEAD_DIM = 64
POOL_HALF_WINDOWS = (1, 2, 4, 8)
POOL_DIM = 256
GROUP_DIM = 256
DILATIONS = (1, 4, 16)
N_SIDE = 64
N_ATTN_HEADS = 12
ADAM_LR, ADAM_B1, ADAM_B2, ADAM_EPS, ADAM_WD, ADAM_STEP = 0.001, 0.9, 0.999, 1e-08, 0.01, 10

N_CHIPS = 4
V7X_VMEM_LIMIT = 60 * 1024 * 1024

_NT = (((1,), (1,)), ((), ()))
_TN = (((0,), (0,)), ((), ()))


def _dot(a, b):
    return jnp.dot(a, b, preferred_element_type=F32)


def _dot_nt(a, b):
    return lax.dot_general(a, b, _NT, preferred_element_type=F32)


def _dot_tn(a, b):
    return lax.dot_general(a, b, _TN, preferred_element_type=F32)


def _params(**kw):
    return pltpu.CompilerParams(vmem_limit_bytes=V7X_VMEM_LIMIT, **kw)


def _rows(tm, width):
    return pl.BlockSpec((tm, width), lambda i: (i, 0))


def _resident(shape):
    return pl.BlockSpec(shape, lambda i: (0,) * len(shape), pipeline_mode=pl.Buffered(1))


def _const(shape):
    return pl.BlockSpec(shape, lambda i: (0,) * len(shape))


def _inv_rms(x):
    return lax.rsqrt(jnp.mean(x * x, axis=-1, keepdims=True) + RMS_EPS)


def _rms_bwd(x, inv, g, dy):
    n = x * inv
    dn = dy * g
    dx = inv * (dn - n * jnp.mean(dn * n, axis=-1, keepdims=True))
    return dx, jnp.sum(dy * n, axis=0, keepdims=True)


def _accumulate(ref, value):
    @pl.when(pl.program_id(0) == 0)
    def _():
        ref[...] = jnp.zeros_like(ref)

    ref[...] += value


class _Rider(typing.NamedTuple):
    operands: list
    landing: typing.Optional[list]
    sems: tuple
    start: typing.Callable
    wait: typing.Callable


def _hosted_call(body, riders, *, name, steps, in_specs, out_specs, out_shape, args, scratch_shapes=()):
    params = _params(dimension_semantics=("arbitrary",))
    riders = list(riders or [])
    if not riders:
        res = pl.pallas_call(body, name=name, grid=(steps,), in_specs=in_specs, out_specs=out_specs, out_shape=out_shape,
                             scratch_shapes=list(scratch_shapes), compiler_params=params)(*args)
        return list(res), []
    n_in, n_out, n_scratch = len(in_specs), len(out_specs), len(scratch_shapes)
    operands, landing, aliases, spans = [], [], {}, []
    for rd in riders:
        lands = rd.landing if rd.landing is not None else [jax.ShapeDtypeStruct(a.shape, a.dtype) for a in rd.operands]
        if rd.landing is None:
            aliases.update({n_in + len(operands) + i: n_out + len(landing) + i for i in range(len(lands))})
        spans.append((len(operands), len(rd.operands), len(landing), len(lands)))
        operands += rd.operands
        landing += lands
    outs_at = n_in + len(operands)
    scratch_at = outs_at + n_out + len(landing)

    def riding(*refs):
        def each(action):
            for i, (rd, (in_at, n_ops, out_at, n_lands)) in enumerate(zip(riders, spans)):
                sems = refs[scratch_at + n_scratch + 2 * i:scratch_at + n_scratch + 2 * i + 2]
                getattr(rd, action)(refs[n_in + in_at:n_in + in_at + n_ops],
                                    refs[outs_at + n_out + out_at:outs_at + n_out + out_at + n_lands], *sems)

        @pl.when(pl.program_id(0) == 0)
        def _():
            each("start")

        body(*refs[:n_in], *refs[outs_at:outs_at + n_out], *refs[scratch_at:scratch_at + n_scratch])

        @pl.when(pl.program_id(0) == steps - 1)
        def _():
            each("wait")

    any_spec = pl.BlockSpec(memory_space=pl.ANY)
    res = pl.pallas_call(
        riding, name=name, grid=(steps,), in_specs=list(in_specs) + [any_spec] * len(operands),
        out_specs=list(out_specs) + [any_spec] * len(landing), out_shape=list(out_shape) + landing,
        scratch_shapes=list(scratch_shapes) + [pltpu.SemaphoreType.DMA(rd.sems) for rd in riders for _ in range(2)],
        input_output_aliases=aliases, compiler_params=params)(*args, *operands)
    return list(res[:n_out]), [list(res[n_out + out_at:n_out + out_at + n_lands]) for _, _, out_at, n_lands in spans]


_SUB_TILE = 256


def _sub_tiles(tm):
    return [pl.ds(r, _SUB_TILE) for r in range(0, tm, _SUB_TILE)]


def _ffn_fwd(x, g_pre, wg_t, wu_t, wd, g_post, target, name, riders=None, tm=512):
    s, d = x.shape
    ff = wd.shape[0]
    with_loss = target is not None

    def body(*refs):
        if with_loss:
            x_ref, gpre_ref, wg_ref, wu_ref, wd_ref, gpost_ref, t_ref, xo_ref, a_ref, b_ref, f_ref, loss_ref = refs
        else:
            x_ref, gpre_ref, wg_ref, wu_ref, wd_ref, gpost_ref, xo_ref, a_ref, b_ref, f_ref = refs
        loss = 0.0
        for rows in _sub_tiles(tm):
            xv = x_ref[rows, :]
            hb = (xv * _inv_rms(xv) * gpre_ref[...]).astype(BF16)
            a = _dot_nt(hb, wg_ref[...])
            b = _dot_nt(hb, wu_ref[...])
            hh = (a * jax.nn.sigmoid(a)) * b
            f = _dot(hh.astype(BF16), wd_ref[...])
            xo = xv + 0.5 * (f * _inv_rms(f) * gpost_ref[...])
            a_ref[rows, :] = a.astype(BF16)
            b_ref[rows, :] = b.astype(BF16)
            f_ref[rows, :] = f
            if with_loss:
                e = xo - t_ref[rows, :]
                xo_ref[rows, :] = e * (1.0 / d)
                loss = loss + 0.5 * jnp.sum(jnp.mean(e * e, axis=-1, keepdims=True))
            else:
                xo_ref[rows, :] = xo
        if with_loss:
            _accumulate(loss_ref, loss)

    in_specs = [_rows(tm, d), _const((1, d)), _resident((ff, d)), _resident((ff, d)), _resident((ff, d)), _const((1, d))]
    args = [x, g_pre, wg_t, wu_t, wd, g_post]
    out_shape = [jax.ShapeDtypeStruct((s, d), F32), jax.ShapeDtypeStruct((s, ff), BF16),
                 jax.ShapeDtypeStruct((s, ff), BF16), jax.ShapeDtypeStruct((s, d), F32)]
    out_specs = [_rows(tm, d), _rows(tm, ff), _rows(tm, ff), _rows(tm, d)]
    if with_loss:
        in_specs.append(_rows(tm, d))
        args.append(target)
        out_shape.append(jax.ShapeDtypeStruct((8, 128), F32))
        out_specs.append(_const((8, 128)))
    return _hosted_call(body, riders, name=name, steps=s // tm, in_specs=in_specs, out_specs=out_specs, out_shape=out_shape, args=args)


def _ffn_bwd(dxo, x, f, a, b, g_pre, g_post, wg_t, wu_t, wd, name, riders=None, tm=256):
    s, d = x.shape
    ff = wd.shape[0]

    def body(dxo_ref, x_ref, f_ref, a_ref, b_ref, gpre_ref, gpost_ref, wg_ref, wu_ref, wd_ref,
             dx_ref, hh_ref, da_ref, db_ref, df_ref, h_ref, dgpre_ref, dgpost_ref):
        dgpre_sum = dgpost_sum = 0.0
        for rows in _sub_tiles(tm):
            dxo_v = dxo_ref[rows, :]
            fv = f_ref[rows, :]
            df, dgpost = _rms_bwd(fv, _inv_rms(fv), gpost_ref[...], 0.5 * dxo_v)
            dfb = df.astype(BF16)
            dhh = _dot_nt(dfb, wd_ref[...])
            av = a_ref[rows, :].astype(F32)
            bv = b_ref[rows, :].astype(F32)
            sig = jax.nn.sigmoid(av)
            sa = av * sig
            da = (dhh * bv * (sig * (1.0 + av * (1.0 - sig)))).astype(BF16)
            db = (dhh * sa).astype(BF16)
            dh = _dot(da, wg_ref[...]) + _dot(db, wu_ref[...])
            xv = x_ref[rows, :]
            inv = _inv_rms(xv)
            dxn, dgpre = _rms_bwd(xv, inv, gpre_ref[...], dh)
            dx_ref[rows, :] = dxo_v + dxn
            hh_ref[rows, :] = (sa * bv).astype(BF16)
            da_ref[rows, :] = da
            db_ref[rows, :] = db
            df_ref[rows, :] = dfb
            h_ref[rows, :] = (xv * inv * gpre_ref[...]).astype(BF16)
            dgpre_sum, dgpost_sum = dgpre_sum + dgpre, dgpost_sum + dgpost
        _accumulate(dgpre_ref, dgpre_sum)
        _accumulate(dgpost_ref, dgpost_sum)

    return _hosted_call(
        body, riders, name=name, steps=s // tm,
        in_specs=[_rows(tm, d), _rows(tm, d), _rows(tm, d), _rows(tm, ff), _rows(tm, ff), _const((1, d)), _const((1, d)),
                  _resident((ff, d)), _resident((ff, d)), _resident((ff, d))],
        out_specs=[_rows(tm, d), _rows(tm, ff), _rows(tm, ff), _rows(tm, ff), _rows(tm, d), _rows(tm, d),
                   _const((1, d)), _const((1, d))],
        out_shape=[jax.ShapeDtypeStruct((s, d), F32), jax.ShapeDtypeStruct((s, ff), BF16), jax.ShapeDtypeStruct((s, ff), BF16),
                   jax.ShapeDtypeStruct((s, ff), BF16), jax.ShapeDtypeStruct((s, d), BF16), jax.ShapeDtypeStruct((s, d), BF16),
                   jax.ShapeDtypeStruct((1, d), F32), jax.ShapeDtypeStruct((1, d), F32)],
        args=[dxo, x, f, a, b, g_pre, g_post, wg_t, wu_t, wd])


def _wgrad(lhs, rhs, name, riders=None, rt=256):
    s, r = lhs.shape
    c = rhs.shape[1]

    def body(l_ref, r_ref, o_ref):
        o_ref[...] = _dot_tn(l_ref[...], r_ref[...])

    (out,), riding = _hosted_call(
        body, riders, name=name, steps=pl.cdiv(r, rt), in_specs=[pl.BlockSpec((s, rt), lambda i: (0, i)), _resident((s, c))],
        out_specs=[pl.BlockSpec((rt, c), lambda i: (i, 0))], out_shape=[jax.ShapeDtypeStruct((r, c), F32)], args=[lhs, rhs])
    return out, riding


def _attn_dtype(dilation):
    return BF16 if dilation == 1 else F32


def _in_fwd(x, g, w_in_t, name, riders=None, tm=1024):
    s, d = x.shape
    d_in = w_in_t.shape[0]
    n_groups = len(DILATIONS)
    dtypes = [_attn_dtype(dil) for dil in DILATIONS] * 3

    def body(x_ref, g_ref, w_ref, h_ref, u_ref, *part_refs):
        xv = x_ref[...]
        hb = (xv * _inv_rms(xv) * g_ref[...]).astype(BF16)
        h_ref[...] = hb
        z = _dot_nt(hb, w_ref[...])
        u_ref[...] = z[:, :POOL_DIM]
        for j, ref in enumerate(part_refs):
            part = z[:, POOL_DIM + GROUP_DIM * j:POOL_DIM + GROUP_DIM * (j + 1)]
            ref[...] = (part * _SCORE_SCALE if j < n_groups else part).astype(ref.dtype)

    return _hosted_call(
        body, riders, name=name, steps=s // tm, in_specs=[_rows(tm, d), _const((1, d)), _resident((d_in, d))],
        out_specs=[_rows(tm, d), _rows(tm, POOL_DIM)] + [_rows(tm, GROUP_DIM)] * len(dtypes),
        out_shape=[jax.ShapeDtypeStruct((s, d), BF16), jax.ShapeDtypeStruct((s, POOL_DIM), F32)]
        + [jax.ShapeDtypeStruct((s, GROUP_DIM), dt) for dt in dtypes],
        args=[x, g, w_in_t])


def _in_bwd(du, dparts, x, dxo, g, w_in_t, name, riders=None, tm=512):
    s, d = x.shape
    d_in = w_in_t.shape[0]
    n_parts = len(dparts)

    def body(du_ref, *refs):
        part_refs = refs[:n_parts]
        x_ref, dxo_ref, g_ref, w_ref, dx_ref, dg_ref = refs[n_parts:]
        dh = _dot(jnp.concatenate([r[...] for r in (du_ref,) + part_refs], axis=1), w_ref[...])
        xv = x_ref[...]
        dxn, dg = _rms_bwd(xv, _inv_rms(xv), g_ref[...], dh)
        dx_ref[...] = dxo_ref[...] + dxn
        _accumulate(dg_ref, dg)

    return _hosted_call(
        body, riders, name=name, steps=s // tm,
        in_specs=[_rows(tm, POOL_DIM)] + [_rows(tm, GROUP_DIM)] * n_parts + [_rows(tm, d), _rows(tm, d), _const((1, d)),
                                                                             _resident((d_in, d))],
        out_specs=[_rows(tm, d), _const((1, d))],
        out_shape=[jax.ShapeDtypeStruct((s, d), F32), jax.ShapeDtypeStruct((1, d), F32)],
        args=[du, *dparts, x, dxo, g, w_in_t])


def _wgrad_parts(parts, rhs, name, riders=None):
    n = len(parts)
    s, rt = parts[0].shape
    c = rhs.shape[1]

    def body(*refs):
        part_refs, r_ref, o_ref, buf, sems = refs[:n], refs[n], refs[n + 1], refs[n + 2], refs[n + 3]

        def fetch(i):
            return pltpu.make_async_copy(part_refs[i], buf.at[i % 2], sems.at[i % 2])

        fetch(0).start()
        for i in range(n):
            if i + 1 < n:
                fetch(i + 1).start()
            fetch(i).wait()
            o_ref[pl.ds(i * rt, rt), :] = _dot_tn(buf[i % 2], r_ref[...])

    (out,), riding = _hosted_call(
        body, riders, name=name, steps=1, in_specs=[pl.BlockSpec(memory_space=pl.ANY)] * n + [_resident((s, c))],
        out_specs=[_resident((n * rt, c))], out_shape=[jax.ShapeDtypeStruct((n * rt, c), F32)], args=[*parts, rhs],
        scratch_shapes=[pltpu.VMEM((2, s, rt), BF16), pltpu.SemaphoreType.DMA((2,))])
    return out, riding


_POOL_HALO = 8


def _pool_chain(v, first_shift):
    n = v.shape[0]
    p2 = v + pltpu.roll(v, first_shift, 0)
    p4 = pltpu.roll(p2, 1, 0) + pltpu.roll(p2, n - 1, 0)
    p8 = pltpu.roll(p4, 2, 0) + pltpu.roll(p4, n - 2, 0)
    p16 = pltpu.roll(p8, 4, 0) + pltpu.roll(p8, n - 4, 0)
    group = lax.broadcasted_iota(jnp.int32, v.shape, 1) // HEAD_DIM
    return jnp.where(group == 0, p2, jnp.where(group == 1, p4, jnp.where(group == 2, p8, p16)))


def _pool_count(t0, rows, s):
    t = t0 + lax.broadcasted_iota(jnp.int32, (rows, POOL_DIM), 0)
    group = lax.broadcasted_iota(jnp.int32, (rows, POOL_DIM), 1) // HEAD_DIM
    half = jnp.where(group == 0, 1, jnp.where(group == 1, 2, jnp.where(group == 2, 4, 8)))
    cnt = jnp.minimum(t + half, s) - jnp.maximum(t - half, 0)
    return jnp.maximum(cnt, 1).astype(F32)


def _pad_rows(ref, pad_ref, s):
    zeros = jnp.zeros((_POOL_HALO, pad_ref.shape[1]), pad_ref.dtype)
    pad_ref[pl.ds(0, _POOL_HALO), :] = zeros
    pad_ref[pl.ds(_POOL_HALO + s, _POOL_HALO), :] = zeros
    pad_ref[pl.ds(_POOL_HALO, s), :] = ref[...]


def _pool_fwd(u, w_bd, scale, name, tm=512):
    s = u.shape[0]
    ext = tm + 2 * _POOL_HALO

    def body(u_ref, w_ref, sc_ref, o_ref, upad):
        _pad_rows(u_ref, upad, s)

        def tile(i, carry):
            t0 = pl.multiple_of(i * tm, tm)
            uv = upad[pl.ds(t0, ext), :]
            win = _pool_chain(uv, 1)[_POOL_HALO:_POOL_HALO + tm]
            y = win / _pool_count(t0, tm, s) - uv[_POOL_HALO:_POOL_HALO + tm]
            o_ref[pl.ds(t0, tm), :] = (_dot(y.astype(BF16), w_ref[...]) * sc_ref[...]).astype(BF16)
            return carry

        lax.fori_loop(0, s // tm, tile, 0)

    return pl.pallas_call(body, name=name, out_shape=jax.ShapeDtypeStruct((s, POOL_DIM), BF16),
                          scratch_shapes=[pltpu.VMEM((s + 2 * _POOL_HALO, POOL_DIM), F32)],
                          compiler_params=_params())(u, w_bd, scale)


def _pool_bwd(u, da, w_bd, scale, name, tm=512):
    s = u.shape[0]
    ext = tm + 2 * _POOL_HALO

    def body(u_ref, da_ref, w_ref, sc_ref, du_ref, dw_ref, dsc_ref, upad, dapad):
        _pad_rows(u_ref, upad, s)
        _pad_rows(da_ref, dapad, s)
        dw_ref[...] = jnp.zeros_like(dw_ref)
        dsc_ref[...] = jnp.zeros_like(dsc_ref)

        def tile(i, carry):
            t0 = pl.multiple_of(i * tm, tm)
            uv = upad[pl.ds(t0, ext), :]
            dav = dapad[pl.ds(t0, ext), :]
            win = _pool_chain(uv, 1)[_POOL_HALO:_POOL_HALO + tm]
            yb = (win / _pool_count(t0, tm, s) - uv[_POOL_HALO:_POOL_HALO + tm]).astype(BF16)
            yl = _dot(yb, w_ref[...])
            da_c = dav[_POOL_HALO:_POOL_HALO + tm]
            dsc_ref[...] += jnp.sum(da_c * yl, axis=0, keepdims=True)
            dyl = (dav * sc_ref[...]).astype(BF16)
            dw_ref[...] += _dot_tn(yb, dyl[_POOL_HALO:_POOL_HALO + tm])
            dy = _dot_nt(dyl, w_ref[...])
            dyc = dy / _pool_count(t0 - _POOL_HALO, ext, s)
            du_ref[pl.ds(t0, tm), :] = (_pool_chain(dyc, ext - 1) - dy)[_POOL_HALO:_POOL_HALO + tm].astype(BF16)
            return carry

        lax.fori_loop(0, s // tm, tile, 0)

    pool_cols = pl.BlockSpec((s, POOL_DIM), lambda i: (0, 0), pipeline_mode=pl.Buffered(1))
    return pl.pallas_call(
        body, name=name, grid=(1,),
        in_specs=[pool_cols, pool_cols, _const((POOL_DIM, POOL_DIM)), _const((1, POOL_DIM))],
        out_specs=[_const((s, POOL_DIM)), _const((POOL_DIM, POOL_DIM)), _const((1, POOL_DIM))],
        out_shape=[jax.ShapeDtypeStruct((s, POOL_DIM), BF16), jax.ShapeDtypeStruct((POOL_DIM, POOL_DIM), F32),
                   jax.ShapeDtypeStruct((1, POOL_DIM), F32)],
        scratch_shapes=[pltpu.VMEM((s + 2 * _POOL_HALO, POOL_DIM), F32), pltpu.VMEM((s + 2 * _POOL_HALO, POOL_DIM), F32)],
        compiler_params=_params(dimension_semantics=("arbitrary",)))(u, da, w_bd, scale)


_BQ = 128
_KW = _BQ + 2 * N_SIDE
_PAIR = 2 * HEAD_DIM
_NEG = -1e30
_ATTN_UNROLL = 8
_SCORE_SCALE = HEAD_DIM ** -0.5


def _stack_heads(x):
    lane_head = lax.broadcasted_iota(jnp.int32, x.shape, 1) // HEAD_DIM
    zero = jnp.zeros_like(x)
    return jnp.concatenate([jnp.where(lane_head == 0, x, zero), jnp.where(lane_head == 1, x, zero)], axis=0)


def _unstack_heads(x):
    lane_head = lax.broadcasted_iota(jnp.int32, (_BQ, _PAIR), 1) // HEAD_DIM
    return jnp.where(lane_head == 0, x[:_BQ], x[_BQ:])


def _stack_cols(x):
    return jnp.concatenate([x[:, 0:1], x[:, HEAD_DIM:HEAD_DIM + 1]], axis=0)


def _fill_bias(bias_ref, slopes_ref, dilation):
    row = lax.broadcasted_iota(jnp.int32, (2 * _BQ, _KW), 0)
    col = lax.broadcasted_iota(jnp.int32, (2 * _BQ, _KW), 1)
    pair = 2 * pl.program_id(0)
    slope = jnp.where(row < _BQ, slopes_ref[pair], slopes_ref[pair + 1]) * float(dilation)

    @pl.when(pl.program_id(1) == 0)
    def _():
        for j in range(3):
            dist = jnp.abs(col - (row & (_BQ - 1)) - j * N_SIDE)
            bias_ref[j] = jnp.where(dist <= N_SIDE, -slope * dist.astype(F32), _NEG)


def _block_window(i, n_blocks, length):
    q0 = pl.multiple_of(i * _BQ, _BQ)
    ws = pl.multiple_of(jnp.clip(q0 - N_SIDE, 0, length - _KW), N_SIDE)
    return q0, ws, jnp.where(i == 0, 0, jnp.where(i == n_blocks - 1, 2, 1))


_FREE_STRIDE = 4


def _residues_per_step(dilation):
    return max(dilation // _FREE_STRIDE, 1)


def _residue_views(dilation, seq, ins, outs, tmps):
    step = pl.program_id(1)
    if dilation <= _FREE_STRIDE:
        def rows(start, count, sub=0):
            return pl.ds(start, count) if dilation == 1 else pl.ds(start * dilation + step, count, stride=dilation)

        return ins, outs, rows, lambda: None
    inner = _residues_per_step(dilation)
    assert inner <= _FREE_STRIDE and len(tmps) == len(ins) + len(outs)
    coarse = pl.ds(step, seq // _FREE_STRIDE, stride=_FREE_STRIDE)
    in_tmps, out_tmps = tmps[:len(ins)], tmps[len(ins):]
    for ref, tmp in zip(ins, in_tmps):
        tmp[...] = ref[coarse, :]

    def flush():
        for ref, tmp in zip(outs, out_tmps):
            ref[coarse, :] = tmp[...]

    return in_tmps, out_tmps, lambda start, count, sub=0: pl.ds(start * inner + sub, count, stride=inner), flush


def _of_sub(ref, sub):
    return ref.at[sub] if len(ref.shape) == 3 else ref


def _attn_call(body, name, dilation, seq, n_in, out_dtypes, scratch, buffers):
    col = pl.BlockSpec((seq, _PAIR), lambda c, r: (0, c), pipeline_mode=pl.Buffered(buffers))
    tmps = [pltpu.VMEM((seq // _FREE_STRIDE, _PAIR), F32)] * (n_in + len(out_dtypes) if dilation > _FREE_STRIDE else 0)
    return pl.pallas_call(
        body, name=name, grid=(GROUP_DIM // _PAIR, dilation // _residues_per_step(dilation)),
        in_specs=[pl.BlockSpec(memory_space=pltpu.SMEM)] + [col] * n_in, out_specs=[col] * len(out_dtypes),
        out_shape=[jax.ShapeDtypeStruct((seq, GROUP_DIM), dt) for dt in out_dtypes], scratch_shapes=scratch + tmps,
        compiler_params=_params(dimension_semantics=("arbitrary", "arbitrary")))


def _staged(dilation, length, rows, sources, scratch):
    if dilation == 1:
        return sources
    for src, dst in zip(sources, scratch):
        for sub in range(_residues_per_step(dilation)):
            dst[sub] = src[rows(0, length, sub), :].astype(BF16)
    return scratch


def _sub_and_block(i, dilation, n_blocks):
    return (0, i) if _residues_per_step(dilation) == 1 else (i // n_blocks, i % n_blocks)


def _attn_fwd(q, k, v, slopes, dilation, name):
    seq = q.shape[0]
    length = seq // dilation
    n_blocks = length // _BQ
    n_stage = 0 if dilation == 1 else 3

    def body(sl_ref, q_ref, k_ref, v_ref, o_ref, lse_ref, *scratch):
        bias_ref, tmps = scratch[n_stage], scratch[n_stage + 1:]
        (q_in, k_in, v_in), (o_out, lse_out), rows, flush = _residue_views(dilation, seq, (q_ref, k_ref, v_ref), (o_ref, lse_ref), tmps)
        qs, ks, vs = _staged(dilation, length, rows, (q_in, k_in, v_in), scratch[:n_stage])
        _fill_bias(bias_ref, sl_ref, dilation)

        def block(i, carry):
            sub, j = _sub_and_block(i, dilation, n_blocks)
            q0, ws, which = _block_window(j, n_blocks, length)
            kw = _of_sub(ks, sub)[pl.ds(ws, _KW), :]
            vw = _of_sub(vs, sub)[pl.ds(ws, _KW), :]
            sc = _dot_nt(_stack_heads(_of_sub(qs, sub)[pl.ds(q0, _BQ), :]), kw) + bias_ref[which]
            m = jnp.max(sc, axis=-1, keepdims=True)
            p = jnp.exp(sc - m)
            den = jnp.sum(p, axis=-1, keepdims=True)
            o_out[rows(q0, _BQ, sub), :] = _unstack_heads(_dot(p.astype(BF16), vw) / den)
            lse_out[rows(q0, _BQ, sub), :] = _unstack_heads(jnp.broadcast_to(m + jnp.log(den), (2 * _BQ, _PAIR)))
            return carry

        lax.fori_loop(0, trips, block, 0, unroll=min(_ATTN_UNROLL, trips))
        flush()

    trips = _residues_per_step(dilation) * n_blocks
    stage = pltpu.VMEM((_residues_per_step(dilation), length, _PAIR), BF16)
    bias = pltpu.VMEM((3, 2 * _BQ, _KW), F32)
    return _attn_call(body, name, dilation, seq, 3, [F32, F32], [stage] * n_stage + [bias], 2)(slopes, q, k, v)


def _attn_bwd(q, k, v, do, lse, cterm, slopes, dilation, name):
    seq = q.shape[0]
    length = seq // dilation
    n_blocks = length // _BQ
    n_stage, n_whole = (0, 0) if dilation == 1 else (4, 3)

    def body(sl_ref, q_ref, k_ref, v_ref, do_ref, lse_ref, c_ref, dq_ref, dk_ref, dv_ref, *scratch):
        dk_acc, dv_acc, bias_ref = scratch[n_stage:n_stage + 3]
        whole, tmps = scratch[n_stage + 3:n_stage + 3 + n_whole], scratch[n_stage + 3 + n_whole:]
        (q_in, k_in, v_in, do_in, lse_in, c_in), (dq_out, dk_out, dv_out), rows, flush = _residue_views(
            dilation, seq, (q_ref, k_ref, v_ref, do_ref, lse_ref, c_ref), whole or (dq_ref, dk_ref, dv_ref), tmps)
        qs, ks, vs, dos = _staged(dilation, length, rows, (q_in, k_in, v_in, do_in), scratch[:n_stage])
        dk_acc[...] = jnp.zeros_like(dk_acc)
        dv_acc[...] = jnp.zeros_like(dv_acc)
        _fill_bias(bias_ref, sl_ref, dilation)

        def block(i, carry):
            sub, j = _sub_and_block(i, dilation, n_blocks)
            q0, ws, which = _block_window(j, n_blocks, length)
            qm = _stack_heads(_of_sub(qs, sub)[pl.ds(q0, _BQ), :])
            dom = _stack_heads(_of_sub(dos, sub)[pl.ds(q0, _BQ), :])
            kw = _of_sub(ks, sub)[pl.ds(ws, _KW), :]
            vw = _of_sub(vs, sub)[pl.ds(ws, _KW), :]
            p = jnp.exp(_dot_nt(qm, kw) + bias_ref[which] - _stack_cols(lse_in[rows(q0, _BQ, sub), :]))
            ds = (p * (_dot_nt(dom, vw) + _stack_cols(c_in[rows(q0, _BQ, sub), :]))).astype(BF16)
            dq_out[rows(q0, _BQ, sub), :] = (_unstack_heads(_dot(ds, kw)) * _SCORE_SCALE).astype(dq_out.dtype)
            dk_acc[sub, pl.ds(ws, _KW), :] += _dot_tn(ds, qm)
            dv_acc[sub, pl.ds(ws, _KW), :] += _dot_tn(p.astype(BF16), dom)
            return carry

        lax.fori_loop(0, trips, block, 0, unroll=min(_ATTN_UNROLL, trips))
        for sub in range(per):
            dk_out[rows(0, length, sub), :] = dk_acc[sub].astype(dk_out.dtype)
            dv_out[rows(0, length, sub), :] = dv_acc[sub].astype(dv_out.dtype)
        flush()
        if whole:
            @pl.when(pl.program_id(1) == dilation // per - 1)
            def _():
                for ref, collected in zip((dq_ref, dk_ref, dv_ref), whole):
                    ref[...] = collected[...].astype(BF16)

    per = _residues_per_step(dilation)
    trips = per * n_blocks
    stage = pltpu.VMEM((per, length, _PAIR), BF16)
    acc = pltpu.VMEM((per, length, _PAIR), F32)
    bias = pltpu.VMEM((3, 2 * _BQ, _KW), F32)
    collect = pltpu.VMEM((seq, _PAIR), F32)
    return _attn_call(body, name, dilation, seq, 6, [BF16] * 3, [stage] * n_stage + [acc] * 2 + [bias] + [collect] * n_whole,
                      2 if dilation == 1 else 1)(slopes, q, k, v, do, lse, cterm)


def _group_weights(lses):
    m = jnp.maximum(jnp.maximum(lses[0], lses[1]), lses[2])
    es = [jnp.exp(l - m) for l in lses]
    den = es[0] + es[1] + es[2]
    return [e / den for e in es]


def _out_fwd(a_pool, outs, lses, x, w_out, g, name, tm=1024):
    s, d = x.shape
    width = POOL_DIM + 3 * GROUP_DIM

    def body(ap_ref, o0, o1, o2, l0, l1, l2, x_ref, w_ref, g_ref, xo_ref, cat_ref):
        alphas = _group_weights([l0[...], l1[...], l2[...]])
        cat = jnp.concatenate([ap_ref[...]] + [(o[...] * al).astype(BF16) for o, al in zip((o0, o1, o2), alphas)], axis=1)
        cat_ref[...] = cat
        mix = _dot(cat, w_ref[...])
        xo_ref[...] = x_ref[...] + mix * _inv_rms(mix) * g_ref[...]

    return pl.pallas_call(
        body, name=name, grid=(s // tm,),
        in_specs=[_rows(tm, POOL_DIM)] + [_rows(tm, GROUP_DIM)] * 6 + [_rows(tm, d), _resident(w_out.shape), _const((1, d))],
        out_specs=[_rows(tm, d), _rows(tm, width)],
        out_shape=[jax.ShapeDtypeStruct((s, d), F32), jax.ShapeDtypeStruct((s, width), BF16)],
        compiler_params=_params(dimension_semantics=("arbitrary",)))(a_pool, *outs, *lses, x, w_out, g)


def _out_bwd(dxo, cat, outs, lses, w_out, g, head_ones, name, tm=1024):
    s, d = dxo.shape

    def body(dxo_ref, cat_ref, o0, o1, o2, l0, l1, l2, w_ref, g_ref, ones_ref, dpool_ref, dmix_ref, do0, do1, do2, c0, c1, c2, dg_ref):
        mv = _dot(cat_ref[...], w_ref[...])
        dmix, dg = _rms_bwd(mv, _inv_rms(mv), g_ref[...], dxo_ref[...])
        dmb = dmix.astype(BF16)
        dmix_ref[...] = dmb
        _accumulate(dg_ref, dg)
        dcat = _dot_nt(dmb, w_ref[...])
        dpool_ref[...] = dcat[:, :POOL_DIM]
        alphas = _group_weights([l0[...], l1[...], l2[...]])
        das = [dcat[:, POOL_DIM + GROUP_DIM * j:POOL_DIM + GROUP_DIM * (j + 1)] for j in range(3)]
        prod = sum(da * (o[...] * al) for da, o, al in zip(das, (o0, o1, o2), alphas))
        hi = prod.astype(BF16)
        lo = (prod - hi.astype(F32)).astype(BF16)
        total = _dot(hi, ones_ref[...]) + _dot(lo, ones_ref[...])
        for da, al, do_ref, c_ref in zip(das, alphas, (do0, do1, do2), (c0, c1, c2)):
            do_ref[...] = (da * al).astype(do_ref.dtype)
            c_ref[...] = -al * total

    return pl.pallas_call(
        body, name=name, grid=(s // tm,),
        in_specs=[_rows(tm, d), _rows(tm, cat.shape[1])] + [_rows(tm, GROUP_DIM)] * 6 + [_resident(w_out.shape), _const((1, d)),
                                                                                        _const((GROUP_DIM, GROUP_DIM))],
        out_specs=[_rows(tm, POOL_DIM), _rows(tm, d)] + [_rows(tm, GROUP_DIM)] * 6 + [_const((1, d))],
        out_shape=[jax.ShapeDtypeStruct((s, POOL_DIM), F32), jax.ShapeDtypeStruct((s, d), BF16)]
        + [jax.ShapeDtypeStruct((s, GROUP_DIM), _attn_dtype(dil)) for dil in DILATIONS]
        + [jax.ShapeDtypeStruct((s, GROUP_DIM), F32)] * 3 + [jax.ShapeDtypeStruct((1, d), F32)],
        compiler_params=_params(dimension_semantics=("arbitrary",)))(dxo, cat, *outs, *lses, w_out, g, head_ones)


def _alibi_slopes():
    return np.array([2.0 ** (-8.0 * (i + 1) / N_ATTN_HEADS) for i in range(N_ATTN_HEADS)], np.float32)


def _block_diag(w_lin):
    n, c, _ = w_lin.shape
    eye = jnp.eye(n, dtype=w_lin.dtype)
    return (eye[:, None, :, None] * w_lin[:, :, None, :]).reshape(n * c, n * c)


class _NoExchange:
    def __init__(self, full):
        self.full, self.grads = full, {}

    def first_weights(self):
        return self.full

    def riders(self, host):
        return []

    def landed(self, host, results):
        return self.full

    def gradient(self, name, grad):
        self.grads[name] = grad

    def small_gradients(self, packed):
        self.small_packed = packed


def _local_step(x, target, small, exchange):
    s, d = x.shape
    slopes = _alibi_slopes()
    group_slopes = [jnp.asarray(slopes[4 * g:4 * g + 4]) for g in range(3)]
    w_bd = _block_diag(small["w_pool_lin"]).astype(BF16)
    head_ones = jnp.asarray(np.kron(np.eye(GROUP_DIM // HEAD_DIM), np.ones((HEAD_DIM, HEAD_DIM))), BF16)

    full = dict(exchange.first_weights())

    def hosted(call, host, *args):
        results, riding = call(*args, host, exchange.riders(host))
        full.update(exchange.landed(host, riding) or {})
        return results

    x1, a1, b1, f1 = hosted(_ffn_fwd, "ffn1_fwd", x, small["g_ffn1_pre"], full["w1_gate"], full["w1_up"], full["w1_down"],
                            small["g_ffn1_post"], None)
    h2, u, *parts = hosted(_in_fwd, "in_fwd", x1, small["g_mix_pre"], full["w_in"])
    qs, ks, vs = parts[0:3], parts[3:6], parts[6:9]
    a_pool = _pool_fwd(u, w_bd, small["pool_scale"], "pool_fwd")
    outs, lses = [], []
    for g, dil in enumerate(DILATIONS):
        o, lse = _attn_fwd(qs[g], ks[g], vs[g], group_slopes[g], dil, f"attn_fwd{g}")
        outs.append(o)
        lses.append(lse)
    x2, cat = _out_fwd(a_pool, outs, lses, x1, full["w_out"], small["g_mix_post"], "out_fwd")
    (dx3, a2, b2, f2, loss_part), _ = _ffn_fwd(x2, small["g_ffn2_pre"], full["w2_gate"], full["w2_up"], full["w2_down"],
                                               small["g_ffn2_post"], target, "ffn2_fwd")

    small_grads = {}

    def ffn_backward(tag, dxo, x_in, f, a, b):
        n = tag[-1]
        dx, hh, da, db, df, h, dg_pre, dg_post = hosted(
            _ffn_bwd, f"{tag}_bwd", dxo, x_in, f, a, b, small[f"g_{tag}_pre"], small[f"g_{tag}_post"],
            full[f"w{n}_gate"], full[f"w{n}_up"], full[f"w{n}_down"])
        small_grads[f"g_{tag}_pre"], small_grads[f"g_{tag}_post"] = dg_pre, dg_post
        if len(small_grads) == len(SMALL):
            exchange.small_gradients(_pack_small(small_grads, loss_part[0, 0]))
        for part, lhs, rhs in (("down", hh, df), ("gate", da, h), ("up", db, h)):
            exchange.gradient(f"w{n}_{part}", hosted(_wgrad, f"{tag}_wgrad_{part}", lhs, rhs))
        return dx

    dx2 = ffn_backward("ffn2", dx3, x2, f2, a2, b2)
    dpool, dmix, *dos_cs, small_grads["g_mix_post"] = _out_bwd(dx2, cat, outs, lses, full["w_out"], small["g_mix_post"],
                                                               head_ones, "out_bwd")
    dos, cs = dos_cs[:3], dos_cs[3:]
    dqs, dks, dvs = [], [], []
    for g, dil in enumerate(DILATIONS):
        dq, dk, dv = _attn_bwd(qs[g], ks[g], vs[g], dos[g], lses[g], cs[g], group_slopes[g], dil, f"attn_bwd{g}")
        dqs.append(dq)
        dks.append(dk)
        dvs.append(dv)
    du, dw_bd, small_grads["pool_scale"] = _pool_bwd(u, dpool, w_bd, small["pool_scale"], "pool_bwd")
    n_pool = len(POOL_HALF_WINDOWS)
    small_grads["w_pool_lin"] = jnp.stack(
        [dw_bd[HEAD_DIM * g:HEAD_DIM * (g + 1), HEAD_DIM * g:HEAD_DIM * (g + 1)] for g in range(n_pool)])
    dz_parts = dqs + dks + dvs
    dx1, small_grads["g_mix_pre"] = hosted(_in_bwd, "in_bwd", du, dz_parts, x1, dx2, small["g_mix_pre"], full["w_in"])
    exchange.gradient("w_in", hosted(_wgrad_parts, "wgrad_in", [du] + dz_parts, h2))
    dx0 = ffn_backward("ffn1", dx1, x, f1, a1, b1)
    exchange.gradient("w_out", hosted(_wgrad, "wgrad_out", cat, dmix))
    return loss_part[0, 0], dx0, small_grads


SEGMENTS = ("w1_gate", "w1_up", "w1_down", "w_in", "w_out", "w2_gate", "w2_up", "w2_down")
TRANSPOSED = ("w1_gate", "w1_up", "w_in", "w2_gate", "w2_up")
ROWS_OUTSIDE = ("w1_gate", "w1_up", "w2_gate", "w2_up")
HALF = 512


def _place():
    x, y, c = lax.axis_index("x"), lax.axis_index("y"), lax.axis_index("c")
    other_chips = [(1 - x, y), (x, 1 - y), (1 - x, 1 - y)]
    return x, y, c, other_chips


def _chip_rows(chip, rows):
    return pl.ds(pl.multiple_of((2 * chip[0] + chip[1]) * rows, 16), rows)


def _cols(c):
    return pl.ds(pl.multiple_of(c * HALF, HALF), HALF)


def _cast_shards(shards, transposed, place, name):
    n = len(shards)
    rows = [w.shape[1] if t else w.shape[0] for w, t in zip(shards, transposed)]

    def body(place_ref, *refs):
        for w_ref, o_ref, t in zip(refs[:n], refs[n:], transposed):
            o_ref[...] = (w_ref[...].T if t else w_ref[...]).astype(BF16)

    once = pl.Buffered(1)
    return pl.pallas_call(
        body, name=name,
        grid_spec=pltpu.PrefetchScalarGridSpec(
            num_scalar_prefetch=1, grid=(1,),
            in_specs=[pl.BlockSpec(w.shape, lambda i, place: (0, 0), pipeline_mode=once) for w in shards],
            out_specs=[pl.BlockSpec((r, 1024), lambda i, place: (place[0], 0), pipeline_mode=once) for r in rows]),
        out_shape=[jax.ShapeDtypeStruct((N_CHIPS * r, 1024), BF16) for r in rows],
        compiler_params=_params(dimension_semantics=("arbitrary",)))(place, *shards)


def _gather_weights(bufs):
    n = len(bufs)
    rows = [b.shape[0] // N_CHIPS for b in bufs]

    def halves(r):
        first = -(-r // 32) * 16
        return (0, first), (first, r - first)

    def body(*refs):
        outs = refs[n:2 * n]
        ici_send, ici_recv, d2d_send, d2d_recv = refs[2 * n:]
        x, y, c, _ = _place()
        me, via_x, via_y, diagonal = (x, y), (1 - x, y), (x, 1 - y), (1 - x, 1 - y)

        def piece(chip, k, h, cols):
            start, size = halves(rows[k])[h]
            return outs[k].at[pl.ds(pl.multiple_of((2 * chip[0] + chip[1]) * rows[k] + start, 16), size), _cols(cols)]

        def ici(path, chip, k, h, to):
            blk = piece(chip, k, h, c)
            return pltpu.make_async_remote_copy(src_ref=blk, dst_ref=blk, send_sem=ici_send.at[path, k, h],
                                                recv_sem=ici_recv.at[path, k, h], device_id=(*to, c), device_id_type=MESH)

        def d2d(slot, chip, k, h, cols):
            blk = piece(chip, k, h, cols)
            return pltpu.make_async_remote_copy(src_ref=blk, dst_ref=blk, send_sem=d2d_send.at[slot, k, h],
                                                recv_sem=d2d_recv.at[slot, k, h], device_id=(x, y, 1 - c), device_id_type=MESH)

        started = [ici(0, me, k, h, via_x) for h in (0, 1) for k in range(n)] + [ici(1, me, k, h, via_y) for h in (1, 0) for k in range(n)]
        for cp in started:
            cp.start()

        def landed(path, slot, chip, k, h, pass_on_to=None):
            ici(path, chip, k, h, me).wait_recv()
            more = [d2d(slot, chip, k, h, c)] + ([ici(2, chip, k, h, pass_on_to)] if pass_on_to else [])
            for cp in more:
                cp.start()
            started.extend(more)

        for k in range(n):
            landed(0, 0, via_x, k, 0, pass_on_to=via_y)
            landed(1, 1, via_y, k, 1, pass_on_to=via_x)
        for k in range(n):
            landed(0, 0, via_x, k, 1)
            landed(1, 1, via_y, k, 0)
        for k in range(n):
            for h in range(2):
                landed(2, 2, diagonal, k, h)
        for slot, chip in enumerate((via_x, via_y, diagonal)):
            for k in range(n):
                for h in range(2):
                    d2d(slot, chip, k, h, 1 - c).wait_recv()
        for cp in started:
            cp.wait_send()

    any_spec = pl.BlockSpec(memory_space=pl.ANY)
    return pl.pallas_call(
        body, name="gather_weights", in_specs=[any_spec] * n, out_specs=[any_spec] * n,
        out_shape=[jax.ShapeDtypeStruct(b.shape, b.dtype) for b in bufs], input_output_aliases={k: k for k in range(n)},
        scratch_shapes=[pltpu.SemaphoreType.DMA((3, n, 2))] * 4)(*bufs)


def _gather_rider(bufs):
    n = len(bufs)
    rows = [b.shape[0] // N_CHIPS for b in bufs]

    def copies(outs, send_sems, recv_sems, inbound):
        x, y, c, chips = _place()
        for j, chip in enumerate(chips):
            for k in range(n):
                src_chip = chip if inbound else (x, y)
                blk = outs[k].at[_chip_rows(src_chip, rows[k]), _cols(c)]
                yield pltpu.make_async_remote_copy(src_ref=blk, dst_ref=blk, send_sem=send_sems.at[j, k], recv_sem=recv_sems.at[j, k],
                                                   device_id=(*chip, c), device_id_type=MESH)

    def start(ins, outs, send_sems, recv_sems):
        for cp in copies(outs, send_sems, recv_sems, False):
            cp.start()

    def wait(ins, outs, send_sems, recv_sems):
        for cp in copies(outs, send_sems, recv_sems, True):
            cp.wait_recv()
        for cp in copies(outs, send_sems, recv_sems, False):
            cp.wait_send()

    return _Rider(list(bufs), None, (3, n), start, wait)


def _forward_rider(bufs):
    n = len(bufs)
    rows = [b.shape[0] // N_CHIPS for b in bufs]

    def copies(outs, send_sems, recv_sems, half):
        x, y, c, chips = _place()
        for j, chip in enumerate(chips):
            for k in range(n):
                blk = outs[k].at[_chip_rows(chip, rows[k]), _cols(half(c))]
                yield pltpu.make_async_remote_copy(src_ref=blk, dst_ref=blk, send_sem=send_sems.at[j, k], recv_sem=recv_sems.at[j, k],
                                                   device_id=(x, y, 1 - c), device_id_type=MESH)

    def start(ins, outs, send_sems, recv_sems):
        for cp in copies(outs, send_sems, recv_sems, lambda c: c):
            cp.start()

    def wait(ins, outs, send_sems, recv_sems):
        for cp in copies(outs, send_sems, recv_sems, lambda c: 1 - c):
            cp.wait_recv()
        for cp in copies(outs, send_sems, recv_sems, lambda c: c):
            cp.wait_send()

    return _Rider(list(bufs), None, (3, n), start, wait)


def _sibling_rider(grads):
    n = len(grads)

    def copies(ins, outs, send_sems, recv_sems):
        x, y, c, _ = _place()
        return [pltpu.make_async_remote_copy(src_ref=ins[k].at[:, pl.ds(1 - c, 1)], dst_ref=outs[k], send_sem=send_sems.at[k],
                                             recv_sem=recv_sems.at[k], device_id=(x, y, 1 - c), device_id_type=MESH)
                for k in range(n)]

    def start(*refs):
        for cp in copies(*refs):
            cp.start()

    def wait(*refs):
        for cp in copies(*refs):
            cp.wait()

    return _Rider(list(grads), [jax.ShapeDtypeStruct((N_CHIPS, 1) + g.shape[2:], F32) for g in grads], (n,), start, wait)


def _alone(rider, name):
    n = len(rider.operands)
    landing = rider.landing if rider.landing is not None else [jax.ShapeDtypeStruct(a.shape, a.dtype) for a in rider.operands]
    n_out = len(landing)

    def body(*refs):
        rider.start(refs[:n], refs[n:n + n_out], *refs[n + n_out:])
        rider.wait(refs[:n], refs[n:n + n_out], *refs[n + n_out:])

    any_spec = pl.BlockSpec(memory_space=pl.ANY)
    return pl.pallas_call(body, name=name, in_specs=[any_spec] * n, out_specs=[any_spec] * n_out, out_shape=landing,
                          input_output_aliases={i: i for i in range(n)} if rider.landing is None else {},
                          scratch_shapes=[pltpu.SemaphoreType.DMA(rider.sems)] * 2)(*rider.operands)


def _chip_sum(grad, from_sibling, place, name):
    rh, width = grad.shape[2:]

    def body(place_ref, g_ref, s_ref, own_ref, all_ref):
        all_ref[...] = (g_ref[...] + s_ref[...]).astype(BF16)
        mine = place_ref[0]
        own_ref[0] = g_ref[mine, 0] + s_ref[mine, 0]

    blk = (N_CHIPS, 1, rh, width)
    once = pl.Buffered(1)
    return pl.pallas_call(
        body, name=name,
        grid_spec=pltpu.PrefetchScalarGridSpec(
            num_scalar_prefetch=1, grid=(1,),
            in_specs=[pl.BlockSpec(blk, lambda i, place: (0, place[1], 0, 0), pipeline_mode=once),
                      pl.BlockSpec(blk, lambda i, place: (0, 0, 0, 0), pipeline_mode=once)],
            out_specs=[pl.BlockSpec((1, rh, width), lambda i, place: (0, 0, 0), pipeline_mode=once),
                       pl.BlockSpec(blk, lambda i, place: (0, 0, 0, 0), pipeline_mode=once)]),
        out_shape=[jax.ShapeDtypeStruct((1, rh, width), F32), jax.ShapeDtypeStruct((N_CHIPS, 1, rh, width), BF16)],
        compiler_params=_params(dimension_semantics=("arbitrary",)))(place, grad, from_sibling)


def _scatter_rider(sums):
    n = len(sums)

    def copies(ins, outs, send_sems, recv_sems):
        x, y, c, chips = _place()
        return [pltpu.make_async_remote_copy(src_ref=ins[k].at[pl.ds(2 * chip[0] + chip[1], 1)], dst_ref=outs[k].at[pl.ds(j, 1)],
                                             send_sem=send_sems.at[j, k], recv_sem=recv_sems.at[j, k],
                                             device_id=(*chip, c), device_id_type=MESH)
                for j, chip in enumerate(chips) for k in range(n)]

    def start(*refs):
        for cp in copies(*refs):
            cp.start()

    def wait(*refs):
        for cp in copies(*refs):
            cp.wait()

    return _Rider(list(sums), [jax.ShapeDtypeStruct((3,) + sm.shape[1:], BF16) for sm in sums], (3, n), start, wait)


def _total_sums(owns, received, name):
    n = len(owns)

    def body(*refs):
        for o_ref, r_ref, t_ref in zip(refs[:n], refs[n:2 * n], refs[2 * n:]):
            total = o_ref[0]
            for j in range(3):
                total = total + r_ref[j, 0].astype(F32)
            t_ref[0] = total

    return _hosted_call(body, None, name=name, steps=1, in_specs=[_resident(a.shape) for a in owns + received],
                        out_specs=[_resident(o.shape) for o in owns], out_shape=[jax.ShapeDtypeStruct(o.shape, F32) for o in owns],
                        args=owns + received)[0]


def _swap_rider(halves):
    n = len(halves)

    def copies(ins, outs, send_sems, recv_sems):
        x, y, c, _ = _place()
        return [pltpu.make_async_remote_copy(src_ref=ins[k], dst_ref=outs[k], send_sem=send_sems.at[k], recv_sem=recv_sems.at[k],
                                             device_id=(x, y, 1 - c), device_id_type=MESH) for k in range(n)]

    def start(*refs):
        for cp in copies(*refs):
            cp.start()

    def wait(*refs):
        for cp in copies(*refs):
            cp.wait()

    return _Rider(list(halves), [jax.ShapeDtypeStruct(h.shape, F32) for h in halves], (n,), start, wait)


N_DEV = 8


def _small_rider(block):
    m_per, width = block.shape

    def copies(ins, outs, send_sems, recv_sems):
        (x_ref,), (out_ref,) = ins, outs
        x, y, c, chips = _place()
        me, sibling = (x, y, c), (x, y, 1 - c)

        def rows(px, py, pc):
            return out_ref.at[pl.ds((4 * px + 2 * py + pc) * m_per, m_per), :]

        def copy(k, blk, to, src=None):
            return pltpu.make_async_remote_copy(src_ref=rows(*blk) if src is None else src, dst_ref=rows(*blk),
                                                send_sem=send_sems.at[k], recv_sem=recv_sems.at[k], device_id=to, device_id_type=MESH)

        mine = pltpu.make_async_copy(x_ref, rows(*me), send_sems.at[7])
        first = [copy(0, me, sibling, src=x_ref)] + [copy(1 + j, me, (*chip, c), src=x_ref) for j, chip in enumerate(chips)]

        def second_hop():
            arriving = [copy(1 + j, (*chip, c), me) for j, chip in enumerate(chips)]
            passed = [copy(4 + j, (*chip, c), sibling) for j, chip in enumerate(chips)]
            last = [copy(0, sibling, me)] + [copy(4 + j, (*chip, 1 - c), me) for j, chip in enumerate(chips)]
            return arriving, passed, last

        return mine, first, second_hop

    def start(*refs):
        mine, first, _ = copies(*refs)
        mine.start()
        for cp in first:
            cp.start()

    def wait(*refs):
        mine, first, second_hop = copies(*refs)
        arriving, passed, last = second_hop()
        for arrived, onward in zip(arriving, passed):
            arrived.wait_recv()
            onward.start()
        for cp in last:
            cp.wait_recv()
        for cp in first + passed:
            cp.wait_send()
        mine.wait()

    return _Rider([block], [jax.ShapeDtypeStruct((N_DEV * m_per, width), F32)], (8,), start, wait)


def _adamw_math(w, g, m, v):
    m = ADAM_B1 * m + (1.0 - ADAM_B1) * g
    v = ADAM_B2 * v + (1.0 - ADAM_B2) * (g * g)
    m_hat = m / (1.0 - ADAM_B1 ** ADAM_STEP)
    v_hat = v / (1.0 - ADAM_B2 ** ADAM_STEP)
    delta = -ADAM_LR * (m_hat / (jnp.sqrt(v_hat) + ADAM_EPS) + ADAM_WD * w)
    return delta, m, v


_ADAMW_COLUMN_BLOCKS = 4


def _adamw(w, mine, siblings, place, m, v, transposed, name):
    rh, width = mine.shape[1:]
    cw = width // _ADAMW_COLUMN_BLOCKS
    place_spec = pl.BlockSpec(memory_space=pltpu.SMEM)
    out_shape = [jax.ShapeDtypeStruct(w.shape, F32)] * 4
    if transposed:
        def body(place_ref, w_ref, mine_ref, sib_ref, m_ref, v_ref, go_ref, d_ref, mo_ref, vo_ref):
            first = place_ref[1] == 0
            g = jnp.concatenate([jnp.where(first, mine_ref[0], sib_ref[0]), jnp.where(first, sib_ref[0], mine_ref[0])], axis=0).T
            go_ref[...] = g
            d_ref[...], mo_ref[...], vo_ref[...] = _adamw_math(w_ref[...], g, m_ref[...], v_ref[...])

        blk = _rows(cw, w.shape[1])
        halves = [pl.BlockSpec((1, rh, cw), lambda i: (0, 0, i))] * 2
        return _hosted_call(body, None, name=name, steps=_ADAMW_COLUMN_BLOCKS, in_specs=[place_spec, blk] + halves + [blk, blk],
                            out_specs=[blk] * 4, out_shape=out_shape, args=[place, w, mine, siblings, m, v])[0]

    def body(place_ref, w_ref, mine_ref, sib_ref, m_ref, v_ref, go_ref, d_ref, mo_ref, vo_ref):
        g = jnp.where(pl.program_id(0) % 2 == place_ref[1], mine_ref[0], sib_ref[0])
        go_ref[...] = g
        d_ref[...], mo_ref[...], vo_ref[...] = _adamw_math(w_ref[...], g, m_ref[...], v_ref[...])

    blk = pl.BlockSpec((rh, cw), lambda i: (i % 2, i // 2))
    halves = [pl.BlockSpec((1, rh, cw), lambda i: (0, 0, i // 2))] * 2
    return _hosted_call(body, None, name=name, steps=2 * _ADAMW_COLUMN_BLOCKS, in_specs=[place_spec, blk] + halves + [blk, blk],
                        out_specs=[blk] * 4, out_shape=out_shape, args=[place, w, mine, siblings, m, v])[0]


def _adamw_small(gathered, w, m, v, name):
    def body(ga_ref, w_ref, m_ref, v_ref, go_ref, d_ref, mo_ref, vo_ref):
        g = ga_ref[0]
        for dev in range(1, N_DEV):
            g = g + ga_ref[dev]
        go_ref[...] = g
        d_ref[...], mo_ref[...], vo_ref[...] = _adamw_math(w_ref[...], g, m_ref[...], v_ref[...])

    return pl.pallas_call(body, name=name, out_shape=[jax.ShapeDtypeStruct(w.shape, F32)] * 4,
                          compiler_params=_params())(gathered, w, m, v)


class _Exchange:
    FIRST = ("w1_gate", "w1_up", "w1_down")
    HOSTS = {"ffn2_wgrad_gate": (("w2_down",), ()), "ffn2_wgrad_up": (("w2_gate",), ("w2_down",)),
             "in_bwd": (("w2_up",), ("w2_gate",)), "wgrad_in": ((), ("w2_up",)),
             "ffn1_wgrad_down": ((), ("w_in",)), "ffn1_wgrad_gate": (("w1_down",), ()), "ffn1_wgrad_up": ((), ("w1_down", "w1_gate")),
             "wgrad_out": ((), ("w1_up",))}
    ALONE = ("w_in", "w1_gate", "w1_up", "w_out")
    SWAP_HOST = "wgrad_out"
    SMALL_HOST = "ffn1_wgrad_gate"

    def __init__(self, bufs, place):
        self.bufs, self.place = bufs, place
        self.later = [k for k in SEGMENTS if k not in self.FIRST]
        self.split, self.own, self.to_send, self.received = {}, {}, {}, {}

    def first_weights(self):
        return dict(zip(self.FIRST, _gather_weights([self.bufs[k] for k in self.FIRST])))

    def riders(self, host):
        if host == "ffn1_fwd":
            return [_gather_rider([self.bufs[k] for k in self.later])]
        if host == "in_fwd":
            return [_forward_rider([self.bufs[k] for k in self.later[1:]])]
        halves, sums = self.HOSTS.get(host, ((), ()))
        riders = ([_sibling_rider([self.split[k] for k in halves])] if halves else []) + (
            [_scatter_rider([self.to_send[k] for k in sums])] if sums else [])
        if host == self.SWAP_HOST:
            self.early = [k for k in SEGMENTS if k in self.received]
            self.mine = dict(zip(self.early, self._totals(self.early, "total_sums_early")))
            riders.append(_swap_rider([self.mine[k] for k in self.early]))
        if host == self.SMALL_HOST:
            riders.append(_small_rider(self.small_packed))
        return riders

    def small_gradients(self, packed):
        self.small_packed = packed

    def landed(self, host, results):
        if host == "ffn1_fwd":
            self.bufs.update(zip(self.later, results[0]))
            return dict(zip(self.later[:1], _alone(_forward_rider([self.bufs[self.later[0]]]), "gather_forward_first")))
        if host == "in_fwd":
            return dict(zip(self.later[1:], results[0]))
        results = list(results)
        if host == self.SWAP_HOST:
            self.siblings = dict(zip(self.early, results.pop()))
        if host == self.SMALL_HOST:
            (self.small_gathered,) = results.pop()
        halves, sums = self.HOSTS.get(host, ((), ()))
        if halves:
            self._chip_sums(halves, results[0])
        if sums:
            self.received.update(zip(sums, results[-1]))

    def gradient(self, name, grad):
        self.split[name] = grad.reshape(N_CHIPS, 2, grad.shape[0] // (2 * N_CHIPS), grad.shape[1])
        if name in self.ALONE:
            self._chip_sums([name], _alone(_sibling_rider([self.split[name]]), f"reduce_sibling_{name}"))

    def _chip_sums(self, names, from_sibling):
        for k, fs in zip(names, from_sibling):
            self.own[k], self.to_send[k] = _chip_sum(self.split[k], fs, self.place, f"chip_sum_{k}")

    def _totals(self, names, call_name):
        return _total_sums([self.own[k] for k in names], [self.received[k] for k in names], call_name)

    def summed_halves(self):
        late = [k for k in SEGMENTS if k not in self.received]
        self.received.update(zip(late, _alone(_scatter_rider([self.to_send[k] for k in late]), "reduce_chips_last")))
        rest = [k for k in SEGMENTS if k not in self.early]
        mine = self._totals(rest, "total_sums")
        self.mine.update(zip(rest, mine))
        self.siblings.update(zip(rest, _alone(_swap_rider(mine), "swap_halves")))
        return [self.mine[k] for k in SEGMENTS], [self.siblings[k] for k in SEGMENTS]


SMALL = ("g_ffn1_pre", "g_ffn1_post", "g_mix_pre", "w_pool_lin", "pool_scale", "g_mix_post", "g_ffn2_pre", "g_ffn2_post")
WEIGHTS = ("g_ffn1_pre", "w1_gate", "w1_up", "w1_down", "g_ffn1_post", "g_mix_pre", "w_in", "w_pool_lin", "pool_scale", "w_out",
           "g_mix_post", "g_ffn2_pre", "w2_gate", "w2_up", "w2_down", "g_ffn2_post")
LANES = 128


def _pack_small(tree, extra=0.0):
    flat = jnp.concatenate([tree[k].reshape(-1) for k in SMALL] + [jnp.reshape(extra, (1,)).astype(F32)])
    rows = -(-flat.shape[0] // (8 * LANES)) * 8
    return jnp.pad(flat, (0, rows * LANES - flat.shape[0])).reshape(rows, LANES)


def _unpack_small(packed, like):
    flat, out, at = packed.reshape(-1), {}, 0
    for k in SMALL:
        size = math.prod(like[k].shape)
        out[k] = flat[at:at + size].reshape(like[k].shape)
        at += size
    return out


def kernel(x, g_ffn1_pre, w1_gate, w1_up, w1_down, g_ffn1_post, g_mix_pre, w_in, w_pool_lin, pool_scale, w_out, g_mix_post, g_ffn2_pre, w2_gate, w2_up, w2_down, g_ffn2_post, loss_target, m_g_ffn1_pre, m_w1_gate, m_w1_up, m_w1_down, m_g_ffn1_post, m_g_mix_pre, m_w_in, m_w_pool_lin, m_pool_scale, m_w_out, m_g_mix_post, m_g_ffn2_pre, m_w2_gate, m_w2_up, m_w2_down, m_g_ffn2_post, v_g_ffn1_pre, v_w1_gate, v_w1_up, v_w1_down, v_g_ffn1_post, v_g_mix_pre, v_w_in, v_w_pool_lin, v_pool_scale, v_w_out, v_g_mix_post, v_g_ffn2_pre, v_w2_gate, v_w2_up, v_w2_down, v_g_ffn2_post):
    given = dict(locals())
    w = {k: given[k] for k in WEIGHTS}
    m = {k: given["m_" + k] for k in WEIGHTS}
    v = {k: given["v_" + k] for k in WEIGHTS}
    small = {k: (w[k][0] if k == "w_pool_lin" else w[k].reshape(1, -1)) for k in SMALL}

    place = jnp.stack([2 * lax.axis_index("x") + lax.axis_index("y"), lax.axis_index("c")]).astype(jnp.int32)
    def as_rows(a, k):
        return jnp.swapaxes(a, 1, 2)[0] if k in ROWS_OUTSIDE else a[0]

    def as_given(a, k):
        return jnp.swapaxes(a[None], 1, 2) if k in ROWS_OUTSIDE else a[None]

    in_kernel = [k for k in TRANSPOSED if k not in ROWS_OUTSIDE]
    bufs = {}
    for tag, names in (("first", _Exchange.FIRST), ("rest", [k for k in SEGMENTS if k not in _Exchange.FIRST])):
        bufs.update(zip(names, _cast_shards([as_rows(w[k], k) for k in names], [k in in_kernel for k in names], place, f"cast_{tag}")))
    exchange = _Exchange(bufs, place)
    _, grad_x, _ = _local_step(x[0], loss_target[0], small, exchange)

    out_grad, out_delta, out_m, out_v = {}, {}, {}, {}
    for k, mine, siblings in zip(SEGMENTS, *exchange.summed_halves()):
        results = _adamw(as_rows(w[k], k), mine, siblings, place, as_rows(m[k], k), as_rows(v[k], k), k in in_kernel, f"adamw_{k}")
        out_grad[k], out_delta[k], out_m[k], out_v[k] = (as_given(a, k) for a in results)

    gathered = exchange.small_gathered.reshape(N_DEV, -1, LANES)
    like = {k: w[k] for k in SMALL}
    results = _adamw_small(gathered, _pack_small(like), _pack_small({k: m[k] for k in SMALL}),
                           _pack_small({k: v[k] for k in SMALL}), "adamw_small")
    for tree, res in zip((out_grad, out_delta, out_m, out_v), results):
        tree.update(_unpack_small(res, like))
    loss = results[0].reshape(-1)[sum(math.prod(like[k].shape) for k in SMALL)]

    return (loss, grad_x[None], *[out_grad[k] for k in WEIGHTS], *[out_delta[k] for k in WEIGHTS],
            *[out_m[k] for k in WEIGHTS], *[out_v[k] for k in WEIGHTS])
```

```python
import math
import typing

import numpy as np
import jax
import jax.numpy as jnp
from jax import lax
from jax.experimental import pallas as pl
from jax.experimental.pallas import tpu as pltpu

F32 = jnp.float32
BF16 = jnp.bfloat16
MESH = pl.DeviceIdType.MESH

RMS_EPS = 1e-6
HEAD_DIM = 64
POOL_HALF_WINDOWS = (1, 2, 4, 8)
POOL_DIM = 256
GROUP_DIM = 256
DILATIONS = (1, 4, 16)
N_SIDE = 64
N_ATTN_HEADS = 12
ADAM_LR, ADAM_B1, ADAM_B2, ADAM_EPS, ADAM_WD, ADAM_STEP = 0.001, 0.9, 0.999, 1e-08, 0.01, 10

N_CHIPS = 4
V7X_VMEM_LIMIT = 60 * 1024 * 1024

_NT = (((1,), (1,)), ((), ()))
_TN = (((0,), (0,)), ((), ()))


def _dot(a, b):
    return jnp.dot(a, b, preferred_element_type=F32)


def _dot_nt(a, b):
    return lax.dot_general(a, b, _NT, preferred_element_type=F32)


def _dot_tn(a, b):
    return lax.dot_general(a, b, _TN, preferred_element_type=F32)


def _params(**kw):
    return pltpu.CompilerParams(vmem_limit_bytes=V7X_VMEM_LIMIT, **kw)


def _rows(tm, width):
    return pl.BlockSpec((tm, width), lambda i: (i, 0))


def _resident(shape):
    return pl.BlockSpec(shape, lambda i: (0,) * len(shape), pipeline_mode=pl.Buffered(1))


def _const(shape):
    return pl.BlockSpec(shape, lambda i: (0,) * len(shape))


def _inv_rms(x):
    return lax.rsqrt(jnp.mean(x * x, axis=-1, keepdims=True) + RMS_EPS)


def _rms_bwd(x, inv, g, dy):
    n = x * inv
    dn = dy * g
    dx = inv * (dn - n * jnp.mean(dn * n, axis=-1, keepdims=True))
    return dx, jnp.sum(dy * n, axis=0, keepdims=True)


def _accumulate(ref, value):
    @pl.when(pl.program_id(0) == 0)
    def _():
        ref[...] = jnp.zeros_like(ref)

    ref[...] += value


class _Rider(typing.NamedTuple):
    operands: list
    landing: typing.Optional[list]
    sems: tuple
    start: typing.Callable
    wait: typing.Callable


def _hosted_call(body, riders, *, name, steps, in_specs, out_specs, out_shape, args, scratch_shapes=()):
    params = _params(dimension_semantics=("arbitrary",))
    riders = list(riders or [])
    if not riders:
        res = pl.pallas_call(body, name=name, grid=(steps,), in_specs=in_specs, out_specs=out_specs, out_shape=out_shape,
                             scratch_shapes=list(scratch_shapes), compiler_params=params)(*args)
        return list(res), []
    n_in, n_out, n_scratch = len(in_specs), len(out_specs), len(scratch_shapes)
    operands, landing, aliases, spans = [], [], {}, []
    for rd in riders:
        lands = rd.landing if rd.landing is not None else [jax.ShapeDtypeStruct(a.shape, a.dtype) for a in rd.operands]
        if rd.landing is None:
            aliases.update({n_in + len(operands) + i: n_out + len(landing) + i for i in range(len(lands))})
        spans.append((len(operands), len(rd.operands), len(landing), len(lands)))
        operands += rd.operands
        landing += lands
    outs_at = n_in + len(operands)
    scratch_at = outs_at + n_out + len(landing)

    def riding(*refs):
        def each(action):
            for i, (rd, (in_at, n_ops, out_at, n_lands)) in enumerate(zip(riders, spans)):
                sems = refs[scratch_at + n_scratch + 2 * i:scratch_at + n_scratch + 2 * i + 2]
                getattr(rd, action)(refs[n_in + in_at:n_in + in_at + n_ops],
                                    refs[outs_at + n_out + out_at:outs_at + n_out + out_at + n_lands], *sems)

        @pl.when(pl.program_id(0) == 0)
        def _():
            each("start")

        body(*refs[:n_in], *refs[outs_at:outs_at + n_out], *refs[scratch_at:scratch_at + n_scratch])

        @pl.when(pl.program_id(0) == steps - 1)
        def _():
            each("wait")

    any_spec = pl.BlockSpec(memory_space=pl.ANY)
    res = pl.pallas_call(
        riding, name=name, grid=(steps,), in_specs=list(in_specs) + [any_spec] * len(operands),
        out_specs=list(out_specs) + [any_spec] * len(landing), out_shape=list(out_shape) + landing,
        scratch_shapes=list(scratch_shapes) + [pltpu.SemaphoreType.DMA(rd.sems) for rd in riders for _ in range(2)],
        input_output_aliases=aliases, compiler_params=params)(*args, *operands)
    return list(res[:n_out]), [list(res[n_out + out_at:n_out + out_at + n_lands]) for _, _, out_at, n_lands in spans]


_SUB_TILE = 256


def _sub_tiles(tm):
    return [pl.ds(r, _SUB_TILE) for r in range(0, tm, _SUB_TILE)]


def _ffn_fwd(x, g_pre, wg_t, wu_t, wd, g_post, target, name, riders=None, tm=512):
    s, d = x.shape
    ff = wd.shape[0]
    with_loss = target is not None

    def body(*refs):
        if with_loss:
            x_ref, gpre_ref, wg_ref, wu_ref, wd_ref, gpost_ref, t_ref, xo_ref, a_ref, b_ref, f_ref, loss_ref = refs
        else:
            x_ref, gpre_ref, wg_ref, wu_ref, wd_ref, gpost_ref, xo_ref, a_ref, b_ref, f_ref = refs
        loss = 0.0
        for rows in _sub_tiles(tm):
            xv = x_ref[rows, :]
            hb = (xv * _inv_rms(xv) * gpre_ref[...]).astype(BF16)
            a = _dot_nt(hb, wg_ref[...])
            b = _dot_nt(hb, wu_ref[...])
            hh = (a * jax.nn.sigmoid(a)) * b
            f = _dot(hh.astype(BF16), wd_ref[...])
            xo = xv + 0.5 * (f * _inv_rms(f) * gpost_ref[...])
            a_ref[rows, :] = a.astype(BF16)
            b_ref[rows, :] = b.astype(BF16)
            f_ref[rows, :] = f
            if with_loss:
                e = xo - t_ref[rows, :]
                xo_ref[rows, :] = e * (1.0 / d)
                loss = loss + 0.5 * jnp.sum(jnp.mean(e * e, axis=-1, keepdims=True))
            else:
                xo_ref[rows, :] = xo
        if with_loss:
            _accumulate(loss_ref, loss)

    in_specs = [_rows(tm, d), _const((1, d)), _resident((ff, d)), _resident((ff, d)), _resident((ff, d)), _const((1, d))]
    args = [x, g_pre, wg_t, wu_t, wd, g_post]
    out_shape = [jax.ShapeDtypeStruct((s, d), F32), jax.ShapeDtypeStruct((s, ff), BF16),
                 jax.ShapeDtypeStruct((s, ff), BF16), jax.ShapeDtypeStruct((s, d), F32)]
    out_specs = [_rows(tm, d), _rows(tm, ff), _rows(tm, ff), _rows(tm, d)]
    if with_loss:
        in_specs.append(_rows(tm, d))
        args.append(target)
        out_shape.append(jax.ShapeDtypeStruct((8, 128), F32))
        out_specs.append(_const((8, 128)))
    return _hosted_call(body, riders, name=name, steps=s // tm, in_specs=in_specs, out_specs=out_specs, out_shape=out_shape, args=args)


def _ffn_bwd(dxo, x, f, a, b, g_pre, g_post, wg_t, wu_t, wd, name, riders=None, tm=256):
    s, d = x.shape
    ff = wd.shape[0]

    def body(dxo_ref, x_ref, f_ref, a_ref, b_ref, gpre_ref, gpost_ref, wg_ref, wu_ref, wd_ref,
             dx_ref, hh_ref, da_ref, db_ref, df_ref, h_ref, dgpre_ref, dgpost_ref):
        dgpre_sum = dgpost_sum = 0.0
        for rows in _sub_tiles(tm):
            dxo_v = dxo_ref[rows, :]
            fv = f_ref[rows, :]
            df, dgpost = _rms_bwd(fv, _inv_rms(fv), gpost_ref[...], 0.5 * dxo_v)
            dfb = df.astype(BF16)
            dhh = _dot_nt(dfb, wd_ref[...])
            av = a_ref[rows, :].astype(F32)
            bv = b_ref[rows, :].astype(F32)
            sig = jax.nn.sigmoid(av)
            sa = av * sig
            da = (dhh * bv * (sig * (1.0 + av * (1.0 - sig)))).astype(BF16)
            db = (dhh * sa).astype(BF16)
            dh = _dot(da, wg_ref[...]) + _dot(db, wu_ref[...])
            xv = x_ref[rows, :]
            inv = _inv_rms(xv)
            dxn, dgpre = _rms_bwd(xv, inv, gpre_ref[...], dh)
            dx_ref[rows, :] = dxo_v + dxn
            hh_ref[rows, :] = (sa * bv).astype(BF16)
            da_ref[rows, :] = da
            db_ref[rows, :] = db
            df_ref[rows, :] = dfb
            h_ref[rows, :] = (xv * inv * gpre_ref[...]).astype(BF16)
            dgpre_sum, dgpost_sum = dgpre_sum + dgpre, dgpost_sum + dgpost
        _accumulate(dgpre_ref, dgpre_sum)
        _accumulate(dgpost_ref, dgpost_sum)

    return _hosted_call(
        body, riders, name=name, steps=s // tm,
        in_specs=[_rows(tm, d), _rows(tm, d), _rows(tm, d), _rows(tm, ff), _rows(tm, ff), _const((1, d)), _const((1, d)),
                  _resident((ff, d)), _resident((ff, d)), _resident((ff, d))],
        out_specs=[_rows(tm, d), _rows(tm, ff), _rows(tm, ff), _rows(tm, ff), _rows(tm, d), _rows(tm, d),
                   _const((1, d)), _const((1, d))],
        out_shape=[jax.ShapeDtypeStruct((s, d), F32), jax.ShapeDtypeStruct((s, ff), BF16), jax.ShapeDtypeStruct((s, ff), BF16),
                   jax.ShapeDtypeStruct((s, ff), BF16), jax.ShapeDtypeStruct((s, d), BF16), jax.ShapeDtypeStruct((s, d), BF16),
                   jax.ShapeDtypeStruct((1, d), F32), jax.ShapeDtypeStruct((1, d), F32)],
        args=[dxo, x, f, a, b, g_pre, g_post, wg_t, wu_t, wd])


def _wgrad(lhs, rhs, name, riders=None, rt=256):
    s, r = lhs.shape
    c = rhs.shape[1]

    def body(l_ref, r_ref, o_ref, o16_ref):
        block = _dot_tn(l_ref[...], r_ref[...])
        o_ref[...] = block
        o16_ref[...] = block.astype(BF16)

    outs, riding = _hosted_call(
        body, riders, name=name, steps=pl.cdiv(r, rt), in_specs=[pl.BlockSpec((s, rt), lambda i: (0, i)), _resident((s, c))],
        out_specs=[pl.BlockSpec((rt, c), lambda i: (i, 0))] * 2, out_shape=[jax.ShapeDtypeStruct((r, c), dt) for dt in (F32, BF16)],
        args=[lhs, rhs])
    return outs, riding


def _attn_dtype(dilation):
    return BF16 if dilation == 1 else F32


def _in_fwd(x, g, w_in_t, name, riders=None, tm=1024):
    s, d = x.shape
    d_in = w_in_t.shape[0]
    n_groups = len(DILATIONS)
    dtypes = [_attn_dtype(dil) for dil in DILATIONS] * 3

    def body(x_ref, g_ref, w_ref, h_ref, u_ref, *part_refs):
        xv = x_ref[...]
        hb = (xv * _inv_rms(xv) * g_ref[...]).astype(BF16)
        h_ref[...] = hb
        z = _dot_nt(hb, w_ref[...])
        u_ref[...] = z[:, :POOL_DIM]
        for j, ref in enumerate(part_refs):
            part = z[:, POOL_DIM + GROUP_DIM * j:POOL_DIM + GROUP_DIM * (j + 1)]
            ref[...] = (part * _SCORE_SCALE if j < n_groups else part).astype(ref.dtype)

    return _hosted_call(
        body, riders, name=name, steps=s // tm, in_specs=[_rows(tm, d), _const((1, d)), _resident((d_in, d))],
        out_specs=[_rows(tm, d), _rows(tm, POOL_DIM)] + [_rows(tm, GROUP_DIM)] * len(dtypes),
        out_shape=[jax.ShapeDtypeStruct((s, d), BF16), jax.ShapeDtypeStruct((s, POOL_DIM), F32)]
        + [jax.ShapeDtypeStruct((s, GROUP_DIM), dt) for dt in dtypes],
        args=[x, g, w_in_t])


def _in_bwd(du, dparts, x, dxo, g, w_in_t, name, riders=None, tm=512):
    s, d = x.shape
    d_in = w_in_t.shape[0]
    n_parts = len(dparts)

    def body(du_ref, *refs):
        part_refs = refs[:n_parts]
        x_ref, dxo_ref, g_ref, w_ref, dx_ref, dg_ref = refs[n_parts:]
        dh = _dot(jnp.concatenate([r[...] for r in (du_ref,) + part_refs], axis=1), w_ref[...])
        xv = x_ref[...]
        dxn, dg = _rms_bwd(xv, _inv_rms(xv), g_ref[...], dh)
        dx_ref[...] = dxo_ref[...] + dxn
        _accumulate(dg_ref, dg)

    return _hosted_call(
        body, riders, name=name, steps=s // tm,
        in_specs=[_rows(tm, POOL_DIM)] + [_rows(tm, GROUP_DIM)] * n_parts + [_rows(tm, d), _rows(tm, d), _const((1, d)),
                                                                             _resident((d_in, d))],
        out_specs=[_rows(tm, d), _const((1, d))],
        out_shape=[jax.ShapeDtypeStruct((s, d), F32), jax.ShapeDtypeStruct((1, d), F32)],
        args=[du, *dparts, x, dxo, g, w_in_t])


def _wgrad_parts(parts, rhs, name, riders=None):
    n = len(parts)
    s, rt = parts[0].shape
    c = rhs.shape[1]

    def body(*refs):
        part_refs, r_ref, o_ref, o16_ref, buf, sems = refs[:n], refs[n], refs[n + 1], refs[n + 2], refs[n + 3], refs[n + 4]

        def fetch(i):
            return pltpu.make_async_copy(part_refs[i], buf.at[i % 2], sems.at[i % 2])

        fetch(0).start()
        for i in range(n):
            if i + 1 < n:
                fetch(i + 1).start()
            fetch(i).wait()
            block = _dot_tn(buf[i % 2], r_ref[...])
            o_ref[pl.ds(i * rt, rt), :] = block
            o16_ref[pl.ds(i * rt, rt), :] = block.astype(BF16)

    outs, riding = _hosted_call(
        body, riders, name=name, steps=1, in_specs=[pl.BlockSpec(memory_space=pl.ANY)] * n + [_resident((s, c))],
        out_specs=[_resident((n * rt, c))] * 2, out_shape=[jax.ShapeDtypeStruct((n * rt, c), dt) for dt in (F32, BF16)],
        args=[*parts, rhs], scratch_shapes=[pltpu.VMEM((2, s, rt), BF16), pltpu.SemaphoreType.DMA((2,))])
    return outs, riding


_POOL_HALO = 8


def _pool_chain(v, first_shift):
    n = v.shape[0]
    p2 = v + pltpu.roll(v, first_shift, 0)
    p4 = pltpu.roll(p2, 1, 0) + pltpu.roll(p2, n - 1, 0)
    p8 = pltpu.roll(p4, 2, 0) + pltpu.roll(p4, n - 2, 0)
    p16 = pltpu.roll(p8, 4, 0) + pltpu.roll(p8, n - 4, 0)
    group = lax.broadcasted_iota(jnp.int32, v.shape, 1) // HEAD_DIM
    return jnp.where(group == 0, p2, jnp.where(group == 1, p4, jnp.where(group == 2, p8, p16)))


def _pool_count(t0, rows, s):
    t = t0 + lax.broadcasted_iota(jnp.int32, (rows, POOL_DIM), 0)
    group = lax.broadcasted_iota(jnp.int32, (rows, POOL_DIM), 1) // HEAD_DIM
    half = jnp.where(group == 0, 1, jnp.where(group == 1, 2, jnp.where(group == 2, 4, 8)))
    cnt = jnp.minimum(t + half, s) - jnp.maximum(t - half, 0)
    return jnp.maximum(cnt, 1).astype(F32)


def _pad_rows(ref, pad_ref, s):
    zeros = jnp.zeros((_POOL_HALO, pad_ref.shape[1]), pad_ref.dtype)
    pad_ref[pl.ds(0, _POOL_HALO), :] = zeros
    pad_ref[pl.ds(_POOL_HALO + s, _POOL_HALO), :] = zeros
    pad_ref[pl.ds(_POOL_HALO, s), :] = ref[...]


def _pool_fwd(u, w_bd, scale, name, tm=512):
    s = u.shape[0]
    ext = tm + 2 * _POOL_HALO

    def body(u_ref, w_ref, sc_ref, o_ref, upad):
        _pad_rows(u_ref, upad, s)

        def tile(i, carry):
            t0 = pl.multiple_of(i * tm, tm)
            uv = upad[pl.ds(t0, ext), :]
            win = _pool_chain(uv, 1)[_POOL_HALO:_POOL_HALO + tm]
            y = win / _pool_count(t0, tm, s) - uv[_POOL_HALO:_POOL_HALO + tm]
            o_ref[pl.ds(t0, tm), :] = (_dot(y.astype(BF16), w_ref[...]) * sc_ref[...]).astype(BF16)
            return carry

        lax.fori_loop(0, s // tm, tile, 0)

    return pl.pallas_call(body, name=name, out_shape=jax.ShapeDtypeStruct((s, POOL_DIM), BF16),
                          scratch_shapes=[pltpu.VMEM((s + 2 * _POOL_HALO, POOL_DIM), F32)],
                          compiler_params=_params())(u, w_bd, scale)


def _pool_bwd(u, da, w_bd, scale, name, tm=512):
    s = u.shape[0]
    ext = tm + 2 * _POOL_HALO

    def body(u_ref, da_ref, w_ref, sc_ref, du_ref, dw_ref, dsc_ref, upad, dapad):
        _pad_rows(u_ref, upad, s)
        _pad_rows(da_ref, dapad, s)
        dw_ref[...] = jnp.zeros_like(dw_ref)
        dsc_ref[...] = jnp.zeros_like(dsc_ref)

        def tile(i, carry):
            t0 = pl.multiple_of(i * tm, tm)
            uv = upad[pl.ds(t0, ext), :]
            dav = dapad[pl.ds(t0, ext), :]
            win = _pool_chain(uv, 1)[_POOL_HALO:_POOL_HALO + tm]
            yb = (win / _pool_count(t0, tm, s) - uv[_POOL_HALO:_POOL_HALO + tm]).astype(BF16)
            yl = _dot(yb, w_ref[...])
            da_c = dav[_POOL_HALO:_POOL_HALO + tm]
            dsc_ref[...] += jnp.sum(da_c * yl, axis=0, keepdims=True)
            dyl = (dav * sc_ref[...]).astype(BF16)
            dw_ref[...] += _dot_tn(yb, dyl[_POOL_HALO:_POOL_HALO + tm])
            dy = _dot_nt(dyl, w_ref[...])
            dyc = dy / _pool_count(t0 - _POOL_HALO, ext, s)
            du_ref[pl.ds(t0, tm), :] = (_pool_chain(dyc, ext - 1) - dy)[_POOL_HALO:_POOL_HALO + tm].astype(BF16)
            return carry

        lax.fori_loop(0, s // tm, tile, 0)

    pool_cols = pl.BlockSpec((s, POOL_DIM), lambda i: (0, 0), pipeline_mode=pl.Buffered(1))
    return pl.pallas_call(
        body, name=name, grid=(1,),
        in_specs=[pool_cols, pool_cols, _const((POOL_DIM, POOL_DIM)), _const((1, POOL_DIM))],
        out_specs=[_const((s, POOL_DIM)), _const((POOL_DIM, POOL_DIM)), _const((1, POOL_DIM))],
        out_shape=[jax.ShapeDtypeStruct((s, POOL_DIM), BF16), jax.ShapeDtypeStruct((POOL_DIM, POOL_DIM), F32),
                   jax.ShapeDtypeStruct((1, POOL_DIM), F32)],
        scratch_shapes=[pltpu.VMEM((s + 2 * _POOL_HALO, POOL_DIM), F32), pltpu.VMEM((s + 2 * _POOL_HALO, POOL_DIM), F32)],
        compiler_params=_params(dimension_semantics=("arbitrary",)))(u, da, w_bd, scale)


_BQ = 128
_KW = _BQ + 2 * N_SIDE
_PAIR = 2 * HEAD_DIM
_NEG = -1e30
_ATTN_UNROLL = 8
_SCORE_SCALE = HEAD_DIM ** -0.5


def _stack_heads(x):
    lane_head = lax.broadcasted_iota(jnp.int32, x.shape, 1) // HEAD_DIM
    zero = jnp.zeros_like(x)
    return jnp.concatenate([jnp.where(lane_head == 0, x, zero), jnp.where(lane_head == 1, x, zero)], axis=0)


def _unstack_heads(x):
    lane_head = lax.broadcasted_iota(jnp.int32, (_BQ, _PAIR), 1) // HEAD_DIM
    return jnp.where(lane_head == 0, x[:_BQ], x[_BQ:])


def _stack_cols(x):
    return jnp.concatenate([x[:, 0:1], x[:, HEAD_DIM:HEAD_DIM + 1]], axis=0)


def _fill_bias(bias_ref, slopes_ref, dilation):
    row = lax.broadcasted_iota(jnp.int32, (2 * _BQ, _KW), 0)
    col = lax.broadcasted_iota(jnp.int32, (2 * _BQ, _KW), 1)
    pair = 2 * pl.program_id(0)
    slope = jnp.where(row < _BQ, slopes_ref[pair], slopes_ref[pair + 1]) * float(dilation)

    @pl.when(pl.program_id(1) == 0)
    def _():
        for j in range(3):
            dist = jnp.abs(col - (row & (_BQ - 1)) - j * N_SIDE)
            bias_ref[j] = jnp.where(dist <= N_SIDE, -slope * dist.astype(F32), _NEG)


def _block_window(i, n_blocks, length):
    q0 = pl.multiple_of(i * _BQ, _BQ)
    ws = pl.multiple_of(jnp.clip(q0 - N_SIDE, 0, length - _KW), N_SIDE)
    return q0, ws, jnp.where(i == 0, 0, jnp.where(i == n_blocks - 1, 2, 1))


_FREE_STRIDE = 4


def _residues_per_step(dilation):
    return max(dilation // _FREE_STRIDE, 1)


def _residue_views(dilation, seq, ins, outs, tmps):
    step = pl.program_id(1)
    if dilation <= _FREE_STRIDE:
        def rows(start, count, sub=0):
            return pl.ds(start, count) if dilation == 1 else pl.ds(start * dilation + step, count, stride=dilation)

        return ins, outs, rows, lambda: None
    inner = _residues_per_step(dilation)
    assert inner <= _FREE_STRIDE and len(tmps) == len(ins) + len(outs)
    coarse = pl.ds(step, seq // _FREE_STRIDE, stride=_FREE_STRIDE)
    in_tmps, out_tmps = tmps[:len(ins)], tmps[len(ins):]
    for ref, tmp in zip(ins, in_tmps):
        tmp[...] = ref[coarse, :]

    def flush():
        for ref, tmp in zip(outs, out_tmps):
            ref[coarse, :] = tmp[...]

    return in_tmps, out_tmps, lambda start, count, sub=0: pl.ds(start * inner + sub, count, stride=inner), flush


def _of_sub(ref, sub):
    return ref.at[sub] if len(ref.shape) == 3 else ref


def _attn_call(body, name, dilation, seq, n_in, out_dtypes, scratch, buffers):
    col = pl.BlockSpec((seq, _PAIR), lambda c, r: (0, c), pipeline_mode=pl.Buffered(buffers))
    tmps = [pltpu.VMEM((seq // _FREE_STRIDE, _PAIR), F32)] * (n_in + len(out_dtypes) if dilation > _FREE_STRIDE else 0)
    return pl.pallas_call(
        body, name=name, grid=(GROUP_DIM // _PAIR, dilation // _residues_per_step(dilation)),
        in_specs=[pl.BlockSpec(memory_space=pltpu.SMEM)] + [col] * n_in, out_specs=[col] * len(out_dtypes),
        out_shape=[jax.ShapeDtypeStruct((seq, GROUP_DIM), dt) for dt in out_dtypes], scratch_shapes=scratch + tmps,
        compiler_params=_params(dimension_semantics=("arbitrary", "arbitrary")))


def _staged(dilation, length, rows, sources, scratch):
    if dilation == 1:
        return sources
    for src, dst in zip(sources, scratch):
        for sub in range(_residues_per_step(dilation)):
            dst[sub] = src[rows(0, length, sub), :].astype(BF16)
    return scratch


def _sub_and_block(i, dilation, n_blocks):
    return (0, i) if _residues_per_step(dilation) == 1 else (i // n_blocks, i % n_blocks)


def _attn_fwd(q, k, v, slopes, dilation, name):
    seq = q.shape[0]
    length = seq // dilation
    n_blocks = length // _BQ
    n_stage = 0 if dilation == 1 else 3

    def body(sl_ref, q_ref, k_ref, v_ref, o_ref, lse_ref, *scratch):
        bias_ref, tmps = scratch[n_stage], scratch[n_stage + 1:]
        (q_in, k_in, v_in), (o_out, lse_out), rows, flush = _residue_views(dilation, seq, (q_ref, k_ref, v_ref), (o_ref, lse_ref), tmps)
        qs, ks, vs = _staged(dilation, length, rows, (q_in, k_in, v_in), scratch[:n_stage])
        _fill_bias(bias_ref, sl_ref, dilation)

        def block(i, carry):
            sub, j = _sub_and_block(i, dilation, n_blocks)
            q0, ws, which = _block_window(j, n_blocks, length)
            kw = _of_sub(ks, sub)[pl.ds(ws, _KW), :]
            vw = _of_sub(vs, sub)[pl.ds(ws, _KW), :]
            sc = _dot_nt(_stack_heads(_of_sub(qs, sub)[pl.ds(q0, _BQ), :]), kw) + bias_ref[which]
            m = jnp.max(sc, axis=-1, keepdims=True)
            p = jnp.exp(sc - m)
            den = jnp.sum(p, axis=-1, keepdims=True)
            o_out[rows(q0, _BQ, sub), :] = _unstack_heads(_dot(p.astype(BF16), vw) / den)
            lse_out[rows(q0, _BQ, sub), :] = _unstack_heads(jnp.broadcast_to(m + jnp.log(den), (2 * _BQ, _PAIR)))
            return carry

        lax.fori_loop(0, trips, block, 0, unroll=min(_ATTN_UNROLL, trips))
        flush()

    trips = _residues_per_step(dilation) * n_blocks
    stage = pltpu.VMEM((_residues_per_step(dilation), length, _PAIR), BF16)
    bias = pltpu.VMEM((3, 2 * _BQ, _KW), F32)
    return _attn_call(body, name, dilation, seq, 3, [F32, F32], [stage] * n_stage + [bias], 2)(slopes, q, k, v)


def _attn_bwd(q, k, v, do, lse, cterm, slopes, dilation, name):
    seq = q.shape[0]
    length = seq // dilation
    n_blocks = length // _BQ
    n_stage, n_whole = (0, 0) if dilation == 1 else (4, 3)

    def body(sl_ref, q_ref, k_ref, v_ref, do_ref, lse_ref, c_ref, dq_ref, dk_ref, dv_ref, *scratch):
        dk_acc, dv_acc, bias_ref = scratch[n_stage:n_stage + 3]
        whole, tmps = scratch[n_stage + 3:n_stage + 3 + n_whole], scratch[n_stage + 3 + n_whole:]
        (q_in, k_in, v_in, do_in, lse_in, c_in), (dq_out, dk_out, dv_out), rows, flush = _residue_views(
            dilation, seq, (q_ref, k_ref, v_ref, do_ref, lse_ref, c_ref), whole or (dq_ref, dk_ref, dv_ref), tmps)
        qs, ks, vs, dos = _staged(dilation, length, rows, (q_in, k_in, v_in, do_in), scratch[:n_stage])
        dk_acc[...] = jnp.zeros_like(dk_acc)
        dv_acc[...] = jnp.zeros_like(dv_acc)
        _fill_bias(bias_ref, sl_ref, dilation)

        def block(i, carry):
            sub, j = _sub_and_block(i, dilation, n_blocks)
            q0, ws, which = _block_window(j, n_blocks, length)
            qm = _stack_heads(_of_sub(qs, sub)[pl.ds(q0, _BQ), :])
            dom = _stack_heads(_of_sub(dos, sub)[pl.ds(q0, _BQ), :])
            kw = _of_sub(ks, sub)[pl.ds(ws, _KW), :]
            vw = _of_sub(vs, sub)[pl.ds(ws, _KW), :]
            p = jnp.exp(_dot_nt(qm, kw) + bias_ref[which] - _stack_cols(lse_in[rows(q0, _BQ, sub), :]))
            ds = (p * (_dot_nt(dom, vw) + _stack_cols(c_in[rows(q0, _BQ, sub), :]))).astype(BF16)
            dq_out[rows(q0, _BQ, sub), :] = (_unstack_heads(_dot(ds, kw)) * _SCORE_SCALE).astype(dq_out.dtype)
            dk_acc[sub, pl.ds(ws, _KW), :] += _dot_tn(ds, qm)
            dv_acc[sub, pl.ds(ws, _KW), :] += _dot_tn(p.astype(BF16), dom)
            return carry

        lax.fori_loop(0, trips, block, 0, unroll=min(_ATTN_UNROLL, trips))
        for sub in range(per):
            dk_out[rows(0, length, sub), :] = dk_acc[sub].astype(dk_out.dtype)
            dv_out[rows(0, length, sub), :] = dv_acc[sub].astype(dv_out.dtype)
        flush()
        if whole:
            @pl.when(pl.program_id(1) == dilation // per - 1)
            def _():
                for ref, collected in zip((dq_ref, dk_ref, dv_ref), whole):
                    ref[...] = collected[...].astype(BF16)

    per = _residues_per_step(dilation)
    trips = per * n_blocks
    stage = pltpu.VMEM((per, length, _PAIR), BF16)
    acc = pltpu.VMEM((per, length, _PAIR), F32)
    bias = pltpu.VMEM((3, 2 * _BQ, _KW), F32)
    collect = pltpu.VMEM((seq, _PAIR), F32)
    return _attn_call(body, name, dilation, seq, 6, [BF16] * 3, [stage] * n_stage + [acc] * 2 + [bias] + [collect] * n_whole,
                      2 if dilation == 1 else 1)(slopes, q, k, v, do, lse, cterm)


def _group_weights(lses):
    m = jnp.maximum(jnp.maximum(lses[0], lses[1]), lses[2])
    es = [jnp.exp(l - m) for l in lses]
    den = es[0] + es[1] + es[2]
    return [e / den for e in es]


def _out_fwd(a_pool, outs, lses, x, w_out, g, name, tm=1024):
    s, d = x.shape
    width = POOL_DIM + 3 * GROUP_DIM

    def body(ap_ref, o0, o1, o2, l0, l1, l2, x_ref, w_ref, g_ref, xo_ref, cat_ref):
        alphas = _group_weights([l0[...], l1[...], l2[...]])
        cat = jnp.concatenate([ap_ref[...]] + [(o[...] * al).astype(BF16) for o, al in zip((o0, o1, o2), alphas)], axis=1)
        cat_ref[...] = cat
        mix = _dot(cat, w_ref[...])
        xo_ref[...] = x_ref[...] + mix * _inv_rms(mix) * g_ref[...]

    return pl.pallas_call(
        body, name=name, grid=(s // tm,),
        in_specs=[_rows(tm, POOL_DIM)] + [_rows(tm, GROUP_DIM)] * 6 + [_rows(tm, d), _resident(w_out.shape), _const((1, d))],
        out_specs=[_rows(tm, d), _rows(tm, width)],
        out_shape=[jax.ShapeDtypeStruct((s, d), F32), jax.ShapeDtypeStruct((s, width), BF16)],
        compiler_params=_params(dimension_semantics=("arbitrary",)))(a_pool, *outs, *lses, x, w_out, g)


def _out_bwd(dxo, cat, outs, lses, w_out, g, head_ones, name, tm=1024):
    s, d = dxo.shape

    def body(dxo_ref, cat_ref, o0, o1, o2, l0, l1, l2, w_ref, g_ref, ones_ref, dpool_ref, dmix_ref, do0, do1, do2, c0, c1, c2, dg_ref):
        mv = _dot(cat_ref[...], w_ref[...])
        dmix, dg = _rms_bwd(mv, _inv_rms(mv), g_ref[...], dxo_ref[...])
        dmb = dmix.astype(BF16)
        dmix_ref[...] = dmb
        _accumulate(dg_ref, dg)
        dcat = _dot_nt(dmb, w_ref[...])
        dpool_ref[...] = dcat[:, :POOL_DIM]
        alphas = _group_weights([l0[...], l1[...], l2[...]])
        das = [dcat[:, POOL_DIM + GROUP_DIM * j:POOL_DIM + GROUP_DIM * (j + 1)] for j in range(3)]
        prod = sum(da * (o[...] * al) for da, o, al in zip(das, (o0, o1, o2), alphas))
        hi = prod.astype(BF16)
        lo = (prod - hi.astype(F32)).astype(BF16)
        total = _dot(hi, ones_ref[...]) + _dot(lo, ones_ref[...])
        for da, al, do_ref, c_ref in zip(das, alphas, (do0, do1, do2), (c0, c1, c2)):
            do_ref[...] = (da * al).astype(do_ref.dtype)
            c_ref[...] = -al * total

    return pl.pallas_call(
        body, name=name, grid=(s // tm,),
        in_specs=[_rows(tm, d), _rows(tm, cat.shape[1])] + [_rows(tm, GROUP_DIM)] * 6 + [_resident(w_out.shape), _const((1, d)),
                                                                                        _const((GROUP_DIM, GROUP_DIM))],
        out_specs=[_rows(tm, POOL_DIM), _rows(tm, d)] + [_rows(tm, GROUP_DIM)] * 6 + [_const((1, d))],
        out_shape=[jax.ShapeDtypeStruct((s, POOL_DIM), F32), jax.ShapeDtypeStruct((s, d), BF16)]
        + [jax.ShapeDtypeStruct((s, GROUP_DIM), _attn_dtype(dil)) for dil in DILATIONS]
        + [jax.ShapeDtypeStruct((s, GROUP_DIM), F32)] * 3 + [jax.ShapeDtypeStruct((1, d), F32)],
        compiler_params=_params(dimension_semantics=("arbitrary",)))(dxo, cat, *outs, *lses, w_out, g, head_ones)


def _alibi_slopes():
    return np.array([2.0 ** (-8.0 * (i + 1) / N_ATTN_HEADS) for i in range(N_ATTN_HEADS)], np.float32)


def _block_diag(w_lin):
    n, c, _ = w_lin.shape
    eye = jnp.eye(n, dtype=w_lin.dtype)
    return (eye[:, None, :, None] * w_lin[:, :, None, :]).reshape(n * c, n * c)


class _NoExchange:
    def __init__(self, full):
        self.full, self.grads = full, {}

    def first_weights(self):
        return self.full

    def riders(self, host):
        return []

    def landed(self, host, results):
        return self.full

    def gradient(self, name, grads):
        self.grads[name] = grads[0]

    def small_gradients(self, packed):
        self.small_packed = packed


def _local_step(x, target, small, exchange):
    s, d = x.shape
    slopes = _alibi_slopes()
    group_slopes = [jnp.asarray(slopes[4 * g:4 * g + 4]) for g in range(3)]
    w_bd = _block_diag(small["w_pool_lin"]).astype(BF16)
    head_ones = jnp.asarray(np.kron(np.eye(GROUP_DIM // HEAD_DIM), np.ones((HEAD_DIM, HEAD_DIM))), BF16)

    full = dict(exchange.first_weights())

    def hosted(call, host, *args):
        results, riding = call(*args, host, exchange.riders(host))
        full.update(exchange.landed(host, riding) or {})
        return results

    x1, a1, b1, f1 = hosted(_ffn_fwd, "ffn1_fwd", x, small["g_ffn1_pre"], full["w1_gate"], full["w1_up"], full["w1_down"],
                            small["g_ffn1_post"], None)
    h2, u, *parts = hosted(_in_fwd, "in_fwd", x1, small["g_mix_pre"], full["w_in"])
    qs, ks, vs = parts[0:3], parts[3:6], parts[6:9]
    a_pool = _pool_fwd(u, w_bd, small["pool_scale"], "pool_fwd")
    outs, lses = [], []
    for g, dil in enumerate(DILATIONS):
        o, lse = _attn_fwd(qs[g], ks[g], vs[g], group_slopes[g], dil, f"attn_fwd{g}")
        outs.append(o)
        lses.append(lse)
    x2, cat = _out_fwd(a_pool, outs, lses, x1, full["w_out"], small["g_mix_post"], "out_fwd")
    (dx3, a2, b2, f2, loss_part), _ = _ffn_fwd(x2, small["g_ffn2_pre"], full["w2_gate"], full["w2_up"], full["w2_down"],
                                               small["g_ffn2_post"], target, "ffn2_fwd")

    small_grads = {}

    def ffn_backward(tag, dxo, x_in, f, a, b):
        n = tag[-1]
        dx, hh, da, db, df, h, dg_pre, dg_post = hosted(
            _ffn_bwd, f"{tag}_bwd", dxo, x_in, f, a, b, small[f"g_{tag}_pre"], small[f"g_{tag}_post"],
            full[f"w{n}_gate"], full[f"w{n}_up"], full[f"w{n}_down"])
        small_grads[f"g_{tag}_pre"], small_grads[f"g_{tag}_post"] = dg_pre, dg_post
        if len(small_grads) == len(SMALL):
            exchange.small_gradients(_pack_small(small_grads, loss_part[0, 0]))
        for part, lhs, rhs in (("down", hh, df), ("gate", da, h), ("up", db, h)):
            exchange.gradient(f"w{n}_{part}", hosted(_wgrad, f"{tag}_wgrad_{part}", lhs, rhs))
        return dx

    dx2 = ffn_backward("ffn2", dx3, x2, f2, a2, b2)
    dpool, dmix, *dos_cs, small_grads["g_mix_post"] = _out_bwd(dx2, cat, outs, lses, full["w_out"], small["g_mix_post"],
                                                               head_ones, "out_bwd")
    dos, cs = dos_cs[:3], dos_cs[3:]
    dqs, dks, dvs = [], [], []
    for g, dil in enumerate(DILATIONS):
        dq, dk, dv = _attn_bwd(qs[g], ks[g], vs[g], dos[g], lses[g], cs[g], group_slopes[g], dil, f"attn_bwd{g}")
        dqs.append(dq)
        dks.append(dk)
        dvs.append(dv)
    du, dw_bd, small_grads["pool_scale"] = _pool_bwd(u, dpool, w_bd, small["pool_scale"], "pool_bwd")
    n_pool = len(POOL_HALF_WINDOWS)
    small_grads["w_pool_lin"] = jnp.stack(
        [dw_bd[HEAD_DIM * g:HEAD_DIM * (g + 1), HEAD_DIM * g:HEAD_DIM * (g + 1)] for g in range(n_pool)])
    dz_parts = dqs + dks + dvs
    dx1, small_grads["g_mix_pre"] = hosted(_in_bwd, "in_bwd", du, dz_parts, x1, dx2, small["g_mix_pre"], full["w_in"])
    exchange.gradient("w_in", hosted(_wgrad_parts, "wgrad_in", [du] + dz_parts, h2))
    dx0 = ffn_backward("ffn1", dx1, x, f1, a1, b1)
    exchange.gradient("w_out", hosted(_wgrad, "wgrad_out", cat, dmix))
    return loss_part[0, 0], dx0, small_grads


SEGMENTS = ("w1_gate", "w1_up", "w1_down", "w_in", "w_out", "w2_gate", "w2_up", "w2_down")
TRANSPOSED = ("w1_gate", "w1_up", "w_in", "w2_gate", "w2_up")
ROWS_OUTSIDE = ("w1_gate", "w1_up", "w2_gate", "w2_up")
HALF = 512


def _place():
    x, y, c = lax.axis_index("x"), lax.axis_index("y"), lax.axis_index("c")
    other_chips = [(1 - x, y), (x, 1 - y), (1 - x, 1 - y)]
    return x, y, c, other_chips


def _chip_rows(chip, rows):
    return pl.ds(pl.multiple_of((2 * chip[0] + chip[1]) * rows, 16), rows)


def _cols(c):
    return pl.ds(pl.multiple_of(c * HALF, HALF), HALF)


def _cast_shards(shards, transposed, place, name):
    n = len(shards)
    rows = [w.shape[1] if t else w.shape[0] for w, t in zip(shards, transposed)]

    def body(place_ref, *refs):
        for w_ref, o_ref, t in zip(refs[:n], refs[n:], transposed):
            o_ref[...] = (w_ref[...].T if t else w_ref[...]).astype(BF16)

    once = pl.Buffered(1)
    return pl.pallas_call(
        body, name=name,
        grid_spec=pltpu.PrefetchScalarGridSpec(
            num_scalar_prefetch=1, grid=(1,),
            in_specs=[pl.BlockSpec(w.shape, lambda i, place: (0, 0), pipeline_mode=once) for w in shards],
            out_specs=[pl.BlockSpec((r, 1024), lambda i, place: (place[0], 0), pipeline_mode=once) for r in rows]),
        out_shape=[jax.ShapeDtypeStruct((N_CHIPS * r, 1024), BF16) for r in rows],
        compiler_params=_params(dimension_semantics=("arbitrary",)))(place, *shards)


def _gather_weights(bufs):
    n = len(bufs)
    rows = [b.shape[0] // N_CHIPS for b in bufs]

    def halves(r):
        first = -(-r // 32) * 16
        return (0, first), (first, r - first)

    def body(*refs):
        outs = refs[n:2 * n]
        ici_send, ici_recv, d2d_send, d2d_recv = refs[2 * n:]
        x, y, c, _ = _place()
        me, via_x, via_y, diagonal = (x, y), (1 - x, y), (x, 1 - y), (1 - x, 1 - y)

        def piece(chip, k, h, cols):
            start, size = halves(rows[k])[h]
            return outs[k].at[pl.ds(pl.multiple_of((2 * chip[0] + chip[1]) * rows[k] + start, 16), size), _cols(cols)]

        def ici(path, chip, k, h, to):
            blk = piece(chip, k, h, c)
            return pltpu.make_async_remote_copy(src_ref=blk, dst_ref=blk, send_sem=ici_send.at[path, k, h],
                                                recv_sem=ici_recv.at[path, k, h], device_id=(*to, c), device_id_type=MESH)

        def d2d(slot, chip, k, h, cols):
            blk = piece(chip, k, h, cols)
            return pltpu.make_async_remote_copy(src_ref=blk, dst_ref=blk, send_sem=d2d_send.at[slot, k, h],
                                                recv_sem=d2d_recv.at[slot, k, h], device_id=(x, y, 1 - c), device_id_type=MESH)

        started = [ici(0, me, k, h, via_x) for h in (0, 1) for k in range(n)] + [ici(1, me, k, h, via_y) for h in (1, 0) for k in range(n)]
        for cp in started:
            cp.start()

        def landed(path, slot, chip, k, h, pass_on_to=None):
            ici(path, chip, k, h, me).wait_recv()
            more = [d2d(slot, chip, k, h, c)] + ([ici(2, chip, k, h, pass_on_to)] if pass_on_to else [])
            for cp in more:
                cp.start()
            started.extend(more)

        for k in range(n):
            landed(0, 0, via_x, k, 0, pass_on_to=via_y)
            landed(1, 1, via_y, k, 1, pass_on_to=via_x)
        for k in range(n):
            landed(0, 0, via_x, k, 1)
            landed(1, 1, via_y, k, 0)
        for k in range(n):
            for h in range(2):
                landed(2, 2, diagonal, k, h)
        for slot, chip in enumerate((via_x, via_y, diagonal)):
            for k in range(n):
                for h in range(2):
                    d2d(slot, chip, k, h, 1 - c).wait_recv()
        for cp in started:
            cp.wait_send()

    any_spec = pl.BlockSpec(memory_space=pl.ANY)
    return pl.pallas_call(
        body, name="gather_weights", in_specs=[any_spec] * n, out_specs=[any_spec] * n,
        out_shape=[jax.ShapeDtypeStruct(b.shape, b.dtype) for b in bufs], input_output_aliases={k: k for k in range(n)},
        scratch_shapes=[pltpu.SemaphoreType.DMA((3, n, 2))] * 4)(*bufs)


def _gather_rider(bufs):
    n = len(bufs)
    rows = [b.shape[0] // N_CHIPS for b in bufs]

    def copies(outs, send_sems, recv_sems, inbound):
        x, y, c, chips = _place()
        for j, chip in enumerate(chips):
            for k in range(n):
                src_chip = chip if inbound else (x, y)
                blk = outs[k].at[_chip_rows(src_chip, rows[k]), _cols(c)]
                yield pltpu.make_async_remote_copy(src_ref=blk, dst_ref=blk, send_sem=send_sems.at[j, k], recv_sem=recv_sems.at[j, k],
                                                   device_id=(*chip, c), device_id_type=MESH)

    def start(ins, outs, send_sems, recv_sems):
        for cp in copies(outs, send_sems, recv_sems, False):
            cp.start()

    def wait(ins, outs, send_sems, recv_sems):
        for cp in copies(outs, send_sems, recv_sems, True):
            cp.wait_recv()
        for cp in copies(outs, send_sems, recv_sems, False):
            cp.wait_send()

    return _Rider(list(bufs), None, (3, n), start, wait)


def _forward_rider(bufs):
    n = len(bufs)
    rows = [b.shape[0] // N_CHIPS for b in bufs]

    def copies(outs, send_sems, recv_sems, half):
        x, y, c, chips = _place()
        for j, chip in enumerate(chips):
            for k in range(n):
                blk = outs[k].at[_chip_rows(chip, rows[k]), _cols(half(c))]
                yield pltpu.make_async_remote_copy(src_ref=blk, dst_ref=blk, send_sem=send_sems.at[j, k], recv_sem=recv_sems.at[j, k],
                                                   device_id=(x, y, 1 - c), device_id_type=MESH)

    def start(ins, outs, send_sems, recv_sems):
        for cp in copies(outs, send_sems, recv_sems, lambda c: c):
            cp.start()

    def wait(ins, outs, send_sems, recv_sems):
        for cp in copies(outs, send_sems, recv_sems, lambda c: 1 - c):
            cp.wait_recv()
        for cp in copies(outs, send_sems, recv_sems, lambda c: c):
            cp.wait_send()

    return _Rider(list(bufs), None, (3, n), start, wait)


def _sibling_rider(grads):
    n = len(grads)

    def copies(ins, outs, send_sems, recv_sems):
        x, y, c, _ = _place()
        return [pltpu.make_async_remote_copy(src_ref=ins[k].at[:, pl.ds(1 - c, 1)], dst_ref=outs[k], send_sem=send_sems.at[k],
                                             recv_sem=recv_sems.at[k], device_id=(x, y, 1 - c), device_id_type=MESH)
                for k in range(n)]

    def start(*refs):
        for cp in copies(*refs):
            cp.start()

    def wait(*refs):
        for cp in copies(*refs):
            cp.wait()

    return _Rider(list(grads), [jax.ShapeDtypeStruct((N_CHIPS, 1) + g.shape[2:], g.dtype) for g in grads], (n,), start, wait)


def _alone(rider, name):
    n = len(rider.operands)
    landing = rider.landing if rider.landing is not None else [jax.ShapeDtypeStruct(a.shape, a.dtype) for a in rider.operands]
    n_out = len(landing)

    def body(*refs):
        rider.start(refs[:n], refs[n:n + n_out], *refs[n + n_out:])
        rider.wait(refs[:n], refs[n:n + n_out], *refs[n + n_out:])

    any_spec = pl.BlockSpec(memory_space=pl.ANY)
    return pl.pallas_call(body, name=name, in_specs=[any_spec] * n, out_specs=[any_spec] * n_out, out_shape=landing,
                          input_output_aliases={i: i for i in range(n)} if rider.landing is None else {},
                          scratch_shapes=[pltpu.SemaphoreType.DMA(rider.sems)] * 2)(*rider.operands)


def _chip_sum(grad, from_sibling, place, name):
    rh, width = grad.shape[2:]

    def body(place_ref, g_ref, s_ref, own_ref, all_ref):
        all_ref[...] = (g_ref[...] + s_ref[...]).astype(BF16)
        mine = place_ref[0]
        own_ref[0] = g_ref[mine, 0] + s_ref[mine, 0]

    blk = (N_CHIPS, 1, rh, width)
    once = pl.Buffered(1)
    return pl.pallas_call(
        body, name=name,
        grid_spec=pltpu.PrefetchScalarGridSpec(
            num_scalar_prefetch=1, grid=(1,),
            in_specs=[pl.BlockSpec(blk, lambda i, place: (0, place[1], 0, 0), pipeline_mode=once),
                      pl.BlockSpec(blk, lambda i, place: (0, 0, 0, 0), pipeline_mode=once)],
            out_specs=[pl.BlockSpec((1, rh, width), lambda i, place: (0, 0, 0), pipeline_mode=once),
                       pl.BlockSpec(blk, lambda i, place: (0, 0, 0, 0), pipeline_mode=once)]),
        out_shape=[jax.ShapeDtypeStruct((1, rh, width), F32), jax.ShapeDtypeStruct((N_CHIPS, 1, rh, width), BF16)],
        compiler_params=_params(dimension_semantics=("arbitrary",)))(place, grad, from_sibling)


def _scatter_rider(sums):
    n = len(sums)

    def copies(ins, outs, send_sems, recv_sems):
        x, y, c, chips = _place()
        return [pltpu.make_async_remote_copy(src_ref=ins[k].at[pl.ds(2 * chip[0] + chip[1], 1)], dst_ref=outs[k].at[pl.ds(j, 1)],
                                             send_sem=send_sems.at[j, k], recv_sem=recv_sems.at[j, k],
                                             device_id=(*chip, c), device_id_type=MESH)
                for j, chip in enumerate(chips) for k in range(n)]

    def start(*refs):
        for cp in copies(*refs):
            cp.start()

    def wait(*refs):
        for cp in copies(*refs):
            cp.wait()

    return _Rider(list(sums), [jax.ShapeDtypeStruct((3,) + sm.shape[1:], BF16) for sm in sums], (3, n), start, wait)


def _total_sums(owns, received, name):
    n = len(owns)

    def body(*refs):
        for o_ref, r_ref, t_ref in zip(refs[:n], refs[n:2 * n], refs[2 * n:]):
            total = o_ref[0]
            for j in range(3):
                total = total + r_ref[j, 0].astype(F32)
            t_ref[0] = total

    return _hosted_call(body, None, name=name, steps=1, in_specs=[_resident(a.shape) for a in owns + received],
                        out_specs=[_resident(o.shape) for o in owns], out_shape=[jax.ShapeDtypeStruct(o.shape, F32) for o in owns],
                        args=owns + received)[0]


def _swap_rider(halves):
    n = len(halves)

    def copies(ins, outs, send_sems, recv_sems):
        x, y, c, _ = _place()
        return [pltpu.make_async_remote_copy(src_ref=ins[k], dst_ref=outs[k], send_sem=send_sems.at[k], recv_sem=recv_sems.at[k],
                                             device_id=(x, y, 1 - c), device_id_type=MESH) for k in range(n)]

    def start(*refs):
        for cp in copies(*refs):
            cp.start()

    def wait(*refs):
        for cp in copies(*refs):
            cp.wait()

    return _Rider(list(halves), [jax.ShapeDtypeStruct(h.shape, F32) for h in halves], (n,), start, wait)


N_DEV = 8


def _small_rider(block):
    m_per, width = block.shape

    def copies(ins, outs, send_sems, recv_sems):
        (x_ref,), (out_ref,) = ins, outs
        x, y, c, chips = _place()
        me, sibling = (x, y, c), (x, y, 1 - c)

        def rows(px, py, pc):
            return out_ref.at[pl.ds((4 * px + 2 * py + pc) * m_per, m_per), :]

        def copy(k, blk, to, src=None):
            return pltpu.make_async_remote_copy(src_ref=rows(*blk) if src is None else src, dst_ref=rows(*blk),
                                                send_sem=send_sems.at[k], recv_sem=recv_sems.at[k], device_id=to, device_id_type=MESH)

        mine = pltpu.make_async_copy(x_ref, rows(*me), send_sems.at[7])
        first = [copy(0, me, sibling, src=x_ref)] + [copy(1 + j, me, (*chip, c), src=x_ref) for j, chip in enumerate(chips)]

        def second_hop():
            arriving = [copy(1 + j, (*chip, c), me) for j, chip in enumerate(chips)]
            passed = [copy(4 + j, (*chip, c), sibling) for j, chip in enumerate(chips)]
            last = [copy(0, sibling, me)] + [copy(4 + j, (*chip, 1 - c), me) for j, chip in enumerate(chips)]
            return arriving, passed, last

        return mine, first, second_hop

    def start(*refs):
        mine, first, _ = copies(*refs)
        mine.start()
        for cp in first:
            cp.start()

    def wait(*refs):
        mine, first, second_hop = copies(*refs)
        arriving, passed, last = second_hop()
        for arrived, onward in zip(arriving, passed):
            arrived.wait_recv()
            onward.start()
        for cp in last:
            cp.wait_recv()
        for cp in first + passed:
            cp.wait_send()
        mine.wait()

    return _Rider([block], [jax.ShapeDtypeStruct((N_DEV * m_per, width), F32)], (8,), start, wait)


def _adamw_math(w, g, m, v):
    m = ADAM_B1 * m + (1.0 - ADAM_B1) * g
    v = ADAM_B2 * v + (1.0 - ADAM_B2) * (g * g)
    m_hat = m / (1.0 - ADAM_B1 ** ADAM_STEP)
    v_hat = v / (1.0 - ADAM_B2 ** ADAM_STEP)
    delta = -ADAM_LR * (m_hat / (jnp.sqrt(v_hat) + ADAM_EPS) + ADAM_WD * w)
    return delta, m, v


def _adamw(w, mine, siblings, place, m, v, transposed, name):
    rh, width = mine.shape[1:]
    place_spec = pl.BlockSpec(memory_space=pltpu.SMEM)
    halves = [_const((1, rh, width))] * 2
    out_shape = [jax.ShapeDtypeStruct(w.shape, F32)] * 4
    if transposed:
        def body(place_ref, w_ref, mine_ref, sib_ref, m_ref, v_ref, go_ref, d_ref, mo_ref, vo_ref):
            first = place_ref[1] == 0
            g = jnp.concatenate([jnp.where(first, mine_ref[0], sib_ref[0]), jnp.where(first, sib_ref[0], mine_ref[0])], axis=0).T
            go_ref[...] = g
            d_ref[...], mo_ref[...], vo_ref[...] = _adamw_math(w_ref[...], g, m_ref[...], v_ref[...])

        whole = _resident(w.shape)
        return _hosted_call(body, None, name=name, steps=1, in_specs=[place_spec, whole] + halves + [whole, whole],
                            out_specs=[whole] * 4, out_shape=out_shape, args=[place, w, mine, siblings, m, v])[0]

    def body(place_ref, w_ref, mine_ref, sib_ref, m_ref, v_ref, go_ref, d_ref, mo_ref, vo_ref):
        g = jnp.where(pl.program_id(0) == place_ref[1], mine_ref[0], sib_ref[0])
        go_ref[...] = g
        d_ref[...], mo_ref[...], vo_ref[...] = _adamw_math(w_ref[...], g, m_ref[...], v_ref[...])

    half = _rows(rh, width)
    return _hosted_call(body, None, name=name, steps=2, in_specs=[place_spec, half] + halves + [half, half],
                        out_specs=[half] * 4, out_shape=out_shape, args=[place, w, mine, siblings, m, v])[0]


def _adamw_small(gathered, w, m, v, name):
    def body(ga_ref, w_ref, m_ref, v_ref, go_ref, d_ref, mo_ref, vo_ref):
        g = ga_ref[0]
        for dev in range(1, N_DEV):
            g = g + ga_ref[dev]
        go_ref[...] = g
        d_ref[...], mo_ref[...], vo_ref[...] = _adamw_math(w_ref[...], g, m_ref[...], v_ref[...])

    return pl.pallas_call(body, name=name, out_shape=[jax.ShapeDtypeStruct(w.shape, F32)] * 4,
                          compiler_params=_params())(gathered, w, m, v)


class _Exchange:
    FIRST = ("w1_gate", "w1_up", "w1_down")
    HOSTS = {"ffn2_wgrad_gate": (("w2_down",), ()), "ffn2_wgrad_up": (("w2_gate",), ("w2_down",)),
             "in_bwd": (("w2_up",), ("w2_gate",)), "wgrad_in": ((), ("w2_up",)),
             "ffn1_wgrad_down": ((), ("w_in",)), "ffn1_wgrad_gate": (("w1_down",), ()), "ffn1_wgrad_up": ((), ("w1_down", "w1_gate")),
             "wgrad_out": ((), ("w1_up",))}
    ALONE = ("w_in", "w1_gate", "w1_up", "w_out")
    SWAP_HOST = "wgrad_out"
    SMALL_HOST = "ffn1_wgrad_gate"

    def __init__(self, bufs, place):
        self.bufs, self.place = bufs, place
        self.later = [k for k in SEGMENTS if k not in self.FIRST]
        self.split, self.wire, self.own, self.to_send, self.received = {}, {}, {}, {}, {}

    def first_weights(self):
        return dict(zip(self.FIRST, _gather_weights([self.bufs[k] for k in self.FIRST])))

    def riders(self, host):
        if host == "ffn1_fwd":
            return [_gather_rider([self.bufs[k] for k in self.later])]
        if host == "in_fwd":
            return [_forward_rider([self.bufs[k] for k in self.later[1:]])]
        halves, sums = self.HOSTS.get(host, ((), ()))
        riders = ([_sibling_rider([self.wire[k] for k in halves])] if halves else []) + (
            [_scatter_rider([self.to_send[k] for k in sums])] if sums else [])
        if host == self.SWAP_HOST:
            self.early = [k for k in SEGMENTS if k in self.received]
            self.mine = dict(zip(self.early, self._totals(self.early, "total_sums_early")))
            riders.append(_swap_rider([self.mine[k] for k in self.early]))
        if host == self.SMALL_HOST:
            riders.append(_small_rider(self.small_packed))
        return riders

    def small_gradients(self, packed):
        self.small_packed = packed

    def landed(self, host, results):
        if host == "ffn1_fwd":
            self.bufs.update(zip(self.later, results[0]))
            return dict(zip(self.later[:1], _alone(_forward_rider([self.bufs[self.later[0]]]), "gather_forward_first")))
        if host == "in_fwd":
            return dict(zip(self.later[1:], results[0]))
        results = list(results)
        if host == self.SWAP_HOST:
            self.siblings = dict(zip(self.early, results.pop()))
        if host == self.SMALL_HOST:
            (self.small_gathered,) = results.pop()
        halves, sums = self.HOSTS.get(host, ((), ()))
        if halves:
            self._chip_sums(halves, results[0])
        if sums:
            self.received.update(zip(sums, results[-1]))

    def gradient(self, name, grads):
        self.split[name], self.wire[name] = (g.reshape(N_CHIPS, 2, g.shape[0] // (2 * N_CHIPS), g.shape[1]) for g in grads)
        if name in self.ALONE:
            self._chip_sums([name], _alone(_sibling_rider([self.wire[name]]), f"reduce_sibling_{name}"))

    def _chip_sums(self, names, from_sibling):
        for k, fs in zip(names, from_sibling):
            self.own[k], self.to_send[k] = _chip_sum(self.split[k], fs, self.place, f"chip_sum_{k}")

    def _totals(self, names, call_name):
        return _total_sums([self.own[k] for k in names], [self.received[k] for k in names], call_name)

    def summed_halves(self):
        late = [k for k in SEGMENTS if k not in self.received]
        self.received.update(zip(late, _alone(_scatter_rider([self.to_send[k] for k in late]), "reduce_chips_last")))
        rest = [k for k in SEGMENTS if k not in self.early]
        mine = self._totals(rest, "total_sums")
        self.mine.update(zip(rest, mine))
        self.siblings.update(zip(rest, _alone(_swap_rider(mine), "swap_halves")))
        return [self.mine[k] for k in SEGMENTS], [self.siblings[k] for k in SEGMENTS]


SMALL = ("g_ffn1_pre", "g_ffn1_post", "g_mix_pre", "w_pool_lin", "pool_scale", "g_mix_post", "g_ffn2_pre", "g_ffn2_post")
WEIGHTS = ("g_ffn1_pre", "w1_gate", "w1_up", "w1_down", "g_ffn1_post", "g_mix_pre", "w_in", "w_pool_lin", "pool_scale", "w_out",
           "g_mix_post", "g_ffn2_pre", "w2_gate", "w2_up", "w2_down", "g_ffn2_post")
LANES = 128


def _pack_small(tree, extra=0.0):
    flat = jnp.concatenate([tree[k].reshape(-1) for k in SMALL] + [jnp.reshape(extra, (1,)).astype(F32)])
    rows = -(-flat.shape[0] // (8 * LANES)) * 8
    return jnp.pad(flat, (0, rows * LANES - flat.shape[0])).reshape(rows, LANES)


def _unpack_small(packed, like):
    flat, out, at = packed.reshape(-1), {}, 0
    for k in SMALL:
        size = math.prod(like[k].shape)
        out[k] = flat[at:at + size].reshape(like[k].shape)
        at += size
    return out


def kernel(x, g_ffn1_pre, w1_gate, w1_up, w1_down, g_ffn1_post, g_mix_pre, w_in, w_pool_lin, pool_scale, w_out, g_mix_post, g_ffn2_pre, w2_gate, w2_up, w2_down, g_ffn2_post, loss_target, m_g_ffn1_pre, m_w1_gate, m_w1_up, m_w1_down, m_g_ffn1_post, m_g_mix_pre, m_w_in, m_w_pool_lin, m_pool_scale, m_w_out, m_g_mix_post, m_g_ffn2_pre, m_w2_gate, m_w2_up, m_w2_down, m_g_ffn2_post, v_g_ffn1_pre, v_w1_gate, v_w1_up, v_w1_down, v_g_ffn1_post, v_g_mix_pre, v_w_in, v_w_pool_lin, v_pool_scale, v_w_out, v_g_mix_post, v_g_ffn2_pre, v_w2_gate, v_w2_up, v_w2_down, v_g_ffn2_post):
    given = dict(locals())
    w = {k: given[k] for k in WEIGHTS}
    m = {k: given["m_" + k] for k in WEIGHTS}
    v = {k: given["v_" + k] for k in WEIGHTS}
    small = {k: (w[k][0] if k == "w_pool_lin" else w[k].reshape(1, -1)) for k in SMALL}

    place = jnp.stack([2 * lax.axis_index("x") + lax.axis_index("y"), lax.axis_index("c")]).astype(jnp.int32)
    def as_rows(a, k):
        return jnp.swapaxes(a, 1, 2)[0] if k in ROWS_OUTSIDE else a[0]

    def as_given(a, k):
        return jnp.swapaxes(a[None], 1, 2) if k in ROWS_OUTSIDE else a[None]

    in_kernel = [k for k in TRANSPOSED if k not in ROWS_OUTSIDE]
    bufs = {}
    for tag, names in (("first", _Exchange.FIRST), ("rest", [k for k in SEGMENTS if k not in _Exchange.FIRST])):
        bufs.update(zip(names, _cast_shards([as_rows(w[k], k) for k in names], [k in in_kernel for k in names], place, f"cast_{tag}")))
    exchange = _Exchange(bufs, place)
    _, grad_x, _ = _local_step(x[0], loss_target[0], small, exchange)

    out_grad, out_delta, out_m, out_v = {}, {}, {}, {}
    for k, mine, siblings in zip(SEGMENTS, *exchange.summed_halves()):
        results = _adamw(as_rows(w[k], k), mine, siblings, place, as_rows(m[k], k), as_rows(v[k], k), k in in_kernel, f"adamw_{k}")
        out_grad[k], out_delta[k], out_m[k], out_v[k] = (as_given(a, k) for a in results)

    gathered = exchange.small_gathered.reshape(N_DEV, -1, LANES)
    like = {k: w[k] for k in SMALL}
    results = _adamw_small(gathered, _pack_small(like), _pack_small({k: m[k] for k in SMALL}),
                           _pack_small({k: v[k] for k in SMALL}), "adamw_small")
    for tree, res in zip((out_grad, out_delta, out_m, out_v), results):
        tree.update(_unpack_small(res, like))
    loss = results[0].reshape(-1)[sum(math.prod(like[k].shape) for k in SMALL)]

    return (loss, grad_x[None], *[out_grad[k] for k in WEIGHTS], *[out_delta[k] for k in WEIGHTS],
            *[out_m[k] for k in WEIGHTS], *[out_v[k] for k in WEIGHTS])
```
